```python
import math
import jax, jax.numpy as jnp
from jax import lax
import numpy as np

D_MODEL = 1024
BATCH = 8
SEQ = 4096
DEPTH = 2

CONV_CH = 512
CONV_WIDTH = 31
DSA_HEADS = 8
DSA_HEAD_DIM = 64
IDX_HEADS = 8
IDX_DIM = 64
DSA_TOPK_MAX = 256
Q_BLOCK = 128
DIL_GROUPS = ((128, 1), (512, 4), (2048, 16))
DIL_HEADS = 8
DIL_HEAD_DIM = 128
BAND_BLOCK = 128
NUM_BUCKETS = 32
MAX_DISTANCE = 2048
N_BIAS_HEADS = 8
D_FF_DENSE = 2816
N_EXPERTS = 8
TOP_K = 2
D_FF_EXPERT = 3584
LN_EPS = 1e-5
NEG = -1e30
N_EVEN = (DEPTH + 1) // 2
N_ODD = DEPTH // 2
DEEPNORM_ALPHA = (2 * DEPTH) ** 0.25
DEEPNORM_BETA = (8 * DEPTH) ** -0.25
EVEN_IN_PARTS = (CONV_CH, CONV_CH, DSA_HEADS * DSA_HEAD_DIM, DSA_HEADS * DSA_HEAD_DIM,
                 DSA_HEADS * DSA_HEAD_DIM, IDX_HEADS * IDX_DIM, IDX_DIM, IDX_HEADS)
EVEN_IN_WIDTH = sum(EVEN_IN_PARTS)
EVEN_SPLITS = [int(s) for s in np.cumsum(EVEN_IN_PARTS)[:-1]]
EVEN_OUT_WIDTH = CONV_CH + DSA_HEADS * DSA_HEAD_DIM
ODD_IN_WIDTH = len(DIL_GROUPS) * 3 * DIL_HEADS * DIL_HEAD_DIM
ODD_OUT_WIDTH = DIL_HEADS * DIL_HEAD_DIM

kernel_name = 'hybrid_conv_dsa_dilated_moe_deepnorm'


def layer_norm(x, g, b):
    xf = x.astype(jnp.float32)
    mu = xf.mean(-1, keepdims=True)
    var = jnp.square(xf - mu).mean(-1, keepdims=True)
    y = (xf - mu) * lax.rsqrt(var + LN_EPS) * g.astype(jnp.float32) + b.astype(jnp.float32)
    return y.astype(x.dtype)


def rel_bucket(dist):
    max_exact = NUM_BUCKETS // 2
    n = dist.astype(jnp.int32)
    nf = jnp.maximum(n, 1).astype(jnp.float32)
    large = max_exact + (jnp.log(nf / max_exact) / math.log(MAX_DISTANCE / max_exact)
                         * (NUM_BUCKETS - max_exact)).astype(jnp.int32)
    large = jnp.minimum(large, NUM_BUCKETS - 1)
    return jnp.where(n < max_exact, n, large)


def conformer_conv(a_val, a_gate, conv_w, conv_b, ln_g, ln_b):
    u = a_val * jax.nn.sigmoid(a_gate)
    y = lax.conv_general_dilated(u, conv_w.astype(u.dtype)[:, None, :], (1,), [(CONV_WIDTH - 1, 0)],
                                 dimension_numbers=('NWC', 'WIO', 'NWC'), feature_group_count=CONV_CH)
    y = layer_norm(y + conv_b, ln_g, ln_b)
    return jax.nn.silu(y)


def dsa_attention(q, k, v, q_idx, k_idx, w_idx, rel_bias):
    B, S, H, Dh = q.shape
    topk = min(DSA_TOPK_MAX, S // 4)
    nqb = S // Q_BLOCK
    key_pos = jnp.arange(S, dtype=jnp.int32)
    w_scaled = w_idx.astype(jnp.float32) * (IDX_HEADS ** -0.5 * IDX_DIM ** -0.5)

    def blocks(t):
        return t.reshape((B, nqb, Q_BLOCK) + t.shape[2:]).swapaxes(0, 1)

    starts = jnp.arange(nqb, dtype=jnp.int32) * Q_BLOCK

    def one_block(args):
        qb, qib, wb, start = args
        q_pos = start + jnp.arange(Q_BLOCK, dtype=jnp.int32)
        causal = key_pos[None, :] <= q_pos[:, None]
        dots = jnp.einsum('bqhe,bse->bqhs', qib, k_idx).astype(jnp.float32)
        score = jnp.einsum('bqh,bqhs->bqs', wb, jax.nn.relu(dots))
        score = jnp.where(causal[None], score, -jnp.inf)
        _, sel = lax.top_k(score, topk)
        k_sel = jax.vmap(lambda kk, ii: kk[ii])(k, sel)
        v_sel = jax.vmap(lambda vv, ii: vv[ii])(v, sel)
        dist = q_pos[None, :, None] - sel
        bias = rel_bias[rel_bucket(jnp.maximum(dist, 0))].astype(jnp.float32)
        logits = jnp.einsum('bqhd,bqkhd->bqhk', qb, k_sel).astype(jnp.float32) * (Dh ** -0.5)
        logits = logits + bias.swapaxes(-1, -2)
        logits = jnp.where((dist >= 0)[:, :, None, :], logits, NEG)
        p = jax.nn.softmax(logits, axis=-1)
        return jnp.einsum('bqhk,bqkhd->bqhd', p.astype(v.dtype), v_sel)

    out = lax.map(one_block, (blocks(q), blocks(q_idx), blocks(w_scaled), starts))
    return out.swapaxes(0, 1).reshape(B, S, H, Dh)


def dilated_branch(q, k, v, dil, span, rel_bias):
    B, S, H, Dh = q.shape
    L = S // dil
    nb = -(-L // BAND_BLOCK)
    Lp = nb * BAND_BLOCK

    def to_sub(t):
        t = t.reshape(B, L, dil, H, Dh).transpose(0, 2, 3, 1, 4)
        t = jnp.pad(t, ((0, 0), (0, 0), (0, 0), (0, Lp - L), (0, 0)))
        return t.reshape(B, dil, H, nb, BAND_BLOCK, Dh)

    def with_prev(t):
        prev = jnp.pad(t, ((0, 0), (0, 0), (0, 0), (1, 0), (0, 0), (0, 0)))[:, :, :, :nb]
        return jnp.concatenate([prev, t], axis=4)

    qs = to_sub(q)
    kk = with_prev(to_sub(k))
    vv = with_prev(to_sub(v))
    logits = jnp.einsum('bghnqe,bghnke->bghnqk', qs, kk).astype(jnp.float32) * (Dh ** -0.5)
    i = jnp.arange(BAND_BLOCK, dtype=jnp.int32)[:, None]
    j = jnp.arange(2 * BAND_BLOCK, dtype=jnp.int32)[None, :]
    dist = BAND_BLOCK + i - j
    bias = rel_bias[rel_bucket(jnp.maximum(dist, 0) * dil)]
    bias = jnp.moveaxis(bias, -1, 0).astype(jnp.float32)
    blk = jnp.arange(nb, dtype=jnp.int32)[:, None, None]
    valid = (dist >= 0) & (dist <= span) & ((blk > 0) | (j >= BAND_BLOCK))
    logits = jnp.where(valid, logits + bias[:, None], NEG)
    m = logits.max(-1, keepdims=True)
    p = jnp.exp(logits - m)
    s = p.sum(-1, keepdims=True)
    o = jnp.einsum('bghnqk,bghnke->bghnqe', p.astype(vv.dtype), vv).astype(jnp.float32) / s
    lse = (m + jnp.log(s))[..., 0]

    def from_sub(t):
        t = t.reshape((B, dil, H, Lp) + t.shape[5:])[:, :, :, :L]
        t = jnp.moveaxis(t, 3, 1)
        return t.reshape((B, S, H) + t.shape[4:])

    return from_sub(o), from_sub(lse)


def swiglu(x, wg, wu, wd):
    return (jax.nn.silu(x @ wg) * (x @ wu)) @ wd


def moe_swiglu(x, router, wg, wu, wd):
    B, S, D = x.shape
    xt = x.reshape(-1, D)
    logits = (xt @ router).astype(jnp.float32)
    top_val, top_idx = lax.top_k(logits, TOP_K)
    gate_w = jax.nn.softmax(top_val, axis=-1)
    gates = jnp.sum(jax.nn.one_hot(top_idx, N_EXPERTS, dtype=jnp.float32) * gate_w[..., None], axis=1)
    y = jnp.zeros(xt.shape, jnp.float32)
    for e in range(N_EXPERTS):
        h = jax.nn.silu(xt @ wg[e]) * (xt @ wu[e])
        y = y + gates[:, e:e + 1] * (h @ wd[e]).astype(jnp.float32)
    return y.astype(x.dtype).reshape(B, S, D)


def even_layer(x, rel_bias, w_in, conv_w, conv_b, conv_ln_g, conv_ln_b, w_out, ln1_g, ln1_b,
               ffn_wg, ffn_wu, ffn_wd, ln2_g, ln2_b):
    B, S, _ = x.shape
    h = x @ w_in
    a_val, a_gate, q, k, v, q_idx, k_idx, w_idx = jnp.split(h, EVEN_SPLITS, axis=-1)
    a_out = conformer_conv(a_val, a_gate, conv_w, conv_b, conv_ln_g, conv_ln_b)
    hd = (B, S, DSA_HEADS, DSA_HEAD_DIM)
    att = dsa_attention(q.reshape(hd), k.reshape(hd), v.reshape(hd),
                        q_idx.reshape(B, S, IDX_HEADS, IDX_DIM), k_idx, w_idx, rel_bias)
    mix = jnp.concatenate([a_out, att.reshape(B, S, -1).astype(a_out.dtype)], axis=-1) @ w_out
    x = layer_norm(DEEPNORM_ALPHA * x + mix, ln1_g, ln1_b)
    return layer_norm(DEEPNORM_ALPHA * x + swiglu(x, ffn_wg, ffn_wu, ffn_wd), ln2_g, ln2_b)


def odd_layer(x, rel_bias, w_in, w_out, ln1_g, ln1_b, router, moe_wg, moe_wu, moe_wd, ln2_g, ln2_b):
    B, S, _ = x.shape
    h = (x @ w_in).reshape(B, S, len(DIL_GROUPS), 3, DIL_HEADS, DIL_HEAD_DIM)
    outs, lses = [], []
    for g, (window, dil) in enumerate(DIL_GROUPS):
        o, lse = dilated_branch(h[:, :, g, 0], h[:, :, g, 1], h[:, :, g, 2], dil, window // dil, rel_bias)
        outs.append(o)
        lses.append(lse)
    wts = jax.nn.softmax(jnp.stack(lses), axis=0)
    o = jnp.sum(wts[..., None] * jnp.stack(outs), axis=0)
    mix = o.astype(x.dtype).reshape(B, S, ODD_OUT_WIDTH) @ w_out
    x = layer_norm(DEEPNORM_ALPHA * x + mix, ln1_g, ln1_b)
    return layer_norm(DEEPNORM_ALPHA * x + moe_swiglu(x, router, moe_wg, moe_wu, moe_wd), ln2_g, ln2_b)


def setup_inputs(seed: int = 0) -> dict:
    key = jax.random.key(seed)
    ks = jax.random.split(key, 32)
    nrm = lambda k, shape, scale: jax.random.normal(k, shape, jnp.float32) * scale
    D = D_MODEL
    return {
        'x': nrm(ks[0], (BATCH, SEQ, D), 1.0),
        'rel_bias': nrm(ks[1], (NUM_BUCKETS, N_BIAS_HEADS), 0.2),
        'even_w_in': nrm(ks[2], (N_EVEN, D, EVEN_IN_WIDTH), D ** -0.5),
        'even_conv_w': nrm(ks[3], (N_EVEN, CONV_WIDTH, CONV_CH), CONV_WIDTH ** -0.5),
        'even_conv_b': nrm(ks[4], (N_EVEN, CONV_CH), 0.02),
        'even_conv_ln_g': 1.0 + nrm(ks[5], (N_EVEN, CONV_CH), 0.02),
        'even_conv_ln_b': nrm(ks[6], (N_EVEN, CONV_CH), 0.02),
        'even_w_out': nrm(ks[7], (N_EVEN, EVEN_OUT_WIDTH, D), EVEN_OUT_WIDTH ** -0.5 * DEEPNORM_BETA),
        'even_ln1_g': 1.0 + nrm(ks[8], (N_EVEN, D), 0.02),
        'even_ln1_b': nrm(ks[9], (N_EVEN, D), 0.02),
        'even_ffn_wg': nrm(ks[10], (N_EVEN, D, D_FF_DENSE), D ** -0.5),
        'even_ffn_wu': nrm(ks[11], (N_EVEN, D, D_FF_DENSE), D ** -0.5),
        'even_ffn_wd': nrm(ks[12], (N_EVEN, D_FF_DENSE, D), D_FF_DENSE ** -0.5 * DEEPNORM_BETA),
        'even_ln2_g': 1.0 + nrm(ks[13], (N_EVEN, D), 0.02),
        'even_ln2_b': nrm(ks[14], (N_EVEN, D), 0.02),
        'odd_w_in': nrm(ks[15], (N_ODD, D, ODD_IN_WIDTH), D ** -0.5),
        'odd_w_out': nrm(ks[16], (N_ODD, ODD_OUT_WIDTH, D), ODD_OUT_WIDTH ** -0.5 * DEEPNORM_BETA),
        'odd_ln1_g': 1.0 + nrm(ks[17], (N_ODD, D), 0.02),
        'odd_ln1_b': nrm(ks[18], (N_ODD, D), 0.02),
        'odd_router': nrm(ks[19], (N_ODD, D, N_EXPERTS), D ** -0.5),
        'odd_moe_wg': nrm(ks[20], (N_ODD, N_EXPERTS, D, D_FF_EXPERT), D ** -0.5),
        'odd_moe_wu': nrm(ks[21], (N_ODD, N_EXPERTS, D, D_FF_EXPERT), D ** -0.5),
        'odd_moe_wd': nrm(ks[22], (N_ODD, N_EXPERTS, D_FF_EXPERT, D), D_FF_EXPERT ** -0.5 * DEEPNORM_BETA),
        'odd_ln2_g': 1.0 + nrm(ks[23], (N_ODD, D), 0.02),
        'odd_ln2_b': nrm(ks[24], (N_ODD, D), 0.02),
    }


def reference(x, rel_bias, even_w_in, even_conv_w, even_conv_b, even_conv_ln_g, even_conv_ln_b,
              even_w_out, even_ln1_g, even_ln1_b, even_ffn_wg, even_ffn_wu, even_ffn_wd,
              even_ln2_g, even_ln2_b, odd_w_in, odd_w_out, odd_ln1_g, odd_ln1_b, odd_router,
              odd_moe_wg, odd_moe_wu, odd_moe_wd, odd_ln2_g, odd_ln2_b):
    for layer in range(DEPTH):
        i = layer // 2
        if layer % 2 == 0:
            x = even_layer(x, rel_bias, even_w_in[i], even_conv_w[i], even_conv_b[i], even_conv_ln_g[i],
                           even_conv_ln_b[i], even_w_out[i], even_ln1_g[i], even_ln1_b[i],
                           even_ffn_wg[i], even_ffn_wu[i], even_ffn_wd[i], even_ln2_g[i], even_ln2_b[i])
        else:
            x = odd_layer(x, rel_bias, odd_w_in[i], odd_w_out[i], odd_ln1_g[i], odd_ln1_b[i], odd_router[i],
                          odd_moe_wg[i], odd_moe_wu[i], odd_moe_wd[i], odd_ln2_g[i], odd_ln2_b[i])
    return x
```

```python
import functools
import math

import jax
import jax.numpy as jnp
from jax import lax
from jax.experimental import pallas as pl
from jax.experimental.pallas import tpu as pltpu

BF16 = jnp.bfloat16
F32 = jnp.float32

D_MODEL = 1024
CONV_CH = 512
CONV_WIDTH = 31
DSA_HEADS = 8
DSA_HEAD_DIM = 64
IDX_HEADS = 8
IDX_DIM = 64
DSA_TOPK_MAX = 256
Q_BLOCK = 128
DIL_GROUPS = ((128, 1), (512, 4), (2048, 16))
DIL_HEADS = 8
DIL_HEAD_DIM = 128
BAND_BLOCK = 128
NUM_BUCKETS = 32
MAX_DISTANCE = 2048
N_EXPERTS = 8
LN_EPS = 1e-5
NEG = -1e30
DEPTH = 2
DEEPNORM_ALPHA = (2 * DEPTH) ** 0.25

LANES = 128
EVEN_IN_PAD = 3200
INT_MIN = -(2 ** 31)
DIL_CHUNK = BAND_BLOCK * max(d for _, d in DIL_GROUPS)

_PARALLEL2 = pltpu.CompilerParams(dimension_semantics=("parallel", "parallel"))


def _layer_norm(y, g, b):
    mu = jnp.mean(y, axis=-1, keepdims=True)
    yc = y - mu
    var = jnp.mean(yc * yc, axis=-1, keepdims=True)
    return yc * lax.rsqrt(var + LN_EPS) * g + b


def _sigmoid(x):
    return 1.0 / (1.0 + jnp.exp(-x))


def _dot_nt(a, b):
    return lax.dot_general(a, b, (((1,), (1,)), ((), ())), preferred_element_type=F32)


def _dot(a, b):
    return jnp.dot(a, b, preferred_element_type=F32)


def _mm_kernel(x_ref, w_ref, o_ref):
    o_ref[...] = _dot(x_ref[...], w_ref[...]).astype(o_ref.dtype)


def _matmul(x, w, tm, tn, out_dtype):
    t, k = x.shape
    n = w.shape[1]
    return pl.pallas_call(
        _mm_kernel,
        grid=(t // tm, n // tn),
        in_specs=[pl.BlockSpec((tm, k), lambda i, j: (i, 0)),
                  pl.BlockSpec((k, tn), lambda i, j: (0, j))],
        out_specs=pl.BlockSpec((tm, tn), lambda i, j: (i, j)),
        out_shape=jax.ShapeDtypeStruct((t, n), out_dtype),
        compiler_params=_PARALLEL2,
        name="matmul",
    )(x, w)


def _mm_perm_kernel(x_ref, w_ref, o_ref, acc_ref, *, dil, tm, tn):
    acc = _dot(x_ref[...], w_ref[...])
    rows = tm // dil
    for c in range(tn // LANES):
        acc_ref[c] = acc[:, c * LANES:(c + 1) * LANES]
    for r in range(dil):
        for c in range(tn // LANES):
            o_ref[0, r, :, c * LANES:(c + 1) * LANES] = (
                acc_ref[c, pl.ds(r, rows, stride=dil), :].astype(o_ref.dtype))


def _matmul_residue_major(x, w, batch, seq, dil, tm, tn):
    t, k = x.shape
    n = w.shape[1]
    if dil == 1:
        return _matmul(x, w, tm, tn, BF16).reshape(batch, seq, n)
    tiles_per_seq = seq // tm
    out = pl.pallas_call(
        functools.partial(_mm_perm_kernel, dil=dil, tm=tm, tn=tn),
        grid=(t // tm, n // tn),
        in_specs=[pl.BlockSpec((tm, k), lambda i, j: (i, 0)),
                  pl.BlockSpec((k, tn), lambda i, j: (0, j))],
        out_specs=pl.BlockSpec((1, dil, tm // dil, tn),
                               lambda i, j: (i // tiles_per_seq, 0, i % tiles_per_seq, j)),
        out_shape=jax.ShapeDtypeStruct((batch, dil, seq // dil, n), BF16),
        scratch_shapes=[pltpu.VMEM((tn // LANES, tm, LANES), F32)],
        compiler_params=_PARALLEL2,
        name="matmul_residue_major",
    )(x, w)
    return out.reshape(batch, seq, n)


_CONV_HALO = 32
_CONV_ROWS = 64


def _conv_kernel(val_ref, gate_ref, w_ref, cb_ref, g_ref, b_ref, o_ref, ext_ref, *, ts):
    s = pl.program_id(1)

    @pl.when(s == 0)
    def _():
        ext_ref[0:_CONV_HALO, :] = jnp.zeros((_CONV_HALO, CONV_CH), F32)

    val = val_ref[0].astype(F32)
    gate = gate_ref[0].astype(F32)
    ext_ref[_CONV_HALO:_CONV_HALO + ts, :] = val * _sigmoid(gate)
    first_tap = _CONV_HALO - (CONV_WIDTH - 1)
    for rc in range(ts // _CONV_ROWS):
        r0 = rc * _CONV_ROWS
        acc = jnp.zeros((_CONV_ROWS, CONV_CH), F32)
        for j in range(CONV_WIDTH):
            lo = r0 + first_tap + j
            acc = acc + ext_ref[lo:lo + _CONV_ROWS, :] * w_ref[j:j + 1, :]
        y = _layer_norm(acc + cb_ref[...], g_ref[...], b_ref[...])
        o_ref[0, r0:r0 + _CONV_ROWS, :] = (y * _sigmoid(y)).astype(o_ref.dtype)
    ext_ref[0:_CONV_HALO, :] = ext_ref[ts:ts + _CONV_HALO, :]


def _conformer_conv(h, conv_w, conv_b, ln_g, ln_b, ts=512):
    b, s, _ = h.shape
    w_pad = jnp.zeros((32, CONV_CH), F32).at[:CONV_WIDTH].set(conv_w)
    vec = lambda v: v.reshape(1, CONV_CH).astype(F32)
    full = lambda shape: pl.BlockSpec(shape, lambda bi, si: (0, 0))
    return pl.pallas_call(
        functools.partial(_conv_kernel, ts=ts),
        grid=(b, s // ts),
        in_specs=[pl.BlockSpec((1, ts, CONV_CH), lambda bi, si: (bi, si, 0)),
                  pl.BlockSpec((1, ts, CONV_CH), lambda bi, si: (bi, si, 1)),
                  full((32, CONV_CH)), full((1, CONV_CH)), full((1, CONV_CH)), full((1, CONV_CH))],
        out_specs=pl.BlockSpec((1, ts, CONV_CH), lambda bi, si: (bi, si, 0)),
        out_shape=jax.ShapeDtypeStruct((b, s, CONV_CH), BF16),
        scratch_shapes=[pltpu.VMEM((ts + _CONV_HALO, CONV_CH), F32)],
        compiler_params=pltpu.CompilerParams(dimension_semantics=("parallel", "arbitrary")),
        name="conformer_conv",
    )(h, h, w_pad, vec(conv_b), vec(ln_g), vec(ln_b))


def _rel_bucket(dist):
    max_exact = NUM_BUCKETS // 2
    n = dist.astype(jnp.int32)
    nf = jnp.maximum(n, 1).astype(F32)
    large = max_exact + (jnp.log(nf / max_exact) / math.log(MAX_DISTANCE / max_exact)
                         * (NUM_BUCKETS - max_exact)).astype(jnp.int32)
    large = jnp.minimum(large, NUM_BUCKETS - 1)
    return jnp.where(n < max_exact, n, large)


def _bias_by_distance(rel_bias, n):
    return rel_bias[_rel_bucket(jnp.arange(n, dtype=jnp.int32))].astype(F32)


def _dsa_bias_tiles(rel_bias, seq):
    nblk = seq // Q_BLOCK
    f = _bias_by_distance(rel_bias, seq)
    d = jnp.arange(-1, nblk, dtype=jnp.int32)[:, None, None] * Q_BLOCK
    r = jnp.arange(Q_BLOCK, dtype=jnp.int32)[None, :, None]
    c = jnp.arange(Q_BLOCK, dtype=jnp.int32)[None, None, :]
    tiles = f[jnp.maximum(d + r - c, 0)]
    return jnp.moveaxis(tiles, -1, 1)


def _dil_bias_tiles(rel_bias):
    max_dil = max(d for _, d in DIL_GROUPS)
    f = _bias_by_distance(rel_bias, 2 * BAND_BLOCK * max_dil)
    i = jnp.arange(BAND_BLOCK, dtype=jnp.int32)[:, None]
    j = jnp.arange(2 * BAND_BLOCK, dtype=jnp.int32)[None, :]
    dist = jnp.maximum(BAND_BLOCK + i - j, 0)
    tiles = jnp.stack([f[dist * dil] for _, dil in DIL_GROUPS])
    return jnp.moveaxis(tiles, -1, 1)


_KC = 2 * Q_BLOCK


def _sortable_key(score):
    score = jnp.where(score == 0.0, 0.0, score)
    bits = lax.bitcast_convert_type(score, jnp.int32)
    return jnp.where(bits < 0, bits ^ jnp.int32(0x7FFFFFFF), bits)


def _dsa_kernel(q_ref, qi_ref, wq_ref, k_ref, v_ref, ki_ref, bias_hbm, o_ref,
                bias_ref, keys_ref, wb_ref, qm_ref, sem, *, topk):
    b = pl.program_id(0)
    i = pl.program_id(1)

    @pl.when((b == 0) & (i == 0))
    def _():
        cp = pltpu.make_async_copy(bias_hbm, bias_ref, sem)
        cp.start()
        cp.wait()

    nch = (i + 2) // 2
    row = lax.broadcasted_iota(jnp.int32, (Q_BLOCK, _KC), 0)
    col = lax.broadcasted_iota(jnp.int32, (Q_BLOCK, _KC), 1)
    q_pos = i * Q_BLOCK + row

    w_scaled = wq_ref[0][:, IDX_DIM:IDX_DIM + IDX_HEADS].astype(F32) * (IDX_HEADS ** -0.5 * IDX_DIM ** -0.5)
    for h in range(IDX_HEADS):
        wb_ref[h] = jnp.broadcast_to(w_scaled[:, h:h + 1], (Q_BLOCK, LANES))
    lane = lax.broadcasted_iota(jnp.int32, (Q_BLOCK, LANES), 1)
    low_half = lane < DSA_HEAD_DIM
    qi = qi_ref[0].astype(F32)
    qi_heads = []
    for p in range(IDX_HEADS // 2):
        pair = qi[:, p * LANES:(p + 1) * LANES]
        qi_heads.append(jnp.where(low_half, pair, 0.0))
        qi_heads.append(jnp.where(low_half, pltpu.roll(pair, IDX_DIM, 1), 0.0))
    qi_stack = jnp.concatenate(qi_heads, axis=0).astype(BF16)
    q_scaled = (q_ref[0].astype(F32) * (DSA_HEAD_DIM ** -0.5)).astype(BF16)
    for p in range(DSA_HEADS // 2):
        qp = q_scaled[:, p * LANES:(p + 1) * LANES]
        qm_ref[2 * p] = jnp.where(low_half, qp, jnp.zeros_like(qp))
        qm_ref[2 * p + 1] = jnp.where(low_half, jnp.zeros_like(qp), qp)

    def score_body(c, carry):
        k0 = pl.multiple_of(c * _KC, _KC)
        ki = ki_ref[0, pl.ds(k0, _KC), :]
        dots = _dot_nt(qi_stack, ki)
        score = jnp.zeros((Q_BLOCK, _KC), F32)
        for h in range(IDX_HEADS):
            wb = wb_ref[h]
            wb2 = jnp.concatenate([wb, wb], axis=1)
            score = score + wb2 * jnp.maximum(dots[h * Q_BLOCK:(h + 1) * Q_BLOCK], 0.0)
        key = jnp.where(k0 + col <= q_pos, _sortable_key(score), jnp.int32(INT_MIN))
        keys_ref[c] = key
        return carry

    lax.fori_loop(0, nch, score_body, 0)

    def count_ge(cand):
        cb = jnp.broadcast_to(cand, (Q_BLOCK, LANES))

        def body(c, acc):
            k0 = pl.multiple_of(c * _KC, _KC)
            kk = keys_ref[c]
            acc = acc + jnp.where(kk[:, :LANES] >= cb, 1.0, 0.0)
            return acc + jnp.where(kk[:, LANES:] >= cb, 1.0, 0.0)

        acc = lax.fori_loop(0, nch, body, jnp.zeros((Q_BLOCK, LANES), F32))
        return jnp.sum(acc, axis=1, keepdims=True)

    kf = float(topk)
    thr0 = jnp.full((Q_BLOCK, 1), INT_MIN, jnp.int32)
    zero = jnp.zeros((Q_BLOCK, 1), jnp.int32)
    thr0 = jnp.where(count_ge(zero) >= kf, zero, thr0)

    def bit_body(t, thr):
        cand = thr | jnp.left_shift(jnp.int32(1), 30 - t)
        return jnp.where(count_ge(cand) >= kf, cand, thr)

    thr = lax.fori_loop(0, 31, bit_body, thr0)
    n_gt = count_ge(jnp.where(thr == jnp.int32(2 ** 31 - 1), thr, thr + 1))
    need = kf - n_gt
    thr_b = jnp.broadcast_to(thr, (Q_BLOCK, _KC))
    need_b = jnp.broadcast_to(need, (Q_BLOCK, _KC))
    tri = (lax.broadcasted_iota(jnp.int32, (_KC, _KC), 0)
           <= lax.broadcasted_iota(jnp.int32, (_KC, _KC), 1)).astype(BF16)

    n_pairs = DSA_HEADS // 2

    def attn_body(c, carry):
        ms, ls, accs, tie_carry = carry
        k0 = pl.multiple_of(c * _KC, _KC)
        kk = keys_ref[c]
        causal = kk > jnp.int32(INT_MIN)
        eq = (kk == thr_b) & causal
        eq_f = jnp.where(eq, 1.0, 0.0)
        rank = _dot(eq_f.astype(BF16), tri) + tie_carry
        sel = ((kk > thr_b) & causal) | (eq & (rank <= need_b))
        tie_carry = tie_carry + jnp.sum(eq_f, axis=1, keepdims=True)
        d = i - 2 * c
        kc = k_ref[0, pl.ds(k0, _KC), :]
        vc = v_ref[0, pl.ds(k0, _KC), :]
        new_ms, new_ls, new_accs = [], [], []
        for p in range(n_pairs):
            kp = kc[:, p * LANES:(p + 1) * LANES]
            vp = vc[:, p * LANES:(p + 1) * LANES]
            pv, alphas = [], []
            for e in range(2):
                h = 2 * p + e
                s = _dot_nt(qm_ref[h], kp)
                bias = jnp.concatenate([bias_ref[d + 1, h], bias_ref[d, h]], axis=1)
                s = jnp.where(sel, s + bias, NEG)
                m_new = jnp.maximum(ms[h], jnp.max(s, axis=1, keepdims=True))
                alpha = jnp.exp(ms[h] - m_new)
                pr = jnp.exp(s - m_new)
                new_ls.append(alpha * ls[h] + jnp.sum(pr, axis=1, keepdims=True))
                new_ms.append(m_new)
                pv.append(_dot(pr.astype(BF16), vp))
                alphas.append(alpha)
            alpha_pair = jnp.where(low_half, alphas[0], alphas[1])
            new_accs.append(alpha_pair * accs[p] + jnp.where(low_half, pv[0], pv[1]))
        return tuple(new_ms), tuple(new_ls), tuple(new_accs), tie_carry

    init = (tuple(jnp.full((Q_BLOCK, 1), NEG, F32) for _ in range(DSA_HEADS)),
            tuple(jnp.zeros((Q_BLOCK, 1), F32) for _ in range(DSA_HEADS)),
            tuple(jnp.zeros((Q_BLOCK, LANES), F32) for _ in range(n_pairs)),
            jnp.zeros((Q_BLOCK, 1), F32))
    _, ls, accs, _ = lax.fori_loop(0, nch, attn_body, init)
    for p in range(n_pairs):
        l_pair = jnp.where(low_half, ls[2 * p], ls[2 * p + 1])
        o_ref[0, :, p * LANES:(p + 1) * LANES] = (accs[p] / l_pair).astype(o_ref.dtype)


def _dsa_attention(h, bias_tiles):
    b, s, _ = h.shape
    width = DSA_HEADS * DSA_HEAD_DIM
    nblk = s // Q_BLOCK
    topk = min(DSA_TOPK_MAX, s // 4)
    kw_block = (CONV_CH * 2 + 4 * width) // LANES
    return pl.pallas_call(
        functools.partial(_dsa_kernel, topk=topk),
        grid=(b, nblk),
        in_specs=[pl.BlockSpec((1, Q_BLOCK, width), lambda bi, qi: (bi, qi, 2)),
                  pl.BlockSpec((1, Q_BLOCK, width), lambda bi, qi: (bi, qi, 5)),
                  pl.BlockSpec((1, Q_BLOCK, LANES), lambda bi, qi: (bi, qi, kw_block)),
                  pl.BlockSpec((1, s, width), lambda bi, qi: (bi, 0, 3)),
                  pl.BlockSpec((1, s, width), lambda bi, qi: (bi, 0, 4)),
                  pl.BlockSpec((1, s, LANES), lambda bi, qi: (bi, 0, kw_block)),
                  pl.BlockSpec(memory_space=pl.ANY)],
        out_specs=pl.BlockSpec((1, Q_BLOCK, width), lambda bi, qi: (bi, qi, 0)),
        out_shape=jax.ShapeDtypeStruct((b, s, width), BF16),
        scratch_shapes=[pltpu.VMEM((nblk + 1, DSA_HEADS, Q_BLOCK, Q_BLOCK), F32),
                        pltpu.VMEM((s // _KC, Q_BLOCK, _KC), jnp.int32),
                        pltpu.VMEM((IDX_HEADS, Q_BLOCK, LANES), F32),
                        pltpu.VMEM((DSA_HEADS, Q_BLOCK, LANES), BF16),
                        pltpu.SemaphoreType.DMA(())],
        compiler_params=pltpu.CompilerParams(dimension_semantics=("arbitrary", "arbitrary")),
        name="dsa_attention",
    )(h, h, h, h, h, h, bias_tiles)


def _dil_kernel(*refs, seq):
    n_g = len(DIL_GROUPS)
    qkv = refs[:3 * n_g]
    bias_ref, o_ref, og_ref, lg_ref = refs[3 * n_g:]
    c = pl.program_id(2)
    ii = lax.broadcasted_iota(jnp.int32, (BAND_BLOCK, BAND_BLOCK), 0)
    jj = lax.broadcasted_iota(jnp.int32, (BAND_BLOCK, BAND_BLOCK), 1)
    valid_prev = jj >= ii
    valid_cur = ii >= jj
    scale = DIL_HEAD_DIM ** -0.5
    for g, (window, dil) in enumerate(DIL_GROUPS):
        assert window // dil == BAND_BLOCK
        q_ref, k_ref, v_ref = qkv[3 * g:3 * g + 3]
        sub_len = seq // dil
        blocks_per_chunk = DIL_CHUNK // (BAND_BLOCK * dil)
        bias_prev = bias_ref[g, 0, :, :BAND_BLOCK]
        bias_cur = bias_ref[g, 0, :, BAND_BLOCK:]
        for r in range(dil):
            for nl in range(blocks_per_chunk):
                row0 = pl.multiple_of(r * sub_len + c * (DIL_CHUNK // dil) + nl * BAND_BLOCK, BAND_BLOCK)
                prow = pl.multiple_of(jnp.maximum(row0 - BAND_BLOCK, 0), BAND_BLOCK)
                qb = q_ref[0, pl.ds(row0, BAND_BLOCK), :]
                s_cur = _dot_nt(qb, k_ref[0, pl.ds(row0, BAND_BLOCK), :]) * scale + bias_cur
                s_prev = _dot_nt(qb, k_ref[0, pl.ds(prow, BAND_BLOCK), :]) * scale + bias_prev
                s_cur = jnp.where(valid_cur, s_cur, NEG)
                has_prev = valid_prev if nl > 0 else (valid_prev & (c > 0))
                s_prev = jnp.where(has_prev, s_prev, NEG)
                m = jnp.maximum(jnp.max(s_cur, axis=1, keepdims=True), jnp.max(s_prev, axis=1, keepdims=True))
                p_cur = jnp.exp(s_cur - m)
                p_prev = jnp.exp(s_prev - m)
                ssum = jnp.sum(p_cur, axis=1, keepdims=True) + jnp.sum(p_prev, axis=1, keepdims=True)
                o = (_dot(p_cur.astype(BF16), v_ref[0, pl.ds(row0, BAND_BLOCK), :])
                     + _dot(p_prev.astype(BF16), v_ref[0, pl.ds(prow, BAND_BLOCK), :])) / ssum
                lse = m + jnp.log(ssum)
                dst = pl.ds(r + nl * BAND_BLOCK * dil, BAND_BLOCK, stride=dil) if dil > 1 else pl.ds(
                    nl * BAND_BLOCK, BAND_BLOCK)
                og_ref[g, dst, :] = o
                lg_ref[g, dst, :] = jnp.broadcast_to(lse, (BAND_BLOCK, LANES))
    rows = 256
    for rc in range(DIL_CHUNK // rows):
        sl = slice(rc * rows, (rc + 1) * rows)
        lses = [lg_ref[g, sl, :] for g in range(n_g)]
        mx = functools.reduce(jnp.maximum, lses)
        ws = [jnp.exp(l - mx) for l in lses]
        den = functools.reduce(lambda a, b_: a + b_, ws)
        num = functools.reduce(lambda a, b_: a + b_, [ws[g] * og_ref[g, sl, :] for g in range(n_g)])
        o_ref[0, sl, :] = (num / den).astype(o_ref.dtype)


def _dilated_attention(hs, bias_tiles):
    b, s, _ = hs[0].shape
    in_specs = []
    args = []
    for hg in hs:
        for part in range(3):
            in_specs.append(pl.BlockSpec((1, s, DIL_HEAD_DIM),
                                         lambda bi, hi, ci, part=part: (bi, 0, part * DIL_HEADS + hi)))
            args.append(hg)
    in_specs.append(pl.BlockSpec((len(DIL_GROUPS), 1, BAND_BLOCK, 2 * BAND_BLOCK),
                                 lambda bi, hi, ci: (0, hi, 0, 0)))
    return pl.pallas_call(
        functools.partial(_dil_kernel, seq=s),
        grid=(b, DIL_HEADS, s // DIL_CHUNK),
        in_specs=in_specs,
        out_specs=pl.BlockSpec((1, DIL_CHUNK, DIL_HEAD_DIM), lambda bi, hi, ci: (bi, ci, hi)),
        out_shape=jax.ShapeDtypeStruct((b, s, DIL_HEADS * DIL_HEAD_DIM), BF16),
        scratch_shapes=[pltpu.VMEM((len(DIL_GROUPS), DIL_CHUNK, LANES), F32),
                        pltpu.VMEM((len(DIL_GROUPS), DIL_CHUNK, LANES), F32)],
        compiler_params=pltpu.CompilerParams(dimension_semantics=("parallel", "parallel", "arbitrary")),
        name="dilated_attention",
    )(*args, bias_tiles)


def _outproj_kernel(*refs, n_in, with_router):
    ins = refs[:n_in]
    ws = refs[n_in:2 * n_in]
    x_ref, g_ref, b_ref = refs[2 * n_in:2 * n_in + 3]
    rest = refs[2 * n_in + 3:]
    mix = _dot(ins[0][...], ws[0][...])
    for a_ref, w_ref in zip(ins[1:], ws[1:]):
        mix = mix + _dot(a_ref[...], w_ref[...])
    xn = _layer_norm(DEEPNORM_ALPHA * x_ref[...] + mix, g_ref[...], b_ref[...])
    if not with_router:
        o_ref, ob_ref = rest
    else:
        router_ref, o_ref, ob_ref, gates_ref = rest
        logits = jnp.dot(xn, router_ref[...], preferred_element_type=F32, precision=lax.Precision.HIGHEST)
        lane = lax.broadcasted_iota(jnp.int32, logits.shape, 1).astype(F32)
        lg = jnp.where(lane < N_EXPERTS, logits, -jnp.inf)
        m1 = jnp.max(lg, axis=1, keepdims=True)
        i1 = jnp.min(jnp.where(lg == m1, lane, float(LANES)), axis=1, keepdims=True)
        lg2 = jnp.where(lane == i1, -jnp.inf, lg)
        m2 = jnp.max(lg2, axis=1, keepdims=True)
        i2 = jnp.min(jnp.where(lg2 == m2, lane, float(LANES)), axis=1, keepdims=True)
        e2 = jnp.exp(m2 - m1)
        den = 1.0 + e2
        gates_ref[...] = jnp.where(lane == i1, 1.0 / den, 0.0) + jnp.where(lane == i2, e2 / den, 0.0)
    o_ref[...] = xn
    ob_ref[...] = xn.astype(BF16)


def _outproj_ln(ins, ws, x, ln_g, ln_b, router=None, tm=512):
    t, d = x.shape
    n_in = len(ins)
    row = lambda width: pl.BlockSpec((tm, width), lambda i: (i, 0))
    full = lambda a: pl.BlockSpec(a.shape, lambda i: (0, 0))
    vec = lambda v: v.reshape(1, d).astype(F32)
    args = list(ins) + list(ws) + [x, vec(ln_g), vec(ln_b)]
    in_specs = [row(a.shape[1]) for a in ins] + [full(w) for w in ws] + [row(d), full(vec(ln_g)), full(vec(ln_b))]
    out_shape = [jax.ShapeDtypeStruct((t, d), F32), jax.ShapeDtypeStruct((t, d), BF16)]
    out_specs = [row(d), row(d)]
    if router is not None:
        router_pad = jnp.zeros((d, LANES), F32).at[:, :N_EXPERTS].set(router)
        args.append(router_pad)
        in_specs.append(full(router_pad))
        out_shape.append(jax.ShapeDtypeStruct((t, LANES), F32))
        out_specs.append(row(LANES))
    return pl.pallas_call(
        functools.partial(_outproj_kernel, n_in=n_in, with_router=router is not None),
        grid=(t // tm,),
        in_specs=in_specs,
        out_specs=out_specs,
        out_shape=out_shape,
        compiler_params=pltpu.CompilerParams(dimension_semantics=("parallel",)),
        name="outproj_ln",
    )(*args)


def _ffn_kernel(xb_ref, x_ref, wg_ref, wu_ref, wd_ref, g_ref, b_ref, o_ref, ob_ref, acc_ref):
    f = pl.program_id(1)

    @pl.when(f == 0)
    def _():
        acc_ref[...] = jnp.zeros_like(acc_ref)

    xb = xb_ref[...]
    hg = _dot(xb, wg_ref[...])
    hu = _dot(xb, wu_ref[...])
    act = (hg * _sigmoid(hg) * hu).astype(BF16)
    acc_ref[...] += _dot(act, wd_ref[...])

    @pl.when(f == pl.num_programs(1) - 1)
    def _():
        xn = _layer_norm(DEEPNORM_ALPHA * x_ref[...] + acc_ref[...], g_ref[...], b_ref[...])
        o_ref[...] = xn
        ob_ref[...] = xn.astype(BF16)


def _ffn_ln(xb, x, wg, wu, wd, ln_g, ln_b, tm=512, tf=1408):
    t, d = x.shape
    ff = wg.shape[1]
    vec = lambda v: v.reshape(1, d).astype(F32)
    return pl.pallas_call(
        _ffn_kernel,
        grid=(t // tm, ff // tf),
        in_specs=[pl.BlockSpec((tm, d), lambda i, f: (i, 0)),
                  pl.BlockSpec((tm, d), lambda i, f: (i, 0)),
                  pl.BlockSpec((d, tf), lambda i, f: (0, f)),
                  pl.BlockSpec((d, tf), lambda i, f: (0, f)),
                  pl.BlockSpec((tf, d), lambda i, f: (f, 0)),
                  pl.BlockSpec((1, d), lambda i, f: (0, 0)),
                  pl.BlockSpec((1, d), lambda i, f: (0, 0))],
        out_specs=[pl.BlockSpec((tm, d), lambda i, f: (i, 0)),
                   pl.BlockSpec((tm, d), lambda i, f: (i, 0))],
        out_shape=[jax.ShapeDtypeStruct((t, d), F32), jax.ShapeDtypeStruct((t, d), BF16)],
        scratch_shapes=[pltpu.VMEM((tm, d), F32)],
        compiler_params=pltpu.CompilerParams(dimension_semantics=("parallel", "arbitrary")),
        name="ffn_ln",
    )(xb, x, wg, wu, wd, vec(ln_g), vec(ln_b))


def _moe_kernel(xb_ref, x_ref, gates_ref, wg_ref, wu_ref, wd_ref, g_ref, b_ref, o_ref, acc_ref):
    e = pl.program_id(1)
    f = pl.program_id(2)

    @pl.when((e == 0) & (f == 0))
    def _():
        acc_ref[...] = jnp.zeros_like(acc_ref)

    gates = gates_ref[...]
    lane = lax.broadcasted_iota(jnp.int32, gates.shape, 1)
    gate_e = jnp.sum(jnp.where(lane == e, gates, 0.0), axis=1, keepdims=True)
    xb = xb_ref[...]
    hg = _dot(xb, wg_ref[0])
    hu = _dot(xb, wu_ref[0])
    act = (hg * _sigmoid(hg) * hu * gate_e).astype(BF16)
    acc_ref[...] += _dot(act, wd_ref[0])

    @pl.when((e == pl.num_programs(1) - 1) & (f == pl.num_programs(2) - 1))
    def _():
        o_ref[...] = _layer_norm(DEEPNORM_ALPHA * x_ref[...] + acc_ref[...], g_ref[...], b_ref[...])


def _moe_ln(xb, x, gates, wg, wu, wd, ln_g, ln_b, tm=512, tf=896):
    t, d = x.shape
    n_e, _, ff = wg.shape
    vec = lambda v: v.reshape(1, d).astype(F32)
    return pl.pallas_call(
        _moe_kernel,
        grid=(t // tm, n_e, ff // tf),
        in_specs=[pl.BlockSpec((tm, d), lambda i, e, f: (i, 0)),
                  pl.BlockSpec((tm, d), lambda i, e, f: (i, 0)),
                  pl.BlockSpec((tm, LANES), lambda i, e, f: (i, 0)),
                  pl.BlockSpec((1, d, tf), lambda i, e, f: (e, 0, f)),
                  pl.BlockSpec((1, d, tf), lambda i, e, f: (e, 0, f)),
                  pl.BlockSpec((1, tf, d), lambda i, e, f: (e, f, 0)),
                  pl.BlockSpec((1, d), lambda i, e, f: (0, 0)),
                  pl.BlockSpec((1, d), lambda i, e, f: (0, 0))],
        out_specs=pl.BlockSpec((tm, d), lambda i, e, f: (i, 0)),
        out_shape=jax.ShapeDtypeStruct((t, d), F32),
        scratch_shapes=[pltpu.VMEM((tm, d), F32)],
        compiler_params=pltpu.CompilerParams(dimension_semantics=("parallel", "arbitrary", "arbitrary")),
        name="moe_ln",
    )(xb, x, gates, wg, wu, wd, vec(ln_g), vec(ln_b))


def _even_layer(x, xb, batch, seq, dsa_bias, w_in, conv_w, conv_b, conv_ln_g, conv_ln_b, w_out,
                ln1_g, ln1_b, ffn_wg, ffn_wu, ffn_wd, ln2_g, ln2_b):
    w_in_p = jnp.zeros((D_MODEL, EVEN_IN_PAD), BF16).at[:, :w_in.shape[1]].set(w_in.astype(BF16))
    h = _matmul(xb, w_in_p, 1024, 640, BF16).reshape(batch, seq, EVEN_IN_PAD)
    a_out = _conformer_conv(h, conv_w, conv_b, conv_ln_g, conv_ln_b)
    att = _dsa_attention(h, dsa_bias)
    w_out_b = w_out.astype(BF16)
    x1, x1b = _outproj_ln([a_out.reshape(-1, CONV_CH), att.reshape(-1, DSA_HEADS * DSA_HEAD_DIM)],
                          [w_out_b[:CONV_CH], w_out_b[CONV_CH:]], x, ln1_g, ln1_b)
    return _ffn_ln(x1b, x1, ffn_wg.astype(BF16), ffn_wu.astype(BF16), ffn_wd.astype(BF16), ln2_g, ln2_b)


def _odd_layer(x, xb, batch, seq, dil_bias, w_in, w_out, ln1_g, ln1_b, router, moe_wg, moe_wu, moe_wd,
               ln2_g, ln2_b):
    gw = 3 * DIL_HEADS * DIL_HEAD_DIM
    w_in_b = w_in.astype(BF16)
    hs = [_matmul_residue_major(xb, w_in_b[:, g * gw:(g + 1) * gw], batch, seq, dil, 1024, 768)
          for g, (_, dil) in enumerate(DIL_GROUPS)]
    o = _dilated_attention(hs, dil_bias).reshape(-1, DIL_HEADS * DIL_HEAD_DIM)
    x1, x1b, gates = _outproj_ln([o], [w_out.astype(BF16)], x, ln1_g, ln1_b, router=router)
    out = _moe_ln(x1b, x1, gates, moe_wg.astype(BF16), moe_wu.astype(BF16), moe_wd.astype(BF16), ln2_g, ln2_b)
    return out, out.astype(BF16)


def kernel(x, rel_bias, even_w_in, even_conv_w, even_conv_b, even_conv_ln_g, even_conv_ln_b, even_w_out, even_ln1_g, even_ln1_b, even_ffn_wg, even_ffn_wu, even_ffn_wd, even_ln2_g, even_ln2_b, odd_w_in, odd_w_out, odd_ln1_g, odd_ln1_b, odd_router, odd_moe_wg, odd_moe_wu, odd_moe_wd, odd_ln2_g, odd_ln2_b):
    batch, seq, d = x.shape
    assert d == D_MODEL and seq % DIL_CHUNK == 0
    depth = even_w_in.shape[0] + odd_w_in.shape[0]
    assert depth == DEPTH
    dsa_bias = _dsa_bias_tiles(rel_bias, seq)
    dil_bias = _dil_bias_tiles(rel_bias)
    xf = x.reshape(batch * seq, d)
    xb = xf.astype(BF16)
    for layer in range(depth):
        i = layer // 2
        if layer % 2 == 0:
            xf, xb = _even_layer(xf, xb, batch, seq, dsa_bias, even_w_in[i], even_conv_w[i], even_conv_b[i],
                                 even_conv_ln_g[i], even_conv_ln_b[i], even_w_out[i], even_ln1_g[i],
                                 even_ln1_b[i], even_ffn_wg[i], even_ffn_wu[i], even_ffn_wd[i],
                                 even_ln2_g[i], even_ln2_b[i])
        else:
            xf, xb = _odd_layer(xf, xb, batch, seq, dil_bias, odd_w_in[i], odd_w_out[i], odd_ln1_g[i],
                                odd_ln1_b[i], odd_router[i], odd_moe_wg[i], odd_moe_wu[i], odd_moe_wd[i],
                                odd_ln2_g[i], odd_ln2_b[i])
    return xf.reshape(batch, seq, d)
```

```python
import functools
import math

import jax
import jax.numpy as jnp
from jax import lax
from jax.experimental import pallas as pl
from jax.experimental.pallas import tpu as pltpu

BF16 = jnp.bfloat16
F32 = jnp.float32

D_MODEL = 1024
CONV_CH = 512
CONV_WIDTH = 31
DSA_HEADS = 8
DSA_HEAD_DIM = 64
IDX_HEADS = 8
IDX_DIM = 64
DSA_TOPK_MAX = 256
Q_BLOCK = 128
DIL_GROUPS = ((128, 1), (512, 4), (2048, 16))
DIL_HEADS = 8
DIL_HEAD_DIM = 128
BAND_BLOCK = 128
NUM_BUCKETS = 32
MAX_DISTANCE = 2048
N_EXPERTS = 8
LN_EPS = 1e-5
NEG = -1e30
DEPTH = 2
DEEPNORM_ALPHA = (2 * DEPTH) ** 0.25

LANES = 128
EVEN_IN_PAD = 3200
INT_MIN = -(2 ** 31)
DIL_CHUNK = BAND_BLOCK * max(d for _, d in DIL_GROUPS)

_PARALLEL2 = pltpu.CompilerParams(dimension_semantics=("parallel", "parallel"))


def _layer_norm(y, g, b):
    mu = jnp.mean(y, axis=-1, keepdims=True)
    yc = y - mu
    var = jnp.mean(yc * yc, axis=-1, keepdims=True)
    return yc * lax.rsqrt(var + LN_EPS) * g + b


def _sigmoid(x):
    return 1.0 / (1.0 + jnp.exp(-x))


def _dot_nt(a, b):
    return lax.dot_general(a, b, (((1,), (1,)), ((), ())), preferred_element_type=F32)


def _dot(a, b):
    return jnp.dot(a, b, preferred_element_type=F32)


def _mm_kernel(x_ref, w_ref, o_ref):
    o_ref[...] = _dot(x_ref[...], w_ref[...]).astype(o_ref.dtype)


def _matmul(x, w, tm, tn, out_dtype):
    t, k = x.shape
    n = w.shape[1]
    return pl.pallas_call(
        _mm_kernel,
        grid=(t // tm, n // tn),
        in_specs=[pl.BlockSpec((tm, k), lambda i, j: (i, 0)),
                  pl.BlockSpec((k, tn), lambda i, j: (0, j))],
        out_specs=pl.BlockSpec((tm, tn), lambda i, j: (i, j)),
        out_shape=jax.ShapeDtypeStruct((t, n), out_dtype),
        compiler_params=_PARALLEL2,
        name="matmul",
    )(x, w)


def _mm_perm_kernel(x_ref, w_ref, o_ref, acc_ref, *, dil, tm, tn):
    acc = _dot(x_ref[...], w_ref[...])
    rows = tm // dil
    for c in range(tn // LANES):
        acc_ref[c] = acc[:, c * LANES:(c + 1) * LANES]
    for r in range(dil):
        for c in range(tn // LANES):
            o_ref[0, r, :, c * LANES:(c + 1) * LANES] = (
                acc_ref[c, pl.ds(r, rows, stride=dil), :].astype(o_ref.dtype))


def _matmul_residue_major(x, w, batch, seq, dil, tm, tn):
    t, k = x.shape
    n = w.shape[1]
    if dil == 1:
        return _matmul(x, w, tm, tn, BF16).reshape(batch, seq, n)
    tiles_per_seq = seq // tm
    out = pl.pallas_call(
        functools.partial(_mm_perm_kernel, dil=dil, tm=tm, tn=tn),
        grid=(t // tm, n // tn),
        in_specs=[pl.BlockSpec((tm, k), lambda i, j: (i, 0)),
                  pl.BlockSpec((k, tn), lambda i, j: (0, j))],
        out_specs=pl.BlockSpec((1, dil, tm // dil, tn),
                               lambda i, j: (i // tiles_per_seq, 0, i % tiles_per_seq, j)),
        out_shape=jax.ShapeDtypeStruct((batch, dil, seq // dil, n), BF16),
        scratch_shapes=[pltpu.VMEM((tn // LANES, tm, LANES), F32)],
        compiler_params=_PARALLEL2,
        name="matmul_residue_major",
    )(x, w)
    return out.reshape(batch, seq, n)


_CONV_HALO = 32
_CONV_ROWS = 64


def _conv_kernel(val_ref, gate_ref, w_ref, cb_ref, g_ref, b_ref, o_ref, ext_ref, *, ts):
    s = pl.program_id(1)

    @pl.when(s == 0)
    def _():
        ext_ref[0:_CONV_HALO, :] = jnp.zeros((_CONV_HALO, CONV_CH), F32)

    val = val_ref[0].astype(F32)
    gate = gate_ref[0].astype(F32)
    ext_ref[_CONV_HALO:_CONV_HALO + ts, :] = val * _sigmoid(gate)
    first_tap = _CONV_HALO - (CONV_WIDTH - 1)
    for rc in range(ts // _CONV_ROWS):
        r0 = rc * _CONV_ROWS
        acc = jnp.zeros((_CONV_ROWS, CONV_CH), F32)
        for j in range(CONV_WIDTH):
            lo = r0 + first_tap + j
            acc = acc + ext_ref[lo:lo + _CONV_ROWS, :] * w_ref[j:j + 1, :]
        y = _layer_norm(acc + cb_ref[...], g_ref[...], b_ref[...])
        o_ref[0, r0:r0 + _CONV_ROWS, :] = (y * _sigmoid(y)).astype(o_ref.dtype)
    ext_ref[0:_CONV_HALO, :] = ext_ref[ts:ts + _CONV_HALO, :]


def _conformer_conv(h, conv_w, conv_b, ln_g, ln_b, ts=512):
    b, s, _ = h.shape
    w_pad = jnp.zeros((32, CONV_CH), F32).at[:CONV_WIDTH].set(conv_w)
    vec = lambda v: v.reshape(1, CONV_CH).astype(F32)
    full = lambda shape: pl.BlockSpec(shape, lambda bi, si: (0, 0))
    return pl.pallas_call(
        functools.partial(_conv_kernel, ts=ts),
        grid=(b, s // ts),
        in_specs=[pl.BlockSpec((1, ts, CONV_CH), lambda bi, si: (bi, si, 0)),
                  pl.BlockSpec((1, ts, CONV_CH), lambda bi, si: (bi, si, 1)),
                  full((32, CONV_CH)), full((1, CONV_CH)), full((1, CONV_CH)), full((1, CONV_CH))],
        out_specs=pl.BlockSpec((1, ts, CONV_CH), lambda bi, si: (bi, si, 0)),
        out_shape=jax.ShapeDtypeStruct((b, s, CONV_CH), BF16),
        scratch_shapes=[pltpu.VMEM((ts + _CONV_HALO, CONV_CH), F32)],
        compiler_params=pltpu.CompilerParams(dimension_semantics=("parallel", "arbitrary")),
        name="conformer_conv",
    )(h, h, w_pad, vec(conv_b), vec(ln_g), vec(ln_b))


def _rel_bucket(dist):
    max_exact = NUM_BUCKETS // 2
    n = dist.astype(jnp.int32)
    nf = jnp.maximum(n, 1).astype(F32)
    large = max_exact + (jnp.log(nf / max_exact) / math.log(MAX_DISTANCE / max_exact)
                         * (NUM_BUCKETS - max_exact)).astype(jnp.int32)
    large = jnp.minimum(large, NUM_BUCKETS - 1)
    return jnp.where(n < max_exact, n, large)


def _bias_by_distance(rel_bias, n):
    return rel_bias[_rel_bucket(jnp.arange(n, dtype=jnp.int32))].astype(F32).T


def _reversed_clamped(f, top, length):
    tail = jnp.broadcast_to(f[:, :1], (f.shape[0], length - top - 1))
    return jnp.concatenate([f[:, ::-1], tail], axis=1)


def _toeplitz_rows(v, rows, m0, width):
    h, length = v.shape
    assert m0 >= rows - 1 and m0 + width <= length - 1
    skew = jnp.tile(v, (1, rows))[:, :rows * (length - 1)].reshape(h, rows, length - 1)
    return skew[:, :, m0:m0 + width]


def _dsa_bias_tiles(rel_bias, seq):
    nblk = seq // Q_BLOCK
    f = _bias_by_distance(rel_bias, seq + 1)
    v = _reversed_clamped(f, seq, Q_BLOCK * (nblk + 3))
    t = _toeplitz_rows(v, Q_BLOCK, Q_BLOCK, Q_BLOCK * (nblk + 1))
    return t.reshape(f.shape[0], Q_BLOCK, nblk + 1, Q_BLOCK).transpose(2, 0, 1, 3)


def _dil_bias_tiles(rel_bias):
    max_dil = max(d for _, d in DIL_GROUPS)
    span = 2 * BAND_BLOCK
    f = _bias_by_distance(rel_bias, span * max_dil + 1)
    tiles = []
    for _, dil in DIL_GROUPS:
        fd = f[:, 0:span * dil + 1:dil]
        v = _reversed_clamped(fd, span, 2 * span)
        tiles.append(_toeplitz_rows(v, BAND_BLOCK, BAND_BLOCK, span))
    return jnp.stack(tiles)


_KC = 2 * Q_BLOCK


def _sortable_key(score):
    score = jnp.where(score == 0.0, 0.0, score)
    bits = lax.bitcast_convert_type(score, jnp.int32)
    return jnp.where(bits < 0, bits ^ jnp.int32(0x7FFFFFFF), bits)


def _dsa_kernel(q_ref, qi_ref, wq_ref, k_ref, v_ref, ki_ref, bias_hbm, o_ref,
                bias_ref, keys_ref, wb_ref, qm_ref, sem, *, topk):
    b = pl.program_id(0)
    i = pl.program_id(1)

    @pl.when((b == 0) & (i == 0))
    def _():
        cp = pltpu.make_async_copy(bias_hbm, bias_ref, sem)
        cp.start()
        cp.wait()

    nch = (i + 2) // 2
    row = lax.broadcasted_iota(jnp.int32, (Q_BLOCK, _KC), 0)
    col = lax.broadcasted_iota(jnp.int32, (Q_BLOCK, _KC), 1)
    q_pos = i * Q_BLOCK + row

    w_scaled = wq_ref[0][:, IDX_DIM:IDX_DIM + IDX_HEADS].astype(F32) * (IDX_HEADS ** -0.5 * IDX_DIM ** -0.5)
    for h in range(IDX_HEADS):
        wb_ref[h] = jnp.broadcast_to(w_scaled[:, h:h + 1], (Q_BLOCK, LANES))
    lane = lax.broadcasted_iota(jnp.int32, (Q_BLOCK, LANES), 1)
    low_half = lane < DSA_HEAD_DIM
    qi = qi_ref[0].astype(F32)
    qi_heads = []
    for p in range(IDX_HEADS // 2):
        pair = qi[:, p * LANES:(p + 1) * LANES]
        qi_heads.append(jnp.where(low_half, pair, 0.0))
        qi_heads.append(jnp.where(low_half, pltpu.roll(pair, IDX_DIM, 1), 0.0))
    qi_stack = jnp.concatenate(qi_heads, axis=0).astype(BF16)
    q_scaled = (q_ref[0].astype(F32) * (DSA_HEAD_DIM ** -0.5)).astype(BF16)
    for p in range(DSA_HEADS // 2):
        qp = q_scaled[:, p * LANES:(p + 1) * LANES]
        qm_ref[2 * p] = jnp.where(low_half, qp, jnp.zeros_like(qp))
        qm_ref[2 * p + 1] = jnp.where(low_half, jnp.zeros_like(qp), qp)

    def score_body(c, carry):
        k0 = pl.multiple_of(c * _KC, _KC)
        ki = ki_ref[0, pl.ds(k0, _KC), :]
        dots = _dot_nt(qi_stack, ki)
        score = jnp.zeros((Q_BLOCK, _KC), F32)
        for h in range(IDX_HEADS):
            wb = wb_ref[h]
            wb2 = jnp.concatenate([wb, wb], axis=1)
            score = score + wb2 * jnp.maximum(dots[h * Q_BLOCK:(h + 1) * Q_BLOCK], 0.0)
        key = jnp.where(k0 + col <= q_pos, _sortable_key(score), jnp.int32(INT_MIN))
        keys_ref[c] = key
        return carry

    lax.fori_loop(0, nch, score_body, 0)

    def count_ge(cand):
        cb = jnp.broadcast_to(cand, (Q_BLOCK, LANES))

        def body(c, acc):
            k0 = pl.multiple_of(c * _KC, _KC)
            kk = keys_ref[c]
            acc = acc + jnp.where(kk[:, :LANES] >= cb, 1.0, 0.0)
            return acc + jnp.where(kk[:, LANES:] >= cb, 1.0, 0.0)

        acc = lax.fori_loop(0, nch, body, jnp.zeros((Q_BLOCK, LANES), F32))
        return jnp.sum(acc, axis=1, keepdims=True)

    kf = float(topk)
    thr0 = jnp.full((Q_BLOCK, 1), INT_MIN, jnp.int32)
    zero = jnp.zeros((Q_BLOCK, 1), jnp.int32)
    thr0 = jnp.where(count_ge(zero) >= kf, zero, thr0)

    def bit_body(t, thr):
        cand = thr | jnp.left_shift(jnp.int32(1), 30 - t)
        return jnp.where(count_ge(cand) >= kf, cand, thr)

    thr = lax.fori_loop(0, 31, bit_body, thr0)
    n_gt = count_ge(jnp.where(thr == jnp.int32(2 ** 31 - 1), thr, thr + 1))
    need = kf - n_gt
    thr_b = jnp.broadcast_to(thr, (Q_BLOCK, _KC))
    need_b = jnp.broadcast_to(need, (Q_BLOCK, _KC))
    tri = (lax.broadcasted_iota(jnp.int32, (_KC, _KC), 0)
           <= lax.broadcasted_iota(jnp.int32, (_KC, _KC), 1)).astype(BF16)

    n_pairs = DSA_HEADS // 2

    def attn_body(c, carry):
        ms, ls, accs, tie_carry = carry
        k0 = pl.multiple_of(c * _KC, _KC)
        kk = keys_ref[c]
        causal = kk > jnp.int32(INT_MIN)
        eq = (kk == thr_b) & causal
        eq_f = jnp.where(eq, 1.0, 0.0)
        rank = _dot(eq_f.astype(BF16), tri) + tie_carry
        sel = ((kk > thr_b) & causal) | (eq & (rank <= need_b))
        tie_carry = tie_carry + jnp.sum(eq_f, axis=1, keepdims=True)
        d = i - 2 * c
        far = bias_ref.shape[0] - 2
        kc = k_ref[0, pl.ds(k0, _KC), :]
        vc = v_ref[0, pl.ds(k0, _KC), :]
        new_ms, new_ls, new_accs = [], [], []
        for p in range(n_pairs):
            kp = kc[:, p * LANES:(p + 1) * LANES]
            vp = vc[:, p * LANES:(p + 1) * LANES]
            pv, alphas = [], []
            for e in range(2):
                h = 2 * p + e
                s = _dot_nt(qm_ref[h], kp)
                bias = jnp.concatenate([bias_ref[far - d, h], bias_ref[far - d + 1, h]], axis=1)
                s = jnp.where(sel, s + bias, NEG)
                m_new = jnp.maximum(ms[h], jnp.max(s, axis=1, keepdims=True))
                alpha = jnp.exp(ms[h] - m_new)
                pr = jnp.exp(s - m_new)
                new_ls.append(alpha * ls[h] + jnp.sum(pr, axis=1, keepdims=True))
                new_ms.append(m_new)
                pv.append(_dot(pr.astype(BF16), vp))
                alphas.append(alpha)
            alpha_pair = jnp.where(low_half, alphas[0], alphas[1])
            new_accs.append(alpha_pair * accs[p] + jnp.where(low_half, pv[0], pv[1]))
        return tuple(new_ms), tuple(new_ls), tuple(new_accs), tie_carry

    init = (tuple(jnp.full((Q_BLOCK, 1), NEG, F32) for _ in range(DSA_HEADS)),
            tuple(jnp.zeros((Q_BLOCK, 1), F32) for _ in range(DSA_HEADS)),
            tuple(jnp.zeros((Q_BLOCK, LANES), F32) for _ in range(n_pairs)),
            jnp.zeros((Q_BLOCK, 1), F32))
    _, ls, accs, _ = lax.fori_loop(0, nch, attn_body, init)
    for p in range(n_pairs):
        l_pair = jnp.where(low_half, ls[2 * p], ls[2 * p + 1])
        o_ref[0, :, p * LANES:(p + 1) * LANES] = (accs[p] / l_pair).astype(o_ref.dtype)


def _dsa_attention(h, bias_tiles):
    b, s, _ = h.shape
    width = DSA_HEADS * DSA_HEAD_DIM
    nblk = s // Q_BLOCK
    topk = min(DSA_TOPK_MAX, s // 4)
    kw_block = (CONV_CH * 2 + 4 * width) // LANES
    return pl.pallas_call(
        functools.partial(_dsa_kernel, topk=topk),
        grid=(b, nblk),
        in_specs=[pl.BlockSpec((1, Q_BLOCK, width), lambda bi, qi: (bi, qi, 2)),
                  pl.BlockSpec((1, Q_BLOCK, width), lambda bi, qi: (bi, qi, 5)),
                  pl.BlockSpec((1, Q_BLOCK, LANES), lambda bi, qi: (bi, qi, kw_block)),
                  pl.BlockSpec((1, s, width), lambda bi, qi: (bi, 0, 3)),
                  pl.BlockSpec((1, s, width), lambda bi, qi: (bi, 0, 4)),
                  pl.BlockSpec((1, s, LANES), lambda bi, qi: (bi, 0, kw_block)),
                  pl.BlockSpec(memory_space=pl.ANY)],
        out_specs=pl.BlockSpec((1, Q_BLOCK, width), lambda bi, qi: (bi, qi, 0)),
        out_shape=jax.ShapeDtypeStruct((b, s, width), BF16),
        scratch_shapes=[pltpu.VMEM((nblk + 1, DSA_HEADS, Q_BLOCK, Q_BLOCK), F32),
                        pltpu.VMEM((s // _KC, Q_BLOCK, _KC), jnp.int32),
                        pltpu.VMEM((IDX_HEADS, Q_BLOCK, LANES), F32),
                        pltpu.VMEM((DSA_HEADS, Q_BLOCK, LANES), BF16),
                        pltpu.SemaphoreType.DMA(())],
        compiler_params=pltpu.CompilerParams(dimension_semantics=("arbitrary", "arbitrary")),
        name="dsa_attention",
    )(h, h, h, h, h, h, bias_tiles)


def _dil_kernel(*refs, seq):
    n_g = len(DIL_GROUPS)
    qkv = refs[:3 * n_g]
    bias_ref, o_ref, og_ref, lg_ref = refs[3 * n_g:]
    c = pl.program_id(2)
    ii = lax.broadcasted_iota(jnp.int32, (BAND_BLOCK, BAND_BLOCK), 0)
    jj = lax.broadcasted_iota(jnp.int32, (BAND_BLOCK, BAND_BLOCK), 1)
    valid_prev = jj >= ii
    valid_cur = ii >= jj
    scale = DIL_HEAD_DIM ** -0.5
    for g, (window, dil) in enumerate(DIL_GROUPS):
        assert window // dil == BAND_BLOCK
        q_ref, k_ref, v_ref = qkv[3 * g:3 * g + 3]
        sub_len = seq // dil
        blocks_per_chunk = DIL_CHUNK // (BAND_BLOCK * dil)
        bias_prev = bias_ref[g, 0, :, :BAND_BLOCK]
        bias_cur = bias_ref[g, 0, :, BAND_BLOCK:]
        for r in range(dil):
            for nl in range(blocks_per_chunk):
                row0 = pl.multiple_of(r * sub_len + c * (DIL_CHUNK // dil) + nl * BAND_BLOCK, BAND_BLOCK)
                prow = pl.multiple_of(jnp.maximum(row0 - BAND_BLOCK, 0), BAND_BLOCK)
                qb = q_ref[0, pl.ds(row0, BAND_BLOCK), :]
                s_cur = _dot_nt(qb, k_ref[0, pl.ds(row0, BAND_BLOCK), :]) * scale + bias_cur
                s_prev = _dot_nt(qb, k_ref[0, pl.ds(prow, BAND_BLOCK), :]) * scale + bias_prev
                s_cur = jnp.where(valid_cur, s_cur, NEG)
                has_prev = valid_prev if nl > 0 else (valid_prev & (c > 0))
                s_prev = jnp.where(has_prev, s_prev, NEG)
                m = jnp.maximum(jnp.max(s_cur, axis=1, keepdims=True), jnp.max(s_prev, axis=1, keepdims=True))
                p_cur = jnp.exp(s_cur - m)
                p_prev = jnp.exp(s_prev - m)
                ssum = jnp.sum(p_cur, axis=1, keepdims=True) + jnp.sum(p_prev, axis=1, keepdims=True)
                o = (_dot(p_cur.astype(BF16), v_ref[0, pl.ds(row0, BAND_BLOCK), :])
                     + _dot(p_prev.astype(BF16), v_ref[0, pl.ds(prow, BAND_BLOCK), :])) / ssum
                lse = m + jnp.log(ssum)
                dst = pl.ds(r + nl * BAND_BLOCK * dil, BAND_BLOCK, stride=dil) if dil > 1 else pl.ds(
                    nl * BAND_BLOCK, BAND_BLOCK)
                og_ref[g, dst, :] = o
                lg_ref[g, dst, :] = jnp.broadcast_to(lse, (BAND_BLOCK, LANES))
    rows = 256
    for rc in range(DIL_CHUNK // rows):
        sl = slice(rc * rows, (rc + 1) * rows)
        lses = [lg_ref[g, sl, :] for g in range(n_g)]
        mx = functools.reduce(jnp.maximum, lses)
        ws = [jnp.exp(l - mx) for l in lses]
        den = functools.reduce(lambda a, b_: a + b_, ws)
        num = functools.reduce(lambda a, b_: a + b_, [ws[g] * og_ref[g, sl, :] for g in range(n_g)])
        o_ref[0, sl, :] = (num / den).astype(o_ref.dtype)


def _dilated_attention(hs, bias_tiles):
    b, s, _ = hs[0].shape
    in_specs = []
    args = []
    for hg in hs:
        for part in range(3):
            in_specs.append(pl.BlockSpec((1, s, DIL_HEAD_DIM),
                                         lambda bi, hi, ci, part=part: (bi, 0, part * DIL_HEADS + hi)))
            args.append(hg)
    in_specs.append(pl.BlockSpec((len(DIL_GROUPS), 1, BAND_BLOCK, 2 * BAND_BLOCK),
                                 lambda bi, hi, ci: (0, hi, 0, 0)))
    return pl.pallas_call(
        functools.partial(_dil_kernel, seq=s),
        grid=(b, DIL_HEADS, s // DIL_CHUNK),
        in_specs=in_specs,
        out_specs=pl.BlockSpec((1, DIL_CHUNK, DIL_HEAD_DIM), lambda bi, hi, ci: (bi, ci, hi)),
        out_shape=jax.ShapeDtypeStruct((b, s, DIL_HEADS * DIL_HEAD_DIM), BF16),
        scratch_shapes=[pltpu.VMEM((len(DIL_GROUPS), DIL_CHUNK, LANES), F32),
                        pltpu.VMEM((len(DIL_GROUPS), DIL_CHUNK, LANES), F32)],
        compiler_params=pltpu.CompilerParams(dimension_semantics=("parallel", "parallel", "arbitrary")),
        name="dilated_attention",
    )(*args, bias_tiles)


def _outproj_kernel(*refs, n_in, with_router):
    ins = refs[:n_in]
    ws = refs[n_in:2 * n_in]
    x_ref, g_ref, b_ref = refs[2 * n_in:2 * n_in + 3]
    rest = refs[2 * n_in + 3:]
    mix = _dot(ins[0][...], ws[0][...])
    for a_ref, w_ref in zip(ins[1:], ws[1:]):
        mix = mix + _dot(a_ref[...], w_ref[...])
    xn = _layer_norm(DEEPNORM_ALPHA * x_ref[...] + mix, g_ref[...], b_ref[...])
    if not with_router:
        o_ref, ob_ref = rest
    else:
        router_ref, o_ref, ob_ref, gates_ref = rest
        logits = jnp.dot(xn, router_ref[...], preferred_element_type=F32, precision=lax.Precision.HIGHEST)
        lane = lax.broadcasted_iota(jnp.int32, logits.shape, 1).astype(F32)
        lg = jnp.where(lane < N_EXPERTS, logits, -jnp.inf)
        m1 = jnp.max(lg, axis=1, keepdims=True)
        i1 = jnp.min(jnp.where(lg == m1, lane, float(LANES)), axis=1, keepdims=True)
        lg2 = jnp.where(lane == i1, -jnp.inf, lg)
        m2 = jnp.max(lg2, axis=1, keepdims=True)
        i2 = jnp.min(jnp.where(lg2 == m2, lane, float(LANES)), axis=1, keepdims=True)
        e2 = jnp.exp(m2 - m1)
        den = 1.0 + e2
        gates_ref[...] = (jnp.where(lane == 0.0, i1, 0.0) + jnp.where(lane == 1.0, i2, 0.0)
                          + jnp.where(lane == 2.0, 1.0 / den, 0.0) + jnp.where(lane == 3.0, e2 / den, 0.0))
    o_ref[...] = xn
    ob_ref[...] = xn.astype(BF16)


def _outproj_ln(ins, ws, x, ln_g, ln_b, router=None, tm=512):
    t, d = x.shape
    n_in = len(ins)
    row = lambda width: pl.BlockSpec((tm, width), lambda i: (i, 0))
    full = lambda a: pl.BlockSpec(a.shape, lambda i: (0, 0))
    vec = lambda v: v.reshape(1, d).astype(F32)
    args = list(ins) + list(ws) + [x, vec(ln_g), vec(ln_b)]
    in_specs = [row(a.shape[1]) for a in ins] + [full(w) for w in ws] + [row(d), full(vec(ln_g)), full(vec(ln_b))]
    out_shape = [jax.ShapeDtypeStruct((t, d), F32), jax.ShapeDtypeStruct((t, d), BF16)]
    out_specs = [row(d), row(d)]
    if router is not None:
        router_pad = jnp.zeros((d, LANES), F32).at[:, :N_EXPERTS].set(router)
        args.append(router_pad)
        in_specs.append(full(router_pad))
        out_shape.append(jax.ShapeDtypeStruct((t, LANES), F32))
        out_specs.append(row(LANES))
    return pl.pallas_call(
        functools.partial(_outproj_kernel, n_in=n_in, with_router=router is not None),
        grid=(t // tm,),
        in_specs=in_specs,
        out_specs=out_specs,
        out_shape=out_shape,
        compiler_params=pltpu.CompilerParams(dimension_semantics=("parallel",)),
        name="outproj_ln",
    )(*args)


def _ffn_kernel(xb_ref, x_ref, wg_ref, wu_ref, wd_ref, g_ref, b_ref, o_ref, ob_ref, acc_ref):
    f = pl.program_id(1)

    @pl.when(f == 0)
    def _():
        acc_ref[...] = jnp.zeros_like(acc_ref)

    xb = xb_ref[...]
    hg = _dot(xb, wg_ref[...])
    hu = _dot(xb, wu_ref[...])
    act = (hg * _sigmoid(hg) * hu).astype(BF16)
    acc_ref[...] += _dot(act, wd_ref[...])

    @pl.when(f == pl.num_programs(1) - 1)
    def _():
        xn = _layer_norm(DEEPNORM_ALPHA * x_ref[...] + acc_ref[...], g_ref[...], b_ref[...])
        o_ref[...] = xn
        ob_ref[...] = xn.astype(BF16)


def _ffn_ln(xb, x, wg, wu, wd, ln_g, ln_b, tm=512, tf=1408):
    t, d = x.shape
    ff = wg.shape[1]
    vec = lambda v: v.reshape(1, d).astype(F32)
    return pl.pallas_call(
        _ffn_kernel,
        grid=(t // tm, ff // tf),
        in_specs=[pl.BlockSpec((tm, d), lambda i, f: (i, 0)),
                  pl.BlockSpec((tm, d), lambda i, f: (i, 0)),
                  pl.BlockSpec((d, tf), lambda i, f: (0, f)),
                  pl.BlockSpec((d, tf), lambda i, f: (0, f)),
                  pl.BlockSpec((tf, d), lambda i, f: (f, 0)),
                  pl.BlockSpec((1, d), lambda i, f: (0, 0)),
                  pl.BlockSpec((1, d), lambda i, f: (0, 0))],
        out_specs=[pl.BlockSpec((tm, d), lambda i, f: (i, 0)),
                   pl.BlockSpec((tm, d), lambda i, f: (i, 0))],
        out_shape=[jax.ShapeDtypeStruct((t, d), F32), jax.ShapeDtypeStruct((t, d), BF16)],
        scratch_shapes=[pltpu.VMEM((tm, d), F32)],
        compiler_params=pltpu.CompilerParams(dimension_semantics=("parallel", "arbitrary")),
        name="ffn_ln",
    )(xb, x, wg, wu, wd, vec(ln_g), vec(ln_b))


_MOE_TM = 512


def _route(route, n_tiles):
    t = route.shape[0]
    expert = route[:, :2].astype(jnp.int32).reshape(1, 2 * t)
    onehot = (expert == jnp.arange(N_EXPERTS, dtype=jnp.int32)[:, None]).astype(jnp.int32)
    csum = jnp.cumsum(onehot, axis=1)
    counts = csum[:, -1]
    padded = (counts + _MOE_TM - 1) // _MOE_TM * _MOE_TM
    ends = jnp.cumsum(padded)
    starts = ends - padded
    pos = jnp.sum(onehot * (csum - 1 + starts[:, None]), axis=0)
    tile_start = jnp.arange(n_tiles, dtype=jnp.int32) * _MOE_TM
    tile_expert = jnp.minimum(jnp.sum((tile_start[:, None] >= ends[None, :]).astype(jnp.int32), axis=1),
                              N_EXPERTS - 1)
    return pos.astype(jnp.int32), tile_expert.astype(jnp.int32)


def _row_scatter_kernel(pos_ref, x_ref, init_ref, o_ref, sem, *, tm):
    del init_ref

    def body(j, carry):
        for choice in range(2):
            p = pos_ref[0, 0, 2 * j + choice]
            pltpu.make_async_copy(x_ref.at[pl.ds(j, 1)], o_ref.at[pl.ds(p, 1)], sem).start()
        return carry

    lax.fori_loop(0, tm, body, 0, unroll=8)
    for _ in range(2):
        pltpu.make_async_copy(x_ref, o_ref.at[pl.ds(0, tm)], sem).wait()


def _row_scatter(x, pos, n_rows, tm=512):
    t, d = x.shape
    return pl.pallas_call(
        functools.partial(_row_scatter_kernel, tm=tm),
        grid=(t // tm,),
        in_specs=[pl.BlockSpec((1, 1, 2 * tm), lambda i: (i, 0, 0), memory_space=pltpu.SMEM),
                  pl.BlockSpec((tm, d), lambda i: (i, 0)),
                  pl.BlockSpec(memory_space=pl.ANY)],
        out_specs=pl.BlockSpec(memory_space=pl.ANY),
        out_shape=jax.ShapeDtypeStruct((n_rows, d), x.dtype),
        scratch_shapes=[pltpu.SemaphoreType.DMA(())],
        input_output_aliases={2: 0},
        compiler_params=pltpu.CompilerParams(dimension_semantics=("arbitrary",)),
        name="moe_row_scatter",
    )(pos.reshape(t // tm, 1, 2 * tm), x, jnp.zeros((n_rows, d), x.dtype))


def _expert_ffn_kernel(te_ref, x_ref, wg_ref, wu_ref, wd_ref, o_ref, acc_ref):
    del te_ref
    f = pl.program_id(1)

    @pl.when(f == 0)
    def _():
        acc_ref[...] = jnp.zeros_like(acc_ref)

    xb = x_ref[...].astype(BF16)
    hg = _dot(xb, wg_ref[0])
    hu = _dot(xb, wu_ref[0])
    act = (hg * _sigmoid(hg) * hu).astype(BF16)
    acc_ref[...] += _dot(act, wd_ref[0])

    @pl.when(f == pl.num_programs(1) - 1)
    def _():
        o_ref[...] = acc_ref[...]


def _expert_ffn(xs, tile_expert, wg, wu, wd, tf=896):
    n_rows, d = xs.shape
    ff = wg.shape[2]
    grid_spec = pltpu.PrefetchScalarGridSpec(
        num_scalar_prefetch=1,
        grid=(n_rows // _MOE_TM, ff // tf),
        in_specs=[pl.BlockSpec((_MOE_TM, d), lambda i, f, te: (i, 0)),
                  pl.BlockSpec((1, d, tf), lambda i, f, te: (te[i], 0, f)),
                  pl.BlockSpec((1, d, tf), lambda i, f, te: (te[i], 0, f)),
                  pl.BlockSpec((1, tf, d), lambda i, f, te: (te[i], f, 0))],
        out_specs=pl.BlockSpec((_MOE_TM, d), lambda i, f, te: (i, 0)),
        scratch_shapes=[pltpu.VMEM((_MOE_TM, d), F32)])
    return pl.pallas_call(
        _expert_ffn_kernel,
        grid_spec=grid_spec,
        out_shape=jax.ShapeDtypeStruct((n_rows, d), F32),
        compiler_params=pltpu.CompilerParams(dimension_semantics=("parallel", "arbitrary")),
        name="moe_expert_ffn",
    )(tile_expert, xs, wg, wu, wd)


def _combine_kernel(pos_ref, route_ref, x_ref, g_ref, b_ref, y_ref, o_ref, ybuf_ref, sem, *, tm):
    def body(j, carry):
        for choice in range(2):
            p = pos_ref[0, 0, 2 * j + choice]
            pltpu.make_async_copy(y_ref.at[pl.ds(p, 1)], ybuf_ref.at[choice, pl.ds(j, 1)], sem).start()
        return carry

    lax.fori_loop(0, tm, body, 0, unroll=8)
    for choice in range(2):
        pltpu.make_async_copy(y_ref.at[pl.ds(0, tm)], ybuf_ref.at[choice], sem).wait()
    route = route_ref[...]
    y = route[:, 2:3] * ybuf_ref[0] + route[:, 3:4] * ybuf_ref[1]
    o_ref[...] = _layer_norm(DEEPNORM_ALPHA * x_ref[...] + y, g_ref[...], b_ref[...])


def _combine_ln(ys, pos, route, x, ln_g, ln_b, tm=512):
    t, d = x.shape
    vec = lambda v: v.reshape(1, d).astype(F32)
    return pl.pallas_call(
        functools.partial(_combine_kernel, tm=tm),
        grid=(t // tm,),
        in_specs=[pl.BlockSpec((1, 1, 2 * tm), lambda i: (i, 0, 0), memory_space=pltpu.SMEM),
                  pl.BlockSpec((tm, LANES), lambda i: (i, 0)),
                  pl.BlockSpec((tm, d), lambda i: (i, 0)),
                  pl.BlockSpec((1, d), lambda i: (0, 0)),
                  pl.BlockSpec((1, d), lambda i: (0, 0)),
                  pl.BlockSpec(memory_space=pl.ANY)],
        out_specs=pl.BlockSpec((tm, d), lambda i: (i, 0)),
        out_shape=jax.ShapeDtypeStruct((t, d), F32),
        scratch_shapes=[pltpu.VMEM((2, tm, d), F32), pltpu.SemaphoreType.DMA(())],
        compiler_params=pltpu.CompilerParams(dimension_semantics=("arbitrary",)),
        name="moe_combine_ln",
    )(pos.reshape(t // tm, 1, 2 * tm), route, x, vec(ln_g), vec(ln_b), ys)


def _moe_ln(x, route, wg, wu, wd, ln_g, ln_b):
    t, _ = x.shape
    n_tiles = 2 * t // _MOE_TM + N_EXPERTS
    pos, tile_expert = _route(route, n_tiles)
    xs = _row_scatter(x, pos, n_tiles * _MOE_TM)
    ys = _expert_ffn(xs, tile_expert, wg, wu, wd)
    return _combine_ln(ys, pos, route, x, ln_g, ln_b)


def _even_layer(x, xb, batch, seq, dsa_bias, w_in, conv_w, conv_b, conv_ln_g, conv_ln_b, w_out,
                ln1_g, ln1_b, ffn_wg, ffn_wu, ffn_wd, ln2_g, ln2_b):
    w_in_p = jnp.zeros((D_MODEL, EVEN_IN_PAD), BF16).at[:, :w_in.shape[1]].set(w_in.astype(BF16))
    h = _matmul(xb, w_in_p, 1024, 640, BF16).reshape(batch, seq, EVEN_IN_PAD)
    a_out = _conformer_conv(h, conv_w, conv_b, conv_ln_g, conv_ln_b)
    att = _dsa_attention(h, dsa_bias)
    w_out_b = w_out.astype(BF16)
    x1, x1b = _outproj_ln([a_out.reshape(-1, CONV_CH), att.reshape(-1, DSA_HEADS * DSA_HEAD_DIM)],
                          [w_out_b[:CONV_CH], w_out_b[CONV_CH:]], x, ln1_g, ln1_b)
    return _ffn_ln(x1b, x1, ffn_wg.astype(BF16), ffn_wu.astype(BF16), ffn_wd.astype(BF16), ln2_g, ln2_b)


def _odd_layer(x, xb, batch, seq, dil_bias, w_in, w_out, ln1_g, ln1_b, router, moe_wg, moe_wu, moe_wd,
               ln2_g, ln2_b):
    gw = 3 * DIL_HEADS * DIL_HEAD_DIM
    w_in_b = w_in.astype(BF16)
    hs = [_matmul_residue_major(xb, w_in_b[:, g * gw:(g + 1) * gw], batch, seq, dil, 1024, 768)
          for g, (_, dil) in enumerate(DIL_GROUPS)]
    o = _dilated_attention(hs, dil_bias).reshape(-1, DIL_HEADS * DIL_HEAD_DIM)
    x1, _, route = _outproj_ln([o], [w_out.astype(BF16)], x, ln1_g, ln1_b, router=router)
    out = _moe_ln(x1, route, moe_wg.astype(BF16), moe_wu.astype(BF16), moe_wd.astype(BF16), ln2_g, ln2_b)
    return out, out.astype(BF16)


def kernel(x, rel_bias, even_w_in, even_conv_w, even_conv_b, even_conv_ln_g, even_conv_ln_b, even_w_out, even_ln1_g, even_ln1_b, even_ffn_wg, even_ffn_wu, even_ffn_wd, even_ln2_g, even_ln2_b, odd_w_in, odd_w_out, odd_ln1_g, odd_ln1_b, odd_router, odd_moe_wg, odd_moe_wu, odd_moe_wd, odd_ln2_g, odd_ln2_b):
    batch, seq, d = x.shape
    assert d == D_MODEL and seq % DIL_CHUNK == 0
    depth = even_w_in.shape[0] + odd_w_in.shape[0]
    assert depth == DEPTH
    dsa_bias = _dsa_bias_tiles(rel_bias, seq)
    dil_bias = _dil_bias_tiles(rel_bias)
    xf = x.reshape(batch * seq, d)
    xb = xf.astype(BF16)
    for layer in range(depth):
        i = layer // 2
        if layer % 2 == 0:
            xf, xb = _even_layer(xf, xb, batch, seq, dsa_bias, even_w_in[i], even_conv_w[i], even_conv_b[i],
                                 even_conv_ln_g[i], even_conv_ln_b[i], even_w_out[i], even_ln1_g[i],
                                 even_ln1_b[i], even_ffn_wg[i], even_ffn_wu[i], even_ffn_wd[i],
                                 even_ln2_g[i], even_ln2_b[i])
        else:
            xf, xb = _odd_layer(xf, xb, batch, seq, dil_bias, odd_w_in[i], odd_w_out[i], odd_ln1_g[i],
                                odd_ln1_b[i], odd_router[i], odd_moe_wg[i], odd_moe_wu[i], odd_moe_wd[i],
                                odd_ln2_g[i], odd_ln2_b[i])
    return xf.reshape(batch, seq, d)
```

```python
import functools
import math

import jax
import jax.numpy as jnp
from jax import lax
from jax.experimental import pallas as pl
from jax.experimental.pallas import tpu as pltpu

BF16 = jnp.bfloat16
F32 = jnp.float32

D_MODEL = 1024
CONV_CH = 512
CONV_WIDTH = 31
DSA_HEADS = 8
DSA_HEAD_DIM = 64
IDX_HEADS = 8
IDX_DIM = 64
DSA_TOPK_MAX = 256
Q_BLOCK = 128
DIL_GROUPS = ((128, 1), (512, 4), (2048, 16))
DIL_HEADS = 8
DIL_HEAD_DIM = 128
BAND_BLOCK = 128
NUM_BUCKETS = 32
MAX_DISTANCE = 2048
N_EXPERTS = 8
LN_EPS = 1e-5
NEG = -1e30
DEPTH = 2
DEEPNORM_ALPHA = (2 * DEPTH) ** 0.25

LANES = 128
EVEN_IN_PAD = 3200
INT_MIN = -(2 ** 31)
DIL_CHUNK = BAND_BLOCK * max(d for _, d in DIL_GROUPS)

_PARALLEL2 = pltpu.CompilerParams(dimension_semantics=("parallel", "parallel"))


def _layer_norm(y, g, b):
    mu = jnp.mean(y, axis=-1, keepdims=True)
    yc = y - mu
    var = jnp.mean(yc * yc, axis=-1, keepdims=True)
    return yc * lax.rsqrt(var + LN_EPS) * g + b


def _sigmoid(x):
    return 1.0 / (1.0 + jnp.exp(-x))


def _dot_nt(a, b):
    return lax.dot_general(a, b, (((1,), (1,)), ((), ())), preferred_element_type=F32)


def _dot(a, b):
    return jnp.dot(a, b, preferred_element_type=F32)


def _mm_kernel(x_ref, w_ref, o_ref):
    o_ref[...] = _dot(x_ref[...], w_ref[...]).astype(o_ref.dtype)


def _matmul(x, w, tm, tn, out_dtype):
    t, k = x.shape
    n = w.shape[1]
    return pl.pallas_call(
        _mm_kernel,
        grid=(t // tm, n // tn),
        in_specs=[pl.BlockSpec((tm, k), lambda i, j: (i, 0)),
                  pl.BlockSpec((k, tn), lambda i, j: (0, j))],
        out_specs=pl.BlockSpec((tm, tn), lambda i, j: (i, j)),
        out_shape=jax.ShapeDtypeStruct((t, n), out_dtype),
        compiler_params=_PARALLEL2,
        name="matmul",
    )(x, w)


def _mm_perm_kernel(x_ref, w_ref, o_ref, acc_ref, *, dil, tm, tn):
    acc = _dot(x_ref[...], w_ref[...])
    rows = tm // dil
    for c in range(tn // LANES):
        acc_ref[c] = acc[:, c * LANES:(c + 1) * LANES]
    for r in range(dil):
        for c in range(tn // LANES):
            o_ref[0, r, :, c * LANES:(c + 1) * LANES] = (
                acc_ref[c, pl.ds(r, rows, stride=dil), :].astype(o_ref.dtype))


def _matmul_residue_major(x, w, batch, seq, dil, tm, tn):
    t, k = x.shape
    n = w.shape[1]
    if dil == 1:
        return _matmul(x, w, tm, tn, BF16).reshape(batch, seq, n)
    tiles_per_seq = seq // tm
    out = pl.pallas_call(
        functools.partial(_mm_perm_kernel, dil=dil, tm=tm, tn=tn),
        grid=(t // tm, n // tn),
        in_specs=[pl.BlockSpec((tm, k), lambda i, j: (i, 0)),
                  pl.BlockSpec((k, tn), lambda i, j: (0, j))],
        out_specs=pl.BlockSpec((1, dil, tm // dil, tn),
                               lambda i, j: (i // tiles_per_seq, 0, i % tiles_per_seq, j)),
        out_shape=jax.ShapeDtypeStruct((batch, dil, seq // dil, n), BF16),
        scratch_shapes=[pltpu.VMEM((tn // LANES, tm, LANES), F32)],
        compiler_params=_PARALLEL2,
        name="matmul_residue_major",
    )(x, w)
    return out.reshape(batch, seq, n)


_CONV_HALO = 32
_CONV_ROWS = 64


def _conv_kernel(val_ref, gate_ref, w_ref, cb_ref, g_ref, b_ref, o_ref, ext_ref, *, ts):
    s = pl.program_id(1)

    @pl.when(s == 0)
    def _():
        ext_ref[0:_CONV_HALO, :] = jnp.zeros((_CONV_HALO, CONV_CH), F32)

    val = val_ref[0].astype(F32)
    gate = gate_ref[0].astype(F32)
    ext_ref[_CONV_HALO:_CONV_HALO + ts, :] = val * _sigmoid(gate)
    first_tap = _CONV_HALO - (CONV_WIDTH - 1)
    for rc in range(ts // _CONV_ROWS):
        r0 = rc * _CONV_ROWS
        acc = jnp.zeros((_CONV_ROWS, CONV_CH), F32)
        for j in range(CONV_WIDTH):
            lo = r0 + first_tap + j
            acc = acc + ext_ref[lo:lo + _CONV_ROWS, :] * w_ref[j:j + 1, :]
        y = _layer_norm(acc + cb_ref[...], g_ref[...], b_ref[...])
        o_ref[0, r0:r0 + _CONV_ROWS, :] = (y * _sigmoid(y)).astype(o_ref.dtype)
    ext_ref[0:_CONV_HALO, :] = ext_ref[ts:ts + _CONV_HALO, :]


def _conformer_conv(h, conv_w, conv_b, ln_g, ln_b, ts=512):
    b, s, _ = h.shape
    w_pad = jnp.zeros((32, CONV_CH), F32).at[:CONV_WIDTH].set(conv_w)
    vec = lambda v: v.reshape(1, CONV_CH).astype(F32)
    full = lambda shape: pl.BlockSpec(shape, lambda bi, si: (0, 0))
    return pl.pallas_call(
        functools.partial(_conv_kernel, ts=ts),
        grid=(b, s // ts),
        in_specs=[pl.BlockSpec((1, ts, CONV_CH), lambda bi, si: (bi, si, 0)),
                  pl.BlockSpec((1, ts, CONV_CH), lambda bi, si: (bi, si, 1)),
                  full((32, CONV_CH)), full((1, CONV_CH)), full((1, CONV_CH)), full((1, CONV_CH))],
        out_specs=pl.BlockSpec((1, ts, CONV_CH), lambda bi, si: (bi, si, 0)),
        out_shape=jax.ShapeDtypeStruct((b, s, CONV_CH), BF16),
        scratch_shapes=[pltpu.VMEM((ts + _CONV_HALO, CONV_CH), F32)],
        compiler_params=pltpu.CompilerParams(dimension_semantics=("parallel", "arbitrary")),
        name="conformer_conv",
    )(h, h, w_pad, vec(conv_b), vec(ln_g), vec(ln_b))


def _rel_bucket(dist):
    max_exact = NUM_BUCKETS // 2
    n = dist.astype(jnp.int32)
    nf = jnp.maximum(n, 1).astype(F32)
    large = max_exact + (jnp.log(nf / max_exact) / math.log(MAX_DISTANCE / max_exact)
                         * (NUM_BUCKETS - max_exact)).astype(jnp.int32)
    large = jnp.minimum(large, NUM_BUCKETS - 1)
    return jnp.where(n < max_exact, n, large)


def _bias_by_distance(rel_bias, n):
    return rel_bias[_rel_bucket(jnp.arange(n, dtype=jnp.int32))].astype(F32).T


def _reversed_clamped(f, top, length):
    tail = jnp.broadcast_to(f[:, :1], (f.shape[0], length - top - 1))
    return jnp.concatenate([f[:, ::-1], tail], axis=1)


def _toeplitz_rows(v, rows, m0, width):
    h, length = v.shape
    assert m0 >= rows - 1 and m0 + width <= length - 1
    skew = jnp.tile(v, (1, rows))[:, :rows * (length - 1)].reshape(h, rows, length - 1)
    return skew[:, :, m0:m0 + width]


def _dsa_bias_tiles(rel_bias, seq):
    nblk = seq // Q_BLOCK
    f = _bias_by_distance(rel_bias, seq + 1)
    v = _reversed_clamped(f, seq, Q_BLOCK * (nblk + 3))
    t = _toeplitz_rows(v, Q_BLOCK, Q_BLOCK, Q_BLOCK * (nblk + 1))
    return t.reshape(f.shape[0], Q_BLOCK, nblk + 1, Q_BLOCK).transpose(2, 0, 1, 3)


def _dil_bias_tiles(rel_bias):
    max_dil = max(d for _, d in DIL_GROUPS)
    span = 2 * BAND_BLOCK
    f = _bias_by_distance(rel_bias, span * max_dil + 1)
    tiles = []
    for _, dil in DIL_GROUPS:
        fd = f[:, 0:span * dil + 1:dil]
        v = _reversed_clamped(fd, span, 2 * span)
        tiles.append(_toeplitz_rows(v, BAND_BLOCK, BAND_BLOCK, span))
    return jnp.stack(tiles)


_KC = 2 * Q_BLOCK


def _sortable_key(score):
    score = jnp.where(score == 0.0, 0.0, score)
    bits = lax.bitcast_convert_type(score, jnp.int32)
    return jnp.where(bits < 0, bits ^ jnp.int32(0x7FFFFFFF), bits)


def _dsa_kernel(q_ref, qi_ref, wq_ref, k_ref, v_ref, ki_ref, bias_hbm, o_ref,
                bias_ref, keys_ref, wb_ref, qm_ref, sem, *, topk):
    b = pl.program_id(0)
    i = pl.program_id(1)

    @pl.when((b == 0) & (i == 0))
    def _():
        cp = pltpu.make_async_copy(bias_hbm, bias_ref, sem)
        cp.start()
        cp.wait()

    nch = (i + 2) // 2
    row = lax.broadcasted_iota(jnp.int32, (Q_BLOCK, _KC), 0)
    col = lax.broadcasted_iota(jnp.int32, (Q_BLOCK, _KC), 1)
    q_pos = i * Q_BLOCK + row

    w_scaled = wq_ref[0][:, IDX_DIM:IDX_DIM + IDX_HEADS].astype(F32) * (IDX_HEADS ** -0.5 * IDX_DIM ** -0.5)
    for h in range(IDX_HEADS):
        wb_ref[h] = jnp.broadcast_to(w_scaled[:, h:h + 1], (Q_BLOCK, LANES))
    lane = lax.broadcasted_iota(jnp.int32, (Q_BLOCK, LANES), 1)
    low_half = lane < DSA_HEAD_DIM
    qi = qi_ref[0].astype(F32)
    qi_heads = []
    for p in range(IDX_HEADS // 2):
        pair = qi[:, p * LANES:(p + 1) * LANES]
        qi_heads.append(jnp.where(low_half, pair, 0.0))
        qi_heads.append(jnp.where(low_half, pltpu.roll(pair, IDX_DIM, 1), 0.0))
    qi_stack = jnp.concatenate(qi_heads, axis=0).astype(BF16)
    q_scaled = (q_ref[0].astype(F32) * (DSA_HEAD_DIM ** -0.5)).astype(BF16)
    for p in range(DSA_HEADS // 2):
        qp = q_scaled[:, p * LANES:(p + 1) * LANES]
        qm_ref[2 * p] = jnp.where(low_half, qp, jnp.zeros_like(qp))
        qm_ref[2 * p + 1] = jnp.where(low_half, jnp.zeros_like(qp), qp)

    def score_body(c, carry):
        k0 = pl.multiple_of(c * _KC, _KC)
        ki = ki_ref[0, pl.ds(k0, _KC), :]
        dots = _dot_nt(qi_stack, ki)
        score = jnp.zeros((Q_BLOCK, _KC), F32)
        for h in range(IDX_HEADS):
            wb = wb_ref[h]
            wb2 = jnp.concatenate([wb, wb], axis=1)
            score = score + wb2 * jnp.maximum(dots[h * Q_BLOCK:(h + 1) * Q_BLOCK], 0.0)
        key = jnp.where(k0 + col <= q_pos, _sortable_key(score), jnp.int32(INT_MIN))
        keys_ref[c] = key
        return carry

    lax.fori_loop(0, nch, score_body, 0)

    def count_ge(cand):
        cb = jnp.broadcast_to(cand, (Q_BLOCK, LANES))

        def body(c, acc):
            k0 = pl.multiple_of(c * _KC, _KC)
            kk = keys_ref[c]
            acc = acc + jnp.where(kk[:, :LANES] >= cb, 1.0, 0.0)
            return acc + jnp.where(kk[:, LANES:] >= cb, 1.0, 0.0)

        acc = lax.fori_loop(0, nch, body, jnp.zeros((Q_BLOCK, LANES), F32))
        return jnp.sum(acc, axis=1, keepdims=True)

    kf = float(topk)
    thr0 = jnp.full((Q_BLOCK, 1), INT_MIN, jnp.int32)
    zero = jnp.zeros((Q_BLOCK, 1), jnp.int32)
    thr0 = jnp.where(count_ge(zero) >= kf, zero, thr0)

    def bit_body(t, thr):
        cand = thr | jnp.left_shift(jnp.int32(1), 30 - t)
        return jnp.where(count_ge(cand) >= kf, cand, thr)

    thr = lax.fori_loop(0, 31, bit_body, thr0)
    n_gt = count_ge(jnp.where(thr == jnp.int32(2 ** 31 - 1), thr, thr + 1))
    need = kf - n_gt
    thr_b = jnp.broadcast_to(thr, (Q_BLOCK, _KC))
    need_b = jnp.broadcast_to(need, (Q_BLOCK, _KC))
    tri = (lax.broadcasted_iota(jnp.int32, (_KC, _KC), 0)
           <= lax.broadcasted_iota(jnp.int32, (_KC, _KC), 1)).astype(BF16)

    n_pairs = DSA_HEADS // 2

    def attn_body(c, carry):
        ms, ls, accs, tie_carry = carry
        k0 = pl.multiple_of(c * _KC, _KC)
        kk = keys_ref[c]
        causal = kk > jnp.int32(INT_MIN)
        eq = (kk == thr_b) & causal
        eq_f = jnp.where(eq, 1.0, 0.0)
        rank = _dot(eq_f.astype(BF16), tri) + tie_carry
        sel = ((kk > thr_b) & causal) | (eq & (rank <= need_b))
        tie_carry = tie_carry + jnp.sum(eq_f, axis=1, keepdims=True)
        d = i - 2 * c
        far = bias_ref.shape[0] - 2
        kc = k_ref[0, pl.ds(k0, _KC), :]
        vc = v_ref[0, pl.ds(k0, _KC), :]
        new_ms, new_ls, new_accs = [], [], []
        for p in range(n_pairs):
            kp = kc[:, p * LANES:(p + 1) * LANES]
            vp = vc[:, p * LANES:(p + 1) * LANES]
            pv, alphas = [], []
            for e in range(2):
                h = 2 * p + e
                s = _dot_nt(qm_ref[h], kp)
                bias = jnp.concatenate([bias_ref[far - d, h], bias_ref[far - d + 1, h]], axis=1)
                s = jnp.where(sel, s + bias, NEG)
                m_new = jnp.maximum(ms[h], jnp.max(s, axis=1, keepdims=True))
                alpha = jnp.exp(ms[h] - m_new)
                pr = jnp.exp(s - m_new)
                new_ls.append(alpha * ls[h] + jnp.sum(pr, axis=1, keepdims=True))
                new_ms.append(m_new)
                pv.append(_dot(pr.astype(BF16), vp))
                alphas.append(alpha)
            alpha_pair = jnp.where(low_half, alphas[0], alphas[1])
            new_accs.append(alpha_pair * accs[p] + jnp.where(low_half, pv[0], pv[1]))
        return tuple(new_ms), tuple(new_ls), tuple(new_accs), tie_carry

    init = (tuple(jnp.full((Q_BLOCK, 1), NEG, F32) for _ in range(DSA_HEADS)),
            tuple(jnp.zeros((Q_BLOCK, 1), F32) for _ in range(DSA_HEADS)),
            tuple(jnp.zeros((Q_BLOCK, LANES), F32) for _ in range(n_pairs)),
            jnp.zeros((Q_BLOCK, 1), F32))
    _, ls, accs, _ = lax.fori_loop(0, nch, attn_body, init)
    for p in range(n_pairs):
        l_pair = jnp.where(low_half, ls[2 * p], ls[2 * p + 1])
        o_ref[0, :, p * LANES:(p + 1) * LANES] = (accs[p] / l_pair).astype(o_ref.dtype)


def _dsa_attention(h, bias_tiles):
    b, s, _ = h.shape
    width = DSA_HEADS * DSA_HEAD_DIM
    nblk = s // Q_BLOCK
    topk = min(DSA_TOPK_MAX, s // 4)
    kw_block = (CONV_CH * 2 + 4 * width) // LANES
    return pl.pallas_call(
        functools.partial(_dsa_kernel, topk=topk),
        grid=(b, nblk),
        in_specs=[pl.BlockSpec((1, Q_BLOCK, width), lambda bi, qi: (bi, qi, 2)),
                  pl.BlockSpec((1, Q_BLOCK, width), lambda bi, qi: (bi, qi, 5)),
                  pl.BlockSpec((1, Q_BLOCK, LANES), lambda bi, qi: (bi, qi, kw_block)),
                  pl.BlockSpec((1, s, width), lambda bi, qi: (bi, 0, 3)),
                  pl.BlockSpec((1, s, width), lambda bi, qi: (bi, 0, 4)),
                  pl.BlockSpec((1, s, LANES), lambda bi, qi: (bi, 0, kw_block)),
                  pl.BlockSpec(memory_space=pl.ANY)],
        out_specs=pl.BlockSpec((1, Q_BLOCK, width), lambda bi, qi: (bi, qi, 0)),
        out_shape=jax.ShapeDtypeStruct((b, s, width), BF16),
        scratch_shapes=[pltpu.VMEM((nblk + 1, DSA_HEADS, Q_BLOCK, Q_BLOCK), F32),
                        pltpu.VMEM((s // _KC, Q_BLOCK, _KC), jnp.int32),
                        pltpu.VMEM((IDX_HEADS, Q_BLOCK, LANES), F32),
                        pltpu.VMEM((DSA_HEADS, Q_BLOCK, LANES), BF16),
                        pltpu.SemaphoreType.DMA(())],
        compiler_params=pltpu.CompilerParams(dimension_semantics=("arbitrary", "arbitrary")),
        name="dsa_attention",
    )(h, h, h, h, h, h, bias_tiles)


_DIL_WAVE = 8


def _dil_kernel(*refs, seq):
    n_g = len(DIL_GROUPS)
    qkv = refs[:3 * n_g]
    bias_ref, o_ref, og_ref, lg_ref = refs[3 * n_g:]
    c = pl.program_id(2)
    ii = lax.broadcasted_iota(jnp.int32, (BAND_BLOCK, BAND_BLOCK), 0)
    jj = lax.broadcasted_iota(jnp.int32, (BAND_BLOCK, BAND_BLOCK), 1)
    valid_prev = jj >= ii
    valid_cur = ii >= jj
    scale = DIL_HEAD_DIM ** -0.5
    for g, (window, dil) in enumerate(DIL_GROUPS):
        assert window // dil == BAND_BLOCK
        q_ref, k_ref, v_ref = qkv[3 * g:3 * g + 3]
        sub_len = seq // dil
        blocks_per_chunk = DIL_CHUNK // (BAND_BLOCK * dil)
        bias_prev = bias_ref[g, 0, :, :BAND_BLOCK]
        bias_cur = bias_ref[g, 0, :, BAND_BLOCK:]
        tiles = [(r, nl) for r in range(dil) for nl in range(blocks_per_chunk)]
        for w0 in range(0, len(tiles), _DIL_WAVE):
            wave = tiles[w0:w0 + _DIL_WAVE]
            cur, prev, mask_prev = [], [], []
            for r, nl in wave:
                row0 = pl.multiple_of(r * sub_len + c * (DIL_CHUNK // dil) + nl * BAND_BLOCK, BAND_BLOCK)
                cur.append(pl.ds(row0, BAND_BLOCK))
                prev.append(pl.ds(pl.multiple_of(jnp.maximum(row0 - BAND_BLOCK, 0), BAND_BLOCK), BAND_BLOCK))
                mask_prev.append(valid_prev if nl > 0 else (valid_prev & (c > 0)))
            gather = lambda ref, slices: jnp.stack([ref[0, sl, :] for sl in slices])
            qb = gather(q_ref, cur)
            s_cur = jnp.einsum("wqd,wkd->wqk", qb, gather(k_ref, cur), preferred_element_type=F32)
            s_prev = jnp.einsum("wqd,wkd->wqk", qb, gather(k_ref, prev), preferred_element_type=F32)
            s_cur = jnp.where(valid_cur, s_cur * scale + bias_cur, NEG)
            s_prev = jnp.where(jnp.stack(mask_prev), s_prev * scale + bias_prev, NEG)
            m = jnp.max(jnp.maximum(s_cur, s_prev), axis=2, keepdims=True)
            p_cur = jnp.exp(s_cur - m)
            p_prev = jnp.exp(s_prev - m)
            ssum = jnp.sum(p_cur + p_prev, axis=2, keepdims=True)
            o = (jnp.einsum("wqk,wkd->wqd", p_cur.astype(BF16), gather(v_ref, cur), preferred_element_type=F32)
                 + jnp.einsum("wqk,wkd->wqd", p_prev.astype(BF16), gather(v_ref, prev),
                              preferred_element_type=F32)) / ssum
            lse = m + jnp.log(ssum)
            for w, (r, nl) in enumerate(wave):
                dst = pl.ds(r + nl * BAND_BLOCK * dil, BAND_BLOCK, stride=dil) if dil > 1 else pl.ds(
                    nl * BAND_BLOCK, BAND_BLOCK)
                og_ref[g, dst, :] = o[w]
                lg_ref[g, dst, :] = jnp.broadcast_to(lse[w], (BAND_BLOCK, LANES))
    rows = 256
    for rc in range(DIL_CHUNK // rows):
        sl = slice(rc * rows, (rc + 1) * rows)
        lses = [lg_ref[g, sl, :] for g in range(n_g)]
        mx = functools.reduce(jnp.maximum, lses)
        ws = [jnp.exp(l - mx) for l in lses]
        den = functools.reduce(lambda a, b_: a + b_, ws)
        num = functools.reduce(lambda a, b_: a + b_, [ws[g] * og_ref[g, sl, :] for g in range(n_g)])
        o_ref[0, sl, :] = (num / den).astype(o_ref.dtype)


def _dilated_attention(hs, bias_tiles):
    b, s, _ = hs[0].shape
    in_specs = []
    args = []
    for hg in hs:
        for part in range(3):
            in_specs.append(pl.BlockSpec((1, s, DIL_HEAD_DIM),
                                         lambda bi, hi, ci, part=part: (bi, 0, part * DIL_HEADS + hi)))
            args.append(hg)
    in_specs.append(pl.BlockSpec((len(DIL_GROUPS), 1, BAND_BLOCK, 2 * BAND_BLOCK),
                                 lambda bi, hi, ci: (0, hi, 0, 0)))
    return pl.pallas_call(
        functools.partial(_dil_kernel, seq=s),
        grid=(b, DIL_HEADS, s // DIL_CHUNK),
        in_specs=in_specs,
        out_specs=pl.BlockSpec((1, DIL_CHUNK, DIL_HEAD_DIM), lambda bi, hi, ci: (bi, ci, hi)),
        out_shape=jax.ShapeDtypeStruct((b, s, DIL_HEADS * DIL_HEAD_DIM), BF16),
        scratch_shapes=[pltpu.VMEM((len(DIL_GROUPS), DIL_CHUNK, LANES), F32),
                        pltpu.VMEM((len(DIL_GROUPS), DIL_CHUNK, LANES), F32)],
        compiler_params=pltpu.CompilerParams(dimension_semantics=("parallel", "parallel", "arbitrary")),
        name="dilated_attention",
    )(*args, bias_tiles)


def _outproj_kernel(*refs, n_in, with_router):
    ins = refs[:n_in]
    ws = refs[n_in:2 * n_in]
    x_ref, g_ref, b_ref = refs[2 * n_in:2 * n_in + 3]
    rest = refs[2 * n_in + 3:]
    mix = _dot(ins[0][...], ws[0][...])
    for a_ref, w_ref in zip(ins[1:], ws[1:]):
        mix = mix + _dot(a_ref[...], w_ref[...])
    xn = _layer_norm(DEEPNORM_ALPHA * x_ref[...] + mix, g_ref[...], b_ref[...])
    if not with_router:
        o_ref, ob_ref = rest
    else:
        router_ref, o_ref, ob_ref, gates_ref, counts_ref, run_ref, tri_ref = rest
        tm = xn.shape[0]

        @pl.when(pl.program_id(0) == 0)
        def _():
            run_ref[...] = jnp.zeros_like(run_ref)
            tri_ref[...] = (lax.broadcasted_iota(jnp.int32, (tm, tm), 0)
                            > lax.broadcasted_iota(jnp.int32, (tm, tm), 1)).astype(BF16)

        logits = jnp.dot(xn, router_ref[...], preferred_element_type=F32, precision=lax.Precision.HIGHEST)
        lane = lax.broadcasted_iota(jnp.int32, logits.shape, 1).astype(F32)
        lg = jnp.where(lane < N_EXPERTS, logits, -jnp.inf)
        m1 = jnp.max(lg, axis=1, keepdims=True)
        i1 = jnp.min(jnp.where(lg == m1, lane, float(LANES)), axis=1, keepdims=True)
        lg2 = jnp.where(lane == i1, -jnp.inf, lg)
        m2 = jnp.max(lg2, axis=1, keepdims=True)
        i2 = jnp.min(jnp.where(lg2 == m2, lane, float(LANES)), axis=1, keepdims=True)
        e2 = jnp.exp(m2 - m1)
        den = 1.0 + e2
        oh1 = jnp.where(lane == i1, 1.0, 0.0)
        oh2 = jnp.where(lane == i2, 1.0, 0.0)
        both = oh1 + oh2
        before = _dot(tri_ref[...], both.astype(BF16)) + run_ref[...]
        r1 = jnp.sum(before * oh1, axis=1, keepdims=True)
        r2 = jnp.sum(before * oh2, axis=1, keepdims=True)
        run_ref[...] += jnp.sum(both, axis=0, keepdims=True)
        counts_ref[...] = run_ref[...]
        gates_ref[...] = (jnp.where(lane == 0.0, i1, 0.0) + jnp.where(lane == 1.0, i2, 0.0)
                          + jnp.where(lane == 2.0, 1.0 / den, 0.0) + jnp.where(lane == 3.0, e2 / den, 0.0)
                          + jnp.where(lane == 4.0, r1, 0.0) + jnp.where(lane == 5.0, r2, 0.0))
    o_ref[...] = xn
    ob_ref[...] = xn.astype(BF16)


def _outproj_ln(ins, ws, x, ln_g, ln_b, router=None, tm=512):
    t, d = x.shape
    n_in = len(ins)
    row = lambda width: pl.BlockSpec((tm, width), lambda i: (i, 0))
    full = lambda a: pl.BlockSpec(a.shape, lambda i: (0, 0))
    vec = lambda v: v.reshape(1, d).astype(F32)
    args = list(ins) + list(ws) + [x, vec(ln_g), vec(ln_b)]
    in_specs = [row(a.shape[1]) for a in ins] + [full(w) for w in ws] + [row(d), full(vec(ln_g)), full(vec(ln_b))]
    out_shape = [jax.ShapeDtypeStruct((t, d), F32), jax.ShapeDtypeStruct((t, d), BF16)]
    out_specs = [row(d), row(d)]
    if router is not None:
        router_pad = jnp.zeros((d, LANES), F32).at[:, :N_EXPERTS].set(router)
        args.append(router_pad)
        in_specs.append(full(router_pad))
        out_shape += [jax.ShapeDtypeStruct((t, LANES), F32), jax.ShapeDtypeStruct((1, LANES), F32)]
        out_specs += [row(LANES), pl.BlockSpec((1, LANES), lambda i: (0, 0))]
        scratch = [pltpu.VMEM((1, LANES), F32), pltpu.VMEM((tm, tm), BF16)]
    else:
        scratch = []
    semantics = "arbitrary" if router is not None else "parallel"
    return pl.pallas_call(
        functools.partial(_outproj_kernel, n_in=n_in, with_router=router is not None),
        grid=(t // tm,),
        in_specs=in_specs,
        out_specs=out_specs,
        out_shape=out_shape,
        scratch_shapes=scratch,
        compiler_params=pltpu.CompilerParams(dimension_semantics=(semantics,)),
        name="outproj_ln",
    )(*args)


def _ffn_kernel(xb_ref, x_ref, wg_ref, wu_ref, wd_ref, g_ref, b_ref, o_ref, ob_ref, acc_ref):
    f = pl.program_id(1)

    @pl.when(f == 0)
    def _():
        acc_ref[...] = jnp.zeros_like(acc_ref)

    xb = xb_ref[...]
    hg = _dot(xb, wg_ref[...])
    hu = _dot(xb, wu_ref[...])
    act = (hg * _sigmoid(hg) * hu).astype(BF16)
    acc_ref[...] += _dot(act, wd_ref[...])

    @pl.when(f == pl.num_programs(1) - 1)
    def _():
        xn = _layer_norm(DEEPNORM_ALPHA * x_ref[...] + acc_ref[...], g_ref[...], b_ref[...])
        o_ref[...] = xn
        ob_ref[...] = xn.astype(BF16)


def _ffn_ln(xb, x, wg, wu, wd, ln_g, ln_b, tm=512, tf=1408):
    t, d = x.shape
    ff = wg.shape[1]
    vec = lambda v: v.reshape(1, d).astype(F32)
    return pl.pallas_call(
        _ffn_kernel,
        grid=(t // tm, ff // tf),
        in_specs=[pl.BlockSpec((tm, d), lambda i, f: (i, 0)),
                  pl.BlockSpec((tm, d), lambda i, f: (i, 0)),
                  pl.BlockSpec((d, tf), lambda i, f: (0, f)),
                  pl.BlockSpec((d, tf), lambda i, f: (0, f)),
                  pl.BlockSpec((tf, d), lambda i, f: (f, 0)),
                  pl.BlockSpec((1, d), lambda i, f: (0, 0)),
                  pl.BlockSpec((1, d), lambda i, f: (0, 0))],
        out_specs=[pl.BlockSpec((tm, d), lambda i, f: (i, 0)),
                   pl.BlockSpec((tm, d), lambda i, f: (i, 0))],
        out_shape=[jax.ShapeDtypeStruct((t, d), F32), jax.ShapeDtypeStruct((t, d), BF16)],
        scratch_shapes=[pltpu.VMEM((tm, d), F32)],
        compiler_params=pltpu.CompilerParams(dimension_semantics=("parallel", "arbitrary")),
        name="ffn_ln",
    )(xb, x, wg, wu, wd, vec(ln_g), vec(ln_b))


_MOE_TM = 512


def _route(route, counts, n_tiles):
    expert = route[:, 0:2].astype(jnp.int32)
    rank = route[:, 4:6].astype(jnp.int32)
    counts = counts[0, :N_EXPERTS].astype(jnp.int32)
    padded = (counts + _MOE_TM - 1) // _MOE_TM * _MOE_TM
    ends = jnp.cumsum(padded)
    starts = ends - padded
    onehot = expert[:, :, None] == jnp.arange(N_EXPERTS, dtype=jnp.int32)
    pos = (rank + jnp.sum(jnp.where(onehot, starts, 0), axis=-1)).reshape(-1)
    tile_start = jnp.arange(n_tiles, dtype=jnp.int32) * _MOE_TM
    tile_expert = jnp.minimum(jnp.sum((tile_start[:, None] >= ends[None, :]).astype(jnp.int32), axis=1),
                              N_EXPERTS - 1)
    return pos.astype(jnp.int32), tile_expert.astype(jnp.int32)


def _row_scatter_kernel(pos_ref, x_ref, init_ref, o_ref, sem, *, tm):
    del init_ref

    def body(j, carry):
        for choice in range(2):
            p = pos_ref[0, 0, 2 * j + choice]
            pltpu.make_async_copy(x_ref.at[pl.ds(j, 1)], o_ref.at[pl.ds(p, 1)], sem).start()
        return carry

    lax.fori_loop(0, tm, body, 0, unroll=8)
    for _ in range(2):
        pltpu.make_async_copy(x_ref, o_ref.at[pl.ds(0, tm)], sem).wait()


def _row_scatter(x, pos, n_rows, tm=512):
    t, d = x.shape
    return pl.pallas_call(
        functools.partial(_row_scatter_kernel, tm=tm),
        grid=(t // tm,),
        in_specs=[pl.BlockSpec((1, 1, 2 * tm), lambda i: (i, 0, 0), memory_space=pltpu.SMEM),
                  pl.BlockSpec((tm, d), lambda i: (i, 0)),
                  pl.BlockSpec(memory_space=pl.ANY)],
        out_specs=pl.BlockSpec(memory_space=pl.ANY),
        out_shape=jax.ShapeDtypeStruct((n_rows, d), x.dtype),
        scratch_shapes=[pltpu.SemaphoreType.DMA(())],
        input_output_aliases={2: 0},
        compiler_params=pltpu.CompilerParams(dimension_semantics=("arbitrary",)),
        name="moe_row_scatter",
    )(pos.reshape(t // tm, 1, 2 * tm), x, jnp.zeros((n_rows, d), x.dtype))


def _expert_ffn_kernel(te_ref, x_ref, wg_ref, wu_ref, wd_ref, o_ref, acc_ref):
    del te_ref
    f = pl.program_id(1)

    @pl.when(f == 0)
    def _():
        acc_ref[...] = jnp.zeros_like(acc_ref)

    xb = x_ref[...].astype(BF16)
    hg = _dot(xb, wg_ref[0])
    hu = _dot(xb, wu_ref[0])
    act = (hg * _sigmoid(hg) * hu).astype(BF16)
    acc_ref[...] += _dot(act, wd_ref[0])

    @pl.when(f == pl.num_programs(1) - 1)
    def _():
        o_ref[...] = acc_ref[...]


def _expert_ffn(xs, tile_expert, wg, wu, wd, tf=896):
    n_rows, d = xs.shape
    ff = wg.shape[2]
    grid_spec = pltpu.PrefetchScalarGridSpec(
        num_scalar_prefetch=1,
        grid=(n_rows // _MOE_TM, ff // tf),
        in_specs=[pl.BlockSpec((_MOE_TM, d), lambda i, f, te: (i, 0)),
                  pl.BlockSpec((1, d, tf), lambda i, f, te: (te[i], 0, f)),
                  pl.BlockSpec((1, d, tf), lambda i, f, te: (te[i], 0, f)),
                  pl.BlockSpec((1, tf, d), lambda i, f, te: (te[i], f, 0))],
        out_specs=pl.BlockSpec((_MOE_TM, d), lambda i, f, te: (i, 0)),
        scratch_shapes=[pltpu.VMEM((_MOE_TM, d), F32)])
    return pl.pallas_call(
        _expert_ffn_kernel,
        grid_spec=grid_spec,
        out_shape=jax.ShapeDtypeStruct((n_rows, d), F32),
        compiler_params=pltpu.CompilerParams(dimension_semantics=("parallel", "arbitrary")),
        name="moe_expert_ffn",
    )(tile_expert, xs, wg, wu, wd)


def _combine_kernel(pos_ref, route_ref, x_ref, g_ref, b_ref, y_ref, o_ref, ybuf_ref, sem, *, tm):
    def body(j, carry):
        for choice in range(2):
            p = pos_ref[0, 0, 2 * j + choice]
            pltpu.make_async_copy(y_ref.at[pl.ds(p, 1)], ybuf_ref.at[choice, pl.ds(j, 1)], sem).start()
        return carry

    lax.fori_loop(0, tm, body, 0, unroll=8)
    for choice in range(2):
        pltpu.make_async_copy(y_ref.at[pl.ds(0, tm)], ybuf_ref.at[choice], sem).wait()
    route = route_ref[...]
    y = route[:, 2:3] * ybuf_ref[0] + route[:, 3:4] * ybuf_ref[1]
    o_ref[...] = _layer_norm(DEEPNORM_ALPHA * x_ref[...] + y, g_ref[...], b_ref[...])


def _combine_ln(ys, pos, route, x, ln_g, ln_b, tm=512):
    t, d = x.shape
    vec = lambda v: v.reshape(1, d).astype(F32)
    return pl.pallas_call(
        functools.partial(_combine_kernel, tm=tm),
        grid=(t // tm,),
        in_specs=[pl.BlockSpec((1, 1, 2 * tm), lambda i: (i, 0, 0), memory_space=pltpu.SMEM),
                  pl.BlockSpec((tm, LANES), lambda i: (i, 0)),
                  pl.BlockSpec((tm, d), lambda i: (i, 0)),
                  pl.BlockSpec((1, d), lambda i: (0, 0)),
                  pl.BlockSpec((1, d), lambda i: (0, 0)),
                  pl.BlockSpec(memory_space=pl.ANY)],
        out_specs=pl.BlockSpec((tm, d), lambda i: (i, 0)),
        out_shape=jax.ShapeDtypeStruct((t, d), F32),
        scratch_shapes=[pltpu.VMEM((2, tm, d), F32), pltpu.SemaphoreType.DMA(())],
        compiler_params=pltpu.CompilerParams(dimension_semantics=("arbitrary",)),
        name="moe_combine_ln",
    )(pos.reshape(t // tm, 1, 2 * tm), route, x, vec(ln_g), vec(ln_b), ys)


def _moe_ln(x, route, counts, wg, wu, wd, ln_g, ln_b):
    t, _ = x.shape
    n_tiles = 2 * t // _MOE_TM + N_EXPERTS
    pos, tile_expert = _route(route, counts, n_tiles)
    xs = _row_scatter(x, pos, n_tiles * _MOE_TM)
    ys = _expert_ffn(xs, tile_expert, wg, wu, wd)
    return _combine_ln(ys, pos, route, x, ln_g, ln_b)


def _even_layer(x, xb, batch, seq, dsa_bias, w_in, conv_w, conv_b, conv_ln_g, conv_ln_b, w_out,
                ln1_g, ln1_b, ffn_wg, ffn_wu, ffn_wd, ln2_g, ln2_b):
    w_in_p = jnp.zeros((D_MODEL, EVEN_IN_PAD), BF16).at[:, :w_in.shape[1]].set(w_in.astype(BF16))
    h = _matmul(xb, w_in_p, 1024, 640, BF16).reshape(batch, seq, EVEN_IN_PAD)
    a_out = _conformer_conv(h, conv_w, conv_b, conv_ln_g, conv_ln_b)
    att = _dsa_attention(h, dsa_bias)
    w_out_b = w_out.astype(BF16)
    x1, x1b = _outproj_ln([a_out.reshape(-1, CONV_CH), att.reshape(-1, DSA_HEADS * DSA_HEAD_DIM)],
                          [w_out_b[:CONV_CH], w_out_b[CONV_CH:]], x, ln1_g, ln1_b)
    return _ffn_ln(x1b, x1, ffn_wg.astype(BF16), ffn_wu.astype(BF16), ffn_wd.astype(BF16), ln2_g, ln2_b)


def _odd_layer(x, xb, batch, seq, dil_bias, w_in, w_out, ln1_g, ln1_b, router, moe_wg, moe_wu, moe_wd,
               ln2_g, ln2_b):
    gw = 3 * DIL_HEADS * DIL_HEAD_DIM
    w_in_b = w_in.astype(BF16)
    hs = [_matmul_residue_major(xb, w_in_b[:, g * gw:(g + 1) * gw], batch, seq, dil, 1024, 768)
          for g, (_, dil) in enumerate(DIL_GROUPS)]
    o = _dilated_attention(hs, dil_bias).reshape(-1, DIL_HEADS * DIL_HEAD_DIM)
    x1, _, route, counts = _outproj_ln([o], [w_out.astype(BF16)], x, ln1_g, ln1_b, router=router)
    out = _moe_ln(x1, route, counts, moe_wg.astype(BF16), moe_wu.astype(BF16), moe_wd.astype(BF16), ln2_g, ln2_b)
    return out, out.astype(BF16)


def kernel(x, rel_bias, even_w_in, even_conv_w, even_conv_b, even_conv_ln_g, even_conv_ln_b, even_w_out, even_ln1_g, even_ln1_b, even_ffn_wg, even_ffn_wu, even_ffn_wd, even_ln2_g, even_ln2_b, odd_w_in, odd_w_out, odd_ln1_g, odd_ln1_b, odd_router, odd_moe_wg, odd_moe_wu, odd_moe_wd, odd_ln2_g, odd_ln2_b):
    batch, seq, d = x.shape
    assert d == D_MODEL and seq % DIL_CHUNK == 0
    depth = even_w_in.shape[0] + odd_w_in.shape[0]
    assert depth == DEPTH
    dsa_bias = _dsa_bias_tiles(rel_bias, seq)
    dil_bias = _dil_bias_tiles(rel_bias)
    xf = x.reshape(batch * seq, d)
    xb = xf.astype(BF16)
    for layer in range(depth):
        i = layer // 2
        if layer % 2 == 0:
            xf, xb = _even_layer(xf, xb, batch, seq, dsa_bias, even_w_in[i], even_conv_w[i], even_conv_b[i],
                                 even_conv_ln_g[i], even_conv_ln_b[i], even_w_out[i], even_ln1_g[i],
                                 even_ln1_b[i], even_ffn_wg[i], even_ffn_wu[i], even_ffn_wd[i],
                                 even_ln2_g[i], even_ln2_b[i])
        else:
            xf, xb = _odd_layer(xf, xb, batch, seq, dil_bias, odd_w_in[i], odd_w_out[i], odd_ln1_g[i],
                                odd_ln1_b[i], odd_router[i], odd_moe_wg[i], odd_moe_wu[i], odd_moe_wd[i],
                                odd_ln2_g[i], odd_ln2_b[i])
    return xf.reshape(batch, seq, d)
```

```python
import functools
import math

import jax
import jax.numpy as jnp
from jax import lax
from jax.experimental import pallas as pl
from jax.experimental.pallas import tpu as pltpu

BF16 = jnp.bfloat16
F32 = jnp.float32

D_MODEL = 1024
CONV_CH = 512
CONV_WIDTH = 31
DSA_HEADS = 8
DSA_HEAD_DIM = 64
IDX_HEADS = 8
IDX_DIM = 64
DSA_TOPK_MAX = 256
Q_BLOCK = 128
DIL_GROUPS = ((128, 1), (512, 4), (2048, 16))
DIL_HEADS = 8
DIL_HEAD_DIM = 128
BAND_BLOCK = 128
NUM_BUCKETS = 32
MAX_DISTANCE = 2048
N_EXPERTS = 8
LN_EPS = 1e-5
NEG = -1e30
DEPTH = 2
DEEPNORM_ALPHA = (2 * DEPTH) ** 0.25

LANES = 128
EVEN_IN_PAD = 3200
INT_MIN = -(2 ** 31)
DIL_CHUNK = BAND_BLOCK * max(d for _, d in DIL_GROUPS)

_PARALLEL2 = pltpu.CompilerParams(dimension_semantics=("parallel", "parallel"))


def _layer_norm(y, g, b):
    mu = jnp.mean(y, axis=-1, keepdims=True)
    yc = y - mu
    var = jnp.mean(yc * yc, axis=-1, keepdims=True)
    return yc * lax.rsqrt(var + LN_EPS) * g + b


def _sigmoid(x):
    return 1.0 / (1.0 + jnp.exp(-x))


def _dot_nt(a, b):
    return lax.dot_general(a, b, (((1,), (1,)), ((), ())), preferred_element_type=F32)


def _dot(a, b):
    return jnp.dot(a, b, preferred_element_type=F32)


def _mm_kernel(x_ref, w_ref, o_ref):
    o_ref[...] = _dot(x_ref[...], w_ref[...]).astype(o_ref.dtype)


def _matmul(x, w, tm, tn, out_dtype):
    t, k = x.shape
    n = w.shape[1]
    return pl.pallas_call(
        _mm_kernel,
        grid=(t // tm, n // tn),
        in_specs=[pl.BlockSpec((tm, k), lambda i, j: (i, 0)),
                  pl.BlockSpec((k, tn), lambda i, j: (0, j))],
        out_specs=pl.BlockSpec((tm, tn), lambda i, j: (i, j)),
        out_shape=jax.ShapeDtypeStruct((t, n), out_dtype),
        compiler_params=_PARALLEL2,
        name="matmul",
    )(x, w)


def _mm_perm_kernel(x_ref, w_ref, o_ref, acc_ref, *, dil, tm, tn):
    acc = _dot(x_ref[...], w_ref[...])
    rows = tm // dil
    for c in range(tn // LANES):
        acc_ref[c] = acc[:, c * LANES:(c + 1) * LANES]
    for r in range(dil):
        for c in range(tn // LANES):
            o_ref[0, r, :, c * LANES:(c + 1) * LANES] = (
                acc_ref[c, pl.ds(r, rows, stride=dil), :].astype(o_ref.dtype))


def _matmul_residue_major(x, w, batch, seq, dil, tm, tn):
    t, k = x.shape
    n = w.shape[1]
    if dil == 1:
        return _matmul(x, w, tm, tn, BF16).reshape(batch, seq, n)
    tiles_per_seq = seq // tm
    out = pl.pallas_call(
        functools.partial(_mm_perm_kernel, dil=dil, tm=tm, tn=tn),
        grid=(t // tm, n // tn),
        in_specs=[pl.BlockSpec((tm, k), lambda i, j: (i, 0)),
                  pl.BlockSpec((k, tn), lambda i, j: (0, j))],
        out_specs=pl.BlockSpec((1, dil, tm // dil, tn),
                               lambda i, j: (i // tiles_per_seq, 0, i % tiles_per_seq, j)),
        out_shape=jax.ShapeDtypeStruct((batch, dil, seq // dil, n), BF16),
        scratch_shapes=[pltpu.VMEM((tn // LANES, tm, LANES), F32)],
        compiler_params=_PARALLEL2,
        name="matmul_residue_major",
    )(x, w)
    return out.reshape(batch, seq, n)


_CONV_HALO = 32
_CONV_ROWS = 64


def _conv_kernel(val_ref, gate_ref, w_ref, cb_ref, g_ref, b_ref, o_ref, ext_ref, *, ts):
    s = pl.program_id(1)

    @pl.when(s == 0)
    def _():
        ext_ref[0:_CONV_HALO, :] = jnp.zeros((_CONV_HALO, CONV_CH), F32)

    val = val_ref[0].astype(F32)
    gate = gate_ref[0].astype(F32)
    ext_ref[_CONV_HALO:_CONV_HALO + ts, :] = val * _sigmoid(gate)
    first_tap = _CONV_HALO - (CONV_WIDTH - 1)
    for rc in range(ts // _CONV_ROWS):
        r0 = rc * _CONV_ROWS
        acc = jnp.zeros((_CONV_ROWS, CONV_CH), F32)
        for j in range(CONV_WIDTH):
            lo = r0 + first_tap + j
            acc = acc + ext_ref[lo:lo + _CONV_ROWS, :] * w_ref[j:j + 1, :]
        y = _layer_norm(acc + cb_ref[...], g_ref[...], b_ref[...])
        o_ref[0, r0:r0 + _CONV_ROWS, :] = (y * _sigmoid(y)).astype(o_ref.dtype)
    ext_ref[0:_CONV_HALO, :] = ext_ref[ts:ts + _CONV_HALO, :]


def _conformer_conv(h, conv_w, conv_b, ln_g, ln_b, ts=512):
    b, s, _ = h.shape
    w_pad = jnp.zeros((32, CONV_CH), F32).at[:CONV_WIDTH].set(conv_w)
    vec = lambda v: v.reshape(1, CONV_CH).astype(F32)
    full = lambda shape: pl.BlockSpec(shape, lambda bi, si: (0, 0))
    return pl.pallas_call(
        functools.partial(_conv_kernel, ts=ts),
        grid=(b, s // ts),
        in_specs=[pl.BlockSpec((1, ts, CONV_CH), lambda bi, si: (bi, si, 0)),
                  pl.BlockSpec((1, ts, CONV_CH), lambda bi, si: (bi, si, 1)),
                  full((32, CONV_CH)), full((1, CONV_CH)), full((1, CONV_CH)), full((1, CONV_CH))],
        out_specs=pl.BlockSpec((1, ts, CONV_CH), lambda bi, si: (bi, si, 0)),
        out_shape=jax.ShapeDtypeStruct((b, s, CONV_CH), BF16),
        scratch_shapes=[pltpu.VMEM((ts + _CONV_HALO, CONV_CH), F32)],
        compiler_params=pltpu.CompilerParams(dimension_semantics=("parallel", "arbitrary")),
        name="conformer_conv",
    )(h, h, w_pad, vec(conv_b), vec(ln_g), vec(ln_b))


def _rel_bucket(dist):
    max_exact = NUM_BUCKETS // 2
    n = dist.astype(jnp.int32)
    nf = jnp.maximum(n, 1).astype(F32)
    large = max_exact + (jnp.log(nf / max_exact) / math.log(MAX_DISTANCE / max_exact)
                         * (NUM_BUCKETS - max_exact)).astype(jnp.int32)
    large = jnp.minimum(large, NUM_BUCKETS - 1)
    return jnp.where(n < max_exact, n, large)


def _bias_by_distance(rel_bias, n):
    return rel_bias[_rel_bucket(jnp.arange(n, dtype=jnp.int32))].astype(F32).T


def _toeplitz_kernel(win_ref, o_ref):
    n_win, span = win_ref.shape
    row = lax.broadcasted_iota(jnp.int32, (LANES, span), 0)
    for n in range(n_win):
        y = jnp.broadcast_to(win_ref[n:n + 1, :], (LANES, span))
        for bit in range(LANES.bit_length() - 1):
            y = jnp.where((row >> bit) & 1 == 1, pltpu.roll(y, 1 << bit, 1), y)
        o_ref[n] = y[:, LANES:]


def _toeplitz_tiles(windows, per_step=8):
    n, span = windows.shape
    width = span - LANES
    return pl.pallas_call(
        _toeplitz_kernel,
        grid=(n // per_step,),
        in_specs=[pl.BlockSpec((per_step, span), lambda i: (i, 0))],
        out_specs=pl.BlockSpec((per_step, LANES, width), lambda i: (i, 0, 0)),
        out_shape=jax.ShapeDtypeStruct((n, LANES, width), F32),
        compiler_params=pltpu.CompilerParams(dimension_semantics=("parallel",)),
        name="bias_toeplitz",
    )(windows)


def _dsa_bias_tiles(rel_bias, seq):
    nblk = seq // Q_BLOCK
    heads = rel_bias.shape[1]
    f = _bias_by_distance(rel_bias, seq)
    u = jnp.concatenate([jnp.broadcast_to(f[:, :1], (heads, 2 * Q_BLOCK)), f], axis=1)
    blocks = u.reshape(heads, nblk + 2, Q_BLOCK)
    windows = jnp.concatenate([blocks[:, :-1], blocks[:, 1:]], axis=-1)
    windows = windows.transpose(1, 0, 2).reshape((nblk + 1) * heads, 2 * Q_BLOCK)
    return _toeplitz_tiles(windows).reshape(nblk + 1, heads, Q_BLOCK, Q_BLOCK)


def _dil_bias_tiles(rel_bias):
    max_dil = max(d for _, d in DIL_GROUPS)
    heads = rel_bias.shape[1]
    span = 2 * BAND_BLOCK
    f = _bias_by_distance(rel_bias, span * max_dil + 1)
    windows = []
    for _, dil in DIL_GROUPS:
        fd = f[:, 0:span * dil + 1:dil]
        tail = jnp.broadcast_to(fd[:, :1], (heads, BAND_BLOCK - 1))
        windows.append(jnp.concatenate([fd[:, ::-1], tail], axis=1))
    tiles = _toeplitz_tiles(jnp.concatenate(windows, axis=0))
    return tiles.reshape(len(DIL_GROUPS), heads, BAND_BLOCK, span)


_KC = 2 * Q_BLOCK


def _sortable_key(score):
    score = jnp.where(score == 0.0, 0.0, score)
    bits = lax.bitcast_convert_type(score, jnp.int32)
    return jnp.where(bits < 0, bits ^ jnp.int32(0x7FFFFFFF), bits)


def _rows_reduce(x, op):
    rows = x.shape[0]
    return op(op(x.reshape(rows // 8, 8, LANES), axis=0), axis=0, keepdims=True)


def _keys_reduce(x, op):
    h, keys, _ = x.shape
    return op(op(x.reshape(h, keys // 8, 8, LANES), axis=1), axis=1, keepdims=True)


def _dsa_t_kernel(q_ref, qi_ref, wq_ref, k_ref, vt_ref, ki_ref, bias_hbm, o_ref,
                  bias_ref, keys_ref, qm_ref, acc_ref, sem, *, topk):
    b = pl.program_id(0)
    i = pl.program_id(1)

    @pl.when((b == 0) & (i == 0))
    def _():
        cp = pltpu.make_async_copy(bias_hbm, bias_ref, sem)
        cp.start()
        cp.wait()

    nch = (i + 2) // 2
    key_off = lax.broadcasted_iota(jnp.int32, (_KC, LANES), 0)
    q_pos = i * Q_BLOCK + lax.broadcasted_iota(jnp.int32, (_KC, LANES), 1)
    lane = lax.broadcasted_iota(jnp.int32, (Q_BLOCK, LANES), 1)
    low_half = lane < DSA_HEAD_DIM

    w_t = (wq_ref[0].astype(F32) * (IDX_HEADS ** -0.5 * IDX_DIM ** -0.5)).T
    qi = qi_ref[0].astype(F32)
    qi_heads = []
    for p in range(IDX_HEADS // 2):
        pair = qi[:, p * LANES:(p + 1) * LANES]
        qi_heads.append(jnp.where(low_half, pair, 0.0))
        qi_heads.append(jnp.where(low_half, pltpu.roll(pair, IDX_DIM, 1), 0.0))
    qi_stack = jnp.concatenate(qi_heads, axis=0).astype(BF16)
    q_scaled = (q_ref[0].astype(F32) * (DSA_HEAD_DIM ** -0.5)).astype(BF16)
    for p in range(DSA_HEADS // 2):
        qp = q_scaled[:, p * LANES:(p + 1) * LANES]
        qm_ref[2 * p] = jnp.where(low_half, qp, jnp.zeros_like(qp))
        qm_ref[2 * p + 1] = jnp.where(low_half, jnp.zeros_like(qp), qp)

    def score_body(c, carry):
        k0 = pl.multiple_of(c * _KC, _KC)
        ki = ki_ref[0, pl.ds(k0, _KC), :]
        dots = _dot_nt(ki, qi_stack)
        score = jnp.zeros((_KC, LANES), F32)
        for h in range(IDX_HEADS):
            w_row = w_t[IDX_DIM + h:IDX_DIM + h + 1, :]
            score = score + w_row * jnp.maximum(dots[:, h * Q_BLOCK:(h + 1) * Q_BLOCK], 0.0)
        keys_ref[c] = jnp.where(k0 + key_off <= q_pos, _sortable_key(score), jnp.int32(INT_MIN))
        return carry

    lax.fori_loop(0, nch, score_body, 0)

    def count_ge(cand):
        cb = jnp.broadcast_to(cand, (8, LANES))
        lanes_acc = 4

        def body(c, acc):
            kk = keys_ref[c].reshape(_KC // (8 * lanes_acc), lanes_acc, 8, LANES)
            return acc + jnp.sum(jnp.where(kk >= cb, 1.0, 0.0), axis=0)

        acc = lax.fori_loop(0, nch, body, jnp.zeros((lanes_acc, 8, LANES), F32))
        return jnp.sum(jnp.sum(acc, axis=0), axis=0, keepdims=True)

    kf = float(topk)
    zero = jnp.zeros((1, LANES), jnp.int32)
    thr0 = jnp.where(count_ge(zero) >= kf, zero, jnp.full((1, LANES), INT_MIN, jnp.int32))

    def bit_body(t, thr):
        cand = thr | jnp.left_shift(jnp.int32(1), 30 - t)
        return jnp.where(count_ge(cand) >= kf, cand, thr)

    thr = lax.fori_loop(0, 31, bit_body, thr0)
    n_gt = count_ge(jnp.where(thr == jnp.int32(2 ** 31 - 1), thr, thr + 1))
    need = kf - n_gt
    tri = (lax.broadcasted_iota(jnp.int32, (_KC, _KC), 1)
           <= lax.broadcasted_iota(jnp.int32, (_KC, _KC), 0)).astype(BF16)

    acc_ref[...] = jnp.zeros_like(acc_ref)

    def attn_body(c, carry):
        ms, ls, tie_carry = carry
        k0 = pl.multiple_of(c * _KC, _KC)
        kk = keys_ref[c]
        causal = kk > jnp.int32(INT_MIN)
        eq = (kk == thr) & causal
        eq_f = jnp.where(eq, 1.0, 0.0)
        rank = _dot(tri, eq_f.astype(BF16)) + tie_carry
        sel = ((kk > thr) & causal) | (eq & (rank <= need))
        tie_carry = tie_carry + _rows_reduce(eq_f, jnp.sum)
        d = i - 2 * c
        kc = k_ref[0, pl.ds(k0, _KC), :]
        k_heads = jnp.stack([kc[:, (h // 2) * LANES:(h // 2 + 1) * LANES] for h in range(DSA_HEADS)])
        s = jnp.einsum("hkd,hqd->hkq", k_heads, qm_ref[...], preferred_element_type=F32)
        bias = jnp.concatenate([bias_ref[d + 1], bias_ref[d]], axis=1)
        s = jnp.where(sel, s + bias, NEG)
        m_new = jnp.maximum(ms, _keys_reduce(s, jnp.max))
        alpha = jnp.exp(ms - m_new)
        pr = jnp.exp(s - m_new)
        ls = alpha * ls + _keys_reduce(pr, jnp.sum)
        vt = vt_ref[0, c].reshape(DSA_HEADS, DSA_HEAD_DIM, _KC)
        pv = jnp.einsum("hdk,hkq->hdq", vt, pr.astype(BF16), preferred_element_type=F32)
        acc_ref[...] = alpha * acc_ref[...] + pv
        return m_new, ls, tie_carry

    init = (jnp.full((DSA_HEADS, 1, LANES), NEG, F32), jnp.zeros((DSA_HEADS, 1, LANES), F32),
            jnp.zeros((1, LANES), F32))
    _, ls, _ = lax.fori_loop(0, nch, attn_body, init)
    out_t = (acc_ref[...] / ls).reshape(DSA_HEADS * DSA_HEAD_DIM, Q_BLOCK)
    o_ref[0] = out_t.T.astype(o_ref.dtype)


def _dsa_attention(h, bias_tiles):
    b, s, _ = h.shape
    width = DSA_HEADS * DSA_HEAD_DIM
    nblk = s // Q_BLOCK
    topk = min(DSA_TOPK_MAX, s // 4)
    kw_block = (CONV_CH * 2 + 4 * width) // LANES
    v_cols = 2 * CONV_CH + 2 * width
    vt = h[:, :, v_cols:v_cols + width].reshape(b, s // _KC, _KC, width).transpose(0, 1, 3, 2)
    return pl.pallas_call(
        functools.partial(_dsa_t_kernel, topk=topk),
        grid=(b, nblk),
        in_specs=[pl.BlockSpec((1, Q_BLOCK, width), lambda bi, qi: (bi, qi, 2)),
                  pl.BlockSpec((1, Q_BLOCK, width), lambda bi, qi: (bi, qi, 5)),
                  pl.BlockSpec((1, Q_BLOCK, LANES), lambda bi, qi: (bi, qi, kw_block)),
                  pl.BlockSpec((1, s, width), lambda bi, qi: (bi, 0, 3)),
                  pl.BlockSpec((1, s // _KC, width, _KC), lambda bi, qi: (bi, 0, 0, 0)),
                  pl.BlockSpec((1, s, LANES), lambda bi, qi: (bi, 0, kw_block)),
                  pl.BlockSpec(memory_space=pl.ANY)],
        out_specs=pl.BlockSpec((1, Q_BLOCK, width), lambda bi, qi: (bi, qi, 0)),
        out_shape=jax.ShapeDtypeStruct((b, s, width), BF16),
        scratch_shapes=[pltpu.VMEM((nblk + 1, DSA_HEADS, Q_BLOCK, Q_BLOCK), F32),
                        pltpu.VMEM((s // _KC, _KC, Q_BLOCK), jnp.int32),
                        pltpu.VMEM((DSA_HEADS, Q_BLOCK, LANES), BF16),
                        pltpu.VMEM((DSA_HEADS, DSA_HEAD_DIM, Q_BLOCK), F32),
                        pltpu.SemaphoreType.DMA(())],
        compiler_params=pltpu.CompilerParams(dimension_semantics=("arbitrary", "arbitrary")),
        name="dsa_attention",
    )(h, h, h, h, vt, h, bias_tiles)


_DIL_WAVE = 8


def _dil_kernel(*refs, seq):
    n_g = len(DIL_GROUPS)
    qkv = refs[:3 * n_g]
    bias_ref, o_ref, og_ref, lg_ref = refs[3 * n_g:]
    c = pl.program_id(2)
    ii = lax.broadcasted_iota(jnp.int32, (BAND_BLOCK, BAND_BLOCK), 0)
    jj = lax.broadcasted_iota(jnp.int32, (BAND_BLOCK, BAND_BLOCK), 1)
    valid_prev = jj >= ii
    valid_cur = ii >= jj
    scale = DIL_HEAD_DIM ** -0.5
    for g, (window, dil) in enumerate(DIL_GROUPS):
        assert window // dil == BAND_BLOCK
        q_ref, k_ref, v_ref = qkv[3 * g:3 * g + 3]
        sub_len = seq // dil
        blocks_per_chunk = DIL_CHUNK // (BAND_BLOCK * dil)
        bias_prev = bias_ref[g, 0, :, :BAND_BLOCK]
        bias_cur = bias_ref[g, 0, :, BAND_BLOCK:]
        tiles = [(r, nl) for r in range(dil) for nl in range(blocks_per_chunk)]
        for w0 in range(0, len(tiles), _DIL_WAVE):
            wave = tiles[w0:w0 + _DIL_WAVE]
            cur, prev, mask_prev = [], [], []
            for r, nl in wave:
                row0 = pl.multiple_of(r * sub_len + c * (DIL_CHUNK // dil) + nl * BAND_BLOCK, BAND_BLOCK)
                cur.append(pl.ds(row0, BAND_BLOCK))
                prev.append(pl.ds(pl.multiple_of(jnp.maximum(row0 - BAND_BLOCK, 0), BAND_BLOCK), BAND_BLOCK))
                mask_prev.append(valid_prev if nl > 0 else (valid_prev & (c > 0)))
            gather = lambda ref, slices: jnp.stack([ref[0, sl, :] for sl in slices])
            qb = gather(q_ref, cur)
            s_cur = jnp.einsum("wqd,wkd->wqk", qb, gather(k_ref, cur), preferred_element_type=F32)
            s_prev = jnp.einsum("wqd,wkd->wqk", qb, gather(k_ref, prev), preferred_element_type=F32)
            s_cur = jnp.where(valid_cur, s_cur * scale + bias_cur, NEG)
            s_prev = jnp.where(jnp.stack(mask_prev), s_prev * scale + bias_prev, NEG)
            m = jnp.max(jnp.maximum(s_cur, s_prev), axis=2, keepdims=True)
            p_cur = jnp.exp(s_cur - m)
            p_prev = jnp.exp(s_prev - m)
            ssum = jnp.sum(p_cur + p_prev, axis=2, keepdims=True)
            o = (jnp.einsum("wqk,wkd->wqd", p_cur.astype(BF16), gather(v_ref, cur), preferred_element_type=F32)
                 + jnp.einsum("wqk,wkd->wqd", p_prev.astype(BF16), gather(v_ref, prev),
                              preferred_element_type=F32)) / ssum
            lse = m + jnp.log(ssum)
            for w, (r, nl) in enumerate(wave):
                dst = pl.ds(r + nl * BAND_BLOCK * dil, BAND_BLOCK, stride=dil) if dil > 1 else pl.ds(
                    nl * BAND_BLOCK, BAND_BLOCK)
                og_ref[g, dst, :] = o[w]
                lg_ref[g, dst, :] = jnp.broadcast_to(lse[w], (BAND_BLOCK, LANES))
    rows = 256
    for rc in range(DIL_CHUNK // rows):
        sl = slice(rc * rows, (rc + 1) * rows)
        lses = [lg_ref[g, sl, :] for g in range(n_g)]
        mx = functools.reduce(jnp.maximum, lses)
        ws = [jnp.exp(l - mx) for l in lses]
        den = functools.reduce(lambda a, b_: a + b_, ws)
        num = functools.reduce(lambda a, b_: a + b_, [ws[g] * og_ref[g, sl, :] for g in range(n_g)])
        o_ref[0, sl, :] = (num / den).astype(o_ref.dtype)


def _dilated_attention(hs, bias_tiles):
    b, s, _ = hs[0].shape
    in_specs = []
    args = []
    for hg in hs:
        for part in range(3):
            in_specs.append(pl.BlockSpec((1, s, DIL_HEAD_DIM),
                                         lambda bi, hi, ci, part=part: (bi, 0, part * DIL_HEADS + hi)))
            args.append(hg)
    in_specs.append(pl.BlockSpec((len(DIL_GROUPS), 1, BAND_BLOCK, 2 * BAND_BLOCK),
                                 lambda bi, hi, ci: (0, hi, 0, 0)))
    return pl.pallas_call(
        functools.partial(_dil_kernel, seq=s),
        grid=(b, DIL_HEADS, s // DIL_CHUNK),
        in_specs=in_specs,
        out_specs=pl.BlockSpec((1, DIL_CHUNK, DIL_HEAD_DIM), lambda bi, hi, ci: (bi, ci, hi)),
        out_shape=jax.ShapeDtypeStruct((b, s, DIL_HEADS * DIL_HEAD_DIM), BF16),
        scratch_shapes=[pltpu.VMEM((len(DIL_GROUPS), DIL_CHUNK, LANES), F32),
                        pltpu.VMEM((len(DIL_GROUPS), DIL_CHUNK, LANES), F32)],
        compiler_params=pltpu.CompilerParams(dimension_semantics=("parallel", "parallel", "arbitrary")),
        name="dilated_attention",
    )(*args, bias_tiles)


def _outproj_kernel(*refs, n_in, with_router):
    ins = refs[:n_in]
    ws = refs[n_in:2 * n_in]
    x_ref, g_ref, b_ref = refs[2 * n_in:2 * n_in + 3]
    rest = refs[2 * n_in + 3:]
    mix = _dot(ins[0][...], ws[0][...])
    for a_ref, w_ref in zip(ins[1:], ws[1:]):
        mix = mix + _dot(a_ref[...], w_ref[...])
    xn = _layer_norm(DEEPNORM_ALPHA * x_ref[...] + mix, g_ref[...], b_ref[...])
    if not with_router:
        o_ref, ob_ref = rest
    else:
        router_ref, o_ref, ob_ref, gates_ref, counts_ref, run_ref, tri_ref = rest
        tm = xn.shape[0]

        @pl.when(pl.program_id(0) == 0)
        def _():
            run_ref[...] = jnp.zeros_like(run_ref)
            tri_ref[...] = (lax.broadcasted_iota(jnp.int32, (tm, tm), 0)
                            > lax.broadcasted_iota(jnp.int32, (tm, tm), 1)).astype(BF16)

        logits = jnp.dot(xn, router_ref[...], preferred_element_type=F32, precision=lax.Precision.HIGHEST)
        lane = lax.broadcasted_iota(jnp.int32, logits.shape, 1).astype(F32)
        lg = jnp.where(lane < N_EXPERTS, logits, -jnp.inf)
        m1 = jnp.max(lg, axis=1, keepdims=True)
        i1 = jnp.min(jnp.where(lg == m1, lane, float(LANES)), axis=1, keepdims=True)
        lg2 = jnp.where(lane == i1, -jnp.inf, lg)
        m2 = jnp.max(lg2, axis=1, keepdims=True)
        i2 = jnp.min(jnp.where(lg2 == m2, lane, float(LANES)), axis=1, keepdims=True)
        e2 = jnp.exp(m2 - m1)
        den = 1.0 + e2
        oh1 = jnp.where(lane == i1, 1.0, 0.0)
        oh2 = jnp.where(lane == i2, 1.0, 0.0)
        both = oh1 + oh2
        before = _dot(tri_ref[...], both.astype(BF16)) + run_ref[...]
        r1 = jnp.sum(before * oh1, axis=1, keepdims=True)
        r2 = jnp.sum(before * oh2, axis=1, keepdims=True)
        run_ref[...] += jnp.sum(both, axis=0, keepdims=True)
        counts_ref[...] = run_ref[...]
        gates_ref[...] = (jnp.where(lane == 0.0, i1, 0.0) + jnp.where(lane == 1.0, i2, 0.0)
                          + jnp.where(lane == 2.0, 1.0 / den, 0.0) + jnp.where(lane == 3.0, e2 / den, 0.0)
                          + jnp.where(lane == 4.0, r1, 0.0) + jnp.where(lane == 5.0, r2, 0.0))
    o_ref[...] = xn
    ob_ref[...] = xn.astype(BF16)


def _outproj_ln(ins, ws, x, ln_g, ln_b, router=None, tm=512):
    t, d = x.shape
    n_in = len(ins)
    row = lambda width: pl.BlockSpec((tm, width), lambda i: (i, 0))
    full = lambda a: pl.BlockSpec(a.shape, lambda i: (0, 0))
    vec = lambda v: v.reshape(1, d).astype(F32)
    args = list(ins) + list(ws) + [x, vec(ln_g), vec(ln_b)]
    in_specs = [row(a.shape[1]) for a in ins] + [full(w) for w in ws] + [row(d), full(vec(ln_g)), full(vec(ln_b))]
    out_shape = [jax.ShapeDtypeStruct((t, d), F32), jax.ShapeDtypeStruct((t, d), BF16)]
    out_specs = [row(d), row(d)]
    if router is not None:
        router_pad = jnp.zeros((d, LANES), F32).at[:, :N_EXPERTS].set(router)
        args.append(router_pad)
        in_specs.append(full(router_pad))
        out_shape += [jax.ShapeDtypeStruct((t, LANES), F32), jax.ShapeDtypeStruct((1, LANES), F32)]
        out_specs += [row(LANES), pl.BlockSpec((1, LANES), lambda i: (0, 0))]
        scratch = [pltpu.VMEM((1, LANES), F32), pltpu.VMEM((tm, tm), BF16)]
    else:
        scratch = []
    semantics = "arbitrary" if router is not None else "parallel"
    return pl.pallas_call(
        functools.partial(_outproj_kernel, n_in=n_in, with_router=router is not None),
        grid=(t // tm,),
        in_specs=in_specs,
        out_specs=out_specs,
        out_shape=out_shape,
        scratch_shapes=scratch,
        compiler_params=pltpu.CompilerParams(dimension_semantics=(semantics,)),
        name="outproj_ln",
    )(*args)


def _ffn_kernel(xb_ref, x_ref, wg_ref, wu_ref, wd_ref, g_ref, b_ref, o_ref, ob_ref, acc_ref):
    f = pl.program_id(1)

    @pl.when(f == 0)
    def _():
        acc_ref[...] = jnp.zeros_like(acc_ref)

    xb = xb_ref[...]
    hg = _dot(xb, wg_ref[...])
    hu = _dot(xb, wu_ref[...])
    act = (hg * _sigmoid(hg) * hu).astype(BF16)
    acc_ref[...] += _dot(act, wd_ref[...])

    @pl.when(f == pl.num_programs(1) - 1)
    def _():
        xn = _layer_norm(DEEPNORM_ALPHA * x_ref[...] + acc_ref[...], g_ref[...], b_ref[...])
        o_ref[...] = xn
        ob_ref[...] = xn.astype(BF16)


def _ffn_ln(xb, x, wg, wu, wd, ln_g, ln_b, tm=512, tf=1408):
    t, d = x.shape
    ff = wg.shape[1]
    vec = lambda v: v.reshape(1, d).astype(F32)
    return pl.pallas_call(
        _ffn_kernel,
        grid=(t // tm, ff // tf),
        in_specs=[pl.BlockSpec((tm, d), lambda i, f: (i, 0)),
                  pl.BlockSpec((tm, d), lambda i, f: (i, 0)),
                  pl.BlockSpec((d, tf), lambda i, f: (0, f)),
                  pl.BlockSpec((d, tf), lambda i, f: (0, f)),
                  pl.BlockSpec((tf, d), lambda i, f: (f, 0)),
                  pl.BlockSpec((1, d), lambda i, f: (0, 0)),
                  pl.BlockSpec((1, d), lambda i, f: (0, 0))],
        out_specs=[pl.BlockSpec((tm, d), lambda i, f: (i, 0)),
                   pl.BlockSpec((tm, d), lambda i, f: (i, 0))],
        out_shape=[jax.ShapeDtypeStruct((t, d), F32), jax.ShapeDtypeStruct((t, d), BF16)],
        scratch_shapes=[pltpu.VMEM((tm, d), F32)],
        compiler_params=pltpu.CompilerParams(dimension_semantics=("parallel", "arbitrary")),
        name="ffn_ln",
    )(xb, x, wg, wu, wd, vec(ln_g), vec(ln_b))


_MOE_TM = 512


def _route(route, counts, n_tiles):
    expert = route[:, 0:2].astype(jnp.int32)
    rank = route[:, 4:6].astype(jnp.int32)
    counts = counts[0, :N_EXPERTS].astype(jnp.int32)
    padded = (counts + _MOE_TM - 1) // _MOE_TM * _MOE_TM
    ends = jnp.cumsum(padded)
    starts = ends - padded
    onehot = expert[:, :, None] == jnp.arange(N_EXPERTS, dtype=jnp.int32)
    pos = (rank + jnp.sum(jnp.where(onehot, starts, 0), axis=-1)).reshape(-1)
    tile_start = jnp.arange(n_tiles, dtype=jnp.int32) * _MOE_TM
    tile_expert = jnp.minimum(jnp.sum((tile_start[:, None] >= ends[None, :]).astype(jnp.int32), axis=1),
                              N_EXPERTS - 1)
    return pos.astype(jnp.int32), tile_expert.astype(jnp.int32)


def _row_scatter_kernel(pos_ref, x_ref, init_ref, o_ref, sem, *, tm):
    del init_ref

    def body(j, carry):
        for choice in range(2):
            p = pos_ref[0, 0, 2 * j + choice]
            pltpu.make_async_copy(x_ref.at[pl.ds(j, 1)], o_ref.at[pl.ds(p, 1)], sem).start()
        return carry

    lax.fori_loop(0, tm, body, 0, unroll=8)
    for _ in range(2):
        pltpu.make_async_copy(x_ref, o_ref.at[pl.ds(0, tm)], sem).wait()


def _row_scatter(x, pos, n_rows, tm=512):
    t, d = x.shape
    return pl.pallas_call(
        functools.partial(_row_scatter_kernel, tm=tm),
        grid=(t // tm,),
        in_specs=[pl.BlockSpec((1, 1, 2 * tm), lambda i: (i, 0, 0), memory_space=pltpu.SMEM),
                  pl.BlockSpec((tm, d), lambda i: (i, 0)),
                  pl.BlockSpec(memory_space=pl.ANY)],
        out_specs=pl.BlockSpec(memory_space=pl.ANY),
        out_shape=jax.ShapeDtypeStruct((n_rows, d), x.dtype),
        scratch_shapes=[pltpu.SemaphoreType.DMA(())],
        input_output_aliases={2: 0},
        compiler_params=pltpu.CompilerParams(dimension_semantics=("arbitrary",)),
        name="moe_row_scatter",
    )(pos.reshape(t // tm, 1, 2 * tm), x, jnp.zeros((n_rows, d), x.dtype))


def _expert_ffn_kernel(te_ref, x_ref, wg_ref, wu_ref, wd_ref, o_ref, acc_ref):
    del te_ref
    f = pl.program_id(1)

    @pl.when(f == 0)
    def _():
        acc_ref[...] = jnp.zeros_like(acc_ref)

    xb = x_ref[...].astype(BF16)
    hg = _dot(xb, wg_ref[0])
    hu = _dot(xb, wu_ref[0])
    act = (hg * _sigmoid(hg) * hu).astype(BF16)
    acc_ref[...] += _dot(act, wd_ref[0])

    @pl.when(f == pl.num_programs(1) - 1)
    def _():
        o_ref[...] = acc_ref[...]


def _expert_ffn(xs, tile_expert, wg, wu, wd, tf=896):
    n_rows, d = xs.shape
    ff = wg.shape[2]
    grid_spec = pltpu.PrefetchScalarGridSpec(
        num_scalar_prefetch=1,
        grid=(n_rows // _MOE_TM, ff // tf),
        in_specs=[pl.BlockSpec((_MOE_TM, d), lambda i, f, te: (i, 0)),
                  pl.BlockSpec((1, d, tf), lambda i, f, te: (te[i], 0, f)),
                  pl.BlockSpec((1, d, tf), lambda i, f, te: (te[i], 0, f)),
                  pl.BlockSpec((1, tf, d), lambda i, f, te: (te[i], f, 0))],
        out_specs=pl.BlockSpec((_MOE_TM, d), lambda i, f, te: (i, 0)),
        scratch_shapes=[pltpu.VMEM((_MOE_TM, d), F32)])
    return pl.pallas_call(
        _expert_ffn_kernel,
        grid_spec=grid_spec,
        out_shape=jax.ShapeDtypeStruct((n_rows, d), F32),
        compiler_params=pltpu.CompilerParams(dimension_semantics=("parallel", "arbitrary")),
        name="moe_expert_ffn",
    )(tile_expert, xs, wg, wu, wd)


def _combine_kernel(pos_ref, route_ref, x_ref, g_ref, b_ref, y_ref, o_ref, ybuf_ref, sem, *, tm):
    def body(j, carry):
        for choice in range(2):
            p = pos_ref[0, 0, 2 * j + choice]
            pltpu.make_async_copy(y_ref.at[pl.ds(p, 1)], ybuf_ref.at[choice, pl.ds(j, 1)], sem).start()
        return carry

    lax.fori_loop(0, tm, body, 0, unroll=8)
    for choice in range(2):
        pltpu.make_async_copy(y_ref.at[pl.ds(0, tm)], ybuf_ref.at[choice], sem).wait()
    route = route_ref[...]
    y = route[:, 2:3] * ybuf_ref[0] + route[:, 3:4] * ybuf_ref[1]
    o_ref[...] = _layer_norm(DEEPNORM_ALPHA * x_ref[...] + y, g_ref[...], b_ref[...])


def _combine_ln(ys, pos, route, x, ln_g, ln_b, tm=512):
    t, d = x.shape
    vec = lambda v: v.reshape(1, d).astype(F32)
    return pl.pallas_call(
        functools.partial(_combine_kernel, tm=tm),
        grid=(t // tm,),
        in_specs=[pl.BlockSpec((1, 1, 2 * tm), lambda i: (i, 0, 0), memory_space=pltpu.SMEM),
                  pl.BlockSpec((tm, LANES), lambda i: (i, 0)),
                  pl.BlockSpec((tm, d), lambda i: (i, 0)),
                  pl.BlockSpec((1, d), lambda i: (0, 0)),
                  pl.BlockSpec((1, d), lambda i: (0, 0)),
                  pl.BlockSpec(memory_space=pl.ANY)],
        out_specs=pl.BlockSpec((tm, d), lambda i: (i, 0)),
        out_shape=jax.ShapeDtypeStruct((t, d), F32),
        scratch_shapes=[pltpu.VMEM((2, tm, d), F32), pltpu.SemaphoreType.DMA(())],
        compiler_params=pltpu.CompilerParams(dimension_semantics=("arbitrary",)),
        name="moe_combine_ln",
    )(pos.reshape(t // tm, 1, 2 * tm), route, x, vec(ln_g), vec(ln_b), ys)


def _moe_ln(x, route, counts, wg, wu, wd, ln_g, ln_b):
    t, _ = x.shape
    n_tiles = 2 * t // _MOE_TM + N_EXPERTS
    pos, tile_expert = _route(route, counts, n_tiles)
    xs = _row_scatter(x, pos, n_tiles * _MOE_TM)
    ys = _expert_ffn(xs, tile_expert, wg, wu, wd)
    return _combine_ln(ys, pos, route, x, ln_g, ln_b)


def _even_layer(x, xb, batch, seq, dsa_bias, w_in, conv_w, conv_b, conv_ln_g, conv_ln_b, w_out,
                ln1_g, ln1_b, ffn_wg, ffn_wu, ffn_wd, ln2_g, ln2_b):
    w_in_p = jnp.zeros((D_MODEL, EVEN_IN_PAD), BF16).at[:, :w_in.shape[1]].set(w_in.astype(BF16))
    h = _matmul(xb, w_in_p, 1024, 640, BF16).reshape(batch, seq, EVEN_IN_PAD)
    a_out = _conformer_conv(h, conv_w, conv_b, conv_ln_g, conv_ln_b)
    att = _dsa_attention(h, dsa_bias)
    w_out_b = w_out.astype(BF16)
    x1, x1b = _outproj_ln([a_out.reshape(-1, CONV_CH), att.reshape(-1, DSA_HEADS * DSA_HEAD_DIM)],
                          [w_out_b[:CONV_CH], w_out_b[CONV_CH:]], x, ln1_g, ln1_b)
    return _ffn_ln(x1b, x1, ffn_wg.astype(BF16), ffn_wu.astype(BF16), ffn_wd.astype(BF16), ln2_g, ln2_b)


def _odd_layer(x, xb, batch, seq, dil_bias, w_in, w_out, ln1_g, ln1_b, router, moe_wg, moe_wu, moe_wd,
               ln2_g, ln2_b):
    gw = 3 * DIL_HEADS * DIL_HEAD_DIM
    w_in_b = w_in.astype(BF16)
    hs = [_matmul_residue_major(xb, w_in_b[:, g * gw:(g + 1) * gw], batch, seq, dil, 1024, 768)
          for g, (_, dil) in enumerate(DIL_GROUPS)]
    o = _dilated_attention(hs, dil_bias).reshape(-1, DIL_HEADS * DIL_HEAD_DIM)
    x1, _, route, counts = _outproj_ln([o], [w_out.astype(BF16)], x, ln1_g, ln1_b, router=router)
    out = _moe_ln(x1, route, counts, moe_wg.astype(BF16), moe_wu.astype(BF16), moe_wd.astype(BF16), ln2_g, ln2_b)
    return out, out.astype(BF16)


def kernel(x, rel_bias, even_w_in, even_conv_w, even_conv_b, even_conv_ln_g, even_conv_ln_b, even_w_out, even_ln1_g, even_ln1_b, even_ffn_wg, even_ffn_wu, even_ffn_wd, even_ln2_g, even_ln2_b, odd_w_in, odd_w_out, odd_ln1_g, odd_ln1_b, odd_router, odd_moe_wg, odd_moe_wu, odd_moe_wd, odd_ln2_g, odd_ln2_b):
    batch, seq, d = x.shape
    assert d == D_MODEL and seq % DIL_CHUNK == 0
    depth = even_w_in.shape[0] + odd_w_in.shape[0]
    assert depth == DEPTH
    dsa_bias = _dsa_bias_tiles(rel_bias, seq)
    dil_bias = _dil_bias_tiles(rel_bias)
    xf = x.reshape(batch * seq, d)
    xb = xf.astype(BF16)
    for layer in range(depth):
        i = layer // 2
        if layer % 2 == 0:
            xf, xb = _even_layer(xf, xb, batch, seq, dsa_bias, even_w_in[i], even_conv_w[i], even_conv_b[i],
                                 even_conv_ln_g[i], even_conv_ln_b[i], even_w_out[i], even_ln1_g[i],
                                 even_ln1_b[i], even_ffn_wg[i], even_ffn_wu[i], even_ffn_wd[i],
                                 even_ln2_g[i], even_ln2_b[i])
        else:
            xf, xb = _odd_layer(xf, xb, batch, seq, dil_bias, odd_w_in[i], odd_w_out[i], odd_ln1_g[i],
                                odd_ln1_b[i], odd_router[i], odd_moe_wg[i], odd_moe_wu[i], odd_moe_wd[i],
                                odd_ln2_g[i], odd_ln2_b[i])
    return xf.reshape(batch, seq, d)
```

```python
import functools
import math

import jax
import jax.numpy as jnp
from jax import lax
from jax.experimental import pallas as pl
from jax.experimental.pallas import tpu as pltpu

BF16 = jnp.bfloat16
F32 = jnp.float32

D_MODEL = 1024
CONV_CH = 512
CONV_WIDTH = 31
DSA_HEADS = 8
DSA_HEAD_DIM = 64
IDX_HEADS = 8
IDX_DIM = 64
DSA_TOPK_MAX = 256
Q_BLOCK = 128
DIL_GROUPS = ((128, 1), (512, 4), (2048, 16))
DIL_HEADS = 8
DIL_HEAD_DIM = 128
BAND_BLOCK = 128
NUM_BUCKETS = 32
MAX_DISTANCE = 2048
N_EXPERTS = 8
LN_EPS = 1e-5
NEG = -1e30
DEPTH = 2
DEEPNORM_ALPHA = (2 * DEPTH) ** 0.25

LANES = 128
EVEN_IN_PAD = 3200
INT_MIN = -(2 ** 31)
DIL_CHUNK = BAND_BLOCK * max(d for _, d in DIL_GROUPS)

_PARALLEL2 = pltpu.CompilerParams(dimension_semantics=("parallel", "parallel"))


def _layer_norm(y, g, b):
    mu = jnp.mean(y, axis=-1, keepdims=True)
    yc = y - mu
    var = jnp.mean(yc * yc, axis=-1, keepdims=True)
    return yc * lax.rsqrt(var + LN_EPS) * g + b


def _sigmoid(x):
    return 1.0 / (1.0 + jnp.exp(-x))


def _dot_nt(a, b):
    return lax.dot_general(a, b, (((1,), (1,)), ((), ())), preferred_element_type=F32)


def _dot(a, b):
    return jnp.dot(a, b, preferred_element_type=F32)


def _mm_kernel(x_ref, w_ref, o_ref):
    o_ref[...] = _dot(x_ref[...], w_ref[...]).astype(o_ref.dtype)


def _matmul(x, w, tm, tn, out_dtype):
    t, k = x.shape
    n = w.shape[1]
    return pl.pallas_call(
        _mm_kernel,
        grid=(t // tm, n // tn),
        in_specs=[pl.BlockSpec((tm, k), lambda i, j: (i, 0)),
                  pl.BlockSpec((k, tn), lambda i, j: (0, j))],
        out_specs=pl.BlockSpec((tm, tn), lambda i, j: (i, j)),
        out_shape=jax.ShapeDtypeStruct((t, n), out_dtype),
        compiler_params=_PARALLEL2,
        name="matmul",
    )(x, w)


def _mm_perm_kernel(x_ref, w_ref, o_ref, acc_ref, *, dil, tm, tn):
    acc = _dot(x_ref[...], w_ref[...])
    rows = tm // dil
    for c in range(tn // LANES):
        acc_ref[c] = acc[:, c * LANES:(c + 1) * LANES]
    for r in range(dil):
        for c in range(tn // LANES):
            o_ref[0, r, :, c * LANES:(c + 1) * LANES] = (
                acc_ref[c, pl.ds(r, rows, stride=dil), :].astype(o_ref.dtype))


def _matmul_residue_major(x, w, batch, seq, dil, tm, tn):
    t, k = x.shape
    n = w.shape[1]
    if dil == 1:
        return _matmul(x, w, tm, tn, BF16).reshape(batch, seq, n)
    tiles_per_seq = seq // tm
    out = pl.pallas_call(
        functools.partial(_mm_perm_kernel, dil=dil, tm=tm, tn=tn),
        grid=(t // tm, n // tn),
        in_specs=[pl.BlockSpec((tm, k), lambda i, j: (i, 0)),
                  pl.BlockSpec((k, tn), lambda i, j: (0, j))],
        out_specs=pl.BlockSpec((1, dil, tm // dil, tn),
                               lambda i, j: (i // tiles_per_seq, 0, i % tiles_per_seq, j)),
        out_shape=jax.ShapeDtypeStruct((batch, dil, seq // dil, n), BF16),
        scratch_shapes=[pltpu.VMEM((tn // LANES, tm, LANES), F32)],
        compiler_params=_PARALLEL2,
        name="matmul_residue_major",
    )(x, w)
    return out.reshape(batch, seq, n)


_CONV_HALO = 32
_CONV_ROWS = 64


def _conv_kernel(val_ref, gate_ref, w_ref, cb_ref, g_ref, b_ref, o_ref, ext_ref, *, ts):
    s = pl.program_id(1)

    @pl.when(s == 0)
    def _():
        ext_ref[0:_CONV_HALO, :] = jnp.zeros((_CONV_HALO, CONV_CH), F32)

    val = val_ref[0].astype(F32)
    gate = gate_ref[0].astype(F32)
    ext_ref[_CONV_HALO:_CONV_HALO + ts, :] = val * _sigmoid(gate)
    first_tap = _CONV_HALO - (CONV_WIDTH - 1)
    for rc in range(ts // _CONV_ROWS):
        r0 = rc * _CONV_ROWS
        acc = jnp.zeros((_CONV_ROWS, CONV_CH), F32)
        for j in range(CONV_WIDTH):
            lo = r0 + first_tap + j
            acc = acc + ext_ref[lo:lo + _CONV_ROWS, :] * w_ref[j:j + 1, :]
        y = _layer_norm(acc + cb_ref[...], g_ref[...], b_ref[...])
        o_ref[0, r0:r0 + _CONV_ROWS, :] = (y * _sigmoid(y)).astype(o_ref.dtype)
    ext_ref[0:_CONV_HALO, :] = ext_ref[ts:ts + _CONV_HALO, :]


def _conformer_conv(h, conv_w, conv_b, ln_g, ln_b, ts=512):
    b, s, _ = h.shape
    w_pad = jnp.zeros((32, CONV_CH), F32).at[:CONV_WIDTH].set(conv_w)
    vec = lambda v: v.reshape(1, CONV_CH).astype(F32)
    full = lambda shape: pl.BlockSpec(shape, lambda bi, si: (0, 0))
    return pl.pallas_call(
        functools.partial(_conv_kernel, ts=ts),
        grid=(b, s // ts),
        in_specs=[pl.BlockSpec((1, ts, CONV_CH), lambda bi, si: (bi, si, 0)),
                  pl.BlockSpec((1, ts, CONV_CH), lambda bi, si: (bi, si, 1)),
                  full((32, CONV_CH)), full((1, CONV_CH)), full((1, CONV_CH)), full((1, CONV_CH))],
        out_specs=pl.BlockSpec((1, ts, CONV_CH), lambda bi, si: (bi, si, 0)),
        out_shape=jax.ShapeDtypeStruct((b, s, CONV_CH), BF16),
        scratch_shapes=[pltpu.VMEM((ts + _CONV_HALO, CONV_CH), F32)],
        compiler_params=pltpu.CompilerParams(dimension_semantics=("parallel", "arbitrary")),
        name="conformer_conv",
    )(h, h, w_pad, vec(conv_b), vec(ln_g), vec(ln_b))


def _rel_bucket(dist):
    max_exact = NUM_BUCKETS // 2
    n = dist.astype(jnp.int32)
    nf = jnp.maximum(n, 1).astype(F32)
    large = max_exact + (jnp.log(nf / max_exact) / math.log(MAX_DISTANCE / max_exact)
                         * (NUM_BUCKETS - max_exact)).astype(jnp.int32)
    large = jnp.minimum(large, NUM_BUCKETS - 1)
    return jnp.where(n < max_exact, n, large)


def _bias_by_distance(rel_bias, n):
    return rel_bias[_rel_bucket(jnp.arange(n, dtype=jnp.int32))].astype(F32).T


def _toeplitz_kernel(win_ref, o_ref):
    n_win, span = win_ref.shape
    row = lax.broadcasted_iota(jnp.int32, (LANES, span), 0)
    for n in range(n_win):
        y = jnp.broadcast_to(win_ref[n:n + 1, :], (LANES, span))
        for bit in range(LANES.bit_length() - 1):
            y = jnp.where((row >> bit) & 1 == 1, pltpu.roll(y, 1 << bit, 1), y)
        o_ref[n] = y[:, LANES:]


def _toeplitz_tiles(windows, per_step=8):
    n, span = windows.shape
    width = span - LANES
    return pl.pallas_call(
        _toeplitz_kernel,
        grid=(n // per_step,),
        in_specs=[pl.BlockSpec((per_step, span), lambda i: (i, 0))],
        out_specs=pl.BlockSpec((per_step, LANES, width), lambda i: (i, 0, 0)),
        out_shape=jax.ShapeDtypeStruct((n, LANES, width), F32),
        compiler_params=pltpu.CompilerParams(dimension_semantics=("parallel",)),
        name="bias_toeplitz",
    )(windows)


def _dsa_bias_tiles(rel_bias, seq):
    nblk = seq // Q_BLOCK
    heads = rel_bias.shape[1]
    f = _bias_by_distance(rel_bias, seq)
    u = jnp.concatenate([jnp.broadcast_to(f[:, :1], (heads, 2 * Q_BLOCK)), f], axis=1)
    blocks = u.reshape(heads, nblk + 2, Q_BLOCK)
    windows = jnp.concatenate([blocks[:, :-1], blocks[:, 1:]], axis=-1)
    windows = windows.transpose(1, 0, 2).reshape((nblk + 1) * heads, 2 * Q_BLOCK)
    return _toeplitz_tiles(windows).reshape(nblk + 1, heads, Q_BLOCK, Q_BLOCK)


def _dil_bias_tiles(rel_bias):
    max_dil = max(d for _, d in DIL_GROUPS)
    heads = rel_bias.shape[1]
    span = 2 * BAND_BLOCK
    f = _bias_by_distance(rel_bias, span * max_dil + 1)
    windows = []
    for _, dil in DIL_GROUPS:
        fd = f[:, 0:span * dil + 1:dil]
        tail = jnp.broadcast_to(fd[:, :1], (heads, BAND_BLOCK - 1))
        windows.append(jnp.concatenate([fd[:, ::-1], tail], axis=1))
    tiles = _toeplitz_tiles(jnp.concatenate(windows, axis=0))
    return tiles.reshape(len(DIL_GROUPS), heads, BAND_BLOCK, span)


_KC = 2 * Q_BLOCK
_SCORE_HEADS = 2


def _sortable_key(score):
    score = jnp.where(score == 0.0, 0.0, score)
    bits = lax.bitcast_convert_type(score, jnp.int32)
    return jnp.where(bits < 0, bits ^ jnp.int32(0x7FFFFFFF), bits)


def _rows_reduce(x, op):
    rows = x.shape[0]
    return op(op(x.reshape(rows // 8, 8, LANES), axis=0), axis=0, keepdims=True)


def _keys_reduce(x, op):
    h, keys, _ = x.shape
    return op(op(x.reshape(h, keys // 8, 8, LANES), axis=1), axis=1, keepdims=True)


def _dsa_t_kernel(q_ref, qi_ref, wq_ref, k_ref, vt_ref, ki_ref, bias_hbm, o_ref,
                  bias_ref, keys_ref, half_ref, qm_ref, acc_ref, sem, *, topk):
    b = pl.program_id(0)
    i = pl.program_id(1)

    @pl.when((b == 0) & (i == 0))
    def _():
        cp = pltpu.make_async_copy(bias_hbm, bias_ref, sem)
        cp.start()
        cp.wait()

    nch = (i + 2) // 2
    key_off = lax.broadcasted_iota(jnp.int32, (_KC, LANES), 0)
    q_pos = i * Q_BLOCK + lax.broadcasted_iota(jnp.int32, (_KC, LANES), 1)
    lane = lax.broadcasted_iota(jnp.int32, (Q_BLOCK, LANES), 1)
    low_half = lane < DSA_HEAD_DIM

    w_t = (wq_ref[0].astype(F32) * (IDX_HEADS ** -0.5 * IDX_DIM ** -0.5)).T
    qi = qi_ref[0].astype(F32)
    qi_heads = []
    for p in range(IDX_HEADS // 2):
        pair = qi[:, p * LANES:(p + 1) * LANES]
        qi_heads.append(jnp.where(low_half, pair, 0.0))
        qi_heads.append(jnp.where(low_half, pltpu.roll(pair, IDX_DIM, 1), 0.0))
    qi_stack = jnp.concatenate(qi_heads, axis=0).astype(BF16)
    q_scaled = (q_ref[0].astype(F32) * (DSA_HEAD_DIM ** -0.5)).astype(BF16)
    for p in range(DSA_HEADS // 2):
        qp = q_scaled[:, p * LANES:(p + 1) * LANES]
        qm_ref[2 * p] = jnp.where(low_half, qp, jnp.zeros_like(qp))
        qm_ref[2 * p + 1] = jnp.where(low_half, jnp.zeros_like(qp), qp)

    def score_body(c, carry):
        k0 = pl.multiple_of(c * _KC, _KC)
        ki = ki_ref[0, pl.ds(k0, _KC), :]
        score = jnp.zeros((_KC, LANES), F32)
        for h0 in range(0, IDX_HEADS, _SCORE_HEADS):
            dots = _dot_nt(ki, qi_stack[h0 * Q_BLOCK:(h0 + _SCORE_HEADS) * Q_BLOCK])
            for j in range(_SCORE_HEADS):
                w_row = w_t[IDX_DIM + h0 + j:IDX_DIM + h0 + j + 1, :]
                score = score + w_row * jnp.maximum(dots[:, j * Q_BLOCK:(j + 1) * Q_BLOCK], 0.0)
        key = jnp.where(k0 + key_off <= q_pos, _sortable_key(score), jnp.int32(INT_MIN))
        keys_ref[c] = key
        half_ref[c] = (key >> 16).astype(jnp.int16)
        return carry

    lax.fori_loop(0, nch, score_body, 0)

    chains = 4

    def count_ge16(cand):
        cb = jnp.broadcast_to(cand, (16, LANES)).astype(jnp.int16)

        def body(c, acc):
            kk = half_ref[c].reshape(_KC // 16, 16, LANES)
            ones = jnp.where(kk >= cb, jnp.int16(1), jnp.int16(0)).reshape(_KC, LANES)
            words = pltpu.bitcast(ones, jnp.int32)
            return acc + jnp.sum(words.reshape(_KC // (16 * chains), chains, 8, LANES), axis=0)

        acc = lax.fori_loop(0, nch, body, jnp.zeros((chains, 8, LANES), jnp.int32))
        tot = jnp.sum(acc, axis=0)
        return jnp.sum(((tot & 0xFFFF) + (tot >> 16)).astype(F32), axis=0, keepdims=True)

    def search16(k_needed):
        lo = jnp.full((1, LANES), -(2 ** 15), jnp.int32)
        zero = jnp.zeros((1, LANES), jnp.int32)
        thr0 = jnp.where(count_ge16(zero) >= k_needed, zero, lo)

        def bit_body(t, thr):
            cand = thr | jnp.left_shift(jnp.int32(1), 14 - t)
            return jnp.where(count_ge16(cand) >= k_needed, cand, thr)

        return lax.fori_loop(0, 15, bit_body, thr0)

    k_top = jnp.full((1, LANES), topk, F32)
    thr_hi = search16(k_top)
    above = count_ge16(jnp.minimum(thr_hi + 1, 2 ** 15 - 1))
    above = jnp.where(thr_hi == 2 ** 15 - 1, 0, above)

    def low_body(c, carry):
        key = keys_ref[c]
        low = (key & 0xFFFF) - 2 ** 15
        half_ref[c] = jnp.where((key >> 16) == thr_hi, low, -(2 ** 15)).astype(jnp.int16)
        return carry

    lax.fori_loop(0, nch, low_body, 0)
    thr_lo = search16(k_top - above)
    thr = (thr_hi << 16) | (thr_lo + 2 ** 15)

    def count_gt(carry_thr):
        cb = jnp.broadcast_to(carry_thr, (8, LANES))

        def body(c, acc):
            kk = keys_ref[c].reshape(_KC // (8 * chains), chains, 8, LANES)
            return acc + jnp.sum(jnp.where(kk > cb, 1.0, 0.0), axis=0)

        acc = lax.fori_loop(0, nch, body, jnp.zeros((chains, 8, LANES), F32))
        return jnp.sum(jnp.sum(acc, axis=0), axis=0, keepdims=True)

    need = float(topk) - count_gt(thr)
    tri = (lax.broadcasted_iota(jnp.int32, (_KC, _KC), 1)
           <= lax.broadcasted_iota(jnp.int32, (_KC, _KC), 0)).astype(BF16)

    acc_ref[...] = jnp.zeros_like(acc_ref)

    def attn_body(c, carry):
        ms, ls, tie_carry = carry
        k0 = pl.multiple_of(c * _KC, _KC)
        kk = keys_ref[c]
        causal = kk > jnp.int32(INT_MIN)
        eq = (kk == thr) & causal
        eq_f = jnp.where(eq, 1.0, 0.0)
        rank = _dot(tri, eq_f.astype(BF16)) + tie_carry
        sel = ((kk > thr) & causal) | (eq & (rank <= need))
        tie_carry = tie_carry + _rows_reduce(eq_f, jnp.sum)
        d = i - 2 * c
        kc = k_ref[0, pl.ds(k0, _KC), :]
        k_heads = jnp.stack([kc[:, (h // 2) * LANES:(h // 2 + 1) * LANES] for h in range(DSA_HEADS)])
        s = jnp.einsum("hkd,hqd->hkq", k_heads, qm_ref[...], preferred_element_type=F32)
        bias = jnp.concatenate([bias_ref[d + 1], bias_ref[d]], axis=1)
        s = jnp.where(sel, s + bias, NEG)
        m_new = jnp.maximum(ms, _keys_reduce(s, jnp.max))
        alpha = jnp.exp(ms - m_new)
        pr = jnp.exp(s - m_new)
        ls = alpha * ls + _keys_reduce(pr, jnp.sum)
        vt = vt_ref[0, c].reshape(DSA_HEADS, DSA_HEAD_DIM, _KC)
        pv = jnp.einsum("hdk,hkq->hdq", vt, pr.astype(BF16), preferred_element_type=F32)
        acc_ref[...] = alpha * acc_ref[...] + pv
        return m_new, ls, tie_carry

    init = (jnp.full((DSA_HEADS, 1, LANES), NEG, F32), jnp.zeros((DSA_HEADS, 1, LANES), F32),
            jnp.zeros((1, LANES), F32))
    _, ls, _ = lax.fori_loop(0, nch, attn_body, init)
    out_t = (acc_ref[...] / ls).reshape(DSA_HEADS * DSA_HEAD_DIM, Q_BLOCK)
    o_ref[0] = out_t.T.astype(o_ref.dtype)


def _dsa_attention(h, bias_tiles):
    b, s, _ = h.shape
    width = DSA_HEADS * DSA_HEAD_DIM
    nblk = s // Q_BLOCK
    topk = min(DSA_TOPK_MAX, s // 4)
    kw_block = (CONV_CH * 2 + 4 * width) // LANES
    v_cols = 2 * CONV_CH + 2 * width
    vt = h[:, :, v_cols:v_cols + width].reshape(b, s // _KC, _KC, width).transpose(0, 1, 3, 2)
    return pl.pallas_call(
        functools.partial(_dsa_t_kernel, topk=topk),
        grid=(b, nblk),
        in_specs=[pl.BlockSpec((1, Q_BLOCK, width), lambda bi, qi: (bi, qi, 2)),
                  pl.BlockSpec((1, Q_BLOCK, width), lambda bi, qi: (bi, qi, 5)),
                  pl.BlockSpec((1, Q_BLOCK, LANES), lambda bi, qi: (bi, qi, kw_block)),
                  pl.BlockSpec((1, s, width), lambda bi, qi: (bi, 0, 3)),
                  pl.BlockSpec((1, s // _KC, width, _KC), lambda bi, qi: (bi, 0, 0, 0)),
                  pl.BlockSpec((1, s, LANES), lambda bi, qi: (bi, 0, kw_block)),
                  pl.BlockSpec(memory_space=pl.ANY)],
        out_specs=pl.BlockSpec((1, Q_BLOCK, width), lambda bi, qi: (bi, qi, 0)),
        out_shape=jax.ShapeDtypeStruct((b, s, width), BF16),
        scratch_shapes=[pltpu.VMEM((nblk + 1, DSA_HEADS, Q_BLOCK, Q_BLOCK), F32),
                        pltpu.VMEM((s // _KC, _KC, Q_BLOCK), jnp.int32),
                        pltpu.VMEM((s // _KC, _KC, Q_BLOCK), jnp.int16),
                        pltpu.VMEM((DSA_HEADS, Q_BLOCK, LANES), BF16),
                        pltpu.VMEM((DSA_HEADS, DSA_HEAD_DIM, Q_BLOCK), F32),
                        pltpu.SemaphoreType.DMA(())],
        compiler_params=pltpu.CompilerParams(dimension_semantics=("arbitrary", "arbitrary")),
        name="dsa_attention",
    )(h, h, h, h, vt, h, bias_tiles)


_DIL_WAVE = 8


def _dil_kernel(*refs, seq):
    n_g = len(DIL_GROUPS)
    qkv = refs[:3 * n_g]
    bias_ref, o_ref, og_ref, lg_ref = refs[3 * n_g:]
    c = pl.program_id(2)
    ii = lax.broadcasted_iota(jnp.int32, (BAND_BLOCK, BAND_BLOCK), 0)
    jj = lax.broadcasted_iota(jnp.int32, (BAND_BLOCK, BAND_BLOCK), 1)
    valid_prev = jj >= ii
    valid_cur = ii >= jj
    scale = DIL_HEAD_DIM ** -0.5
    for g, (window, dil) in enumerate(DIL_GROUPS):
        assert window // dil == BAND_BLOCK
        q_ref, k_ref, v_ref = qkv[3 * g:3 * g + 3]
        sub_len = seq // dil
        blocks_per_chunk = DIL_CHUNK // (BAND_BLOCK * dil)
        bias_prev = bias_ref[g, 0, :, :BAND_BLOCK]
        bias_cur = bias_ref[g, 0, :, BAND_BLOCK:]
        tiles = [(r, nl) for r in range(dil) for nl in range(blocks_per_chunk)]
        for w0 in range(0, len(tiles), _DIL_WAVE):
            wave = tiles[w0:w0 + _DIL_WAVE]
            cur, prev, mask_prev = [], [], []
            for r, nl in wave:
                row0 = pl.multiple_of(r * sub_len + c * (DIL_CHUNK // dil) + nl * BAND_BLOCK, BAND_BLOCK)
                cur.append(pl.ds(row0, BAND_BLOCK))
                prev.append(pl.ds(pl.multiple_of(jnp.maximum(row0 - BAND_BLOCK, 0), BAND_BLOCK), BAND_BLOCK))
                mask_prev.append(valid_prev if nl > 0 else (valid_prev & (c > 0)))
            gather = lambda ref, slices: jnp.stack([ref[0, sl, :] for sl in slices])
            qb = gather(q_ref, cur)
            s_cur = jnp.einsum("wqd,wkd->wqk", qb, gather(k_ref, cur), preferred_element_type=F32)
            s_prev = jnp.einsum("wqd,wkd->wqk", qb, gather(k_ref, prev), preferred_element_type=F32)
            s_cur = jnp.where(valid_cur, s_cur * scale + bias_cur, NEG)
            s_prev = jnp.where(jnp.stack(mask_prev), s_prev * scale + bias_prev, NEG)
            m = jnp.max(jnp.maximum(s_cur, s_prev), axis=2, keepdims=True)
            p_cur = jnp.exp(s_cur - m)
            p_prev = jnp.exp(s_prev - m)
            ssum = jnp.sum(p_cur + p_prev, axis=2, keepdims=True)
            o = (jnp.einsum("wqk,wkd->wqd", p_cur.astype(BF16), gather(v_ref, cur), preferred_element_type=F32)
                 + jnp.einsum("wqk,wkd->wqd", p_prev.astype(BF16), gather(v_ref, prev),
                              preferred_element_type=F32)) / ssum
            lse = m + jnp.log(ssum)
            for w, (r, nl) in enumerate(wave):
                dst = pl.ds(r + nl * BAND_BLOCK * dil, BAND_BLOCK, stride=dil) if dil > 1 else pl.ds(
                    nl * BAND_BLOCK, BAND_BLOCK)
                og_ref[g, dst, :] = o[w]
                lg_ref[g, dst, :] = jnp.broadcast_to(lse[w], (BAND_BLOCK, LANES))
    rows = 256
    for rc in range(DIL_CHUNK // rows):
        sl = slice(rc * rows, (rc + 1) * rows)
        lses = [lg_ref[g, sl, :] for g in range(n_g)]
        mx = functools.reduce(jnp.maximum, lses)
        ws = [jnp.exp(l - mx) for l in lses]
        den = functools.reduce(lambda a, b_: a + b_, ws)
        num = functools.reduce(lambda a, b_: a + b_, [ws[g] * og_ref[g, sl, :] for g in range(n_g)])
        o_ref[0, sl, :] = (num / den).astype(o_ref.dtype)


def _dilated_attention(hs, bias_tiles):
    b, s, _ = hs[0].shape
    in_specs = []
    args = []
    for hg in hs:
        for part in range(3):
            in_specs.append(pl.BlockSpec((1, s, DIL_HEAD_DIM),
                                         lambda bi, hi, ci, part=part: (bi, 0, part * DIL_HEADS + hi)))
            args.append(hg)
    in_specs.append(pl.BlockSpec((len(DIL_GROUPS), 1, BAND_BLOCK, 2 * BAND_BLOCK),
                                 lambda bi, hi, ci: (0, hi, 0, 0)))
    return pl.pallas_call(
        functools.partial(_dil_kernel, seq=s),
        grid=(b, DIL_HEADS, s // DIL_CHUNK),
        in_specs=in_specs,
        out_specs=pl.BlockSpec((1, DIL_CHUNK, DIL_HEAD_DIM), lambda bi, hi, ci: (bi, ci, hi)),
        out_shape=jax.ShapeDtypeStruct((b, s, DIL_HEADS * DIL_HEAD_DIM), BF16),
        scratch_shapes=[pltpu.VMEM((len(DIL_GROUPS), DIL_CHUNK, LANES), F32),
                        pltpu.VMEM((len(DIL_GROUPS), DIL_CHUNK, LANES), F32)],
        compiler_params=pltpu.CompilerParams(dimension_semantics=("parallel", "parallel", "arbitrary")),
        name="dilated_attention",
    )(*args, bias_tiles)


def _outproj_kernel(*refs, n_in, with_router):
    ins = refs[:n_in]
    ws = refs[n_in:2 * n_in]
    x_ref, g_ref, b_ref = refs[2 * n_in:2 * n_in + 3]
    rest = refs[2 * n_in + 3:]
    mix = _dot(ins[0][...], ws[0][...])
    for a_ref, w_ref in zip(ins[1:], ws[1:]):
        mix = mix + _dot(a_ref[...], w_ref[...])
    xn = _layer_norm(DEEPNORM_ALPHA * x_ref[...] + mix, g_ref[...], b_ref[...])
    if not with_router:
        o_ref, ob_ref = rest
    else:
        router_ref, o_ref, ob_ref, gates_ref, counts_ref, run_ref, tri_ref = rest
        tm = xn.shape[0]

        @pl.when(pl.program_id(0) == 0)
        def _():
            run_ref[...] = jnp.zeros_like(run_ref)
            tri_ref[...] = (lax.broadcasted_iota(jnp.int32, (tm, tm), 0)
                            > lax.broadcasted_iota(jnp.int32, (tm, tm), 1)).astype(BF16)

        logits = jnp.dot(xn, router_ref[...], preferred_element_type=F32, precision=lax.Precision.HIGHEST)
        lane = lax.broadcasted_iota(jnp.int32, logits.shape, 1).astype(F32)
        lg = jnp.where(lane < N_EXPERTS, logits, -jnp.inf)
        m1 = jnp.max(lg, axis=1, keepdims=True)
        i1 = jnp.min(jnp.where(lg == m1, lane, float(LANES)), axis=1, keepdims=True)
        lg2 = jnp.where(lane == i1, -jnp.inf, lg)
        m2 = jnp.max(lg2, axis=1, keepdims=True)
        i2 = jnp.min(jnp.where(lg2 == m2, lane, float(LANES)), axis=1, keepdims=True)
        e2 = jnp.exp(m2 - m1)
        den = 1.0 + e2
        oh1 = jnp.where(lane == i1, 1.0, 0.0)
        oh2 = jnp.where(lane == i2, 1.0, 0.0)
        both = oh1 + oh2
        before = _dot(tri_ref[...], both.astype(BF16)) + run_ref[...]
        r1 = jnp.sum(before * oh1, axis=1, keepdims=True)
        r2 = jnp.sum(before * oh2, axis=1, keepdims=True)
        run_ref[...] += jnp.sum(both, axis=0, keepdims=True)
        counts_ref[...] = run_ref[...]
        gates_ref[...] = (jnp.where(lane == 0.0, i1, 0.0) + jnp.where(lane == 1.0, i2, 0.0)
                          + jnp.where(lane == 2.0, 1.0 / den, 0.0) + jnp.where(lane == 3.0, e2 / den, 0.0)
                          + jnp.where(lane == 4.0, r1, 0.0) + jnp.where(lane == 5.0, r2, 0.0))
    o_ref[...] = xn
    ob_ref[...] = xn.astype(BF16)


def _outproj_ln(ins, ws, x, ln_g, ln_b, router=None, tm=512):
    t, d = x.shape
    n_in = len(ins)
    row = lambda width: pl.BlockSpec((tm, width), lambda i: (i, 0))
    full = lambda a: pl.BlockSpec(a.shape, lambda i: (0, 0))
    vec = lambda v: v.reshape(1, d).astype(F32)
    args = list(ins) + list(ws) + [x, vec(ln_g), vec(ln_b)]
    in_specs = [row(a.shape[1]) for a in ins] + [full(w) for w in ws] + [row(d), full(vec(ln_g)), full(vec(ln_b))]
    out_shape = [jax.ShapeDtypeStruct((t, d), F32), jax.ShapeDtypeStruct((t, d), BF16)]
    out_specs = [row(d), row(d)]
    if router is not None:
        router_pad = jnp.zeros((d, LANES), F32).at[:, :N_EXPERTS].set(router)
        args.append(router_pad)
        in_specs.append(full(router_pad))
        out_shape += [jax.ShapeDtypeStruct((t, LANES), F32), jax.ShapeDtypeStruct((1, LANES), F32)]
        out_specs += [row(LANES), pl.BlockSpec((1, LANES), lambda i: (0, 0))]
        scratch = [pltpu.VMEM((1, LANES), F32), pltpu.VMEM((tm, tm), BF16)]
    else:
        scratch = []
    semantics = "arbitrary" if router is not None else "parallel"
    return pl.pallas_call(
        functools.partial(_outproj_kernel, n_in=n_in, with_router=router is not None),
        grid=(t // tm,),
        in_specs=in_specs,
        out_specs=out_specs,
        out_shape=out_shape,
        scratch_shapes=scratch,
        compiler_params=pltpu.CompilerParams(dimension_semantics=(semantics,)),
        name="outproj_ln",
    )(*args)


def _ffn_kernel(xb_ref, x_ref, wg_ref, wu_ref, wd_ref, g_ref, b_ref, o_ref, ob_ref, acc_ref):
    f = pl.program_id(1)

    @pl.when(f == 0)
    def _():
        acc_ref[...] = jnp.zeros_like(acc_ref)

    xb = xb_ref[...]
    hg = _dot(xb, wg_ref[...])
    hu = _dot(xb, wu_ref[...])
    act = (hg * _sigmoid(hg) * hu).astype(BF16)
    acc_ref[...] += _dot(act, wd_ref[...])

    @pl.when(f == pl.num_programs(1) - 1)
    def _():
        xn = _layer_norm(DEEPNORM_ALPHA * x_ref[...] + acc_ref[...], g_ref[...], b_ref[...])
        o_ref[...] = xn
        ob_ref[...] = xn.astype(BF16)


def _ffn_ln(xb, x, wg, wu, wd, ln_g, ln_b, tm=512, tf=1408):
    t, d = x.shape
    ff = wg.shape[1]
    vec = lambda v: v.reshape(1, d).astype(F32)
    return pl.pallas_call(
        _ffn_kernel,
        grid=(t // tm, ff // tf),
        in_specs=[pl.BlockSpec((tm, d), lambda i, f: (i, 0)),
                  pl.BlockSpec((tm, d), lambda i, f: (i, 0)),
                  pl.BlockSpec((d, tf), lambda i, f: (0, f)),
                  pl.BlockSpec((d, tf), lambda i, f: (0, f)),
                  pl.BlockSpec((tf, d), lambda i, f: (f, 0)),
                  pl.BlockSpec((1, d), lambda i, f: (0, 0)),
                  pl.BlockSpec((1, d), lambda i, f: (0, 0))],
        out_specs=[pl.BlockSpec((tm, d), lambda i, f: (i, 0)),
                   pl.BlockSpec((tm, d), lambda i, f: (i, 0))],
        out_shape=[jax.ShapeDtypeStruct((t, d), F32), jax.ShapeDtypeStruct((t, d), BF16)],
        scratch_shapes=[pltpu.VMEM((tm, d), F32)],
        compiler_params=pltpu.CompilerParams(dimension_semantics=("parallel", "arbitrary")),
        name="ffn_ln",
    )(xb, x, wg, wu, wd, vec(ln_g), vec(ln_b))


_MOE_TM = 512


def _route(route, counts, n_tiles):
    expert = route[:, 0:2].astype(jnp.int32)
    rank = route[:, 4:6].astype(jnp.int32)
    counts = counts[0, :N_EXPERTS].astype(jnp.int32)
    padded = (counts + _MOE_TM - 1) // _MOE_TM * _MOE_TM
    ends = jnp.cumsum(padded)
    starts = ends - padded
    onehot = expert[:, :, None] == jnp.arange(N_EXPERTS, dtype=jnp.int32)
    pos = (rank + jnp.sum(jnp.where(onehot, starts, 0), axis=-1)).reshape(-1)
    tile_start = jnp.arange(n_tiles, dtype=jnp.int32) * _MOE_TM
    tile_expert = jnp.minimum(jnp.sum((tile_start[:, None] >= ends[None, :]).astype(jnp.int32), axis=1),
                              N_EXPERTS - 1)
    return pos.astype(jnp.int32), tile_expert.astype(jnp.int32)


def _row_scatter_kernel(pos_ref, x_ref, init_ref, o_ref, sem, *, tm):
    del init_ref

    def body(j, carry):
        for choice in range(2):
            p = pos_ref[0, 0, 2 * j + choice]
            pltpu.make_async_copy(x_ref.at[pl.ds(j, 1)], o_ref.at[pl.ds(p, 1)], sem).start()
        return carry

    lax.fori_loop(0, tm, body, 0, unroll=8)
    for _ in range(2):
        pltpu.make_async_copy(x_ref, o_ref.at[pl.ds(0, tm)], sem).wait()


def _row_scatter(x, pos, n_rows, tm=512):
    t, d = x.shape
    return pl.pallas_call(
        functools.partial(_row_scatter_kernel, tm=tm),
        grid=(t // tm,),
        in_specs=[pl.BlockSpec((1, 1, 2 * tm), lambda i: (i, 0, 0), memory_space=pltpu.SMEM),
                  pl.BlockSpec((tm, d), lambda i: (i, 0)),
                  pl.BlockSpec(memory_space=pl.ANY)],
        out_specs=pl.BlockSpec(memory_space=pl.ANY),
        out_shape=jax.ShapeDtypeStruct((n_rows, d), x.dtype),
        scratch_shapes=[pltpu.SemaphoreType.DMA(())],
        input_output_aliases={2: 0},
        compiler_params=pltpu.CompilerParams(dimension_semantics=("arbitrary",)),
        name="moe_row_scatter",
    )(pos.reshape(t // tm, 1, 2 * tm), x, jnp.zeros((n_rows, d), x.dtype))


def _expert_ffn_kernel(te_ref, x_ref, wg_ref, wu_ref, wd_ref, o_ref, acc_ref):
    del te_ref
    f = pl.program_id(1)

    @pl.when(f == 0)
    def _():
        acc_ref[...] = jnp.zeros_like(acc_ref)

    xb = x_ref[...].astype(BF16)
    hg = _dot(xb, wg_ref[0])
    hu = _dot(xb, wu_ref[0])
    act = (hg * _sigmoid(hg) * hu).astype(BF16)
    acc_ref[...] += _dot(act, wd_ref[0])

    @pl.when(f == pl.num_programs(1) - 1)
    def _():
        o_ref[...] = acc_ref[...]


def _expert_ffn(xs, tile_expert, wg, wu, wd, tf=896):
    n_rows, d = xs.shape
    ff = wg.shape[2]
    grid_spec = pltpu.PrefetchScalarGridSpec(
        num_scalar_prefetch=1,
        grid=(n_rows // _MOE_TM, ff // tf),
        in_specs=[pl.BlockSpec((_MOE_TM, d), lambda i, f, te: (i, 0)),
                  pl.BlockSpec((1, d, tf), lambda i, f, te: (te[i], 0, f)),
                  pl.BlockSpec((1, d, tf), lambda i, f, te: (te[i], 0, f)),
                  pl.BlockSpec((1, tf, d), lambda i, f, te: (te[i], f, 0))],
        out_specs=pl.BlockSpec((_MOE_TM, d), lambda i, f, te: (i, 0)),
        scratch_shapes=[pltpu.VMEM((_MOE_TM, d), F32)])
    return pl.pallas_call(
        _expert_ffn_kernel,
        grid_spec=grid_spec,
        out_shape=jax.ShapeDtypeStruct((n_rows, d), F32),
        compiler_params=pltpu.CompilerParams(dimension_semantics=("parallel", "arbitrary")),
        name="moe_expert_ffn",
    )(tile_expert, xs, wg, wu, wd)


def _combine_kernel(pos_ref, route_ref, x_ref, g_ref, b_ref, y_ref, o_ref, ybuf_ref, sem, *, tm):
    def body(j, carry):
        for choice in range(2):
            p = pos_ref[0, 0, 2 * j + choice]
            pltpu.make_async_copy(y_ref.at[pl.ds(p, 1)], ybuf_ref.at[choice, pl.ds(j, 1)], sem).start()
        return carry

    lax.fori_loop(0, tm, body, 0, unroll=8)
    for choice in range(2):
        pltpu.make_async_copy(y_ref.at[pl.ds(0, tm)], ybuf_ref.at[choice], sem).wait()
    route = route_ref[...]
    y = route[:, 2:3] * ybuf_ref[0] + route[:, 3:4] * ybuf_ref[1]
    o_ref[...] = _layer_norm(DEEPNORM_ALPHA * x_ref[...] + y, g_ref[...], b_ref[...])


def _combine_ln(ys, pos, route, x, ln_g, ln_b, tm=512):
    t, d = x.shape
    vec = lambda v: v.reshape(1, d).astype(F32)
    return pl.pallas_call(
        functools.partial(_combine_kernel, tm=tm),
        grid=(t // tm,),
        in_specs=[pl.BlockSpec((1, 1, 2 * tm), lambda i: (i, 0, 0), memory_space=pltpu.SMEM),
                  pl.BlockSpec((tm, LANES), lambda i: (i, 0)),
                  pl.BlockSpec((tm, d), lambda i: (i, 0)),
                  pl.BlockSpec((1, d), lambda i: (0, 0)),
                  pl.BlockSpec((1, d), lambda i: (0, 0)),
                  pl.BlockSpec(memory_space=pl.ANY)],
        out_specs=pl.BlockSpec((tm, d), lambda i: (i, 0)),
        out_shape=jax.ShapeDtypeStruct((t, d), F32),
        scratch_shapes=[pltpu.VMEM((2, tm, d), F32), pltpu.SemaphoreType.DMA(())],
        compiler_params=pltpu.CompilerParams(dimension_semantics=("arbitrary",)),
        name="moe_combine_ln",
    )(pos.reshape(t // tm, 1, 2 * tm), route, x, vec(ln_g), vec(ln_b), ys)


def _moe_ln(x, route, counts, wg, wu, wd, ln_g, ln_b):
    t, _ = x.shape
    n_tiles = 2 * t // _MOE_TM + N_EXPERTS
    pos, tile_expert = _route(route, counts, n_tiles)
    xs = _row_scatter(x, pos, n_tiles * _MOE_TM)
    ys = _expert_ffn(xs, tile_expert, wg, wu, wd)
    return _combine_ln(ys, pos, route, x, ln_g, ln_b)


def _even_layer(x, xb, batch, seq, dsa_bias, w_in, conv_w, conv_b, conv_ln_g, conv_ln_b, w_out,
                ln1_g, ln1_b, ffn_wg, ffn_wu, ffn_wd, ln2_g, ln2_b):
    w_in_p = jnp.zeros((D_MODEL, EVEN_IN_PAD), BF16).at[:, :w_in.shape[1]].set(w_in.astype(BF16))
    h = _matmul(xb, w_in_p, 1024, 640, BF16).reshape(batch, seq, EVEN_IN_PAD)
    a_out = _conformer_conv(h, conv_w, conv_b, conv_ln_g, conv_ln_b)
    att = _dsa_attention(h, dsa_bias)
    w_out_b = w_out.astype(BF16)
    x1, x1b = _outproj_ln([a_out.reshape(-1, CONV_CH), att.reshape(-1, DSA_HEADS * DSA_HEAD_DIM)],
                          [w_out_b[:CONV_CH], w_out_b[CONV_CH:]], x, ln1_g, ln1_b)
    return _ffn_ln(x1b, x1, ffn_wg.astype(BF16), ffn_wu.astype(BF16), ffn_wd.astype(BF16), ln2_g, ln2_b)


def _odd_layer(x, xb, batch, seq, dil_bias, w_in, w_out, ln1_g, ln1_b, router, moe_wg, moe_wu, moe_wd,
               ln2_g, ln2_b):
    gw = 3 * DIL_HEADS * DIL_HEAD_DIM
    w_in_b = w_in.astype(BF16)
    hs = [_matmul_residue_major(xb, w_in_b[:, g * gw:(g + 1) * gw], batch, seq, dil, 1024, 768)
          for g, (_, dil) in enumerate(DIL_GROUPS)]
    o = _dilated_attention(hs, dil_bias).reshape(-1, DIL_HEADS * DIL_HEAD_DIM)
    x1, _, route, counts = _outproj_ln([o], [w_out.astype(BF16)], x, ln1_g, ln1_b, router=router)
    out = _moe_ln(x1, route, counts, moe_wg.astype(BF16), moe_wu.astype(BF16), moe_wd.astype(BF16), ln2_g, ln2_b)
    return out, out.astype(BF16)


def kernel(x, rel_bias, even_w_in, even_conv_w, even_conv_b, even_conv_ln_g, even_conv_ln_b, even_w_out, even_ln1_g, even_ln1_b, even_ffn_wg, even_ffn_wu, even_ffn_wd, even_ln2_g, even_ln2_b, odd_w_in, odd_w_out, odd_ln1_g, odd_ln1_b, odd_router, odd_moe_wg, odd_moe_wu, odd_moe_wd, odd_ln2_g, odd_ln2_b):
    batch, seq, d = x.shape
    assert d == D_MODEL and seq % DIL_CHUNK == 0
    depth = even_w_in.shape[0] + odd_w_in.shape[0]
    assert depth == DEPTH
    dsa_bias = _dsa_bias_tiles(rel_bias, seq)
    dil_bias = _dil_bias_tiles(rel_bias)
    xf = x.reshape(batch * seq, d)
    xb = xf.astype(BF16)
    for layer in range(depth):
        i = layer // 2
        if layer % 2 == 0:
            xf, xb = _even_layer(xf, xb, batch, seq, dsa_bias, even_w_in[i], even_conv_w[i], even_conv_b[i],
                                 even_conv_ln_g[i], even_conv_ln_b[i], even_w_out[i], even_ln1_g[i],
                                 even_ln1_b[i], even_ffn_wg[i], even_ffn_wu[i], even_ffn_wd[i],
                                 even_ln2_g[i], even_ln2_b[i])
        else:
            xf, xb = _odd_layer(xf, xb, batch, seq, dil_bias, odd_w_in[i], odd_w_out[i], odd_ln1_g[i],
                                odd_ln1_b[i], odd_router[i], odd_moe_wg[i], odd_moe_wu[i], odd_moe_wd[i],
                                odd_ln2_g[i], odd_ln2_b[i])
    return xf.reshape(batch, seq, d)
```

```python
import functools
import math

import jax
import jax.numpy as jnp
from jax import lax
from jax.experimental import pallas as pl
from jax.experimental.pallas import tpu as pltpu

BF16 = jnp.bfloat16
F32 = jnp.float32

D_MODEL = 1024
CONV_CH = 512
CONV_WIDTH = 31
DSA_HEADS = 8
DSA_HEAD_DIM = 64
IDX_HEADS = 8
IDX_DIM = 64
DSA_TOPK_MAX = 256
Q_BLOCK = 128
DIL_GROUPS = ((128, 1), (512, 4), (2048, 16))
DIL_HEADS = 8
DIL_HEAD_DIM = 128
BAND_BLOCK = 128
NUM_BUCKETS = 32
MAX_DISTANCE = 2048
N_EXPERTS = 8
LN_EPS = 1e-5
NEG = -1e30
DEPTH = 2
DEEPNORM_ALPHA = (2 * DEPTH) ** 0.25

LANES = 128
EVEN_IN_PAD = 3200
INT_MIN = -(2 ** 31)
DIL_CHUNK = BAND_BLOCK * max(d for _, d in DIL_GROUPS)

_PARALLEL2 = pltpu.CompilerParams(dimension_semantics=("parallel", "parallel"))


def _layer_norm(y, g, b):
    mu = jnp.mean(y, axis=-1, keepdims=True)
    yc = y - mu
    var = jnp.mean(yc * yc, axis=-1, keepdims=True)
    return yc * lax.rsqrt(var + LN_EPS) * g + b


def _sigmoid(x):
    return 1.0 / (1.0 + jnp.exp(-x))


def _dot_nt(a, b):
    return lax.dot_general(a, b, (((1,), (1,)), ((), ())), preferred_element_type=F32)


def _dot(a, b):
    return jnp.dot(a, b, preferred_element_type=F32)


def _mm_kernel(x_ref, w_ref, o_ref):
    o_ref[...] = _dot(x_ref[...], w_ref[...]).astype(o_ref.dtype)


def _matmul(x, w, tm, tn, out_dtype):
    t, k = x.shape
    n = w.shape[1]
    return pl.pallas_call(
        _mm_kernel,
        grid=(t // tm, n // tn),
        in_specs=[pl.BlockSpec((tm, k), lambda i, j: (i, 0)),
                  pl.BlockSpec((k, tn), lambda i, j: (0, j))],
        out_specs=pl.BlockSpec((tm, tn), lambda i, j: (i, j)),
        out_shape=jax.ShapeDtypeStruct((t, n), out_dtype),
        compiler_params=_PARALLEL2,
        name="matmul",
    )(x, w)


def _mm_perm_kernel(x_ref, w_ref, o_ref, acc_ref, *, dil, tm, tn):
    acc = _dot(x_ref[...], w_ref[...])
    rows = tm // dil
    for c in range(tn // LANES):
        acc_ref[c] = acc[:, c * LANES:(c + 1) * LANES]
    for r in range(dil):
        for c in range(tn // LANES):
            o_ref[0, r, :, c * LANES:(c + 1) * LANES] = (
                acc_ref[c, pl.ds(r, rows, stride=dil), :].astype(o_ref.dtype))


def _matmul_residue_major(x, w, batch, seq, dil, tm, tn):
    t, k = x.shape
    n = w.shape[1]
    if dil == 1:
        return _matmul(x, w, tm, tn, BF16).reshape(batch, seq, n)
    tiles_per_seq = seq // tm
    out = pl.pallas_call(
        functools.partial(_mm_perm_kernel, dil=dil, tm=tm, tn=tn),
        grid=(t // tm, n // tn),
        in_specs=[pl.BlockSpec((tm, k), lambda i, j: (i, 0)),
                  pl.BlockSpec((k, tn), lambda i, j: (0, j))],
        out_specs=pl.BlockSpec((1, dil, tm // dil, tn),
                               lambda i, j: (i // tiles_per_seq, 0, i % tiles_per_seq, j)),
        out_shape=jax.ShapeDtypeStruct((batch, dil, seq // dil, n), BF16),
        scratch_shapes=[pltpu.VMEM((tn // LANES, tm, LANES), F32)],
        compiler_params=_PARALLEL2,
        name="matmul_residue_major",
    )(x, w)
    return out.reshape(batch, seq, n)


_CONV_HALO = 32
_CONV_ROWS = 64


def _conv_kernel(val_ref, gate_ref, w_ref, cb_ref, g_ref, b_ref, o_ref, ext_ref, *, ts):
    s = pl.program_id(1)

    @pl.when(s == 0)
    def _():
        ext_ref[0:_CONV_HALO, :] = jnp.zeros((_CONV_HALO, CONV_CH), F32)

    val = val_ref[0].astype(F32)
    gate = gate_ref[0].astype(F32)
    ext_ref[_CONV_HALO:_CONV_HALO + ts, :] = val * _sigmoid(gate)
    first_tap = _CONV_HALO - (CONV_WIDTH - 1)
    for rc in range(ts // _CONV_ROWS):
        r0 = rc * _CONV_ROWS
        acc = jnp.zeros((_CONV_ROWS, CONV_CH), F32)
        for j in range(CONV_WIDTH):
            lo = r0 + first_tap + j
            acc = acc + ext_ref[lo:lo + _CONV_ROWS, :] * w_ref[j:j + 1, :]
        y = _layer_norm(acc + cb_ref[...], g_ref[...], b_ref[...])
        o_ref[0, r0:r0 + _CONV_ROWS, :] = (y * _sigmoid(y)).astype(o_ref.dtype)
    ext_ref[0:_CONV_HALO, :] = ext_ref[ts:ts + _CONV_HALO, :]


def _conformer_conv(h, conv_w, conv_b, ln_g, ln_b, ts=512):
    b, s, _ = h.shape
    w_pad = jnp.zeros((32, CONV_CH), F32).at[:CONV_WIDTH].set(conv_w)
    vec = lambda v: v.reshape(1, CONV_CH).astype(F32)
    full = lambda shape: pl.BlockSpec(shape, lambda bi, si: (0, 0))
    return pl.pallas_call(
        functools.partial(_conv_kernel, ts=ts),
        grid=(b, s // ts),
        in_specs=[pl.BlockSpec((1, ts, CONV_CH), lambda bi, si: (bi, si, 0)),
                  pl.BlockSpec((1, ts, CONV_CH), lambda bi, si: (bi, si, 1)),
                  full((32, CONV_CH)), full((1, CONV_CH)), full((1, CONV_CH)), full((1, CONV_CH))],
        out_specs=pl.BlockSpec((1, ts, CONV_CH), lambda bi, si: (bi, si, 0)),
        out_shape=jax.ShapeDtypeStruct((b, s, CONV_CH), BF16),
        scratch_shapes=[pltpu.VMEM((ts + _CONV_HALO, CONV_CH), F32)],
        compiler_params=pltpu.CompilerParams(dimension_semantics=("parallel", "arbitrary")),
        name="conformer_conv",
    )(h, h, w_pad, vec(conv_b), vec(ln_g), vec(ln_b))


def _rel_bucket(dist):
    max_exact = NUM_BUCKETS // 2
    n = dist.astype(jnp.int32)
    nf = jnp.maximum(n, 1).astype(F32)
    large = max_exact + (jnp.log(nf / max_exact) / math.log(MAX_DISTANCE / max_exact)
                         * (NUM_BUCKETS - max_exact)).astype(jnp.int32)
    large = jnp.minimum(large, NUM_BUCKETS - 1)
    return jnp.where(n < max_exact, n, large)


def _bias_by_distance(rel_bias, n):
    return rel_bias[_rel_bucket(jnp.arange(n, dtype=jnp.int32))].astype(F32).T


def _toeplitz_kernel(win_ref, o_ref):
    n_win, span = win_ref.shape
    row = lax.broadcasted_iota(jnp.int32, (LANES, span), 0)
    for n in range(n_win):
        y = jnp.broadcast_to(win_ref[n:n + 1, :], (LANES, span))
        for bit in range(LANES.bit_length() - 1):
            y = jnp.where((row >> bit) & 1 == 1, pltpu.roll(y, 1 << bit, 1), y)
        o_ref[n] = y[:, LANES:]


def _toeplitz_tiles(windows, per_step=8):
    n, span = windows.shape
    width = span - LANES
    return pl.pallas_call(
        _toeplitz_kernel,
        grid=(n // per_step,),
        in_specs=[pl.BlockSpec((per_step, span), lambda i: (i, 0))],
        out_specs=pl.BlockSpec((per_step, LANES, width), lambda i: (i, 0, 0)),
        out_shape=jax.ShapeDtypeStruct((n, LANES, width), F32),
        compiler_params=pltpu.CompilerParams(dimension_semantics=("parallel",)),
        name="bias_toeplitz",
    )(windows)


def _dsa_bias_tiles(rel_bias, seq):
    nblk = seq // Q_BLOCK
    heads = rel_bias.shape[1]
    f = _bias_by_distance(rel_bias, seq)
    u = jnp.concatenate([jnp.broadcast_to(f[:, :1], (heads, 2 * Q_BLOCK)), f], axis=1)
    blocks = u.reshape(heads, nblk + 2, Q_BLOCK)
    windows = jnp.concatenate([blocks[:, :-1], blocks[:, 1:]], axis=-1)
    windows = windows.transpose(1, 0, 2).reshape((nblk + 1) * heads, 2 * Q_BLOCK)
    return _toeplitz_tiles(windows).reshape(nblk + 1, heads, Q_BLOCK, Q_BLOCK)


def _dil_bias_tiles(rel_bias):
    max_dil = max(d for _, d in DIL_GROUPS)
    heads = rel_bias.shape[1]
    span = 2 * BAND_BLOCK
    f = _bias_by_distance(rel_bias, span * max_dil + 1)
    windows = []
    for _, dil in DIL_GROUPS:
        fd = f[:, 0:span * dil + 1:dil]
        tail = jnp.broadcast_to(fd[:, :1], (heads, BAND_BLOCK - 1))
        windows.append(jnp.concatenate([fd[:, ::-1], tail], axis=1))
    tiles = _toeplitz_tiles(jnp.concatenate(windows, axis=0))
    return tiles.reshape(len(DIL_GROUPS), heads, BAND_BLOCK, span)


_KC = 2 * Q_BLOCK
_VT_ROWS = DSA_HEAD_DIM + 16


def _sortable_key(score):
    score = jnp.where(score == 0.0, 0.0, score)
    bits = lax.bitcast_convert_type(score, jnp.int32)
    return jnp.where(bits < 0, bits ^ jnp.int32(0x7FFFFFFF), bits)


def _rows_reduce(x, op):
    rows = x.shape[0]
    return op(op(x.reshape(rows // 8, 8, LANES), axis=0), axis=0, keepdims=True)


def _keys_reduce(x, op):
    h, keys, _ = x.shape
    return op(op(x.reshape(h, keys // 8, 8, LANES), axis=1), axis=1, keepdims=True)


def _dsa_t_kernel(q_ref, qi_ref, wq_ref, k_ref, vt_ref, ki_ref, bias_hbm, o_ref,
                  bias_ref, keys_ref, qm_ref, acc_ref, sem, *, topk):
    b = pl.program_id(0)
    i = pl.program_id(1)

    @pl.when((b == 0) & (i == 0))
    def _():
        cp = pltpu.make_async_copy(bias_hbm, bias_ref, sem)
        cp.start()
        cp.wait()

    nch = (i + 2) // 2
    key_off = lax.broadcasted_iota(jnp.int32, (_KC, LANES), 0)
    q_pos = i * Q_BLOCK + lax.broadcasted_iota(jnp.int32, (_KC, LANES), 1)
    lane = lax.broadcasted_iota(jnp.int32, (Q_BLOCK, LANES), 1)
    low_half = lane < DSA_HEAD_DIM

    w_t = (wq_ref[0].astype(F32) * (IDX_HEADS ** -0.5 * IDX_DIM ** -0.5)).T
    qi = qi_ref[0].astype(F32)
    qi_heads = []
    for p in range(IDX_HEADS // 2):
        pair = qi[:, p * LANES:(p + 1) * LANES]
        qi_heads.append(jnp.where(low_half, pair, 0.0))
        qi_heads.append(jnp.where(low_half, pltpu.roll(pair, IDX_DIM, 1), 0.0))
    qi_stack = jnp.concatenate(qi_heads, axis=0).astype(BF16)
    q_scaled = (q_ref[0].astype(F32) * (DSA_HEAD_DIM ** -0.5)).astype(BF16)
    for p in range(DSA_HEADS // 2):
        qp = q_scaled[:, p * LANES:(p + 1) * LANES]
        qm_ref[2 * p] = jnp.where(low_half, qp, jnp.zeros_like(qp))
        qm_ref[2 * p + 1] = jnp.where(low_half, jnp.zeros_like(qp), qp)

    def score_body(c, carry):
        k0 = pl.multiple_of(c * _KC, _KC)
        ki = ki_ref[0, pl.ds(k0, _KC), :]
        score = jnp.zeros((_KC, LANES), F32)
        dots = _dot_nt(ki, qi_stack)
        for h in range(IDX_HEADS):
            w_row = w_t[IDX_DIM + h:IDX_DIM + h + 1, :]
            score = score + w_row * jnp.maximum(dots[:, h * Q_BLOCK:(h + 1) * Q_BLOCK], 0.0)
        keys_ref[c] = jnp.where(k0 + key_off <= q_pos, _sortable_key(score), jnp.int32(INT_MIN))
        return carry

    lax.fori_loop(0, nch, score_body, 0)

    def count_ge(cand):
        cb = jnp.broadcast_to(cand, (8, LANES))
        chains = 4

        def body(c, acc):
            kk = keys_ref[c].reshape(_KC // (8 * chains), chains, 8, LANES)
            return acc + jnp.sum(jnp.where(kk >= cb, 1.0, 0.0), axis=0)

        acc = lax.fori_loop(0, nch, body, jnp.zeros((chains, 8, LANES), F32))
        return jnp.sum(jnp.sum(acc, axis=0), axis=0, keepdims=True)

    kf = float(topk)
    zero = jnp.zeros((1, LANES), jnp.int32)
    thr0 = jnp.where(count_ge(zero) >= kf, zero, jnp.full((1, LANES), INT_MIN, jnp.int32))

    def bit_body(t, thr):
        cand = thr | jnp.left_shift(jnp.int32(1), 30 - t)
        return jnp.where(count_ge(cand) >= kf, cand, thr)

    thr = lax.fori_loop(0, 31, bit_body, thr0)
    n_gt = count_ge(jnp.where(thr == jnp.int32(2 ** 31 - 1), thr, thr + 1))
    need = kf - n_gt
    tri = (lax.broadcasted_iota(jnp.int32, (_KC, _KC), 1)
           <= lax.broadcasted_iota(jnp.int32, (_KC, _KC), 0)).astype(BF16)

    acc_ref[...] = jnp.zeros_like(acc_ref)

    def attn_body(c, carry):
        ms, tie_carry = carry
        k0 = pl.multiple_of(c * _KC, _KC)
        kk = keys_ref[c]
        causal = kk > jnp.int32(INT_MIN)
        eq = (kk == thr) & causal
        eq_f = jnp.where(eq, 1.0, 0.0)
        rank = _dot(tri, eq_f.astype(BF16)) + tie_carry
        sel = ((kk > thr) & causal) | (eq & (rank <= need))
        tie_carry = tie_carry + _rows_reduce(eq_f, jnp.sum)
        d = i - 2 * c
        kc = k_ref[0, pl.ds(k0, _KC), :]
        k_heads = jnp.stack([kc[:, (h // 2) * LANES:(h // 2 + 1) * LANES] for h in range(DSA_HEADS)])
        s = jnp.einsum("hkd,hqd->hkq", k_heads, qm_ref[...], preferred_element_type=F32)
        bias = jnp.concatenate([bias_ref[d + 1], bias_ref[d]], axis=1)
        s = jnp.where(sel, s + bias, NEG)
        m_new = jnp.maximum(ms, _keys_reduce(s, jnp.max))
        alpha = jnp.exp(ms - m_new)
        pr = jnp.exp((s - m_new).astype(BF16))
        pv = jnp.einsum("hdk,hkq->hdq", vt_ref[0, c], pr, preferred_element_type=F32)
        acc_ref[...] = alpha * acc_ref[...] + pv
        return m_new, tie_carry

    init = (jnp.full((DSA_HEADS, 1, LANES), NEG, F32), jnp.zeros((1, LANES), F32))
    lax.fori_loop(0, nch, attn_body, init)
    acc = acc_ref[...]
    out_t = acc[:, :DSA_HEAD_DIM] / acc[:, DSA_HEAD_DIM:DSA_HEAD_DIM + 1]
    o_ref[0] = out_t.reshape(DSA_HEADS * DSA_HEAD_DIM, Q_BLOCK).T.astype(o_ref.dtype)


def _dsa_attention(h, bias_tiles):
    b, s, _ = h.shape
    width = DSA_HEADS * DSA_HEAD_DIM
    nblk = s // Q_BLOCK
    topk = min(DSA_TOPK_MAX, s // 4)
    kw_block = (CONV_CH * 2 + 4 * width) // LANES
    v_cols = 2 * CONV_CH + 2 * width
    vt = h[:, :, v_cols:v_cols + width].reshape(b, s // _KC, _KC, DSA_HEADS, DSA_HEAD_DIM).transpose(0, 1, 3, 4, 2)
    vt = jnp.concatenate([vt, jnp.ones((b, s // _KC, DSA_HEADS, _VT_ROWS - DSA_HEAD_DIM, _KC), BF16)], axis=3)
    return pl.pallas_call(
        functools.partial(_dsa_t_kernel, topk=topk),
        grid=(b, nblk),
        in_specs=[pl.BlockSpec((1, Q_BLOCK, width), lambda bi, qi: (bi, qi, 2)),
                  pl.BlockSpec((1, Q_BLOCK, width), lambda bi, qi: (bi, qi, 5)),
                  pl.BlockSpec((1, Q_BLOCK, LANES), lambda bi, qi: (bi, qi, kw_block)),
                  pl.BlockSpec((1, s, width), lambda bi, qi: (bi, 0, 3)),
                  pl.BlockSpec((1, s // _KC, DSA_HEADS, _VT_ROWS, _KC), lambda bi, qi: (bi, 0, 0, 0, 0)),
                  pl.BlockSpec((1, s, LANES), lambda bi, qi: (bi, 0, kw_block)),
                  pl.BlockSpec(memory_space=pl.ANY)],
        out_specs=pl.BlockSpec((1, Q_BLOCK, width), lambda bi, qi: (bi, qi, 0)),
        out_shape=jax.ShapeDtypeStruct((b, s, width), BF16),
        scratch_shapes=[pltpu.VMEM((nblk + 1, DSA_HEADS, Q_BLOCK, Q_BLOCK), F32),
                        pltpu.VMEM((s // _KC, _KC, Q_BLOCK), jnp.int32),
                        pltpu.VMEM((DSA_HEADS, Q_BLOCK, LANES), BF16),
                        pltpu.VMEM((DSA_HEADS, _VT_ROWS, Q_BLOCK), F32),
                        pltpu.SemaphoreType.DMA(())],
        compiler_params=pltpu.CompilerParams(dimension_semantics=("arbitrary", "arbitrary")),
        name="dsa_attention",
    )(h, h, h, h, vt, h, bias_tiles)


_DIL_WAVE = 8


def _dil_kernel(*refs, seq):
    n_g = len(DIL_GROUPS)
    qkv = refs[:3 * n_g]
    bias_ref, o_ref, og_ref, lg_ref = refs[3 * n_g:]
    c = pl.program_id(2)
    ii = lax.broadcasted_iota(jnp.int32, (BAND_BLOCK, BAND_BLOCK), 0)
    jj = lax.broadcasted_iota(jnp.int32, (BAND_BLOCK, BAND_BLOCK), 1)
    valid_prev = jj >= ii
    valid_cur = ii >= jj
    scale = DIL_HEAD_DIM ** -0.5
    for g, (window, dil) in enumerate(DIL_GROUPS):
        assert window // dil == BAND_BLOCK
        q_ref, k_ref, v_ref = qkv[3 * g:3 * g + 3]
        sub_len = seq // dil
        blocks_per_chunk = DIL_CHUNK // (BAND_BLOCK * dil)
        bias_prev = bias_ref[g, 0, :, :BAND_BLOCK]
        bias_cur = bias_ref[g, 0, :, BAND_BLOCK:]
        tiles = [(r, nl) for r in range(dil) for nl in range(blocks_per_chunk)]
        for w0 in range(0, len(tiles), _DIL_WAVE):
            wave = tiles[w0:w0 + _DIL_WAVE]
            cur, prev, mask_prev = [], [], []
            for r, nl in wave:
                row0 = pl.multiple_of(r * sub_len + c * (DIL_CHUNK // dil) + nl * BAND_BLOCK, BAND_BLOCK)
                cur.append(pl.ds(row0, BAND_BLOCK))
                prev.append(pl.ds(pl.multiple_of(jnp.maximum(row0 - BAND_BLOCK, 0), BAND_BLOCK), BAND_BLOCK))
                mask_prev.append(valid_prev if nl > 0 else (valid_prev & (c > 0)))
            gather = lambda ref, slices: jnp.stack([ref[0, sl, :] for sl in slices])
            qb = gather(q_ref, cur)
            s_cur = jnp.einsum("wqd,wkd->wqk", qb, gather(k_ref, cur), preferred_element_type=F32)
            s_prev = jnp.einsum("wqd,wkd->wqk", qb, gather(k_ref, prev), preferred_element_type=F32)
            s_cur = jnp.where(valid_cur, s_cur * scale + bias_cur, NEG)
            s_prev = jnp.where(jnp.stack(mask_prev), s_prev * scale + bias_prev, NEG)
            m = jnp.max(jnp.maximum(s_cur, s_prev), axis=2, keepdims=True)
            p_cur = jnp.exp(s_cur - m)
            p_prev = jnp.exp(s_prev - m)
            ssum = jnp.sum(p_cur + p_prev, axis=2, keepdims=True)
            o = (jnp.einsum("wqk,wkd->wqd", p_cur.astype(BF16), gather(v_ref, cur), preferred_element_type=F32)
                 + jnp.einsum("wqk,wkd->wqd", p_prev.astype(BF16), gather(v_ref, prev),
                              preferred_element_type=F32)) / ssum
            lse = m + jnp.log(ssum)
            for w, (r, nl) in enumerate(wave):
                dst = pl.ds(r + nl * BAND_BLOCK * dil, BAND_BLOCK, stride=dil) if dil > 1 else pl.ds(
                    nl * BAND_BLOCK, BAND_BLOCK)
                og_ref[g, dst, :] = o[w]
                lg_ref[g, dst, :] = jnp.broadcast_to(lse[w], (BAND_BLOCK, LANES))
    rows = 256
    for rc in range(DIL_CHUNK // rows):
        sl = slice(rc * rows, (rc + 1) * rows)
        lses = [lg_ref[g, sl, :] for g in range(n_g)]
        mx = functools.reduce(jnp.maximum, lses)
        ws = [jnp.exp(l - mx) for l in lses]
        den = functools.reduce(lambda a, b_: a + b_, ws)
        num = functools.reduce(lambda a, b_: a + b_, [ws[g] * og_ref[g, sl, :] for g in range(n_g)])
        o_ref[0, sl, :] = (num / den).astype(o_ref.dtype)


def _dilated_attention(hs, bias_tiles):
    b, s, _ = hs[0].shape
    in_specs = []
    args = []
    for hg in hs:
        for part in range(3):
            in_specs.append(pl.BlockSpec((1, s, DIL_HEAD_DIM),
                                         lambda bi, hi, ci, part=part: (bi, 0, part * DIL_HEADS + hi)))
            args.append(hg)
    in_specs.append(pl.BlockSpec((len(DIL_GROUPS), 1, BAND_BLOCK, 2 * BAND_BLOCK),
                                 lambda bi, hi, ci: (0, hi, 0, 0)))
    return pl.pallas_call(
        functools.partial(_dil_kernel, seq=s),
        grid=(b, DIL_HEADS, s // DIL_CHUNK),
        in_specs=in_specs,
        out_specs=pl.BlockSpec((1, DIL_CHUNK, DIL_HEAD_DIM), lambda bi, hi, ci: (bi, ci, hi)),
        out_shape=jax.ShapeDtypeStruct((b, s, DIL_HEADS * DIL_HEAD_DIM), BF16),
        scratch_shapes=[pltpu.VMEM((len(DIL_GROUPS), DIL_CHUNK, LANES), F32),
                        pltpu.VMEM((len(DIL_GROUPS), DIL_CHUNK, LANES), F32)],
        compiler_params=pltpu.CompilerParams(dimension_semantics=("parallel", "parallel", "arbitrary")),
        name="dilated_attention",
    )(*args, bias_tiles)


def _outproj_kernel(*refs, n_in, with_router):
    ins = refs[:n_in]
    ws = refs[n_in:2 * n_in]
    x_ref, g_ref, b_ref = refs[2 * n_in:2 * n_in + 3]
    rest = refs[2 * n_in + 3:]
    mix = _dot(ins[0][...], ws[0][...])
    for a_ref, w_ref in zip(ins[1:], ws[1:]):
        mix = mix + _dot(a_ref[...], w_ref[...])
    xn = _layer_norm(DEEPNORM_ALPHA * x_ref[...] + mix, g_ref[...], b_ref[...])
    if not with_router:
        o_ref, ob_ref = rest
    else:
        router_ref, o_ref, ob_ref, gates_ref, counts_ref, run_ref, tri_ref = rest
        tm = xn.shape[0]

        @pl.when(pl.program_id(0) == 0)
        def _():
            run_ref[...] = jnp.zeros_like(run_ref)
            tri_ref[...] = (lax.broadcasted_iota(jnp.int32, (tm, tm), 0)
                            > lax.broadcasted_iota(jnp.int32, (tm, tm), 1)).astype(BF16)

        logits = jnp.dot(xn, router_ref[...], preferred_element_type=F32, precision=lax.Precision.HIGHEST)
        lane = lax.broadcasted_iota(jnp.int32, logits.shape, 1).astype(F32)
        lg = jnp.where(lane < N_EXPERTS, logits, -jnp.inf)
        m1 = jnp.max(lg, axis=1, keepdims=True)
        i1 = jnp.min(jnp.where(lg == m1, lane, float(LANES)), axis=1, keepdims=True)
        lg2 = jnp.where(lane == i1, -jnp.inf, lg)
        m2 = jnp.max(lg2, axis=1, keepdims=True)
        i2 = jnp.min(jnp.where(lg2 == m2, lane, float(LANES)), axis=1, keepdims=True)
        e2 = jnp.exp(m2 - m1)
        den = 1.0 + e2
        oh1 = jnp.where(lane == i1, 1.0, 0.0)
        oh2 = jnp.where(lane == i2, 1.0, 0.0)
        both = oh1 + oh2
        before = _dot(tri_ref[...], both.astype(BF16)) + run_ref[...]
        r1 = jnp.sum(before * oh1, axis=1, keepdims=True)
        r2 = jnp.sum(before * oh2, axis=1, keepdims=True)
        run_ref[...] += jnp.sum(both, axis=0, keepdims=True)
        counts_ref[...] = run_ref[...]
        gates_ref[...] = (jnp.where(lane == 0.0, i1, 0.0) + jnp.where(lane == 1.0, i2, 0.0)
                          + jnp.where(lane == 2.0, 1.0 / den, 0.0) + jnp.where(lane == 3.0, e2 / den, 0.0)
                          + jnp.where(lane == 4.0, r1, 0.0) + jnp.where(lane == 5.0, r2, 0.0))
    o_ref[...] = xn
    ob_ref[...] = xn.astype(BF16)


def _outproj_ln(ins, ws, x, ln_g, ln_b, router=None, tm=512):
    t, d = x.shape
    n_in = len(ins)
    row = lambda width: pl.BlockSpec((tm, width), lambda i: (i, 0))
    full = lambda a: pl.BlockSpec(a.shape, lambda i: (0, 0))
    vec = lambda v: v.reshape(1, d).astype(F32)
    args = list(ins) + list(ws) + [x, vec(ln_g), vec(ln_b)]
    in_specs = [row(a.shape[1]) for a in ins] + [full(w) for w in ws] + [row(d), full(vec(ln_g)), full(vec(ln_b))]
    out_shape = [jax.ShapeDtypeStruct((t, d), F32), jax.ShapeDtypeStruct((t, d), BF16)]
    out_specs = [row(d), row(d)]
    if router is not None:
        router_pad = jnp.zeros((d, LANES), F32).at[:, :N_EXPERTS].set(router)
        args.append(router_pad)
        in_specs.append(full(router_pad))
        out_shape += [jax.ShapeDtypeStruct((t, LANES), F32), jax.ShapeDtypeStruct((1, LANES), F32)]
        out_specs += [row(LANES), pl.BlockSpec((1, LANES), lambda i: (0, 0))]
        scratch = [pltpu.VMEM((1, LANES), F32), pltpu.VMEM((tm, tm), BF16)]
    else:
        scratch = []
    semantics = "arbitrary" if router is not None else "parallel"
    return pl.pallas_call(
        functools.partial(_outproj_kernel, n_in=n_in, with_router=router is not None),
        grid=(t // tm,),
        in_specs=in_specs,
        out_specs=out_specs,
        out_shape=out_shape,
        scratch_shapes=scratch,
        compiler_params=pltpu.CompilerParams(dimension_semantics=(semantics,)),
        name="outproj_ln",
    )(*args)


def _ffn_kernel(xb_ref, x_ref, wg_ref, wu_ref, wd_ref, g_ref, b_ref, o_ref, ob_ref, acc_ref):
    f = pl.program_id(1)

    @pl.when(f == 0)
    def _():
        acc_ref[...] = jnp.zeros_like(acc_ref)

    xb = xb_ref[...]
    hg = _dot(xb, wg_ref[...])
    hu = _dot(xb, wu_ref[...])
    act = (hg * _sigmoid(hg) * hu).astype(BF16)
    acc_ref[...] += _dot(act, wd_ref[...])

    @pl.when(f == pl.num_programs(1) - 1)
    def _():
        xn = _layer_norm(DEEPNORM_ALPHA * x_ref[...] + acc_ref[...], g_ref[...], b_ref[...])
        o_ref[...] = xn
        ob_ref[...] = xn.astype(BF16)


def _ffn_ln(xb, x, wg, wu, wd, ln_g, ln_b, tm=512, tf=1408):
    t, d = x.shape
    ff = wg.shape[1]
    vec = lambda v: v.reshape(1, d).astype(F32)
    return pl.pallas_call(
        _ffn_kernel,
        grid=(t // tm, ff // tf),
        in_specs=[pl.BlockSpec((tm, d), lambda i, f: (i, 0)),
                  pl.BlockSpec((tm, d), lambda i, f: (i, 0)),
                  pl.BlockSpec((d, tf), lambda i, f: (0, f)),
                  pl.BlockSpec((d, tf), lambda i, f: (0, f)),
                  pl.BlockSpec((tf, d), lambda i, f: (f, 0)),
                  pl.BlockSpec((1, d), lambda i, f: (0, 0)),
                  pl.BlockSpec((1, d), lambda i, f: (0, 0))],
        out_specs=[pl.BlockSpec((tm, d), lambda i, f: (i, 0)),
                   pl.BlockSpec((tm, d), lambda i, f: (i, 0))],
        out_shape=[jax.ShapeDtypeStruct((t, d), F32), jax.ShapeDtypeStruct((t, d), BF16)],
        scratch_shapes=[pltpu.VMEM((tm, d), F32)],
        compiler_params=pltpu.CompilerParams(dimension_semantics=("parallel", "arbitrary")),
        name="ffn_ln",
    )(xb, x, wg, wu, wd, vec(ln_g), vec(ln_b))


_MOE_TM = 1024


def _route(route, counts, n_tiles):
    expert = route[:, 0:2].astype(jnp.int32)
    rank = route[:, 4:6].astype(jnp.int32)
    counts = counts[0, :N_EXPERTS].astype(jnp.int32)
    padded = (counts + _MOE_TM - 1) // _MOE_TM * _MOE_TM
    ends = jnp.cumsum(padded)
    starts = ends - padded
    onehot = expert[:, :, None] == jnp.arange(N_EXPERTS, dtype=jnp.int32)
    pos = (rank + jnp.sum(jnp.where(onehot, starts, 0), axis=-1)).reshape(-1)
    tile_start = jnp.arange(n_tiles, dtype=jnp.int32) * _MOE_TM
    tile_expert = jnp.minimum(jnp.sum((tile_start[:, None] >= ends[None, :]).astype(jnp.int32), axis=1),
                              N_EXPERTS - 1)
    return pos.astype(jnp.int32), tile_expert.astype(jnp.int32)


def _row_scatter_kernel(pos_ref, x_ref, init_ref, o_ref, sem, *, tm):
    del init_ref

    def body(j, carry):
        for choice in range(2):
            p = pos_ref[0, 0, 2 * j + choice]
            pltpu.make_async_copy(x_ref.at[pl.ds(j, 1)], o_ref.at[pl.ds(p, 1)], sem).start()
        return carry

    lax.fori_loop(0, tm, body, 0, unroll=8)
    for _ in range(2):
        pltpu.make_async_copy(x_ref, o_ref.at[pl.ds(0, tm)], sem).wait()


def _row_scatter(x, pos, n_rows, tm=512):
    t, d = x.shape
    return pl.pallas_call(
        functools.partial(_row_scatter_kernel, tm=tm),
        grid=(t // tm,),
        in_specs=[pl.BlockSpec((1, 1, 2 * tm), lambda i: (i, 0, 0), memory_space=pltpu.SMEM),
                  pl.BlockSpec((tm, d), lambda i: (i, 0)),
                  pl.BlockSpec(memory_space=pl.ANY)],
        out_specs=pl.BlockSpec(memory_space=pl.ANY),
        out_shape=jax.ShapeDtypeStruct((n_rows, d), x.dtype),
        scratch_shapes=[pltpu.SemaphoreType.DMA(())],
        input_output_aliases={2: 0},
        compiler_params=pltpu.CompilerParams(dimension_semantics=("arbitrary",)),
        name="moe_row_scatter",
    )(pos.reshape(t // tm, 1, 2 * tm), x, jnp.zeros((n_rows, d), x.dtype))


def _expert_ffn_kernel(te_ref, x_ref, wg_ref, wu_ref, wd_ref, o_ref, acc_ref):
    del te_ref
    f = pl.program_id(1)

    @pl.when(f == 0)
    def _():
        acc_ref[...] = jnp.zeros_like(acc_ref)

    xb = x_ref[...].astype(BF16)
    hg = _dot(xb, wg_ref[0].astype(BF16))
    hu = _dot(xb, wu_ref[0].astype(BF16))
    act = (hg * _sigmoid(hg) * hu).astype(BF16)
    acc_ref[...] += _dot(act, wd_ref[0].astype(BF16))

    @pl.when(f == pl.num_programs(1) - 1)
    def _():
        o_ref[...] = acc_ref[...]


def _expert_ffn(xs, tile_expert, wg, wu, wd, tf=512):
    n_rows, d = xs.shape
    ff = wg.shape[2]
    grid_spec = pltpu.PrefetchScalarGridSpec(
        num_scalar_prefetch=1,
        grid=(n_rows // _MOE_TM, ff // tf),
        in_specs=[pl.BlockSpec((_MOE_TM, d), lambda i, f, te: (i, 0)),
                  pl.BlockSpec((1, d, tf), lambda i, f, te: (te[i], 0, f)),
                  pl.BlockSpec((1, d, tf), lambda i, f, te: (te[i], 0, f)),
                  pl.BlockSpec((1, tf, d), lambda i, f, te: (te[i], f, 0))],
        out_specs=pl.BlockSpec((_MOE_TM, d), lambda i, f, te: (i, 0)),
        scratch_shapes=[pltpu.VMEM((_MOE_TM, d), F32)])
    return pl.pallas_call(
        _expert_ffn_kernel,
        grid_spec=grid_spec,
        out_shape=jax.ShapeDtypeStruct((n_rows, d), F32),
        compiler_params=pltpu.CompilerParams(dimension_semantics=("parallel", "arbitrary")),
        name="moe_expert_ffn",
    )(tile_expert, xs, wg, wu, wd)


def _combine_kernel(pos_ref, route_ref, x_ref, g_ref, b_ref, y_ref, o_ref, ybuf_ref, sem, *, tm):
    def body(j, carry):
        for choice in range(2):
            p = pos_ref[0, 0, 2 * j + choice]
            pltpu.make_async_copy(y_ref.at[pl.ds(p, 1)], ybuf_ref.at[choice, pl.ds(j, 1)], sem).start()
        return carry

    lax.fori_loop(0, tm, body, 0, unroll=8)
    for choice in range(2):
        pltpu.make_async_copy(y_ref.at[pl.ds(0, tm)], ybuf_ref.at[choice], sem).wait()
    route = route_ref[...]
    y = route[:, 2:3] * ybuf_ref[0] + route[:, 3:4] * ybuf_ref[1]
    o_ref[...] = _layer_norm(DEEPNORM_ALPHA * x_ref[...] + y, g_ref[...], b_ref[...])


def _combine_ln(ys, pos, route, x, ln_g, ln_b, tm=512):
    t, d = x.shape
    vec = lambda v: v.reshape(1, d).astype(F32)
    return pl.pallas_call(
        functools.partial(_combine_kernel, tm=tm),
        grid=(t // tm,),
        in_specs=[pl.BlockSpec((1, 1, 2 * tm), lambda i: (i, 0, 0), memory_space=pltpu.SMEM),
                  pl.BlockSpec((tm, LANES), lambda i: (i, 0)),
                  pl.BlockSpec((tm, d), lambda i: (i, 0)),
                  pl.BlockSpec((1, d), lambda i: (0, 0)),
                  pl.BlockSpec((1, d), lambda i: (0, 0)),
                  pl.BlockSpec(memory_space=pl.ANY)],
        out_specs=pl.BlockSpec((tm, d), lambda i: (i, 0)),
        out_shape=jax.ShapeDtypeStruct((t, d), F32),
        scratch_shapes=[pltpu.VMEM((2, tm, d), F32), pltpu.SemaphoreType.DMA(())],
        compiler_params=pltpu.CompilerParams(dimension_semantics=("arbitrary",)),
        name="moe_combine_ln",
    )(pos.reshape(t // tm, 1, 2 * tm), route, x, vec(ln_g), vec(ln_b), ys)


def _moe_ln(x, route, counts, wg, wu, wd, ln_g, ln_b):
    t, _ = x.shape
    n_tiles = 2 * t // _MOE_TM + N_EXPERTS
    pos, tile_expert = _route(route, counts, n_tiles)
    xs = _row_scatter(x, pos, n_tiles * _MOE_TM)
    ys = _expert_ffn(xs, tile_expert, wg, wu, wd)
    return _combine_ln(ys, pos, route, x, ln_g, ln_b)


def _even_layer(x, xb, batch, seq, dsa_bias, w_in, conv_w, conv_b, conv_ln_g, conv_ln_b, w_out,
                ln1_g, ln1_b, ffn_wg, ffn_wu, ffn_wd, ln2_g, ln2_b):
    w_in_p = jnp.zeros((D_MODEL, EVEN_IN_PAD), BF16).at[:, :w_in.shape[1]].set(w_in.astype(BF16))
    h = _matmul(xb, w_in_p, 1024, 640, BF16).reshape(batch, seq, EVEN_IN_PAD)
    a_out = _conformer_conv(h, conv_w, conv_b, conv_ln_g, conv_ln_b)
    att = _dsa_attention(h, dsa_bias)
    w_out_b = w_out.astype(BF16)
    x1, x1b = _outproj_ln([a_out.reshape(-1, CONV_CH), att.reshape(-1, DSA_HEADS * DSA_HEAD_DIM)],
                          [w_out_b[:CONV_CH], w_out_b[CONV_CH:]], x, ln1_g, ln1_b)
    return _ffn_ln(x1b, x1, ffn_wg.astype(BF16), ffn_wu.astype(BF16), ffn_wd.astype(BF16), ln2_g, ln2_b)


def _odd_layer(x, xb, batch, seq, dil_bias, w_in, w_out, ln1_g, ln1_b, router, moe_wg, moe_wu, moe_wd,
               ln2_g, ln2_b):
    gw = 3 * DIL_HEADS * DIL_HEAD_DIM
    w_in_b = w_in.astype(BF16)
    hs = [_matmul_residue_major(xb, w_in_b[:, g * gw:(g + 1) * gw], batch, seq, dil, 1024, 768)
          for g, (_, dil) in enumerate(DIL_GROUPS)]
    o = _dilated_attention(hs, dil_bias).reshape(-1, DIL_HEADS * DIL_HEAD_DIM)
    x1, _, route, counts = _outproj_ln([o], [w_out.astype(BF16)], x, ln1_g, ln1_b, router=router)
    out = _moe_ln(x1, route, counts, moe_wg, moe_wu, moe_wd, ln2_g, ln2_b)
    return out, out.astype(BF16)


def kernel(x, rel_bias, even_w_in, even_conv_w, even_conv_b, even_conv_ln_g, even_conv_ln_b, even_w_out, even_ln1_g, even_ln1_b, even_ffn_wg, even_ffn_wu, even_ffn_wd, even_ln2_g, even_ln2_b, odd_w_in, odd_w_out, odd_ln1_g, odd_ln1_b, odd_router, odd_moe_wg, odd_moe_wu, odd_moe_wd, odd_ln2_g, odd_ln2_b):
    batch, seq, d = x.shape
    assert d == D_MODEL and seq % DIL_CHUNK == 0
    depth = even_w_in.shape[0] + odd_w_in.shape[0]
    assert depth == DEPTH
    dsa_bias = _dsa_bias_tiles(rel_bias, seq)
    dil_bias = _dil_bias_tiles(rel_bias)
    xf = x.reshape(batch * seq, d)
    xb = xf.astype(BF16)
    for layer in range(depth):
        i = layer // 2
        if layer % 2 == 0:
            xf, xb = _even_layer(xf, xb, batch, seq, dsa_bias, even_w_in[i], even_conv_w[i], even_conv_b[i],
                                 even_conv_ln_g[i], even_conv_ln_b[i], even_w_out[i], even_ln1_g[i],
                                 even_ln1_b[i], even_ffn_wg[i], even_ffn_wu[i], even_ffn_wd[i],
                                 even_ln2_g[i], even_ln2_b[i])
        else:
            xf, xb = _odd_layer(xf, xb, batch, seq, dil_bias, odd_w_in[i], odd_w_out[i], odd_ln1_g[i],
                                odd_ln1_b[i], odd_router[i], odd_moe_wg[i], odd_moe_wu[i], odd_moe_wd[i],
                                odd_ln2_g[i], odd_ln2_b[i])
    return xf.reshape(batch, seq, d)
```

```python
import functools
import math

import jax
import jax.numpy as jnp
from jax import lax
from jax.experimental import pallas as pl
from jax.experimental.pallas import tpu as pltpu

BF16 = jnp.bfloat16
F32 = jnp.float32

D_MODEL = 1024
CONV_CH = 512
CONV_WIDTH = 31
DSA_HEADS = 8
DSA_HEAD_DIM = 64
IDX_HEADS = 8
IDX_DIM = 64
DSA_TOPK_MAX = 256
Q_BLOCK = 128
DIL_GROUPS = ((128, 1), (512, 4), (2048, 16))
DIL_HEADS = 8
DIL_HEAD_DIM = 128
BAND_BLOCK = 128
NUM_BUCKETS = 32
MAX_DISTANCE = 2048
N_EXPERTS = 8
LN_EPS = 1e-5
NEG = -1e30
DEPTH = 2
DEEPNORM_ALPHA = (2 * DEPTH) ** 0.25

LANES = 128
EVEN_IN_PAD = 3200
INT_MIN = -(2 ** 31)
DIL_CHUNK = BAND_BLOCK * max(d for _, d in DIL_GROUPS)

_PARALLEL2 = pltpu.CompilerParams(dimension_semantics=("parallel", "parallel"))


def _layer_norm(y, g, b):
    mu = jnp.mean(y, axis=-1, keepdims=True)
    yc = y - mu
    var = jnp.mean(yc * yc, axis=-1, keepdims=True)
    return yc * lax.rsqrt(var + LN_EPS) * g + b


def _sigmoid(x):
    return 1.0 / (1.0 + jnp.exp(-x))


def _dot_nt(a, b):
    return lax.dot_general(a, b, (((1,), (1,)), ((), ())), preferred_element_type=F32)


def _dot(a, b):
    return jnp.dot(a, b, preferred_element_type=F32)


def _mm_kernel(x_ref, w_ref, o_ref):
    o_ref[...] = _dot(x_ref[...], w_ref[...]).astype(o_ref.dtype)


def _matmul(x, w, tm, tn, out_dtype):
    t, k = x.shape
    n = w.shape[1]
    return pl.pallas_call(
        _mm_kernel,
        grid=(t // tm, n // tn),
        in_specs=[pl.BlockSpec((tm, k), lambda i, j: (i, 0)),
                  pl.BlockSpec((k, tn), lambda i, j: (0, j))],
        out_specs=pl.BlockSpec((tm, tn), lambda i, j: (i, j)),
        out_shape=jax.ShapeDtypeStruct((t, n), out_dtype),
        compiler_params=_PARALLEL2,
        name="matmul",
    )(x, w)


def _mm_perm_kernel(x_ref, w_ref, o_ref, acc_ref, *, dil, tm, tn):
    acc = _dot(x_ref[...], w_ref[...])
    rows = tm // dil
    for c in range(tn // LANES):
        acc_ref[c] = acc[:, c * LANES:(c + 1) * LANES]
    for r in range(dil):
        for c in range(tn // LANES):
            o_ref[0, r, :, c * LANES:(c + 1) * LANES] = (
                acc_ref[c, pl.ds(r, rows, stride=dil), :].astype(o_ref.dtype))


def _matmul_residue_major(x, w, batch, seq, dil, tm, tn):
    t, k = x.shape
    n = w.shape[1]
    if dil == 1:
        return _matmul(x, w, tm, tn, BF16).reshape(batch, seq, n)
    tiles_per_seq = seq // tm
    out = pl.pallas_call(
        functools.partial(_mm_perm_kernel, dil=dil, tm=tm, tn=tn),
        grid=(t // tm, n // tn),
        in_specs=[pl.BlockSpec((tm, k), lambda i, j: (i, 0)),
                  pl.BlockSpec((k, tn), lambda i, j: (0, j))],
        out_specs=pl.BlockSpec((1, dil, tm // dil, tn),
                               lambda i, j: (i // tiles_per_seq, 0, i % tiles_per_seq, j)),
        out_shape=jax.ShapeDtypeStruct((batch, dil, seq // dil, n), BF16),
        scratch_shapes=[pltpu.VMEM((tn // LANES, tm, LANES), F32)],
        compiler_params=_PARALLEL2,
        name="matmul_residue_major",
    )(x, w)
    return out.reshape(batch, seq, n)


_CONV_HALO = 32
_CONV_ROWS = 64


def _conv_kernel(val_ref, gate_ref, w_ref, cb_ref, g_ref, b_ref, o_ref, ext_ref, *, ts):
    s = pl.program_id(1)

    @pl.when(s == 0)
    def _():
        ext_ref[0:_CONV_HALO, :] = jnp.zeros((_CONV_HALO, CONV_CH), F32)

    val = val_ref[0].astype(F32)
    gate = gate_ref[0].astype(F32)
    ext_ref[_CONV_HALO:_CONV_HALO + ts, :] = val * _sigmoid(gate)
    first_tap = _CONV_HALO - (CONV_WIDTH - 1)
    for rc in range(ts // _CONV_ROWS):
        r0 = rc * _CONV_ROWS
        acc = jnp.zeros((_CONV_ROWS, CONV_CH), F32)
        for j in range(CONV_WIDTH):
            lo = r0 + first_tap + j
            acc = acc + ext_ref[lo:lo + _CONV_ROWS, :] * w_ref[j:j + 1, :]
        y = _layer_norm(acc + cb_ref[...], g_ref[...], b_ref[...])
        o_ref[0, r0:r0 + _CONV_ROWS, :] = (y * _sigmoid(y)).astype(o_ref.dtype)
    ext_ref[0:_CONV_HALO, :] = ext_ref[ts:ts + _CONV_HALO, :]


def _conformer_conv(h, conv_w, conv_b, ln_g, ln_b, ts=512):
    b, s, _ = h.shape
    w_pad = jnp.zeros((32, CONV_CH), F32).at[:CONV_WIDTH].set(conv_w)
    vec = lambda v: v.reshape(1, CONV_CH).astype(F32)
    full = lambda shape: pl.BlockSpec(shape, lambda bi, si: (0, 0))
    return pl.pallas_call(
        functools.partial(_conv_kernel, ts=ts),
        grid=(b, s // ts),
        in_specs=[pl.BlockSpec((1, ts, CONV_CH), lambda bi, si: (bi, si, 0)),
                  pl.BlockSpec((1, ts, CONV_CH), lambda bi, si: (bi, si, 1)),
                  full((32, CONV_CH)), full((1, CONV_CH)), full((1, CONV_CH)), full((1, CONV_CH))],
        out_specs=pl.BlockSpec((1, ts, CONV_CH), lambda bi, si: (bi, si, 0)),
        out_shape=jax.ShapeDtypeStruct((b, s, CONV_CH), BF16),
        scratch_shapes=[pltpu.VMEM((ts + _CONV_HALO, CONV_CH), F32)],
        compiler_params=pltpu.CompilerParams(dimension_semantics=("parallel", "arbitrary")),
        name="conformer_conv",
    )(h, h, w_pad, vec(conv_b), vec(ln_g), vec(ln_b))


def _rel_bucket(dist):
    max_exact = NUM_BUCKETS // 2
    n = dist.astype(jnp.int32)
    nf = jnp.maximum(n, 1).astype(F32)
    large = max_exact + (jnp.log(nf / max_exact) / math.log(MAX_DISTANCE / max_exact)
                         * (NUM_BUCKETS - max_exact)).astype(jnp.int32)
    large = jnp.minimum(large, NUM_BUCKETS - 1)
    return jnp.where(n < max_exact, n, large)


def _bias_by_distance(rel_bias, n):
    return rel_bias[_rel_bucket(jnp.arange(n, dtype=jnp.int32))].astype(F32).T


def _toeplitz_kernel(win_ref, o_ref):
    n_win, span = win_ref.shape
    for n in range(n_win):
        y = jnp.broadcast_to(win_ref[n:n + 1, :], (LANES, span))
        y = pltpu.roll(y, 0, 1, stride=1, stride_axis=0)
        o_ref[n] = y[:, LANES:]


def _toeplitz_tiles(windows, per_step=8):
    n, span = windows.shape
    width = span - LANES
    return pl.pallas_call(
        _toeplitz_kernel,
        grid=(n // per_step,),
        in_specs=[pl.BlockSpec((per_step, span), lambda i: (i, 0))],
        out_specs=pl.BlockSpec((per_step, LANES, width), lambda i: (i, 0, 0)),
        out_shape=jax.ShapeDtypeStruct((n, LANES, width), F32),
        compiler_params=pltpu.CompilerParams(dimension_semantics=("parallel",)),
        name="bias_toeplitz",
    )(windows)


def _dsa_bias_tiles(rel_bias, seq):
    nblk = seq // Q_BLOCK
    heads = rel_bias.shape[1]
    f = _bias_by_distance(rel_bias, seq)
    u = jnp.concatenate([jnp.broadcast_to(f[:, :1], (heads, 2 * Q_BLOCK)), f], axis=1)
    blocks = u.reshape(heads, nblk + 2, Q_BLOCK)
    windows = jnp.concatenate([blocks[:, :-1], blocks[:, 1:]], axis=-1)
    windows = windows.transpose(1, 0, 2).reshape((nblk + 1) * heads, 2 * Q_BLOCK)
    return _toeplitz_tiles(windows).reshape(nblk + 1, heads, Q_BLOCK, Q_BLOCK)


def _dil_bias_tiles(rel_bias):
    max_dil = max(d for _, d in DIL_GROUPS)
    heads = rel_bias.shape[1]
    span = 2 * BAND_BLOCK
    f = _bias_by_distance(rel_bias, span * max_dil + 1)
    windows = []
    for _, dil in DIL_GROUPS:
        fd = f[:, 0:span * dil + 1:dil]
        tail = jnp.broadcast_to(fd[:, :1], (heads, BAND_BLOCK - 1))
        windows.append(jnp.concatenate([fd[:, ::-1], tail], axis=1))
    tiles = _toeplitz_tiles(jnp.concatenate(windows, axis=0))
    return tiles.reshape(len(DIL_GROUPS), heads, BAND_BLOCK, span)


_KC = 2 * Q_BLOCK
_VT_ROWS = DSA_HEAD_DIM + 16


def _sortable_key(score):
    score = jnp.where(score == 0.0, 0.0, score)
    bits = lax.bitcast_convert_type(score, jnp.int32)
    return jnp.where(bits < 0, bits ^ jnp.int32(0x7FFFFFFF), bits)


def _rows_reduce(x, op):
    rows = x.shape[0]
    return op(op(x.reshape(rows // 8, 8, LANES), axis=0), axis=0, keepdims=True)


def _keys_reduce(x, op):
    h, keys, _ = x.shape
    return op(op(x.reshape(h, keys // 8, 8, LANES), axis=1), axis=1, keepdims=True)


def _dsa_t_kernel(q_ref, qi_ref, wq_ref, k_ref, vt_ref, ki_ref, bias_hbm, o_ref,
                  bias_ref, keys_ref, qm_ref, acc_ref, sem, *, topk):
    b = pl.program_id(0)
    i = pl.program_id(1)

    @pl.when((b == 0) & (i == 0))
    def _():
        cp = pltpu.make_async_copy(bias_hbm, bias_ref, sem)
        cp.start()
        cp.wait()

    nch = (i + 2) // 2
    key_off = lax.broadcasted_iota(jnp.int32, (_KC, LANES), 0)
    q_pos = i * Q_BLOCK + lax.broadcasted_iota(jnp.int32, (_KC, LANES), 1)
    lane = lax.broadcasted_iota(jnp.int32, (Q_BLOCK, LANES), 1)
    low_half = lane < DSA_HEAD_DIM

    w_t = (wq_ref[0].astype(F32) * (IDX_HEADS ** -0.5 * IDX_DIM ** -0.5)).T
    qi = qi_ref[0].astype(F32)
    qi_heads = []
    for p in range(IDX_HEADS // 2):
        pair = qi[:, p * LANES:(p + 1) * LANES]
        qi_heads.append(jnp.where(low_half, pair, 0.0))
        qi_heads.append(jnp.where(low_half, pltpu.roll(pair, IDX_DIM, 1), 0.0))
    qi_stack = jnp.concatenate(qi_heads, axis=0).astype(BF16)
    q_scaled = (q_ref[0].astype(F32) * (DSA_HEAD_DIM ** -0.5)).astype(BF16)
    for p in range(DSA_HEADS // 2):
        qp = q_scaled[:, p * LANES:(p + 1) * LANES]
        qm_ref[2 * p] = jnp.where(low_half, qp, jnp.zeros_like(qp))
        qm_ref[2 * p + 1] = jnp.where(low_half, jnp.zeros_like(qp), qp)

    def score_body(c, carry):
        k0 = pl.multiple_of(c * _KC, _KC)
        ki = ki_ref[0, pl.ds(k0, _KC), :]
        score = jnp.zeros((_KC, LANES), F32)
        dots = _dot_nt(ki, qi_stack)
        for h in range(IDX_HEADS):
            w_row = w_t[IDX_DIM + h:IDX_DIM + h + 1, :]
            score = score + w_row * jnp.maximum(dots[:, h * Q_BLOCK:(h + 1) * Q_BLOCK], 0.0)
        keys_ref[c] = jnp.where(k0 + key_off <= q_pos, _sortable_key(score), jnp.int32(INT_MIN))
        return carry

    lax.fori_loop(0, nch, score_body, 0)

    def count_ge(cand):
        cb = jnp.broadcast_to(cand, (8, LANES))
        chains = 4

        def body(c, acc):
            kk = keys_ref[c].reshape(_KC // (8 * chains), chains, 8, LANES)
            return acc + jnp.sum(jnp.where(kk >= cb, 1.0, 0.0), axis=0)

        acc = lax.fori_loop(0, nch, body, jnp.zeros((chains, 8, LANES), F32))
        return jnp.sum(jnp.sum(acc, axis=0), axis=0, keepdims=True)

    kf = float(topk)
    zero = jnp.zeros((1, LANES), jnp.int32)
    thr0 = jnp.where(count_ge(zero) >= kf, zero, jnp.full((1, LANES), INT_MIN, jnp.int32))

    def bit_body(t, thr):
        cand = thr | jnp.left_shift(jnp.int32(1), 30 - t)
        return jnp.where(count_ge(cand) >= kf, cand, thr)

    thr = lax.fori_loop(0, 31, bit_body, thr0)
    n_gt = count_ge(jnp.where(thr == jnp.int32(2 ** 31 - 1), thr, thr + 1))
    need = kf - n_gt
    tri = (lax.broadcasted_iota(jnp.int32, (_KC, _KC), 1)
           <= lax.broadcasted_iota(jnp.int32, (_KC, _KC), 0)).astype(BF16)

    acc_ref[...] = jnp.zeros_like(acc_ref)

    def attn_body(c, carry):
        ms, tie_carry = carry
        k0 = pl.multiple_of(c * _KC, _KC)
        kk = keys_ref[c]
        causal = kk > jnp.int32(INT_MIN)
        eq = (kk == thr) & causal
        eq_f = jnp.where(eq, 1.0, 0.0)
        rank = _dot(tri, eq_f.astype(BF16)) + tie_carry
        sel = ((kk > thr) & causal) | (eq & (rank <= need))
        tie_carry = tie_carry + _rows_reduce(eq_f, jnp.sum)
        d = i - 2 * c
        kc = k_ref[0, pl.ds(k0, _KC), :]
        k_heads = jnp.stack([kc[:, (h // 2) * LANES:(h // 2 + 1) * LANES] for h in range(DSA_HEADS)])
        s = jnp.einsum("hkd,hqd->hkq", k_heads, qm_ref[...], preferred_element_type=F32)
        bias = jnp.concatenate([bias_ref[d + 1], bias_ref[d]], axis=1)
        s = jnp.where(sel, s + bias, NEG)
        m_new = jnp.maximum(ms, _keys_reduce(s, jnp.max))
        alpha = jnp.exp(ms - m_new)
        pr = jnp.exp((s - m_new).astype(BF16))
        pv = jnp.einsum("hdk,hkq->hdq", vt_ref[0, c], pr, preferred_element_type=F32)
        acc_ref[...] = alpha * acc_ref[...] + pv
        return m_new, tie_carry

    init = (jnp.full((DSA_HEADS, 1, LANES), NEG, F32), jnp.zeros((1, LANES), F32))
    lax.fori_loop(0, nch, attn_body, init)
    acc = acc_ref[...]
    out_t = acc[:, :DSA_HEAD_DIM] / acc[:, DSA_HEAD_DIM:DSA_HEAD_DIM + 1]
    o_ref[0] = out_t.reshape(DSA_HEADS * DSA_HEAD_DIM, Q_BLOCK).T.astype(o_ref.dtype)


def _dsa_attention(h, bias_tiles):
    b, s, _ = h.shape
    width = DSA_HEADS * DSA_HEAD_DIM
    nblk = s // Q_BLOCK
    topk = min(DSA_TOPK_MAX, s // 4)
    kw_block = (CONV_CH * 2 + 4 * width) // LANES
    v_cols = 2 * CONV_CH + 2 * width
    vt = h[:, :, v_cols:v_cols + width].reshape(b, s // _KC, _KC, DSA_HEADS, DSA_HEAD_DIM).transpose(0, 1, 3, 4, 2)
    vt = jnp.concatenate([vt, jnp.ones((b, s // _KC, DSA_HEADS, _VT_ROWS - DSA_HEAD_DIM, _KC), BF16)], axis=3)
    return pl.pallas_call(
        functools.partial(_dsa_t_kernel, topk=topk),
        grid=(b, nblk),
        in_specs=[pl.BlockSpec((1, Q_BLOCK, width), lambda bi, qi: (bi, qi, 2)),
                  pl.BlockSpec((1, Q_BLOCK, width), lambda bi, qi: (bi, qi, 5)),
                  pl.BlockSpec((1, Q_BLOCK, LANES), lambda bi, qi: (bi, qi, kw_block)),
                  pl.BlockSpec((1, s, width), lambda bi, qi: (bi, 0, 3)),
                  pl.BlockSpec((1, s // _KC, DSA_HEADS, _VT_ROWS, _KC), lambda bi, qi: (bi, 0, 0, 0, 0)),
                  pl.BlockSpec((1, s, LANES), lambda bi, qi: (bi, 0, kw_block)),
                  pl.BlockSpec(memory_space=pl.ANY)],
        out_specs=pl.BlockSpec((1, Q_BLOCK, width), lambda bi, qi: (bi, qi, 0)),
        out_shape=jax.ShapeDtypeStruct((b, s, width), BF16),
        scratch_shapes=[pltpu.VMEM((nblk + 1, DSA_HEADS, Q_BLOCK, Q_BLOCK), F32),
                        pltpu.VMEM((s // _KC, _KC, Q_BLOCK), jnp.int32),
                        pltpu.VMEM((DSA_HEADS, Q_BLOCK, LANES), BF16),
                        pltpu.VMEM((DSA_HEADS, _VT_ROWS, Q_BLOCK), F32),
                        pltpu.SemaphoreType.DMA(())],
        compiler_params=pltpu.CompilerParams(dimension_semantics=("arbitrary", "arbitrary")),
        name="dsa_attention",
    )(h, h, h, h, vt, h, bias_tiles)


_DIL_WAVE = 8


def _dil_kernel(*refs, seq):
    n_g = len(DIL_GROUPS)
    qkv = refs[:3 * n_g]
    bias_ref, o_ref, og_ref, lg_ref = refs[3 * n_g:]
    c = pl.program_id(2)
    ii = lax.broadcasted_iota(jnp.int32, (BAND_BLOCK, BAND_BLOCK), 0)
    jj = lax.broadcasted_iota(jnp.int32, (BAND_BLOCK, BAND_BLOCK), 1)
    valid_prev = jj >= ii
    valid_cur = ii >= jj
    scale = DIL_HEAD_DIM ** -0.5
    for g, (window, dil) in enumerate(DIL_GROUPS):
        assert window // dil == BAND_BLOCK
        q_ref, k_ref, v_ref = qkv[3 * g:3 * g + 3]
        sub_len = seq // dil
        blocks_per_chunk = DIL_CHUNK // (BAND_BLOCK * dil)
        bias_prev = bias_ref[g, 0, :, :BAND_BLOCK]
        bias_cur = bias_ref[g, 0, :, BAND_BLOCK:]
        tiles = [(r, nl) for r in range(dil) for nl in range(blocks_per_chunk)]
        for w0 in range(0, len(tiles), _DIL_WAVE):
            wave = tiles[w0:w0 + _DIL_WAVE]
            cur, prev, mask_prev = [], [], []
            for r, nl in wave:
                row0 = pl.multiple_of(r * sub_len + c * (DIL_CHUNK // dil) + nl * BAND_BLOCK, BAND_BLOCK)
                cur.append(pl.ds(row0, BAND_BLOCK))
                prev.append(pl.ds(pl.multiple_of(jnp.maximum(row0 - BAND_BLOCK, 0), BAND_BLOCK), BAND_BLOCK))
                mask_prev.append(valid_prev if nl > 0 else (valid_prev & (c > 0)))
            gather = lambda ref, slices: jnp.stack([ref[0, sl, :] for sl in slices])
            qb = gather(q_ref, cur)
            s_cur = jnp.einsum("wqd,wkd->wqk", qb, gather(k_ref, cur), preferred_element_type=F32)
            s_prev = jnp.einsum("wqd,wkd->wqk", qb, gather(k_ref, prev), preferred_element_type=F32)
            s_cur = jnp.where(valid_cur, s_cur * scale + bias_cur, NEG)
            s_prev = jnp.where(jnp.stack(mask_prev), s_prev * scale + bias_prev, NEG)
            m = jnp.max(jnp.maximum(s_cur, s_prev), axis=2, keepdims=True)
            p_cur = jnp.exp(s_cur - m)
            p_prev = jnp.exp(s_prev - m)
            ssum = jnp.sum(p_cur + p_prev, axis=2, keepdims=True)
            o = (jnp.einsum("wqk,wkd->wqd", p_cur.astype(BF16), gather(v_ref, cur), preferred_element_type=F32)
                 + jnp.einsum("wqk,wkd->wqd", p_prev.astype(BF16), gather(v_ref, prev),
                              preferred_element_type=F32)) / ssum
            lse = m + jnp.log(ssum)
            for w, (r, nl) in enumerate(wave):
                dst = pl.ds(r + nl * BAND_BLOCK * dil, BAND_BLOCK, stride=dil) if dil > 1 else pl.ds(
                    nl * BAND_BLOCK, BAND_BLOCK)
                og_ref[g, dst, :] = o[w]
                lg_ref[g, dst, :] = jnp.broadcast_to(lse[w], (BAND_BLOCK, LANES))
    rows = 256
    for rc in range(DIL_CHUNK // rows):
        sl = slice(rc * rows, (rc + 1) * rows)
        lses = [lg_ref[g, sl, :] for g in range(n_g)]
        mx = functools.reduce(jnp.maximum, lses)
        ws = [jnp.exp(l - mx) for l in lses]
        den = functools.reduce(lambda a, b_: a + b_, ws)
        num = functools.reduce(lambda a, b_: a + b_, [ws[g] * og_ref[g, sl, :] for g in range(n_g)])
        o_ref[0, sl, :] = (num / den).astype(o_ref.dtype)


def _dilated_attention(hs, bias_tiles):
    b, s, _ = hs[0].shape
    in_specs = []
    args = []
    for hg in hs:
        for part in range(3):
            in_specs.append(pl.BlockSpec((1, s, DIL_HEAD_DIM),
                                         lambda bi, hi, ci, part=part: (bi, 0, part * DIL_HEADS + hi)))
            args.append(hg)
    in_specs.append(pl.BlockSpec((len(DIL_GROUPS), 1, BAND_BLOCK, 2 * BAND_BLOCK),
                                 lambda bi, hi, ci: (0, hi, 0, 0)))
    return pl.pallas_call(
        functools.partial(_dil_kernel, seq=s),
        grid=(b, DIL_HEADS, s // DIL_CHUNK),
        in_specs=in_specs,
        out_specs=pl.BlockSpec((1, DIL_CHUNK, DIL_HEAD_DIM), lambda bi, hi, ci: (bi, ci, hi)),
        out_shape=jax.ShapeDtypeStruct((b, s, DIL_HEADS * DIL_HEAD_DIM), BF16),
        scratch_shapes=[pltpu.VMEM((len(DIL_GROUPS), DIL_CHUNK, LANES), F32),
                        pltpu.VMEM((len(DIL_GROUPS), DIL_CHUNK, LANES), F32)],
        compiler_params=pltpu.CompilerParams(dimension_semantics=("parallel", "parallel", "arbitrary")),
        name="dilated_attention",
    )(*args, bias_tiles)


def _outproj_kernel(*refs, n_in, with_router):
    ins = refs[:n_in]
    ws = refs[n_in:2 * n_in]
    x_ref, g_ref, b_ref = refs[2 * n_in:2 * n_in + 3]
    rest = refs[2 * n_in + 3:]
    mix = _dot(ins[0][...], ws[0][...])
    for a_ref, w_ref in zip(ins[1:], ws[1:]):
        mix = mix + _dot(a_ref[...], w_ref[...])
    xn = _layer_norm(DEEPNORM_ALPHA * x_ref[...] + mix, g_ref[...], b_ref[...])
    if not with_router:
        o_ref, ob_ref = rest
    else:
        router_ref, o_ref, ob_ref, gates_ref, counts_ref, run_ref, tri_ref = rest
        tm = xn.shape[0]

        @pl.when(pl.program_id(0) == 0)
        def _():
            run_ref[...] = jnp.zeros_like(run_ref)
            tri_ref[...] = (lax.broadcasted_iota(jnp.int32, (tm, tm), 0)
                            > lax.broadcasted_iota(jnp.int32, (tm, tm), 1)).astype(BF16)

        logits = _dot(xn.astype(BF16), router_ref[...])
        lane = lax.broadcasted_iota(jnp.int32, logits.shape, 1).astype(F32)
        lg = jnp.where(lane < N_EXPERTS, logits, -jnp.inf)
        m1 = jnp.max(lg, axis=1, keepdims=True)
        i1 = jnp.min(jnp.where(lg == m1, lane, float(LANES)), axis=1, keepdims=True)
        lg2 = jnp.where(lane == i1, -jnp.inf, lg)
        m2 = jnp.max(lg2, axis=1, keepdims=True)
        i2 = jnp.min(jnp.where(lg2 == m2, lane, float(LANES)), axis=1, keepdims=True)
        e2 = jnp.exp(m2 - m1)
        den = 1.0 + e2
        oh1 = jnp.where(lane == i1, 1.0, 0.0)
        oh2 = jnp.where(lane == i2, 1.0, 0.0)
        both = oh1 + oh2
        before = _dot(tri_ref[...], both.astype(BF16)) + run_ref[...]
        r1 = jnp.sum(before * oh1, axis=1, keepdims=True)
        r2 = jnp.sum(before * oh2, axis=1, keepdims=True)
        run_ref[...] += jnp.sum(both, axis=0, keepdims=True)
        counts_ref[...] = run_ref[...]
        gates_ref[...] = (jnp.where(lane == 0.0, i1, 0.0) + jnp.where(lane == 1.0, i2, 0.0)
                          + jnp.where(lane == 2.0, 1.0 / den, 0.0) + jnp.where(lane == 3.0, e2 / den, 0.0)
                          + jnp.where(lane == 4.0, r1, 0.0) + jnp.where(lane == 5.0, r2, 0.0))
    o_ref[...] = xn
    ob_ref[...] = xn.astype(BF16)


def _outproj_ln(ins, ws, x, ln_g, ln_b, router=None, tm=512):
    t, d = x.shape
    n_in = len(ins)
    row = lambda width: pl.BlockSpec((tm, width), lambda i: (i, 0))
    full = lambda a: pl.BlockSpec(a.shape, lambda i: (0, 0))
    vec = lambda v: v.reshape(1, d).astype(F32)
    args = list(ins) + list(ws) + [x, vec(ln_g), vec(ln_b)]
    in_specs = [row(a.shape[1]) for a in ins] + [full(w) for w in ws] + [row(d), full(vec(ln_g)), full(vec(ln_b))]
    out_shape = [jax.ShapeDtypeStruct((t, d), F32), jax.ShapeDtypeStruct((t, d), BF16)]
    out_specs = [row(d), row(d)]
    if router is not None:
        router_pad = jnp.zeros((d, LANES), BF16).at[:, :N_EXPERTS].set(router.astype(BF16))
        args.append(router_pad)
        in_specs.append(full(router_pad))
        out_shape += [jax.ShapeDtypeStruct((t, LANES), F32), jax.ShapeDtypeStruct((1, LANES), F32)]
        out_specs += [row(LANES), pl.BlockSpec((1, LANES), lambda i: (0, 0))]
        scratch = [pltpu.VMEM((1, LANES), F32), pltpu.VMEM((tm, tm), BF16)]
    else:
        scratch = []
    semantics = "arbitrary" if router is not None else "parallel"
    return pl.pallas_call(
        functools.partial(_outproj_kernel, n_in=n_in, with_router=router is not None),
        grid=(t // tm,),
        in_specs=in_specs,
        out_specs=out_specs,
        out_shape=out_shape,
        scratch_shapes=scratch,
        compiler_params=pltpu.CompilerParams(dimension_semantics=(semantics,)),
        name="outproj_ln",
    )(*args)


def _ffn_kernel(xb_ref, x_ref, wg_ref, wu_ref, wd_ref, g_ref, b_ref, o_ref, ob_ref, acc_ref):
    f = pl.program_id(1)

    @pl.when(f == 0)
    def _():
        acc_ref[...] = jnp.zeros_like(acc_ref)

    xb = xb_ref[...]
    hg = _dot(xb, wg_ref[...])
    hu = _dot(xb, wu_ref[...])
    act = (hg * _sigmoid(hg) * hu).astype(BF16)
    acc_ref[...] += _dot(act, wd_ref[...])

    @pl.when(f == pl.num_programs(1) - 1)
    def _():
        xn = _layer_norm(DEEPNORM_ALPHA * x_ref[...] + acc_ref[...], g_ref[...], b_ref[...])
        o_ref[...] = xn
        ob_ref[...] = xn.astype(BF16)


def _ffn_ln(xb, x, wg, wu, wd, ln_g, ln_b, tm=512, tf=1408):
    t, d = x.shape
    ff = wg.shape[1]
    vec = lambda v: v.reshape(1, d).astype(F32)
    return pl.pallas_call(
        _ffn_kernel,
        grid=(t // tm, ff // tf),
        in_specs=[pl.BlockSpec((tm, d), lambda i, f: (i, 0)),
                  pl.BlockSpec((tm, d), lambda i, f: (i, 0)),
                  pl.BlockSpec((d, tf), lambda i, f: (0, f)),
                  pl.BlockSpec((d, tf), lambda i, f: (0, f)),
                  pl.BlockSpec((tf, d), lambda i, f: (f, 0)),
                  pl.BlockSpec((1, d), lambda i, f: (0, 0)),
                  pl.BlockSpec((1, d), lambda i, f: (0, 0))],
        out_specs=[pl.BlockSpec((tm, d), lambda i, f: (i, 0)),
                   pl.BlockSpec((tm, d), lambda i, f: (i, 0))],
        out_shape=[jax.ShapeDtypeStruct((t, d), F32), jax.ShapeDtypeStruct((t, d), BF16)],
        scratch_shapes=[pltpu.VMEM((tm, d), F32)],
        compiler_params=pltpu.CompilerParams(dimension_semantics=("parallel", "arbitrary")),
        name="ffn_ln",
    )(xb, x, wg, wu, wd, vec(ln_g), vec(ln_b))


_MOE_TM = 1024


def _route(route, counts, n_tiles):
    expert = route[:, 0:2].astype(jnp.int32)
    rank = route[:, 4:6].astype(jnp.int32)
    counts = counts[0, :N_EXPERTS].astype(jnp.int32)
    padded = (counts + _MOE_TM - 1) // _MOE_TM * _MOE_TM
    ends = jnp.cumsum(padded)
    starts = ends - padded
    onehot = expert[:, :, None] == jnp.arange(N_EXPERTS, dtype=jnp.int32)
    pos = (rank + jnp.sum(jnp.where(onehot, starts, 0), axis=-1)).reshape(-1)
    tile_start = jnp.arange(n_tiles, dtype=jnp.int32) * _MOE_TM
    tile_expert = jnp.minimum(jnp.sum((tile_start[:, None] >= ends[None, :]).astype(jnp.int32), axis=1),
                              N_EXPERTS - 1)
    return pos.astype(jnp.int32), tile_expert.astype(jnp.int32)


def _row_scatter_kernel(pos_ref, x_ref, init_ref, o_ref, sem, *, tm):
    del init_ref

    def body(j, carry):
        for choice in range(2):
            p = pos_ref[0, 0, 2 * j + choice]
            pltpu.make_async_copy(x_ref.at[pl.ds(j, 1)], o_ref.at[pl.ds(p, 1)], sem).start(priority=choice)
        return carry

    lax.fori_loop(0, tm, body, 0, unroll=8)
    for _ in range(2):
        pltpu.make_async_copy(x_ref, o_ref.at[pl.ds(0, tm)], sem).wait()


def _row_scatter(x, pos, n_rows, tm=512):
    t, d = x.shape
    return pl.pallas_call(
        functools.partial(_row_scatter_kernel, tm=tm),
        grid=(t // tm,),
        in_specs=[pl.BlockSpec((1, 1, 2 * tm), lambda i: (i, 0, 0), memory_space=pltpu.SMEM),
                  pl.BlockSpec((tm, d), lambda i: (i, 0)),
                  pl.BlockSpec(memory_space=pl.ANY)],
        out_specs=pl.BlockSpec(memory_space=pl.ANY),
        out_shape=jax.ShapeDtypeStruct((n_rows, d), x.dtype),
        scratch_shapes=[pltpu.SemaphoreType.DMA(())],
        input_output_aliases={2: 0},
        compiler_params=pltpu.CompilerParams(dimension_semantics=("arbitrary",)),
        name="moe_row_scatter",
    )(pos.reshape(t // tm, 1, 2 * tm), x, jnp.zeros((n_rows, d), x.dtype))


def _expert_ffn_kernel(te_ref, x_ref, wg_ref, wu_ref, wd_ref, o_ref, acc_ref):
    del te_ref
    f = pl.program_id(1)

    @pl.when(f == 0)
    def _():
        acc_ref[...] = jnp.zeros_like(acc_ref)

    xb = x_ref[...].astype(BF16)
    hg = _dot(xb, wg_ref[0].astype(BF16))
    hu = _dot(xb, wu_ref[0].astype(BF16))
    act = (hg * _sigmoid(hg) * hu).astype(BF16)
    acc_ref[...] += _dot(act, wd_ref[0].astype(BF16))

    @pl.when(f == pl.num_programs(1) - 1)
    def _():
        o_ref[...] = acc_ref[...]


def _expert_ffn(xs, tile_expert, wg, wu, wd, tf=512):
    n_rows, d = xs.shape
    ff = wg.shape[2]
    grid_spec = pltpu.PrefetchScalarGridSpec(
        num_scalar_prefetch=1,
        grid=(n_rows // _MOE_TM, ff // tf),
        in_specs=[pl.BlockSpec((_MOE_TM, d), lambda i, f, te: (i, 0)),
                  pl.BlockSpec((1, d, tf), lambda i, f, te: (te[i], 0, f)),
                  pl.BlockSpec((1, d, tf), lambda i, f, te: (te[i], 0, f)),
                  pl.BlockSpec((1, tf, d), lambda i, f, te: (te[i], f, 0))],
        out_specs=pl.BlockSpec((_MOE_TM, d), lambda i, f, te: (i, 0)),
        scratch_shapes=[pltpu.VMEM((_MOE_TM, d), F32)])
    return pl.pallas_call(
        _expert_ffn_kernel,
        grid_spec=grid_spec,
        out_shape=jax.ShapeDtypeStruct((n_rows, d), F32),
        compiler_params=pltpu.CompilerParams(dimension_semantics=("parallel", "arbitrary")),
        name="moe_expert_ffn",
    )(tile_expert, xs, wg, wu, wd)


def _combine_kernel(pos_ref, route_ref, x_ref, g_ref, b_ref, y_ref, o_ref, ybuf_ref, sem, *, tm):
    def body(j, carry):
        for choice in range(2):
            p = pos_ref[0, 0, 2 * j + choice]
            pltpu.make_async_copy(y_ref.at[pl.ds(p, 1)], ybuf_ref.at[choice, pl.ds(j, 1)],
                                  sem).start(priority=choice)
        return carry

    lax.fori_loop(0, tm, body, 0, unroll=8)
    for choice in range(2):
        pltpu.make_async_copy(y_ref.at[pl.ds(0, tm)], ybuf_ref.at[choice], sem).wait()
    route = route_ref[...]
    y = route[:, 2:3] * ybuf_ref[0] + route[:, 3:4] * ybuf_ref[1]
    o_ref[...] = _layer_norm(DEEPNORM_ALPHA * x_ref[...] + y, g_ref[...], b_ref[...])


def _combine_ln(ys, pos, route, x, ln_g, ln_b, tm=512):
    t, d = x.shape
    vec = lambda v: v.reshape(1, d).astype(F32)
    return pl.pallas_call(
        functools.partial(_combine_kernel, tm=tm),
        grid=(t // tm,),
        in_specs=[pl.BlockSpec((1, 1, 2 * tm), lambda i: (i, 0, 0), memory_space=pltpu.SMEM),
                  pl.BlockSpec((tm, LANES), lambda i: (i, 0)),
                  pl.BlockSpec((tm, d), lambda i: (i, 0)),
                  pl.BlockSpec((1, d), lambda i: (0, 0)),
                  pl.BlockSpec((1, d), lambda i: (0, 0)),
                  pl.BlockSpec(memory_space=pl.ANY)],
        out_specs=pl.BlockSpec((tm, d), lambda i: (i, 0)),
        out_shape=jax.ShapeDtypeStruct((t, d), F32),
        scratch_shapes=[pltpu.VMEM((2, tm, d), F32), pltpu.SemaphoreType.DMA(())],
        compiler_params=pltpu.CompilerParams(dimension_semantics=("arbitrary",)),
        name="moe_combine_ln",
    )(pos.reshape(t // tm, 1, 2 * tm), route, x, vec(ln_g), vec(ln_b), ys)


def _moe_ln(x, route, counts, wg, wu, wd, ln_g, ln_b):
    t, _ = x.shape
    n_tiles = 2 * t // _MOE_TM + N_EXPERTS
    pos, tile_expert = _route(route, counts, n_tiles)
    xs = _row_scatter(x, pos, n_tiles * _MOE_TM)
    ys = _expert_ffn(xs, tile_expert, wg, wu, wd)
    return _combine_ln(ys, pos, route, x, ln_g, ln_b)


def _even_layer(x, xb, batch, seq, dsa_bias, w_in, conv_w, conv_b, conv_ln_g, conv_ln_b, w_out,
                ln1_g, ln1_b, ffn_wg, ffn_wu, ffn_wd, ln2_g, ln2_b):
    w_in_p = jnp.zeros((D_MODEL, EVEN_IN_PAD), BF16).at[:, :w_in.shape[1]].set(w_in.astype(BF16))
    h = _matmul(xb, w_in_p, 1024, 640, BF16).reshape(batch, seq, EVEN_IN_PAD)
    a_out = _conformer_conv(h, conv_w, conv_b, conv_ln_g, conv_ln_b)
    att = _dsa_attention(h, dsa_bias)
    w_out_b = w_out.astype(BF16)
    x1, x1b = _outproj_ln([a_out.reshape(-1, CONV_CH), att.reshape(-1, DSA_HEADS * DSA_HEAD_DIM)],
                          [w_out_b[:CONV_CH], w_out_b[CONV_CH:]], x, ln1_g, ln1_b)
    return _ffn_ln(x1b, x1, ffn_wg.astype(BF16), ffn_wu.astype(BF16), ffn_wd.astype(BF16), ln2_g, ln2_b)


def _odd_layer(x, xb, batch, seq, dil_bias, w_in, w_out, ln1_g, ln1_b, router, moe_wg, moe_wu, moe_wd,
               ln2_g, ln2_b):
    gw = 3 * DIL_HEADS * DIL_HEAD_DIM
    w_in_b = w_in.astype(BF16)
    hs = [_matmul_residue_major(xb, w_in_b[:, g * gw:(g + 1) * gw], batch, seq, dil, 1024, 768)
          for g, (_, dil) in enumerate(DIL_GROUPS)]
    o = _dilated_attention(hs, dil_bias).reshape(-1, DIL_HEADS * DIL_HEAD_DIM)
    x1, _, route, counts = _outproj_ln([o], [w_out.astype(BF16)], x, ln1_g, ln1_b, router=router)
    out = _moe_ln(x1, route, counts, moe_wg, moe_wu, moe_wd, ln2_g, ln2_b)
    return out, out.astype(BF16)


def kernel(x, rel_bias, even_w_in, even_conv_w, even_conv_b, even_conv_ln_g, even_conv_ln_b, even_w_out, even_ln1_g, even_ln1_b, even_ffn_wg, even_ffn_wu, even_ffn_wd, even_ln2_g, even_ln2_b, odd_w_in, odd_w_out, odd_ln1_g, odd_ln1_b, odd_router, odd_moe_wg, odd_moe_wu, odd_moe_wd, odd_ln2_g, odd_ln2_b):
    batch, seq, d = x.shape
    assert d == D_MODEL and seq % DIL_CHUNK == 0
    depth = even_w_in.shape[0] + odd_w_in.shape[0]
    assert depth == DEPTH
    dsa_bias = _dsa_bias_tiles(rel_bias, seq)
    dil_bias = _dil_bias_tiles(rel_bias)
    xf = x.reshape(batch * seq, d)
    xb = xf.astype(BF16)
    for layer in range(depth):
        i = layer // 2
        if layer % 2 == 0:
            xf, xb = _even_layer(xf, xb, batch, seq, dsa_bias, even_w_in[i], even_conv_w[i], even_conv_b[i],
                                 even_conv_ln_g[i], even_conv_ln_b[i], even_w_out[i], even_ln1_g[i],
                                 even_ln1_b[i], even_ffn_wg[i], even_ffn_wu[i], even_ffn_wd[i],
                                 even_ln2_g[i], even_ln2_b[i])
        else:
            xf, xb = _odd_layer(xf, xb, batch, seq, dil_bias, odd_w_in[i], odd_w_out[i], odd_ln1_g[i],
                                odd_ln1_b[i], odd_router[i], odd_moe_wg[i], odd_moe_wu[i], odd_moe_wd[i],
                                odd_ln2_g[i], odd_ln2_b[i])
    return xf.reshape(batch, seq, d)
```

```python
import functools
import math

import jax
import jax.numpy as jnp
from jax import lax
from jax.experimental import pallas as pl
from jax.experimental.pallas import tpu as pltpu

BF16 = jnp.bfloat16
F32 = jnp.float32

D_MODEL = 1024
CONV_CH = 512
CONV_WIDTH = 31
DSA_HEADS = 8
DSA_HEAD_DIM = 64
IDX_HEADS = 8
IDX_DIM = 64
DSA_TOPK_MAX = 256
Q_BLOCK = 128
DIL_GROUPS = ((128, 1), (512, 4), (2048, 16))
DIL_HEADS = 8
DIL_HEAD_DIM = 128
BAND_BLOCK = 128
NUM_BUCKETS = 32
MAX_DISTANCE = 2048
N_EXPERTS = 8
LN_EPS = 1e-5
NEG = -1e30
DEPTH = 2
DEEPNORM_ALPHA = (2 * DEPTH) ** 0.25

LANES = 128
EVEN_IN_PAD = 3200
INT_MIN = -(2 ** 31)
DIL_CHUNK = BAND_BLOCK * max(d for _, d in DIL_GROUPS)

_PARALLEL2 = pltpu.CompilerParams(dimension_semantics=("parallel", "parallel"))


def _layer_norm(y, g, b):
    mu = jnp.mean(y, axis=-1, keepdims=True)
    yc = y - mu
    var = jnp.mean(yc * yc, axis=-1, keepdims=True)
    return yc * lax.rsqrt(var + LN_EPS) * g + b


def _sigmoid(x):
    return 1.0 / (1.0 + jnp.exp(-x))


def _dot_nt(a, b):
    return lax.dot_general(a, b, (((1,), (1,)), ((), ())), preferred_element_type=F32)


def _dot(a, b):
    return jnp.dot(a, b, preferred_element_type=F32)


def _mm_kernel(x_ref, w_ref, o_ref):
    o_ref[...] = _dot(x_ref[...], w_ref[...]).astype(o_ref.dtype)


def _matmul(x, w, tm, tn, out_dtype):
    t, k = x.shape
    n = w.shape[1]
    return pl.pallas_call(
        _mm_kernel,
        grid=(t // tm, n // tn),
        in_specs=[pl.BlockSpec((tm, k), lambda i, j: (i, 0)),
                  pl.BlockSpec((k, tn), lambda i, j: (0, j))],
        out_specs=pl.BlockSpec((tm, tn), lambda i, j: (i, j)),
        out_shape=jax.ShapeDtypeStruct((t, n), out_dtype),
        compiler_params=_PARALLEL2,
        name="matmul",
    )(x, w)


def _mm_perm_kernel(x_ref, w_ref, o_ref, acc_ref, *, dil, tm, tn):
    acc = _dot(x_ref[...], w_ref[...])
    rows = tm // dil
    for c in range(tn // LANES):
        acc_ref[c] = acc[:, c * LANES:(c + 1) * LANES]
    for r in range(dil):
        for c in range(tn // LANES):
            o_ref[0, r, :, c * LANES:(c + 1) * LANES] = (
                acc_ref[c, pl.ds(r, rows, stride=dil), :].astype(o_ref.dtype))


def _matmul_residue_major(x, w, batch, seq, dil, tm, tn):
    t, k = x.shape
    n = w.shape[1]
    if dil == 1:
        return _matmul(x, w, tm, tn, BF16).reshape(batch, seq, n)
    tiles_per_seq = seq // tm
    out = pl.pallas_call(
        functools.partial(_mm_perm_kernel, dil=dil, tm=tm, tn=tn),
        grid=(t // tm, n // tn),
        in_specs=[pl.BlockSpec((tm, k), lambda i, j: (i, 0)),
                  pl.BlockSpec((k, tn), lambda i, j: (0, j))],
        out_specs=pl.BlockSpec((1, dil, tm // dil, tn),
                               lambda i, j: (i // tiles_per_seq, 0, i % tiles_per_seq, j)),
        out_shape=jax.ShapeDtypeStruct((batch, dil, seq // dil, n), BF16),
        scratch_shapes=[pltpu.VMEM((tn // LANES, tm, LANES), F32)],
        compiler_params=_PARALLEL2,
        name="matmul_residue_major",
    )(x, w)
    return out.reshape(batch, seq, n)


_CONV_HALO = 32
_CONV_ROWS = 64


def _conv_kernel(val_ref, gate_ref, w_ref, cb_ref, g_ref, b_ref, o_ref, ext_ref, *, ts):
    s = pl.program_id(1)

    @pl.when(s == 0)
    def _():
        ext_ref[0:_CONV_HALO, :] = jnp.zeros((_CONV_HALO, CONV_CH), F32)

    val = val_ref[0].astype(F32)
    gate = gate_ref[0].astype(F32)
    ext_ref[_CONV_HALO:_CONV_HALO + ts, :] = val * _sigmoid(gate)
    first_tap = _CONV_HALO - (CONV_WIDTH - 1)
    for rc in range(ts // _CONV_ROWS):
        r0 = rc * _CONV_ROWS
        acc = jnp.zeros((_CONV_ROWS, CONV_CH), F32)
        for j in range(CONV_WIDTH):
            lo = r0 + first_tap + j
            acc = acc + ext_ref[lo:lo + _CONV_ROWS, :] * w_ref[j:j + 1, :]
        y = _layer_norm(acc + cb_ref[...], g_ref[...], b_ref[...])
        o_ref[0, r0:r0 + _CONV_ROWS, :] = (y * _sigmoid(y)).astype(o_ref.dtype)
    ext_ref[0:_CONV_HALO, :] = ext_ref[ts:ts + _CONV_HALO, :]


def _conformer_conv(h, conv_w, conv_b, ln_g, ln_b, ts=512):
    b, s, _ = h.shape
    w_pad = jnp.zeros((32, CONV_CH), F32).at[:CONV_WIDTH].set(conv_w)
    vec = lambda v: v.reshape(1, CONV_CH).astype(F32)
    full = lambda shape: pl.BlockSpec(shape, lambda bi, si: (0, 0))
    return pl.pallas_call(
        functools.partial(_conv_kernel, ts=ts),
        grid=(b, s // ts),
        in_specs=[pl.BlockSpec((1, ts, CONV_CH), lambda bi, si: (bi, si, 0)),
                  pl.BlockSpec((1, ts, CONV_CH), lambda bi, si: (bi, si, 1)),
                  full((32, CONV_CH)), full((1, CONV_CH)), full((1, CONV_CH)), full((1, CONV_CH))],
        out_specs=pl.BlockSpec((1, ts, CONV_CH), lambda bi, si: (bi, si, 0)),
        out_shape=jax.ShapeDtypeStruct((b, s, CONV_CH), BF16),
        scratch_shapes=[pltpu.VMEM((ts + _CONV_HALO, CONV_CH), F32)],
        compiler_params=pltpu.CompilerParams(dimension_semantics=("parallel", "arbitrary")),
        name="conformer_conv",
    )(h, h, w_pad, vec(conv_b), vec(ln_g), vec(ln_b))


def _rel_bucket(dist):
    max_exact = NUM_BUCKETS // 2
    n = dist.astype(jnp.int32)
    nf = jnp.maximum(n, 1).astype(F32)
    large = max_exact + (jnp.log(nf / max_exact) / math.log(MAX_DISTANCE / max_exact)
                         * (NUM_BUCKETS - max_exact)).astype(jnp.int32)
    large = jnp.minimum(large, NUM_BUCKETS - 1)
    return jnp.where(n < max_exact, n, large)


def _bias_by_distance(rel_bias, n):
    return rel_bias[_rel_bucket(jnp.arange(n, dtype=jnp.int32))].astype(F32).T


def _toeplitz_kernel(win_ref, o_ref):
    n_win, span = win_ref.shape
    for n in range(n_win):
        y = jnp.broadcast_to(win_ref[n:n + 1, :], (LANES, span))
        y = pltpu.roll(y, 0, 1, stride=1, stride_axis=0)
        o_ref[n] = y[:, LANES:]


def _toeplitz_tiles(windows, per_step=8):
    n, span = windows.shape
    width = span - LANES
    return pl.pallas_call(
        _toeplitz_kernel,
        grid=(n // per_step,),
        in_specs=[pl.BlockSpec((per_step, span), lambda i: (i, 0))],
        out_specs=pl.BlockSpec((per_step, LANES, width), lambda i: (i, 0, 0)),
        out_shape=jax.ShapeDtypeStruct((n, LANES, width), F32),
        compiler_params=pltpu.CompilerParams(dimension_semantics=("parallel",)),
        name="bias_toeplitz",
    )(windows)


def _dsa_bias_tiles(rel_bias, seq):
    nblk = seq // Q_BLOCK
    heads = rel_bias.shape[1]
    f = _bias_by_distance(rel_bias, seq)
    u = jnp.concatenate([jnp.broadcast_to(f[:, :1], (heads, 2 * Q_BLOCK)), f], axis=1)
    blocks = u.reshape(heads, nblk + 2, Q_BLOCK)
    windows = jnp.concatenate([blocks[:, :-1], blocks[:, 1:]], axis=-1)
    windows = windows.transpose(1, 0, 2).reshape((nblk + 1) * heads, 2 * Q_BLOCK)
    return _toeplitz_tiles(windows).reshape(nblk + 1, heads, Q_BLOCK, Q_BLOCK)


def _dil_bias_tiles(rel_bias):
    max_dil = max(d for _, d in DIL_GROUPS)
    heads = rel_bias.shape[1]
    span = 2 * BAND_BLOCK
    f = _bias_by_distance(rel_bias, span * max_dil + 1)
    windows = []
    for _, dil in DIL_GROUPS:
        fd = f[:, 0:span * dil + 1:dil]
        tail = jnp.broadcast_to(fd[:, :1], (heads, BAND_BLOCK - 1))
        windows.append(jnp.concatenate([fd[:, ::-1], tail], axis=1))
    tiles = _toeplitz_tiles(jnp.concatenate(windows, axis=0))
    return tiles.reshape(len(DIL_GROUPS), heads, BAND_BLOCK, span)


_KC = 2 * Q_BLOCK
_VT_ROWS = DSA_HEAD_DIM + 16


def _sortable_key(score):
    score = jnp.where(score == 0.0, 0.0, score)
    bits = lax.bitcast_convert_type(score, jnp.int32)
    return jnp.where(bits < 0, bits ^ jnp.int32(0x7FFFFFFF), bits)


def _rows_reduce(x, op):
    rows = x.shape[0]
    return op(op(x.reshape(rows // 8, 8, LANES), axis=0), axis=0, keepdims=True)


def _keys_reduce(x, op):
    h, keys, _ = x.shape
    return op(op(x.reshape(h, keys // 8, 8, LANES), axis=1), axis=1, keepdims=True)


def _dsa_t_kernel(q_ref, qi_ref, wq_ref, k_ref, vt_ref, ki_ref, bias_hbm, o_ref,
                  bias_ref, keys_ref, qm_ref, acc_ref, dots_a_ref, dots_b_ref, sem, *, topk):
    b = pl.program_id(0)
    i = pl.program_id(1)

    @pl.when((b == 0) & (i == 0))
    def _():
        cp = pltpu.make_async_copy(bias_hbm, bias_ref, sem)
        cp.start()
        cp.wait()

    nch = (i + 2) // 2
    key_off = lax.broadcasted_iota(jnp.int32, (_KC, LANES), 0)
    q_pos = i * Q_BLOCK + lax.broadcasted_iota(jnp.int32, (_KC, LANES), 1)
    lane = lax.broadcasted_iota(jnp.int32, (Q_BLOCK, LANES), 1)
    low_half = lane < DSA_HEAD_DIM

    w_t = (wq_ref[0].astype(F32) * (IDX_HEADS ** -0.5 * IDX_DIM ** -0.5)).T
    qi = qi_ref[0].astype(F32)
    qi_heads = []
    for p in range(IDX_HEADS // 2):
        pair = qi[:, p * LANES:(p + 1) * LANES]
        qi_heads.append(jnp.where(low_half, pair, 0.0))
        qi_heads.append(jnp.where(low_half, pltpu.roll(pair, IDX_DIM, 1), 0.0))
    qi_stack = jnp.concatenate(qi_heads, axis=0).astype(BF16)
    q_scaled = (q_ref[0].astype(F32) * (DSA_HEAD_DIM ** -0.5)).astype(BF16)
    for p in range(DSA_HEADS // 2):
        qp = q_scaled[:, p * LANES:(p + 1) * LANES]
        qm_ref[2 * p] = jnp.where(low_half, qp, jnp.zeros_like(qp))
        qm_ref[2 * p + 1] = jnp.where(low_half, jnp.zeros_like(qp), qp)

    def index_dots(c, dst_ref):
        c = jnp.minimum(c, nch - 1)
        k0 = pl.multiple_of(c * _KC, _KC)
        ki = ki_ref[0, pl.ds(k0, _KC), :]
        dst_ref[...] = _dot_nt(ki, qi_stack)

    def consume_dots(c, src_ref):
        c = jnp.minimum(c, nch - 1)
        k0 = pl.multiple_of(c * _KC, _KC)
        score = jnp.zeros((_KC, LANES), F32)
        for h in range(IDX_HEADS):
            w_row = w_t[IDX_DIM + h:IDX_DIM + h + 1, :]
            score = score + w_row * jnp.maximum(src_ref[:, h * Q_BLOCK:(h + 1) * Q_BLOCK], 0.0)
        keys_ref[c] = jnp.where(k0 + key_off <= q_pos, _sortable_key(score), jnp.int32(INT_MIN))

    index_dots(0, dots_a_ref)

    def score_body(c2, carry):
        index_dots(2 * c2 + 1, dots_b_ref)
        consume_dots(2 * c2, dots_a_ref)
        index_dots(2 * c2 + 2, dots_a_ref)
        consume_dots(2 * c2 + 1, dots_b_ref)
        return carry

    lax.fori_loop(0, (nch + 1) // 2, score_body, 0)

    def count_ge(cand):
        cb = jnp.broadcast_to(cand, (8, LANES))
        chains = 4

        def body(c, acc):
            kk = keys_ref[c].reshape(_KC // (8 * chains), chains, 8, LANES)
            return acc + jnp.sum(jnp.where(kk >= cb, 1.0, 0.0), axis=0)

        acc = lax.fori_loop(0, nch, body, jnp.zeros((chains, 8, LANES), F32))
        return jnp.sum(jnp.sum(acc, axis=0), axis=0, keepdims=True)

    kf = float(topk)
    zero = jnp.zeros((1, LANES), jnp.int32)
    thr0 = jnp.where(count_ge(zero) >= kf, zero, jnp.full((1, LANES), INT_MIN, jnp.int32))

    def bit_body(t, thr):
        cand = thr | jnp.left_shift(jnp.int32(1), 30 - t)
        return jnp.where(count_ge(cand) >= kf, cand, thr)

    thr = lax.fori_loop(0, 31, bit_body, thr0)
    n_gt = count_ge(jnp.where(thr == jnp.int32(2 ** 31 - 1), thr, thr + 1))
    need = kf - n_gt
    tri = (lax.broadcasted_iota(jnp.int32, (_KC, _KC), 1)
           <= lax.broadcasted_iota(jnp.int32, (_KC, _KC), 0)).astype(BF16)

    acc_ref[...] = jnp.zeros_like(acc_ref)

    def attn_body(c, carry):
        ms, tie_carry = carry
        k0 = pl.multiple_of(c * _KC, _KC)
        kk = keys_ref[c]
        causal = kk > jnp.int32(INT_MIN)
        eq = (kk == thr) & causal
        eq_f = jnp.where(eq, 1.0, 0.0)
        rank = _dot(tri, eq_f.astype(BF16)) + tie_carry
        sel = ((kk > thr) & causal) | (eq & (rank <= need))
        tie_carry = tie_carry + _rows_reduce(eq_f, jnp.sum)
        d = i - 2 * c
        kc = k_ref[0, pl.ds(k0, _KC), :]
        k_heads = jnp.stack([kc[:, (h // 2) * LANES:(h // 2 + 1) * LANES] for h in range(DSA_HEADS)])
        s = jnp.einsum("hkd,hqd->hkq", k_heads, qm_ref[...], preferred_element_type=F32)
        bias = jnp.concatenate([bias_ref[d + 1], bias_ref[d]], axis=1)
        s = jnp.where(sel, s + bias, NEG)
        m_new = jnp.maximum(ms, _keys_reduce(s, jnp.max))
        alpha = jnp.exp(ms - m_new)
        pr = jnp.exp((s - m_new).astype(BF16))
        pv = jnp.einsum("hdk,hkq->hdq", vt_ref[0, c], pr, preferred_element_type=F32)
        acc_ref[...] = alpha * acc_ref[...] + pv
        return m_new, tie_carry

    init = (jnp.full((DSA_HEADS, 1, LANES), NEG, F32), jnp.zeros((1, LANES), F32))
    lax.fori_loop(0, nch, attn_body, init)
    acc = acc_ref[...]
    out_t = acc[:, :DSA_HEAD_DIM] / acc[:, DSA_HEAD_DIM:DSA_HEAD_DIM + 1]
    o_ref[0] = out_t.reshape(DSA_HEADS * DSA_HEAD_DIM, Q_BLOCK).T.astype(o_ref.dtype)


def _dsa_attention(h, bias_tiles):
    b, s, _ = h.shape
    width = DSA_HEADS * DSA_HEAD_DIM
    nblk = s // Q_BLOCK
    topk = min(DSA_TOPK_MAX, s // 4)
    kw_block = (CONV_CH * 2 + 4 * width) // LANES
    v_cols = 2 * CONV_CH + 2 * width
    vt = h[:, :, v_cols:v_cols + width].reshape(b, s // _KC, _KC, DSA_HEADS, DSA_HEAD_DIM).transpose(0, 1, 3, 4, 2)
    vt = jnp.concatenate([vt, jnp.ones((b, s // _KC, DSA_HEADS, _VT_ROWS - DSA_HEAD_DIM, _KC), BF16)], axis=3)
    return pl.pallas_call(
        functools.partial(_dsa_t_kernel, topk=topk),
        grid=(b, nblk),
        in_specs=[pl.BlockSpec((1, Q_BLOCK, width), lambda bi, qi: (bi, qi, 2)),
                  pl.BlockSpec((1, Q_BLOCK, width), lambda bi, qi: (bi, qi, 5)),
                  pl.BlockSpec((1, Q_BLOCK, LANES), lambda bi, qi: (bi, qi, kw_block)),
                  pl.BlockSpec((1, s, width), lambda bi, qi: (bi, 0, 3)),
                  pl.BlockSpec((1, s // _KC, DSA_HEADS, _VT_ROWS, _KC), lambda bi, qi: (bi, 0, 0, 0, 0)),
                  pl.BlockSpec((1, s, LANES), lambda bi, qi: (bi, 0, kw_block)),
                  pl.BlockSpec(memory_space=pl.ANY)],
        out_specs=pl.BlockSpec((1, Q_BLOCK, width), lambda bi, qi: (bi, qi, 0)),
        out_shape=jax.ShapeDtypeStruct((b, s, width), BF16),
        scratch_shapes=[pltpu.VMEM((nblk + 1, DSA_HEADS, Q_BLOCK, Q_BLOCK), F32),
                        pltpu.VMEM((s // _KC, _KC, Q_BLOCK), jnp.int32),
                        pltpu.VMEM((DSA_HEADS, Q_BLOCK, LANES), BF16),
                        pltpu.VMEM((DSA_HEADS, _VT_ROWS, Q_BLOCK), F32),
                        pltpu.VMEM((_KC, IDX_HEADS * Q_BLOCK), F32),
                        pltpu.VMEM((_KC, IDX_HEADS * Q_BLOCK), F32),
                        pltpu.SemaphoreType.DMA(())],
        compiler_params=pltpu.CompilerParams(dimension_semantics=("arbitrary", "arbitrary")),
        name="dsa_attention",
    )(h, h, h, h, vt, h, bias_tiles)


_DIL_WAVE = 8


def _dil_kernel(*refs, seq):
    n_g = len(DIL_GROUPS)
    qkv = refs[:3 * n_g]
    bias_ref, o_ref, og_ref, lg_ref = refs[3 * n_g:]
    c = pl.program_id(2)
    ii = lax.broadcasted_iota(jnp.int32, (BAND_BLOCK, BAND_BLOCK), 0)
    jj = lax.broadcasted_iota(jnp.int32, (BAND_BLOCK, BAND_BLOCK), 1)
    valid_prev = jj >= ii
    valid_cur = ii >= jj
    scale = DIL_HEAD_DIM ** -0.5
    for g, (window, dil) in enumerate(DIL_GROUPS):
        assert window // dil == BAND_BLOCK
        q_ref, k_ref, v_ref = qkv[3 * g:3 * g + 3]
        sub_len = seq // dil
        blocks_per_chunk = DIL_CHUNK // (BAND_BLOCK * dil)
        bias_prev = bias_ref[g, 0, :, :BAND_BLOCK]
        bias_cur = bias_ref[g, 0, :, BAND_BLOCK:]
        tiles = [(r, nl) for r in range(dil) for nl in range(blocks_per_chunk)]
        for w0 in range(0, len(tiles), _DIL_WAVE):
            wave = tiles[w0:w0 + _DIL_WAVE]
            cur, prev, mask_prev = [], [], []
            for r, nl in wave:
                row0 = pl.multiple_of(r * sub_len + c * (DIL_CHUNK // dil) + nl * BAND_BLOCK, BAND_BLOCK)
                cur.append(pl.ds(row0, BAND_BLOCK))
                prev.append(pl.ds(pl.multiple_of(jnp.maximum(row0 - BAND_BLOCK, 0), BAND_BLOCK), BAND_BLOCK))
                mask_prev.append(valid_prev if nl > 0 else (valid_prev & (c > 0)))
            gather = lambda ref, slices: jnp.stack([ref[0, sl, :] for sl in slices])
            qb = gather(q_ref, cur)
            s_cur = jnp.einsum("wqd,wkd->wqk", qb, gather(k_ref, cur), preferred_element_type=F32)
            s_prev = jnp.einsum("wqd,wkd->wqk", qb, gather(k_ref, prev), preferred_element_type=F32)
            s_cur = jnp.where(valid_cur, s_cur * scale + bias_cur, NEG)
            s_prev = jnp.where(jnp.stack(mask_prev), s_prev * scale + bias_prev, NEG)
            m = jnp.max(jnp.maximum(s_cur, s_prev), axis=2, keepdims=True)
            p_cur = jnp.exp(s_cur - m)
            p_prev = jnp.exp(s_prev - m)
            ssum = jnp.sum(p_cur + p_prev, axis=2, keepdims=True)
            o = (jnp.einsum("wqk,wkd->wqd", p_cur.astype(BF16), gather(v_ref, cur), preferred_element_type=F32)
                 + jnp.einsum("wqk,wkd->wqd", p_prev.astype(BF16), gather(v_ref, prev),
                              preferred_element_type=F32)) / ssum
            lse = m + jnp.log(ssum)
            for w, (r, nl) in enumerate(wave):
                dst = pl.ds(r + nl * BAND_BLOCK * dil, BAND_BLOCK, stride=dil) if dil > 1 else pl.ds(
                    nl * BAND_BLOCK, BAND_BLOCK)
                og_ref[g, dst, :] = o[w]
                lg_ref[g, dst, :] = jnp.broadcast_to(lse[w], (BAND_BLOCK, LANES))
    rows = 256
    for rc in range(DIL_CHUNK // rows):
        sl = slice(rc * rows, (rc + 1) * rows)
        lses = [lg_ref[g, sl, :] for g in range(n_g)]
        mx = functools.reduce(jnp.maximum, lses)
        ws = [jnp.exp(l - mx) for l in lses]
        den = functools.reduce(lambda a, b_: a + b_, ws)
        num = functools.reduce(lambda a, b_: a + b_, [ws[g] * og_ref[g, sl, :] for g in range(n_g)])
        o_ref[0, sl, :] = (num / den).astype(o_ref.dtype)


def _dilated_attention(hs, bias_tiles):
    b, s, _ = hs[0].shape
    in_specs = []
    args = []
    for hg in hs:
        for part in range(3):
            in_specs.append(pl.BlockSpec((1, s, DIL_HEAD_DIM),
                                         lambda bi, hi, ci, part=part: (bi, 0, part * DIL_HEADS + hi)))
            args.append(hg)
    in_specs.append(pl.BlockSpec((len(DIL_GROUPS), 1, BAND_BLOCK, 2 * BAND_BLOCK),
                                 lambda bi, hi, ci: (0, hi, 0, 0)))
    return pl.pallas_call(
        functools.partial(_dil_kernel, seq=s),
        grid=(b, DIL_HEADS, s // DIL_CHUNK),
        in_specs=in_specs,
        out_specs=pl.BlockSpec((1, DIL_CHUNK, DIL_HEAD_DIM), lambda bi, hi, ci: (bi, ci, hi)),
        out_shape=jax.ShapeDtypeStruct((b, s, DIL_HEADS * DIL_HEAD_DIM), BF16),
        scratch_shapes=[pltpu.VMEM((len(DIL_GROUPS), DIL_CHUNK, LANES), F32),
                        pltpu.VMEM((len(DIL_GROUPS), DIL_CHUNK, LANES), F32)],
        compiler_params=pltpu.CompilerParams(dimension_semantics=("parallel", "parallel", "arbitrary")),
        name="dilated_attention",
    )(*args, bias_tiles)


def _outproj_kernel(*refs, n_in, with_router):
    ins = refs[:n_in]
    ws = refs[n_in:2 * n_in]
    x_ref, g_ref, b_ref = refs[2 * n_in:2 * n_in + 3]
    rest = refs[2 * n_in + 3:]
    mix = _dot(ins[0][...], ws[0][...])
    for a_ref, w_ref in zip(ins[1:], ws[1:]):
        mix = mix + _dot(a_ref[...], w_ref[...])
    xn = _layer_norm(DEEPNORM_ALPHA * x_ref[...] + mix, g_ref[...], b_ref[...])
    if not with_router:
        o_ref, ob_ref = rest
    else:
        router_ref, o_ref, ob_ref, gates_ref, counts_ref, run_ref, tri_ref = rest
        tm = xn.shape[0]

        @pl.when(pl.program_id(0) == 0)
        def _():
            run_ref[...] = jnp.zeros_like(run_ref)
            tri_ref[...] = (lax.broadcasted_iota(jnp.int32, (tm, tm), 0)
                            > lax.broadcasted_iota(jnp.int32, (tm, tm), 1)).astype(BF16)

        logits = _dot(xn.astype(BF16), router_ref[...])
        lane = lax.broadcasted_iota(jnp.int32, logits.shape, 1).astype(F32)
        lg = jnp.where(lane < N_EXPERTS, logits, -jnp.inf)
        m1 = jnp.max(lg, axis=1, keepdims=True)
        i1 = jnp.min(jnp.where(lg == m1, lane, float(LANES)), axis=1, keepdims=True)
        lg2 = jnp.where(lane == i1, -jnp.inf, lg)
        m2 = jnp.max(lg2, axis=1, keepdims=True)
        i2 = jnp.min(jnp.where(lg2 == m2, lane, float(LANES)), axis=1, keepdims=True)
        e2 = jnp.exp(m2 - m1)
        den = 1.0 + e2
        oh1 = jnp.where(lane == i1, 1.0, 0.0)
        oh2 = jnp.where(lane == i2, 1.0, 0.0)
        both = oh1 + oh2
        before = _dot(tri_ref[...], both.astype(BF16)) + run_ref[...]
        r1 = jnp.sum(before * oh1, axis=1, keepdims=True)
        r2 = jnp.sum(before * oh2, axis=1, keepdims=True)
        run_ref[...] += jnp.sum(both, axis=0, keepdims=True)
        counts_ref[...] = run_ref[...]
        gates_ref[...] = (jnp.where(lane == 0.0, i1, 0.0) + jnp.where(lane == 1.0, i2, 0.0)
                          + jnp.where(lane == 2.0, 1.0 / den, 0.0) + jnp.where(lane == 3.0, e2 / den, 0.0)
                          + jnp.where(lane == 4.0, r1, 0.0) + jnp.where(lane == 5.0, r2, 0.0))
    o_ref[...] = xn
    ob_ref[...] = xn.astype(BF16)


def _outproj_ln(ins, ws, x, ln_g, ln_b, router=None, tm=512):
    t, d = x.shape
    n_in = len(ins)
    row = lambda width: pl.BlockSpec((tm, width), lambda i: (i, 0))
    full = lambda a: pl.BlockSpec(a.shape, lambda i: (0, 0))
    vec = lambda v: v.reshape(1, d).astype(F32)
    args = list(ins) + list(ws) + [x, vec(ln_g), vec(ln_b)]
    in_specs = [row(a.shape[1]) for a in ins] + [full(w) for w in ws] + [row(d), full(vec(ln_g)), full(vec(ln_b))]
    out_shape = [jax.ShapeDtypeStruct((t, d), F32), jax.ShapeDtypeStruct((t, d), BF16)]
    out_specs = [row(d), row(d)]
    if router is not None:
        router_pad = jnp.zeros((d, LANES), BF16).at[:, :N_EXPERTS].set(router.astype(BF16))
        args.append(router_pad)
        in_specs.append(full(router_pad))
        out_shape += [jax.ShapeDtypeStruct((t, LANES), F32), jax.ShapeDtypeStruct((1, LANES), F32)]
        out_specs += [row(LANES), pl.BlockSpec((1, LANES), lambda i: (0, 0))]
        scratch = [pltpu.VMEM((1, LANES), F32), pltpu.VMEM((tm, tm), BF16)]
    else:
        scratch = []
    semantics = "arbitrary" if router is not None else "parallel"
    return pl.pallas_call(
        functools.partial(_outproj_kernel, n_in=n_in, with_router=router is not None),
        grid=(t // tm,),
        in_specs=in_specs,
        out_specs=out_specs,
        out_shape=out_shape,
        scratch_shapes=scratch,
        compiler_params=pltpu.CompilerParams(dimension_semantics=(semantics,)),
        name="outproj_ln",
    )(*args)


def _ffn_kernel(xb_ref, x_ref, wg_ref, wu_ref, wd_ref, g_ref, b_ref, o_ref, ob_ref, acc_ref):
    f = pl.program_id(1)

    @pl.when(f == 0)
    def _():
        acc_ref[...] = jnp.zeros_like(acc_ref)

    xb = xb_ref[...]
    hg = _dot(xb, wg_ref[...])
    hu = _dot(xb, wu_ref[...])
    act = (hg * _sigmoid(hg) * hu).astype(BF16)
    acc_ref[...] += _dot(act, wd_ref[...])

    @pl.when(f == pl.num_programs(1) - 1)
    def _():
        xn = _layer_norm(DEEPNORM_ALPHA * x_ref[...] + acc_ref[...], g_ref[...], b_ref[...])
        o_ref[...] = xn
        ob_ref[...] = xn.astype(BF16)


def _ffn_ln(xb, x, wg, wu, wd, ln_g, ln_b, tm=512, tf=1408):
    t, d = x.shape
    ff = wg.shape[1]
    vec = lambda v: v.reshape(1, d).astype(F32)
    return pl.pallas_call(
        _ffn_kernel,
        grid=(t // tm, ff // tf),
        in_specs=[pl.BlockSpec((tm, d), lambda i, f: (i, 0)),
                  pl.BlockSpec((tm, d), lambda i, f: (i, 0)),
                  pl.BlockSpec((d, tf), lambda i, f: (0, f)),
                  pl.BlockSpec((d, tf), lambda i, f: (0, f)),
                  pl.BlockSpec((tf, d), lambda i, f: (f, 0)),
                  pl.BlockSpec((1, d), lambda i, f: (0, 0)),
                  pl.BlockSpec((1, d), lambda i, f: (0, 0))],
        out_specs=[pl.BlockSpec((tm, d), lambda i, f: (i, 0)),
                   pl.BlockSpec((tm, d), lambda i, f: (i, 0))],
        out_shape=[jax.ShapeDtypeStruct((t, d), F32), jax.ShapeDtypeStruct((t, d), BF16)],
        scratch_shapes=[pltpu.VMEM((tm, d), F32)],
        compiler_params=pltpu.CompilerParams(dimension_semantics=("parallel", "arbitrary")),
        name="ffn_ln",
    )(xb, x, wg, wu, wd, vec(ln_g), vec(ln_b))


_MOE_TM = 1024


def _route(route, counts, n_tiles):
    expert = route[:, 0:2].astype(jnp.int32)
    rank = route[:, 4:6].astype(jnp.int32)
    counts = counts[0, :N_EXPERTS].astype(jnp.int32)
    padded = (counts + _MOE_TM - 1) // _MOE_TM * _MOE_TM
    ends = jnp.cumsum(padded)
    starts = ends - padded
    onehot = expert[:, :, None] == jnp.arange(N_EXPERTS, dtype=jnp.int32)
    pos = (rank + jnp.sum(jnp.where(onehot, starts, 0), axis=-1)).reshape(-1)
    tile_start = jnp.arange(n_tiles, dtype=jnp.int32) * _MOE_TM
    tile_expert = jnp.minimum(jnp.sum((tile_start[:, None] >= ends[None, :]).astype(jnp.int32), axis=1),
                              N_EXPERTS - 1)
    return pos.astype(jnp.int32), tile_expert.astype(jnp.int32)


def _row_scatter_kernel(pos_ref, x_ref, init_ref, o_ref, sem, *, tm):
    del init_ref

    def body(j, carry):
        for choice in range(2):
            p = pos_ref[0, 0, 2 * j + choice]
            pltpu.make_async_copy(x_ref.at[pl.ds(j, 1)], o_ref.at[pl.ds(p, 1)], sem).start(priority=choice)
        return carry

    lax.fori_loop(0, tm, body, 0, unroll=8)
    for _ in range(2):
        pltpu.make_async_copy(x_ref, o_ref.at[pl.ds(0, tm)], sem).wait()


def _row_scatter(x, pos, n_rows, tm=512):
    t, d = x.shape
    return pl.pallas_call(
        functools.partial(_row_scatter_kernel, tm=tm),
        grid=(t // tm,),
        in_specs=[pl.BlockSpec((1, 1, 2 * tm), lambda i: (i, 0, 0), memory_space=pltpu.SMEM),
                  pl.BlockSpec((tm, d), lambda i: (i, 0)),
                  pl.BlockSpec(memory_space=pl.ANY)],
        out_specs=pl.BlockSpec(memory_space=pl.ANY),
        out_shape=jax.ShapeDtypeStruct((n_rows, d), x.dtype),
        scratch_shapes=[pltpu.SemaphoreType.DMA(())],
        input_output_aliases={2: 0},
        compiler_params=pltpu.CompilerParams(dimension_semantics=("arbitrary",)),
        name="moe_row_scatter",
    )(pos.reshape(t // tm, 1, 2 * tm), x, jnp.zeros((n_rows, d), x.dtype))


def _expert_ffn_kernel(te_ref, x_ref, wg_ref, wu_ref, wd_ref, o_ref, acc_ref):
    del te_ref
    f = pl.program_id(1)

    @pl.when(f == 0)
    def _():
        acc_ref[...] = jnp.zeros_like(acc_ref)

    xb = x_ref[...].astype(BF16)
    hg = _dot(xb, wg_ref[0].astype(BF16))
    hu = _dot(xb, wu_ref[0].astype(BF16))
    act = (hg * _sigmoid(hg) * hu).astype(BF16)
    acc_ref[...] += _dot(act, wd_ref[0].astype(BF16))

    @pl.when(f == pl.num_programs(1) - 1)
    def _():
        o_ref[...] = acc_ref[...]


def _expert_ffn(xs, tile_expert, wg, wu, wd, tf=512):
    n_rows, d = xs.shape
    ff = wg.shape[2]
    grid_spec = pltpu.PrefetchScalarGridSpec(
        num_scalar_prefetch=1,
        grid=(n_rows // _MOE_TM, ff // tf),
        in_specs=[pl.BlockSpec((_MOE_TM, d), lambda i, f, te: (i, 0)),
                  pl.BlockSpec((1, d, tf), lambda i, f, te: (te[i], 0, f)),
                  pl.BlockSpec((1, d, tf), lambda i, f, te: (te[i], 0, f)),
                  pl.BlockSpec((1, tf, d), lambda i, f, te: (te[i], f, 0))],
        out_specs=pl.BlockSpec((_MOE_TM, d), lambda i, f, te: (i, 0)),
        scratch_shapes=[pltpu.VMEM((_MOE_TM, d), F32)])
    return pl.pallas_call(
        _expert_ffn_kernel,
        grid_spec=grid_spec,
        out_shape=jax.ShapeDtypeStruct((n_rows, d), F32),
        compiler_params=pltpu.CompilerParams(dimension_semantics=("parallel", "arbitrary")),
        name="moe_expert_ffn",
    )(tile_expert, xs, wg, wu, wd)


def _combine_kernel(pos_ref, route_ref, x_ref, g_ref, b_ref, y_ref, o_ref, ybuf_ref, sem, *, tm):
    def body(j, carry):
        for choice in range(2):
            p = pos_ref[0, 0, 2 * j + choice]
            pltpu.make_async_copy(y_ref.at[pl.ds(p, 1)], ybuf_ref.at[choice, pl.ds(j, 1)],
                                  sem).start(priority=choice)
        return carry

    lax.fori_loop(0, tm, body, 0, unroll=8)
    for choice in range(2):
        pltpu.make_async_copy(y_ref.at[pl.ds(0, tm)], ybuf_ref.at[choice], sem).wait()
    route = route_ref[...]
    y = route[:, 2:3] * ybuf_ref[0] + route[:, 3:4] * ybuf_ref[1]
    o_ref[...] = _layer_norm(DEEPNORM_ALPHA * x_ref[...] + y, g_ref[...], b_ref[...])


def _combine_ln(ys, pos, route, x, ln_g, ln_b, tm=512):
    t, d = x.shape
    vec = lambda v: v.reshape(1, d).astype(F32)
    return pl.pallas_call(
        functools.partial(_combine_kernel, tm=tm),
        grid=(t // tm,),
        in_specs=[pl.BlockSpec((1, 1, 2 * tm), lambda i: (i, 0, 0), memory_space=pltpu.SMEM),
                  pl.BlockSpec((tm, LANES), lambda i: (i, 0)),
                  pl.BlockSpec((tm, d), lambda i: (i, 0)),
                  pl.BlockSpec((1, d), lambda i: (0, 0)),
                  pl.BlockSpec((1, d), lambda i: (0, 0)),
                  pl.BlockSpec(memory_space=pl.ANY)],
        out_specs=pl.BlockSpec((tm, d), lambda i: (i, 0)),
        out_shape=jax.ShapeDtypeStruct((t, d), F32),
        scratch_shapes=[pltpu.VMEM((2, tm, d), F32), pltpu.SemaphoreType.DMA(())],
        compiler_params=pltpu.CompilerParams(dimension_semantics=("arbitrary",)),
        name="moe_combine_ln",
    )(pos.reshape(t // tm, 1, 2 * tm), route, x, vec(ln_g), vec(ln_b), ys)


def _moe_ln(x, route, counts, wg, wu, wd, ln_g, ln_b):
    t, _ = x.shape
    n_tiles = 2 * t // _MOE_TM + N_EXPERTS
    pos, tile_expert = _route(route, counts, n_tiles)
    xs = _row_scatter(x, pos, n_tiles * _MOE_TM)
    ys = _expert_ffn(xs, tile_expert, wg, wu, wd)
    return _combine_ln(ys, pos, route, x, ln_g, ln_b)


def _even_layer(x, xb, batch, seq, dsa_bias, w_in, conv_w, conv_b, conv_ln_g, conv_ln_b, w_out,
                ln1_g, ln1_b, ffn_wg, ffn_wu, ffn_wd, ln2_g, ln2_b):
    w_in_p = jnp.zeros((D_MODEL, EVEN_IN_PAD), BF16).at[:, :w_in.shape[1]].set(w_in.astype(BF16))
    h = _matmul(xb, w_in_p, 2048, 640, BF16).reshape(batch, seq, EVEN_IN_PAD)
    a_out = _conformer_conv(h, conv_w, conv_b, conv_ln_g, conv_ln_b)
    att = _dsa_attention(h, dsa_bias)
    w_out_b = w_out.astype(BF16)
    x1, x1b = _outproj_ln([a_out.reshape(-1, CONV_CH), att.reshape(-1, DSA_HEADS * DSA_HEAD_DIM)],
                          [w_out_b[:CONV_CH], w_out_b[CONV_CH:]], x, ln1_g, ln1_b)
    return _ffn_ln(x1b, x1, ffn_wg.astype(BF16), ffn_wu.astype(BF16), ffn_wd.astype(BF16), ln2_g, ln2_b)


def _odd_layer(x, xb, batch, seq, dil_bias, w_in, w_out, ln1_g, ln1_b, router, moe_wg, moe_wu, moe_wd,
               ln2_g, ln2_b):
    gw = 3 * DIL_HEADS * DIL_HEAD_DIM
    w_in_b = w_in.astype(BF16)
    hs = [_matmul_residue_major(xb, w_in_b[:, g * gw:(g + 1) * gw], batch, seq, dil, 2048, 768)
          for g, (_, dil) in enumerate(DIL_GROUPS)]
    o = _dilated_attention(hs, dil_bias).reshape(-1, DIL_HEADS * DIL_HEAD_DIM)
    x1, _, route, counts = _outproj_ln([o], [w_out.astype(BF16)], x, ln1_g, ln1_b, router=router)
    out = _moe_ln(x1, route, counts, moe_wg, moe_wu, moe_wd, ln2_g, ln2_b)
    return out, out.astype(BF16)


def kernel(x, rel_bias, even_w_in, even_conv_w, even_conv_b, even_conv_ln_g, even_conv_ln_b, even_w_out, even_ln1_g, even_ln1_b, even_ffn_wg, even_ffn_wu, even_ffn_wd, even_ln2_g, even_ln2_b, odd_w_in, odd_w_out, odd_ln1_g, odd_ln1_b, odd_router, odd_moe_wg, odd_moe_wu, odd_moe_wd, odd_ln2_g, odd_ln2_b):
    batch, seq, d = x.shape
    assert d == D_MODEL and seq % DIL_CHUNK == 0
    depth = even_w_in.shape[0] + odd_w_in.shape[0]
    assert depth == DEPTH
    dsa_bias = _dsa_bias_tiles(rel_bias, seq)
    dil_bias = _dil_bias_tiles(rel_bias)
    xf = x.reshape(batch * seq, d)
    xb = xf.astype(BF16)
    for layer in range(depth):
        i = layer // 2
        if layer % 2 == 0:
            xf, xb = _even_layer(xf, xb, batch, seq, dsa_bias, even_w_in[i], even_conv_w[i], even_conv_b[i],
                                 even_conv_ln_g[i], even_conv_ln_b[i], even_w_out[i], even_ln1_g[i],
                                 even_ln1_b[i], even_ffn_wg[i], even_ffn_wu[i], even_ffn_wd[i],
                                 even_ln2_g[i], even_ln2_b[i])
        else:
            xf, xb = _odd_layer(xf, xb, batch, seq, dil_bias, odd_w_in[i], odd_w_out[i], odd_ln1_g[i],
                                odd_ln1_b[i], odd_router[i], odd_moe_wg[i], odd_moe_wu[i], odd_moe_wd[i],
                                odd_ln2_g[i], odd_ln2_b[i])
    return xf.reshape(batch, seq, d)
```

```python
import functools
import math

import jax
import jax.numpy as jnp
from jax import lax
from jax.experimental import pallas as pl
from jax.experimental.pallas import tpu as pltpu

BF16 = jnp.bfloat16
F32 = jnp.float32

D_MODEL = 1024
CONV_CH = 512
CONV_WIDTH = 31
DSA_HEADS = 8
DSA_HEAD_DIM = 64
IDX_HEADS = 8
IDX_DIM = 64
DSA_TOPK_MAX = 256
Q_BLOCK = 128
DIL_GROUPS = ((128, 1), (512, 4), (2048, 16))
DIL_HEADS = 8
DIL_HEAD_DIM = 128
BAND_BLOCK = 128
NUM_BUCKETS = 32
MAX_DISTANCE = 2048
N_EXPERTS = 8
LN_EPS = 1e-5
NEG = -1e30
DEPTH = 2
DEEPNORM_ALPHA = (2 * DEPTH) ** 0.25

LANES = 128
EVEN_IN_PAD = 3200
INT_MIN = -(2 ** 31)
DIL_CHUNK = BAND_BLOCK * max(d for _, d in DIL_GROUPS)

_PARALLEL2 = pltpu.CompilerParams(dimension_semantics=("parallel", "parallel"))


def _layer_norm(y, g, b):
    mu = jnp.mean(y, axis=-1, keepdims=True)
    yc = y - mu
    var = jnp.mean(yc * yc, axis=-1, keepdims=True)
    return yc * lax.rsqrt(var + LN_EPS) * g + b


def _sigmoid(x):
    return 1.0 / (1.0 + jnp.exp(-x))


def _dot_nt(a, b):
    return lax.dot_general(a, b, (((1,), (1,)), ((), ())), preferred_element_type=F32)


def _dot(a, b):
    return jnp.dot(a, b, preferred_element_type=F32)


def _mm_kernel(x_ref, w_ref, o_ref):
    o_ref[...] = _dot(x_ref[...], w_ref[...]).astype(o_ref.dtype)


def _matmul(x, w, tm, tn, out_dtype):
    t, k = x.shape
    n = w.shape[1]
    return pl.pallas_call(
        _mm_kernel,
        grid=(t // tm, n // tn),
        in_specs=[pl.BlockSpec((tm, k), lambda i, j: (i, 0)),
                  pl.BlockSpec((k, tn), lambda i, j: (0, j))],
        out_specs=pl.BlockSpec((tm, tn), lambda i, j: (i, j)),
        out_shape=jax.ShapeDtypeStruct((t, n), out_dtype),
        compiler_params=_PARALLEL2,
        name="matmul",
    )(x, w)


def _mm_perm_kernel(x_ref, w_ref, o_ref, acc_ref, *, dil, tm, tn):
    acc = _dot(x_ref[...], w_ref[...])
    rows = tm // dil
    for c in range(tn // LANES):
        acc_ref[c] = acc[:, c * LANES:(c + 1) * LANES]
    for r in range(dil):
        for c in range(tn // LANES):
            o_ref[0, r, :, c * LANES:(c + 1) * LANES] = (
                acc_ref[c, pl.ds(r, rows, stride=dil), :].astype(o_ref.dtype))


def _matmul_residue_major(x, w, batch, seq, dil, tm, tn):
    t, k = x.shape
    n = w.shape[1]
    if dil == 1:
        return _matmul(x, w, tm, tn, BF16).reshape(batch, seq, n)
    tiles_per_seq = seq // tm
    out = pl.pallas_call(
        functools.partial(_mm_perm_kernel, dil=dil, tm=tm, tn=tn),
        grid=(t // tm, n // tn),
        in_specs=[pl.BlockSpec((tm, k), lambda i, j: (i, 0)),
                  pl.BlockSpec((k, tn), lambda i, j: (0, j))],
        out_specs=pl.BlockSpec((1, dil, tm // dil, tn),
                               lambda i, j: (i // tiles_per_seq, 0, i % tiles_per_seq, j)),
        out_shape=jax.ShapeDtypeStruct((batch, dil, seq // dil, n), BF16),
        scratch_shapes=[pltpu.VMEM((tn // LANES, tm, LANES), F32)],
        compiler_params=_PARALLEL2,
        name="matmul_residue_major",
    )(x, w)
    return out.reshape(batch, seq, n)


_CONV_HALO = 32
_CONV_ROWS = 64


def _conv_kernel(val_ref, gate_ref, w_ref, cb_ref, g_ref, b_ref, o_ref, ext_ref, *, ts):
    s = pl.program_id(1)

    @pl.when(s == 0)
    def _():
        ext_ref[0:_CONV_HALO, :] = jnp.zeros((_CONV_HALO, CONV_CH), F32)

    val = val_ref[0].astype(F32)
    gate = gate_ref[0].astype(F32)
    ext_ref[_CONV_HALO:_CONV_HALO + ts, :] = val * _sigmoid(gate)
    first_tap = _CONV_HALO - (CONV_WIDTH - 1)
    for rc in range(ts // _CONV_ROWS):
        r0 = rc * _CONV_ROWS
        acc = jnp.zeros((_CONV_ROWS, CONV_CH), F32)
        for j in range(CONV_WIDTH):
            lo = r0 + first_tap + j
            acc = acc + ext_ref[lo:lo + _CONV_ROWS, :] * w_ref[j:j + 1, :]
        y = _layer_norm(acc + cb_ref[...], g_ref[...], b_ref[...])
        o_ref[0, r0:r0 + _CONV_ROWS, :] = (y * _sigmoid(y)).astype(o_ref.dtype)
    ext_ref[0:_CONV_HALO, :] = ext_ref[ts:ts + _CONV_HALO, :]


def _conformer_conv(h, conv_w, conv_b, ln_g, ln_b, ts=512):
    b, s, _ = h.shape
    w_pad = jnp.zeros((32, CONV_CH), F32).at[:CONV_WIDTH].set(conv_w)
    vec = lambda v: v.reshape(1, CONV_CH).astype(F32)
    full = lambda shape: pl.BlockSpec(shape, lambda bi, si: (0, 0))
    return pl.pallas_call(
        functools.partial(_conv_kernel, ts=ts),
        grid=(b, s // ts),
        in_specs=[pl.BlockSpec((1, ts, CONV_CH), lambda bi, si: (bi, si, 0)),
                  pl.BlockSpec((1, ts, CONV_CH), lambda bi, si: (bi, si, 1)),
                  full((32, CONV_CH)), full((1, CONV_CH)), full((1, CONV_CH)), full((1, CONV_CH))],
        out_specs=pl.BlockSpec((1, ts, CONV_CH), lambda bi, si: (bi, si, 0)),
        out_shape=jax.ShapeDtypeStruct((b, s, CONV_CH), BF16),
        scratch_shapes=[pltpu.VMEM((ts + _CONV_HALO, CONV_CH), F32)],
        compiler_params=pltpu.CompilerParams(dimension_semantics=("parallel", "arbitrary")),
        name="conformer_conv",
    )(h, h, w_pad, vec(conv_b), vec(ln_g), vec(ln_b))


def _rel_bucket(dist):
    max_exact = NUM_BUCKETS // 2
    n = dist.astype(jnp.int32)
    nf = jnp.maximum(n, 1).astype(F32)
    large = max_exact + (jnp.log(nf / max_exact) / math.log(MAX_DISTANCE / max_exact)
                         * (NUM_BUCKETS - max_exact)).astype(jnp.int32)
    large = jnp.minimum(large, NUM_BUCKETS - 1)
    return jnp.where(n < max_exact, n, large)


def _bias_by_distance(rel_bias, n):
    return rel_bias[_rel_bucket(jnp.arange(n, dtype=jnp.int32))].astype(F32).T


def _toeplitz_kernel(win_ref, o_ref):
    n_win, span = win_ref.shape
    for n in range(n_win):
        y = jnp.broadcast_to(win_ref[n:n + 1, :], (LANES, span))
        y = pltpu.roll(y, 0, 1, stride=1, stride_axis=0)
        o_ref[n] = y[:, LANES:]


def _toeplitz_tiles(windows, per_step=8):
    n, span = windows.shape
    width = span - LANES
    return pl.pallas_call(
        _toeplitz_kernel,
        grid=(n // per_step,),
        in_specs=[pl.BlockSpec((per_step, span), lambda i: (i, 0))],
        out_specs=pl.BlockSpec((per_step, LANES, width), lambda i: (i, 0, 0)),
        out_shape=jax.ShapeDtypeStruct((n, LANES, width), F32),
        compiler_params=pltpu.CompilerParams(dimension_semantics=("parallel",)),
        name="bias_toeplitz",
    )(windows)


def _dsa_bias_tiles(rel_bias, seq):
    nblk = seq // Q_BLOCK
    heads = rel_bias.shape[1]
    f = _bias_by_distance(rel_bias, seq)
    u = jnp.concatenate([jnp.broadcast_to(f[:, :1], (heads, 2 * Q_BLOCK)), f], axis=1)
    blocks = u.reshape(heads, nblk + 2, Q_BLOCK)
    windows = jnp.concatenate([blocks[:, :-1], blocks[:, 1:]], axis=-1)
    windows = windows.transpose(1, 0, 2).reshape((nblk + 1) * heads, 2 * Q_BLOCK)
    return _toeplitz_tiles(windows).reshape(nblk + 1, heads, Q_BLOCK, Q_BLOCK)


def _dil_bias_tiles(rel_bias):
    max_dil = max(d for _, d in DIL_GROUPS)
    heads = rel_bias.shape[1]
    span = 2 * BAND_BLOCK
    f = _bias_by_distance(rel_bias, span * max_dil + 1)
    windows = []
    for _, dil in DIL_GROUPS:
        fd = f[:, 0:span * dil + 1:dil]
        tail = jnp.broadcast_to(fd[:, :1], (heads, BAND_BLOCK - 1))
        windows.append(jnp.concatenate([fd[:, ::-1], tail], axis=1))
    tiles = _toeplitz_tiles(jnp.concatenate(windows, axis=0))
    return tiles.reshape(len(DIL_GROUPS), heads, BAND_BLOCK, span)


_KC = 2 * Q_BLOCK
_VT_ROWS = DSA_HEAD_DIM + 16


def _sortable_key(score):
    score = jnp.where(score == 0.0, 0.0, score)
    bits = lax.bitcast_convert_type(score, jnp.int32)
    return jnp.where(bits < 0, bits ^ jnp.int32(0x7FFFFFFF), bits)


def _rows_reduce(x, op):
    rows = x.shape[0]
    return op(op(x.reshape(rows // 8, 8, LANES), axis=0), axis=0, keepdims=True)


def _keys_reduce(x, op):
    h, keys, _ = x.shape
    return op(op(x.reshape(h, keys // 8, 8, LANES), axis=1), axis=1, keepdims=True)


def _dsa_t_kernel(q_ref, qi_ref, wq_ref, k_ref, vt_ref, ki_ref, bias_hbm, o_ref,
                  bias_ref, keys_ref, qm_ref, acc_ref, dots_a_ref, dots_b_ref, sem, *, topk):
    b = pl.program_id(0)
    i = pl.program_id(1)

    @pl.when((b == 0) & (i == 0))
    def _():
        cp = pltpu.make_async_copy(bias_hbm, bias_ref, sem)
        cp.start()
        cp.wait()

    nch = (i + 2) // 2
    key_off = lax.broadcasted_iota(jnp.int32, (_KC, LANES), 0)
    q_pos = i * Q_BLOCK + lax.broadcasted_iota(jnp.int32, (_KC, LANES), 1)
    lane = lax.broadcasted_iota(jnp.int32, (Q_BLOCK, LANES), 1)
    low_half = lane < DSA_HEAD_DIM

    w_t = (wq_ref[0].astype(F32) * (IDX_HEADS ** -0.5 * IDX_DIM ** -0.5)).T
    qi = qi_ref[0].astype(F32)
    qi_heads = []
    for p in range(IDX_HEADS // 2):
        pair = qi[:, p * LANES:(p + 1) * LANES]
        qi_heads.append(jnp.where(low_half, pair, 0.0))
        qi_heads.append(jnp.where(low_half, pltpu.roll(pair, IDX_DIM, 1), 0.0))
    qi_stack = jnp.concatenate(qi_heads, axis=0).astype(BF16)
    q_scaled = (q_ref[0].astype(F32) * (DSA_HEAD_DIM ** -0.5)).astype(BF16)
    for p in range(DSA_HEADS // 2):
        qp = q_scaled[:, p * LANES:(p + 1) * LANES]
        qm_ref[2 * p] = jnp.where(low_half, qp, jnp.zeros_like(qp))
        qm_ref[2 * p + 1] = jnp.where(low_half, jnp.zeros_like(qp), qp)

    def index_dots(c, dst_ref):
        c = jnp.minimum(c, nch - 1)
        k0 = pl.multiple_of(c * _KC, _KC)
        ki = ki_ref[0, pl.ds(k0, _KC), :]
        dst_ref[...] = _dot_nt(ki, qi_stack)

    def consume_dots(c, src_ref):
        c = jnp.minimum(c, nch - 1)
        k0 = pl.multiple_of(c * _KC, _KC)
        score = jnp.zeros((_KC, LANES), F32)
        for h in range(IDX_HEADS):
            w_row = w_t[IDX_DIM + h:IDX_DIM + h + 1, :]
            score = score + w_row * jnp.maximum(src_ref[:, h * Q_BLOCK:(h + 1) * Q_BLOCK], 0.0)
        keys_ref[c] = jnp.where(k0 + key_off <= q_pos, _sortable_key(score), jnp.int32(INT_MIN))

    index_dots(0, dots_a_ref)

    def score_body(c2, carry):
        index_dots(2 * c2 + 1, dots_b_ref)
        consume_dots(2 * c2, dots_a_ref)
        index_dots(2 * c2 + 2, dots_a_ref)
        consume_dots(2 * c2 + 1, dots_b_ref)
        return carry

    lax.fori_loop(0, (nch + 1) // 2, score_body, 0)

    def count_ge(cand):
        cb = jnp.broadcast_to(cand, (8, LANES))
        chains = 4

        def body(c, acc):
            kk = keys_ref[c].reshape(_KC // (8 * chains), chains, 8, LANES)
            return acc + jnp.sum(jnp.where(kk >= cb, 1.0, 0.0), axis=0)

        acc = lax.fori_loop(0, nch, body, jnp.zeros((chains, 8, LANES), F32))
        return jnp.sum(jnp.sum(acc, axis=0), axis=0, keepdims=True)

    kf = float(topk)
    zero = jnp.zeros((1, LANES), jnp.int32)
    thr0 = jnp.where(count_ge(zero) >= kf, zero, jnp.full((1, LANES), INT_MIN, jnp.int32))

    def bit_body(t, thr):
        cand = thr | jnp.left_shift(jnp.int32(1), 30 - t)
        return jnp.where(count_ge(cand) >= kf, cand, thr)

    thr = lax.fori_loop(0, 31, bit_body, thr0)
    n_gt = count_ge(jnp.where(thr == jnp.int32(2 ** 31 - 1), thr, thr + 1))
    need = kf - n_gt
    tri = (lax.broadcasted_iota(jnp.int32, (_KC, _KC), 1)
           <= lax.broadcasted_iota(jnp.int32, (_KC, _KC), 0)).astype(BF16)

    acc_ref[...] = jnp.zeros_like(acc_ref)

    def attn_body(c, carry):
        ms, tie_carry = carry
        k0 = pl.multiple_of(c * _KC, _KC)
        kk = keys_ref[c]
        causal = kk > jnp.int32(INT_MIN)
        eq = (kk == thr) & causal
        eq_f = jnp.where(eq, 1.0, 0.0)
        rank = _dot(tri, eq_f.astype(BF16)) + tie_carry
        sel = ((kk > thr) & causal) | (eq & (rank <= need))
        tie_carry = tie_carry + _rows_reduce(eq_f, jnp.sum)
        d = i - 2 * c
        kc = k_ref[0, pl.ds(k0, _KC), :]
        k_heads = jnp.stack([kc[:, (h // 2) * LANES:(h // 2 + 1) * LANES] for h in range(DSA_HEADS)])
        s = jnp.einsum("hkd,hqd->hkq", k_heads, qm_ref[...], preferred_element_type=F32)
        bias = jnp.concatenate([bias_ref[d + 1], bias_ref[d]], axis=1)
        s = jnp.where(sel, s + bias, NEG)
        m_new = jnp.maximum(ms, _keys_reduce(s, jnp.max))
        alpha = jnp.exp(ms - m_new)
        pr = jnp.exp((s - m_new).astype(BF16))
        pv = jnp.einsum("hdk,hkq->hdq", vt_ref[0, c], pr, preferred_element_type=F32)
        acc_ref[...] = alpha * acc_ref[...] + pv
        return m_new, tie_carry

    init = (jnp.full((DSA_HEADS, 1, LANES), NEG, F32), jnp.zeros((1, LANES), F32))
    lax.fori_loop(0, nch, attn_body, init)
    acc = acc_ref[...]
    out_t = acc[:, :DSA_HEAD_DIM] / acc[:, DSA_HEAD_DIM:DSA_HEAD_DIM + 1]
    o_ref[0] = out_t.reshape(DSA_HEADS * DSA_HEAD_DIM, Q_BLOCK).T.astype(o_ref.dtype)


def _dsa_attention(h, bias_tiles):
    b, s, _ = h.shape
    width = DSA_HEADS * DSA_HEAD_DIM
    nblk = s // Q_BLOCK
    topk = min(DSA_TOPK_MAX, s // 4)
    kw_block = (CONV_CH * 2 + 4 * width) // LANES
    v_cols = 2 * CONV_CH + 2 * width
    vt = h[:, :, v_cols:v_cols + width].reshape(b, s // _KC, _KC, DSA_HEADS, DSA_HEAD_DIM).transpose(0, 1, 3, 4, 2)
    vt = jnp.concatenate([vt, jnp.ones((b, s // _KC, DSA_HEADS, _VT_ROWS - DSA_HEAD_DIM, _KC), BF16)], axis=3)
    return pl.pallas_call(
        functools.partial(_dsa_t_kernel, topk=topk),
        grid=(b, nblk),
        in_specs=[pl.BlockSpec((1, Q_BLOCK, width), lambda bi, qi: (bi, qi, 2)),
                  pl.BlockSpec((1, Q_BLOCK, width), lambda bi, qi: (bi, qi, 5)),
                  pl.BlockSpec((1, Q_BLOCK, LANES), lambda bi, qi: (bi, qi, kw_block)),
                  pl.BlockSpec((1, s, width), lambda bi, qi: (bi, 0, 3)),
                  pl.BlockSpec((1, s // _KC, DSA_HEADS, _VT_ROWS, _KC), lambda bi, qi: (bi, 0, 0, 0, 0)),
                  pl.BlockSpec((1, s, LANES), lambda bi, qi: (bi, 0, kw_block)),
                  pl.BlockSpec(memory_space=pl.ANY)],
        out_specs=pl.BlockSpec((1, Q_BLOCK, width), lambda bi, qi: (bi, qi, 0)),
        out_shape=jax.ShapeDtypeStruct((b, s, width), BF16),
        scratch_shapes=[pltpu.VMEM((nblk + 1, DSA_HEADS, Q_BLOCK, Q_BLOCK), F32),
                        pltpu.VMEM((s // _KC, _KC, Q_BLOCK), jnp.int32),
                        pltpu.VMEM((DSA_HEADS, Q_BLOCK, LANES), BF16),
                        pltpu.VMEM((DSA_HEADS, _VT_ROWS, Q_BLOCK), F32),
                        pltpu.VMEM((_KC, IDX_HEADS * Q_BLOCK), F32),
                        pltpu.VMEM((_KC, IDX_HEADS * Q_BLOCK), F32),
                        pltpu.SemaphoreType.DMA(())],
        compiler_params=pltpu.CompilerParams(dimension_semantics=("arbitrary", "arbitrary")),
        name="dsa_attention",
    )(h, h, h, h, vt, h, bias_tiles)


_DIL_WAVE = 8


def _dil_kernel(*refs, seq):
    n_g = len(DIL_GROUPS)
    qkv = refs[:3 * n_g]
    bias_ref, o_ref, og_ref, lg_ref = refs[3 * n_g:]
    c = pl.program_id(2)
    ii = lax.broadcasted_iota(jnp.int32, (BAND_BLOCK, BAND_BLOCK), 0)
    jj = lax.broadcasted_iota(jnp.int32, (BAND_BLOCK, BAND_BLOCK), 1)
    valid_prev = jj >= ii
    valid_cur = ii >= jj
    scale = DIL_HEAD_DIM ** -0.5
    for g, (window, dil) in enumerate(DIL_GROUPS):
        assert window // dil == BAND_BLOCK
        q_ref, k_ref, v_ref = qkv[3 * g:3 * g + 3]
        sub_len = seq // dil
        blocks_per_chunk = DIL_CHUNK // (BAND_BLOCK * dil)
        bias_prev = bias_ref[g, 0, :, :BAND_BLOCK]
        bias_cur = bias_ref[g, 0, :, BAND_BLOCK:]
        tiles = [(r, nl) for r in range(dil) for nl in range(blocks_per_chunk)]
        for w0 in range(0, len(tiles), _DIL_WAVE):
            wave = tiles[w0:w0 + _DIL_WAVE]
            cur, prev, mask_prev = [], [], []
            for r, nl in wave:
                row0 = pl.multiple_of(r * sub_len + c * (DIL_CHUNK // dil) + nl * BAND_BLOCK, BAND_BLOCK)
                cur.append(pl.ds(row0, BAND_BLOCK))
                prev.append(pl.ds(pl.multiple_of(jnp.maximum(row0 - BAND_BLOCK, 0), BAND_BLOCK), BAND_BLOCK))
                mask_prev.append(valid_prev if nl > 0 else (valid_prev & (c > 0)))
            gather = lambda ref, slices: jnp.stack([ref[0, sl, :] for sl in slices])
            qb = gather(q_ref, cur)
            s_cur = jnp.einsum("wqd,wkd->wqk", qb, gather(k_ref, cur), preferred_element_type=F32)
            s_prev = jnp.einsum("wqd,wkd->wqk", qb, gather(k_ref, prev), preferred_element_type=F32)
            s_cur = jnp.where(valid_cur, s_cur * scale + bias_cur, NEG)
            s_prev = jnp.where(jnp.stack(mask_prev), s_prev * scale + bias_prev, NEG)
            m = jnp.max(jnp.maximum(s_cur, s_prev), axis=2, keepdims=True)
            p_cur = jnp.exp(s_cur - m)
            p_prev = jnp.exp(s_prev - m)
            ssum = jnp.sum(p_cur + p_prev, axis=2, keepdims=True)
            o = (jnp.einsum("wqk,wkd->wqd", p_cur.astype(BF16), gather(v_ref, cur), preferred_element_type=F32)
                 + jnp.einsum("wqk,wkd->wqd", p_prev.astype(BF16), gather(v_ref, prev),
                              preferred_element_type=F32)) / ssum
            lse = m + jnp.log(ssum)
            for w, (r, nl) in enumerate(wave):
                dst = pl.ds(r + nl * BAND_BLOCK * dil, BAND_BLOCK, stride=dil) if dil > 1 else pl.ds(
                    nl * BAND_BLOCK, BAND_BLOCK)
                og_ref[g, dst, :] = o[w]
                lg_ref[g, dst, :] = jnp.broadcast_to(lse[w], (BAND_BLOCK, LANES))
    rows = 256
    for rc in range(DIL_CHUNK // rows):
        sl = slice(rc * rows, (rc + 1) * rows)
        lses = [lg_ref[g, sl, :] for g in range(n_g)]
        mx = functools.reduce(jnp.maximum, lses)
        ws = [jnp.exp(l - mx) for l in lses]
        den = functools.reduce(lambda a, b_: a + b_, ws)
        num = functools.reduce(lambda a, b_: a + b_, [ws[g] * og_ref[g, sl, :] for g in range(n_g)])
        o_ref[0, sl, :] = (num / den).astype(o_ref.dtype)


def _dilated_attention(hs, bias_tiles):
    b, s, _ = hs[0].shape
    in_specs = []
    args = []
    for hg in hs:
        for part in range(3):
            in_specs.append(pl.BlockSpec((1, s, DIL_HEAD_DIM),
                                         lambda bi, hi, ci, part=part: (bi, 0, part * DIL_HEADS + hi)))
            args.append(hg)
    in_specs.append(pl.BlockSpec((len(DIL_GROUPS), 1, BAND_BLOCK, 2 * BAND_BLOCK),
                                 lambda bi, hi, ci: (0, hi, 0, 0)))
    return pl.pallas_call(
        functools.partial(_dil_kernel, seq=s),
        grid=(b, DIL_HEADS, s // DIL_CHUNK),
        in_specs=in_specs,
        out_specs=pl.BlockSpec((1, DIL_CHUNK, DIL_HEAD_DIM), lambda bi, hi, ci: (bi, ci, hi)),
        out_shape=jax.ShapeDtypeStruct((b, s, DIL_HEADS * DIL_HEAD_DIM), BF16),
        scratch_shapes=[pltpu.VMEM((len(DIL_GROUPS), DIL_CHUNK, LANES), F32),
                        pltpu.VMEM((len(DIL_GROUPS), DIL_CHUNK, LANES), F32)],
        compiler_params=pltpu.CompilerParams(dimension_semantics=("parallel", "parallel", "arbitrary")),
        name="dilated_attention",
    )(*args, bias_tiles)


def _outproj_kernel(*refs, n_in, with_router):
    ins = refs[:n_in]
    ws = refs[n_in:2 * n_in]
    x_ref, g_ref, b_ref = refs[2 * n_in:2 * n_in + 3]
    rest = refs[2 * n_in + 3:]
    mix = _dot(ins[0][...], ws[0][...])
    for a_ref, w_ref in zip(ins[1:], ws[1:]):
        mix = mix + _dot(a_ref[...], w_ref[...])
    xn = _layer_norm(DEEPNORM_ALPHA * x_ref[...] + mix, g_ref[...], b_ref[...])
    if not with_router:
        o_ref, ob_ref = rest
    else:
        router_ref, o_ref, ob_ref, gates_ref, counts_ref, run_ref, tri_ref = rest
        tm = xn.shape[0]

        @pl.when(pl.program_id(0) == 0)
        def _():
            run_ref[...] = jnp.zeros_like(run_ref)
            tri_ref[...] = (lax.broadcasted_iota(jnp.int32, (tm, tm), 0)
                            > lax.broadcasted_iota(jnp.int32, (tm, tm), 1)).astype(BF16)

        logits = _dot(xn.astype(BF16), router_ref[...])
        lane = lax.broadcasted_iota(jnp.int32, logits.shape, 1).astype(F32)
        lg = jnp.where(lane < N_EXPERTS, logits, -jnp.inf)
        m1 = jnp.max(lg, axis=1, keepdims=True)
        i1 = jnp.min(jnp.where(lg == m1, lane, float(LANES)), axis=1, keepdims=True)
        lg2 = jnp.where(lane == i1, -jnp.inf, lg)
        m2 = jnp.max(lg2, axis=1, keepdims=True)
        i2 = jnp.min(jnp.where(lg2 == m2, lane, float(LANES)), axis=1, keepdims=True)
        e2 = jnp.exp(m2 - m1)
        den = 1.0 + e2
        oh1 = jnp.where(lane == i1, 1.0, 0.0)
        oh2 = jnp.where(lane == i2, 1.0, 0.0)
        both = oh1 + oh2
        before = _dot(tri_ref[...], both.astype(BF16)) + run_ref[...]
        r1 = jnp.sum(before * oh1, axis=1, keepdims=True)
        r2 = jnp.sum(before * oh2, axis=1, keepdims=True)
        run_ref[...] += jnp.sum(both, axis=0, keepdims=True)
        counts_ref[...] = run_ref[...]
        gates_ref[...] = (jnp.where(lane == 0.0, i1, 0.0) + jnp.where(lane == 1.0, i2, 0.0)
                          + jnp.where(lane == 2.0, 1.0 / den, 0.0) + jnp.where(lane == 3.0, e2 / den, 0.0)
                          + jnp.where(lane == 4.0, r1, 0.0) + jnp.where(lane == 5.0, r2, 0.0))
    o_ref[...] = xn
    ob_ref[...] = xn.astype(BF16)


def _outproj_ln(ins, ws, x, ln_g, ln_b, router=None, tm=512):
    t, d = x.shape
    n_in = len(ins)
    row = lambda width: pl.BlockSpec((tm, width), lambda i: (i, 0))
    full = lambda a: pl.BlockSpec(a.shape, lambda i: (0, 0))
    vec = lambda v: v.reshape(1, d).astype(F32)
    args = list(ins) + list(ws) + [x, vec(ln_g), vec(ln_b)]
    in_specs = [row(a.shape[1]) for a in ins] + [full(w) for w in ws] + [row(d), full(vec(ln_g)), full(vec(ln_b))]
    out_shape = [jax.ShapeDtypeStruct((t, d), F32), jax.ShapeDtypeStruct((t, d), BF16)]
    out_specs = [row(d), row(d)]
    if router is not None:
        router_pad = jnp.zeros((d, LANES), BF16).at[:, :N_EXPERTS].set(router.astype(BF16))
        args.append(router_pad)
        in_specs.append(full(router_pad))
        out_shape += [jax.ShapeDtypeStruct((t, LANES), F32), jax.ShapeDtypeStruct((1, LANES), F32)]
        out_specs += [row(LANES), pl.BlockSpec((1, LANES), lambda i: (0, 0))]
        scratch = [pltpu.VMEM((1, LANES), F32), pltpu.VMEM((tm, tm), BF16)]
    else:
        scratch = []
    semantics = "arbitrary" if router is not None else "parallel"
    return pl.pallas_call(
        functools.partial(_outproj_kernel, n_in=n_in, with_router=router is not None),
        grid=(t // tm,),
        in_specs=in_specs,
        out_specs=out_specs,
        out_shape=out_shape,
        scratch_shapes=scratch,
        compiler_params=pltpu.CompilerParams(dimension_semantics=(semantics,)),
        name="outproj_ln",
    )(*args)


def _ffn_kernel(xb_ref, x_ref, wg_ref, wu_ref, wd_ref, g_ref, b_ref, o_ref, ob_ref, acc_ref):
    f = pl.program_id(1)

    @pl.when(f == 0)
    def _():
        acc_ref[...] = jnp.zeros_like(acc_ref)

    xb = xb_ref[...]
    hg = _dot(xb, wg_ref[...])
    hu = _dot(xb, wu_ref[...])
    act = (hg * _sigmoid(hg) * hu).astype(BF16)
    acc_ref[...] += _dot(act, wd_ref[...])

    @pl.when(f == pl.num_programs(1) - 1)
    def _():
        xn = _layer_norm(DEEPNORM_ALPHA * x_ref[...] + acc_ref[...], g_ref[...], b_ref[...])
        o_ref[...] = xn
        ob_ref[...] = xn.astype(BF16)


def _ffn_ln(xb, x, wg, wu, wd, ln_g, ln_b, tm=512, tf=1408):
    t, d = x.shape
    ff = wg.shape[1]
    vec = lambda v: v.reshape(1, d).astype(F32)
    return pl.pallas_call(
        _ffn_kernel,
        grid=(t // tm, ff // tf),
        in_specs=[pl.BlockSpec((tm, d), lambda i, f: (i, 0)),
                  pl.BlockSpec((tm, d), lambda i, f: (i, 0)),
                  pl.BlockSpec((d, tf), lambda i, f: (0, f)),
                  pl.BlockSpec((d, tf), lambda i, f: (0, f)),
                  pl.BlockSpec((tf, d), lambda i, f: (f, 0)),
                  pl.BlockSpec((1, d), lambda i, f: (0, 0)),
                  pl.BlockSpec((1, d), lambda i, f: (0, 0))],
        out_specs=[pl.BlockSpec((tm, d), lambda i, f: (i, 0)),
                   pl.BlockSpec((tm, d), lambda i, f: (i, 0))],
        out_shape=[jax.ShapeDtypeStruct((t, d), F32), jax.ShapeDtypeStruct((t, d), BF16)],
        scratch_shapes=[pltpu.VMEM((tm, d), F32)],
        compiler_params=pltpu.CompilerParams(dimension_semantics=("parallel", "arbitrary")),
        name="ffn_ln",
    )(xb, x, wg, wu, wd, vec(ln_g), vec(ln_b))


_MOE_TM = 1024


def _route(route, counts, n_tiles):
    expert = route[:, 0:2].astype(jnp.int32)
    rank = route[:, 4:6].astype(jnp.int32)
    counts = counts[0, :N_EXPERTS].astype(jnp.int32)
    padded = (counts + _MOE_TM - 1) // _MOE_TM * _MOE_TM
    ends = jnp.cumsum(padded)
    starts = ends - padded
    onehot = expert[:, :, None] == jnp.arange(N_EXPERTS, dtype=jnp.int32)
    pos = (rank + jnp.sum(jnp.where(onehot, starts, 0), axis=-1)).reshape(-1)
    tile_start = jnp.arange(n_tiles, dtype=jnp.int32) * _MOE_TM
    tile_expert = jnp.minimum(jnp.sum((tile_start[:, None] >= ends[None, :]).astype(jnp.int32), axis=1),
                              N_EXPERTS - 1)
    return pos.astype(jnp.int32), tile_expert.astype(jnp.int32)


_ROW_TILE = 8


def _to_row_tiles(dst_ref, value):
    rows = value.shape[0]
    for c in range(_ROW_TILE):
        dst_ref[pl.ds(c, rows, stride=_ROW_TILE), :] = value[:, c * LANES:(c + 1) * LANES]


def _from_row_tiles(src_ref, rows, lead=()):
    parts = [src_ref[lead + (pl.ds(c, rows, stride=_ROW_TILE), slice(None))] for c in range(_ROW_TILE)]
    return jnp.concatenate(parts, axis=1)


def _row_scatter_kernel(pos_ref, x_ref, init_ref, o_ref, xt_ref, sem, *, tm):
    del init_ref
    _to_row_tiles(xt_ref, x_ref[...])

    def body(j, carry):
        src = xt_ref.at[pl.ds(pl.multiple_of(j * _ROW_TILE, _ROW_TILE), _ROW_TILE)]
        for choice in range(2):
            p = pos_ref[0, 0, 2 * j + choice]
            dst = o_ref.at[pl.ds(pl.multiple_of(p * _ROW_TILE, _ROW_TILE), _ROW_TILE)]
            pltpu.make_async_copy(src, dst, sem).start()
        return carry

    lax.fori_loop(0, tm, body, 0, unroll=8)
    for _ in range(2):
        pltpu.make_async_copy(xt_ref, o_ref.at[pl.ds(0, tm * _ROW_TILE)], sem).wait()


def _row_scatter(x, pos, n_rows, tm=512):
    t, d = x.shape
    assert d == _ROW_TILE * LANES
    return pl.pallas_call(
        functools.partial(_row_scatter_kernel, tm=tm),
        grid=(t // tm,),
        in_specs=[pl.BlockSpec((1, 1, 2 * tm), lambda i: (i, 0, 0), memory_space=pltpu.SMEM),
                  pl.BlockSpec((tm, d), lambda i: (i, 0)),
                  pl.BlockSpec(memory_space=pl.ANY)],
        out_specs=pl.BlockSpec(memory_space=pl.ANY),
        out_shape=jax.ShapeDtypeStruct((n_rows * _ROW_TILE, LANES), x.dtype),
        scratch_shapes=[pltpu.VMEM((tm * _ROW_TILE, LANES), x.dtype), pltpu.SemaphoreType.DMA(())],
        input_output_aliases={2: 0},
        compiler_params=pltpu.CompilerParams(dimension_semantics=("arbitrary",)),
        name="moe_row_scatter",
    )(pos.reshape(t // tm, 1, 2 * tm), x, jnp.zeros((n_rows * _ROW_TILE, LANES), x.dtype))


def _expert_ffn_kernel(te_ref, x_ref, wg_ref, wu_ref, wd_ref, o_ref, xb_ref, acc_ref):
    del te_ref
    f = pl.program_id(1)

    @pl.when(f == 0)
    def _():
        acc_ref[...] = jnp.zeros_like(acc_ref)
        xb_ref[...] = _from_row_tiles(x_ref, _MOE_TM).astype(BF16)

    xb = xb_ref[...]
    hg = _dot(xb, wg_ref[0].astype(BF16))
    hu = _dot(xb, wu_ref[0].astype(BF16))
    act = (hg * _sigmoid(hg) * hu).astype(BF16)
    acc_ref[...] += _dot(act, wd_ref[0].astype(BF16))

    @pl.when(f == pl.num_programs(1) - 1)
    def _():
        _to_row_tiles(o_ref, acc_ref[...])


def _expert_ffn(xs, tile_expert, wg, wu, wd, tf=512):
    n_rows = xs.shape[0] // _ROW_TILE
    d = _ROW_TILE * LANES
    ff = wg.shape[2]
    tile_rows = _MOE_TM * _ROW_TILE
    grid_spec = pltpu.PrefetchScalarGridSpec(
        num_scalar_prefetch=1,
        grid=(n_rows // _MOE_TM, ff // tf),
        in_specs=[pl.BlockSpec((tile_rows, LANES), lambda i, f, te: (i, 0)),
                  pl.BlockSpec((1, d, tf), lambda i, f, te: (te[i], 0, f)),
                  pl.BlockSpec((1, d, tf), lambda i, f, te: (te[i], 0, f)),
                  pl.BlockSpec((1, tf, d), lambda i, f, te: (te[i], f, 0))],
        out_specs=pl.BlockSpec((tile_rows, LANES), lambda i, f, te: (i, 0)),
        scratch_shapes=[pltpu.VMEM((_MOE_TM, d), BF16), pltpu.VMEM((_MOE_TM, d), F32)])
    return pl.pallas_call(
        _expert_ffn_kernel,
        grid_spec=grid_spec,
        out_shape=jax.ShapeDtypeStruct(xs.shape, F32),
        compiler_params=pltpu.CompilerParams(dimension_semantics=("parallel", "arbitrary")),
        name="moe_expert_ffn",
    )(tile_expert, xs, wg, wu, wd)


def _combine_kernel(pos_ref, route_ref, x_ref, g_ref, b_ref, y_ref, o_ref, ybuf_ref, sem, *, tm):
    def body(j, carry):
        for choice in range(2):
            p = pos_ref[0, 0, 2 * j + choice]
            src = y_ref.at[pl.ds(pl.multiple_of(p * _ROW_TILE, _ROW_TILE), _ROW_TILE)]
            dst = ybuf_ref.at[choice, pl.ds(pl.multiple_of(j * _ROW_TILE, _ROW_TILE), _ROW_TILE)]
            pltpu.make_async_copy(src, dst, sem).start()
        return carry

    lax.fori_loop(0, tm, body, 0, unroll=8)
    for choice in range(2):
        pltpu.make_async_copy(y_ref.at[pl.ds(0, tm * _ROW_TILE)], ybuf_ref.at[choice], sem).wait()
    route = route_ref[...]
    y = route[:, 2:3] * _from_row_tiles(ybuf_ref, tm, (0,)) + route[:, 3:4] * _from_row_tiles(ybuf_ref, tm, (1,))
    o_ref[...] = _layer_norm(DEEPNORM_ALPHA * x_ref[...] + y, g_ref[...], b_ref[...])


def _combine_ln(ys, pos, route, x, ln_g, ln_b, tm=512):
    t, d = x.shape
    vec = lambda v: v.reshape(1, d).astype(F32)
    return pl.pallas_call(
        functools.partial(_combine_kernel, tm=tm),
        grid=(t // tm,),
        in_specs=[pl.BlockSpec((1, 1, 2 * tm), lambda i: (i, 0, 0), memory_space=pltpu.SMEM),
                  pl.BlockSpec((tm, LANES), lambda i: (i, 0)),
                  pl.BlockSpec((tm, d), lambda i: (i, 0)),
                  pl.BlockSpec((1, d), lambda i: (0, 0)),
                  pl.BlockSpec((1, d), lambda i: (0, 0)),
                  pl.BlockSpec(memory_space=pl.ANY)],
        out_specs=pl.BlockSpec((tm, d), lambda i: (i, 0)),
        out_shape=jax.ShapeDtypeStruct((t, d), F32),
        scratch_shapes=[pltpu.VMEM((2, tm * _ROW_TILE, LANES), F32), pltpu.SemaphoreType.DMA(())],
        compiler_params=pltpu.CompilerParams(dimension_semantics=("arbitrary",)),
        name="moe_combine_ln",
    )(pos.reshape(t // tm, 1, 2 * tm), route, x, vec(ln_g), vec(ln_b), ys)


def _moe_ln(x, route, counts, wg, wu, wd, ln_g, ln_b):
    t, _ = x.shape
    n_tiles = 2 * t // _MOE_TM + N_EXPERTS
    pos, tile_expert = _route(route, counts, n_tiles)
    xs = _row_scatter(x, pos, n_tiles * _MOE_TM)
    ys = _expert_ffn(xs, tile_expert, wg, wu, wd)
    return _combine_ln(ys, pos, route, x, ln_g, ln_b)


def _even_layer(x, xb, batch, seq, dsa_bias, w_in, conv_w, conv_b, conv_ln_g, conv_ln_b, w_out,
                ln1_g, ln1_b, ffn_wg, ffn_wu, ffn_wd, ln2_g, ln2_b):
    w_in_p = jnp.zeros((D_MODEL, EVEN_IN_PAD), BF16).at[:, :w_in.shape[1]].set(w_in.astype(BF16))
    h = _matmul(xb, w_in_p, 2048, 640, BF16).reshape(batch, seq, EVEN_IN_PAD)
    a_out = _conformer_conv(h, conv_w, conv_b, conv_ln_g, conv_ln_b)
    att = _dsa_attention(h, dsa_bias)
    w_out_b = w_out.astype(BF16)
    x1, x1b = _outproj_ln([a_out.reshape(-1, CONV_CH), att.reshape(-1, DSA_HEADS * DSA_HEAD_DIM)],
                          [w_out_b[:CONV_CH], w_out_b[CONV_CH:]], x, ln1_g, ln1_b)
    return _ffn_ln(x1b, x1, ffn_wg.astype(BF16), ffn_wu.astype(BF16), ffn_wd.astype(BF16), ln2_g, ln2_b)


def _odd_layer(x, xb, batch, seq, dil_bias, w_in, w_out, ln1_g, ln1_b, router, moe_wg, moe_wu, moe_wd,
               ln2_g, ln2_b):
    gw = 3 * DIL_HEADS * DIL_HEAD_DIM
    w_in_b = w_in.astype(BF16)
    hs = [_matmul_residue_major(xb, w_in_b[:, g * gw:(g + 1) * gw], batch, seq, dil, 2048, 768)
          for g, (_, dil) in enumerate(DIL_GROUPS)]
    o = _dilated_attention(hs, dil_bias).reshape(-1, DIL_HEADS * DIL_HEAD_DIM)
    x1, _, route, counts = _outproj_ln([o], [w_out.astype(BF16)], x, ln1_g, ln1_b, router=router)
    out = _moe_ln(x1, route, counts, moe_wg, moe_wu, moe_wd, ln2_g, ln2_b)
    return out, out.astype(BF16)


def kernel(x, rel_bias, even_w_in, even_conv_w, even_conv_b, even_conv_ln_g, even_conv_ln_b, even_w_out, even_ln1_g, even_ln1_b, even_ffn_wg, even_ffn_wu, even_ffn_wd, even_ln2_g, even_ln2_b, odd_w_in, odd_w_out, odd_ln1_g, odd_ln1_b, odd_router, odd_moe_wg, odd_moe_wu, odd_moe_wd, odd_ln2_g, odd_ln2_b):
    batch, seq, d = x.shape
    assert d == D_MODEL and seq % DIL_CHUNK == 0
    depth = even_w_in.shape[0] + odd_w_in.shape[0]
    assert depth == DEPTH
    dsa_bias = _dsa_bias_tiles(rel_bias, seq)
    dil_bias = _dil_bias_tiles(rel_bias)
    xf = x.reshape(batch * seq, d)
    xb = xf.astype(BF16)
    for layer in range(depth):
        i = layer // 2
        if layer % 2 == 0:
            xf, xb = _even_layer(xf, xb, batch, seq, dsa_bias, even_w_in[i], even_conv_w[i], even_conv_b[i],
                                 even_conv_ln_g[i], even_conv_ln_b[i], even_w_out[i], even_ln1_g[i],
                                 even_ln1_b[i], even_ffn_wg[i], even_ffn_wu[i], even_ffn_wd[i],
                                 even_ln2_g[i], even_ln2_b[i])
        else:
            xf, xb = _odd_layer(xf, xb, batch, seq, dil_bias, odd_w_in[i], odd_w_out[i], odd_ln1_g[i],
                                odd_ln1_b[i], odd_router[i], odd_moe_wg[i], odd_moe_wu[i], odd_moe_wd[i],
                                odd_ln2_g[i], odd_ln2_b[i])
    return xf.reshape(batch, seq, d)
```

```python
import functools
import math

import jax
import jax.numpy as jnp
from jax import lax
from jax.experimental import pallas as pl
from jax.experimental.pallas import tpu as pltpu

BF16 = jnp.bfloat16
F32 = jnp.float32

D_MODEL = 1024
CONV_CH = 512
CONV_WIDTH = 31
DSA_HEADS = 8
DSA_HEAD_DIM = 64
IDX_HEADS = 8
IDX_DIM = 64
DSA_TOPK_MAX = 256
Q_BLOCK = 128
DIL_GROUPS = ((128, 1), (512, 4), (2048, 16))
DIL_HEADS = 8
DIL_HEAD_DIM = 128
BAND_BLOCK = 128
NUM_BUCKETS = 32
MAX_DISTANCE = 2048
N_EXPERTS = 8
LN_EPS = 1e-5
NEG = -1e30
DEPTH = 2
DEEPNORM_ALPHA = (2 * DEPTH) ** 0.25

LANES = 128
EVEN_IN_PAD = 3200
INT_MIN = -(2 ** 31)
DIL_CHUNK = BAND_BLOCK * max(d for _, d in DIL_GROUPS)

_PARALLEL2 = pltpu.CompilerParams(dimension_semantics=("parallel", "parallel"))


def _layer_norm(y, g, b):
    mu = jnp.mean(y, axis=-1, keepdims=True)
    yc = y - mu
    var = jnp.mean(yc * yc, axis=-1, keepdims=True)
    return yc * lax.rsqrt(var + LN_EPS) * g + b


def _sigmoid(x):
    return 1.0 / (1.0 + jnp.exp(-x))


def _dot_nt(a, b):
    return lax.dot_general(a, b, (((1,), (1,)), ((), ())), preferred_element_type=F32)


def _dot(a, b):
    return jnp.dot(a, b, preferred_element_type=F32)


def _mm_kernel(x_ref, w_ref, o_ref):
    o_ref[...] = _dot(x_ref[...].astype(BF16), w_ref[...]).astype(o_ref.dtype)


def _matmul(x, w, tm, tn, out_dtype):
    t, k = x.shape
    n = w.shape[1]
    return pl.pallas_call(
        _mm_kernel,
        grid=(t // tm, n // tn),
        in_specs=[pl.BlockSpec((tm, k), lambda i, j: (i, 0)),
                  pl.BlockSpec((k, tn), lambda i, j: (0, j))],
        out_specs=pl.BlockSpec((tm, tn), lambda i, j: (i, j)),
        out_shape=jax.ShapeDtypeStruct((t, n), out_dtype),
        compiler_params=_PARALLEL2,
        name="matmul",
    )(x, w)


def _mm_perm_kernel(x_ref, w_ref, o_ref, acc_ref, *, dil, tm, tn):
    acc = _dot(x_ref[...].astype(BF16), w_ref[...])
    rows = tm // dil
    for c in range(tn // LANES):
        acc_ref[c] = acc[:, c * LANES:(c + 1) * LANES]
    for r in range(dil):
        for c in range(tn // LANES):
            o_ref[0, r, :, c * LANES:(c + 1) * LANES] = (
                acc_ref[c, pl.ds(r, rows, stride=dil), :].astype(o_ref.dtype))


def _matmul_residue_major(x, w, batch, seq, dil, tm, tn):
    t, k = x.shape
    n = w.shape[1]
    if dil == 1:
        return _matmul(x, w, tm, tn, BF16).reshape(batch, seq, n)
    tiles_per_seq = seq // tm
    out = pl.pallas_call(
        functools.partial(_mm_perm_kernel, dil=dil, tm=tm, tn=tn),
        grid=(t // tm, n // tn),
        in_specs=[pl.BlockSpec((tm, k), lambda i, j: (i, 0)),
                  pl.BlockSpec((k, tn), lambda i, j: (0, j))],
        out_specs=pl.BlockSpec((1, dil, tm // dil, tn),
                               lambda i, j: (i // tiles_per_seq, 0, i % tiles_per_seq, j)),
        out_shape=jax.ShapeDtypeStruct((batch, dil, seq // dil, n), BF16),
        scratch_shapes=[pltpu.VMEM((tn // LANES, tm, LANES), F32)],
        compiler_params=_PARALLEL2,
        name="matmul_residue_major",
    )(x, w)
    return out.reshape(batch, seq, n)


_CONV_HALO = 32
_CONV_ROWS = 64
_SUBLANES = 8


def _conv_kernel(val_ref, gate_ref, w_ref, cb_ref, g_ref, b_ref, o_ref, ext_ref, *, ts):
    s = pl.program_id(1)

    @pl.when(s == 0)
    def _():
        ext_ref[0, 0:_CONV_HALO, :] = jnp.zeros((_CONV_HALO, CONV_CH), F32)

    val = val_ref[0].astype(F32)
    gate = gate_ref[0].astype(F32)
    ext_ref[0, _CONV_HALO:_CONV_HALO + ts, :] = val * _sigmoid(gate)
    span = ts + _CONV_HALO - _SUBLANES
    for k in range(1, _SUBLANES):
        ext_ref[k, 0:span, :] = ext_ref[0, k:k + span, :]
    first_tap = _CONV_HALO - (CONV_WIDTH - 1)
    for rc in range(ts // _CONV_ROWS):
        r0 = rc * _CONV_ROWS
        acc = jnp.zeros((_CONV_ROWS, CONV_CH), F32)
        for j in range(CONV_WIDTH):
            shift = (first_tap + j) % _SUBLANES
            lo = r0 + first_tap + j - shift
            acc = acc + ext_ref[shift, lo:lo + _CONV_ROWS, :] * w_ref[j:j + 1, :]
        y = _layer_norm(acc + cb_ref[...], g_ref[...], b_ref[...])
        o_ref[0, r0:r0 + _CONV_ROWS, :] = (y * _sigmoid(y)).astype(o_ref.dtype)
    ext_ref[0, 0:_CONV_HALO, :] = ext_ref[0, ts:ts + _CONV_HALO, :]


def _conformer_conv(h, conv_w, conv_b, ln_g, ln_b, ts=512):
    b, s, _ = h.shape
    w_pad = jnp.zeros((32, CONV_CH), F32).at[:CONV_WIDTH].set(conv_w)
    vec = lambda v: v.reshape(1, CONV_CH).astype(F32)
    full = lambda shape: pl.BlockSpec(shape, lambda bi, si: (0, 0))
    return pl.pallas_call(
        functools.partial(_conv_kernel, ts=ts),
        grid=(b, s // ts),
        in_specs=[pl.BlockSpec((1, ts, CONV_CH), lambda bi, si: (bi, si, 0)),
                  pl.BlockSpec((1, ts, CONV_CH), lambda bi, si: (bi, si, 1)),
                  full((32, CONV_CH)), full((1, CONV_CH)), full((1, CONV_CH)), full((1, CONV_CH))],
        out_specs=pl.BlockSpec((1, ts, CONV_CH), lambda bi, si: (bi, si, 0)),
        out_shape=jax.ShapeDtypeStruct((b, s, CONV_CH), BF16),
        scratch_shapes=[pltpu.VMEM((_SUBLANES, ts + _CONV_HALO, CONV_CH), F32)],
        compiler_params=pltpu.CompilerParams(dimension_semantics=("parallel", "arbitrary")),
        name="conformer_conv",
    )(h, h, w_pad, vec(conv_b), vec(ln_g), vec(ln_b))


def _rel_bucket(dist):
    max_exact = NUM_BUCKETS // 2
    n = dist.astype(jnp.int32)
    nf = jnp.maximum(n, 1).astype(F32)
    large = max_exact + (jnp.log(nf / max_exact) / math.log(MAX_DISTANCE / max_exact)
                         * (NUM_BUCKETS - max_exact)).astype(jnp.int32)
    large = jnp.minimum(large, NUM_BUCKETS - 1)
    return jnp.where(n < max_exact, n, large)


def _bias_by_distance(rel_bias, n):
    return rel_bias[_rel_bucket(jnp.arange(n, dtype=jnp.int32))].astype(F32).T


def _toeplitz_kernel(win_ref, o_ref):
    n_win, span = win_ref.shape
    for n in range(n_win):
        y = jnp.broadcast_to(win_ref[n:n + 1, :], (LANES, span))
        y = pltpu.roll(y, 0, 1, stride=1, stride_axis=0)
        o_ref[n] = y[:, LANES:]


def _toeplitz_tiles(windows, per_step=8):
    n, span = windows.shape
    width = span - LANES
    return pl.pallas_call(
        _toeplitz_kernel,
        grid=(n // per_step,),
        in_specs=[pl.BlockSpec((per_step, span), lambda i: (i, 0))],
        out_specs=pl.BlockSpec((per_step, LANES, width), lambda i: (i, 0, 0)),
        out_shape=jax.ShapeDtypeStruct((n, LANES, width), F32),
        compiler_params=pltpu.CompilerParams(dimension_semantics=("parallel",)),
        name="bias_toeplitz",
    )(windows)


def _dsa_bias_tiles(rel_bias, seq):
    nblk = seq // Q_BLOCK
    heads = rel_bias.shape[1]
    f = _bias_by_distance(rel_bias, seq)
    u = jnp.concatenate([jnp.broadcast_to(f[:, :1], (heads, 2 * Q_BLOCK)), f], axis=1)
    blocks = u.reshape(heads, nblk + 2, Q_BLOCK)
    windows = jnp.concatenate([blocks[:, :-1], blocks[:, 1:]], axis=-1)
    windows = windows.transpose(1, 0, 2).reshape((nblk + 1) * heads, 2 * Q_BLOCK)
    return _toeplitz_tiles(windows).reshape(nblk + 1, heads, Q_BLOCK, Q_BLOCK)


def _dil_bias_tiles(rel_bias):
    max_dil = max(d for _, d in DIL_GROUPS)
    heads = rel_bias.shape[1]
    span = 2 * BAND_BLOCK
    f = _bias_by_distance(rel_bias, span * max_dil + 1)
    windows = []
    for _, dil in DIL_GROUPS:
        fd = f[:, 0:span * dil + 1:dil]
        tail = jnp.broadcast_to(fd[:, :1], (heads, BAND_BLOCK - 1))
        windows.append(jnp.concatenate([fd[:, ::-1], tail], axis=1))
    tiles = _toeplitz_tiles(jnp.concatenate(windows, axis=0))
    return tiles.reshape(len(DIL_GROUPS), heads, BAND_BLOCK, span)


_KC = 2 * Q_BLOCK
_VT_ROWS = DSA_HEAD_DIM + 16


def _sortable_key(score):
    score = jnp.where(score == 0.0, 0.0, score)
    bits = lax.bitcast_convert_type(score, jnp.int32)
    return jnp.where(bits < 0, bits ^ jnp.int32(0x7FFFFFFF), bits)


def _rows_reduce(x, op):
    rows = x.shape[0]
    return op(op(x.reshape(rows // 8, 8, LANES), axis=0), axis=0, keepdims=True)


def _keys_reduce(x, op):
    h, keys, _ = x.shape
    return op(op(x.reshape(h, keys // 8, 8, LANES), axis=1), axis=1, keepdims=True)


def _dsa_t_kernel(q_ref, qi_ref, wq_ref, k_ref, vt_ref, ki_ref, bias_hbm, o_ref,
                  bias_ref, keys_ref, qm_ref, acc_ref, dots_a_ref, dots_b_ref, sem, *, topk):
    b = pl.program_id(0)
    i = pl.program_id(1)

    @pl.when((b == 0) & (i == 0))
    def _():
        cp = pltpu.make_async_copy(bias_hbm, bias_ref, sem)
        cp.start()
        cp.wait()

    nch = (i + 2) // 2
    key_off = lax.broadcasted_iota(jnp.int32, (_KC, LANES), 0)
    q_pos = i * Q_BLOCK + lax.broadcasted_iota(jnp.int32, (_KC, LANES), 1)
    lane = lax.broadcasted_iota(jnp.int32, (Q_BLOCK, LANES), 1)
    low_half = lane < DSA_HEAD_DIM

    w_t = (wq_ref[0].astype(F32) * (IDX_HEADS ** -0.5 * IDX_DIM ** -0.5)).T
    qi = qi_ref[0].astype(F32)
    qi_heads = []
    for p in range(IDX_HEADS // 2):
        pair = qi[:, p * LANES:(p + 1) * LANES]
        qi_heads.append(jnp.where(low_half, pair, 0.0))
        qi_heads.append(jnp.where(low_half, pltpu.roll(pair, IDX_DIM, 1), 0.0))
    qi_stack = jnp.concatenate(qi_heads, axis=0).astype(BF16)
    q_scaled = (q_ref[0].astype(F32) * (DSA_HEAD_DIM ** -0.5)).astype(BF16)
    for p in range(DSA_HEADS // 2):
        qp = q_scaled[:, p * LANES:(p + 1) * LANES]
        qm_ref[2 * p] = jnp.where(low_half, qp, jnp.zeros_like(qp))
        qm_ref[2 * p + 1] = jnp.where(low_half, jnp.zeros_like(qp), qp)

    def index_dots(c, dst_ref):
        c = jnp.minimum(c, nch - 1)
        k0 = pl.multiple_of(c * _KC, _KC)
        ki = ki_ref[0, pl.ds(k0, _KC), :]
        dst_ref[...] = _dot_nt(ki, qi_stack)

    def consume_dots(c, src_ref):
        c = jnp.minimum(c, nch - 1)
        k0 = pl.multiple_of(c * _KC, _KC)
        score = jnp.zeros((_KC, LANES), F32)
        for h in range(IDX_HEADS):
            w_row = w_t[IDX_DIM + h:IDX_DIM + h + 1, :]
            score = score + w_row * jnp.maximum(src_ref[:, h * Q_BLOCK:(h + 1) * Q_BLOCK], 0.0)
        keys_ref[c] = jnp.where(k0 + key_off <= q_pos, _sortable_key(score), jnp.int32(INT_MIN))

    index_dots(0, dots_a_ref)

    def score_body(c2, carry):
        index_dots(2 * c2 + 1, dots_b_ref)
        consume_dots(2 * c2, dots_a_ref)
        index_dots(2 * c2 + 2, dots_a_ref)
        consume_dots(2 * c2 + 1, dots_b_ref)
        return carry

    lax.fori_loop(0, (nch + 1) // 2, score_body, 0)

    def count_ge(cand):
        cb = jnp.broadcast_to(cand, (8, LANES))
        chains = 4

        def body(c, acc):
            kk = keys_ref[c].reshape(_KC // (8 * chains), chains, 8, LANES)
            return acc + jnp.sum(jnp.where(kk >= cb, 1.0, 0.0), axis=0)

        acc = lax.fori_loop(0, nch, body, jnp.zeros((chains, 8, LANES), F32))
        return jnp.sum(jnp.sum(acc, axis=0), axis=0, keepdims=True)

    kf = float(topk)
    zero = jnp.zeros((1, LANES), jnp.int32)
    thr0 = jnp.where(count_ge(zero) >= kf, zero, jnp.full((1, LANES), INT_MIN, jnp.int32))

    def bit_body(t, thr):
        cand = thr | jnp.left_shift(jnp.int32(1), 30 - t)
        return jnp.where(count_ge(cand) >= kf, cand, thr)

    thr = lax.fori_loop(0, 31, bit_body, thr0)
    n_gt = count_ge(jnp.where(thr == jnp.int32(2 ** 31 - 1), thr, thr + 1))
    need = kf - n_gt
    tri = (lax.broadcasted_iota(jnp.int32, (_KC, _KC), 1)
           <= lax.broadcasted_iota(jnp.int32, (_KC, _KC), 0)).astype(BF16)

    acc_ref[...] = jnp.zeros_like(acc_ref)

    def attn_body(c, carry):
        ms, tie_carry = carry
        k0 = pl.multiple_of(c * _KC, _KC)
        kk = keys_ref[c]
        causal = kk > jnp.int32(INT_MIN)
        eq = (kk == thr) & causal
        eq_f = jnp.where(eq, 1.0, 0.0)
        rank = _dot(tri, eq_f.astype(BF16)) + tie_carry
        sel = ((kk > thr) & causal) | (eq & (rank <= need))
        tie_carry = tie_carry + _rows_reduce(eq_f, jnp.sum)
        d = i - 2 * c
        kc = k_ref[0, pl.ds(k0, _KC), :]
        k_heads = jnp.stack([kc[:, (h // 2) * LANES:(h // 2 + 1) * LANES] for h in range(DSA_HEADS)])
        s = jnp.einsum("hkd,hqd->hkq", k_heads, qm_ref[...], preferred_element_type=F32)
        bias = jnp.concatenate([bias_ref[d + 1], bias_ref[d]], axis=1)
        s = jnp.where(sel, s + bias, NEG)
        m_new = jnp.maximum(ms, _keys_reduce(s, jnp.max))
        alpha = jnp.exp(ms - m_new)
        pr = jnp.exp((s - m_new).astype(BF16))
        pv = jnp.einsum("hdk,hkq->hdq", vt_ref[0, c], pr, preferred_element_type=F32)
        acc_ref[...] = alpha * acc_ref[...] + pv
        return m_new, tie_carry

    init = (jnp.full((DSA_HEADS, 1, LANES), NEG, F32), jnp.zeros((1, LANES), F32))
    lax.fori_loop(0, nch, attn_body, init)
    acc = acc_ref[...]
    out_t = acc[:, :DSA_HEAD_DIM] / acc[:, DSA_HEAD_DIM:DSA_HEAD_DIM + 1]
    o_ref[0] = out_t.reshape(DSA_HEADS * DSA_HEAD_DIM, Q_BLOCK).T.astype(o_ref.dtype)


def _dsa_attention(h, bias_tiles):
    b, s, _ = h.shape
    width = DSA_HEADS * DSA_HEAD_DIM
    nblk = s // Q_BLOCK
    topk = min(DSA_TOPK_MAX, s // 4)
    kw_block = (CONV_CH * 2 + 4 * width) // LANES
    v_cols = 2 * CONV_CH + 2 * width
    vt = h[:, :, v_cols:v_cols + width].reshape(b, s // _KC, _KC, DSA_HEADS, DSA_HEAD_DIM).transpose(0, 1, 3, 4, 2)
    vt = jnp.concatenate([vt, jnp.ones((b, s // _KC, DSA_HEADS, _VT_ROWS - DSA_HEAD_DIM, _KC), BF16)], axis=3)
    return pl.pallas_call(
        functools.partial(_dsa_t_kernel, topk=topk),
        grid=(b, nblk),
        in_specs=[pl.BlockSpec((1, Q_BLOCK, width), lambda bi, qi: (bi, qi, 2)),
                  pl.BlockSpec((1, Q_BLOCK, width), lambda bi, qi: (bi, qi, 5)),
                  pl.BlockSpec((1, Q_BLOCK, LANES), lambda bi, qi: (bi, qi, kw_block)),
                  pl.BlockSpec((1, s, width), lambda bi, qi: (bi, 0, 3)),
                  pl.BlockSpec((1, s // _KC, DSA_HEADS, _VT_ROWS, _KC), lambda bi, qi: (bi, 0, 0, 0, 0)),
                  pl.BlockSpec((1, s, LANES), lambda bi, qi: (bi, 0, kw_block)),
                  pl.BlockSpec(memory_space=pl.ANY)],
        out_specs=pl.BlockSpec((1, Q_BLOCK, width), lambda bi, qi: (bi, qi, 0)),
        out_shape=jax.ShapeDtypeStruct((b, s, width), BF16),
        scratch_shapes=[pltpu.VMEM((nblk + 1, DSA_HEADS, Q_BLOCK, Q_BLOCK), F32),
                        pltpu.VMEM((s // _KC, _KC, Q_BLOCK), jnp.int32),
                        pltpu.VMEM((DSA_HEADS, Q_BLOCK, LANES), BF16),
                        pltpu.VMEM((DSA_HEADS, _VT_ROWS, Q_BLOCK), F32),
                        pltpu.VMEM((_KC, IDX_HEADS * Q_BLOCK), F32),
                        pltpu.VMEM((_KC, IDX_HEADS * Q_BLOCK), F32),
                        pltpu.SemaphoreType.DMA(())],
        compiler_params=pltpu.CompilerParams(dimension_semantics=("arbitrary", "arbitrary")),
        name="dsa_attention",
    )(h, h, h, h, vt, h, bias_tiles)


_DIL_WAVE = 8


def _dil_kernel(*refs, seq):
    n_g = len(DIL_GROUPS)
    qkv = refs[:3 * n_g]
    bias_ref, o_ref, og_ref, lg_ref = refs[3 * n_g:]
    c = pl.program_id(2)
    ii = lax.broadcasted_iota(jnp.int32, (BAND_BLOCK, BAND_BLOCK), 0)
    jj = lax.broadcasted_iota(jnp.int32, (BAND_BLOCK, BAND_BLOCK), 1)
    valid_prev = jj >= ii
    valid_cur = ii >= jj
    scale = DIL_HEAD_DIM ** -0.5
    for g, (window, dil) in enumerate(DIL_GROUPS):
        assert window // dil == BAND_BLOCK
        q_ref, k_ref, v_ref = qkv[3 * g:3 * g + 3]
        sub_len = seq // dil
        blocks_per_chunk = DIL_CHUNK // (BAND_BLOCK * dil)
        bias_prev = bias_ref[g, 0, :, :BAND_BLOCK]
        bias_cur = bias_ref[g, 0, :, BAND_BLOCK:]
        tiles = [(r, nl) for r in range(dil) for nl in range(blocks_per_chunk)]
        for w0 in range(0, len(tiles), _DIL_WAVE):
            wave = tiles[w0:w0 + _DIL_WAVE]
            cur, prev, mask_prev = [], [], []
            for r, nl in wave:
                row0 = pl.multiple_of(r * sub_len + c * (DIL_CHUNK // dil) + nl * BAND_BLOCK, BAND_BLOCK)
                cur.append(pl.ds(row0, BAND_BLOCK))
                prev.append(pl.ds(pl.multiple_of(jnp.maximum(row0 - BAND_BLOCK, 0), BAND_BLOCK), BAND_BLOCK))
                mask_prev.append(valid_prev if nl > 0 else (valid_prev & (c > 0)))
            gather = lambda ref, slices: jnp.stack([ref[0, sl, :] for sl in slices])
            qb = gather(q_ref, cur)
            s_cur = jnp.einsum("wqd,wkd->wqk", qb, gather(k_ref, cur), preferred_element_type=F32)
            s_prev = jnp.einsum("wqd,wkd->wqk", qb, gather(k_ref, prev), preferred_element_type=F32)
            s_cur = jnp.where(valid_cur, s_cur * scale + bias_cur, NEG)
            s_prev = jnp.where(jnp.stack(mask_prev), s_prev * scale + bias_prev, NEG)
            m = jnp.max(jnp.maximum(s_cur, s_prev), axis=2, keepdims=True)
            p_cur = jnp.exp(s_cur - m)
            p_prev = jnp.exp(s_prev - m)
            ssum = jnp.sum(p_cur + p_prev, axis=2, keepdims=True)
            o = (jnp.einsum("wqk,wkd->wqd", p_cur.astype(BF16), gather(v_ref, cur), preferred_element_type=F32)
                 + jnp.einsum("wqk,wkd->wqd", p_prev.astype(BF16), gather(v_ref, prev),
                              preferred_element_type=F32)) / ssum
            lse = m + jnp.log(ssum)
            for w, (r, nl) in enumerate(wave):
                dst = pl.ds(r + nl * BAND_BLOCK * dil, BAND_BLOCK, stride=dil) if dil > 1 else pl.ds(
                    nl * BAND_BLOCK, BAND_BLOCK)
                og_ref[g, dst, :] = o[w]
                lg_ref[g, dst, :] = jnp.broadcast_to(lse[w], (BAND_BLOCK, LANES))
    rows = 256
    for rc in range(DIL_CHUNK // rows):
        sl = slice(rc * rows, (rc + 1) * rows)
        lses = [lg_ref[g, sl, :] for g in range(n_g)]
        mx = functools.reduce(jnp.maximum, lses)
        ws = [jnp.exp(l - mx) for l in lses]
        den = functools.reduce(lambda a, b_: a + b_, ws)
        num = functools.reduce(lambda a, b_: a + b_, [ws[g] * og_ref[g, sl, :] for g in range(n_g)])
        o_ref[0, sl, :] = (num / den).astype(o_ref.dtype)


def _dilated_attention(hs, bias_tiles):
    b, s, _ = hs[0].shape
    in_specs = []
    args = []
    for hg in hs:
        for part in range(3):
            in_specs.append(pl.BlockSpec((1, s, DIL_HEAD_DIM),
                                         lambda bi, hi, ci, part=part: (bi, 0, part * DIL_HEADS + hi)))
            args.append(hg)
    in_specs.append(pl.BlockSpec((len(DIL_GROUPS), 1, BAND_BLOCK, 2 * BAND_BLOCK),
                                 lambda bi, hi, ci: (0, hi, 0, 0)))
    return pl.pallas_call(
        functools.partial(_dil_kernel, seq=s),
        grid=(b, DIL_HEADS, s // DIL_CHUNK),
        in_specs=in_specs,
        out_specs=pl.BlockSpec((1, DIL_CHUNK, DIL_HEAD_DIM), lambda bi, hi, ci: (bi, ci, hi)),
        out_shape=jax.ShapeDtypeStruct((b, s, DIL_HEADS * DIL_HEAD_DIM), BF16),
        scratch_shapes=[pltpu.VMEM((len(DIL_GROUPS), DIL_CHUNK, LANES), F32),
                        pltpu.VMEM((len(DIL_GROUPS), DIL_CHUNK, LANES), F32)],
        compiler_params=pltpu.CompilerParams(dimension_semantics=("parallel", "parallel", "arbitrary")),
        name="dilated_attention",
    )(*args, bias_tiles)


def _outproj_kernel(*refs, n_in, with_router):
    ins = refs[:n_in]
    ws = refs[n_in:2 * n_in]
    x_ref, g_ref, b_ref = refs[2 * n_in:2 * n_in + 3]
    rest = refs[2 * n_in + 3:]
    mix = _dot(ins[0][...], ws[0][...])
    for a_ref, w_ref in zip(ins[1:], ws[1:]):
        mix = mix + _dot(a_ref[...], w_ref[...])
    xn = _layer_norm(DEEPNORM_ALPHA * x_ref[...] + mix, g_ref[...], b_ref[...])
    if not with_router:
        o_ref, ob_ref = rest
    else:
        router_ref, o_ref, ob_ref, gates_ref, counts_ref, run_ref, tri_ref = rest
        tm = xn.shape[0]

        @pl.when(pl.program_id(0) == 0)
        def _():
            run_ref[...] = jnp.zeros_like(run_ref)
            tri_ref[...] = (lax.broadcasted_iota(jnp.int32, (tm, tm), 0)
                            > lax.broadcasted_iota(jnp.int32, (tm, tm), 1)).astype(BF16)

        logits = _dot(xn.astype(BF16), router_ref[...])
        lane = lax.broadcasted_iota(jnp.int32, logits.shape, 1).astype(F32)
        lg = jnp.where(lane < N_EXPERTS, logits, -jnp.inf)
        m1 = jnp.max(lg, axis=1, keepdims=True)
        i1 = jnp.min(jnp.where(lg == m1, lane, float(LANES)), axis=1, keepdims=True)
        lg2 = jnp.where(lane == i1, -jnp.inf, lg)
        m2 = jnp.max(lg2, axis=1, keepdims=True)
        i2 = jnp.min(jnp.where(lg2 == m2, lane, float(LANES)), axis=1, keepdims=True)
        e2 = jnp.exp(m2 - m1)
        den = 1.0 + e2
        oh1 = jnp.where(lane == i1, 1.0, 0.0)
        oh2 = jnp.where(lane == i2, 1.0, 0.0)
        both = oh1 + oh2
        before = _dot(tri_ref[...], both.astype(BF16)) + run_ref[...]
        r1 = jnp.sum(before * oh1, axis=1, keepdims=True)
        r2 = jnp.sum(before * oh2, axis=1, keepdims=True)
        run_ref[...] += jnp.sum(both, axis=0, keepdims=True)
        counts_ref[...] = run_ref[...]
        gates_ref[...] = (jnp.where(lane == 0.0, i1, 0.0) + jnp.where(lane == 1.0, i2, 0.0)
                          + jnp.where(lane == 2.0, 1.0 / den, 0.0) + jnp.where(lane == 3.0, e2 / den, 0.0)
                          + jnp.where(lane == 4.0, r1, 0.0) + jnp.where(lane == 5.0, r2, 0.0))
    o_ref[...] = xn
    ob_ref[...] = xn.astype(BF16)


def _outproj_ln(ins, ws, x, ln_g, ln_b, router=None, tm=512):
    t, d = x.shape
    n_in = len(ins)
    row = lambda width: pl.BlockSpec((tm, width), lambda i: (i, 0))
    full = lambda a: pl.BlockSpec(a.shape, lambda i: (0, 0))
    vec = lambda v: v.reshape(1, d).astype(F32)
    args = list(ins) + list(ws) + [x, vec(ln_g), vec(ln_b)]
    in_specs = [row(a.shape[1]) for a in ins] + [full(w) for w in ws] + [row(d), full(vec(ln_g)), full(vec(ln_b))]
    out_shape = [jax.ShapeDtypeStruct((t, d), F32), jax.ShapeDtypeStruct((t, d), BF16)]
    out_specs = [row(d), row(d)]
    if router is not None:
        router_pad = jnp.zeros((d, LANES), BF16).at[:, :N_EXPERTS].set(router.astype(BF16))
        args.append(router_pad)
        in_specs.append(full(router_pad))
        out_shape += [jax.ShapeDtypeStruct((t, LANES), F32), jax.ShapeDtypeStruct((1, LANES), F32)]
        out_specs += [row(LANES), pl.BlockSpec((1, LANES), lambda i: (0, 0))]
        scratch = [pltpu.VMEM((1, LANES), F32), pltpu.VMEM((tm, tm), BF16)]
    else:
        scratch = []
    semantics = "arbitrary" if router is not None else "parallel"
    return pl.pallas_call(
        functools.partial(_outproj_kernel, n_in=n_in, with_router=router is not None),
        grid=(t // tm,),
        in_specs=in_specs,
        out_specs=out_specs,
        out_shape=out_shape,
        scratch_shapes=scratch,
        compiler_params=pltpu.CompilerParams(dimension_semantics=(semantics,)),
        name="outproj_ln",
    )(*args)


def _ffn_kernel(xb_ref, x_ref, wg_ref, wu_ref, wd_ref, g_ref, b_ref, o_ref, ob_ref, acc_ref):
    f = pl.program_id(1)

    @pl.when(f == 0)
    def _():
        acc_ref[...] = jnp.zeros_like(acc_ref)

    xb = xb_ref[...]
    hg = _dot(xb, wg_ref[...])
    hu = _dot(xb, wu_ref[...])
    act = (hg * _sigmoid(hg) * hu).astype(BF16)
    acc_ref[...] += _dot(act, wd_ref[...])

    @pl.when(f == pl.num_programs(1) - 1)
    def _():
        xn = _layer_norm(DEEPNORM_ALPHA * x_ref[...] + acc_ref[...], g_ref[...], b_ref[...])
        o_ref[...] = xn
        ob_ref[...] = xn.astype(BF16)


def _ffn_ln(xb, x, wg, wu, wd, ln_g, ln_b, tm=512, tf=1408):
    t, d = x.shape
    ff = wg.shape[1]
    vec = lambda v: v.reshape(1, d).astype(F32)
    return pl.pallas_call(
        _ffn_kernel,
        grid=(t // tm, ff // tf),
        in_specs=[pl.BlockSpec((tm, d), lambda i, f: (i, 0)),
                  pl.BlockSpec((tm, d), lambda i, f: (i, 0)),
                  pl.BlockSpec((d, tf), lambda i, f: (0, f)),
                  pl.BlockSpec((d, tf), lambda i, f: (0, f)),
                  pl.BlockSpec((tf, d), lambda i, f: (f, 0)),
                  pl.BlockSpec((1, d), lambda i, f: (0, 0)),
                  pl.BlockSpec((1, d), lambda i, f: (0, 0))],
        out_specs=[pl.BlockSpec((tm, d), lambda i, f: (i, 0)),
                   pl.BlockSpec((tm, d), lambda i, f: (i, 0))],
        out_shape=[jax.ShapeDtypeStruct((t, d), F32), jax.ShapeDtypeStruct((t, d), BF16)],
        scratch_shapes=[pltpu.VMEM((tm, d), F32)],
        compiler_params=pltpu.CompilerParams(dimension_semantics=("parallel", "arbitrary")),
        name="ffn_ln",
    )(xb, x, wg, wu, wd, vec(ln_g), vec(ln_b))


_MOE_TM = 1024


def _route(route, counts, n_tiles):
    expert = route[:, 0:2].astype(jnp.int32)
    rank = route[:, 4:6].astype(jnp.int32)
    counts = counts[0, :N_EXPERTS].astype(jnp.int32)
    padded = (counts + _MOE_TM - 1) // _MOE_TM * _MOE_TM
    ends = jnp.cumsum(padded)
    starts = ends - padded
    onehot = expert[:, :, None] == jnp.arange(N_EXPERTS, dtype=jnp.int32)
    pos = (rank + jnp.sum(jnp.where(onehot, starts, 0), axis=-1)).reshape(-1)
    tile_start = jnp.arange(n_tiles, dtype=jnp.int32) * _MOE_TM
    tile_expert = jnp.minimum(jnp.sum((tile_start[:, None] >= ends[None, :]).astype(jnp.int32), axis=1),
                              N_EXPERTS - 1)
    return pos.astype(jnp.int32), tile_expert.astype(jnp.int32)


def _row_scatter_kernel(pos_ref, x_ref, init_ref, o_ref, sem, *, tm):
    del init_ref

    def body(j, carry):
        for choice in range(2):
            p = pos_ref[0, 0, 2 * j + choice]
            pltpu.make_async_copy(x_ref.at[pl.ds(j, 1)], o_ref.at[pl.ds(p, 1)], sem).start(priority=choice)
        return carry

    lax.fori_loop(0, tm, body, 0, unroll=8)
    for _ in range(2):
        pltpu.make_async_copy(x_ref, o_ref.at[pl.ds(0, tm)], sem).wait()


def _row_scatter(x, pos, n_rows, tm=512):
    t, d = x.shape
    return pl.pallas_call(
        functools.partial(_row_scatter_kernel, tm=tm),
        grid=(t // tm,),
        in_specs=[pl.BlockSpec((1, 1, 2 * tm), lambda i: (i, 0, 0), memory_space=pltpu.SMEM),
                  pl.BlockSpec((tm, d), lambda i: (i, 0)),
                  pl.BlockSpec(memory_space=pl.ANY)],
        out_specs=pl.BlockSpec(memory_space=pl.ANY),
        out_shape=jax.ShapeDtypeStruct((n_rows, d), x.dtype),
        scratch_shapes=[pltpu.SemaphoreType.DMA(())],
        input_output_aliases={2: 0},
        compiler_params=pltpu.CompilerParams(dimension_semantics=("arbitrary",)),
        name="moe_row_scatter",
    )(pos.reshape(t // tm, 1, 2 * tm), x, jnp.zeros((n_rows, d), x.dtype))


def _expert_ffn_kernel(te_ref, x_ref, wg_ref, wu_ref, wd_ref, o_ref, acc_ref):
    del te_ref
    f = pl.program_id(1)

    @pl.when(f == 0)
    def _():
        acc_ref[...] = jnp.zeros_like(acc_ref)

    xb = x_ref[...].astype(BF16)
    hg = _dot(xb, wg_ref[0].astype(BF16))
    hu = _dot(xb, wu_ref[0].astype(BF16))
    act = (hg * _sigmoid(hg) * hu).astype(BF16)
    acc_ref[...] += _dot(act, wd_ref[0].astype(BF16))

    @pl.when(f == pl.num_programs(1) - 1)
    def _():
        o_ref[...] = acc_ref[...]


def _expert_ffn(xs, tile_expert, wg, wu, wd, tf=512):
    n_rows, d = xs.shape
    ff = wg.shape[2]
    grid_spec = pltpu.PrefetchScalarGridSpec(
        num_scalar_prefetch=1,
        grid=(n_rows // _MOE_TM, ff // tf),
        in_specs=[pl.BlockSpec((_MOE_TM, d), lambda i, f, te: (i, 0)),
                  pl.BlockSpec((1, d, tf), lambda i, f, te: (te[i], 0, f)),
                  pl.BlockSpec((1, d, tf), lambda i, f, te: (te[i], 0, f)),
                  pl.BlockSpec((1, tf, d), lambda i, f, te: (te[i], f, 0))],
        out_specs=pl.BlockSpec((_MOE_TM, d), lambda i, f, te: (i, 0)),
        scratch_shapes=[pltpu.VMEM((_MOE_TM, d), F32)])
    return pl.pallas_call(
        _expert_ffn_kernel,
        grid_spec=grid_spec,
        out_shape=jax.ShapeDtypeStruct((n_rows, d), F32),
        compiler_params=pltpu.CompilerParams(dimension_semantics=("parallel", "arbitrary")),
        name="moe_expert_ffn",
    )(tile_expert, xs, wg, wu, wd)


def _combine_kernel(pos_ref, route_ref, x_ref, g_ref, b_ref, y_ref, o_ref, ybuf_ref, sem, *, tm):
    def body(j, carry):
        for choice in range(2):
            p = pos_ref[0, 0, 2 * j + choice]
            pltpu.make_async_copy(y_ref.at[pl.ds(p, 1)], ybuf_ref.at[choice, pl.ds(j, 1)],
                                  sem).start(priority=choice)
        return carry

    lax.fori_loop(0, tm, body, 0, unroll=8)
    for choice in range(2):
        pltpu.make_async_copy(y_ref.at[pl.ds(0, tm)], ybuf_ref.at[choice], sem).wait()
    route = route_ref[...]
    y = route[:, 2:3] * ybuf_ref[0] + route[:, 3:4] * ybuf_ref[1]
    o_ref[...] = _layer_norm(DEEPNORM_ALPHA * x_ref[...] + y, g_ref[...], b_ref[...])


def _combine_ln(ys, pos, route, x, ln_g, ln_b, tm=512):
    t, d = x.shape
    vec = lambda v: v.reshape(1, d).astype(F32)
    return pl.pallas_call(
        functools.partial(_combine_kernel, tm=tm),
        grid=(t // tm,),
        in_specs=[pl.BlockSpec((1, 1, 2 * tm), lambda i: (i, 0, 0), memory_space=pltpu.SMEM),
                  pl.BlockSpec((tm, LANES), lambda i: (i, 0)),
                  pl.BlockSpec((tm, d), lambda i: (i, 0)),
                  pl.BlockSpec((1, d), lambda i: (0, 0)),
                  pl.BlockSpec((1, d), lambda i: (0, 0)),
                  pl.BlockSpec(memory_space=pl.ANY)],
        out_specs=pl.BlockSpec((tm, d), lambda i: (i, 0)),
        out_shape=jax.ShapeDtypeStruct((t, d), F32),
        scratch_shapes=[pltpu.VMEM((2, tm, d), F32), pltpu.SemaphoreType.DMA(())],
        compiler_params=pltpu.CompilerParams(dimension_semantics=("arbitrary",)),
        name="moe_combine_ln",
    )(pos.reshape(t // tm, 1, 2 * tm), route, x, vec(ln_g), vec(ln_b), ys)


def _moe_ln(x, route, counts, wg, wu, wd, ln_g, ln_b):
    t, _ = x.shape
    n_tiles = 2 * t // _MOE_TM + N_EXPERTS
    pos, tile_expert = _route(route, counts, n_tiles)
    xs = _row_scatter(x, pos, n_tiles * _MOE_TM)
    ys = _expert_ffn(xs, tile_expert, wg, wu, wd)
    return _combine_ln(ys, pos, route, x, ln_g, ln_b)


def _even_layer(x, xb, batch, seq, dsa_bias, w_in, conv_w, conv_b, conv_ln_g, conv_ln_b, w_out,
                ln1_g, ln1_b, ffn_wg, ffn_wu, ffn_wd, ln2_g, ln2_b):
    w_in_p = jnp.zeros((D_MODEL, EVEN_IN_PAD), BF16).at[:, :w_in.shape[1]].set(w_in.astype(BF16))
    h = _matmul(x if xb is None else xb, w_in_p, 2048, 640, BF16).reshape(batch, seq, EVEN_IN_PAD)
    a_out = _conformer_conv(h, conv_w, conv_b, conv_ln_g, conv_ln_b)
    att = _dsa_attention(h, dsa_bias)
    w_out_b = w_out.astype(BF16)
    x1, x1b = _outproj_ln([a_out.reshape(-1, CONV_CH), att.reshape(-1, DSA_HEADS * DSA_HEAD_DIM)],
                          [w_out_b[:CONV_CH], w_out_b[CONV_CH:]], x, ln1_g, ln1_b)
    return _ffn_ln(x1b, x1, ffn_wg.astype(BF16), ffn_wu.astype(BF16), ffn_wd.astype(BF16), ln2_g, ln2_b)


def _odd_layer(x, xb, batch, seq, dil_bias, w_in, w_out, ln1_g, ln1_b, router, moe_wg, moe_wu, moe_wd,
               ln2_g, ln2_b):
    gw = 3 * DIL_HEADS * DIL_HEAD_DIM
    w_in_b = w_in.astype(BF16)
    hs = [_matmul_residue_major(x if xb is None else xb, w_in_b[:, g * gw:(g + 1) * gw], batch, seq, dil, 2048, 768)
          for g, (_, dil) in enumerate(DIL_GROUPS)]
    o = _dilated_attention(hs, dil_bias).reshape(-1, DIL_HEADS * DIL_HEAD_DIM)
    x1, _, route, counts = _outproj_ln([o], [w_out.astype(BF16)], x, ln1_g, ln1_b, router=router)
    out = _moe_ln(x1, route, counts, moe_wg, moe_wu, moe_wd, ln2_g, ln2_b)
    return out, out.astype(BF16)


def kernel(x, rel_bias, even_w_in, even_conv_w, even_conv_b, even_conv_ln_g, even_conv_ln_b, even_w_out, even_ln1_g, even_ln1_b, even_ffn_wg, even_ffn_wu, even_ffn_wd, even_ln2_g, even_ln2_b, odd_w_in, odd_w_out, odd_ln1_g, odd_ln1_b, odd_router, odd_moe_wg, odd_moe_wu, odd_moe_wd, odd_ln2_g, odd_ln2_b):
    batch, seq, d = x.shape
    assert d == D_MODEL and seq % DIL_CHUNK == 0
    depth = even_w_in.shape[0] + odd_w_in.shape[0]
    assert depth == DEPTH
    dsa_bias = _dsa_bias_tiles(rel_bias, seq)
    dil_bias = _dil_bias_tiles(rel_bias)
    xf = x.reshape(batch * seq, d)
    xb = None
    for layer in range(depth):
        i = layer // 2
        if layer % 2 == 0:
            xf, xb = _even_layer(xf, xb, batch, seq, dsa_bias, even_w_in[i], even_conv_w[i], even_conv_b[i],
                                 even_conv_ln_g[i], even_conv_ln_b[i], even_w_out[i], even_ln1_g[i],
                                 even_ln1_b[i], even_ffn_wg[i], even_ffn_wu[i], even_ffn_wd[i],
                                 even_ln2_g[i], even_ln2_b[i])
        else:
            xf, xb = _odd_layer(xf, xb, batch, seq, dil_bias, odd_w_in[i], odd_w_out[i], odd_ln1_g[i],
                                odd_ln1_b[i], odd_router[i], odd_moe_wg[i], odd_moe_wu[i], odd_moe_wd[i],
                                odd_ln2_g[i], odd_ln2_b[i])
    return xf.reshape(batch, seq, d)
```

```python
import functools
import math

import jax
import jax.numpy as jnp
from jax import lax
from jax.experimental import pallas as pl
from jax.experimental.pallas import tpu as pltpu

BF16 = jnp.bfloat16
F32 = jnp.float32

D_MODEL = 1024
CONV_CH = 512
CONV_WIDTH = 31
DSA_HEADS = 8
DSA_HEAD_DIM = 64
IDX_HEADS = 8
IDX_DIM = 64
DSA_TOPK_MAX = 256
Q_BLOCK = 128
DIL_GROUPS = ((128, 1), (512, 4), (2048, 16))
DIL_HEADS = 8
DIL_HEAD_DIM = 128
BAND_BLOCK = 128
NUM_BUCKETS = 32
MAX_DISTANCE = 2048
N_EXPERTS = 8
LN_EPS = 1e-5
NEG = -1e30
DEPTH = 2
DEEPNORM_ALPHA = (2 * DEPTH) ** 0.25

LANES = 128
EVEN_IN_PAD = 3200
INT_MIN = -(2 ** 31)
DIL_CHUNK = BAND_BLOCK * max(d for _, d in DIL_GROUPS)

_PARALLEL2 = pltpu.CompilerParams(dimension_semantics=("parallel", "parallel"))


def _layer_norm(y, g, b):
    mu = jnp.mean(y, axis=-1, keepdims=True)
    yc = y - mu
    var = jnp.mean(yc * yc, axis=-1, keepdims=True)
    return yc * lax.rsqrt(var + LN_EPS) * g + b


def _sigmoid(x):
    return 1.0 / (1.0 + jnp.exp(-x))


def _dot_nt(a, b):
    return lax.dot_general(a, b, (((1,), (1,)), ((), ())), preferred_element_type=F32)


def _dot(a, b):
    return jnp.dot(a, b, preferred_element_type=F32)


def _mm_kernel(x_ref, w_ref, o_ref):
    o_ref[...] = _dot(x_ref[...].astype(BF16), w_ref[...]).astype(o_ref.dtype)


def _matmul(x, w, tm, tn, out_dtype):
    t, k = x.shape
    n = w.shape[1]
    return pl.pallas_call(
        _mm_kernel,
        grid=(t // tm, n // tn),
        in_specs=[pl.BlockSpec((tm, k), lambda i, j: (i, 0)),
                  pl.BlockSpec((k, tn), lambda i, j: (0, j))],
        out_specs=pl.BlockSpec((tm, tn), lambda i, j: (i, j)),
        out_shape=jax.ShapeDtypeStruct((t, n), out_dtype),
        compiler_params=_PARALLEL2,
        name="matmul",
    )(x, w)


_MAX_ROW_STRIDE = 4


def _mm_perm_kernel(x_ref, w_ref, o_ref, acc_ref, tmp_ref, *, dil, tm, tn):
    acc = _dot(x_ref[...].astype(BF16), w_ref[...])
    for c in range(tn // LANES):
        acc_ref[c] = acc[:, c * LANES:(c + 1) * LANES]
    if dil <= _MAX_ROW_STRIDE:
        rows = tm // dil
        for r in range(dil):
            for c in range(tn // LANES):
                o_ref[0, r, :, c * LANES:(c + 1) * LANES] = (
                    acc_ref[c, pl.ds(r, rows, stride=dil), :].astype(o_ref.dtype))
        return
    s1 = _MAX_ROW_STRIDE
    s2 = dil // s1
    assert s2 <= _MAX_ROW_STRIDE
    rows1 = tm // s1
    rows = tm // dil
    for c in range(tn // LANES):
        for r1 in range(s1):
            tmp_ref[c, r1 * rows1:(r1 + 1) * rows1, :] = acc_ref[c, pl.ds(r1, rows1, stride=s1), :]
    for r1 in range(s1):
        for r2 in range(s2):
            for c in range(tn // LANES):
                o_ref[0, s1 * r2 + r1, :, c * LANES:(c + 1) * LANES] = (
                    tmp_ref[c, pl.ds(r1 * rows1 + r2, rows, stride=s2), :].astype(o_ref.dtype))


def _matmul_residue_major(x, w, batch, seq, dil, tm, tn):
    t, k = x.shape
    n = w.shape[1]
    if dil == 1:
        return _matmul(x, w, tm, tn, BF16).reshape(batch, seq, n)
    tiles_per_seq = seq // tm
    out = pl.pallas_call(
        functools.partial(_mm_perm_kernel, dil=dil, tm=tm, tn=tn),
        grid=(t // tm, n // tn),
        in_specs=[pl.BlockSpec((tm, k), lambda i, j: (i, 0)),
                  pl.BlockSpec((k, tn), lambda i, j: (0, j))],
        out_specs=pl.BlockSpec((1, dil, tm // dil, tn),
                               lambda i, j: (i // tiles_per_seq, 0, i % tiles_per_seq, j)),
        out_shape=jax.ShapeDtypeStruct((batch, dil, seq // dil, n), BF16),
        scratch_shapes=[pltpu.VMEM((tn // LANES, tm, LANES), F32), pltpu.VMEM((tn // LANES, tm, LANES), F32)],
        compiler_params=_PARALLEL2,
        name="matmul_residue_major",
    )(x, w)
    return out.reshape(batch, seq, n)


_CONV_HALO = 32
_CONV_ROWS = 64
_SUBLANES = 8


def _conv_kernel(val_ref, gate_ref, w_ref, cb_ref, g_ref, b_ref, o_ref, ext_ref, *, ts):
    s = pl.program_id(1)

    @pl.when(s == 0)
    def _():
        ext_ref[0, 0:_CONV_HALO, :] = jnp.zeros((_CONV_HALO, CONV_CH), F32)

    val = val_ref[0].astype(F32)
    gate = gate_ref[0].astype(F32)
    ext_ref[0, _CONV_HALO:_CONV_HALO + ts, :] = val * _sigmoid(gate)
    span = ts + _CONV_HALO - _SUBLANES
    for k in range(1, _SUBLANES):
        ext_ref[k, 0:span, :] = ext_ref[0, k:k + span, :]
    first_tap = _CONV_HALO - (CONV_WIDTH - 1)
    for rc in range(ts // _CONV_ROWS):
        r0 = rc * _CONV_ROWS
        acc = jnp.zeros((_CONV_ROWS, CONV_CH), F32)
        for j in range(CONV_WIDTH):
            shift = (first_tap + j) % _SUBLANES
            lo = r0 + first_tap + j - shift
            acc = acc + ext_ref[shift, lo:lo + _CONV_ROWS, :] * w_ref[j:j + 1, :]
        y = _layer_norm(acc + cb_ref[...], g_ref[...], b_ref[...])
        o_ref[0, r0:r0 + _CONV_ROWS, :] = (y * _sigmoid(y)).astype(o_ref.dtype)
    ext_ref[0, 0:_CONV_HALO, :] = ext_ref[0, ts:ts + _CONV_HALO, :]


def _conformer_conv(h, conv_w, conv_b, ln_g, ln_b, ts=512):
    b, s, _ = h.shape
    w_pad = jnp.zeros((32, CONV_CH), F32).at[:CONV_WIDTH].set(conv_w)
    vec = lambda v: v.reshape(1, CONV_CH).astype(F32)
    full = lambda shape: pl.BlockSpec(shape, lambda bi, si: (0, 0))
    return pl.pallas_call(
        functools.partial(_conv_kernel, ts=ts),
        grid=(b, s // ts),
        in_specs=[pl.BlockSpec((1, ts, CONV_CH), lambda bi, si: (bi, si, 0)),
                  pl.BlockSpec((1, ts, CONV_CH), lambda bi, si: (bi, si, 1)),
                  full((32, CONV_CH)), full((1, CONV_CH)), full((1, CONV_CH)), full((1, CONV_CH))],
        out_specs=pl.BlockSpec((1, ts, CONV_CH), lambda bi, si: (bi, si, 0)),
        out_shape=jax.ShapeDtypeStruct((b, s, CONV_CH), BF16),
        scratch_shapes=[pltpu.VMEM((_SUBLANES, ts + _CONV_HALO, CONV_CH), F32)],
        compiler_params=pltpu.CompilerParams(dimension_semantics=("parallel", "arbitrary")),
        name="conformer_conv",
    )(h, h, w_pad, vec(conv_b), vec(ln_g), vec(ln_b))


def _rel_bucket(dist):
    max_exact = NUM_BUCKETS // 2
    n = dist.astype(jnp.int32)
    nf = jnp.maximum(n, 1).astype(F32)
    large = max_exact + (jnp.log(nf / max_exact) / math.log(MAX_DISTANCE / max_exact)
                         * (NUM_BUCKETS - max_exact)).astype(jnp.int32)
    large = jnp.minimum(large, NUM_BUCKETS - 1)
    return jnp.where(n < max_exact, n, large)


def _bias_by_distance(rel_bias, n):
    return rel_bias[_rel_bucket(jnp.arange(n, dtype=jnp.int32))].astype(F32).T


def _toeplitz_kernel(win_ref, o_ref):
    n_win, span = win_ref.shape
    for n in range(n_win):
        y = jnp.broadcast_to(win_ref[n:n + 1, :], (LANES, span))
        y = pltpu.roll(y, 0, 1, stride=1, stride_axis=0)
        o_ref[n] = y[:, LANES:]


def _toeplitz_tiles(windows, per_step=8):
    n, span = windows.shape
    width = span - LANES
    return pl.pallas_call(
        _toeplitz_kernel,
        grid=(n // per_step,),
        in_specs=[pl.BlockSpec((per_step, span), lambda i: (i, 0))],
        out_specs=pl.BlockSpec((per_step, LANES, width), lambda i: (i, 0, 0)),
        out_shape=jax.ShapeDtypeStruct((n, LANES, width), F32),
        compiler_params=pltpu.CompilerParams(dimension_semantics=("parallel",)),
        name="bias_toeplitz",
    )(windows)


def _dsa_bias_tiles(rel_bias, seq):
    nblk = seq // Q_BLOCK
    heads = rel_bias.shape[1]
    f = _bias_by_distance(rel_bias, seq)
    u = jnp.concatenate([jnp.broadcast_to(f[:, :1], (heads, 2 * Q_BLOCK)), f], axis=1)
    blocks = u.reshape(heads, nblk + 2, Q_BLOCK)
    windows = jnp.concatenate([blocks[:, :-1], blocks[:, 1:]], axis=-1)
    windows = windows.transpose(1, 0, 2).reshape((nblk + 1) * heads, 2 * Q_BLOCK)
    return _toeplitz_tiles(windows).reshape(nblk + 1, heads, Q_BLOCK, Q_BLOCK)


def _dil_bias_tiles(rel_bias):
    max_dil = max(d for _, d in DIL_GROUPS)
    heads = rel_bias.shape[1]
    span = 2 * BAND_BLOCK
    f = _bias_by_distance(rel_bias, span * max_dil + 1)
    windows = []
    for _, dil in DIL_GROUPS:
        fd = f[:, 0:span * dil + 1:dil]
        tail = jnp.broadcast_to(fd[:, :1], (heads, BAND_BLOCK - 1))
        windows.append(jnp.concatenate([fd[:, ::-1], tail], axis=1))
    tiles = _toeplitz_tiles(jnp.concatenate(windows, axis=0))
    return tiles.reshape(len(DIL_GROUPS), heads, BAND_BLOCK, span)


_KC = 2 * Q_BLOCK
_VT_ROWS = DSA_HEAD_DIM + 16


def _sortable_key(score):
    score = jnp.where(score == 0.0, 0.0, score)
    bits = lax.bitcast_convert_type(score, jnp.int32)
    return jnp.where(bits < 0, bits ^ jnp.int32(0x7FFFFFFF), bits)


def _rows_reduce(x, op):
    rows = x.shape[0]
    return op(op(x.reshape(rows // 8, 8, LANES), axis=0), axis=0, keepdims=True)


def _keys_reduce(x, op):
    h, keys, _ = x.shape
    return op(op(x.reshape(h, keys // 8, 8, LANES), axis=1), axis=1, keepdims=True)


def _dsa_t_kernel(q_ref, qi_ref, wq_ref, k_ref, vt_ref, ki_ref, bias_hbm, o_ref,
                  bias_ref, keys_ref, qm_ref, acc_ref, dots_a_ref, dots_b_ref, sem, *, topk):
    b = pl.program_id(0)
    i = pl.program_id(1)

    @pl.when((b == 0) & (i == 0))
    def _():
        cp = pltpu.make_async_copy(bias_hbm, bias_ref, sem)
        cp.start()
        cp.wait()

    nch = (i + 2) // 2
    key_off = lax.broadcasted_iota(jnp.int32, (_KC, LANES), 0)
    q_pos = i * Q_BLOCK + lax.broadcasted_iota(jnp.int32, (_KC, LANES), 1)
    lane = lax.broadcasted_iota(jnp.int32, (Q_BLOCK, LANES), 1)
    low_half = lane < DSA_HEAD_DIM

    w_t = (wq_ref[0].astype(F32) * (IDX_HEADS ** -0.5 * IDX_DIM ** -0.5)).T
    qi = qi_ref[0].astype(F32)
    qi_heads = []
    for p in range(IDX_HEADS // 2):
        pair = qi[:, p * LANES:(p + 1) * LANES]
        qi_heads.append(jnp.where(low_half, pair, 0.0))
        qi_heads.append(jnp.where(low_half, pltpu.roll(pair, IDX_DIM, 1), 0.0))
    qi_stack = jnp.concatenate(qi_heads, axis=0).astype(BF16)
    q_scaled = (q_ref[0].astype(F32) * (DSA_HEAD_DIM ** -0.5)).astype(BF16)
    for p in range(DSA_HEADS // 2):
        qp = q_scaled[:, p * LANES:(p + 1) * LANES]
        qm_ref[2 * p] = jnp.where(low_half, qp, jnp.zeros_like(qp))
        qm_ref[2 * p + 1] = jnp.where(low_half, jnp.zeros_like(qp), qp)

    def index_dots(c, dst_ref):
        c = jnp.minimum(c, nch - 1)
        k0 = pl.multiple_of(c * _KC, _KC)
        ki = ki_ref[0, pl.ds(k0, _KC), :]
        dst_ref[...] = _dot_nt(ki, qi_stack)

    def consume_dots(c, src_ref):
        k0 = c * _KC
        score = jnp.zeros((_KC, LANES), F32)
        for h in range(IDX_HEADS):
            w_row = w_t[IDX_DIM + h:IDX_DIM + h + 1, :]
            score = score + w_row * jnp.maximum(src_ref[:, h * Q_BLOCK:(h + 1) * Q_BLOCK], 0.0)
        keys_ref[c] = jnp.where(k0 + key_off <= q_pos, _sortable_key(score), jnp.int32(INT_MIN))

    index_dots(0, dots_a_ref)

    def score_body(c2, carry):
        index_dots(2 * c2 + 1, dots_b_ref)
        consume_dots(2 * c2, dots_a_ref)
        index_dots(2 * c2 + 2, dots_a_ref)
        consume_dots(2 * c2 + 1, dots_b_ref)
        return carry

    lax.fori_loop(0, (nch + 1) // 2, score_body, 0)

    def count_ge(cand):
        cb = jnp.broadcast_to(cand, (8, LANES))
        chains = 4

        def body(c2, acc):
            for half in range(2):
                kk = keys_ref[2 * c2 + half].reshape(_KC // (8 * chains), chains, 8, LANES)
                acc = acc + jnp.sum(jnp.where(kk >= cb, 1.0, 0.0), axis=0)
            return acc

        acc = lax.fori_loop(0, (nch + 1) // 2, body, jnp.zeros((chains, 8, LANES), F32))
        return jnp.sum(jnp.sum(acc, axis=0), axis=0, keepdims=True)

    kf = float(topk)
    zero = jnp.zeros((1, LANES), jnp.int32)
    thr0 = jnp.where(count_ge(zero) >= kf, zero, jnp.full((1, LANES), INT_MIN, jnp.int32))

    def bit_body(t, thr):
        cand = thr | jnp.left_shift(jnp.int32(1), 30 - t)
        return jnp.where(count_ge(cand) >= kf, cand, thr)

    thr = lax.fori_loop(0, 31, bit_body, thr0)
    n_gt = count_ge(jnp.where(thr == jnp.int32(2 ** 31 - 1), thr, thr + 1))
    need = kf - n_gt
    tri = (lax.broadcasted_iota(jnp.int32, (_KC, _KC), 1)
           <= lax.broadcasted_iota(jnp.int32, (_KC, _KC), 0)).astype(BF16)

    acc_ref[...] = jnp.zeros_like(acc_ref)

    def attn_body(c, carry):
        ms, tie_carry = carry
        k0 = pl.multiple_of(c * _KC, _KC)
        kk = keys_ref[c]
        causal = kk > jnp.int32(INT_MIN)
        eq = (kk == thr) & causal
        eq_f = jnp.where(eq, 1.0, 0.0)
        rank = _dot(tri, eq_f.astype(BF16)) + tie_carry
        sel = ((kk > thr) & causal) | (eq & (rank <= need))
        tie_carry = tie_carry + _rows_reduce(eq_f, jnp.sum)
        d = i - 2 * c
        kc = k_ref[0, pl.ds(k0, _KC), :]
        k_heads = jnp.stack([kc[:, (h // 2) * LANES:(h // 2 + 1) * LANES] for h in range(DSA_HEADS)])
        s = jnp.einsum("hkd,hqd->hkq", k_heads, qm_ref[...], preferred_element_type=F32)
        bias = jnp.concatenate([bias_ref[d + 1], bias_ref[d]], axis=1)
        s = jnp.where(sel, s + bias, NEG)
        m_new = jnp.maximum(ms, _keys_reduce(s, jnp.max))
        alpha = jnp.exp(ms - m_new)
        pr = jnp.exp((s - m_new).astype(BF16))
        pv = jnp.einsum("hdk,hkq->hdq", vt_ref[0, c], pr, preferred_element_type=F32)
        acc_ref[...] = alpha * acc_ref[...] + pv
        return m_new, tie_carry

    init = (jnp.full((DSA_HEADS, 1, LANES), NEG, F32), jnp.zeros((1, LANES), F32))
    lax.fori_loop(0, nch, attn_body, init)
    acc = acc_ref[...]
    out_t = acc[:, :DSA_HEAD_DIM] / acc[:, DSA_HEAD_DIM:DSA_HEAD_DIM + 1]
    o_ref[0] = out_t.reshape(DSA_HEADS * DSA_HEAD_DIM, Q_BLOCK).T.astype(o_ref.dtype)


def _dsa_attention(h, bias_tiles):
    b, s, _ = h.shape
    width = DSA_HEADS * DSA_HEAD_DIM
    nblk = s // Q_BLOCK
    topk = min(DSA_TOPK_MAX, s // 4)
    kw_block = (CONV_CH * 2 + 4 * width) // LANES
    v_cols = 2 * CONV_CH + 2 * width
    vt = h[:, :, v_cols:v_cols + width].reshape(b, s // _KC, _KC, DSA_HEADS, DSA_HEAD_DIM).transpose(0, 1, 3, 4, 2)
    vt = jnp.concatenate([vt, jnp.ones((b, s // _KC, DSA_HEADS, _VT_ROWS - DSA_HEAD_DIM, _KC), BF16)], axis=3)
    return pl.pallas_call(
        functools.partial(_dsa_t_kernel, topk=topk),
        grid=(b, nblk),
        in_specs=[pl.BlockSpec((1, Q_BLOCK, width), lambda bi, qi: (bi, qi, 2)),
                  pl.BlockSpec((1, Q_BLOCK, width), lambda bi, qi: (bi, qi, 5)),
                  pl.BlockSpec((1, Q_BLOCK, LANES), lambda bi, qi: (bi, qi, kw_block)),
                  pl.BlockSpec((1, s, width), lambda bi, qi: (bi, 0, 3)),
                  pl.BlockSpec((1, s // _KC, DSA_HEADS, _VT_ROWS, _KC), lambda bi, qi: (bi, 0, 0, 0, 0)),
                  pl.BlockSpec((1, s, LANES), lambda bi, qi: (bi, 0, kw_block)),
                  pl.BlockSpec(memory_space=pl.ANY)],
        out_specs=pl.BlockSpec((1, Q_BLOCK, width), lambda bi, qi: (bi, qi, 0)),
        out_shape=jax.ShapeDtypeStruct((b, s, width), BF16),
        scratch_shapes=[pltpu.VMEM((nblk + 1, DSA_HEADS, Q_BLOCK, Q_BLOCK), F32),
                        pltpu.VMEM((s // _KC, _KC, Q_BLOCK), jnp.int32),
                        pltpu.VMEM((DSA_HEADS, Q_BLOCK, LANES), BF16),
                        pltpu.VMEM((DSA_HEADS, _VT_ROWS, Q_BLOCK), F32),
                        pltpu.VMEM((_KC, IDX_HEADS * Q_BLOCK), F32),
                        pltpu.VMEM((_KC, IDX_HEADS * Q_BLOCK), F32),
                        pltpu.SemaphoreType.DMA(())],
        compiler_params=pltpu.CompilerParams(dimension_semantics=("arbitrary", "arbitrary")),
        name="dsa_attention",
    )(h, h, h, h, vt, h, bias_tiles)


_DIL_WAVE = 8


def _dil_kernel(*refs, seq):
    n_g = len(DIL_GROUPS)
    qkv = refs[:3 * n_g]
    bias_ref, o_ref, og_ref, lg_ref = refs[3 * n_g:]
    c = pl.program_id(2)
    ii = lax.broadcasted_iota(jnp.int32, (BAND_BLOCK, BAND_BLOCK), 0)
    jj = lax.broadcasted_iota(jnp.int32, (BAND_BLOCK, BAND_BLOCK), 1)
    valid_prev = jj >= ii
    valid_cur = ii >= jj
    scale = DIL_HEAD_DIM ** -0.5
    for g, (window, dil) in enumerate(DIL_GROUPS):
        assert window // dil == BAND_BLOCK
        q_ref, k_ref, v_ref = qkv[3 * g:3 * g + 3]
        sub_len = seq // dil
        blocks_per_chunk = DIL_CHUNK // (BAND_BLOCK * dil)
        bias_prev = bias_ref[g, 0, :, :BAND_BLOCK]
        bias_cur = bias_ref[g, 0, :, BAND_BLOCK:]
        tiles = [(r, nl) for r in range(dil) for nl in range(blocks_per_chunk)]
        for w0 in range(0, len(tiles), _DIL_WAVE):
            wave = tiles[w0:w0 + _DIL_WAVE]
            cur, prev, mask_prev = [], [], []
            for r, nl in wave:
                row0 = pl.multiple_of(r * sub_len + c * (DIL_CHUNK // dil) + nl * BAND_BLOCK, BAND_BLOCK)
                cur.append(pl.ds(row0, BAND_BLOCK))
                prev.append(pl.ds(pl.multiple_of(jnp.maximum(row0 - BAND_BLOCK, 0), BAND_BLOCK), BAND_BLOCK))
                mask_prev.append(valid_prev if nl > 0 else (valid_prev & (c > 0)))
            gather = lambda ref, slices: jnp.stack([ref[0, sl, :] for sl in slices])
            qb = gather(q_ref, cur)
            s_cur = jnp.einsum("wqd,wkd->wqk", qb, gather(k_ref, cur), preferred_element_type=F32)
            s_prev = jnp.einsum("wqd,wkd->wqk", qb, gather(k_ref, prev), preferred_element_type=F32)
            s_cur = jnp.where(valid_cur, s_cur * scale + bias_cur, NEG)
            s_prev = jnp.where(jnp.stack(mask_prev), s_prev * scale + bias_prev, NEG)
            m = jnp.max(jnp.maximum(s_cur, s_prev), axis=2, keepdims=True)
            p_cur = jnp.exp(s_cur - m)
            p_prev = jnp.exp(s_prev - m)
            ssum = jnp.sum(p_cur + p_prev, axis=2, keepdims=True)
            o = (jnp.einsum("wqk,wkd->wqd", p_cur.astype(BF16), gather(v_ref, cur), preferred_element_type=F32)
                 + jnp.einsum("wqk,wkd->wqd", p_prev.astype(BF16), gather(v_ref, prev),
                              preferred_element_type=F32)) / ssum
            lse = m + jnp.log(ssum)
            for w, (r, nl) in enumerate(wave):
                dst = pl.ds(r + nl * BAND_BLOCK * dil, BAND_BLOCK, stride=dil) if dil > 1 else pl.ds(
                    nl * BAND_BLOCK, BAND_BLOCK)
                og_ref[g, dst, :] = o[w]
                lg_ref[g, dst, :] = jnp.broadcast_to(lse[w], (BAND_BLOCK, LANES))
    rows = 256
    for rc in range(DIL_CHUNK // rows):
        sl = slice(rc * rows, (rc + 1) * rows)
        lses = [lg_ref[g, sl, :] for g in range(n_g)]
        mx = functools.reduce(jnp.maximum, lses)
        ws = [jnp.exp(l - mx) for l in lses]
        den = functools.reduce(lambda a, b_: a + b_, ws)
        num = functools.reduce(lambda a, b_: a + b_, [ws[g] * og_ref[g, sl, :] for g in range(n_g)])
        o_ref[0, sl, :] = (num / den).astype(o_ref.dtype)


def _dilated_attention(hs, bias_tiles):
    b, s, _ = hs[0].shape
    in_specs = []
    args = []
    for hg in hs:
        for part in range(3):
            in_specs.append(pl.BlockSpec((1, s, DIL_HEAD_DIM),
                                         lambda bi, hi, ci, part=part: (bi, 0, part * DIL_HEADS + hi)))
            args.append(hg)
    in_specs.append(pl.BlockSpec((len(DIL_GROUPS), 1, BAND_BLOCK, 2 * BAND_BLOCK),
                                 lambda bi, hi, ci: (0, hi, 0, 0)))
    return pl.pallas_call(
        functools.partial(_dil_kernel, seq=s),
        grid=(b, DIL_HEADS, s // DIL_CHUNK),
        in_specs=in_specs,
        out_specs=pl.BlockSpec((1, DIL_CHUNK, DIL_HEAD_DIM), lambda bi, hi, ci: (bi, ci, hi)),
        out_shape=jax.ShapeDtypeStruct((b, s, DIL_HEADS * DIL_HEAD_DIM), BF16),
        scratch_shapes=[pltpu.VMEM((len(DIL_GROUPS), DIL_CHUNK, LANES), F32),
                        pltpu.VMEM((len(DIL_GROUPS), DIL_CHUNK, LANES), F32)],
        compiler_params=pltpu.CompilerParams(dimension_semantics=("parallel", "parallel", "arbitrary")),
        name="dilated_attention",
    )(*args, bias_tiles)


def _outproj_kernel(*refs, n_in, with_router):
    ins = refs[:n_in]
    ws = refs[n_in:2 * n_in]
    x_ref, g_ref, b_ref = refs[2 * n_in:2 * n_in + 3]
    rest = refs[2 * n_in + 3:]
    mix = _dot(ins[0][...], ws[0][...])
    for a_ref, w_ref in zip(ins[1:], ws[1:]):
        mix = mix + _dot(a_ref[...], w_ref[...])
    xn = _layer_norm(DEEPNORM_ALPHA * x_ref[...] + mix, g_ref[...], b_ref[...])
    if not with_router:
        o_ref, ob_ref = rest
    else:
        router_ref, o_ref, ob_ref, gates_ref, counts_ref, run_ref, tri_ref = rest
        tm = xn.shape[0]

        @pl.when(pl.program_id(0) == 0)
        def _():
            run_ref[...] = jnp.zeros_like(run_ref)
            tri_ref[...] = (lax.broadcasted_iota(jnp.int32, (tm, tm), 0)
                            > lax.broadcasted_iota(jnp.int32, (tm, tm), 1)).astype(BF16)

        logits = _dot(xn.astype(BF16), router_ref[...])
        lane = lax.broadcasted_iota(jnp.int32, logits.shape, 1).astype(F32)
        lg = jnp.where(lane < N_EXPERTS, logits, -jnp.inf)
        m1 = jnp.max(lg, axis=1, keepdims=True)
        i1 = jnp.min(jnp.where(lg == m1, lane, float(LANES)), axis=1, keepdims=True)
        lg2 = jnp.where(lane == i1, -jnp.inf, lg)
        m2 = jnp.max(lg2, axis=1, keepdims=True)
        i2 = jnp.min(jnp.where(lg2 == m2, lane, float(LANES)), axis=1, keepdims=True)
        e2 = jnp.exp(m2 - m1)
        den = 1.0 + e2
        oh1 = jnp.where(lane == i1, 1.0, 0.0)
        oh2 = jnp.where(lane == i2, 1.0, 0.0)
        both = oh1 + oh2
        before = _dot(tri_ref[...], both.astype(BF16)) + run_ref[...]
        r1 = jnp.sum(before * oh1, axis=1, keepdims=True)
        r2 = jnp.sum(before * oh2, axis=1, keepdims=True)
        run_ref[...] += jnp.sum(both, axis=0, keepdims=True)
        counts_ref[...] = run_ref[...]
        gates_ref[...] = (jnp.where(lane == 0.0, i1, 0.0) + jnp.where(lane == 1.0, i2, 0.0)
                          + jnp.where(lane == 2.0, 1.0 / den, 0.0) + jnp.where(lane == 3.0, e2 / den, 0.0)
                          + jnp.where(lane == 4.0, r1, 0.0) + jnp.where(lane == 5.0, r2, 0.0))
    o_ref[...] = xn
    ob_ref[...] = xn.astype(BF16)


def _outproj_ln(ins, ws, x, ln_g, ln_b, router=None, tm=512):
    t, d = x.shape
    n_in = len(ins)
    row = lambda width: pl.BlockSpec((tm, width), lambda i: (i, 0))
    full = lambda a: pl.BlockSpec(a.shape, lambda i: (0, 0))
    vec = lambda v: v.reshape(1, d).astype(F32)
    args = list(ins) + list(ws) + [x, vec(ln_g), vec(ln_b)]
    in_specs = [row(a.shape[1]) for a in ins] + [full(w) for w in ws] + [row(d), full(vec(ln_g)), full(vec(ln_b))]
    out_shape = [jax.ShapeDtypeStruct((t, d), F32), jax.ShapeDtypeStruct((t, d), BF16)]
    out_specs = [row(d), row(d)]
    if router is not None:
        router_pad = jnp.zeros((d, LANES), BF16).at[:, :N_EXPERTS].set(router.astype(BF16))
        args.append(router_pad)
        in_specs.append(full(router_pad))
        out_shape += [jax.ShapeDtypeStruct((t, LANES), F32), jax.ShapeDtypeStruct((1, LANES), F32)]
        out_specs += [row(LANES), pl.BlockSpec((1, LANES), lambda i: (0, 0))]
        scratch = [pltpu.VMEM((1, LANES), F32), pltpu.VMEM((tm, tm), BF16)]
    else:
        scratch = []
    semantics = "arbitrary" if router is not None else "parallel"
    return pl.pallas_call(
        functools.partial(_outproj_kernel, n_in=n_in, with_router=router is not None),
        grid=(t // tm,),
        in_specs=in_specs,
        out_specs=out_specs,
        out_shape=out_shape,
        scratch_shapes=scratch,
        compiler_params=pltpu.CompilerParams(dimension_semantics=(semantics,)),
        name="outproj_ln",
    )(*args)


def _ffn_kernel(xb_ref, x_ref, wg_ref, wu_ref, wd_ref, g_ref, b_ref, o_ref, ob_ref, acc_ref):
    f = pl.program_id(1)

    @pl.when(f == 0)
    def _():
        acc_ref[...] = jnp.zeros_like(acc_ref)

    xb = xb_ref[...]
    hg = _dot(xb, wg_ref[...])
    hu = _dot(xb, wu_ref[...])
    act = (hg * _sigmoid(hg) * hu).astype(BF16)
    acc_ref[...] += _dot(act, wd_ref[...])

    @pl.when(f == pl.num_programs(1) - 1)
    def _():
        xn = _layer_norm(DEEPNORM_ALPHA * x_ref[...] + acc_ref[...], g_ref[...], b_ref[...])
        o_ref[...] = xn
        ob_ref[...] = xn.astype(BF16)


def _ffn_ln(xb, x, wg, wu, wd, ln_g, ln_b, tm=512, tf=1408):
    t, d = x.shape
    ff = wg.shape[1]
    vec = lambda v: v.reshape(1, d).astype(F32)
    return pl.pallas_call(
        _ffn_kernel,
        grid=(t // tm, ff // tf),
        in_specs=[pl.BlockSpec((tm, d), lambda i, f: (i, 0)),
                  pl.BlockSpec((tm, d), lambda i, f: (i, 0)),
                  pl.BlockSpec((d, tf), lambda i, f: (0, f)),
                  pl.BlockSpec((d, tf), lambda i, f: (0, f)),
                  pl.BlockSpec((tf, d), lambda i, f: (f, 0)),
                  pl.BlockSpec((1, d), lambda i, f: (0, 0)),
                  pl.BlockSpec((1, d), lambda i, f: (0, 0))],
        out_specs=[pl.BlockSpec((tm, d), lambda i, f: (i, 0)),
                   pl.BlockSpec((tm, d), lambda i, f: (i, 0))],
        out_shape=[jax.ShapeDtypeStruct((t, d), F32), jax.ShapeDtypeStruct((t, d), BF16)],
        scratch_shapes=[pltpu.VMEM((tm, d), F32)],
        compiler_params=pltpu.CompilerParams(dimension_semantics=("parallel", "arbitrary")),
        name="ffn_ln",
    )(xb, x, wg, wu, wd, vec(ln_g), vec(ln_b))


_MOE_TM = 1024


def _route(route, counts, n_tiles):
    expert = route[:, 0:2].astype(jnp.int32)
    rank = route[:, 4:6].astype(jnp.int32)
    counts = counts[0, :N_EXPERTS].astype(jnp.int32)
    padded = (counts + _MOE_TM - 1) // _MOE_TM * _MOE_TM
    ends = jnp.cumsum(padded)
    starts = ends - padded
    onehot = expert[:, :, None] == jnp.arange(N_EXPERTS, dtype=jnp.int32)
    pos = (rank + jnp.sum(jnp.where(onehot, starts, 0), axis=-1)).reshape(-1)
    tile_start = jnp.arange(n_tiles, dtype=jnp.int32) * _MOE_TM
    tile_expert = jnp.minimum(jnp.sum((tile_start[:, None] >= ends[None, :]).astype(jnp.int32), axis=1),
                              N_EXPERTS - 1)
    return pos.astype(jnp.int32), tile_expert.astype(jnp.int32)


def _row_scatter_kernel(pos_ref, x_ref, init_ref, o_ref, sem, *, tm):
    del init_ref

    def body(j, carry):
        for choice in range(2):
            p = pos_ref[0, 0, 2 * j + choice]
            pltpu.make_async_copy(x_ref.at[pl.ds(j, 1)], o_ref.at[pl.ds(p, 1)], sem).start(priority=choice)
        return carry

    lax.fori_loop(0, tm, body, 0, unroll=8)
    for _ in range(2):
        pltpu.make_async_copy(x_ref, o_ref.at[pl.ds(0, tm)], sem).wait()


def _row_scatter(x, pos, n_rows, tm=512):
    t, d = x.shape
    return pl.pallas_call(
        functools.partial(_row_scatter_kernel, tm=tm),
        grid=(t // tm,),
        in_specs=[pl.BlockSpec((1, 1, 2 * tm), lambda i: (i, 0, 0), memory_space=pltpu.SMEM),
                  pl.BlockSpec((tm, d), lambda i: (i, 0)),
                  pl.BlockSpec(memory_space=pl.ANY)],
        out_specs=pl.BlockSpec(memory_space=pl.ANY),
        out_shape=jax.ShapeDtypeStruct((n_rows, d), x.dtype),
        scratch_shapes=[pltpu.SemaphoreType.DMA(())],
        input_output_aliases={2: 0},
        compiler_params=pltpu.CompilerParams(dimension_semantics=("arbitrary",)),
        name="moe_row_scatter",
    )(pos.reshape(t // tm, 1, 2 * tm), x, jnp.zeros((n_rows, d), x.dtype))


def _expert_ffn_kernel(te_ref, x_ref, wg_ref, wu_ref, wd_ref, o_ref, acc_ref):
    del te_ref
    f = pl.program_id(1)

    @pl.when(f == 0)
    def _():
        acc_ref[...] = jnp.zeros_like(acc_ref)

    xb = x_ref[...].astype(BF16)
    hg = _dot(xb, wg_ref[0].astype(BF16))
    hu = _dot(xb, wu_ref[0].astype(BF16))
    act = (hg * _sigmoid(hg) * hu).astype(BF16)
    acc_ref[...] += _dot(act, wd_ref[0].astype(BF16))

    @pl.when(f == pl.num_programs(1) - 1)
    def _():
        o_ref[...] = acc_ref[...]


def _expert_ffn(xs, tile_expert, wg, wu, wd, tf=512):
    n_rows, d = xs.shape
    ff = wg.shape[2]
    grid_spec = pltpu.PrefetchScalarGridSpec(
        num_scalar_prefetch=1,
        grid=(n_rows // _MOE_TM, ff // tf),
        in_specs=[pl.BlockSpec((_MOE_TM, d), lambda i, f, te: (i, 0)),
                  pl.BlockSpec((1, d, tf), lambda i, f, te: (te[i], 0, f)),
                  pl.BlockSpec((1, d, tf), lambda i, f, te: (te[i], 0, f)),
                  pl.BlockSpec((1, tf, d), lambda i, f, te: (te[i], f, 0))],
        out_specs=pl.BlockSpec((_MOE_TM, d), lambda i, f, te: (i, 0)),
        scratch_shapes=[pltpu.VMEM((_MOE_TM, d), F32)])
    return pl.pallas_call(
        _expert_ffn_kernel,
        grid_spec=grid_spec,
        out_shape=jax.ShapeDtypeStruct((n_rows, d), F32),
        compiler_params=pltpu.CompilerParams(dimension_semantics=("parallel", "arbitrary")),
        name="moe_expert_ffn",
    )(tile_expert, xs, wg, wu, wd)


def _combine_kernel(pos_ref, route_ref, x_ref, g_ref, b_ref, y_ref, o_ref, ybuf_ref, sem, *, tm):
    def body(j, carry):
        for choice in range(2):
            p = pos_ref[0, 0, 2 * j + choice]
            pltpu.make_async_copy(y_ref.at[pl.ds(p, 1)], ybuf_ref.at[choice, pl.ds(j, 1)],
                                  sem).start(priority=choice)
        return carry

    lax.fori_loop(0, tm, body, 0, unroll=8)
    for choice in range(2):
        pltpu.make_async_copy(y_ref.at[pl.ds(0, tm)], ybuf_ref.at[choice], sem).wait()
    route = route_ref[...]
    y = route[:, 2:3] * ybuf_ref[0] + route[:, 3:4] * ybuf_ref[1]
    o_ref[...] = _layer_norm(DEEPNORM_ALPHA * x_ref[...] + y, g_ref[...], b_ref[...])


def _combine_ln(ys, pos, route, x, ln_g, ln_b, tm=512):
    t, d = x.shape
    vec = lambda v: v.reshape(1, d).astype(F32)
    return pl.pallas_call(
        functools.partial(_combine_kernel, tm=tm),
        grid=(t // tm,),
        in_specs=[pl.BlockSpec((1, 1, 2 * tm), lambda i: (i, 0, 0), memory_space=pltpu.SMEM),
                  pl.BlockSpec((tm, LANES), lambda i: (i, 0)),
                  pl.BlockSpec((tm, d), lambda i: (i, 0)),
                  pl.BlockSpec((1, d), lambda i: (0, 0)),
                  pl.BlockSpec((1, d), lambda i: (0, 0)),
                  pl.BlockSpec(memory_space=pl.ANY)],
        out_specs=pl.BlockSpec((tm, d), lambda i: (i, 0)),
        out_shape=jax.ShapeDtypeStruct((t, d), F32),
        scratch_shapes=[pltpu.VMEM((2, tm, d), F32), pltpu.SemaphoreType.DMA(())],
        compiler_params=pltpu.CompilerParams(dimension_semantics=("arbitrary",)),
        name="moe_combine_ln",
    )(pos.reshape(t // tm, 1, 2 * tm), route, x, vec(ln_g), vec(ln_b), ys)


def _moe_ln(x, route, counts, wg, wu, wd, ln_g, ln_b):
    t, _ = x.shape
    n_tiles = 2 * t // _MOE_TM + N_EXPERTS
    pos, tile_expert = _route(route, counts, n_tiles)
    xs = _row_scatter(x, pos, n_tiles * _MOE_TM)
    ys = _expert_ffn(xs, tile_expert, wg, wu, wd)
    return _combine_ln(ys, pos, route, x, ln_g, ln_b)


def _even_layer(x, xb, batch, seq, dsa_bias, w_in, conv_w, conv_b, conv_ln_g, conv_ln_b, w_out,
                ln1_g, ln1_b, ffn_wg, ffn_wu, ffn_wd, ln2_g, ln2_b):
    w_in_p = jnp.zeros((D_MODEL, EVEN_IN_PAD), BF16).at[:, :w_in.shape[1]].set(w_in.astype(BF16))
    h = _matmul(x if xb is None else xb, w_in_p, 2048, 640, BF16).reshape(batch, seq, EVEN_IN_PAD)
    a_out = _conformer_conv(h, conv_w, conv_b, conv_ln_g, conv_ln_b)
    att = _dsa_attention(h, dsa_bias)
    w_out_b = w_out.astype(BF16)
    x1, x1b = _outproj_ln([a_out.reshape(-1, CONV_CH), att.reshape(-1, DSA_HEADS * DSA_HEAD_DIM)],
                          [w_out_b[:CONV_CH], w_out_b[CONV_CH:]], x, ln1_g, ln1_b)
    return _ffn_ln(x1b, x1, ffn_wg.astype(BF16), ffn_wu.astype(BF16), ffn_wd.astype(BF16), ln2_g, ln2_b)


def _odd_layer(x, xb, batch, seq, dil_bias, w_in, w_out, ln1_g, ln1_b, router, moe_wg, moe_wu, moe_wd,
               ln2_g, ln2_b):
    gw = 3 * DIL_HEADS * DIL_HEAD_DIM
    w_in_b = w_in.astype(BF16)
    hs = [_matmul_residue_major(x if xb is None else xb, w_in_b[:, g * gw:(g + 1) * gw], batch, seq, dil, 2048, 768)
          for g, (_, dil) in enumerate(DIL_GROUPS)]
    o = _dilated_attention(hs, dil_bias).reshape(-1, DIL_HEADS * DIL_HEAD_DIM)
    x1, _, route, counts = _outproj_ln([o], [w_out.astype(BF16)], x, ln1_g, ln1_b, router=router)
    out = _moe_ln(x1, route, counts, moe_wg, moe_wu, moe_wd, ln2_g, ln2_b)
    return out, out.astype(BF16)


def kernel(x, rel_bias, even_w_in, even_conv_w, even_conv_b, even_conv_ln_g, even_conv_ln_b, even_w_out, even_ln1_g, even_ln1_b, even_ffn_wg, even_ffn_wu, even_ffn_wd, even_ln2_g, even_ln2_b, odd_w_in, odd_w_out, odd_ln1_g, odd_ln1_b, odd_router, odd_moe_wg, odd_moe_wu, odd_moe_wd, odd_ln2_g, odd_ln2_b):
    batch, seq, d = x.shape
    assert d == D_MODEL and seq % DIL_CHUNK == 0
    depth = even_w_in.shape[0] + odd_w_in.shape[0]
    assert depth == DEPTH
    dsa_bias = _dsa_bias_tiles(rel_bias, seq)
    dil_bias = _dil_bias_tiles(rel_bias)
    xf = x.reshape(batch * seq, d)
    xb = None
    for layer in range(depth):
        i = layer // 2
        if layer % 2 == 0:
            xf, xb = _even_layer(xf, xb, batch, seq, dsa_bias, even_w_in[i], even_conv_w[i], even_conv_b[i],
                                 even_conv_ln_g[i], even_conv_ln_b[i], even_w_out[i], even_ln1_g[i],
                                 even_ln1_b[i], even_ffn_wg[i], even_ffn_wu[i], even_ffn_wd[i],
                                 even_ln2_g[i], even_ln2_b[i])
        else:
            xf, xb = _odd_layer(xf, xb, batch, seq, dil_bias, odd_w_in[i], odd_w_out[i], odd_ln1_g[i],
                                odd_ln1_b[i], odd_router[i], odd_moe_wg[i], odd_moe_wu[i], odd_moe_wd[i],
                                odd_ln2_g[i], odd_ln2_b[i])
    return xf.reshape(batch, seq, d)
```

```python
import functools
import math

import jax
import jax.numpy as jnp
from jax import lax
from jax.experimental import pallas as pl
from jax.experimental.pallas import tpu as pltpu

BF16 = jnp.bfloat16
F32 = jnp.float32

D_MODEL = 1024
CONV_CH = 512
CONV_WIDTH = 31
DSA_HEADS = 8
DSA_HEAD_DIM = 64
IDX_HEADS = 8
IDX_DIM = 64
DSA_TOPK_MAX = 256
Q_BLOCK = 128
DIL_GROUPS = ((128, 1), (512, 4), (2048, 16))
DIL_HEADS = 8
DIL_HEAD_DIM = 128
BAND_BLOCK = 128
NUM_BUCKETS = 32
MAX_DISTANCE = 2048
N_EXPERTS = 8
LN_EPS = 1e-5
NEG = -1e30
DEPTH = 2
DEEPNORM_ALPHA = (2 * DEPTH) ** 0.25

LANES = 128
EVEN_IN_PAD = 3200
INT_MIN = -(2 ** 31)
DIL_CHUNK = BAND_BLOCK * max(d for _, d in DIL_GROUPS)

_PARALLEL2 = pltpu.CompilerParams(dimension_semantics=("parallel", "parallel"))


def _layer_norm(y, g, b):
    mu = jnp.mean(y, axis=-1, keepdims=True)
    yc = y - mu
    var = jnp.mean(yc * yc, axis=-1, keepdims=True)
    return yc * lax.rsqrt(var + LN_EPS) * g + b


def _sigmoid(x):
    return 1.0 / (1.0 + jnp.exp(-x))


def _dot_nt(a, b):
    return lax.dot_general(a, b, (((1,), (1,)), ((), ())), preferred_element_type=F32)


def _dot(a, b):
    return jnp.dot(a, b, preferred_element_type=F32)


def _mm_kernel(x_ref, w_ref, o_ref):
    o_ref[...] = _dot(x_ref[...].astype(BF16), w_ref[...]).astype(o_ref.dtype)


def _matmul(x, w, tm, tn, out_dtype):
    t, k = x.shape
    n = w.shape[1]
    return pl.pallas_call(
        _mm_kernel,
        grid=(t // tm, n // tn),
        in_specs=[pl.BlockSpec((tm, k), lambda i, j: (i, 0)),
                  pl.BlockSpec((k, tn), lambda i, j: (0, j))],
        out_specs=pl.BlockSpec((tm, tn), lambda i, j: (i, j)),
        out_shape=jax.ShapeDtypeStruct((t, n), out_dtype),
        compiler_params=_PARALLEL2,
        name="matmul",
    )(x, w)


_MAX_ROW_STRIDE = 4


def _mm_perm_kernel(x_ref, w_ref, o_ref, acc_ref, tmp_ref, *, dil, tm, tn):
    acc = _dot(x_ref[...].astype(BF16), w_ref[...])
    for c in range(tn // LANES):
        acc_ref[c] = acc[:, c * LANES:(c + 1) * LANES]
    if dil <= _MAX_ROW_STRIDE:
        rows = tm // dil
        for r in range(dil):
            for c in range(tn // LANES):
                o_ref[0, r, :, c * LANES:(c + 1) * LANES] = (
                    acc_ref[c, pl.ds(r, rows, stride=dil), :].astype(o_ref.dtype))
        return
    s1 = _MAX_ROW_STRIDE
    s2 = dil // s1
    assert s2 <= _MAX_ROW_STRIDE
    rows1 = tm // s1
    rows = tm // dil
    for c in range(tn // LANES):
        for r1 in range(s1):
            tmp_ref[c, r1 * rows1:(r1 + 1) * rows1, :] = acc_ref[c, pl.ds(r1, rows1, stride=s1), :]
    for r1 in range(s1):
        for r2 in range(s2):
            for c in range(tn // LANES):
                o_ref[0, s1 * r2 + r1, :, c * LANES:(c + 1) * LANES] = (
                    tmp_ref[c, pl.ds(r1 * rows1 + r2, rows, stride=s2), :].astype(o_ref.dtype))


def _matmul_residue_major(x, w, batch, seq, dil, tm, tn):
    t, k = x.shape
    n = w.shape[1]
    if dil == 1:
        return _matmul(x, w, tm, tn, BF16).reshape(batch, seq, n)
    tiles_per_seq = seq // tm
    out = pl.pallas_call(
        functools.partial(_mm_perm_kernel, dil=dil, tm=tm, tn=tn),
        grid=(t // tm, n // tn),
        in_specs=[pl.BlockSpec((tm, k), lambda i, j: (i, 0)),
                  pl.BlockSpec((k, tn), lambda i, j: (0, j))],
        out_specs=pl.BlockSpec((1, dil, tm // dil, tn),
                               lambda i, j: (i // tiles_per_seq, 0, i % tiles_per_seq, j)),
        out_shape=jax.ShapeDtypeStruct((batch, dil, seq // dil, n), BF16),
        scratch_shapes=[pltpu.VMEM((tn // LANES, tm, LANES), F32), pltpu.VMEM((tn // LANES, tm, LANES), F32)],
        compiler_params=_PARALLEL2,
        name="matmul_residue_major",
    )(x, w)
    return out.reshape(batch, seq, n)


_CONV_HALO = 32
_CONV_ROWS = 64
_SUBLANES = 8


def _conv_kernel(val_ref, gate_ref, w_ref, cb_ref, g_ref, b_ref, o_ref, ext_ref, *, ts):
    s = pl.program_id(1)

    @pl.when(s == 0)
    def _():
        ext_ref[0, 0:_CONV_HALO, :] = jnp.zeros((_CONV_HALO, CONV_CH), F32)

    val = val_ref[0].astype(F32)
    gate = gate_ref[0].astype(F32)
    ext_ref[0, _CONV_HALO:_CONV_HALO + ts, :] = val * _sigmoid(gate)
    span = ts + _CONV_HALO - _SUBLANES
    for k in range(1, _SUBLANES):
        ext_ref[k, 0:span, :] = ext_ref[0, k:k + span, :]
    first_tap = _CONV_HALO - (CONV_WIDTH - 1)
    for rc in range(ts // _CONV_ROWS):
        r0 = rc * _CONV_ROWS
        acc = jnp.zeros((_CONV_ROWS, CONV_CH), F32)
        for j in range(CONV_WIDTH):
            shift = (first_tap + j) % _SUBLANES
            lo = r0 + first_tap + j - shift
            acc = acc + ext_ref[shift, lo:lo + _CONV_ROWS, :] * w_ref[j:j + 1, :]
        y = _layer_norm(acc + cb_ref[...], g_ref[...], b_ref[...])
        o_ref[0, r0:r0 + _CONV_ROWS, :] = (y * _sigmoid(y)).astype(o_ref.dtype)
    ext_ref[0, 0:_CONV_HALO, :] = ext_ref[0, ts:ts + _CONV_HALO, :]


def _conformer_conv(h, conv_w, conv_b, ln_g, ln_b, ts=512):
    b, s, _ = h.shape
    w_pad = jnp.zeros((32, CONV_CH), F32).at[:CONV_WIDTH].set(conv_w)
    vec = lambda v: v.reshape(1, CONV_CH).astype(F32)
    full = lambda shape: pl.BlockSpec(shape, lambda bi, si: (0, 0))
    return pl.pallas_call(
        functools.partial(_conv_kernel, ts=ts),
        grid=(b, s // ts),
        in_specs=[pl.BlockSpec((1, ts, CONV_CH), lambda bi, si: (bi, si, 0)),
                  pl.BlockSpec((1, ts, CONV_CH), lambda bi, si: (bi, si, 1)),
                  full((32, CONV_CH)), full((1, CONV_CH)), full((1, CONV_CH)), full((1, CONV_CH))],
        out_specs=pl.BlockSpec((1, ts, CONV_CH), lambda bi, si: (bi, si, 0)),
        out_shape=jax.ShapeDtypeStruct((b, s, CONV_CH), BF16),
        scratch_shapes=[pltpu.VMEM((_SUBLANES, ts + _CONV_HALO, CONV_CH), F32)],
        compiler_params=pltpu.CompilerParams(dimension_semantics=("parallel", "arbitrary")),
        name="conformer_conv",
    )(h, h, w_pad, vec(conv_b), vec(ln_g), vec(ln_b))


def _rel_bucket(dist):
    max_exact = NUM_BUCKETS // 2
    n = dist.astype(jnp.int32)
    nf = jnp.maximum(n, 1).astype(F32)
    large = max_exact + (jnp.log(nf / max_exact) / math.log(MAX_DISTANCE / max_exact)
                         * (NUM_BUCKETS - max_exact)).astype(jnp.int32)
    large = jnp.minimum(large, NUM_BUCKETS - 1)
    return jnp.where(n < max_exact, n, large)


def _bias_by_distance(rel_bias, n):
    return rel_bias[_rel_bucket(jnp.arange(n, dtype=jnp.int32))].astype(F32).T


def _toeplitz_kernel(win_ref, o_ref):
    n_win, span = win_ref.shape
    for n in range(n_win):
        y = jnp.broadcast_to(win_ref[n:n + 1, :], (LANES, span))
        y = pltpu.roll(y, 0, 1, stride=1, stride_axis=0)
        o_ref[n] = y[:, LANES:]


def _toeplitz_tiles(windows, per_step=8):
    n, span = windows.shape
    width = span - LANES
    return pl.pallas_call(
        _toeplitz_kernel,
        grid=(n // per_step,),
        in_specs=[pl.BlockSpec((per_step, span), lambda i: (i, 0))],
        out_specs=pl.BlockSpec((per_step, LANES, width), lambda i: (i, 0, 0)),
        out_shape=jax.ShapeDtypeStruct((n, LANES, width), F32),
        compiler_params=pltpu.CompilerParams(dimension_semantics=("parallel",)),
        name="bias_toeplitz",
    )(windows)


def _dsa_bias_tiles(rel_bias, seq):
    nblk = seq // Q_BLOCK
    heads = rel_bias.shape[1]
    f = _bias_by_distance(rel_bias, seq)
    u = jnp.concatenate([jnp.broadcast_to(f[:, :1], (heads, 2 * Q_BLOCK)), f], axis=1)
    blocks = u.reshape(heads, nblk + 2, Q_BLOCK)
    windows = jnp.concatenate([blocks[:, :-1], blocks[:, 1:]], axis=-1)
    windows = windows.transpose(1, 0, 2).reshape((nblk + 1) * heads, 2 * Q_BLOCK)
    return _toeplitz_tiles(windows).reshape(nblk + 1, heads, Q_BLOCK, Q_BLOCK)


def _dil_bias_tiles(rel_bias):
    max_dil = max(d for _, d in DIL_GROUPS)
    heads = rel_bias.shape[1]
    span = 2 * BAND_BLOCK
    f = _bias_by_distance(rel_bias, span * max_dil + 1)
    windows = []
    for _, dil in DIL_GROUPS:
        fd = f[:, 0:span * dil + 1:dil]
        tail = jnp.broadcast_to(fd[:, :1], (heads, BAND_BLOCK - 1))
        windows.append(jnp.concatenate([fd[:, ::-1], tail], axis=1))
    tiles = _toeplitz_tiles(jnp.concatenate(windows, axis=0))
    return tiles.reshape(len(DIL_GROUPS), heads, BAND_BLOCK, span)


_KC = 2 * Q_BLOCK
_VT_ROWS = DSA_HEAD_DIM + 16


def _sortable_key(score):
    score = jnp.where(score == 0.0, 0.0, score)
    bits = lax.bitcast_convert_type(score, jnp.int32)
    return jnp.where(bits < 0, bits ^ jnp.int32(0x7FFFFFFF), bits)


def _rows_reduce(x, op):
    rows = x.shape[0]
    return op(op(x.reshape(rows // 8, 8, LANES), axis=0), axis=0, keepdims=True)


def _keys_reduce(x, op):
    h, keys, _ = x.shape
    return op(op(x.reshape(h, keys // 8, 8, LANES), axis=1), axis=1, keepdims=True)


def _dsa_t_kernel(q_ref, qi_ref, wq_ref, k_ref, vt_ref, ki_ref, bias_hbm, o_ref,
                  bias_ref, keys_ref, qm_ref, acc_ref, dots_a_ref, dots_b_ref, sem, *, topk):
    b = pl.program_id(0)
    i = pl.program_id(1)

    @pl.when((b == 0) & (i == 0))
    def _():
        cp = pltpu.make_async_copy(bias_hbm, bias_ref, sem)
        cp.start()
        cp.wait()

    nch = (i + 2) // 2
    key_off = lax.broadcasted_iota(jnp.int32, (_KC, LANES), 0)
    q_pos = i * Q_BLOCK + lax.broadcasted_iota(jnp.int32, (_KC, LANES), 1)
    lane = lax.broadcasted_iota(jnp.int32, (Q_BLOCK, LANES), 1)
    low_half = lane < DSA_HEAD_DIM

    w_t = (wq_ref[0].astype(F32) * (IDX_HEADS ** -0.5 * IDX_DIM ** -0.5)).T
    qi = qi_ref[0].astype(F32)
    qi_heads = []
    for p in range(IDX_HEADS // 2):
        pair = qi[:, p * LANES:(p + 1) * LANES]
        qi_heads.append(jnp.where(low_half, pair, 0.0))
        qi_heads.append(jnp.where(low_half, pltpu.roll(pair, IDX_DIM, 1), 0.0))
    qi_stack = jnp.concatenate(qi_heads, axis=0).astype(BF16)
    q_scaled = (q_ref[0].astype(F32) * (DSA_HEAD_DIM ** -0.5)).astype(BF16)
    for p in range(DSA_HEADS // 2):
        qp = q_scaled[:, p * LANES:(p + 1) * LANES]
        qm_ref[2 * p] = jnp.where(low_half, qp, jnp.zeros_like(qp))
        qm_ref[2 * p + 1] = jnp.where(low_half, jnp.zeros_like(qp), qp)

    def index_dots(c, dst_ref):
        c = jnp.minimum(c, nch - 1)
        k0 = pl.multiple_of(c * _KC, _KC)
        ki = ki_ref[0, pl.ds(k0, _KC), :]
        dst_ref[...] = _dot_nt(ki, qi_stack)

    def consume_dots(c, src_ref):
        k0 = c * _KC
        score = jnp.zeros((_KC, LANES), F32)
        for h in range(IDX_HEADS):
            w_row = w_t[IDX_DIM + h:IDX_DIM + h + 1, :]
            score = score + w_row * jnp.maximum(src_ref[:, h * Q_BLOCK:(h + 1) * Q_BLOCK], 0.0)
        keys_ref[c] = jnp.where(k0 + key_off <= q_pos, _sortable_key(score), jnp.int32(INT_MIN))

    index_dots(0, dots_a_ref)

    def score_body(c2, carry):
        index_dots(2 * c2 + 1, dots_b_ref)
        consume_dots(2 * c2, dots_a_ref)
        index_dots(2 * c2 + 2, dots_a_ref)
        consume_dots(2 * c2 + 1, dots_b_ref)
        return carry

    lax.fori_loop(0, (nch + 1) // 2, score_body, 0)

    def count_ge(cand):
        cb = jnp.broadcast_to(cand, (8, LANES))
        chains = 4

        def body(c2, acc):
            for half in range(2):
                kk = keys_ref[2 * c2 + half].reshape(_KC // (8 * chains), chains, 8, LANES)
                acc = acc + jnp.sum(jnp.where(kk >= cb, 1.0, 0.0), axis=0)
            return acc

        acc = lax.fori_loop(0, (nch + 1) // 2, body, jnp.zeros((chains, 8, LANES), F32))
        return jnp.sum(jnp.sum(acc, axis=0), axis=0, keepdims=True)

    kf = float(topk)
    zero = jnp.zeros((1, LANES), jnp.int32)
    thr0 = jnp.where(count_ge(zero) >= kf, zero, jnp.full((1, LANES), INT_MIN, jnp.int32))

    def bit_body(t, thr):
        cand = thr | jnp.left_shift(jnp.int32(1), 30 - t)
        return jnp.where(count_ge(cand) >= kf, cand, thr)

    thr = lax.fori_loop(0, 31, bit_body, thr0)
    n_gt = count_ge(jnp.where(thr == jnp.int32(2 ** 31 - 1), thr, thr + 1))
    need = kf - n_gt
    tri = (lax.broadcasted_iota(jnp.int32, (_KC, _KC), 1)
           <= lax.broadcasted_iota(jnp.int32, (_KC, _KC), 0)).astype(BF16)

    acc_ref[...] = jnp.zeros_like(acc_ref)

    def attn_body(c, carry):
        ms, tie_carry = carry
        k0 = pl.multiple_of(c * _KC, _KC)
        kk = keys_ref[c]
        causal = kk > jnp.int32(INT_MIN)
        eq = (kk == thr) & causal
        eq_f = jnp.where(eq, 1.0, 0.0)
        rank = _dot(tri, eq_f.astype(BF16)) + tie_carry
        sel = ((kk > thr) & causal) | (eq & (rank <= need))
        tie_carry = tie_carry + _rows_reduce(eq_f, jnp.sum)
        d = i - 2 * c
        kc = k_ref[0, pl.ds(k0, _KC), :]
        k_heads = jnp.stack([kc[:, (h // 2) * LANES:(h // 2 + 1) * LANES] for h in range(DSA_HEADS)])
        s = jnp.einsum("hkd,hqd->hkq", k_heads, qm_ref[...], preferred_element_type=F32)
        bias = jnp.concatenate([bias_ref[d + 1], bias_ref[d]], axis=1)
        s = jnp.where(sel, s + bias, NEG)
        m_new = jnp.maximum(ms, _keys_reduce(s, jnp.max))
        alpha = jnp.exp(ms - m_new)
        pr = jnp.exp((s - m_new).astype(BF16))
        pv = jnp.einsum("hdk,hkq->hdq", vt_ref[0, c], pr, preferred_element_type=F32)
        acc_ref[...] = alpha * acc_ref[...] + pv
        return m_new, tie_carry

    init = (jnp.full((DSA_HEADS, 1, LANES), NEG, F32), jnp.zeros((1, LANES), F32))
    lax.fori_loop(0, nch, attn_body, init)
    acc = acc_ref[...]
    out_t = acc[:, :DSA_HEAD_DIM] / acc[:, DSA_HEAD_DIM:DSA_HEAD_DIM + 1]
    o_ref[0] = out_t.reshape(DSA_HEADS * DSA_HEAD_DIM, Q_BLOCK).T.astype(o_ref.dtype)


def _dsa_attention(h, bias_tiles):
    b, s, _ = h.shape
    width = DSA_HEADS * DSA_HEAD_DIM
    nblk = s // Q_BLOCK
    topk = min(DSA_TOPK_MAX, s // 4)
    kw_block = (CONV_CH * 2 + 4 * width) // LANES
    v_cols = 2 * CONV_CH + 2 * width
    vt = h[:, :, v_cols:v_cols + width].reshape(b, s // _KC, _KC, DSA_HEADS, DSA_HEAD_DIM).transpose(0, 1, 3, 4, 2)
    vt = jnp.concatenate([vt, jnp.ones((b, s // _KC, DSA_HEADS, _VT_ROWS - DSA_HEAD_DIM, _KC), BF16)], axis=3)
    return pl.pallas_call(
        functools.partial(_dsa_t_kernel, topk=topk),
        grid=(b, nblk),
        in_specs=[pl.BlockSpec((1, Q_BLOCK, width), lambda bi, qi: (bi, qi, 2)),
                  pl.BlockSpec((1, Q_BLOCK, width), lambda bi, qi: (bi, qi, 5)),
                  pl.BlockSpec((1, Q_BLOCK, LANES), lambda bi, qi: (bi, qi, kw_block)),
                  pl.BlockSpec((1, s, width), lambda bi, qi: (bi, 0, 3)),
                  pl.BlockSpec((1, s // _KC, DSA_HEADS, _VT_ROWS, _KC), lambda bi, qi: (bi, 0, 0, 0, 0)),
                  pl.BlockSpec((1, s, LANES), lambda bi, qi: (bi, 0, kw_block)),
                  pl.BlockSpec(memory_space=pl.ANY)],
        out_specs=pl.BlockSpec((1, Q_BLOCK, width), lambda bi, qi: (bi, qi, 0)),
        out_shape=jax.ShapeDtypeStruct((b, s, width), BF16),
        scratch_shapes=[pltpu.VMEM((nblk + 1, DSA_HEADS, Q_BLOCK, Q_BLOCK), F32),
                        pltpu.VMEM((s // _KC, _KC, Q_BLOCK), jnp.int32),
                        pltpu.VMEM((DSA_HEADS, Q_BLOCK, LANES), BF16),
                        pltpu.VMEM((DSA_HEADS, _VT_ROWS, Q_BLOCK), F32),
                        pltpu.VMEM((_KC, IDX_HEADS * Q_BLOCK), F32),
                        pltpu.VMEM((_KC, IDX_HEADS * Q_BLOCK), F32),
                        pltpu.SemaphoreType.DMA(())],
        compiler_params=pltpu.CompilerParams(dimension_semantics=("arbitrary", "arbitrary")),
        name="dsa_attention",
    )(h, h, h, h, vt, h, bias_tiles)


_DIL_WAVE = 8


def _dil_kernel(*refs, seq):
    n_g = len(DIL_GROUPS)
    qkv = refs[:3 * n_g]
    bias_ref, o_ref, og_ref, lg_ref = refs[3 * n_g:]
    c = pl.program_id(2)
    ii = lax.broadcasted_iota(jnp.int32, (BAND_BLOCK, BAND_BLOCK), 0)
    jj = lax.broadcasted_iota(jnp.int32, (BAND_BLOCK, BAND_BLOCK), 1)
    valid_prev = jj >= ii
    valid_cur = ii >= jj
    scale = DIL_HEAD_DIM ** -0.5
    for g, (window, dil) in enumerate(DIL_GROUPS):
        assert window // dil == BAND_BLOCK
        q_ref, k_ref, v_ref = qkv[3 * g:3 * g + 3]
        sub_len = seq // dil
        blocks_per_chunk = DIL_CHUNK // (BAND_BLOCK * dil)
        bias_prev = bias_ref[g, 0, :, :BAND_BLOCK]
        bias_cur = bias_ref[g, 0, :, BAND_BLOCK:]
        tiles = [(r, nl) for r in range(dil) for nl in range(blocks_per_chunk)]
        for w0 in range(0, len(tiles), _DIL_WAVE):
            wave = tiles[w0:w0 + _DIL_WAVE]
            cur, prev, mask_prev = [], [], []
            for r, nl in wave:
                row0 = pl.multiple_of(r * sub_len + c * (DIL_CHUNK // dil) + nl * BAND_BLOCK, BAND_BLOCK)
                cur.append(pl.ds(row0, BAND_BLOCK))
                prev.append(pl.ds(pl.multiple_of(jnp.maximum(row0 - BAND_BLOCK, 0), BAND_BLOCK), BAND_BLOCK))
                mask_prev.append(valid_prev if nl > 0 else (valid_prev & (c > 0)))
            gather = lambda ref, slices: jnp.stack([ref[0, sl, :] for sl in slices])
            qb = gather(q_ref, cur)
            s_cur = jnp.einsum("wqd,wkd->wqk", qb, gather(k_ref, cur), preferred_element_type=F32)
            s_prev = jnp.einsum("wqd,wkd->wqk", qb, gather(k_ref, prev), preferred_element_type=F32)
            s_cur = jnp.where(valid_cur, s_cur * scale + bias_cur, NEG)
            s_prev = jnp.where(jnp.stack(mask_prev), s_prev * scale + bias_prev, NEG)
            m = jnp.max(jnp.maximum(s_cur, s_prev), axis=2, keepdims=True)
            p_cur = jnp.exp(s_cur - m)
            p_prev = jnp.exp(s_prev - m)
            ssum = jnp.sum(p_cur + p_prev, axis=2, keepdims=True)
            o = (jnp.einsum("wqk,wkd->wqd", p_cur.astype(BF16), gather(v_ref, cur), preferred_element_type=F32)
                 + jnp.einsum("wqk,wkd->wqd", p_prev.astype(BF16), gather(v_ref, prev),
                              preferred_element_type=F32)) / ssum
            lse = m + jnp.log(ssum)
            for w, (r, nl) in enumerate(wave):
                dst = pl.ds(r + nl * BAND_BLOCK * dil, BAND_BLOCK, stride=dil) if dil > 1 else pl.ds(
                    nl * BAND_BLOCK, BAND_BLOCK)
                og_ref[g, dst, :] = o[w]
                lg_ref[g, dst, :] = jnp.broadcast_to(lse[w], (BAND_BLOCK, LANES))
    rows = 256
    for rc in range(DIL_CHUNK // rows):
        sl = slice(rc * rows, (rc + 1) * rows)
        lses = [lg_ref[g, sl, :] for g in range(n_g)]
        mx = functools.reduce(jnp.maximum, lses)
        ws = [jnp.exp(l - mx) for l in lses]
        den = functools.reduce(lambda a, b_: a + b_, ws)
        num = functools.reduce(lambda a, b_: a + b_, [ws[g] * og_ref[g, sl, :] for g in range(n_g)])
        o_ref[0, sl, :] = (num / den).astype(o_ref.dtype)


def _dilated_attention(hs, bias_tiles):
    b, s, _ = hs[0].shape
    in_specs = []
    args = []
    for hg in hs:
        for part in range(3):
            in_specs.append(pl.BlockSpec((1, s, DIL_HEAD_DIM),
                                         lambda bi, hi, ci, part=part: (bi, 0, part * DIL_HEADS + hi)))
            args.append(hg)
    in_specs.append(pl.BlockSpec((len(DIL_GROUPS), 1, BAND_BLOCK, 2 * BAND_BLOCK),
                                 lambda bi, hi, ci: (0, hi, 0, 0)))
    return pl.pallas_call(
        functools.partial(_dil_kernel, seq=s),
        grid=(b, DIL_HEADS, s // DIL_CHUNK),
        in_specs=in_specs,
        out_specs=pl.BlockSpec((1, DIL_CHUNK, DIL_HEAD_DIM), lambda bi, hi, ci: (bi, ci, hi)),
        out_shape=jax.ShapeDtypeStruct((b, s, DIL_HEADS * DIL_HEAD_DIM), BF16),
        scratch_shapes=[pltpu.VMEM((len(DIL_GROUPS), DIL_CHUNK, LANES), F32),
                        pltpu.VMEM((len(DIL_GROUPS), DIL_CHUNK, LANES), F32)],
        compiler_params=pltpu.CompilerParams(dimension_semantics=("parallel", "parallel", "arbitrary")),
        name="dilated_attention",
    )(*args, bias_tiles)


def _outproj_kernel(*refs, n_in, with_router):
    ins = refs[:n_in]
    ws = refs[n_in:2 * n_in]
    x_ref, g_ref, b_ref = refs[2 * n_in:2 * n_in + 3]
    rest = refs[2 * n_in + 3:]
    mix = _dot(ins[0][...], ws[0][...])
    for a_ref, w_ref in zip(ins[1:], ws[1:]):
        mix = mix + _dot(a_ref[...], w_ref[...])
    xn = _layer_norm(DEEPNORM_ALPHA * x_ref[...] + mix, g_ref[...], b_ref[...])
    if not with_router:
        o_ref, ob_ref = rest
    else:
        router_ref, o_ref, ob_ref, gates_ref, counts_ref, run_ref, tri_ref = rest
        tm = xn.shape[0]

        @pl.when(pl.program_id(0) == 0)
        def _():
            run_ref[...] = jnp.zeros_like(run_ref)
            tri_ref[...] = (lax.broadcasted_iota(jnp.int32, (tm, tm), 0)
                            > lax.broadcasted_iota(jnp.int32, (tm, tm), 1)).astype(BF16)

        logits = _dot(xn.astype(BF16), router_ref[...])
        lane = lax.broadcasted_iota(jnp.int32, logits.shape, 1).astype(F32)
        lg = jnp.where(lane < N_EXPERTS, logits, -jnp.inf)
        m1 = jnp.max(lg, axis=1, keepdims=True)
        i1 = jnp.min(jnp.where(lg == m1, lane, float(LANES)), axis=1, keepdims=True)
        lg2 = jnp.where(lane == i1, -jnp.inf, lg)
        m2 = jnp.max(lg2, axis=1, keepdims=True)
        i2 = jnp.min(jnp.where(lg2 == m2, lane, float(LANES)), axis=1, keepdims=True)
        e2 = jnp.exp(m2 - m1)
        den = 1.0 + e2
        oh1 = jnp.where(lane == i1, 1.0, 0.0)
        oh2 = jnp.where(lane == i2, 1.0, 0.0)
        both = oh1 + oh2
        before = _dot(tri_ref[...], both.astype(BF16)) + run_ref[...]
        r1 = jnp.sum(before * oh1, axis=1, keepdims=True)
        r2 = jnp.sum(before * oh2, axis=1, keepdims=True)
        run_ref[...] += jnp.sum(both, axis=0, keepdims=True)
        counts_ref[...] = run_ref[...]
        gates_ref[...] = (jnp.where(lane == 0.0, i1, 0.0) + jnp.where(lane == 1.0, i2, 0.0)
                          + jnp.where(lane == 2.0, 1.0 / den, 0.0) + jnp.where(lane == 3.0, e2 / den, 0.0)
                          + jnp.where(lane == 4.0, r1, 0.0) + jnp.where(lane == 5.0, r2, 0.0))
    o_ref[...] = xn
    ob_ref[...] = xn.astype(BF16)


def _outproj_ln(ins, ws, x, ln_g, ln_b, router=None, tm=512):
    t, d = x.shape
    n_in = len(ins)
    row = lambda width: pl.BlockSpec((tm, width), lambda i: (i, 0))
    full = lambda a: pl.BlockSpec(a.shape, lambda i: (0, 0))
    vec = lambda v: v.reshape(1, d).astype(F32)
    args = list(ins) + list(ws) + [x, vec(ln_g), vec(ln_b)]
    in_specs = [row(a.shape[1]) for a in ins] + [full(w) for w in ws] + [row(d), full(vec(ln_g)), full(vec(ln_b))]
    out_shape = [jax.ShapeDtypeStruct((t, d), F32), jax.ShapeDtypeStruct((t, d), BF16)]
    out_specs = [row(d), row(d)]
    if router is not None:
        router_pad = jnp.zeros((d, LANES), BF16).at[:, :N_EXPERTS].set(router.astype(BF16))
        args.append(router_pad)
        in_specs.append(full(router_pad))
        out_shape += [jax.ShapeDtypeStruct((t, LANES), F32), jax.ShapeDtypeStruct((1, LANES), F32)]
        out_specs += [row(LANES), pl.BlockSpec((1, LANES), lambda i: (0, 0))]
        scratch = [pltpu.VMEM((1, LANES), F32), pltpu.VMEM((tm, tm), BF16)]
    else:
        scratch = []
    semantics = "arbitrary" if router is not None else "parallel"
    return pl.pallas_call(
        functools.partial(_outproj_kernel, n_in=n_in, with_router=router is not None),
        grid=(t // tm,),
        in_specs=in_specs,
        out_specs=out_specs,
        out_shape=out_shape,
        scratch_shapes=scratch,
        compiler_params=pltpu.CompilerParams(dimension_semantics=(semantics,)),
        name="outproj_ln",
    )(*args)


def _ffn_kernel(xb_ref, x_ref, wg_ref, wu_ref, wd_ref, g_ref, b_ref, o_ref, ob_ref, acc_ref):
    f = pl.program_id(1)

    @pl.when(f == 0)
    def _():
        acc_ref[...] = jnp.zeros_like(acc_ref)

    xb = xb_ref[...]
    hg = _dot(xb, wg_ref[...])
    hu = _dot(xb, wu_ref[...])
    act = (hg * _sigmoid(hg) * hu).astype(BF16)
    acc_ref[...] += _dot(act, wd_ref[...])

    @pl.when(f == pl.num_programs(1) - 1)
    def _():
        xn = _layer_norm(DEEPNORM_ALPHA * x_ref[...] + acc_ref[...], g_ref[...], b_ref[...])
        o_ref[...] = xn
        ob_ref[...] = xn.astype(BF16)


def _ffn_ln(xb, x, wg, wu, wd, ln_g, ln_b, tm=512, tf=1408):
    t, d = x.shape
    ff = wg.shape[1]
    vec = lambda v: v.reshape(1, d).astype(F32)
    return pl.pallas_call(
        _ffn_kernel,
        grid=(t // tm, ff // tf),
        in_specs=[pl.BlockSpec((tm, d), lambda i, f: (i, 0)),
                  pl.BlockSpec((tm, d), lambda i, f: (i, 0)),
                  pl.BlockSpec((d, tf), lambda i, f: (0, f)),
                  pl.BlockSpec((d, tf), lambda i, f: (0, f)),
                  pl.BlockSpec((tf, d), lambda i, f: (f, 0)),
                  pl.BlockSpec((1, d), lambda i, f: (0, 0)),
                  pl.BlockSpec((1, d), lambda i, f: (0, 0))],
        out_specs=[pl.BlockSpec((tm, d), lambda i, f: (i, 0)),
                   pl.BlockSpec((tm, d), lambda i, f: (i, 0))],
        out_shape=[jax.ShapeDtypeStruct((t, d), F32), jax.ShapeDtypeStruct((t, d), BF16)],
        scratch_shapes=[pltpu.VMEM((tm, d), F32)],
        compiler_params=pltpu.CompilerParams(dimension_semantics=("parallel", "arbitrary")),
        name="ffn_ln",
    )(xb, x, wg, wu, wd, vec(ln_g), vec(ln_b))


_MOE_TM = 1024


def _route(route, counts, n_tiles):
    expert = route[:, 0:2].astype(jnp.int32)
    rank = route[:, 4:6].astype(jnp.int32)
    counts = counts[0, :N_EXPERTS].astype(jnp.int32)
    padded = (counts + _MOE_TM - 1) // _MOE_TM * _MOE_TM
    ends = jnp.cumsum(padded)
    starts = ends - padded
    onehot = expert[:, :, None] == jnp.arange(N_EXPERTS, dtype=jnp.int32)
    pos = (rank + jnp.sum(jnp.where(onehot, starts, 0), axis=-1)).reshape(-1)
    tile_start = jnp.arange(n_tiles, dtype=jnp.int32) * _MOE_TM
    tile_expert = jnp.minimum(jnp.sum((tile_start[:, None] >= ends[None, :]).astype(jnp.int32), axis=1),
                              N_EXPERTS - 1)
    return pos.astype(jnp.int32), tile_expert.astype(jnp.int32)


def _row_scatter_kernel(pos_ref, x_ref, init_ref, o_ref, sem, *, tm):
    del init_ref

    def body(j, carry):
        for choice in range(2):
            p = pos_ref[0, 0, 2 * j + choice]
            pltpu.make_async_copy(x_ref.at[pl.ds(j, 1)], o_ref.at[pl.ds(p, 1)], sem).start(priority=choice)
        return carry

    lax.fori_loop(0, tm, body, 0, unroll=8)
    for _ in range(2):
        pltpu.make_async_copy(x_ref, o_ref.at[pl.ds(0, tm)], sem).wait()


def _row_scatter(x, pos, init, tm=512):
    t, d = x.shape
    n_rows = init.shape[0]
    return pl.pallas_call(
        functools.partial(_row_scatter_kernel, tm=tm),
        grid=(t // tm,),
        in_specs=[pl.BlockSpec((1, 1, 2 * tm), lambda i: (i, 0, 0), memory_space=pltpu.SMEM),
                  pl.BlockSpec((tm, d), lambda i: (i, 0)),
                  pl.BlockSpec(memory_space=pl.ANY)],
        out_specs=pl.BlockSpec(memory_space=pl.ANY),
        out_shape=jax.ShapeDtypeStruct((n_rows, d), x.dtype),
        scratch_shapes=[pltpu.SemaphoreType.DMA(())],
        input_output_aliases={2: 0},
        compiler_params=pltpu.CompilerParams(dimension_semantics=("arbitrary",)),
        name="moe_row_scatter",
    )(pos.reshape(t // tm, 1, 2 * tm), x, init)


def _expert_ffn_kernel(te_ref, x_ref, wg_ref, wu_ref, wd_ref, o_ref, acc_ref):
    del te_ref
    f = pl.program_id(1)

    @pl.when(f == 0)
    def _():
        acc_ref[...] = jnp.zeros_like(acc_ref)

    xb = x_ref[...].astype(BF16)
    hg = _dot(xb, wg_ref[0].astype(BF16))
    hu = _dot(xb, wu_ref[0].astype(BF16))
    act = (hg * _sigmoid(hg) * hu).astype(BF16)
    acc_ref[...] += _dot(act, wd_ref[0].astype(BF16))

    @pl.when(f == pl.num_programs(1) - 1)
    def _():
        o_ref[...] = acc_ref[...]


def _expert_ffn(xs, tile_expert, wg, wu, wd, tf=512):
    n_rows, d = xs.shape
    ff = wg.shape[2]
    grid_spec = pltpu.PrefetchScalarGridSpec(
        num_scalar_prefetch=1,
        grid=(n_rows // _MOE_TM, ff // tf),
        in_specs=[pl.BlockSpec((_MOE_TM, d), lambda i, f, te: (i, 0)),
                  pl.BlockSpec((1, d, tf), lambda i, f, te: (te[i], 0, f)),
                  pl.BlockSpec((1, d, tf), lambda i, f, te: (te[i], 0, f)),
                  pl.BlockSpec((1, tf, d), lambda i, f, te: (te[i], f, 0))],
        out_specs=pl.BlockSpec((_MOE_TM, d), lambda i, f, te: (i, 0)),
        scratch_shapes=[pltpu.VMEM((_MOE_TM, d), F32)])
    return pl.pallas_call(
        _expert_ffn_kernel,
        grid_spec=grid_spec,
        out_shape=jax.ShapeDtypeStruct((n_rows, d), F32),
        compiler_params=pltpu.CompilerParams(dimension_semantics=("parallel", "arbitrary")),
        name="moe_expert_ffn",
    )(tile_expert, xs, wg, wu, wd)


def _combine_kernel(pos_ref, pos_next_ref, route_ref, x_ref, g_ref, b_ref, y_ref, o_ref, ybuf_ref, sems, *, tm):
    i = pl.program_id(0)
    n = pl.num_programs(0)

    def gather(src_pos_ref, slot):
        def body(j, carry):
            for choice in range(2):
                p = src_pos_ref[0, 0, 2 * j + choice]
                pltpu.make_async_copy(y_ref.at[pl.ds(p, 1)], ybuf_ref.at[slot, choice, pl.ds(j, 1)],
                                      sems.at[slot]).start(priority=choice)
            return carry

        lax.fori_loop(0, tm, body, 0, unroll=8)

    @pl.when(i == 0)
    def _():
        gather(pos_ref, 0)

    slot = i % 2

    @pl.when(i + 1 < n)
    def _():
        gather(pos_next_ref, 1 - slot)

    for choice in range(2):
        pltpu.make_async_copy(y_ref.at[pl.ds(0, tm)], ybuf_ref.at[slot, choice], sems.at[slot]).wait()
    route = route_ref[...]
    y = route[:, 2:3] * ybuf_ref[slot, 0] + route[:, 3:4] * ybuf_ref[slot, 1]
    o_ref[...] = _layer_norm(DEEPNORM_ALPHA * x_ref[...] + y, g_ref[...], b_ref[...])


def _combine_ln(ys, pos, route, x, ln_g, ln_b, tm=512):
    t, d = x.shape
    vec = lambda v: v.reshape(1, d).astype(F32)
    last = t // tm - 1
    pos3 = pos.reshape(t // tm, 1, 2 * tm)
    return pl.pallas_call(
        functools.partial(_combine_kernel, tm=tm),
        grid=(t // tm,),
        in_specs=[pl.BlockSpec((1, 1, 2 * tm), lambda i: (i, 0, 0), memory_space=pltpu.SMEM),
                  pl.BlockSpec((1, 1, 2 * tm), lambda i: (jnp.minimum(i + 1, last), 0, 0), memory_space=pltpu.SMEM),
                  pl.BlockSpec((tm, LANES), lambda i: (i, 0)),
                  pl.BlockSpec((tm, d), lambda i: (i, 0)),
                  pl.BlockSpec((1, d), lambda i: (0, 0)),
                  pl.BlockSpec((1, d), lambda i: (0, 0)),
                  pl.BlockSpec(memory_space=pl.ANY)],
        out_specs=pl.BlockSpec((tm, d), lambda i: (i, 0)),
        out_shape=jax.ShapeDtypeStruct((t, d), F32),
        scratch_shapes=[pltpu.VMEM((2, 2, tm, d), F32), pltpu.SemaphoreType.DMA((2,))],
        compiler_params=pltpu.CompilerParams(dimension_semantics=("arbitrary",)),
        name="moe_combine_ln",
    )(pos3, pos3, route, x, vec(ln_g), vec(ln_b), ys)


def _moe_ln(x, route, counts, wg, wu, wd, ln_g, ln_b):
    t, _ = x.shape
    n_tiles = 2 * t // _MOE_TM + N_EXPERTS
    pos, tile_expert = _route(route, counts, n_tiles)
    xs = _row_scatter(x, pos, jnp.zeros((n_tiles * _MOE_TM, x.shape[1]), x.dtype))
    ys = _expert_ffn(xs, tile_expert, wg, wu, wd)
    return _combine_ln(ys, pos, route, x, ln_g, ln_b)


def _even_layer(x, xb, batch, seq, dsa_bias, w_in, conv_w, conv_b, conv_ln_g, conv_ln_b, w_out,
                ln1_g, ln1_b, ffn_wg, ffn_wu, ffn_wd, ln2_g, ln2_b):
    w_in_p = jnp.zeros((D_MODEL, EVEN_IN_PAD), BF16).at[:, :w_in.shape[1]].set(w_in.astype(BF16))
    h = _matmul(x if xb is None else xb, w_in_p, 2048, 640, BF16).reshape(batch, seq, EVEN_IN_PAD)
    a_out = _conformer_conv(h, conv_w, conv_b, conv_ln_g, conv_ln_b)
    att = _dsa_attention(h, dsa_bias)
    w_out_b = w_out.astype(BF16)
    x1, x1b = _outproj_ln([a_out.reshape(-1, CONV_CH), att.reshape(-1, DSA_HEADS * DSA_HEAD_DIM)],
                          [w_out_b[:CONV_CH], w_out_b[CONV_CH:]], x, ln1_g, ln1_b)
    return _ffn_ln(x1b, x1, ffn_wg.astype(BF16), ffn_wu.astype(BF16), ffn_wd.astype(BF16), ln2_g, ln2_b)


def _odd_layer(x, xb, batch, seq, dil_bias, w_in, w_out, ln1_g, ln1_b, router, moe_wg, moe_wu, moe_wd,
               ln2_g, ln2_b):
    gw = 3 * DIL_HEADS * DIL_HEAD_DIM
    w_in_b = w_in.astype(BF16)
    hs = [_matmul_residue_major(x if xb is None else xb, w_in_b[:, g * gw:(g + 1) * gw], batch, seq, dil, 2048, 768)
          for g, (_, dil) in enumerate(DIL_GROUPS)]
    o = _dilated_attention(hs, dil_bias).reshape(-1, DIL_HEADS * DIL_HEAD_DIM)
    x1, _, route, counts = _outproj_ln([o], [w_out.astype(BF16)], x, ln1_g, ln1_b, router=router)
    out = _moe_ln(x1, route, counts, moe_wg, moe_wu, moe_wd, ln2_g, ln2_b)
    return out, out.astype(BF16)


def kernel(x, rel_bias, even_w_in, even_conv_w, even_conv_b, even_conv_ln_g, even_conv_ln_b, even_w_out, even_ln1_g, even_ln1_b, even_ffn_wg, even_ffn_wu, even_ffn_wd, even_ln2_g, even_ln2_b, odd_w_in, odd_w_out, odd_ln1_g, odd_ln1_b, odd_router, odd_moe_wg, odd_moe_wu, odd_moe_wd, odd_ln2_g, odd_ln2_b):
    batch, seq, d = x.shape
    assert d == D_MODEL and seq % DIL_CHUNK == 0
    depth = even_w_in.shape[0] + odd_w_in.shape[0]
    assert depth == DEPTH
    dsa_bias = _dsa_bias_tiles(rel_bias, seq)
    dil_bias = _dil_bias_tiles(rel_bias)
    xf = x.reshape(batch * seq, d)
    xb = None
    for layer in range(depth):
        i = layer // 2
        if layer % 2 == 0:
            xf, xb = _even_layer(xf, xb, batch, seq, dsa_bias, even_w_in[i], even_conv_w[i], even_conv_b[i],
                                 even_conv_ln_g[i], even_conv_ln_b[i], even_w_out[i], even_ln1_g[i],
                                 even_ln1_b[i], even_ffn_wg[i], even_ffn_wu[i], even_ffn_wd[i],
                                 even_ln2_g[i], even_ln2_b[i])
        else:
            xf, xb = _odd_layer(xf, xb, batch, seq, dil_bias, odd_w_in[i], odd_w_out[i], odd_ln1_g[i],
                                odd_ln1_b[i], odd_router[i], odd_moe_wg[i], odd_moe_wu[i], odd_moe_wd[i],
                                odd_ln2_g[i], odd_ln2_b[i])
    return xf.reshape(batch, seq, d)
```

```python
import functools
import math

import jax
import jax.numpy as jnp
from jax import lax
from jax.experimental import pallas as pl
from jax.experimental.pallas import tpu as pltpu

BF16 = jnp.bfloat16
F32 = jnp.float32

D_MODEL = 1024
CONV_CH = 512
CONV_WIDTH = 31
DSA_HEADS = 8
DSA_HEAD_DIM = 64
IDX_HEADS = 8
IDX_DIM = 64
DSA_TOPK_MAX = 256
Q_BLOCK = 128
DIL_GROUPS = ((128, 1), (512, 4), (2048, 16))
DIL_HEADS = 8
DIL_HEAD_DIM = 128
BAND_BLOCK = 128
NUM_BUCKETS = 32
MAX_DISTANCE = 2048
N_EXPERTS = 8
LN_EPS = 1e-5
NEG = -1e30
DEPTH = 2
DEEPNORM_ALPHA = (2 * DEPTH) ** 0.25

LANES = 128
EVEN_IN_PAD = 3200
INT_MIN = -(2 ** 31)
DIL_CHUNK = BAND_BLOCK * max(d for _, d in DIL_GROUPS)

_PARALLEL2 = pltpu.CompilerParams(dimension_semantics=("parallel", "parallel"))


def _layer_norm(y, g, b):
    mu = jnp.mean(y, axis=-1, keepdims=True)
    yc = y - mu
    var = jnp.mean(yc * yc, axis=-1, keepdims=True)
    return yc * lax.rsqrt(var + LN_EPS) * g + b


def _sigmoid(x):
    return 1.0 / (1.0 + jnp.exp(-x))


def _dot_nt(a, b):
    return lax.dot_general(a, b, (((1,), (1,)), ((), ())), preferred_element_type=F32)


def _dot(a, b):
    return jnp.dot(a, b, preferred_element_type=F32)


def _mm_kernel(x_ref, w_ref, o_ref):
    o_ref[...] = _dot(x_ref[...].astype(BF16), w_ref[...]).astype(o_ref.dtype)


def _matmul(x, w, tm, tn, out_dtype):
    t, k = x.shape
    n = w.shape[1]
    return pl.pallas_call(
        _mm_kernel,
        grid=(t // tm, n // tn),
        in_specs=[pl.BlockSpec((tm, k), lambda i, j: (i, 0)),
                  pl.BlockSpec((k, tn), lambda i, j: (0, j))],
        out_specs=pl.BlockSpec((tm, tn), lambda i, j: (i, j)),
        out_shape=jax.ShapeDtypeStruct((t, n), out_dtype),
        compiler_params=_PARALLEL2,
        name="matmul",
    )(x, w)


_MAX_ROW_STRIDE = 4


def _mm_perm_kernel(x_ref, w_ref, o_ref, acc_ref, tmp_ref, *, dil, tm, tn):
    acc = _dot(x_ref[...].astype(BF16), w_ref[...])
    for c in range(tn // LANES):
        acc_ref[c] = acc[:, c * LANES:(c + 1) * LANES]
    if dil <= _MAX_ROW_STRIDE:
        rows = tm // dil
        for r in range(dil):
            for c in range(tn // LANES):
                o_ref[0, r, :, c * LANES:(c + 1) * LANES] = (
                    acc_ref[c, pl.ds(r, rows, stride=dil), :].astype(o_ref.dtype))
        return
    s1 = _MAX_ROW_STRIDE
    s2 = dil // s1
    assert s2 <= _MAX_ROW_STRIDE
    rows1 = tm // s1
    rows = tm // dil
    for c in range(tn // LANES):
        for r1 in range(s1):
            tmp_ref[c, r1 * rows1:(r1 + 1) * rows1, :] = acc_ref[c, pl.ds(r1, rows1, stride=s1), :]
    for r1 in range(s1):
        for r2 in range(s2):
            for c in range(tn // LANES):
                o_ref[0, s1 * r2 + r1, :, c * LANES:(c + 1) * LANES] = (
                    tmp_ref[c, pl.ds(r1 * rows1 + r2, rows, stride=s2), :].astype(o_ref.dtype))


def _matmul_residue_major(x, w, batch, seq, dil, tm, tn):
    t, k = x.shape
    n = w.shape[1]
    if dil == 1:
        return _matmul(x, w, tm, tn, BF16).reshape(batch, seq, n)
    tiles_per_seq = seq // tm
    out = pl.pallas_call(
        functools.partial(_mm_perm_kernel, dil=dil, tm=tm, tn=tn),
        grid=(t // tm, n // tn),
        in_specs=[pl.BlockSpec((tm, k), lambda i, j: (i, 0)),
                  pl.BlockSpec((k, tn), lambda i, j: (0, j))],
        out_specs=pl.BlockSpec((1, dil, tm // dil, tn),
                               lambda i, j: (i // tiles_per_seq, 0, i % tiles_per_seq, j)),
        out_shape=jax.ShapeDtypeStruct((batch, dil, seq // dil, n), BF16),
        scratch_shapes=[pltpu.VMEM((tn // LANES, tm, LANES), F32), pltpu.VMEM((tn // LANES, tm, LANES), F32)],
        compiler_params=_PARALLEL2,
        name="matmul_residue_major",
    )(x, w)
    return out.reshape(batch, seq, n)


_CONV_HALO = 32
_CONV_ROWS = 64
_SUBLANES = 8


def _conv_kernel(val_ref, gate_ref, w_ref, cb_ref, g_ref, b_ref, o_ref, ext_ref, *, ts):
    s = pl.program_id(1)

    @pl.when(s == 0)
    def _():
        ext_ref[0, 0:_CONV_HALO, :] = jnp.zeros((_CONV_HALO, CONV_CH), F32)

    val = val_ref[0].astype(F32)
    gate = gate_ref[0].astype(F32)
    ext_ref[0, _CONV_HALO:_CONV_HALO + ts, :] = val * _sigmoid(gate)
    span = ts + _CONV_HALO - _SUBLANES
    for k in range(1, _SUBLANES):
        ext_ref[k, 0:span, :] = ext_ref[0, k:k + span, :]
    first_tap = _CONV_HALO - (CONV_WIDTH - 1)
    for rc in range(ts // _CONV_ROWS):
        r0 = rc * _CONV_ROWS
        acc = jnp.zeros((_CONV_ROWS, CONV_CH), F32)
        for j in range(CONV_WIDTH):
            shift = (first_tap + j) % _SUBLANES
            lo = r0 + first_tap + j - shift
            acc = acc + ext_ref[shift, lo:lo + _CONV_ROWS, :] * w_ref[j:j + 1, :]
        y = _layer_norm(acc + cb_ref[...], g_ref[...], b_ref[...])
        o_ref[0, r0:r0 + _CONV_ROWS, :] = (y * _sigmoid(y)).astype(o_ref.dtype)
    ext_ref[0, 0:_CONV_HALO, :] = ext_ref[0, ts:ts + _CONV_HALO, :]


def _conformer_conv(h, conv_w, conv_b, ln_g, ln_b, ts=512):
    b, s, _ = h.shape
    w_pad = jnp.zeros((32, CONV_CH), F32).at[:CONV_WIDTH].set(conv_w)
    vec = lambda v: v.reshape(1, CONV_CH).astype(F32)
    full = lambda shape: pl.BlockSpec(shape, lambda bi, si: (0, 0))
    return pl.pallas_call(
        functools.partial(_conv_kernel, ts=ts),
        grid=(b, s // ts),
        in_specs=[pl.BlockSpec((1, ts, CONV_CH), lambda bi, si: (bi, si, 0)),
                  pl.BlockSpec((1, ts, CONV_CH), lambda bi, si: (bi, si, 1)),
                  full((32, CONV_CH)), full((1, CONV_CH)), full((1, CONV_CH)), full((1, CONV_CH))],
        out_specs=pl.BlockSpec((1, ts, CONV_CH), lambda bi, si: (bi, si, 0)),
        out_shape=jax.ShapeDtypeStruct((b, s, CONV_CH), BF16),
        scratch_shapes=[pltpu.VMEM((_SUBLANES, ts + _CONV_HALO, CONV_CH), F32)],
        compiler_params=pltpu.CompilerParams(dimension_semantics=("parallel", "arbitrary")),
        name="conformer_conv",
    )(h, h, w_pad, vec(conv_b), vec(ln_g), vec(ln_b))


def _rel_bucket(dist):
    max_exact = NUM_BUCKETS // 2
    n = dist.astype(jnp.int32)
    nf = jnp.maximum(n, 1).astype(F32)
    large = max_exact + (jnp.log(nf / max_exact) / math.log(MAX_DISTANCE / max_exact)
                         * (NUM_BUCKETS - max_exact)).astype(jnp.int32)
    large = jnp.minimum(large, NUM_BUCKETS - 1)
    return jnp.where(n < max_exact, n, large)


def _bias_by_distance(rel_bias, n):
    return rel_bias[_rel_bucket(jnp.arange(n, dtype=jnp.int32))].astype(F32).T


def _toeplitz_kernel(win_ref, o_ref):
    n_win, span = win_ref.shape
    for n in range(n_win):
        y = jnp.broadcast_to(win_ref[n:n + 1, :], (LANES, span))
        y = pltpu.roll(y, 0, 1, stride=1, stride_axis=0)
        o_ref[n] = y[:, LANES:]


def _toeplitz_tiles(windows, per_step=8):
    n, span = windows.shape
    width = span - LANES
    return pl.pallas_call(
        _toeplitz_kernel,
        grid=(n // per_step,),
        in_specs=[pl.BlockSpec((per_step, span), lambda i: (i, 0))],
        out_specs=pl.BlockSpec((per_step, LANES, width), lambda i: (i, 0, 0)),
        out_shape=jax.ShapeDtypeStruct((n, LANES, width), F32),
        compiler_params=pltpu.CompilerParams(dimension_semantics=("parallel",)),
        name="bias_toeplitz",
    )(windows)


def _dsa_bias_tiles(rel_bias, seq):
    nblk = seq // Q_BLOCK
    heads = rel_bias.shape[1]
    f = _bias_by_distance(rel_bias, seq)
    u = jnp.concatenate([jnp.broadcast_to(f[:, :1], (heads, 2 * Q_BLOCK)), f], axis=1)
    blocks = u.reshape(heads, nblk + 2, Q_BLOCK)
    windows = jnp.concatenate([blocks[:, :-1], blocks[:, 1:]], axis=-1)
    windows = windows.transpose(1, 0, 2).reshape((nblk + 1) * heads, 2 * Q_BLOCK)
    return _toeplitz_tiles(windows).reshape(nblk + 1, heads, Q_BLOCK, Q_BLOCK)


def _dil_bias_tiles(rel_bias):
    max_dil = max(d for _, d in DIL_GROUPS)
    heads = rel_bias.shape[1]
    span = 2 * BAND_BLOCK
    f = _bias_by_distance(rel_bias, span * max_dil + 1)
    windows = []
    for _, dil in DIL_GROUPS:
        fd = f[:, 0:span * dil + 1:dil]
        tail = jnp.broadcast_to(fd[:, :1], (heads, BAND_BLOCK - 1))
        windows.append(jnp.concatenate([fd[:, ::-1], tail], axis=1))
    tiles = _toeplitz_tiles(jnp.concatenate(windows, axis=0))
    return tiles.reshape(len(DIL_GROUPS), heads, BAND_BLOCK, span)


_KC = 2 * Q_BLOCK
_VT_ROWS = DSA_HEAD_DIM + 16


def _sortable_key(score):
    score = jnp.where(score == 0.0, 0.0, score)
    bits = lax.bitcast_convert_type(score, jnp.int32)
    return jnp.where(bits < 0, bits ^ jnp.int32(0x7FFFFFFF), bits)


def _rows_reduce(x, op):
    rows = x.shape[0]
    return op(op(x.reshape(rows // 8, 8, LANES), axis=0), axis=0, keepdims=True)


def _keys_reduce(x, op):
    h, keys, _ = x.shape
    return op(op(x.reshape(h, keys // 8, 8, LANES), axis=1), axis=1, keepdims=True)


def _dsa_t_kernel(q_ref, qi_ref, wq_ref, k_ref, vt_ref, ki_ref, bias_hbm, o_ref,
                  bias_ref, keys_ref, qm_ref, acc_ref, dots_a_ref, dots_b_ref, sem, *, topk):
    b = pl.program_id(0)
    i = pl.program_id(1)

    @pl.when((b == 0) & (i == 0))
    def _():
        cp = pltpu.make_async_copy(bias_hbm, bias_ref, sem)
        cp.start()
        cp.wait()

    nch = (i + 2) // 2
    key_off = lax.broadcasted_iota(jnp.int32, (_KC, LANES), 0)
    q_pos = i * Q_BLOCK + lax.broadcasted_iota(jnp.int32, (_KC, LANES), 1)
    lane = lax.broadcasted_iota(jnp.int32, (Q_BLOCK, LANES), 1)
    low_half = lane < DSA_HEAD_DIM

    w_t = (wq_ref[0].astype(F32) * (IDX_HEADS ** -0.5 * IDX_DIM ** -0.5)).T
    qi = qi_ref[0].astype(F32)
    qi_heads = []
    for p in range(IDX_HEADS // 2):
        pair = qi[:, p * LANES:(p + 1) * LANES]
        qi_heads.append(jnp.where(low_half, pair, 0.0))
        qi_heads.append(jnp.where(low_half, pltpu.roll(pair, IDX_DIM, 1), 0.0))
    qi_stack = jnp.concatenate(qi_heads, axis=0).astype(BF16)
    q_scaled = (q_ref[0].astype(F32) * (DSA_HEAD_DIM ** -0.5)).astype(BF16)
    for p in range(DSA_HEADS // 2):
        qp = q_scaled[:, p * LANES:(p + 1) * LANES]
        qm_ref[2 * p] = jnp.where(low_half, qp, jnp.zeros_like(qp))
        qm_ref[2 * p + 1] = jnp.where(low_half, jnp.zeros_like(qp), qp)

    def index_dots(c, dst_ref):
        c = jnp.minimum(c, nch - 1)
        k0 = pl.multiple_of(c * _KC, _KC)
        ki = ki_ref[0, pl.ds(k0, _KC), :]
        dst_ref[...] = _dot_nt(ki, qi_stack)

    def consume_dots(c, src_ref):
        k0 = c * _KC
        score = jnp.zeros((_KC, LANES), F32)
        for h in range(IDX_HEADS):
            w_row = w_t[IDX_DIM + h:IDX_DIM + h + 1, :]
            score = score + w_row * jnp.maximum(src_ref[:, h * Q_BLOCK:(h + 1) * Q_BLOCK], 0.0)
        keys_ref[c] = jnp.where(k0 + key_off <= q_pos, _sortable_key(score), jnp.int32(INT_MIN))

    index_dots(0, dots_a_ref)

    def score_body(c2, carry):
        index_dots(2 * c2 + 1, dots_b_ref)
        consume_dots(2 * c2, dots_a_ref)
        index_dots(2 * c2 + 2, dots_a_ref)
        consume_dots(2 * c2 + 1, dots_b_ref)
        return carry

    lax.fori_loop(0, (nch + 1) // 2, score_body, 0)

    def count_ge(cand):
        cb = jnp.broadcast_to(cand, (8, LANES))
        chains = 4

        def body(c2, acc):
            for half in range(2):
                kk = keys_ref[2 * c2 + half].reshape(_KC // (8 * chains), chains, 8, LANES)
                acc = acc + jnp.sum(jnp.where(kk >= cb, 1.0, 0.0), axis=0)
            return acc

        acc = lax.fori_loop(0, (nch + 1) // 2, body, jnp.zeros((chains, 8, LANES), F32))
        return jnp.sum(jnp.sum(acc, axis=0), axis=0, keepdims=True)

    kf = float(topk)
    zero = jnp.zeros((1, LANES), jnp.int32)
    thr0 = jnp.where(count_ge(zero) >= kf, zero, jnp.full((1, LANES), INT_MIN, jnp.int32))

    def bit_body(t, thr):
        cand = thr | jnp.left_shift(jnp.int32(1), 30 - t)
        return jnp.where(count_ge(cand) >= kf, cand, thr)

    thr = lax.fori_loop(0, 31, bit_body, thr0)
    n_gt = count_ge(jnp.where(thr == jnp.int32(2 ** 31 - 1), thr, thr + 1))
    need = kf - n_gt
    tri = (lax.broadcasted_iota(jnp.int32, (_KC, _KC), 1)
           <= lax.broadcasted_iota(jnp.int32, (_KC, _KC), 0)).astype(BF16)

    acc_ref[...] = jnp.zeros_like(acc_ref)

    def attn_body(c, carry):
        ms, tie_carry = carry
        k0 = pl.multiple_of(c * _KC, _KC)
        kk = keys_ref[c]
        causal = kk > jnp.int32(INT_MIN)
        eq = (kk == thr) & causal
        eq_f = jnp.where(eq, 1.0, 0.0)
        rank = _dot(tri, eq_f.astype(BF16)) + tie_carry
        sel = ((kk > thr) & causal) | (eq & (rank <= need))
        tie_carry = tie_carry + _rows_reduce(eq_f, jnp.sum)
        d = i - 2 * c
        kc = k_ref[0, pl.ds(k0, _KC), :]
        k_heads = jnp.stack([kc[:, (h // 2) * LANES:(h // 2 + 1) * LANES] for h in range(DSA_HEADS)])
        s = jnp.einsum("hkd,hqd->hkq", k_heads, qm_ref[...], preferred_element_type=F32)
        bias = jnp.concatenate([bias_ref[d + 1], bias_ref[d]], axis=1)
        s = jnp.where(sel, s + bias, NEG)
        m_new = jnp.maximum(ms, _keys_reduce(s, jnp.max))
        alpha = jnp.exp(ms - m_new)
        pr = jnp.exp((s - m_new).astype(BF16))
        pv = jnp.einsum("hdk,hkq->hdq", vt_ref[0, c], pr, preferred_element_type=F32)
        acc_ref[...] = alpha * acc_ref[...] + pv
        return m_new, tie_carry

    init = (jnp.full((DSA_HEADS, 1, LANES), NEG, F32), jnp.zeros((1, LANES), F32))
    lax.fori_loop(0, nch, attn_body, init)
    acc = acc_ref[...]
    out_t = acc[:, :DSA_HEAD_DIM] / acc[:, DSA_HEAD_DIM:DSA_HEAD_DIM + 1]
    o_ref[0] = out_t.reshape(DSA_HEADS * DSA_HEAD_DIM, Q_BLOCK).T.astype(o_ref.dtype)


def _dsa_attention(h, bias_tiles):
    b, s, _ = h.shape
    width = DSA_HEADS * DSA_HEAD_DIM
    nblk = s // Q_BLOCK
    topk = min(DSA_TOPK_MAX, s // 4)
    kw_block = (CONV_CH * 2 + 4 * width) // LANES
    v_cols = 2 * CONV_CH + 2 * width
    vt = h[:, :, v_cols:v_cols + width].reshape(b, s // _KC, _KC, DSA_HEADS, DSA_HEAD_DIM).transpose(0, 1, 3, 4, 2)
    vt = jnp.concatenate([vt, jnp.ones((b, s // _KC, DSA_HEADS, _VT_ROWS - DSA_HEAD_DIM, _KC), BF16)], axis=3)
    return pl.pallas_call(
        functools.partial(_dsa_t_kernel, topk=topk),
        grid=(b, nblk),
        in_specs=[pl.BlockSpec((1, Q_BLOCK, width), lambda bi, qi: (bi, qi, 2)),
                  pl.BlockSpec((1, Q_BLOCK, width), lambda bi, qi: (bi, qi, 5)),
                  pl.BlockSpec((1, Q_BLOCK, LANES), lambda bi, qi: (bi, qi, kw_block)),
                  pl.BlockSpec((1, s, width), lambda bi, qi: (bi, 0, 3)),
                  pl.BlockSpec((1, s // _KC, DSA_HEADS, _VT_ROWS, _KC), lambda bi, qi: (bi, 0, 0, 0, 0)),
                  pl.BlockSpec((1, s, LANES), lambda bi, qi: (bi, 0, kw_block)),
                  pl.BlockSpec(memory_space=pl.ANY)],
        out_specs=pl.BlockSpec((1, Q_BLOCK, width), lambda bi, qi: (bi, qi, 0)),
        out_shape=jax.ShapeDtypeStruct((b, s, width), BF16),
        scratch_shapes=[pltpu.VMEM((nblk + 1, DSA_HEADS, Q_BLOCK, Q_BLOCK), F32),
                        pltpu.VMEM((s // _KC, _KC, Q_BLOCK), jnp.int32),
                        pltpu.VMEM((DSA_HEADS, Q_BLOCK, LANES), BF16),
                        pltpu.VMEM((DSA_HEADS, _VT_ROWS, Q_BLOCK), F32),
                        pltpu.VMEM((_KC, IDX_HEADS * Q_BLOCK), F32),
                        pltpu.VMEM((_KC, IDX_HEADS * Q_BLOCK), F32),
                        pltpu.SemaphoreType.DMA(())],
        compiler_params=pltpu.CompilerParams(dimension_semantics=("arbitrary", "arbitrary")),
        name="dsa_attention",
    )(h, h, h, h, vt, h, bias_tiles)


_DIL_WAVE = 16


def _dil_kernel(*refs, seq):
    n_g = len(DIL_GROUPS)
    qkv = refs[:3 * n_g]
    bias_ref, o_ref, og_ref, lg_ref = refs[3 * n_g:]
    c = pl.program_id(2)
    ii = lax.broadcasted_iota(jnp.int32, (BAND_BLOCK, BAND_BLOCK), 0)
    jj = lax.broadcasted_iota(jnp.int32, (BAND_BLOCK, BAND_BLOCK), 1)
    valid_prev = jj >= ii
    valid_cur = ii >= jj
    scale = DIL_HEAD_DIM ** -0.5
    for g, (window, dil) in enumerate(DIL_GROUPS):
        assert window // dil == BAND_BLOCK
        q_ref, k_ref, v_ref = qkv[3 * g:3 * g + 3]
        sub_len = seq // dil
        blocks_per_chunk = DIL_CHUNK // (BAND_BLOCK * dil)
        bias_prev = bias_ref[g, 0, :, :BAND_BLOCK]
        bias_cur = bias_ref[g, 0, :, BAND_BLOCK:]
        tiles = [(r, nl) for r in range(dil) for nl in range(blocks_per_chunk)]
        for w0 in range(0, len(tiles), _DIL_WAVE):
            wave = tiles[w0:w0 + _DIL_WAVE]
            cur, prev, mask_prev = [], [], []
            for r, nl in wave:
                row0 = pl.multiple_of(r * sub_len + c * (DIL_CHUNK // dil) + nl * BAND_BLOCK, BAND_BLOCK)
                cur.append(pl.ds(row0, BAND_BLOCK))
                prev.append(pl.ds(pl.multiple_of(jnp.maximum(row0 - BAND_BLOCK, 0), BAND_BLOCK), BAND_BLOCK))
                mask_prev.append(valid_prev if nl > 0 else (valid_prev & (c > 0)))
            gather = lambda ref, slices: jnp.stack([ref[0, sl, :] for sl in slices])
            qb = gather(q_ref, cur)
            s_cur = jnp.einsum("wqd,wkd->wqk", qb, gather(k_ref, cur), preferred_element_type=F32)
            s_prev = jnp.einsum("wqd,wkd->wqk", qb, gather(k_ref, prev), preferred_element_type=F32)
            s_cur = jnp.where(valid_cur, s_cur * scale + bias_cur, NEG)
            s_prev = jnp.where(jnp.stack(mask_prev), s_prev * scale + bias_prev, NEG)
            m = jnp.max(jnp.maximum(s_cur, s_prev), axis=2, keepdims=True)
            p_cur = jnp.exp(s_cur - m)
            p_prev = jnp.exp(s_prev - m)
            ssum = jnp.sum(p_cur + p_prev, axis=2, keepdims=True)
            o = (jnp.einsum("wqk,wkd->wqd", p_cur.astype(BF16), gather(v_ref, cur), preferred_element_type=F32)
                 + jnp.einsum("wqk,wkd->wqd", p_prev.astype(BF16), gather(v_ref, prev),
                              preferred_element_type=F32)) / ssum
            lse = m + jnp.log(ssum)
            for w, (r, nl) in enumerate(wave):
                dst = pl.ds(r + nl * BAND_BLOCK * dil, BAND_BLOCK, stride=dil) if dil > 1 else pl.ds(
                    nl * BAND_BLOCK, BAND_BLOCK)
                og_ref[g, dst, :] = o[w]
                lg_ref[g, dst, :] = jnp.broadcast_to(lse[w], (BAND_BLOCK, LANES))
    rows = 256
    for rc in range(DIL_CHUNK // rows):
        sl = slice(rc * rows, (rc + 1) * rows)
        lses = [lg_ref[g, sl, :] for g in range(n_g)]
        mx = functools.reduce(jnp.maximum, lses)
        ws = [jnp.exp(l - mx) for l in lses]
        den = functools.reduce(lambda a, b_: a + b_, ws)
        num = functools.reduce(lambda a, b_: a + b_, [ws[g] * og_ref[g, sl, :] for g in range(n_g)])
        o_ref[0, sl, :] = (num / den).astype(o_ref.dtype)


def _dilated_attention(hs, bias_tiles):
    b, s, _ = hs[0].shape
    in_specs = []
    args = []
    for hg in hs:
        for part in range(3):
            in_specs.append(pl.BlockSpec((1, s, DIL_HEAD_DIM),
                                         lambda bi, hi, ci, part=part: (bi, 0, part * DIL_HEADS + hi)))
            args.append(hg)
    in_specs.append(pl.BlockSpec((len(DIL_GROUPS), 1, BAND_BLOCK, 2 * BAND_BLOCK),
                                 lambda bi, hi, ci: (0, hi, 0, 0)))
    return pl.pallas_call(
        functools.partial(_dil_kernel, seq=s),
        grid=(b, DIL_HEADS, s // DIL_CHUNK),
        in_specs=in_specs,
        out_specs=pl.BlockSpec((1, DIL_CHUNK, DIL_HEAD_DIM), lambda bi, hi, ci: (bi, ci, hi)),
        out_shape=jax.ShapeDtypeStruct((b, s, DIL_HEADS * DIL_HEAD_DIM), BF16),
        scratch_shapes=[pltpu.VMEM((len(DIL_GROUPS), DIL_CHUNK, LANES), F32),
                        pltpu.VMEM((len(DIL_GROUPS), DIL_CHUNK, LANES), F32)],
        compiler_params=pltpu.CompilerParams(dimension_semantics=("parallel", "parallel", "arbitrary")),
        name="dilated_attention",
    )(*args, bias_tiles)


def _outproj_kernel(*refs, n_in, with_router):
    ins = refs[:n_in]
    ws = refs[n_in:2 * n_in]
    x_ref, g_ref, b_ref = refs[2 * n_in:2 * n_in + 3]
    rest = refs[2 * n_in + 3:]
    mix = _dot(ins[0][...], ws[0][...])
    for a_ref, w_ref in zip(ins[1:], ws[1:]):
        mix = mix + _dot(a_ref[...], w_ref[...])
    xn = _layer_norm(DEEPNORM_ALPHA * x_ref[...] + mix, g_ref[...], b_ref[...])
    if not with_router:
        o_ref, ob_ref = rest
    else:
        router_ref, o_ref, gates_ref, counts_ref, run_ref, tri_ref = rest
        ob_ref = None
        tm = xn.shape[0]

        @pl.when(pl.program_id(0) == 0)
        def _():
            run_ref[...] = jnp.zeros_like(run_ref)
            tri_ref[...] = (lax.broadcasted_iota(jnp.int32, (tm, tm), 0)
                            > lax.broadcasted_iota(jnp.int32, (tm, tm), 1)).astype(BF16)

        logits = _dot(xn.astype(BF16), router_ref[...])
        lane = lax.broadcasted_iota(jnp.int32, logits.shape, 1).astype(F32)
        lg = jnp.where(lane < N_EXPERTS, logits, -jnp.inf)
        m1 = jnp.max(lg, axis=1, keepdims=True)
        i1 = jnp.min(jnp.where(lg == m1, lane, float(LANES)), axis=1, keepdims=True)
        lg2 = jnp.where(lane == i1, -jnp.inf, lg)
        m2 = jnp.max(lg2, axis=1, keepdims=True)
        i2 = jnp.min(jnp.where(lg2 == m2, lane, float(LANES)), axis=1, keepdims=True)
        e2 = jnp.exp(m2 - m1)
        den = 1.0 + e2
        oh1 = jnp.where(lane == i1, 1.0, 0.0)
        oh2 = jnp.where(lane == i2, 1.0, 0.0)
        both = oh1 + oh2
        before = _dot(tri_ref[...], both.astype(BF16)) + run_ref[...]
        r1 = jnp.sum(before * oh1, axis=1, keepdims=True)
        r2 = jnp.sum(before * oh2, axis=1, keepdims=True)
        run_ref[...] += jnp.sum(both, axis=0, keepdims=True)
        counts_ref[...] = run_ref[...]
        gates_ref[...] = (jnp.where(lane == 0.0, i1, 0.0) + jnp.where(lane == 1.0, i2, 0.0)
                          + jnp.where(lane == 2.0, 1.0 / den, 0.0) + jnp.where(lane == 3.0, e2 / den, 0.0)
                          + jnp.where(lane == 4.0, r1, 0.0) + jnp.where(lane == 5.0, r2, 0.0))
    o_ref[...] = xn
    if ob_ref is not None:
        ob_ref[...] = xn.astype(BF16)


def _outproj_ln(ins, ws, x, ln_g, ln_b, router=None, tm=512):
    t, d = x.shape
    n_in = len(ins)
    row = lambda width: pl.BlockSpec((tm, width), lambda i: (i, 0))
    full = lambda a: pl.BlockSpec(a.shape, lambda i: (0, 0))
    vec = lambda v: v.reshape(1, d).astype(F32)
    args = list(ins) + list(ws) + [x, vec(ln_g), vec(ln_b)]
    in_specs = [row(a.shape[1]) for a in ins] + [full(w) for w in ws] + [row(d), full(vec(ln_g)), full(vec(ln_b))]
    out_shape = [jax.ShapeDtypeStruct((t, d), F32), jax.ShapeDtypeStruct((t, d), BF16)]
    out_specs = [row(d), row(d)]
    if router is not None:
        router_pad = jnp.zeros((d, LANES), BF16).at[:, :N_EXPERTS].set(router.astype(BF16))
        args.append(router_pad)
        in_specs.append(full(router_pad))
        out_shape = [out_shape[0], jax.ShapeDtypeStruct((t, LANES), F32), jax.ShapeDtypeStruct((1, LANES), F32)]
        out_specs = [row(d), row(LANES), pl.BlockSpec((1, LANES), lambda i: (0, 0))]
        scratch = [pltpu.VMEM((1, LANES), F32), pltpu.VMEM((tm, tm), BF16)]
    else:
        scratch = []
    semantics = "arbitrary" if router is not None else "parallel"
    return pl.pallas_call(
        functools.partial(_outproj_kernel, n_in=n_in, with_router=router is not None),
        grid=(t // tm,),
        in_specs=in_specs,
        out_specs=out_specs,
        out_shape=out_shape,
        scratch_shapes=scratch,
        compiler_params=pltpu.CompilerParams(dimension_semantics=(semantics,)),
        name="outproj_ln",
    )(*args)


def _ffn_kernel(xb_ref, x_ref, wg_ref, wu_ref, wd_ref, g_ref, b_ref, o_ref, ob_ref, acc_ref):
    f = pl.program_id(1)

    @pl.when(f == 0)
    def _():
        acc_ref[...] = jnp.zeros_like(acc_ref)

    xb = xb_ref[...]
    hg = _dot(xb, wg_ref[...])
    hu = _dot(xb, wu_ref[...])
    act = (hg * _sigmoid(hg) * hu).astype(BF16)
    acc_ref[...] += _dot(act, wd_ref[...])

    @pl.when(f == pl.num_programs(1) - 1)
    def _():
        xn = _layer_norm(DEEPNORM_ALPHA * x_ref[...] + acc_ref[...], g_ref[...], b_ref[...])
        o_ref[...] = xn
        ob_ref[...] = xn.astype(BF16)


def _ffn_ln(xb, x, wg, wu, wd, ln_g, ln_b, tm=512, tf=1408):
    t, d = x.shape
    ff = wg.shape[1]
    vec = lambda v: v.reshape(1, d).astype(F32)
    return pl.pallas_call(
        _ffn_kernel,
        grid=(t // tm, ff // tf),
        in_specs=[pl.BlockSpec((tm, d), lambda i, f: (i, 0)),
                  pl.BlockSpec((tm, d), lambda i, f: (i, 0)),
                  pl.BlockSpec((d, tf), lambda i, f: (0, f)),
                  pl.BlockSpec((d, tf), lambda i, f: (0, f)),
                  pl.BlockSpec((tf, d), lambda i, f: (f, 0)),
                  pl.BlockSpec((1, d), lambda i, f: (0, 0)),
                  pl.BlockSpec((1, d), lambda i, f: (0, 0))],
        out_specs=[pl.BlockSpec((tm, d), lambda i, f: (i, 0)),
                   pl.BlockSpec((tm, d), lambda i, f: (i, 0))],
        out_shape=[jax.ShapeDtypeStruct((t, d), F32), jax.ShapeDtypeStruct((t, d), BF16)],
        scratch_shapes=[pltpu.VMEM((tm, d), F32)],
        compiler_params=pltpu.CompilerParams(dimension_semantics=("parallel", "arbitrary")),
        name="ffn_ln",
    )(xb, x, wg, wu, wd, vec(ln_g), vec(ln_b))


_MOE_TM = 1024


def _route(route, counts, n_tiles):
    expert = route[:, 0:2].astype(jnp.int32)
    rank = route[:, 4:6].astype(jnp.int32)
    counts = counts[0, :N_EXPERTS].astype(jnp.int32)
    padded = (counts + _MOE_TM - 1) // _MOE_TM * _MOE_TM
    ends = jnp.cumsum(padded)
    starts = ends - padded
    onehot = expert[:, :, None] == jnp.arange(N_EXPERTS, dtype=jnp.int32)
    pos = (rank + jnp.sum(jnp.where(onehot, starts, 0), axis=-1)).reshape(-1)
    tile_start = jnp.arange(n_tiles, dtype=jnp.int32) * _MOE_TM
    tile_expert = jnp.minimum(jnp.sum((tile_start[:, None] >= ends[None, :]).astype(jnp.int32), axis=1),
                              N_EXPERTS - 1)
    return pos.astype(jnp.int32), tile_expert.astype(jnp.int32)


def _row_scatter_kernel(pos_ref, x_ref, init_ref, o_ref, sem, *, tm):
    del init_ref

    def body(j, carry):
        for choice in range(2):
            p = pos_ref[0, 0, 2 * j + choice]
            pltpu.make_async_copy(x_ref.at[pl.ds(j, 1)], o_ref.at[pl.ds(p, 1)], sem).start()
        return carry

    lax.fori_loop(0, tm, body, 0, unroll=8)
    for _ in range(2):
        pltpu.make_async_copy(x_ref, o_ref.at[pl.ds(0, tm)], sem).wait()


def _row_scatter(x, pos, init, tm=512):
    t, d = x.shape
    n_rows = init.shape[0]
    return pl.pallas_call(
        functools.partial(_row_scatter_kernel, tm=tm),
        grid=(t // tm,),
        in_specs=[pl.BlockSpec((1, 1, 2 * tm), lambda i: (i, 0, 0), memory_space=pltpu.SMEM),
                  pl.BlockSpec((tm, d), lambda i: (i, 0)),
                  pl.BlockSpec(memory_space=pl.ANY)],
        out_specs=pl.BlockSpec(memory_space=pl.ANY),
        out_shape=jax.ShapeDtypeStruct((n_rows, d), x.dtype),
        scratch_shapes=[pltpu.SemaphoreType.DMA(())],
        input_output_aliases={2: 0},
        compiler_params=pltpu.CompilerParams(dimension_semantics=("arbitrary",)),
        name="moe_row_scatter",
    )(pos.reshape(t // tm, 1, 2 * tm), x, init)


def _expert_ffn_kernel(te_ref, x_ref, wg_ref, wu_ref, wd_ref, o_ref, acc_ref):
    del te_ref
    f = pl.program_id(1)

    @pl.when(f == 0)
    def _():
        acc_ref[...] = jnp.zeros_like(acc_ref)

    xb = x_ref[...].astype(BF16)
    hg = _dot(xb, wg_ref[0].astype(BF16))
    hu = _dot(xb, wu_ref[0].astype(BF16))
    act = (hg * _sigmoid(hg) * hu).astype(BF16)
    acc_ref[...] += _dot(act, wd_ref[0].astype(BF16))

    @pl.when(f == pl.num_programs(1) - 1)
    def _():
        o_ref[...] = acc_ref[...]


def _expert_ffn(xs, tile_expert, wg, wu, wd, tf=512):
    n_rows, d = xs.shape
    ff = wg.shape[2]
    grid_spec = pltpu.PrefetchScalarGridSpec(
        num_scalar_prefetch=1,
        grid=(n_rows // _MOE_TM, ff // tf),
        in_specs=[pl.BlockSpec((_MOE_TM, d), lambda i, f, te: (i, 0)),
                  pl.BlockSpec((1, d, tf), lambda i, f, te: (te[i], 0, f)),
                  pl.BlockSpec((1, d, tf), lambda i, f, te: (te[i], 0, f)),
                  pl.BlockSpec((1, tf, d), lambda i, f, te: (te[i], f, 0))],
        out_specs=pl.BlockSpec((_MOE_TM, d), lambda i, f, te: (i, 0)),
        scratch_shapes=[pltpu.VMEM((_MOE_TM, d), F32)])
    return pl.pallas_call(
        _expert_ffn_kernel,
        grid_spec=grid_spec,
        out_shape=jax.ShapeDtypeStruct((n_rows, d), F32),
        compiler_params=pltpu.CompilerParams(dimension_semantics=("parallel", "arbitrary")),
        name="moe_expert_ffn",
    )(tile_expert, xs, wg, wu, wd)


def _combine_kernel(pos_ref, pos_next_ref, route_ref, x_ref, g_ref, b_ref, y_ref, o_ref, ybuf_ref, sems, *, tm):
    i = pl.program_id(0)
    n = pl.num_programs(0)

    def gather(src_pos_ref, slot):
        def body(j, carry):
            for choice in range(2):
                p = src_pos_ref[0, 0, 2 * j + choice]
                pltpu.make_async_copy(y_ref.at[pl.ds(p, 1)], ybuf_ref.at[slot, choice, pl.ds(j, 1)],
                                      sems.at[slot]).start()
            return carry

        lax.fori_loop(0, tm, body, 0, unroll=8)

    @pl.when(i == 0)
    def _():
        gather(pos_ref, 0)

    slot = i % 2

    @pl.when(i + 1 < n)
    def _():
        gather(pos_next_ref, 1 - slot)

    for choice in range(2):
        pltpu.make_async_copy(y_ref.at[pl.ds(0, tm)], ybuf_ref.at[slot, choice], sems.at[slot]).wait()
    route = route_ref[...]
    y = route[:, 2:3] * ybuf_ref[slot, 0] + route[:, 3:4] * ybuf_ref[slot, 1]
    o_ref[...] = _layer_norm(DEEPNORM_ALPHA * x_ref[...] + y, g_ref[...], b_ref[...])


def _combine_ln(ys, pos, route, x, ln_g, ln_b, tm=512):
    t, d = x.shape
    vec = lambda v: v.reshape(1, d).astype(F32)
    last = t // tm - 1
    pos3 = pos.reshape(t // tm, 1, 2 * tm)
    return pl.pallas_call(
        functools.partial(_combine_kernel, tm=tm),
        grid=(t // tm,),
        in_specs=[pl.BlockSpec((1, 1, 2 * tm), lambda i: (i, 0, 0), memory_space=pltpu.SMEM),
                  pl.BlockSpec((1, 1, 2 * tm), lambda i: (jnp.minimum(i + 1, last), 0, 0), memory_space=pltpu.SMEM),
                  pl.BlockSpec((tm, LANES), lambda i: (i, 0)),
                  pl.BlockSpec((tm, d), lambda i: (i, 0)),
                  pl.BlockSpec((1, d), lambda i: (0, 0)),
                  pl.BlockSpec((1, d), lambda i: (0, 0)),
                  pl.BlockSpec(memory_space=pl.ANY)],
        out_specs=pl.BlockSpec((tm, d), lambda i: (i, 0)),
        out_shape=jax.ShapeDtypeStruct((t, d), F32),
        scratch_shapes=[pltpu.VMEM((2, 2, tm, d), F32), pltpu.SemaphoreType.DMA((2,))],
        compiler_params=pltpu.CompilerParams(dimension_semantics=("arbitrary",)),
        name="moe_combine_ln",
    )(pos3, pos3, route, x, vec(ln_g), vec(ln_b), ys)


def _moe_ln(x, route, counts, wg, wu, wd, ln_g, ln_b):
    t, _ = x.shape
    n_tiles = 2 * t // _MOE_TM + N_EXPERTS
    pos, tile_expert = _route(route, counts, n_tiles)
    xs = _row_scatter(x, pos, jnp.zeros((n_tiles * _MOE_TM, x.shape[1]), x.dtype))
    ys = _expert_ffn(xs, tile_expert, wg, wu, wd)
    return _combine_ln(ys, pos, route, x, ln_g, ln_b)


def _even_layer(x, xb, batch, seq, dsa_bias, w_in, conv_w, conv_b, conv_ln_g, conv_ln_b, w_out,
                ln1_g, ln1_b, ffn_wg, ffn_wu, ffn_wd, ln2_g, ln2_b):
    w_in_p = jnp.zeros((D_MODEL, EVEN_IN_PAD), BF16).at[:, :w_in.shape[1]].set(w_in.astype(BF16))
    h = _matmul(x if xb is None else xb, w_in_p, 1024, EVEN_IN_PAD, BF16).reshape(batch, seq, EVEN_IN_PAD)
    a_out = _conformer_conv(h, conv_w, conv_b, conv_ln_g, conv_ln_b)
    att = _dsa_attention(h, dsa_bias)
    w_out_b = w_out.astype(BF16)
    x1, x1b = _outproj_ln([a_out.reshape(-1, CONV_CH), att.reshape(-1, DSA_HEADS * DSA_HEAD_DIM)],
                          [w_out_b[:CONV_CH], w_out_b[CONV_CH:]], x, ln1_g, ln1_b)
    return _ffn_ln(x1b, x1, ffn_wg.astype(BF16), ffn_wu.astype(BF16), ffn_wd.astype(BF16), ln2_g, ln2_b)


def _odd_layer(x, xb, batch, seq, dil_bias, w_in, w_out, ln1_g, ln1_b, router, moe_wg, moe_wu, moe_wd,
               ln2_g, ln2_b):
    gw = 3 * DIL_HEADS * DIL_HEAD_DIM
    w_in_b = w_in.astype(BF16)
    hs = [_matmul_residue_major(x if xb is None else xb, w_in_b[:, g * gw:(g + 1) * gw], batch, seq, dil, 2048, 768)
          for g, (_, dil) in enumerate(DIL_GROUPS)]
    o = _dilated_attention(hs, dil_bias).reshape(-1, DIL_HEADS * DIL_HEAD_DIM)
    x1, route, counts = _outproj_ln([o], [w_out.astype(BF16)], x, ln1_g, ln1_b, router=router)
    out = _moe_ln(x1, route, counts, moe_wg, moe_wu, moe_wd, ln2_g, ln2_b)
    return out, out.astype(BF16)


def kernel(x, rel_bias, even_w_in, even_conv_w, even_conv_b, even_conv_ln_g, even_conv_ln_b, even_w_out, even_ln1_g, even_ln1_b, even_ffn_wg, even_ffn_wu, even_ffn_wd, even_ln2_g, even_ln2_b, odd_w_in, odd_w_out, odd_ln1_g, odd_ln1_b, odd_router, odd_moe_wg, odd_moe_wu, odd_moe_wd, odd_ln2_g, odd_ln2_b):
    batch, seq, d = x.shape
    assert d == D_MODEL and seq % DIL_CHUNK == 0
    depth = even_w_in.shape[0] + odd_w_in.shape[0]
    assert depth == DEPTH
    dsa_bias = _dsa_bias_tiles(rel_bias, seq)
    dil_bias = _dil_bias_tiles(rel_bias)
    xf = x.reshape(batch * seq, d)
    xb = None
    for layer in range(depth):
        i = layer // 2
        if layer % 2 == 0:
            xf, xb = _even_layer(xf, xb, batch, seq, dsa_bias, even_w_in[i], even_conv_w[i], even_conv_b[i],
                                 even_conv_ln_g[i], even_conv_ln_b[i], even_w_out[i], even_ln1_g[i],
                                 even_ln1_b[i], even_ffn_wg[i], even_ffn_wu[i], even_ffn_wd[i],
                                 even_ln2_g[i], even_ln2_b[i])
        else:
            xf, xb = _odd_layer(xf, xb, batch, seq, dil_bias, odd_w_in[i], odd_w_out[i], odd_ln1_g[i],
                                odd_ln1_b[i], odd_router[i], odd_moe_wg[i], odd_moe_wu[i], odd_moe_wd[i],
                                odd_ln2_g[i], odd_ln2_b[i])
    return xf.reshape(batch, seq, d)
```

```python
import functools
import math

import jax
import jax.numpy as jnp
from jax import lax
from jax.experimental import pallas as pl
from jax.experimental.pallas import tpu as pltpu

BF16 = jnp.bfloat16
F32 = jnp.float32

D_MODEL = 1024
CONV_CH = 512
CONV_WIDTH = 31
DSA_HEADS = 8
DSA_HEAD_DIM = 64
IDX_HEADS = 8
IDX_DIM = 64
DSA_TOPK_MAX = 256
Q_BLOCK = 128
DIL_GROUPS = ((128, 1), (512, 4), (2048, 16))
DIL_HEADS = 8
DIL_HEAD_DIM = 128
BAND_BLOCK = 128
NUM_BUCKETS = 32
MAX_DISTANCE = 2048
N_EXPERTS = 8
LN_EPS = 1e-5
NEG = -1e30
DEPTH = 2
DEEPNORM_ALPHA = (2 * DEPTH) ** 0.25

LANES = 128
EVEN_IN_PAD = 3200
INT_MIN = -(2 ** 31)
DIL_CHUNK = BAND_BLOCK * max(d for _, d in DIL_GROUPS)

_PARALLEL2 = pltpu.CompilerParams(dimension_semantics=("parallel", "parallel"))


def _layer_norm(y, g, b):
    mu = jnp.mean(y, axis=-1, keepdims=True)
    yc = y - mu
    var = jnp.mean(yc * yc, axis=-1, keepdims=True)
    return yc * lax.rsqrt(var + LN_EPS) * g + b


def _sigmoid(x):
    return 1.0 / (1.0 + jnp.exp(-x))


def _dot_nt(a, b):
    return lax.dot_general(a, b, (((1,), (1,)), ((), ())), preferred_element_type=F32)


def _dot(a, b):
    return jnp.dot(a, b, preferred_element_type=F32)


def _mm_kernel(x_ref, w_ref, o_ref):
    o_ref[...] = _dot(x_ref[...].astype(BF16), w_ref[...]).astype(o_ref.dtype)


def _matmul(x, w, tm, tn, out_dtype):
    t, k = x.shape
    n = w.shape[1]
    return pl.pallas_call(
        _mm_kernel,
        grid=(t // tm, n // tn),
        in_specs=[pl.BlockSpec((tm, k), lambda i, j: (i, 0)),
                  pl.BlockSpec((k, tn), lambda i, j: (0, j))],
        out_specs=pl.BlockSpec((tm, tn), lambda i, j: (i, j)),
        out_shape=jax.ShapeDtypeStruct((t, n), out_dtype),
        compiler_params=_PARALLEL2,
        name="matmul",
    )(x, w)


_MAX_ROW_STRIDE = 4


def _mm_perm_kernel(x_ref, w_ref, o_ref, acc_ref, tmp_ref, *, dil, tm, tn):
    acc = _dot(x_ref[...].astype(BF16), w_ref[...])
    for c in range(tn // LANES):
        acc_ref[c] = acc[:, c * LANES:(c + 1) * LANES]
    if dil <= _MAX_ROW_STRIDE:
        rows = tm // dil
        for r in range(dil):
            for c in range(tn // LANES):
                o_ref[0, r, :, c * LANES:(c + 1) * LANES] = (
                    acc_ref[c, pl.ds(r, rows, stride=dil), :].astype(o_ref.dtype))
        return
    s1 = _MAX_ROW_STRIDE
    s2 = dil // s1
    assert s2 <= _MAX_ROW_STRIDE
    rows1 = tm // s1
    rows = tm // dil
    for c in range(tn // LANES):
        for r1 in range(s1):
            tmp_ref[c, r1 * rows1:(r1 + 1) * rows1, :] = acc_ref[c, pl.ds(r1, rows1, stride=s1), :]
    for r1 in range(s1):
        for r2 in range(s2):
            for c in range(tn // LANES):
                o_ref[0, s1 * r2 + r1, :, c * LANES:(c + 1) * LANES] = (
                    tmp_ref[c, pl.ds(r1 * rows1 + r2, rows, stride=s2), :].astype(o_ref.dtype))


def _matmul_residue_major(x, w, batch, seq, dil, tm, tn):
    t, k = x.shape
    n = w.shape[1]
    if dil == 1:
        return _matmul(x, w, tm, tn, BF16).reshape(batch, seq, n)
    tiles_per_seq = seq // tm
    out = pl.pallas_call(
        functools.partial(_mm_perm_kernel, dil=dil, tm=tm, tn=tn),
        grid=(t // tm, n // tn),
        in_specs=[pl.BlockSpec((tm, k), lambda i, j: (i, 0)),
                  pl.BlockSpec((k, tn), lambda i, j: (0, j))],
        out_specs=pl.BlockSpec((1, dil, tm // dil, tn),
                               lambda i, j: (i // tiles_per_seq, 0, i % tiles_per_seq, j)),
        out_shape=jax.ShapeDtypeStruct((batch, dil, seq // dil, n), BF16),
        scratch_shapes=[pltpu.VMEM((tn // LANES, tm, LANES), F32), pltpu.VMEM((tn // LANES, tm, LANES), F32)],
        compiler_params=_PARALLEL2,
        name="matmul_residue_major",
    )(x, w)
    return out.reshape(batch, seq, n)


_CONV_HALO = 32
_CONV_ROWS = 64
_SUBLANES = 8


def _conv_kernel(val_ref, gate_ref, w_ref, cb_ref, g_ref, b_ref, o_ref, ext_ref, *, ts):
    s = pl.program_id(1)

    @pl.when(s == 0)
    def _():
        ext_ref[0, 0:_CONV_HALO, :] = jnp.zeros((_CONV_HALO, CONV_CH), F32)

    val = val_ref[0].astype(F32)
    gate = gate_ref[0].astype(F32)
    ext_ref[0, _CONV_HALO:_CONV_HALO + ts, :] = val * _sigmoid(gate)
    span = ts + _CONV_HALO - _SUBLANES
    for k in range(1, _SUBLANES):
        ext_ref[k, 0:span, :] = ext_ref[0, k:k + span, :]
    first_tap = _CONV_HALO - (CONV_WIDTH - 1)
    for rc in range(ts // _CONV_ROWS):
        r0 = rc * _CONV_ROWS
        acc = jnp.zeros((_CONV_ROWS, CONV_CH), F32)
        for j in range(CONV_WIDTH):
            shift = (first_tap + j) % _SUBLANES
            lo = r0 + first_tap + j - shift
            acc = acc + ext_ref[shift, lo:lo + _CONV_ROWS, :] * w_ref[j:j + 1, :]
        y = _layer_norm(acc + cb_ref[...], g_ref[...], b_ref[...])
        o_ref[0, r0:r0 + _CONV_ROWS, :] = (y * _sigmoid(y)).astype(o_ref.dtype)
    ext_ref[0, 0:_CONV_HALO, :] = ext_ref[0, ts:ts + _CONV_HALO, :]


def _conformer_conv(h, conv_w, conv_b, ln_g, ln_b, ts=512):
    b, s, _ = h.shape
    w_pad = jnp.zeros((32, CONV_CH), F32).at[:CONV_WIDTH].set(conv_w)
    vec = lambda v: v.reshape(1, CONV_CH).astype(F32)
    full = lambda shape: pl.BlockSpec(shape, lambda bi, si: (0, 0))
    return pl.pallas_call(
        functools.partial(_conv_kernel, ts=ts),
        grid=(b, s // ts),
        in_specs=[pl.BlockSpec((1, ts, CONV_CH), lambda bi, si: (bi, si, 0)),
                  pl.BlockSpec((1, ts, CONV_CH), lambda bi, si: (bi, si, 1)),
                  full((32, CONV_CH)), full((1, CONV_CH)), full((1, CONV_CH)), full((1, CONV_CH))],
        out_specs=pl.BlockSpec((1, ts, CONV_CH), lambda bi, si: (bi, si, 0)),
        out_shape=jax.ShapeDtypeStruct((b, s, CONV_CH), BF16),
        scratch_shapes=[pltpu.VMEM((_SUBLANES, ts + _CONV_HALO, CONV_CH), F32)],
        compiler_params=pltpu.CompilerParams(dimension_semantics=("parallel", "arbitrary")),
        name="conformer_conv",
    )(h, h, w_pad, vec(conv_b), vec(ln_g), vec(ln_b))


def _rel_bucket(dist):
    max_exact = NUM_BUCKETS // 2
    n = dist.astype(jnp.int32)
    nf = jnp.maximum(n, 1).astype(F32)
    large = max_exact + (jnp.log(nf / max_exact) / math.log(MAX_DISTANCE / max_exact)
                         * (NUM_BUCKETS - max_exact)).astype(jnp.int32)
    large = jnp.minimum(large, NUM_BUCKETS - 1)
    return jnp.where(n < max_exact, n, large)


def _bias_by_distance(rel_bias, n):
    return rel_bias[_rel_bucket(jnp.arange(n, dtype=jnp.int32))].astype(F32).T


def _toeplitz_kernel(win_ref, o_ref):
    n_win, span = win_ref.shape
    for n in range(n_win):
        y = jnp.broadcast_to(win_ref[n:n + 1, :], (LANES, span))
        y = pltpu.roll(y, 0, 1, stride=1, stride_axis=0)
        o_ref[n] = y[:, LANES:]


def _toeplitz_tiles(windows, per_step=8):
    n, span = windows.shape
    width = span - LANES
    return pl.pallas_call(
        _toeplitz_kernel,
        grid=(n // per_step,),
        in_specs=[pl.BlockSpec((per_step, span), lambda i: (i, 0))],
        out_specs=pl.BlockSpec((per_step, LANES, width), lambda i: (i, 0, 0)),
        out_shape=jax.ShapeDtypeStruct((n, LANES, width), F32),
        compiler_params=pltpu.CompilerParams(dimension_semantics=("parallel",)),
        name="bias_toeplitz",
    )(windows)


def _dsa_bias_tiles(rel_bias, seq):
    nblk = seq // Q_BLOCK
    heads = rel_bias.shape[1]
    f = _bias_by_distance(rel_bias, seq)
    u = jnp.concatenate([jnp.broadcast_to(f[:, :1], (heads, 2 * Q_BLOCK)), f], axis=1)
    blocks = u.reshape(heads, nblk + 2, Q_BLOCK)
    windows = jnp.concatenate([blocks[:, :-1], blocks[:, 1:]], axis=-1)
    windows = windows.transpose(1, 0, 2).reshape((nblk + 1) * heads, 2 * Q_BLOCK)
    return _toeplitz_tiles(windows).reshape(nblk + 1, heads, Q_BLOCK, Q_BLOCK)


def _dil_bias_tiles(rel_bias):
    max_dil = max(d for _, d in DIL_GROUPS)
    heads = rel_bias.shape[1]
    span = 2 * BAND_BLOCK
    f = _bias_by_distance(rel_bias, span * max_dil + 1)
    windows = []
    for _, dil in DIL_GROUPS:
        fd = f[:, 0:span * dil + 1:dil]
        tail = jnp.broadcast_to(fd[:, :1], (heads, BAND_BLOCK - 1))
        windows.append(jnp.concatenate([fd[:, ::-1], tail], axis=1))
    tiles = _toeplitz_tiles(jnp.concatenate(windows, axis=0))
    return tiles.reshape(len(DIL_GROUPS), heads, BAND_BLOCK, span)


_KC = 2 * Q_BLOCK
_VT_ROWS = DSA_HEAD_DIM + 16


def _sortable_key(score):
    score = jnp.where(score == 0.0, 0.0, score)
    bits = lax.bitcast_convert_type(score, jnp.int32)
    return jnp.where(bits < 0, bits ^ jnp.int32(0x7FFFFFFF), bits)


def _rows_reduce(x, op):
    rows = x.shape[0]
    return op(op(x.reshape(rows // 8, 8, LANES), axis=0), axis=0, keepdims=True)


def _keys_reduce(x, op):
    h, keys, _ = x.shape
    return op(op(x.reshape(h, keys // 8, 8, LANES), axis=1), axis=1, keepdims=True)


def _dsa_t_kernel(q_ref, qi_ref, wq_ref, k_ref, vt_ref, ki_ref, bias_hbm, o_ref,
                  bias_ref, keys_ref, qm_ref, acc_ref, dots_a_ref, dots_b_ref, sem, *, topk):
    b = pl.program_id(0)
    i = pl.program_id(1)

    @pl.when((b == 0) & (i == 0))
    def _():
        cp = pltpu.make_async_copy(bias_hbm, bias_ref, sem)
        cp.start()
        cp.wait()

    nch = (i + 2) // 2
    key_off = lax.broadcasted_iota(jnp.int32, (_KC, LANES), 0)
    q_pos = i * Q_BLOCK + lax.broadcasted_iota(jnp.int32, (_KC, LANES), 1)
    lane = lax.broadcasted_iota(jnp.int32, (Q_BLOCK, LANES), 1)
    low_half = lane < DSA_HEAD_DIM

    w_t = (wq_ref[0].astype(F32) * (IDX_HEADS ** -0.5 * IDX_DIM ** -0.5)).T
    qi = qi_ref[0].astype(F32)
    qi_heads = []
    for p in range(IDX_HEADS // 2):
        pair = qi[:, p * LANES:(p + 1) * LANES]
        qi_heads.append(jnp.where(low_half, pair, 0.0))
        qi_heads.append(jnp.where(low_half, pltpu.roll(pair, IDX_DIM, 1), 0.0))
    qi_stack = jnp.concatenate(qi_heads, axis=0).astype(BF16)
    q_scaled = (q_ref[0].astype(F32) * (DSA_HEAD_DIM ** -0.5)).astype(BF16)
    for p in range(DSA_HEADS // 2):
        qp = q_scaled[:, p * LANES:(p + 1) * LANES]
        qm_ref[2 * p] = jnp.where(low_half, qp, jnp.zeros_like(qp))
        qm_ref[2 * p + 1] = jnp.where(low_half, jnp.zeros_like(qp), qp)

    def index_dots(c, dst_ref):
        c = jnp.minimum(c, nch - 1)
        k0 = pl.multiple_of(c * _KC, _KC)
        ki = ki_ref[0, pl.ds(k0, _KC), :]
        dst_ref[...] = _dot_nt(ki, qi_stack)

    def consume_dots(c, src_ref):
        k0 = c * _KC
        score = jnp.zeros((_KC, LANES), F32)
        for h in range(IDX_HEADS):
            w_row = w_t[IDX_DIM + h:IDX_DIM + h + 1, :]
            score = score + w_row * jnp.maximum(src_ref[:, h * Q_BLOCK:(h + 1) * Q_BLOCK], 0.0)
        keys_ref[c] = jnp.where(k0 + key_off <= q_pos, _sortable_key(score), jnp.int32(INT_MIN))

    index_dots(0, dots_a_ref)

    def score_body(c2, carry):
        index_dots(2 * c2 + 1, dots_b_ref)
        consume_dots(2 * c2, dots_a_ref)
        index_dots(2 * c2 + 2, dots_a_ref)
        consume_dots(2 * c2 + 1, dots_b_ref)
        return carry

    lax.fori_loop(0, (nch + 1) // 2, score_body, 0)

    def count_ge(cand):
        cb = jnp.broadcast_to(cand, (8, LANES))
        chains = 4

        def body(c2, acc):
            for half in range(2):
                kk = keys_ref[2 * c2 + half].reshape(_KC // (8 * chains), chains, 8, LANES)
                acc = acc + jnp.sum(jnp.where(kk >= cb, 1.0, 0.0), axis=0)
            return acc

        acc = lax.fori_loop(0, (nch + 1) // 2, body, jnp.zeros((chains, 8, LANES), F32))
        return jnp.sum(jnp.sum(acc, axis=0), axis=0, keepdims=True)

    kf = float(topk)
    zero = jnp.zeros((1, LANES), jnp.int32)
    thr0 = jnp.where(count_ge(zero) >= kf, zero, jnp.full((1, LANES), INT_MIN, jnp.int32))

    def bit_body(t, thr):
        cand = thr | jnp.left_shift(jnp.int32(1), 30 - t)
        return jnp.where(count_ge(cand) >= kf, cand, thr)

    thr = lax.fori_loop(0, 31, bit_body, thr0)
    n_gt = count_ge(jnp.where(thr == jnp.int32(2 ** 31 - 1), thr, thr + 1))
    need = kf - n_gt
    tri = (lax.broadcasted_iota(jnp.int32, (_KC, _KC), 1)
           <= lax.broadcasted_iota(jnp.int32, (_KC, _KC), 0)).astype(BF16)

    acc_ref[...] = jnp.zeros_like(acc_ref)

    def attn_body(c, carry):
        ms, tie_carry = carry
        k0 = pl.multiple_of(c * _KC, _KC)
        kk = keys_ref[c]
        causal = kk > jnp.int32(INT_MIN)
        eq = (kk == thr) & causal
        eq_f = jnp.where(eq, 1.0, 0.0)
        rank = _dot(tri, eq_f.astype(BF16)) + tie_carry
        sel = ((kk > thr) & causal) | (eq & (rank <= need))
        tie_carry = tie_carry + _rows_reduce(eq_f, jnp.sum)
        d = i - 2 * c
        kc = k_ref[0, pl.ds(k0, _KC), :]
        k_heads = jnp.stack([kc[:, (h // 2) * LANES:(h // 2 + 1) * LANES] for h in range(DSA_HEADS)])
        s = jnp.einsum("hkd,hqd->hkq", k_heads, qm_ref[...], preferred_element_type=F32)
        bias = jnp.concatenate([bias_ref[d + 1], bias_ref[d]], axis=1)
        s = jnp.where(sel, s + bias, NEG)
        m_new = jnp.maximum(ms, _keys_reduce(s, jnp.max))
        alpha = jnp.exp(ms - m_new)
        pr = jnp.exp((s - m_new).astype(BF16))
        pv = jnp.einsum("hdk,hkq->hdq", vt_ref[0, c], pr, preferred_element_type=F32)
        acc_ref[...] = alpha * acc_ref[...] + pv
        return m_new, tie_carry

    init = (jnp.full((DSA_HEADS, 1, LANES), NEG, F32), jnp.zeros((1, LANES), F32))
    lax.fori_loop(0, nch, attn_body, init)
    acc = acc_ref[...]
    out_t = acc[:, :DSA_HEAD_DIM] / acc[:, DSA_HEAD_DIM:DSA_HEAD_DIM + 1]
    o_ref[0] = out_t.reshape(DSA_HEADS * DSA_HEAD_DIM, Q_BLOCK).T.astype(o_ref.dtype)


_VT_CHUNKS = 4


def _vt_kernel(v_ref, o_ref):
    for c in range(_VT_CHUNKS):
        vt = v_ref[0, c * _KC:(c + 1) * _KC, :].astype(F32).T
        o_ref[0, c, :, :DSA_HEAD_DIM, :] = vt.reshape(DSA_HEADS, DSA_HEAD_DIM, _KC).astype(o_ref.dtype)
        o_ref[0, c, :, DSA_HEAD_DIM:, :] = jnp.ones((DSA_HEADS, _VT_ROWS - DSA_HEAD_DIM, _KC), o_ref.dtype)


def _values_key_minor(h):
    b, s, _ = h.shape
    width = DSA_HEADS * DSA_HEAD_DIM
    v_block = (2 * CONV_CH + 2 * width) // width
    return pl.pallas_call(
        _vt_kernel,
        grid=(b, s // (_VT_CHUNKS * _KC)),
        in_specs=[pl.BlockSpec((1, _VT_CHUNKS * _KC, width), lambda bi, ci: (bi, ci, v_block))],
        out_specs=pl.BlockSpec((1, _VT_CHUNKS, DSA_HEADS, _VT_ROWS, _KC), lambda bi, ci: (bi, ci, 0, 0, 0)),
        out_shape=jax.ShapeDtypeStruct((b, s // _KC, DSA_HEADS, _VT_ROWS, _KC), BF16),
        compiler_params=_PARALLEL2,
        name="dsa_values_key_minor",
    )(h)


def _dsa_attention(h, bias_tiles):
    b, s, _ = h.shape
    width = DSA_HEADS * DSA_HEAD_DIM
    nblk = s // Q_BLOCK
    topk = min(DSA_TOPK_MAX, s // 4)
    kw_block = (CONV_CH * 2 + 4 * width) // LANES
    vt = _values_key_minor(h)
    return pl.pallas_call(
        functools.partial(_dsa_t_kernel, topk=topk),
        grid=(b, nblk),
        in_specs=[pl.BlockSpec((1, Q_BLOCK, width), lambda bi, qi: (bi, qi, 2)),
                  pl.BlockSpec((1, Q_BLOCK, width), lambda bi, qi: (bi, qi, 5)),
                  pl.BlockSpec((1, Q_BLOCK, LANES), lambda bi, qi: (bi, qi, kw_block)),
                  pl.BlockSpec((1, s, width), lambda bi, qi: (bi, 0, 3)),
                  pl.BlockSpec((1, s // _KC, DSA_HEADS, _VT_ROWS, _KC), lambda bi, qi: (bi, 0, 0, 0, 0)),
                  pl.BlockSpec((1, s, LANES), lambda bi, qi: (bi, 0, kw_block)),
                  pl.BlockSpec(memory_space=pl.ANY)],
        out_specs=pl.BlockSpec((1, Q_BLOCK, width), lambda bi, qi: (bi, qi, 0)),
        out_shape=jax.ShapeDtypeStruct((b, s, width), BF16),
        scratch_shapes=[pltpu.VMEM((nblk + 1, DSA_HEADS, Q_BLOCK, Q_BLOCK), F32),
                        pltpu.VMEM((s // _KC, _KC, Q_BLOCK), jnp.int32),
                        pltpu.VMEM((DSA_HEADS, Q_BLOCK, LANES), BF16),
                        pltpu.VMEM((DSA_HEADS, _VT_ROWS, Q_BLOCK), F32),
                        pltpu.VMEM((_KC, IDX_HEADS * Q_BLOCK), F32),
                        pltpu.VMEM((_KC, IDX_HEADS * Q_BLOCK), F32),
                        pltpu.SemaphoreType.DMA(())],
        compiler_params=pltpu.CompilerParams(dimension_semantics=("arbitrary", "arbitrary")),
        name="dsa_attention",
    )(h, h, h, h, vt, h, bias_tiles)


_DIL_WAVE = 16


def _dil_kernel(*refs, seq):
    n_g = len(DIL_GROUPS)
    qkv = refs[:3 * n_g]
    bias_ref, o_ref, og_ref, lg_ref = refs[3 * n_g:]
    c = pl.program_id(2)
    ii = lax.broadcasted_iota(jnp.int32, (BAND_BLOCK, BAND_BLOCK), 0)
    jj = lax.broadcasted_iota(jnp.int32, (BAND_BLOCK, BAND_BLOCK), 1)
    valid_prev = jj >= ii
    valid_cur = ii >= jj
    scale = DIL_HEAD_DIM ** -0.5
    for g, (window, dil) in enumerate(DIL_GROUPS):
        assert window // dil == BAND_BLOCK
        q_ref, k_ref, v_ref = qkv[3 * g:3 * g + 3]
        sub_len = seq // dil
        blocks_per_chunk = DIL_CHUNK // (BAND_BLOCK * dil)
        bias_prev = bias_ref[g, 0, :, :BAND_BLOCK]
        bias_cur = bias_ref[g, 0, :, BAND_BLOCK:]
        tiles = [(r, nl) for r in range(dil) for nl in range(blocks_per_chunk)]
        for w0 in range(0, len(tiles), _DIL_WAVE):
            wave = tiles[w0:w0 + _DIL_WAVE]
            cur, prev, mask_prev = [], [], []
            for r, nl in wave:
                row0 = pl.multiple_of(r * sub_len + c * (DIL_CHUNK // dil) + nl * BAND_BLOCK, BAND_BLOCK)
                cur.append(pl.ds(row0, BAND_BLOCK))
                prev.append(pl.ds(pl.multiple_of(jnp.maximum(row0 - BAND_BLOCK, 0), BAND_BLOCK), BAND_BLOCK))
                mask_prev.append(valid_prev if nl > 0 else (valid_prev & (c > 0)))
            gather = lambda ref, slices: jnp.stack([ref[0, sl, :] for sl in slices])
            qb = gather(q_ref, cur)
            s_cur = jnp.einsum("wqd,wkd->wqk", qb, gather(k_ref, cur), preferred_element_type=F32)
            s_prev = jnp.einsum("wqd,wkd->wqk", qb, gather(k_ref, prev), preferred_element_type=F32)
            s_cur = jnp.where(valid_cur, s_cur * scale + bias_cur, NEG)
            s_prev = jnp.where(jnp.stack(mask_prev), s_prev * scale + bias_prev, NEG)
            m = jnp.max(jnp.maximum(s_cur, s_prev), axis=2, keepdims=True)
            p_cur = jnp.exp(s_cur - m)
            p_prev = jnp.exp(s_prev - m)
            ssum = jnp.sum(p_cur + p_prev, axis=2, keepdims=True)
            o = (jnp.einsum("wqk,wkd->wqd", p_cur.astype(BF16), gather(v_ref, cur), preferred_element_type=F32)
                 + jnp.einsum("wqk,wkd->wqd", p_prev.astype(BF16), gather(v_ref, prev),
                              preferred_element_type=F32)) / ssum
            lse = m + jnp.log(ssum)
            for w, (r, nl) in enumerate(wave):
                dst = pl.ds(r + nl * BAND_BLOCK * dil, BAND_BLOCK, stride=dil) if dil > 1 else pl.ds(
                    nl * BAND_BLOCK, BAND_BLOCK)
                og_ref[g, dst, :] = o[w]
                lg_ref[g, dst, :] = jnp.broadcast_to(lse[w], (BAND_BLOCK, LANES))
    rows = 256
    for rc in range(DIL_CHUNK // rows):
        sl = slice(rc * rows, (rc + 1) * rows)
        lses = [lg_ref[g, sl, :] for g in range(n_g)]
        mx = functools.reduce(jnp.maximum, lses)
        ws = [jnp.exp(l - mx) for l in lses]
        den = functools.reduce(lambda a, b_: a + b_, ws)
        num = functools.reduce(lambda a, b_: a + b_, [ws[g] * og_ref[g, sl, :] for g in range(n_g)])
        o_ref[0, sl, :] = (num / den).astype(o_ref.dtype)


def _dilated_attention(hs, bias_tiles):
    b, s, _ = hs[0].shape
    in_specs = []
    args = []
    for hg in hs:
        for part in range(3):
            in_specs.append(pl.BlockSpec((1, s, DIL_HEAD_DIM),
                                         lambda bi, hi, ci, part=part: (bi, 0, part * DIL_HEADS + hi)))
            args.append(hg)
    in_specs.append(pl.BlockSpec((len(DIL_GROUPS), 1, BAND_BLOCK, 2 * BAND_BLOCK),
                                 lambda bi, hi, ci: (0, hi, 0, 0)))
    return pl.pallas_call(
        functools.partial(_dil_kernel, seq=s),
        grid=(b, DIL_HEADS, s // DIL_CHUNK),
        in_specs=in_specs,
        out_specs=pl.BlockSpec((1, DIL_CHUNK, DIL_HEAD_DIM), lambda bi, hi, ci: (bi, ci, hi)),
        out_shape=jax.ShapeDtypeStruct((b, s, DIL_HEADS * DIL_HEAD_DIM), BF16),
        scratch_shapes=[pltpu.VMEM((len(DIL_GROUPS), DIL_CHUNK, LANES), F32),
                        pltpu.VMEM((len(DIL_GROUPS), DIL_CHUNK, LANES), F32)],
        compiler_params=pltpu.CompilerParams(dimension_semantics=("parallel", "parallel", "arbitrary")),
        name="dilated_attention",
    )(*args, bias_tiles)


def _outproj_kernel(*refs, n_in, with_router):
    ins = refs[:n_in]
    ws = refs[n_in:2 * n_in]
    x_ref, g_ref, b_ref = refs[2 * n_in:2 * n_in + 3]
    rest = refs[2 * n_in + 3:]
    mix = _dot(ins[0][...], ws[0][...])
    for a_ref, w_ref in zip(ins[1:], ws[1:]):
        mix = mix + _dot(a_ref[...], w_ref[...])
    xn = _layer_norm(DEEPNORM_ALPHA * x_ref[...] + mix, g_ref[...], b_ref[...])
    if not with_router:
        o_ref, ob_ref = rest
    else:
        router_ref, o_ref, gates_ref, counts_ref, run_ref, tri_ref = rest
        ob_ref = None
        tm = xn.shape[0]

        @pl.when(pl.program_id(0) == 0)
        def _():
            run_ref[...] = jnp.zeros_like(run_ref)
            tri_ref[...] = (lax.broadcasted_iota(jnp.int32, (tm, tm), 0)
                            > lax.broadcasted_iota(jnp.int32, (tm, tm), 1)).astype(BF16)

        logits = _dot(xn.astype(BF16), router_ref[...])
        lane = lax.broadcasted_iota(jnp.int32, logits.shape, 1).astype(F32)
        lg = jnp.where(lane < N_EXPERTS, logits, -jnp.inf)
        m1 = jnp.max(lg, axis=1, keepdims=True)
        i1 = jnp.min(jnp.where(lg == m1, lane, float(LANES)), axis=1, keepdims=True)
        lg2 = jnp.where(lane == i1, -jnp.inf, lg)
        m2 = jnp.max(lg2, axis=1, keepdims=True)
        i2 = jnp.min(jnp.where(lg2 == m2, lane, float(LANES)), axis=1, keepdims=True)
        e2 = jnp.exp(m2 - m1)
        den = 1.0 + e2
        oh1 = jnp.where(lane == i1, 1.0, 0.0)
        oh2 = jnp.where(lane == i2, 1.0, 0.0)
        both = oh1 + oh2
        before = _dot(tri_ref[...], both.astype(BF16)) + run_ref[...]
        r1 = jnp.sum(before * oh1, axis=1, keepdims=True)
        r2 = jnp.sum(before * oh2, axis=1, keepdims=True)
        run_ref[...] += jnp.sum(both, axis=0, keepdims=True)
        counts_ref[...] = run_ref[...]
        gates_ref[...] = (jnp.where(lane == 0.0, i1, 0.0) + jnp.where(lane == 1.0, i2, 0.0)
                          + jnp.where(lane == 2.0, 1.0 / den, 0.0) + jnp.where(lane == 3.0, e2 / den, 0.0)
                          + jnp.where(lane == 4.0, r1, 0.0) + jnp.where(lane == 5.0, r2, 0.0))
    o_ref[...] = xn
    if ob_ref is not None:
        ob_ref[...] = xn.astype(BF16)


def _outproj_ln(ins, ws, x, ln_g, ln_b, router=None, tm=512):
    t, d = x.shape
    n_in = len(ins)
    row = lambda width: pl.BlockSpec((tm, width), lambda i: (i, 0))
    full = lambda a: pl.BlockSpec(a.shape, lambda i: (0, 0))
    vec = lambda v: v.reshape(1, d).astype(F32)
    args = list(ins) + list(ws) + [x, vec(ln_g), vec(ln_b)]
    in_specs = [row(a.shape[1]) for a in ins] + [full(w) for w in ws] + [row(d), full(vec(ln_g)), full(vec(ln_b))]
    out_shape = [jax.ShapeDtypeStruct((t, d), F32), jax.ShapeDtypeStruct((t, d), BF16)]
    out_specs = [row(d), row(d)]
    if router is not None:
        router_pad = jnp.zeros((d, LANES), BF16).at[:, :N_EXPERTS].set(router.astype(BF16))
        args.append(router_pad)
        in_specs.append(full(router_pad))
        out_shape = [out_shape[0], jax.ShapeDtypeStruct((t, LANES), F32), jax.ShapeDtypeStruct((1, LANES), F32)]
        out_specs = [row(d), row(LANES), pl.BlockSpec((1, LANES), lambda i: (0, 0))]
        scratch = [pltpu.VMEM((1, LANES), F32), pltpu.VMEM((tm, tm), BF16)]
    else:
        scratch = []
    semantics = "arbitrary" if router is not None else "parallel"
    return pl.pallas_call(
        functools.partial(_outproj_kernel, n_in=n_in, with_router=router is not None),
        grid=(t // tm,),
        in_specs=in_specs,
        out_specs=out_specs,
        out_shape=out_shape,
        scratch_shapes=scratch,
        compiler_params=pltpu.CompilerParams(dimension_semantics=(semantics,)),
        name="outproj_ln",
    )(*args)


def _ffn_kernel(xb_ref, x_ref, wg_ref, wu_ref, wd_ref, g_ref, b_ref, o_ref, ob_ref, acc_ref):
    f = pl.program_id(1)

    @pl.when(f == 0)
    def _():
        acc_ref[...] = jnp.zeros_like(acc_ref)

    xb = xb_ref[...]
    hg = _dot(xb, wg_ref[...])
    hu = _dot(xb, wu_ref[...])
    act = (hg * _sigmoid(hg) * hu).astype(BF16)
    acc_ref[...] += _dot(act, wd_ref[...])

    @pl.when(f == pl.num_programs(1) - 1)
    def _():
        xn = _layer_norm(DEEPNORM_ALPHA * x_ref[...] + acc_ref[...], g_ref[...], b_ref[...])
        o_ref[...] = xn
        ob_ref[...] = xn.astype(BF16)


def _ffn_ln(xb, x, wg, wu, wd, ln_g, ln_b, tm=512, tf=1408):
    t, d = x.shape
    ff = wg.shape[1]
    vec = lambda v: v.reshape(1, d).astype(F32)
    return pl.pallas_call(
        _ffn_kernel,
        grid=(t // tm, ff // tf),
        in_specs=[pl.BlockSpec((tm, d), lambda i, f: (i, 0)),
                  pl.BlockSpec((tm, d), lambda i, f: (i, 0)),
                  pl.BlockSpec((d, tf), lambda i, f: (0, f)),
                  pl.BlockSpec((d, tf), lambda i, f: (0, f)),
                  pl.BlockSpec((tf, d), lambda i, f: (f, 0)),
                  pl.BlockSpec((1, d), lambda i, f: (0, 0)),
                  pl.BlockSpec((1, d), lambda i, f: (0, 0))],
        out_specs=[pl.BlockSpec((tm, d), lambda i, f: (i, 0)),
                   pl.BlockSpec((tm, d), lambda i, f: (i, 0))],
        out_shape=[jax.ShapeDtypeStruct((t, d), F32), jax.ShapeDtypeStruct((t, d), BF16)],
        scratch_shapes=[pltpu.VMEM((tm, d), F32)],
        compiler_params=pltpu.CompilerParams(dimension_semantics=("parallel", "arbitrary")),
        name="ffn_ln",
    )(xb, x, wg, wu, wd, vec(ln_g), vec(ln_b))


_MOE_TM = 1024


def _route(route, counts, n_tiles):
    expert = route[:, 0:2].astype(jnp.int32)
    rank = route[:, 4:6].astype(jnp.int32)
    counts = counts[0, :N_EXPERTS].astype(jnp.int32)
    padded = (counts + _MOE_TM - 1) // _MOE_TM * _MOE_TM
    ends = jnp.cumsum(padded)
    starts = ends - padded
    onehot = expert[:, :, None] == jnp.arange(N_EXPERTS, dtype=jnp.int32)
    pos = (rank + jnp.sum(jnp.where(onehot, starts, 0), axis=-1)).reshape(-1)
    tile_start = jnp.arange(n_tiles, dtype=jnp.int32) * _MOE_TM
    tile_expert = jnp.minimum(jnp.sum((tile_start[:, None] >= ends[None, :]).astype(jnp.int32), axis=1),
                              N_EXPERTS - 1)
    return pos.astype(jnp.int32), tile_expert.astype(jnp.int32)


def _row_scatter_kernel(pos_ref, x_ref, init_ref, o_ref, sem, *, tm):
    del init_ref

    def body(j, carry):
        for choice in range(2):
            p = pos_ref[0, 0, 2 * j + choice]
            pltpu.make_async_copy(x_ref.at[pl.ds(j, 1)], o_ref.at[pl.ds(p, 1)], sem).start()
        return carry

    lax.fori_loop(0, tm, body, 0, unroll=8)
    for _ in range(2):
        pltpu.make_async_copy(x_ref, o_ref.at[pl.ds(0, tm)], sem).wait()


def _row_scatter(x, pos, init, tm=512):
    t, d = x.shape
    n_rows = init.shape[0]
    return pl.pallas_call(
        functools.partial(_row_scatter_kernel, tm=tm),
        grid=(t // tm,),
        in_specs=[pl.BlockSpec((1, 1, 2 * tm), lambda i: (i, 0, 0), memory_space=pltpu.SMEM),
                  pl.BlockSpec((tm, d), lambda i: (i, 0)),
                  pl.BlockSpec(memory_space=pl.ANY)],
        out_specs=pl.BlockSpec(memory_space=pl.ANY),
        out_shape=jax.ShapeDtypeStruct((n_rows, d), x.dtype),
        scratch_shapes=[pltpu.SemaphoreType.DMA(())],
        input_output_aliases={2: 0},
        compiler_params=pltpu.CompilerParams(dimension_semantics=("arbitrary",)),
        name="moe_row_scatter",
    )(pos.reshape(t // tm, 1, 2 * tm), x, init)


def _expert_ffn_kernel(te_ref, x_ref, wg_ref, wu_ref, wd_ref, o_ref, acc_ref):
    del te_ref
    f = pl.program_id(1)

    @pl.when(f == 0)
    def _():
        acc_ref[...] = jnp.zeros_like(acc_ref)

    xb = x_ref[...].astype(BF16)
    hg = _dot(xb, wg_ref[0].astype(BF16))
    hu = _dot(xb, wu_ref[0].astype(BF16))
    act = (hg * _sigmoid(hg) * hu).astype(BF16)
    acc_ref[...] += _dot(act, wd_ref[0].astype(BF16))

    @pl.when(f == pl.num_programs(1) - 1)
    def _():
        o_ref[...] = acc_ref[...]


def _expert_ffn(xs, tile_expert, wg, wu, wd, tf=512):
    n_rows, d = xs.shape
    ff = wg.shape[2]
    grid_spec = pltpu.PrefetchScalarGridSpec(
        num_scalar_prefetch=1,
        grid=(n_rows // _MOE_TM, ff // tf),
        in_specs=[pl.BlockSpec((_MOE_TM, d), lambda i, f, te: (i, 0)),
                  pl.BlockSpec((1, d, tf), lambda i, f, te: (te[i], 0, f)),
                  pl.BlockSpec((1, d, tf), lambda i, f, te: (te[i], 0, f)),
                  pl.BlockSpec((1, tf, d), lambda i, f, te: (te[i], f, 0))],
        out_specs=pl.BlockSpec((_MOE_TM, d), lambda i, f, te: (i, 0)),
        scratch_shapes=[pltpu.VMEM((_MOE_TM, d), F32)])
    return pl.pallas_call(
        _expert_ffn_kernel,
        grid_spec=grid_spec,
        out_shape=jax.ShapeDtypeStruct((n_rows, d), F32),
        compiler_params=pltpu.CompilerParams(dimension_semantics=("parallel", "arbitrary")),
        name="moe_expert_ffn",
    )(tile_expert, xs, wg, wu, wd)


def _combine_kernel(pos_ref, pos_next_ref, route_ref, x_ref, g_ref, b_ref, y_ref, o_ref, ybuf_ref, sems, *, tm):
    i = pl.program_id(0)
    n = pl.num_programs(0)

    def gather(src_pos_ref, slot):
        def body(j, carry):
            for choice in range(2):
                p = src_pos_ref[0, 0, 2 * j + choice]
                pltpu.make_async_copy(y_ref.at[pl.ds(p, 1)], ybuf_ref.at[slot, choice, pl.ds(j, 1)],
                                      sems.at[slot]).start()
            return carry

        lax.fori_loop(0, tm, body, 0, unroll=8)

    @pl.when(i == 0)
    def _():
        gather(pos_ref, 0)

    slot = i % 2

    @pl.when(i + 1 < n)
    def _():
        gather(pos_next_ref, 1 - slot)

    for choice in range(2):
        pltpu.make_async_copy(y_ref.at[pl.ds(0, tm)], ybuf_ref.at[slot, choice], sems.at[slot]).wait()
    route = route_ref[...]
    y = route[:, 2:3] * ybuf_ref[slot, 0] + route[:, 3:4] * ybuf_ref[slot, 1]
    o_ref[...] = _layer_norm(DEEPNORM_ALPHA * x_ref[...] + y, g_ref[...], b_ref[...])


def _combine_ln(ys, pos, route, x, ln_g, ln_b, tm=512):
    t, d = x.shape
    vec = lambda v: v.reshape(1, d).astype(F32)
    last = t // tm - 1
    pos3 = pos.reshape(t // tm, 1, 2 * tm)
    return pl.pallas_call(
        functools.partial(_combine_kernel, tm=tm),
        grid=(t // tm,),
        in_specs=[pl.BlockSpec((1, 1, 2 * tm), lambda i: (i, 0, 0), memory_space=pltpu.SMEM),
                  pl.BlockSpec((1, 1, 2 * tm), lambda i: (jnp.minimum(i + 1, last), 0, 0), memory_space=pltpu.SMEM),
                  pl.BlockSpec((tm, LANES), lambda i: (i, 0)),
                  pl.BlockSpec((tm, d), lambda i: (i, 0)),
                  pl.BlockSpec((1, d), lambda i: (0, 0)),
                  pl.BlockSpec((1, d), lambda i: (0, 0)),
                  pl.BlockSpec(memory_space=pl.ANY)],
        out_specs=pl.BlockSpec((tm, d), lambda i: (i, 0)),
        out_shape=jax.ShapeDtypeStruct((t, d), F32),
        scratch_shapes=[pltpu.VMEM((2, 2, tm, d), F32), pltpu.SemaphoreType.DMA((2,))],
        compiler_params=pltpu.CompilerParams(dimension_semantics=("arbitrary",)),
        name="moe_combine_ln",
    )(pos3, pos3, route, x, vec(ln_g), vec(ln_b), ys)


def _moe_ln(x, route, counts, wg, wu, wd, ln_g, ln_b):
    t, _ = x.shape
    n_tiles = 2 * t // _MOE_TM + N_EXPERTS
    pos, tile_expert = _route(route, counts, n_tiles)
    xs = _row_scatter(x, pos, jnp.zeros((n_tiles * _MOE_TM, x.shape[1]), x.dtype))
    ys = _expert_ffn(xs, tile_expert, wg, wu, wd)
    return _combine_ln(ys, pos, route, x, ln_g, ln_b)


def _even_layer(x, xb, batch, seq, dsa_bias, w_in, conv_w, conv_b, conv_ln_g, conv_ln_b, w_out,
                ln1_g, ln1_b, ffn_wg, ffn_wu, ffn_wd, ln2_g, ln2_b):
    w_in_p = jnp.zeros((D_MODEL, EVEN_IN_PAD), BF16).at[:, :w_in.shape[1]].set(w_in.astype(BF16))
    h = _matmul(x if xb is None else xb, w_in_p, 1024, EVEN_IN_PAD, BF16).reshape(batch, seq, EVEN_IN_PAD)
    a_out = _conformer_conv(h, conv_w, conv_b, conv_ln_g, conv_ln_b)
    att = _dsa_attention(h, dsa_bias)
    w_out_b = w_out.astype(BF16)
    x1, x1b = _outproj_ln([a_out.reshape(-1, CONV_CH), att.reshape(-1, DSA_HEADS * DSA_HEAD_DIM)],
                          [w_out_b[:CONV_CH], w_out_b[CONV_CH:]], x, ln1_g, ln1_b)
    return _ffn_ln(x1b, x1, ffn_wg.astype(BF16), ffn_wu.astype(BF16), ffn_wd.astype(BF16), ln2_g, ln2_b)


def _odd_layer(x, xb, batch, seq, dil_bias, w_in, w_out, ln1_g, ln1_b, router, moe_wg, moe_wu, moe_wd,
               ln2_g, ln2_b):
    gw = 3 * DIL_HEADS * DIL_HEAD_DIM
    w_in_b = w_in.astype(BF16)
    hs = [_matmul_residue_major(x if xb is None else xb, w_in_b[:, g * gw:(g + 1) * gw], batch, seq, dil, 2048, 768)
          for g, (_, dil) in enumerate(DIL_GROUPS)]
    o = _dilated_attention(hs, dil_bias).reshape(-1, DIL_HEADS * DIL_HEAD_DIM)
    x1, route, counts = _outproj_ln([o], [w_out.astype(BF16)], x, ln1_g, ln1_b, router=router)
    out = _moe_ln(x1, route, counts, moe_wg, moe_wu, moe_wd, ln2_g, ln2_b)
    return out, out.astype(BF16)


def kernel(x, rel_bias, even_w_in, even_conv_w, even_conv_b, even_conv_ln_g, even_conv_ln_b, even_w_out, even_ln1_g, even_ln1_b, even_ffn_wg, even_ffn_wu, even_ffn_wd, even_ln2_g, even_ln2_b, odd_w_in, odd_w_out, odd_ln1_g, odd_ln1_b, odd_router, odd_moe_wg, odd_moe_wu, odd_moe_wd, odd_ln2_g, odd_ln2_b):
    batch, seq, d = x.shape
    assert d == D_MODEL and seq % DIL_CHUNK == 0
    depth = even_w_in.shape[0] + odd_w_in.shape[0]
    assert depth == DEPTH
    dsa_bias = _dsa_bias_tiles(rel_bias, seq)
    dil_bias = _dil_bias_tiles(rel_bias)
    xf = x.reshape(batch * seq, d)
    xb = None
    for layer in range(depth):
        i = layer // 2
        if layer % 2 == 0:
            xf, xb = _even_layer(xf, xb, batch, seq, dsa_bias, even_w_in[i], even_conv_w[i], even_conv_b[i],
                                 even_conv_ln_g[i], even_conv_ln_b[i], even_w_out[i], even_ln1_g[i],
                                 even_ln1_b[i], even_ffn_wg[i], even_ffn_wu[i], even_ffn_wd[i],
                                 even_ln2_g[i], even_ln2_b[i])
        else:
            xf, xb = _odd_layer(xf, xb, batch, seq, dil_bias, odd_w_in[i], odd_w_out[i], odd_ln1_g[i],
                                odd_ln1_b[i], odd_router[i], odd_moe_wg[i], odd_moe_wu[i], odd_moe_wd[i],
                                odd_ln2_g[i], odd_ln2_b[i])
    return xf.reshape(batch, seq, d)
```

```python
import functools
import math

import jax
import jax.numpy as jnp
from jax import lax
from jax.experimental import pallas as pl
from jax.experimental.pallas import tpu as pltpu

BF16 = jnp.bfloat16
F32 = jnp.float32

D_MODEL = 1024
CONV_CH = 512
CONV_WIDTH = 31
DSA_HEADS = 8
DSA_HEAD_DIM = 64
IDX_HEADS = 8
IDX_DIM = 64
DSA_TOPK_MAX = 256
Q_BLOCK = 128
DIL_GROUPS = ((128, 1), (512, 4), (2048, 16))
DIL_HEADS = 8
DIL_HEAD_DIM = 128
BAND_BLOCK = 128
NUM_BUCKETS = 32
MAX_DISTANCE = 2048
N_EXPERTS = 8
LN_EPS = 1e-5
NEG = -1e30
DEPTH = 2
DEEPNORM_ALPHA = (2 * DEPTH) ** 0.25

LANES = 128
EVEN_IN_PAD = 3200
INT_MIN = -(2 ** 31)
DIL_CHUNK = BAND_BLOCK * max(d for _, d in DIL_GROUPS)

_PARALLEL2 = pltpu.CompilerParams(dimension_semantics=("parallel", "parallel"))


def _layer_norm(y, g, b):
    mu = jnp.mean(y, axis=-1, keepdims=True)
    yc = y - mu
    var = jnp.mean(yc * yc, axis=-1, keepdims=True)
    return yc * lax.rsqrt(var + LN_EPS) * g + b


def _sigmoid(x):
    return 1.0 / (1.0 + jnp.exp(-x))


def _dot_nt(a, b):
    return lax.dot_general(a, b, (((1,), (1,)), ((), ())), preferred_element_type=F32)


def _dot(a, b):
    return jnp.dot(a, b, preferred_element_type=F32)


def _mm_kernel(x_ref, w_ref, o_ref):
    o_ref[...] = _dot(x_ref[...].astype(BF16), w_ref[...]).astype(o_ref.dtype)


def _matmul(x, w, tm, tn, out_dtype):
    t, k = x.shape
    n = w.shape[1]
    return pl.pallas_call(
        _mm_kernel,
        grid=(t // tm, n // tn),
        in_specs=[pl.BlockSpec((tm, k), lambda i, j: (i, 0)),
                  pl.BlockSpec((k, tn), lambda i, j: (0, j))],
        out_specs=pl.BlockSpec((tm, tn), lambda i, j: (i, j)),
        out_shape=jax.ShapeDtypeStruct((t, n), out_dtype),
        compiler_params=_PARALLEL2,
        name="matmul",
    )(x, w)


_MAX_ROW_STRIDE = 4


def _mm_perm_kernel(x_ref, w_ref, o_ref, xp_ref, a_ref, b_ref, *, dil, tm):
    rows = tm // dil

    @pl.when(pl.program_id(1) == 0)
    def _():
        for c in range(x_ref.shape[1] // LANES):
            cols = slice(c * LANES, (c + 1) * LANES)
            a_ref[...] = x_ref[:, cols].astype(F32)
            if dil <= _MAX_ROW_STRIDE:
                for r in range(dil):
                    xp_ref[r * rows:(r + 1) * rows, cols] = a_ref[pl.ds(r, rows, stride=dil), :].astype(BF16)
                continue
            s1 = _MAX_ROW_STRIDE
            s2 = dil // s1
            assert s2 <= _MAX_ROW_STRIDE
            rows1 = tm // s1
            for r1 in range(s1):
                b_ref[r1 * rows1:(r1 + 1) * rows1, :] = a_ref[pl.ds(r1, rows1, stride=s1), :]
            for r1 in range(s1):
                for r2 in range(s2):
                    r = s1 * r2 + r1
                    xp_ref[r * rows:(r + 1) * rows, cols] = (
                        b_ref[pl.ds(r1 * rows1 + r2, rows, stride=s2), :].astype(BF16))

    acc = _dot(xp_ref[...], w_ref[...])
    o_ref[0] = acc.reshape(dil, rows, acc.shape[1]).astype(o_ref.dtype)


def _matmul_residue_major(x, w, batch, seq, dil, tm, tn):
    t, k = x.shape
    n = w.shape[1]
    if dil == 1:
        return _matmul(x, w, tm, tn, BF16).reshape(batch, seq, n)
    tiles_per_seq = seq // tm
    out = pl.pallas_call(
        functools.partial(_mm_perm_kernel, dil=dil, tm=tm),
        grid=(t // tm, n // tn),
        in_specs=[pl.BlockSpec((tm, k), lambda i, j: (i, 0)),
                  pl.BlockSpec((k, tn), lambda i, j: (0, j))],
        out_specs=pl.BlockSpec((1, dil, tm // dil, tn),
                               lambda i, j: (i // tiles_per_seq, 0, i % tiles_per_seq, j)),
        out_shape=jax.ShapeDtypeStruct((batch, dil, seq // dil, n), BF16),
        scratch_shapes=[pltpu.VMEM((tm, k), BF16), pltpu.VMEM((tm, LANES), F32), pltpu.VMEM((tm, LANES), F32)],
        compiler_params=pltpu.CompilerParams(dimension_semantics=("parallel", "arbitrary")),
        name="matmul_residue_major",
    )(x, w)
    return out.reshape(batch, seq, n)


_CONV_HALO = 32
_CONV_ROWS = 128
_SUBLANES = 8


def _conv_kernel(val_ref, gate_ref, w_ref, cb_ref, g_ref, b_ref, o_ref, ext_ref, *, ts):
    s = pl.program_id(1)

    @pl.when(s == 0)
    def _():
        ext_ref[0, 0:_CONV_HALO, :] = jnp.zeros((_CONV_HALO, CONV_CH), F32)

    val = val_ref[0].astype(F32)
    gate = gate_ref[0].astype(F32)
    ext_ref[0, _CONV_HALO:_CONV_HALO + ts, :] = val * _sigmoid(gate)
    span = ts + _CONV_HALO - _SUBLANES
    for k in range(1, _SUBLANES):
        ext_ref[k, 0:span, :] = ext_ref[0, k:k + span, :]
    first_tap = _CONV_HALO - (CONV_WIDTH - 1)
    for rc in range(ts // _CONV_ROWS):
        r0 = rc * _CONV_ROWS
        acc = jnp.zeros((_CONV_ROWS, CONV_CH), F32)
        for j in range(CONV_WIDTH):
            shift = (first_tap + j) % _SUBLANES
            lo = r0 + first_tap + j - shift
            acc = acc + ext_ref[shift, lo:lo + _CONV_ROWS, :] * w_ref[j:j + 1, :]
        y = _layer_norm(acc + cb_ref[...], g_ref[...], b_ref[...])
        o_ref[0, r0:r0 + _CONV_ROWS, :] = (y * _sigmoid(y)).astype(o_ref.dtype)
    ext_ref[0, 0:_CONV_HALO, :] = ext_ref[0, ts:ts + _CONV_HALO, :]


def _conformer_conv(h, conv_w, conv_b, ln_g, ln_b, ts=512):
    b, s, _ = h.shape
    w_pad = jnp.zeros((32, CONV_CH), F32).at[:CONV_WIDTH].set(conv_w)
    vec = lambda v: v.reshape(1, CONV_CH).astype(F32)
    full = lambda shape: pl.BlockSpec(shape, lambda bi, si: (0, 0))
    return pl.pallas_call(
        functools.partial(_conv_kernel, ts=ts),
        grid=(b, s // ts),
        in_specs=[pl.BlockSpec((1, ts, CONV_CH), lambda bi, si: (bi, si, 0)),
                  pl.BlockSpec((1, ts, CONV_CH), lambda bi, si: (bi, si, 1)),
                  full((32, CONV_CH)), full((1, CONV_CH)), full((1, CONV_CH)), full((1, CONV_CH))],
        out_specs=pl.BlockSpec((1, ts, CONV_CH), lambda bi, si: (bi, si, 0)),
        out_shape=jax.ShapeDtypeStruct((b, s, CONV_CH), BF16),
        scratch_shapes=[pltpu.VMEM((_SUBLANES, ts + _CONV_HALO, CONV_CH), F32)],
        compiler_params=pltpu.CompilerParams(dimension_semantics=("parallel", "arbitrary")),
        name="conformer_conv",
    )(h, h, w_pad, vec(conv_b), vec(ln_g), vec(ln_b))


def _rel_bucket(dist):
    max_exact = NUM_BUCKETS // 2
    n = dist.astype(jnp.int32)
    nf = jnp.maximum(n, 1).astype(F32)
    large = max_exact + (jnp.log(nf / max_exact) / math.log(MAX_DISTANCE / max_exact)
                         * (NUM_BUCKETS - max_exact)).astype(jnp.int32)
    large = jnp.minimum(large, NUM_BUCKETS - 1)
    return jnp.where(n < max_exact, n, large)


def _bias_by_distance(rel_bias, n):
    return rel_bias[_rel_bucket(jnp.arange(n, dtype=jnp.int32))].astype(F32).T


def _toeplitz_kernel(win_ref, o_ref):
    n_win, span = win_ref.shape
    for n in range(n_win):
        y = jnp.broadcast_to(win_ref[n:n + 1, :], (LANES, span))
        y = pltpu.roll(y, 0, 1, stride=1, stride_axis=0)
        o_ref[n] = y[:, LANES:]


def _toeplitz_tiles(windows, per_step=8):
    n, span = windows.shape
    width = span - LANES
    return pl.pallas_call(
        _toeplitz_kernel,
        grid=(n // per_step,),
        in_specs=[pl.BlockSpec((per_step, span), lambda i: (i, 0))],
        out_specs=pl.BlockSpec((per_step, LANES, width), lambda i: (i, 0, 0)),
        out_shape=jax.ShapeDtypeStruct((n, LANES, width), F32),
        compiler_params=pltpu.CompilerParams(dimension_semantics=("parallel",)),
        name="bias_toeplitz",
    )(windows)


def _dsa_bias_tiles(rel_bias, seq):
    nblk = seq // Q_BLOCK
    heads = rel_bias.shape[1]
    f = _bias_by_distance(rel_bias, seq)
    u = jnp.concatenate([jnp.broadcast_to(f[:, :1], (heads, 2 * Q_BLOCK)), f], axis=1)
    blocks = u.reshape(heads, nblk + 2, Q_BLOCK)
    windows = jnp.concatenate([blocks[:, :-1], blocks[:, 1:]], axis=-1)
    windows = windows.transpose(1, 0, 2).reshape((nblk + 1) * heads, 2 * Q_BLOCK)
    return _toeplitz_tiles(windows).reshape(nblk + 1, heads, Q_BLOCK, Q_BLOCK)


def _dil_bias_tiles(rel_bias):
    max_dil = max(d for _, d in DIL_GROUPS)
    heads = rel_bias.shape[1]
    span = 2 * BAND_BLOCK
    f = _bias_by_distance(rel_bias, span * max_dil + 1)
    windows = []
    for _, dil in DIL_GROUPS:
        fd = f[:, 0:span * dil + 1:dil]
        tail = jnp.broadcast_to(fd[:, :1], (heads, BAND_BLOCK - 1))
        windows.append(jnp.concatenate([fd[:, ::-1], tail], axis=1))
    tiles = _toeplitz_tiles(jnp.concatenate(windows, axis=0))
    return tiles.reshape(len(DIL_GROUPS), heads, BAND_BLOCK, span)


_KC = 2 * Q_BLOCK
_VT_ROWS = DSA_HEAD_DIM + 16


def _sortable_key(score):
    score = jnp.where(score == 0.0, 0.0, score)
    bits = lax.bitcast_convert_type(score, jnp.int32)
    return jnp.where(bits < 0, bits ^ jnp.int32(0x7FFFFFFF), bits)


def _rows_reduce(x, op):
    rows = x.shape[0]
    return op(op(x.reshape(rows // 8, 8, LANES), axis=0), axis=0, keepdims=True)


def _keys_reduce(x, op):
    h, keys, _ = x.shape
    return op(op(x.reshape(h, keys // 8, 8, LANES), axis=1), axis=1, keepdims=True)


def _dsa_t_kernel(q_ref, qi_ref, wq_ref, k_ref, vt_ref, ki_ref, bias_hbm, o_ref,
                  bias_ref, keys_ref, qm_ref, acc_ref, dots_a_ref, dots_b_ref, sem, *, topk):
    b = pl.program_id(0)
    i = pl.program_id(1)

    @pl.when((b == 0) & (i == 0))
    def _():
        cp = pltpu.make_async_copy(bias_hbm, bias_ref, sem)
        cp.start()
        cp.wait()

    nch = (i + 2) // 2
    key_off = lax.broadcasted_iota(jnp.int32, (_KC, LANES), 0)
    q_pos = i * Q_BLOCK + lax.broadcasted_iota(jnp.int32, (_KC, LANES), 1)
    lane = lax.broadcasted_iota(jnp.int32, (Q_BLOCK, LANES), 1)
    low_half = lane < DSA_HEAD_DIM

    w_t = (wq_ref[0].astype(F32) * (IDX_HEADS ** -0.5 * IDX_DIM ** -0.5)).T
    qi = qi_ref[0].astype(F32)
    qi_heads = []
    for p in range(IDX_HEADS // 2):
        pair = qi[:, p * LANES:(p + 1) * LANES]
        qi_heads.append(jnp.where(low_half, pair, 0.0))
        qi_heads.append(jnp.where(low_half, pltpu.roll(pair, IDX_DIM, 1), 0.0))
    qi_stack = jnp.concatenate(qi_heads, axis=0).astype(BF16)
    q_scaled = (q_ref[0].astype(F32) * (DSA_HEAD_DIM ** -0.5)).astype(BF16)
    for p in range(DSA_HEADS // 2):
        qp = q_scaled[:, p * LANES:(p + 1) * LANES]
        qm_ref[2 * p] = jnp.where(low_half, qp, jnp.zeros_like(qp))
        qm_ref[2 * p + 1] = jnp.where(low_half, jnp.zeros_like(qp), qp)

    def index_dots(c, dst_ref):
        c = jnp.minimum(c, nch - 1)
        k0 = pl.multiple_of(c * _KC, _KC)
        ki = ki_ref[0, pl.ds(k0, _KC), :]
        dst_ref[...] = _dot_nt(ki, qi_stack)

    def consume_dots(c, src_ref):
        k0 = c * _KC
        score = jnp.zeros((_KC, LANES), F32)
        for h in range(IDX_HEADS):
            w_row = w_t[IDX_DIM + h:IDX_DIM + h + 1, :]
            score = score + w_row * jnp.maximum(src_ref[:, h * Q_BLOCK:(h + 1) * Q_BLOCK], 0.0)
        keys_ref[c] = jnp.where(k0 + key_off <= q_pos, _sortable_key(score), jnp.int32(INT_MIN))

    index_dots(0, dots_a_ref)

    def score_body(c2, carry):
        index_dots(2 * c2 + 1, dots_b_ref)
        consume_dots(2 * c2, dots_a_ref)
        index_dots(2 * c2 + 2, dots_a_ref)
        consume_dots(2 * c2 + 1, dots_b_ref)
        return carry

    lax.fori_loop(0, (nch + 1) // 2, score_body, 0)

    def count_ge(cand):
        cb = jnp.broadcast_to(cand, (8, LANES))
        chains = 4

        def body(c2, acc):
            for half in range(2):
                kk = keys_ref[2 * c2 + half].reshape(_KC // (8 * chains), chains, 8, LANES)
                acc = acc + jnp.sum(jnp.where(kk >= cb, 1.0, 0.0), axis=0)
            return acc

        acc = lax.fori_loop(0, (nch + 1) // 2, body, jnp.zeros((chains, 8, LANES), F32))
        return jnp.sum(jnp.sum(acc, axis=0), axis=0, keepdims=True)

    kf = float(topk)
    zero = jnp.zeros((1, LANES), jnp.int32)
    thr0 = jnp.where(count_ge(zero) >= kf, zero, jnp.full((1, LANES), INT_MIN, jnp.int32))

    def bit_body(t, thr):
        cand = thr | jnp.left_shift(jnp.int32(1), 30 - t)
        return jnp.where(count_ge(cand) >= kf, cand, thr)

    thr = lax.fori_loop(0, 31, bit_body, thr0)
    n_gt = count_ge(jnp.where(thr == jnp.int32(2 ** 31 - 1), thr, thr + 1))
    need = kf - n_gt
    tri = (lax.broadcasted_iota(jnp.int32, (_KC, _KC), 1)
           <= lax.broadcasted_iota(jnp.int32, (_KC, _KC), 0)).astype(BF16)

    acc_ref[...] = jnp.zeros_like(acc_ref)

    def attn_body(c, carry):
        ms, tie_carry = carry
        k0 = pl.multiple_of(c * _KC, _KC)
        kk = keys_ref[c]
        causal = kk > jnp.int32(INT_MIN)
        eq = (kk == thr) & causal
        eq_f = jnp.where(eq, 1.0, 0.0)
        rank = _dot(tri, eq_f.astype(BF16)) + tie_carry
        sel = ((kk > thr) & causal) | (eq & (rank <= need))
        tie_carry = tie_carry + _rows_reduce(eq_f, jnp.sum)
        d = i - 2 * c
        kc = k_ref[0, pl.ds(k0, _KC), :]
        k_heads = jnp.stack([kc[:, (h // 2) * LANES:(h // 2 + 1) * LANES] for h in range(DSA_HEADS)])
        s = jnp.einsum("hkd,hqd->hkq", k_heads, qm_ref[...], preferred_element_type=F32)
        bias = jnp.concatenate([bias_ref[d + 1], bias_ref[d]], axis=1)
        s = jnp.where(sel, s + bias, NEG)
        m_new = jnp.maximum(ms, _keys_reduce(s, jnp.max))
        alpha = jnp.exp(ms - m_new)
        pr = jnp.exp((s - m_new).astype(BF16))
        pv = jnp.einsum("hdk,hkq->hdq", vt_ref[0, c], pr, preferred_element_type=F32)
        acc_ref[...] = alpha * acc_ref[...] + pv
        return m_new, tie_carry

    init = (jnp.full((DSA_HEADS, 1, LANES), NEG, F32), jnp.zeros((1, LANES), F32))
    lax.fori_loop(0, nch, attn_body, init)
    acc = acc_ref[...]
    out_t = acc[:, :DSA_HEAD_DIM] / acc[:, DSA_HEAD_DIM:DSA_HEAD_DIM + 1]
    o_ref[0] = out_t.reshape(DSA_HEADS * DSA_HEAD_DIM, Q_BLOCK).T.astype(o_ref.dtype)


_VT_CHUNKS = 4


def _vt_kernel(v_ref, o_ref):
    for c in range(_VT_CHUNKS):
        vt = v_ref[0, c * _KC:(c + 1) * _KC, :].astype(F32).T
        o_ref[0, c, :, :DSA_HEAD_DIM, :] = vt.reshape(DSA_HEADS, DSA_HEAD_DIM, _KC).astype(o_ref.dtype)
        o_ref[0, c, :, DSA_HEAD_DIM:, :] = jnp.ones((DSA_HEADS, _VT_ROWS - DSA_HEAD_DIM, _KC), o_ref.dtype)


def _values_key_minor(h):
    b, s, _ = h.shape
    width = DSA_HEADS * DSA_HEAD_DIM
    v_block = (2 * CONV_CH + 2 * width) // width
    return pl.pallas_call(
        _vt_kernel,
        grid=(b, s // (_VT_CHUNKS * _KC)),
        in_specs=[pl.BlockSpec((1, _VT_CHUNKS * _KC, width), lambda bi, ci: (bi, ci, v_block))],
        out_specs=pl.BlockSpec((1, _VT_CHUNKS, DSA_HEADS, _VT_ROWS, _KC), lambda bi, ci: (bi, ci, 0, 0, 0)),
        out_shape=jax.ShapeDtypeStruct((b, s // _KC, DSA_HEADS, _VT_ROWS, _KC), BF16),
        compiler_params=_PARALLEL2,
        name="dsa_values_key_minor",
    )(h)


def _dsa_attention(h, bias_tiles):
    b, s, _ = h.shape
    width = DSA_HEADS * DSA_HEAD_DIM
    nblk = s // Q_BLOCK
    topk = min(DSA_TOPK_MAX, s // 4)
    kw_block = (CONV_CH * 2 + 4 * width) // LANES
    vt = _values_key_minor(h)
    return pl.pallas_call(
        functools.partial(_dsa_t_kernel, topk=topk),
        grid=(b, nblk),
        in_specs=[pl.BlockSpec((1, Q_BLOCK, width), lambda bi, qi: (bi, qi, 2)),
                  pl.BlockSpec((1, Q_BLOCK, width), lambda bi, qi: (bi, qi, 5)),
                  pl.BlockSpec((1, Q_BLOCK, LANES), lambda bi, qi: (bi, qi, kw_block)),
                  pl.BlockSpec((1, s, width), lambda bi, qi: (bi, 0, 3)),
                  pl.BlockSpec((1, s // _KC, DSA_HEADS, _VT_ROWS, _KC), lambda bi, qi: (bi, 0, 0, 0, 0)),
                  pl.BlockSpec((1, s, LANES), lambda bi, qi: (bi, 0, kw_block)),
                  pl.BlockSpec(memory_space=pl.ANY)],
        out_specs=pl.BlockSpec((1, Q_BLOCK, width), lambda bi, qi: (bi, qi, 0)),
        out_shape=jax.ShapeDtypeStruct((b, s, width), BF16),
        scratch_shapes=[pltpu.VMEM((nblk + 1, DSA_HEADS, Q_BLOCK, Q_BLOCK), F32),
                        pltpu.VMEM((s // _KC, _KC, Q_BLOCK), jnp.int32),
                        pltpu.VMEM((DSA_HEADS, Q_BLOCK, LANES), BF16),
                        pltpu.VMEM((DSA_HEADS, _VT_ROWS, Q_BLOCK), F32),
                        pltpu.VMEM((_KC, IDX_HEADS * Q_BLOCK), F32),
                        pltpu.VMEM((_KC, IDX_HEADS * Q_BLOCK), F32),
                        pltpu.SemaphoreType.DMA(())],
        compiler_params=pltpu.CompilerParams(dimension_semantics=("arbitrary", "arbitrary")),
        name="dsa_attention",
    )(h, h, h, h, vt, h, bias_tiles)


_DIL_WAVE = 16


def _dil_kernel(*refs, seq):
    n_g = len(DIL_GROUPS)
    qkv = refs[:3 * n_g]
    bias_ref, o_ref, og_ref, lg_ref = refs[3 * n_g:]
    c = pl.program_id(2)
    ii = lax.broadcasted_iota(jnp.int32, (BAND_BLOCK, BAND_BLOCK), 0)
    jj = lax.broadcasted_iota(jnp.int32, (BAND_BLOCK, BAND_BLOCK), 1)
    valid_prev = jj >= ii
    valid_cur = ii >= jj
    scale = DIL_HEAD_DIM ** -0.5
    for g, (window, dil) in enumerate(DIL_GROUPS):
        assert window // dil == BAND_BLOCK
        q_ref, k_ref, v_ref = qkv[3 * g:3 * g + 3]
        sub_len = seq // dil
        blocks_per_chunk = DIL_CHUNK // (BAND_BLOCK * dil)
        bias_prev = bias_ref[g, 0, :, :BAND_BLOCK]
        bias_cur = bias_ref[g, 0, :, BAND_BLOCK:]
        tiles = [(r, nl) for r in range(dil) for nl in range(blocks_per_chunk)]
        for w0 in range(0, len(tiles), _DIL_WAVE):
            wave = tiles[w0:w0 + _DIL_WAVE]
            cur, prev, mask_prev = [], [], []
            for r, nl in wave:
                row0 = pl.multiple_of(r * sub_len + c * (DIL_CHUNK // dil) + nl * BAND_BLOCK, BAND_BLOCK)
                cur.append(pl.ds(row0, BAND_BLOCK))
                prev.append(pl.ds(pl.multiple_of(jnp.maximum(row0 - BAND_BLOCK, 0), BAND_BLOCK), BAND_BLOCK))
                mask_prev.append(valid_prev if nl > 0 else (valid_prev & (c > 0)))
            gather = lambda ref, slices: jnp.stack([ref[0, sl, :] for sl in slices])
            qb = gather(q_ref, cur)
            s_cur = jnp.einsum("wqd,wkd->wqk", qb, gather(k_ref, cur), preferred_element_type=F32)
            s_prev = jnp.einsum("wqd,wkd->wqk", qb, gather(k_ref, prev), preferred_element_type=F32)
            s_cur = jnp.where(valid_cur, s_cur * scale + bias_cur, NEG)
            s_prev = jnp.where(jnp.stack(mask_prev), s_prev * scale + bias_prev, NEG)
            m = jnp.max(jnp.maximum(s_cur, s_prev), axis=2, keepdims=True)
            p_cur = jnp.exp(s_cur - m)
            p_prev = jnp.exp(s_prev - m)
            ssum = jnp.sum(p_cur + p_prev, axis=2, keepdims=True)
            o = (jnp.einsum("wqk,wkd->wqd", p_cur.astype(BF16), gather(v_ref, cur), preferred_element_type=F32)
                 + jnp.einsum("wqk,wkd->wqd", p_prev.astype(BF16), gather(v_ref, prev),
                              preferred_element_type=F32)) / ssum
            lse = m + jnp.log(ssum)
            for w, (r, nl) in enumerate(wave):
                dst = pl.ds(r + nl * BAND_BLOCK * dil, BAND_BLOCK, stride=dil) if dil > 1 else pl.ds(
                    nl * BAND_BLOCK, BAND_BLOCK)
                og_ref[g, dst, :] = o[w]
                lg_ref[g, dst, :] = jnp.broadcast_to(lse[w], (BAND_BLOCK, LANES))
    rows = 256
    for rc in range(DIL_CHUNK // rows):
        sl = slice(rc * rows, (rc + 1) * rows)
        lses = [lg_ref[g, sl, :] for g in range(n_g)]
        mx = functools.reduce(jnp.maximum, lses)
        ws = [jnp.exp(l - mx) for l in lses]
        den = functools.reduce(lambda a, b_: a + b_, ws)
        num = functools.reduce(lambda a, b_: a + b_, [ws[g] * og_ref[g, sl, :] for g in range(n_g)])
        o_ref[0, sl, :] = (num / den).astype(o_ref.dtype)


def _dilated_attention(hs, bias_tiles):
    b, s, _ = hs[0].shape
    in_specs = []
    args = []
    for hg in hs:
        for part in range(3):
            in_specs.append(pl.BlockSpec((1, s, DIL_HEAD_DIM),
                                         lambda bi, hi, ci, part=part: (bi, 0, part * DIL_HEADS + hi)))
            args.append(hg)
    in_specs.append(pl.BlockSpec((len(DIL_GROUPS), 1, BAND_BLOCK, 2 * BAND_BLOCK),
                                 lambda bi, hi, ci: (0, hi, 0, 0)))
    return pl.pallas_call(
        functools.partial(_dil_kernel, seq=s),
        grid=(b, DIL_HEADS, s // DIL_CHUNK),
        in_specs=in_specs,
        out_specs=pl.BlockSpec((1, DIL_CHUNK, DIL_HEAD_DIM), lambda bi, hi, ci: (bi, ci, hi)),
        out_shape=jax.ShapeDtypeStruct((b, s, DIL_HEADS * DIL_HEAD_DIM), BF16),
        scratch_shapes=[pltpu.VMEM((len(DIL_GROUPS), DIL_CHUNK, LANES), F32),
                        pltpu.VMEM((len(DIL_GROUPS), DIL_CHUNK, LANES), F32)],
        compiler_params=pltpu.CompilerParams(dimension_semantics=("parallel", "parallel", "arbitrary")),
        name="dilated_attention",
    )(*args, bias_tiles)


def _outproj_kernel(*refs, n_in, with_router):
    ins = refs[:n_in]
    ws = refs[n_in:2 * n_in]
    x_ref, g_ref, b_ref = refs[2 * n_in:2 * n_in + 3]
    rest = refs[2 * n_in + 3:]
    mix = _dot(ins[0][...], ws[0][...])
    for a_ref, w_ref in zip(ins[1:], ws[1:]):
        mix = mix + _dot(a_ref[...], w_ref[...])
    xn = _layer_norm(DEEPNORM_ALPHA * x_ref[...] + mix, g_ref[...], b_ref[...])
    if not with_router:
        o_ref, ob_ref = rest
    else:
        router_ref, o_ref, gates_ref, counts_ref, run_ref, tri_ref = rest
        ob_ref = None
        tm = xn.shape[0]

        @pl.when(pl.program_id(0) == 0)
        def _():
            run_ref[...] = jnp.zeros_like(run_ref)
            tri_ref[...] = (lax.broadcasted_iota(jnp.int32, (tm, tm), 0)
                            > lax.broadcasted_iota(jnp.int32, (tm, tm), 1)).astype(BF16)

        logits = _dot(xn.astype(BF16), router_ref[...])
        lane = lax.broadcasted_iota(jnp.int32, logits.shape, 1).astype(F32)
        lg = jnp.where(lane < N_EXPERTS, logits, -jnp.inf)
        m1 = jnp.max(lg, axis=1, keepdims=True)
        i1 = jnp.min(jnp.where(lg == m1, lane, float(LANES)), axis=1, keepdims=True)
        lg2 = jnp.where(lane == i1, -jnp.inf, lg)
        m2 = jnp.max(lg2, axis=1, keepdims=True)
        i2 = jnp.min(jnp.where(lg2 == m2, lane, float(LANES)), axis=1, keepdims=True)
        e2 = jnp.exp(m2 - m1)
        den = 1.0 + e2
        oh1 = jnp.where(lane == i1, 1.0, 0.0)
        oh2 = jnp.where(lane == i2, 1.0, 0.0)
        both = oh1 + oh2
        before = _dot(tri_ref[...], both.astype(BF16)) + run_ref[...]
        r1 = jnp.sum(before * oh1, axis=1, keepdims=True)
        r2 = jnp.sum(before * oh2, axis=1, keepdims=True)
        run_ref[...] += jnp.sum(both, axis=0, keepdims=True)
        counts_ref[...] = run_ref[...]
        gates_ref[...] = (jnp.where(lane == 0.0, i1, 0.0) + jnp.where(lane == 1.0, i2, 0.0)
                          + jnp.where(lane == 2.0, 1.0 / den, 0.0) + jnp.where(lane == 3.0, e2 / den, 0.0)
                          + jnp.where(lane == 4.0, r1, 0.0) + jnp.where(lane == 5.0, r2, 0.0))
    o_ref[...] = xn
    if ob_ref is not None:
        ob_ref[...] = xn.astype(BF16)


def _outproj_ln(ins, ws, x, ln_g, ln_b, router=None, tm=512):
    t, d = x.shape
    n_in = len(ins)
    row = lambda width: pl.BlockSpec((tm, width), lambda i: (i, 0))
    full = lambda a: pl.BlockSpec(a.shape, lambda i: (0, 0))
    vec = lambda v: v.reshape(1, d).astype(F32)
    args = list(ins) + list(ws) + [x, vec(ln_g), vec(ln_b)]
    in_specs = [row(a.shape[1]) for a in ins] + [full(w) for w in ws] + [row(d), full(vec(ln_g)), full(vec(ln_b))]
    out_shape = [jax.ShapeDtypeStruct((t, d), F32), jax.ShapeDtypeStruct((t, d), BF16)]
    out_specs = [row(d), row(d)]
    if router is not None:
        router_pad = jnp.zeros((d, LANES), BF16).at[:, :N_EXPERTS].set(router.astype(BF16))
        args.append(router_pad)
        in_specs.append(full(router_pad))
        out_shape = [out_shape[0], jax.ShapeDtypeStruct((t, LANES), F32), jax.ShapeDtypeStruct((1, LANES), F32)]
        out_specs = [row(d), row(LANES), pl.BlockSpec((1, LANES), lambda i: (0, 0))]
        scratch = [pltpu.VMEM((1, LANES), F32), pltpu.VMEM((tm, tm), BF16)]
    else:
        scratch = []
    semantics = "arbitrary" if router is not None else "parallel"
    return pl.pallas_call(
        functools.partial(_outproj_kernel, n_in=n_in, with_router=router is not None),
        grid=(t // tm,),
        in_specs=in_specs,
        out_specs=out_specs,
        out_shape=out_shape,
        scratch_shapes=scratch,
        compiler_params=pltpu.CompilerParams(dimension_semantics=(semantics,)),
        name="outproj_ln",
    )(*args)


def _ffn_kernel(xb_ref, x_ref, wg_ref, wu_ref, wd_ref, g_ref, b_ref, o_ref, ob_ref, acc_ref):
    f = pl.program_id(1)

    @pl.when(f == 0)
    def _():
        acc_ref[...] = jnp.zeros_like(acc_ref)

    xb = xb_ref[...]
    hg = _dot(xb, wg_ref[...])
    hu = _dot(xb, wu_ref[...])
    act = (hg * _sigmoid(hg) * hu).astype(BF16)
    acc_ref[...] += _dot(act, wd_ref[...])

    @pl.when(f == pl.num_programs(1) - 1)
    def _():
        xn = _layer_norm(DEEPNORM_ALPHA * x_ref[...] + acc_ref[...], g_ref[...], b_ref[...])
        o_ref[...] = xn
        ob_ref[...] = xn.astype(BF16)


def _ffn_ln(xb, x, wg, wu, wd, ln_g, ln_b, tm=512, tf=1408):
    t, d = x.shape
    ff = wg.shape[1]
    vec = lambda v: v.reshape(1, d).astype(F32)
    return pl.pallas_call(
        _ffn_kernel,
        grid=(t // tm, ff // tf),
        in_specs=[pl.BlockSpec((tm, d), lambda i, f: (i, 0)),
                  pl.BlockSpec((tm, d), lambda i, f: (i, 0)),
                  pl.BlockSpec((d, tf), lambda i, f: (0, f)),
                  pl.BlockSpec((d, tf), lambda i, f: (0, f)),
                  pl.BlockSpec((tf, d), lambda i, f: (f, 0)),
                  pl.BlockSpec((1, d), lambda i, f: (0, 0)),
                  pl.BlockSpec((1, d), lambda i, f: (0, 0))],
        out_specs=[pl.BlockSpec((tm, d), lambda i, f: (i, 0)),
                   pl.BlockSpec((tm, d), lambda i, f: (i, 0))],
        out_shape=[jax.ShapeDtypeStruct((t, d), F32), jax.ShapeDtypeStruct((t, d), BF16)],
        scratch_shapes=[pltpu.VMEM((tm, d), F32)],
        compiler_params=pltpu.CompilerParams(dimension_semantics=("parallel", "arbitrary")),
        name="ffn_ln",
    )(xb, x, wg, wu, wd, vec(ln_g), vec(ln_b))


_MOE_TM = 1024


def _route(route, counts, n_tiles):
    expert = route[:, 0:2].astype(jnp.int32)
    rank = route[:, 4:6].astype(jnp.int32)
    counts = counts[0, :N_EXPERTS].astype(jnp.int32)
    padded = (counts + _MOE_TM - 1) // _MOE_TM * _MOE_TM
    ends = jnp.cumsum(padded)
    starts = ends - padded
    onehot = expert[:, :, None] == jnp.arange(N_EXPERTS, dtype=jnp.int32)
    pos = (rank + jnp.sum(jnp.where(onehot, starts, 0), axis=-1)).reshape(-1)
    tile_start = jnp.arange(n_tiles, dtype=jnp.int32) * _MOE_TM
    tile_expert = jnp.minimum(jnp.sum((tile_start[:, None] >= ends[None, :]).astype(jnp.int32), axis=1),
                              N_EXPERTS - 1)
    return pos.astype(jnp.int32), tile_expert.astype(jnp.int32)


def _row_scatter_kernel(pos_ref, x_ref, init_ref, o_ref, sem, *, tm):
    del init_ref

    def body(j, carry):
        for choice in range(2):
            p = pos_ref[0, 0, 2 * j + choice]
            pltpu.make_async_copy(x_ref.at[pl.ds(j, 1)], o_ref.at[pl.ds(p, 1)], sem).start()
        return carry

    lax.fori_loop(0, tm, body, 0, unroll=8)
    for _ in range(2):
        pltpu.make_async_copy(x_ref, o_ref.at[pl.ds(0, tm)], sem).wait()


def _row_scatter(x, pos, init, tm=512):
    t, d = x.shape
    n_rows = init.shape[0]
    return pl.pallas_call(
        functools.partial(_row_scatter_kernel, tm=tm),
        grid=(t // tm,),
        in_specs=[pl.BlockSpec((1, 1, 2 * tm), lambda i: (i, 0, 0), memory_space=pltpu.SMEM),
                  pl.BlockSpec((tm, d), lambda i: (i, 0)),
                  pl.BlockSpec(memory_space=pl.ANY)],
        out_specs=pl.BlockSpec(memory_space=pl.ANY),
        out_shape=jax.ShapeDtypeStruct((n_rows, d), x.dtype),
        scratch_shapes=[pltpu.SemaphoreType.DMA(())],
        input_output_aliases={2: 0},
        compiler_params=pltpu.CompilerParams(dimension_semantics=("arbitrary",)),
        name="moe_row_scatter",
    )(pos.reshape(t // tm, 1, 2 * tm), x, init)


def _expert_ffn_kernel(te_ref, x_ref, wg_ref, wu_ref, wd_ref, o_ref, acc_ref):
    del te_ref
    f = pl.program_id(1)

    @pl.when(f == 0)
    def _():
        acc_ref[...] = jnp.zeros_like(acc_ref)

    xb = x_ref[...].astype(BF16)
    hg = _dot(xb, wg_ref[0].astype(BF16))
    hu = _dot(xb, wu_ref[0].astype(BF16))
    act = (hg * _sigmoid(hg) * hu).astype(BF16)
    acc_ref[...] += _dot(act, wd_ref[0].astype(BF16))

    @pl.when(f == pl.num_programs(1) - 1)
    def _():
        o_ref[...] = acc_ref[...]


def _expert_ffn(xs, tile_expert, wg, wu, wd, tf=512):
    n_rows, d = xs.shape
    ff = wg.shape[2]
    grid_spec = pltpu.PrefetchScalarGridSpec(
        num_scalar_prefetch=1,
        grid=(n_rows // _MOE_TM, ff // tf),
        in_specs=[pl.BlockSpec((_MOE_TM, d), lambda i, f, te: (i, 0)),
                  pl.BlockSpec((1, d, tf), lambda i, f, te: (te[i], 0, f)),
                  pl.BlockSpec((1, d, tf), lambda i, f, te: (te[i], 0, f)),
                  pl.BlockSpec((1, tf, d), lambda i, f, te: (te[i], f, 0))],
        out_specs=pl.BlockSpec((_MOE_TM, d), lambda i, f, te: (i, 0)),
        scratch_shapes=[pltpu.VMEM((_MOE_TM, d), F32)])
    return pl.pallas_call(
        _expert_ffn_kernel,
        grid_spec=grid_spec,
        out_shape=jax.ShapeDtypeStruct((n_rows, d), F32),
        compiler_params=pltpu.CompilerParams(dimension_semantics=("parallel", "arbitrary")),
        name="moe_expert_ffn",
    )(tile_expert, xs, wg, wu, wd)


def _combine_kernel(pos_ref, pos_next_ref, route_ref, x_ref, g_ref, b_ref, y_ref, o_ref, ybuf_ref, sems, *, tm):
    i = pl.program_id(0)
    n = pl.num_programs(0)

    def gather(src_pos_ref, slot):
        def body(j, carry):
            for choice in range(2):
                p = src_pos_ref[0, 0, 2 * j + choice]
                pltpu.make_async_copy(y_ref.at[pl.ds(p, 1)], ybuf_ref.at[slot, choice, pl.ds(j, 1)],
                                      sems.at[slot]).start()
            return carry

        lax.fori_loop(0, tm, body, 0, unroll=8)

    @pl.when(i == 0)
    def _():
        gather(pos_ref, 0)

    slot = i % 2

    @pl.when(i + 1 < n)
    def _():
        gather(pos_next_ref, 1 - slot)

    for choice in range(2):
        pltpu.make_async_copy(y_ref.at[pl.ds(0, tm)], ybuf_ref.at[slot, choice], sems.at[slot]).wait()
    route = route_ref[...]
    y = route[:, 2:3] * ybuf_ref[slot, 0] + route[:, 3:4] * ybuf_ref[slot, 1]
    o_ref[...] = _layer_norm(DEEPNORM_ALPHA * x_ref[...] + y, g_ref[...], b_ref[...])


def _combine_ln(ys, pos, route, x, ln_g, ln_b, tm=512):
    t, d = x.shape
    vec = lambda v: v.reshape(1, d).astype(F32)
    last = t // tm - 1
    pos3 = pos.reshape(t // tm, 1, 2 * tm)
    return pl.pallas_call(
        functools.partial(_combine_kernel, tm=tm),
        grid=(t // tm,),
        in_specs=[pl.BlockSpec((1, 1, 2 * tm), lambda i: (i, 0, 0), memory_space=pltpu.SMEM),
                  pl.BlockSpec((1, 1, 2 * tm), lambda i: (jnp.minimum(i + 1, last), 0, 0), memory_space=pltpu.SMEM),
                  pl.BlockSpec((tm, LANES), lambda i: (i, 0)),
                  pl.BlockSpec((tm, d), lambda i: (i, 0)),
                  pl.BlockSpec((1, d), lambda i: (0, 0)),
                  pl.BlockSpec((1, d), lambda i: (0, 0)),
                  pl.BlockSpec(memory_space=pl.ANY)],
        out_specs=pl.BlockSpec((tm, d), lambda i: (i, 0)),
        out_shape=jax.ShapeDtypeStruct((t, d), F32),
        scratch_shapes=[pltpu.VMEM((2, 2, tm, d), F32), pltpu.SemaphoreType.DMA((2,))],
        compiler_params=pltpu.CompilerParams(dimension_semantics=("arbitrary",)),
        name="moe_combine_ln",
    )(pos3, pos3, route, x, vec(ln_g), vec(ln_b), ys)


def _moe_ln(x, route, counts, wg, wu, wd, ln_g, ln_b):
    t, _ = x.shape
    n_tiles = 2 * t // _MOE_TM + N_EXPERTS
    pos, tile_expert = _route(route, counts, n_tiles)
    xs = _row_scatter(x, pos, jnp.zeros((n_tiles * _MOE_TM, x.shape[1]), x.dtype))
    ys = _expert_ffn(xs, tile_expert, wg, wu, wd)
    return _combine_ln(ys, pos, route, x, ln_g, ln_b)


def _even_layer(x, xb, batch, seq, dsa_bias, w_in, conv_w, conv_b, conv_ln_g, conv_ln_b, w_out,
                ln1_g, ln1_b, ffn_wg, ffn_wu, ffn_wd, ln2_g, ln2_b):
    w_in_p = jnp.zeros((D_MODEL, EVEN_IN_PAD), BF16).at[:, :w_in.shape[1]].set(w_in.astype(BF16))
    h = _matmul(x if xb is None else xb, w_in_p, 1024, EVEN_IN_PAD, BF16).reshape(batch, seq, EVEN_IN_PAD)
    a_out = _conformer_conv(h, conv_w, conv_b, conv_ln_g, conv_ln_b)
    att = _dsa_attention(h, dsa_bias)
    w_out_b = w_out.astype(BF16)
    x1, x1b = _outproj_ln([a_out.reshape(-1, CONV_CH), att.reshape(-1, DSA_HEADS * DSA_HEAD_DIM)],
                          [w_out_b[:CONV_CH], w_out_b[CONV_CH:]], x, ln1_g, ln1_b)
    return _ffn_ln(x1b, x1, ffn_wg.astype(BF16), ffn_wu.astype(BF16), ffn_wd.astype(BF16), ln2_g, ln2_b)


def _odd_layer(x, xb, batch, seq, dil_bias, w_in, w_out, ln1_g, ln1_b, router, moe_wg, moe_wu, moe_wd,
               ln2_g, ln2_b):
    gw = 3 * DIL_HEADS * DIL_HEAD_DIM
    w_in_b = w_in.astype(BF16)
    hs = [_matmul_residue_major(x if xb is None else xb, w_in_b[:, g * gw:(g + 1) * gw], batch, seq, dil, 2048, 768)
          for g, (_, dil) in enumerate(DIL_GROUPS)]
    o = _dilated_attention(hs, dil_bias).reshape(-1, DIL_HEADS * DIL_HEAD_DIM)
    x1, route, counts = _outproj_ln([o], [w_out.astype(BF16)], x, ln1_g, ln1_b, router=router)
    out = _moe_ln(x1, route, counts, moe_wg, moe_wu, moe_wd, ln2_g, ln2_b)
    return out, out.astype(BF16)


def kernel(x, rel_bias, even_w_in, even_conv_w, even_conv_b, even_conv_ln_g, even_conv_ln_b, even_w_out, even_ln1_g, even_ln1_b, even_ffn_wg, even_ffn_wu, even_ffn_wd, even_ln2_g, even_ln2_b, odd_w_in, odd_w_out, odd_ln1_g, odd_ln1_b, odd_router, odd_moe_wg, odd_moe_wu, odd_moe_wd, odd_ln2_g, odd_ln2_b):
    batch, seq, d = x.shape
    assert d == D_MODEL and seq % DIL_CHUNK == 0
    depth = even_w_in.shape[0] + odd_w_in.shape[0]
    assert depth == DEPTH
    dsa_bias = _dsa_bias_tiles(rel_bias, seq)
    dil_bias = _dil_bias_tiles(rel_bias)
    xf = x.reshape(batch * seq, d)
    xb = None
    for layer in range(depth):
        i = layer // 2
        if layer % 2 == 0:
            xf, xb = _even_layer(xf, xb, batch, seq, dsa_bias, even_w_in[i], even_conv_w[i], even_conv_b[i],
                                 even_conv_ln_g[i], even_conv_ln_b[i], even_w_out[i], even_ln1_g[i],
                                 even_ln1_b[i], even_ffn_wg[i], even_ffn_wu[i], even_ffn_wd[i],
                                 even_ln2_g[i], even_ln2_b[i])
        else:
            xf, xb = _odd_layer(xf, xb, batch, seq, dil_bias, odd_w_in[i], odd_w_out[i], odd_ln1_g[i],
                                odd_ln1_b[i], odd_router[i], odd_moe_wg[i], odd_moe_wu[i], odd_moe_wd[i],
                                odd_ln2_g[i], odd_ln2_b[i])
    return xf.reshape(batch, seq, d)
```

```python
import functools
import math

import jax
import jax.numpy as jnp
from jax import lax
from jax.experimental import pallas as pl
from jax.experimental.pallas import tpu as pltpu

BF16 = jnp.bfloat16
F32 = jnp.float32

D_MODEL = 1024
CONV_CH = 512
CONV_WIDTH = 31
DSA_HEADS = 8
DSA_HEAD_DIM = 64
IDX_HEADS = 8
IDX_DIM = 64
DSA_TOPK_MAX = 256
Q_BLOCK = 128
DIL_GROUPS = ((128, 1), (512, 4), (2048, 16))
DIL_HEADS = 8
DIL_HEAD_DIM = 128
BAND_BLOCK = 128
NUM_BUCKETS = 32
MAX_DISTANCE = 2048
N_EXPERTS = 8
LN_EPS = 1e-5
NEG = -1e30
DEPTH = 2
DEEPNORM_ALPHA = (2 * DEPTH) ** 0.25

LANES = 128
EVEN_IN_PAD = 3200
INT_MIN = -(2 ** 31)
DIL_CHUNK = BAND_BLOCK * max(d for _, d in DIL_GROUPS)

_PARALLEL2 = pltpu.CompilerParams(dimension_semantics=("parallel", "parallel"))


def _layer_norm(y, g, b):
    mu = jnp.mean(y, axis=-1, keepdims=True)
    yc = y - mu
    var = jnp.mean(yc * yc, axis=-1, keepdims=True)
    return yc * lax.rsqrt(var + LN_EPS) * g + b


def _sigmoid(x):
    return 1.0 / (1.0 + jnp.exp(-x))


def _dot_nt(a, b):
    return lax.dot_general(a, b, (((1,), (1,)), ((), ())), preferred_element_type=F32)


def _dot(a, b):
    return jnp.dot(a, b, preferred_element_type=F32)


def _mm_kernel(x_ref, w_ref, o_ref):
    o_ref[...] = _dot(x_ref[...].astype(BF16), w_ref[...]).astype(o_ref.dtype)


def _matmul(x, w, tm, tn, out_dtype):
    t, k = x.shape
    n = w.shape[1]
    return pl.pallas_call(
        _mm_kernel,
        grid=(t // tm, n // tn),
        in_specs=[pl.BlockSpec((tm, k), lambda i, j: (i, 0)),
                  pl.BlockSpec((k, tn), lambda i, j: (0, j))],
        out_specs=pl.BlockSpec((tm, tn), lambda i, j: (i, j)),
        out_shape=jax.ShapeDtypeStruct((t, n), out_dtype),
        compiler_params=_PARALLEL2,
        name="matmul",
    )(x, w)


_MAX_ROW_STRIDE = 4


def _mm_perm_kernel(x_ref, w_ref, o_ref, xp_ref, a_ref, b_ref, *, dil, tm):
    rows = tm // dil

    @pl.when(pl.program_id(1) == 0)
    def _():
        for c in range(x_ref.shape[1] // LANES):
            cols = slice(c * LANES, (c + 1) * LANES)
            a_ref[...] = x_ref[:, cols].astype(F32)
            if dil <= _MAX_ROW_STRIDE:
                for r in range(dil):
                    xp_ref[r * rows:(r + 1) * rows, cols] = a_ref[pl.ds(r, rows, stride=dil), :].astype(BF16)
                continue
            s1 = _MAX_ROW_STRIDE
            s2 = dil // s1
            assert s2 <= _MAX_ROW_STRIDE
            rows1 = tm // s1
            for r1 in range(s1):
                b_ref[r1 * rows1:(r1 + 1) * rows1, :] = a_ref[pl.ds(r1, rows1, stride=s1), :]
            for r1 in range(s1):
                for r2 in range(s2):
                    r = s1 * r2 + r1
                    xp_ref[r * rows:(r + 1) * rows, cols] = (
                        b_ref[pl.ds(r1 * rows1 + r2, rows, stride=s2), :].astype(BF16))

    acc = _dot(xp_ref[...], w_ref[...])
    o_ref[0] = acc.reshape(dil, rows, acc.shape[1]).astype(o_ref.dtype)


def _matmul_residue_major(x, w, batch, seq, dil, tm, tn):
    t, k = x.shape
    n = w.shape[1]
    if dil == 1:
        return _matmul(x, w, tm, tn, BF16).reshape(batch, seq, n)
    tiles_per_seq = seq // tm
    out = pl.pallas_call(
        functools.partial(_mm_perm_kernel, dil=dil, tm=tm),
        grid=(t // tm, n // tn),
        in_specs=[pl.BlockSpec((tm, k), lambda i, j: (i, 0)),
                  pl.BlockSpec((k, tn), lambda i, j: (0, j))],
        out_specs=pl.BlockSpec((1, dil, tm // dil, tn),
                               lambda i, j: (i // tiles_per_seq, 0, i % tiles_per_seq, j)),
        out_shape=jax.ShapeDtypeStruct((batch, dil, seq // dil, n), BF16),
        scratch_shapes=[pltpu.VMEM((tm, k), BF16), pltpu.VMEM((tm, LANES), F32), pltpu.VMEM((tm, LANES), F32)],
        compiler_params=pltpu.CompilerParams(dimension_semantics=("parallel", "arbitrary")),
        name="matmul_residue_major",
    )(x, w)
    return out.reshape(batch, seq, n)


_CONV_HALO = 32
_CONV_ROWS = 128
_SUBLANES = 8


def _conv_kernel(val_ref, gate_ref, w_ref, cb_ref, g_ref, b_ref, o_ref, ext_ref, *, ts):
    s = pl.program_id(1)

    @pl.when(s == 0)
    def _():
        ext_ref[0, 0:_CONV_HALO, :] = jnp.zeros((_CONV_HALO, CONV_CH), F32)

    val = val_ref[0].astype(F32)
    gate = gate_ref[0].astype(F32)
    ext_ref[0, _CONV_HALO:_CONV_HALO + ts, :] = val * _sigmoid(gate)
    span = ts + _CONV_HALO - _SUBLANES
    for k in range(1, _SUBLANES):
        ext_ref[k, 0:span, :] = ext_ref[0, k:k + span, :]
    first_tap = _CONV_HALO - (CONV_WIDTH - 1)
    for rc in range(ts // _CONV_ROWS):
        r0 = rc * _CONV_ROWS
        acc = jnp.zeros((_CONV_ROWS, CONV_CH), F32)
        for j in range(CONV_WIDTH):
            shift = (first_tap + j) % _SUBLANES
            lo = r0 + first_tap + j - shift
            acc = acc + ext_ref[shift, lo:lo + _CONV_ROWS, :] * w_ref[j:j + 1, :]
        y = _layer_norm(acc + cb_ref[...], g_ref[...], b_ref[...])
        o_ref[0, r0:r0 + _CONV_ROWS, :] = (y * _sigmoid(y)).astype(o_ref.dtype)
    ext_ref[0, 0:_CONV_HALO, :] = ext_ref[0, ts:ts + _CONV_HALO, :]


def _conformer_conv(h, conv_w, conv_b, ln_g, ln_b, ts=512):
    b, s, _ = h.shape
    w_pad = jnp.zeros((32, CONV_CH), F32).at[:CONV_WIDTH].set(conv_w)
    vec = lambda v: v.reshape(1, CONV_CH).astype(F32)
    full = lambda shape: pl.BlockSpec(shape, lambda bi, si: (0, 0))
    return pl.pallas_call(
        functools.partial(_conv_kernel, ts=ts),
        grid=(b, s // ts),
        in_specs=[pl.BlockSpec((1, ts, CONV_CH), lambda bi, si: (bi, si, 0)),
                  pl.BlockSpec((1, ts, CONV_CH), lambda bi, si: (bi, si, 1)),
                  full((32, CONV_CH)), full((1, CONV_CH)), full((1, CONV_CH)), full((1, CONV_CH))],
        out_specs=pl.BlockSpec((1, ts, CONV_CH), lambda bi, si: (bi, si, 0)),
        out_shape=jax.ShapeDtypeStruct((b, s, CONV_CH), BF16),
        scratch_shapes=[pltpu.VMEM((_SUBLANES, ts + _CONV_HALO, CONV_CH), F32)],
        compiler_params=pltpu.CompilerParams(dimension_semantics=("parallel", "arbitrary")),
        name="conformer_conv",
    )(h, h, w_pad, vec(conv_b), vec(ln_g), vec(ln_b))


def _rel_bucket(dist):
    max_exact = NUM_BUCKETS // 2
    n = dist.astype(jnp.int32)
    nf = jnp.maximum(n, 1).astype(F32)
    large = max_exact + (jnp.log(nf / max_exact) / math.log(MAX_DISTANCE / max_exact)
                         * (NUM_BUCKETS - max_exact)).astype(jnp.int32)
    large = jnp.minimum(large, NUM_BUCKETS - 1)
    return jnp.where(n < max_exact, n, large)


def _bias_by_distance(rel_bias, n):
    return rel_bias[_rel_bucket(jnp.arange(n, dtype=jnp.int32))].astype(F32).T


def _toeplitz_kernel(win_ref, o_ref):
    n_win, span = win_ref.shape
    for n in range(n_win):
        y = jnp.broadcast_to(win_ref[n:n + 1, :], (LANES, span))
        y = pltpu.roll(y, 0, 1, stride=1, stride_axis=0)
        o_ref[n] = y[:, LANES:]


def _toeplitz_tiles(windows, per_step=8):
    n, span = windows.shape
    width = span - LANES
    return pl.pallas_call(
        _toeplitz_kernel,
        grid=(n // per_step,),
        in_specs=[pl.BlockSpec((per_step, span), lambda i: (i, 0))],
        out_specs=pl.BlockSpec((per_step, LANES, width), lambda i: (i, 0, 0)),
        out_shape=jax.ShapeDtypeStruct((n, LANES, width), F32),
        compiler_params=pltpu.CompilerParams(dimension_semantics=("parallel",)),
        name="bias_toeplitz",
    )(windows)


def _dsa_bias_tiles(rel_bias, seq):
    nblk = seq // Q_BLOCK
    heads = rel_bias.shape[1]
    f = _bias_by_distance(rel_bias, seq)
    u = jnp.concatenate([jnp.broadcast_to(f[:, :1], (heads, 2 * Q_BLOCK)), f], axis=1)
    blocks = u.reshape(heads, nblk + 2, Q_BLOCK)
    windows = jnp.concatenate([blocks[:, :-1], blocks[:, 1:]], axis=-1)
    windows = windows.transpose(1, 0, 2).reshape((nblk + 1) * heads, 2 * Q_BLOCK)
    return _toeplitz_tiles(windows).reshape(nblk + 1, heads, Q_BLOCK, Q_BLOCK)


def _dil_bias_tiles(rel_bias):
    max_dil = max(d for _, d in DIL_GROUPS)
    heads = rel_bias.shape[1]
    span = 2 * BAND_BLOCK
    f = _bias_by_distance(rel_bias, span * max_dil + 1)
    windows = []
    for _, dil in DIL_GROUPS:
        fd = f[:, 0:span * dil + 1:dil]
        tail = jnp.broadcast_to(fd[:, :1], (heads, BAND_BLOCK - 1))
        windows.append(jnp.concatenate([fd[:, ::-1], tail], axis=1))
    tiles = _toeplitz_tiles(jnp.concatenate(windows, axis=0))
    return tiles.reshape(len(DIL_GROUPS), heads, BAND_BLOCK, span)


_KC = 2 * Q_BLOCK
_VT_ROWS = DSA_HEAD_DIM + 16


def _sortable_key(score):
    score = jnp.where(score == 0.0, 0.0, score)
    bits = lax.bitcast_convert_type(score, jnp.int32)
    return jnp.where(bits < 0, bits ^ jnp.int32(0x7FFFFFFF), bits)


def _rows_reduce(x, op):
    rows = x.shape[0]
    return op(op(x.reshape(rows // 8, 8, LANES), axis=0), axis=0, keepdims=True)


def _keys_reduce(x, op):
    h, keys, _ = x.shape
    return op(op(x.reshape(h, keys // 8, 8, LANES), axis=1), axis=1, keepdims=True)


def _dsa_t_kernel(q_ref, qi_ref, wq_ref, k_ref, vt_ref, ki_ref, bias_hbm, o_ref,
                  bias_ref, keys_ref, qm_ref, acc_ref, dots_a_ref, dots_b_ref, sem, *, topk):
    b = pl.program_id(0)
    i = pl.program_id(1)

    @pl.when((b == 0) & (i == 0))
    def _():
        cp = pltpu.make_async_copy(bias_hbm, bias_ref, sem)
        cp.start()
        cp.wait()

    nch = (i + 2) // 2
    key_off = lax.broadcasted_iota(jnp.int32, (_KC, LANES), 0)
    q_pos = i * Q_BLOCK + lax.broadcasted_iota(jnp.int32, (_KC, LANES), 1)
    lane = lax.broadcasted_iota(jnp.int32, (Q_BLOCK, LANES), 1)
    low_half = lane < DSA_HEAD_DIM

    w_t = (wq_ref[0].astype(F32) * (IDX_HEADS ** -0.5 * IDX_DIM ** -0.5)).T
    qi = qi_ref[0].astype(F32)
    qi_heads = []
    for p in range(IDX_HEADS // 2):
        pair = qi[:, p * LANES:(p + 1) * LANES]
        qi_heads.append(jnp.where(low_half, pair, 0.0))
        qi_heads.append(jnp.where(low_half, pltpu.roll(pair, IDX_DIM, 1), 0.0))
    qi_stack = jnp.concatenate(qi_heads, axis=0).astype(BF16)
    q_scaled = (q_ref[0].astype(F32) * (DSA_HEAD_DIM ** -0.5)).astype(BF16)
    for p in range(DSA_HEADS // 2):
        qp = q_scaled[:, p * LANES:(p + 1) * LANES]
        qm_ref[2 * p] = jnp.where(low_half, qp, jnp.zeros_like(qp))
        qm_ref[2 * p + 1] = jnp.where(low_half, jnp.zeros_like(qp), qp)

    def index_dots(c, dst_ref):
        c = jnp.minimum(c, nch - 1)
        k0 = pl.multiple_of(c * _KC, _KC)
        ki = ki_ref[0, pl.ds(k0, _KC), :]
        dst_ref[...] = _dot_nt(ki, qi_stack)

    def consume_dots(c, src_ref):
        k0 = c * _KC
        score = jnp.zeros((_KC, LANES), F32)
        for h in range(IDX_HEADS):
            w_row = w_t[IDX_DIM + h:IDX_DIM + h + 1, :]
            score = score + w_row * jnp.maximum(src_ref[:, h * Q_BLOCK:(h + 1) * Q_BLOCK], 0.0)
        keys_ref[c] = jnp.where(k0 + key_off <= q_pos, _sortable_key(score), jnp.int32(INT_MIN))

    index_dots(0, dots_a_ref)

    def score_body(c2, carry):
        index_dots(2 * c2 + 1, dots_b_ref)
        consume_dots(2 * c2, dots_a_ref)
        index_dots(2 * c2 + 2, dots_a_ref)
        consume_dots(2 * c2 + 1, dots_b_ref)
        return carry

    lax.fori_loop(0, (nch + 1) // 2, score_body, 0)

    def count_ge(cand):
        cb = jnp.broadcast_to(cand, (8, LANES))
        chains = 4

        def body(c2, acc):
            for half in range(2):
                kk = keys_ref[2 * c2 + half].reshape(_KC // (8 * chains), chains, 8, LANES)
                acc = acc + jnp.sum(jnp.where(kk >= cb, 1.0, 0.0), axis=0)
            return acc

        acc = lax.fori_loop(0, (nch + 1) // 2, body, jnp.zeros((chains, 8, LANES), F32))
        return jnp.sum(jnp.sum(acc, axis=0), axis=0, keepdims=True)

    kf = float(topk)
    zero = jnp.zeros((1, LANES), jnp.int32)
    n_zero = count_ge(zero)
    lowest = jnp.full((1, LANES), INT_MIN, jnp.int32)
    start = (jnp.where(n_zero >= kf, zero, lowest), jnp.where(n_zero >= kf, n_zero, kf))

    def bit_body(t, carry):
        thr, n_ge = carry
        cand = thr | jnp.left_shift(jnp.int32(1), 30 - t)
        n_cand = count_ge(cand)
        take = n_cand >= kf
        return jnp.where(take, cand, thr), jnp.where(take, n_cand, n_ge)

    thr, n_ge = lax.fori_loop(0, 31, bit_body, start)
    tied = jnp.where((n_ge > kf) & (thr > lowest), 1.0, 0.0)
    any_tied = jnp.max(tied) > 0.0

    acc_ref[...] = jnp.zeros_like(acc_ref)

    def selection_with_ties():
        n_gt = count_ge(jnp.where(thr == jnp.int32(2 ** 31 - 1), thr, thr + 1))
        need = kf - n_gt
        tri = (lax.broadcasted_iota(jnp.int32, (_KC, _KC), 1)
               <= lax.broadcasted_iota(jnp.int32, (_KC, _KC), 0)).astype(BF16)

        def select(kk, tie_carry):
            causal = kk > jnp.int32(INT_MIN)
            eq = (kk == thr) & causal
            eq_f = jnp.where(eq, 1.0, 0.0)
            rank = _dot(tri, eq_f.astype(BF16)) + tie_carry
            sel = ((kk > thr) & causal) | (eq & (rank <= need))
            return sel, tie_carry + _rows_reduce(eq_f, jnp.sum)

        return select

    def selection_without_ties():
        return lambda kk, tie_carry: ((kk >= thr) & (kk > jnp.int32(INT_MIN)), tie_carry)

    def attention(select):
        lax.fori_loop(0, nch, functools.partial(attn_body, select),
                      (jnp.full((DSA_HEADS, 1, LANES), NEG, F32), jnp.zeros((1, LANES), F32)))

    def attn_body(select, c, carry):
        ms, tie_carry = carry
        k0 = pl.multiple_of(c * _KC, _KC)
        sel, tie_carry = select(keys_ref[c], tie_carry)
        d = i - 2 * c
        kc = k_ref[0, pl.ds(k0, _KC), :]
        k_heads = jnp.stack([kc[:, (h // 2) * LANES:(h // 2 + 1) * LANES] for h in range(DSA_HEADS)])
        s = jnp.einsum("hkd,hqd->hkq", k_heads, qm_ref[...], preferred_element_type=F32)
        bias = jnp.concatenate([bias_ref[d + 1], bias_ref[d]], axis=1)
        s = jnp.where(sel, s + bias, NEG)
        m_new = jnp.maximum(ms, _keys_reduce(s, jnp.max))
        alpha = jnp.exp(ms - m_new)
        pr = jnp.exp((s - m_new).astype(BF16))
        pv = jnp.einsum("hdk,hkq->hdq", vt_ref[0, c], pr, preferred_element_type=F32)
        acc_ref[...] = alpha * acc_ref[...] + pv
        return m_new, tie_carry

    lax.cond(any_tied, lambda: attention(selection_with_ties()), lambda: attention(selection_without_ties()))
    acc = acc_ref[...]
    out_t = acc[:, :DSA_HEAD_DIM] / acc[:, DSA_HEAD_DIM:DSA_HEAD_DIM + 1]
    o_ref[0] = out_t.reshape(DSA_HEADS * DSA_HEAD_DIM, Q_BLOCK).T.astype(o_ref.dtype)


_VT_CHUNKS = 4


def _vt_kernel(v_ref, o_ref):
    for c in range(_VT_CHUNKS):
        vt = v_ref[0, c * _KC:(c + 1) * _KC, :].astype(F32).T
        o_ref[0, c, :, :DSA_HEAD_DIM, :] = vt.reshape(DSA_HEADS, DSA_HEAD_DIM, _KC).astype(o_ref.dtype)
        o_ref[0, c, :, DSA_HEAD_DIM:, :] = jnp.ones((DSA_HEADS, _VT_ROWS - DSA_HEAD_DIM, _KC), o_ref.dtype)


def _values_key_minor(h):
    b, s, _ = h.shape
    width = DSA_HEADS * DSA_HEAD_DIM
    v_block = (2 * CONV_CH + 2 * width) // width
    return pl.pallas_call(
        _vt_kernel,
        grid=(b, s // (_VT_CHUNKS * _KC)),
        in_specs=[pl.BlockSpec((1, _VT_CHUNKS * _KC, width), lambda bi, ci: (bi, ci, v_block))],
        out_specs=pl.BlockSpec((1, _VT_CHUNKS, DSA_HEADS, _VT_ROWS, _KC), lambda bi, ci: (bi, ci, 0, 0, 0)),
        out_shape=jax.ShapeDtypeStruct((b, s // _KC, DSA_HEADS, _VT_ROWS, _KC), BF16),
        compiler_params=_PARALLEL2,
        name="dsa_values_key_minor",
    )(h)


def _dsa_attention(h, bias_tiles):
    b, s, _ = h.shape
    width = DSA_HEADS * DSA_HEAD_DIM
    nblk = s // Q_BLOCK
    topk = min(DSA_TOPK_MAX, s // 4)
    kw_block = (CONV_CH * 2 + 4 * width) // LANES
    vt = _values_key_minor(h)
    return pl.pallas_call(
        functools.partial(_dsa_t_kernel, topk=topk),
        grid=(b, nblk),
        in_specs=[pl.BlockSpec((1, Q_BLOCK, width), lambda bi, qi: (bi, qi, 2)),
                  pl.BlockSpec((1, Q_BLOCK, width), lambda bi, qi: (bi, qi, 5)),
                  pl.BlockSpec((1, Q_BLOCK, LANES), lambda bi, qi: (bi, qi, kw_block)),
                  pl.BlockSpec((1, s, width), lambda bi, qi: (bi, 0, 3)),
                  pl.BlockSpec((1, s // _KC, DSA_HEADS, _VT_ROWS, _KC), lambda bi, qi: (bi, 0, 0, 0, 0)),
                  pl.BlockSpec((1, s, LANES), lambda bi, qi: (bi, 0, kw_block)),
                  pl.BlockSpec(memory_space=pl.ANY)],
        out_specs=pl.BlockSpec((1, Q_BLOCK, width), lambda bi, qi: (bi, qi, 0)),
        out_shape=jax.ShapeDtypeStruct((b, s, width), BF16),
        scratch_shapes=[pltpu.VMEM((nblk + 1, DSA_HEADS, Q_BLOCK, Q_BLOCK), F32),
                        pltpu.VMEM((s // _KC, _KC, Q_BLOCK), jnp.int32),
                        pltpu.VMEM((DSA_HEADS, Q_BLOCK, LANES), BF16),
                        pltpu.VMEM((DSA_HEADS, _VT_ROWS, Q_BLOCK), F32),
                        pltpu.VMEM((_KC, IDX_HEADS * Q_BLOCK), F32),
                        pltpu.VMEM((_KC, IDX_HEADS * Q_BLOCK), F32),
                        pltpu.SemaphoreType.DMA(())],
        compiler_params=pltpu.CompilerParams(dimension_semantics=("arbitrary", "arbitrary")),
        name="dsa_attention",
    )(h, h, h, h, vt, h, bias_tiles)


_DIL_WAVE = 16


def _dil_kernel(*refs, seq):
    n_g = len(DIL_GROUPS)
    qkv = refs[:3 * n_g]
    bias_ref, o_ref, og_ref, lg_ref = refs[3 * n_g:]
    c = pl.program_id(2)
    ii = lax.broadcasted_iota(jnp.int32, (BAND_BLOCK, BAND_BLOCK), 0)
    jj = lax.broadcasted_iota(jnp.int32, (BAND_BLOCK, BAND_BLOCK), 1)
    valid_prev = jj >= ii
    valid_cur = ii >= jj
    scale = DIL_HEAD_DIM ** -0.5
    for g, (window, dil) in enumerate(DIL_GROUPS):
        assert window // dil == BAND_BLOCK
        q_ref, k_ref, v_ref = qkv[3 * g:3 * g + 3]
        sub_len = seq // dil
        blocks_per_chunk = DIL_CHUNK // (BAND_BLOCK * dil)
        bias_prev = bias_ref[g, 0, :, :BAND_BLOCK]
        bias_cur = bias_ref[g, 0, :, BAND_BLOCK:]
        tiles = [(r, nl) for r in range(dil) for nl in range(blocks_per_chunk)]
        for w0 in range(0, len(tiles), _DIL_WAVE):
            wave = tiles[w0:w0 + _DIL_WAVE]
            cur, prev, mask_prev = [], [], []
            for r, nl in wave:
                row0 = pl.multiple_of(r * sub_len + c * (DIL_CHUNK // dil) + nl * BAND_BLOCK, BAND_BLOCK)
                cur.append(pl.ds(row0, BAND_BLOCK))
                prev.append(pl.ds(pl.multiple_of(jnp.maximum(row0 - BAND_BLOCK, 0), BAND_BLOCK), BAND_BLOCK))
                mask_prev.append(valid_prev if nl > 0 else (valid_prev & (c > 0)))
            gather = lambda ref, slices: jnp.stack([ref[0, sl, :] for sl in slices])
            qb = gather(q_ref, cur)
            s_cur = jnp.einsum("wqd,wkd->wqk", qb, gather(k_ref, cur), preferred_element_type=F32)
            s_prev = jnp.einsum("wqd,wkd->wqk", qb, gather(k_ref, prev), preferred_element_type=F32)
            s_cur = jnp.where(valid_cur, s_cur * scale + bias_cur, NEG)
            s_prev = jnp.where(jnp.stack(mask_prev), s_prev * scale + bias_prev, NEG)
            m = jnp.max(jnp.maximum(s_cur, s_prev), axis=2, keepdims=True)
            p_cur = jnp.exp(s_cur - m)
            p_prev = jnp.exp(s_prev - m)
            ssum = jnp.sum(p_cur + p_prev, axis=2, keepdims=True)
            o = (jnp.einsum("wqk,wkd->wqd", p_cur.astype(BF16), gather(v_ref, cur), preferred_element_type=F32)
                 + jnp.einsum("wqk,wkd->wqd", p_prev.astype(BF16), gather(v_ref, prev),
                              preferred_element_type=F32)) / ssum
            lse = m + jnp.log(ssum)
            for w, (r, nl) in enumerate(wave):
                dst = pl.ds(r + nl * BAND_BLOCK * dil, BAND_BLOCK, stride=dil) if dil > 1 else pl.ds(
                    nl * BAND_BLOCK, BAND_BLOCK)
                og_ref[g, dst, :] = o[w]
                lg_ref[g, dst, :] = jnp.broadcast_to(lse[w], (BAND_BLOCK, LANES))
    rows = 256
    for rc in range(DIL_CHUNK // rows):
        sl = slice(rc * rows, (rc + 1) * rows)
        lses = [lg_ref[g, sl, :] for g in range(n_g)]
        mx = functools.reduce(jnp.maximum, lses)
        ws = [jnp.exp(l - mx) for l in lses]
        den = functools.reduce(lambda a, b_: a + b_, ws)
        num = functools.reduce(lambda a, b_: a + b_, [ws[g] * og_ref[g, sl, :] for g in range(n_g)])
        o_ref[0, sl, :] = (num / den).astype(o_ref.dtype)


def _dilated_attention(hs, bias_tiles):
    b, s, _ = hs[0].shape
    in_specs = []
    args = []
    for hg in hs:
        for part in range(3):
            in_specs.append(pl.BlockSpec((1, s, DIL_HEAD_DIM),
                                         lambda bi, hi, ci, part=part: (bi, 0, part * DIL_HEADS + hi)))
            args.append(hg)
    in_specs.append(pl.BlockSpec((len(DIL_GROUPS), 1, BAND_BLOCK, 2 * BAND_BLOCK),
                                 lambda bi, hi, ci: (0, hi, 0, 0)))
    return pl.pallas_call(
        functools.partial(_dil_kernel, seq=s),
        grid=(b, DIL_HEADS, s // DIL_CHUNK),
        in_specs=in_specs,
        out_specs=pl.BlockSpec((1, DIL_CHUNK, DIL_HEAD_DIM), lambda bi, hi, ci: (bi, ci, hi)),
        out_shape=jax.ShapeDtypeStruct((b, s, DIL_HEADS * DIL_HEAD_DIM), BF16),
        scratch_shapes=[pltpu.VMEM((len(DIL_GROUPS), DIL_CHUNK, LANES), F32),
                        pltpu.VMEM((len(DIL_GROUPS), DIL_CHUNK, LANES), F32)],
        compiler_params=pltpu.CompilerParams(dimension_semantics=("parallel", "parallel", "arbitrary")),
        name="dilated_attention",
    )(*args, bias_tiles)


def _outproj_kernel(*refs, n_in, with_router):
    ins = refs[:n_in]
    ws = refs[n_in:2 * n_in]
    x_ref, g_ref, b_ref = refs[2 * n_in:2 * n_in + 3]
    rest = refs[2 * n_in + 3:]
    mix = _dot(ins[0][...], ws[0][...])
    for a_ref, w_ref in zip(ins[1:], ws[1:]):
        mix = mix + _dot(a_ref[...], w_ref[...])
    xn = _layer_norm(DEEPNORM_ALPHA * x_ref[...] + mix, g_ref[...], b_ref[...])
    if not with_router:
        o_ref, ob_ref = rest
    else:
        router_ref, o_ref, gates_ref, counts_ref, run_ref, tri_ref = rest
        ob_ref = None
        tm = xn.shape[0]

        @pl.when(pl.program_id(0) == 0)
        def _():
            run_ref[...] = jnp.zeros_like(run_ref)
            tri_ref[...] = (lax.broadcasted_iota(jnp.int32, (tm, tm), 0)
                            > lax.broadcasted_iota(jnp.int32, (tm, tm), 1)).astype(BF16)

        logits = _dot(xn.astype(BF16), router_ref[...])
        lane = lax.broadcasted_iota(jnp.int32, logits.shape, 1).astype(F32)
        lg = jnp.where(lane < N_EXPERTS, logits, -jnp.inf)
        m1 = jnp.max(lg, axis=1, keepdims=True)
        i1 = jnp.min(jnp.where(lg == m1, lane, float(LANES)), axis=1, keepdims=True)
        lg2 = jnp.where(lane == i1, -jnp.inf, lg)
        m2 = jnp.max(lg2, axis=1, keepdims=True)
        i2 = jnp.min(jnp.where(lg2 == m2, lane, float(LANES)), axis=1, keepdims=True)
        e2 = jnp.exp(m2 - m1)
        den = 1.0 + e2
        oh1 = jnp.where(lane == i1, 1.0, 0.0)
        oh2 = jnp.where(lane == i2, 1.0, 0.0)
        both = oh1 + oh2
        before = _dot(tri_ref[...], both.astype(BF16)) + run_ref[...]
        r1 = jnp.sum(before * oh1, axis=1, keepdims=True)
        r2 = jnp.sum(before * oh2, axis=1, keepdims=True)
        run_ref[...] += jnp.sum(both, axis=0, keepdims=True)
        counts_ref[...] = run_ref[...]
        gates_ref[...] = (jnp.where(lane == 0.0, i1, 0.0) + jnp.where(lane == 1.0, i2, 0.0)
                          + jnp.where(lane == 2.0, 1.0 / den, 0.0) + jnp.where(lane == 3.0, e2 / den, 0.0)
                          + jnp.where(lane == 4.0, r1, 0.0) + jnp.where(lane == 5.0, r2, 0.0))
    o_ref[...] = xn
    if ob_ref is not None:
        ob_ref[...] = xn.astype(BF16)


def _outproj_ln(ins, ws, x, ln_g, ln_b, router=None, tm=512):
    t, d = x.shape
    n_in = len(ins)
    row = lambda width: pl.BlockSpec((tm, width), lambda i: (i, 0))
    full = lambda a: pl.BlockSpec(a.shape, lambda i: (0, 0))
    vec = lambda v: v.reshape(1, d).astype(F32)
    args = list(ins) + list(ws) + [x, vec(ln_g), vec(ln_b)]
    in_specs = [row(a.shape[1]) for a in ins] + [full(w) for w in ws] + [row(d), full(vec(ln_g)), full(vec(ln_b))]
    out_shape = [jax.ShapeDtypeStruct((t, d), F32), jax.ShapeDtypeStruct((t, d), BF16)]
    out_specs = [row(d), row(d)]
    if router is not None:
        router_pad = jnp.zeros((d, LANES), BF16).at[:, :N_EXPERTS].set(router.astype(BF16))
        args.append(router_pad)
        in_specs.append(full(router_pad))
        out_shape = [out_shape[0], jax.ShapeDtypeStruct((t, LANES), F32), jax.ShapeDtypeStruct((1, LANES), F32)]
        out_specs = [row(d), row(LANES), pl.BlockSpec((1, LANES), lambda i: (0, 0))]
        scratch = [pltpu.VMEM((1, LANES), F32), pltpu.VMEM((tm, tm), BF16)]
    else:
        scratch = []
    semantics = "arbitrary" if router is not None else "parallel"
    return pl.pallas_call(
        functools.partial(_outproj_kernel, n_in=n_in, with_router=router is not None),
        grid=(t // tm,),
        in_specs=in_specs,
        out_specs=out_specs,
        out_shape=out_shape,
        scratch_shapes=scratch,
        compiler_params=pltpu.CompilerParams(dimension_semantics=(semantics,)),
        name="outproj_ln",
    )(*args)


def _ffn_kernel(xb_ref, x_ref, wg_ref, wu_ref, wd_ref, g_ref, b_ref, o_ref, ob_ref, acc_ref):
    f = pl.program_id(1)

    @pl.when(f == 0)
    def _():
        acc_ref[...] = jnp.zeros_like(acc_ref)

    xb = xb_ref[...]
    hg = _dot(xb, wg_ref[...])
    hu = _dot(xb, wu_ref[...])
    act = (hg * _sigmoid(hg) * hu).astype(BF16)
    acc_ref[...] += _dot(act, wd_ref[...])

    @pl.when(f == pl.num_programs(1) - 1)
    def _():
        xn = _layer_norm(DEEPNORM_ALPHA * x_ref[...] + acc_ref[...], g_ref[...], b_ref[...])
        o_ref[...] = xn
        ob_ref[...] = xn.astype(BF16)


def _ffn_ln(xb, x, wg, wu, wd, ln_g, ln_b, tm=512, tf=1408):
    t, d = x.shape
    ff = wg.shape[1]
    vec = lambda v: v.reshape(1, d).astype(F32)
    return pl.pallas_call(
        _ffn_kernel,
        grid=(t // tm, ff // tf),
        in_specs=[pl.BlockSpec((tm, d), lambda i, f: (i, 0)),
                  pl.BlockSpec((tm, d), lambda i, f: (i, 0)),
                  pl.BlockSpec((d, tf), lambda i, f: (0, f)),
                  pl.BlockSpec((d, tf), lambda i, f: (0, f)),
                  pl.BlockSpec((tf, d), lambda i, f: (f, 0)),
                  pl.BlockSpec((1, d), lambda i, f: (0, 0)),
                  pl.BlockSpec((1, d), lambda i, f: (0, 0))],
        out_specs=[pl.BlockSpec((tm, d), lambda i, f: (i, 0)),
                   pl.BlockSpec((tm, d), lambda i, f: (i, 0))],
        out_shape=[jax.ShapeDtypeStruct((t, d), F32), jax.ShapeDtypeStruct((t, d), BF16)],
        scratch_shapes=[pltpu.VMEM((tm, d), F32)],
        compiler_params=pltpu.CompilerParams(dimension_semantics=("parallel", "arbitrary")),
        name="ffn_ln",
    )(xb, x, wg, wu, wd, vec(ln_g), vec(ln_b))


_MOE_TM = 1024


def _route(route, counts, n_tiles):
    expert = route[:, 0:2].astype(jnp.int32)
    rank = route[:, 4:6].astype(jnp.int32)
    counts = counts[0, :N_EXPERTS].astype(jnp.int32)
    padded = (counts + _MOE_TM - 1) // _MOE_TM * _MOE_TM
    ends = jnp.cumsum(padded)
    starts = ends - padded
    onehot = expert[:, :, None] == jnp.arange(N_EXPERTS, dtype=jnp.int32)
    pos = (rank + jnp.sum(jnp.where(onehot, starts, 0), axis=-1)).reshape(-1)
    tile_start = jnp.arange(n_tiles, dtype=jnp.int32) * _MOE_TM
    tile_expert = jnp.minimum(jnp.sum((tile_start[:, None] >= ends[None, :]).astype(jnp.int32), axis=1),
                              N_EXPERTS - 1)
    return pos.astype(jnp.int32), tile_expert.astype(jnp.int32)


def _row_scatter_kernel(pos_ref, x_ref, init_ref, o_ref, sem, *, tm):
    del init_ref

    def body(j, carry):
        for choice in range(2):
            p = pos_ref[0, 0, 2 * j + choice]
            pltpu.make_async_copy(x_ref.at[pl.ds(j, 1)], o_ref.at[pl.ds(p, 1)], sem).start()
        return carry

    lax.fori_loop(0, tm, body, 0, unroll=8)
    for _ in range(2):
        pltpu.make_async_copy(x_ref, o_ref.at[pl.ds(0, tm)], sem).wait()


def _row_scatter(x, pos, init, tm=512):
    t, d = x.shape
    n_rows = init.shape[0]
    return pl.pallas_call(
        functools.partial(_row_scatter_kernel, tm=tm),
        grid=(t // tm,),
        in_specs=[pl.BlockSpec((1, 1, 2 * tm), lambda i: (i, 0, 0), memory_space=pltpu.SMEM),
                  pl.BlockSpec((tm, d), lambda i: (i, 0)),
                  pl.BlockSpec(memory_space=pl.ANY)],
        out_specs=pl.BlockSpec(memory_space=pl.ANY),
        out_shape=jax.ShapeDtypeStruct((n_rows, d), x.dtype),
        scratch_shapes=[pltpu.SemaphoreType.DMA(())],
        input_output_aliases={2: 0},
        compiler_params=pltpu.CompilerParams(dimension_semantics=("arbitrary",)),
        name="moe_row_scatter",
    )(pos.reshape(t // tm, 1, 2 * tm), x, init)


def _expert_ffn_kernel(te_ref, x_ref, wg_ref, wu_ref, wd_ref, o_ref, acc_ref):
    del te_ref
    f = pl.program_id(1)

    @pl.when(f == 0)
    def _():
        acc_ref[...] = jnp.zeros_like(acc_ref)

    xb = x_ref[...].astype(BF16)
    hg = _dot(xb, wg_ref[0].astype(BF16))
    hu = _dot(xb, wu_ref[0].astype(BF16))
    act = (hg * _sigmoid(hg) * hu).astype(BF16)
    acc_ref[...] += _dot(act, wd_ref[0].astype(BF16))

    @pl.when(f == pl.num_programs(1) - 1)
    def _():
        o_ref[...] = acc_ref[...]


def _expert_ffn(xs, tile_expert, wg, wu, wd, tf=512):
    n_rows, d = xs.shape
    ff = wg.shape[2]
    grid_spec = pltpu.PrefetchScalarGridSpec(
        num_scalar_prefetch=1,
        grid=(n_rows // _MOE_TM, ff // tf),
        in_specs=[pl.BlockSpec((_MOE_TM, d), lambda i, f, te: (i, 0)),
                  pl.BlockSpec((1, d, tf), lambda i, f, te: (te[i], 0, f)),
                  pl.BlockSpec((1, d, tf), lambda i, f, te: (te[i], 0, f)),
                  pl.BlockSpec((1, tf, d), lambda i, f, te: (te[i], f, 0))],
        out_specs=pl.BlockSpec((_MOE_TM, d), lambda i, f, te: (i, 0)),
        scratch_shapes=[pltpu.VMEM((_MOE_TM, d), F32)])
    return pl.pallas_call(
        _expert_ffn_kernel,
        grid_spec=grid_spec,
        out_shape=jax.ShapeDtypeStruct((n_rows, d), F32),
        compiler_params=pltpu.CompilerParams(dimension_semantics=("parallel", "arbitrary")),
        name="moe_expert_ffn",
    )(tile_expert, xs, wg, wu, wd)


def _combine_kernel(pos_ref, pos_next_ref, route_ref, x_ref, g_ref, b_ref, y_ref, o_ref, ybuf_ref, sems, *, tm):
    i = pl.program_id(0)
    n = pl.num_programs(0)

    def gather(src_pos_ref, slot):
        def body(j, carry):
            for choice in range(2):
                p = src_pos_ref[0, 0, 2 * j + choice]
                pltpu.make_async_copy(y_ref.at[pl.ds(p, 1)], ybuf_ref.at[slot, choice, pl.ds(j, 1)],
                                      sems.at[slot]).start()
            return carry

        lax.fori_loop(0, tm, body, 0, unroll=8)

    @pl.when(i == 0)
    def _():
        gather(pos_ref, 0)

    slot = i % 2

    @pl.when(i + 1 < n)
    def _():
        gather(pos_next_ref, 1 - slot)

    for choice in range(2):
        pltpu.make_async_copy(y_ref.at[pl.ds(0, tm)], ybuf_ref.at[slot, choice], sems.at[slot]).wait()
    route = route_ref[...]
    y = route[:, 2:3] * ybuf_ref[slot, 0] + route[:, 3:4] * ybuf_ref[slot, 1]
    o_ref[...] = _layer_norm(DEEPNORM_ALPHA * x_ref[...] + y, g_ref[...], b_ref[...])


def _combine_ln(ys, pos, route, x, ln_g, ln_b, tm=512):
    t, d = x.shape
    vec = lambda v: v.reshape(1, d).astype(F32)
    last = t // tm - 1
    pos3 = pos.reshape(t // tm, 1, 2 * tm)
    return pl.pallas_call(
        functools.partial(_combine_kernel, tm=tm),
        grid=(t // tm,),
        in_specs=[pl.BlockSpec((1, 1, 2 * tm), lambda i: (i, 0, 0), memory_space=pltpu.SMEM),
                  pl.BlockSpec((1, 1, 2 * tm), lambda i: (jnp.minimum(i + 1, last), 0, 0), memory_space=pltpu.SMEM),
                  pl.BlockSpec((tm, LANES), lambda i: (i, 0)),
                  pl.BlockSpec((tm, d), lambda i: (i, 0)),
                  pl.BlockSpec((1, d), lambda i: (0, 0)),
                  pl.BlockSpec((1, d), lambda i: (0, 0)),
                  pl.BlockSpec(memory_space=pl.ANY)],
        out_specs=pl.BlockSpec((tm, d), lambda i: (i, 0)),
        out_shape=jax.ShapeDtypeStruct((t, d), F32),
        scratch_shapes=[pltpu.VMEM((2, 2, tm, d), F32), pltpu.SemaphoreType.DMA((2,))],
        compiler_params=pltpu.CompilerParams(dimension_semantics=("arbitrary",)),
        name="moe_combine_ln",
    )(pos3, pos3, route, x, vec(ln_g), vec(ln_b), ys)


def _moe_ln(x, route, counts, wg, wu, wd, ln_g, ln_b):
    t, _ = x.shape
    n_tiles = 2 * t // _MOE_TM + N_EXPERTS
    pos, tile_expert = _route(route, counts, n_tiles)
    xs = _row_scatter(x, pos, jnp.zeros((n_tiles * _MOE_TM, x.shape[1]), x.dtype))
    ys = _expert_ffn(xs, tile_expert, wg, wu, wd)
    return _combine_ln(ys, pos, route, x, ln_g, ln_b)


def _even_layer(x, xb, batch, seq, dsa_bias, w_in, conv_w, conv_b, conv_ln_g, conv_ln_b, w_out,
                ln1_g, ln1_b, ffn_wg, ffn_wu, ffn_wd, ln2_g, ln2_b):
    w_in_p = jnp.zeros((D_MODEL, EVEN_IN_PAD), BF16).at[:, :w_in.shape[1]].set(w_in.astype(BF16))
    h = _matmul(x if xb is None else xb, w_in_p, 1024, EVEN_IN_PAD, BF16).reshape(batch, seq, EVEN_IN_PAD)
    a_out = _conformer_conv(h, conv_w, conv_b, conv_ln_g, conv_ln_b)
    att = _dsa_attention(h, dsa_bias)
    w_out_b = w_out.astype(BF16)
    x1, x1b = _outproj_ln([a_out.reshape(-1, CONV_CH), att.reshape(-1, DSA_HEADS * DSA_HEAD_DIM)],
                          [w_out_b[:CONV_CH], w_out_b[CONV_CH:]], x, ln1_g, ln1_b)
    return _ffn_ln(x1b, x1, ffn_wg.astype(BF16), ffn_wu.astype(BF16), ffn_wd.astype(BF16), ln2_g, ln2_b)


def _odd_layer(x, xb, batch, seq, dil_bias, w_in, w_out, ln1_g, ln1_b, router, moe_wg, moe_wu, moe_wd,
               ln2_g, ln2_b):
    gw = 3 * DIL_HEADS * DIL_HEAD_DIM
    w_in_b = w_in.astype(BF16)
    hs = [_matmul_residue_major(x if xb is None else xb, w_in_b[:, g * gw:(g + 1) * gw], batch, seq, dil, 2048, 768)
          for g, (_, dil) in enumerate(DIL_GROUPS)]
    o = _dilated_attention(hs, dil_bias).reshape(-1, DIL_HEADS * DIL_HEAD_DIM)
    x1, route, counts = _outproj_ln([o], [w_out.astype(BF16)], x, ln1_g, ln1_b, router=router)
    out = _moe_ln(x1, route, counts, moe_wg, moe_wu, moe_wd, ln2_g, ln2_b)
    return out, out.astype(BF16)


def kernel(x, rel_bias, even_w_in, even_conv_w, even_conv_b, even_conv_ln_g, even_conv_ln_b, even_w_out, even_ln1_g, even_ln1_b, even_ffn_wg, even_ffn_wu, even_ffn_wd, even_ln2_g, even_ln2_b, odd_w_in, odd_w_out, odd_ln1_g, odd_ln1_b, odd_router, odd_moe_wg, odd_moe_wu, odd_moe_wd, odd_ln2_g, odd_ln2_b):
    batch, seq, d = x.shape
    assert d == D_MODEL and seq % DIL_CHUNK == 0
    depth = even_w_in.shape[0] + odd_w_in.shape[0]
    assert depth == DEPTH
    dsa_bias = _dsa_bias_tiles(rel_bias, seq)
    dil_bias = _dil_bias_tiles(rel_bias)
    xf = x.reshape(batch * seq, d)
    xb = None
    for layer in range(depth):
        i = layer // 2
        if layer % 2 == 0:
            xf, xb = _even_layer(xf, xb, batch, seq, dsa_bias, even_w_in[i], even_conv_w[i], even_conv_b[i],
                                 even_conv_ln_g[i], even_conv_ln_b[i], even_w_out[i], even_ln1_g[i],
                                 even_ln1_b[i], even_ffn_wg[i], even_ffn_wu[i], even_ffn_wd[i],
                                 even_ln2_g[i], even_ln2_b[i])
        else:
            xf, xb = _odd_layer(xf, xb, batch, seq, dil_bias, odd_w_in[i], odd_w_out[i], odd_ln1_g[i],
                                odd_ln1_b[i], odd_router[i], odd_moe_wg[i], odd_moe_wu[i], odd_moe_wd[i],
                                odd_ln2_g[i], odd_ln2_b[i])
    return xf.reshape(batch, seq, d)
```

```python
import functools
import math

import jax
import jax.numpy as jnp
from jax import lax
from jax.experimental import pallas as pl
from jax.experimental.pallas import tpu as pltpu

BF16 = jnp.bfloat16
F32 = jnp.float32

D_MODEL = 1024
CONV_CH = 512
CONV_WIDTH = 31
DSA_HEADS = 8
DSA_HEAD_DIM = 64
IDX_HEADS = 8
IDX_DIM = 64
DSA_TOPK_MAX = 256
Q_BLOCK = 128
DIL_GROUPS = ((128, 1), (512, 4), (2048, 16))
DIL_HEADS = 8
DIL_HEAD_DIM = 128
BAND_BLOCK = 128
NUM_BUCKETS = 32
MAX_DISTANCE = 2048
N_EXPERTS = 8
LN_EPS = 1e-5
NEG = -1e30
DEPTH = 2
DEEPNORM_ALPHA = (2 * DEPTH) ** 0.25

LANES = 128
EVEN_IN_PAD = 3200
INT_MIN = -(2 ** 31)
DIL_CHUNK = BAND_BLOCK * max(d for _, d in DIL_GROUPS)

_PARALLEL2 = pltpu.CompilerParams(dimension_semantics=("parallel", "parallel"))


def _layer_norm(y, g, b):
    mu = jnp.mean(y, axis=-1, keepdims=True)
    yc = y - mu
    var = jnp.mean(yc * yc, axis=-1, keepdims=True)
    return yc * lax.rsqrt(var + LN_EPS) * g + b


def _sigmoid(x):
    return 1.0 / (1.0 + jnp.exp(-x))


def _dot_nt(a, b):
    return lax.dot_general(a, b, (((1,), (1,)), ((), ())), preferred_element_type=F32)


def _dot(a, b):
    return jnp.dot(a, b, preferred_element_type=F32)


def _mm_kernel(x_ref, w_ref, o_ref):
    o_ref[...] = _dot(x_ref[...].astype(BF16), w_ref[...]).astype(o_ref.dtype)


def _matmul(x, w, tm, tn, out_dtype):
    t, k = x.shape
    n = w.shape[1]
    return pl.pallas_call(
        _mm_kernel,
        grid=(t // tm, n // tn),
        in_specs=[pl.BlockSpec((tm, k), lambda i, j: (i, 0)),
                  pl.BlockSpec((k, tn), lambda i, j: (0, j))],
        out_specs=pl.BlockSpec((tm, tn), lambda i, j: (i, j)),
        out_shape=jax.ShapeDtypeStruct((t, n), out_dtype),
        compiler_params=_PARALLEL2,
        name="matmul",
    )(x, w)


_MAX_ROW_STRIDE = 4


def _mm_perm_kernel(x_ref, w_ref, o_ref, xp_ref, a_ref, b_ref, *, dil, tm):
    rows = tm // dil

    @pl.when(pl.program_id(1) == 0)
    def _():
        for c in range(x_ref.shape[1] // LANES):
            cols = slice(c * LANES, (c + 1) * LANES)
            a_ref[...] = x_ref[:, cols].astype(F32)
            if dil <= _MAX_ROW_STRIDE:
                for r in range(dil):
                    xp_ref[r * rows:(r + 1) * rows, cols] = a_ref[pl.ds(r, rows, stride=dil), :].astype(BF16)
                continue
            s1 = _MAX_ROW_STRIDE
            s2 = dil // s1
            assert s2 <= _MAX_ROW_STRIDE
            rows1 = tm // s1
            for r1 in range(s1):
                b_ref[r1 * rows1:(r1 + 1) * rows1, :] = a_ref[pl.ds(r1, rows1, stride=s1), :]
            for r1 in range(s1):
                for r2 in range(s2):
                    r = s1 * r2 + r1
                    xp_ref[r * rows:(r + 1) * rows, cols] = (
                        b_ref[pl.ds(r1 * rows1 + r2, rows, stride=s2), :].astype(BF16))

    acc = _dot(xp_ref[...], w_ref[...])
    o_ref[0] = acc.reshape(dil, rows, acc.shape[1]).astype(o_ref.dtype)


def _matmul_residue_major(x, w, batch, seq, dil, tm, tn):
    t, k = x.shape
    n = w.shape[1]
    if dil == 1:
        return _matmul(x, w, tm, tn, BF16).reshape(batch, seq, n)
    tiles_per_seq = seq // tm
    out = pl.pallas_call(
        functools.partial(_mm_perm_kernel, dil=dil, tm=tm),
        grid=(t // tm, n // tn),
        in_specs=[pl.BlockSpec((tm, k), lambda i, j: (i, 0)),
                  pl.BlockSpec((k, tn), lambda i, j: (0, j))],
        out_specs=pl.BlockSpec((1, dil, tm // dil, tn),
                               lambda i, j: (i // tiles_per_seq, 0, i % tiles_per_seq, j)),
        out_shape=jax.ShapeDtypeStruct((batch, dil, seq // dil, n), BF16),
        scratch_shapes=[pltpu.VMEM((tm, k), BF16), pltpu.VMEM((tm, LANES), F32), pltpu.VMEM((tm, LANES), F32)],
        compiler_params=pltpu.CompilerParams(dimension_semantics=("parallel", "arbitrary")),
        name="matmul_residue_major",
    )(x, w)
    return out.reshape(batch, seq, n)


_CONV_HALO = 32
_CONV_ROWS = 128
_SUBLANES = 8


def _conv_kernel(val_ref, gate_ref, w_ref, cb_ref, g_ref, b_ref, o_ref, ext_ref, *, ts):
    s = pl.program_id(1)

    @pl.when(s == 0)
    def _():
        ext_ref[0, 0:_CONV_HALO, :] = jnp.zeros((_CONV_HALO, CONV_CH), F32)

    val = val_ref[0].astype(F32)
    gate = gate_ref[0].astype(F32)
    ext_ref[0, _CONV_HALO:_CONV_HALO + ts, :] = val * _sigmoid(gate)
    span = ts + _CONV_HALO - _SUBLANES
    for k in range(1, _SUBLANES):
        ext_ref[k, 0:span, :] = ext_ref[0, k:k + span, :]
    first_tap = _CONV_HALO - (CONV_WIDTH - 1)
    for rc in range(ts // _CONV_ROWS):
        r0 = rc * _CONV_ROWS
        acc = jnp.zeros((_CONV_ROWS, CONV_CH), F32)
        for j in range(CONV_WIDTH):
            shift = (first_tap + j) % _SUBLANES
            lo = r0 + first_tap + j - shift
            acc = acc + ext_ref[shift, lo:lo + _CONV_ROWS, :] * w_ref[j:j + 1, :]
        y = _layer_norm(acc + cb_ref[...], g_ref[...], b_ref[...])
        o_ref[0, r0:r0 + _CONV_ROWS, :] = (y * _sigmoid(y)).astype(o_ref.dtype)
    ext_ref[0, 0:_CONV_HALO, :] = ext_ref[0, ts:ts + _CONV_HALO, :]


def _conformer_conv(h, conv_w, conv_b, ln_g, ln_b, ts=512):
    b, s, _ = h.shape
    w_pad = jnp.zeros((32, CONV_CH), F32).at[:CONV_WIDTH].set(conv_w)
    vec = lambda v: v.reshape(1, CONV_CH).astype(F32)
    full = lambda shape: pl.BlockSpec(shape, lambda bi, si: (0, 0))
    return pl.pallas_call(
        functools.partial(_conv_kernel, ts=ts),
        grid=(b, s // ts),
        in_specs=[pl.BlockSpec((1, ts, CONV_CH), lambda bi, si: (bi, si, 0)),
                  pl.BlockSpec((1, ts, CONV_CH), lambda bi, si: (bi, si, 1)),
                  full((32, CONV_CH)), full((1, CONV_CH)), full((1, CONV_CH)), full((1, CONV_CH))],
        out_specs=pl.BlockSpec((1, ts, CONV_CH), lambda bi, si: (bi, si, 0)),
        out_shape=jax.ShapeDtypeStruct((b, s, CONV_CH), BF16),
        scratch_shapes=[pltpu.VMEM((_SUBLANES, ts + _CONV_HALO, CONV_CH), F32)],
        compiler_params=pltpu.CompilerParams(dimension_semantics=("parallel", "arbitrary")),
        name="conformer_conv",
    )(h, h, w_pad, vec(conv_b), vec(ln_g), vec(ln_b))


def _rel_bucket(dist):
    max_exact = NUM_BUCKETS // 2
    n = dist.astype(jnp.int32)
    nf = jnp.maximum(n, 1).astype(F32)
    large = max_exact + (jnp.log(nf / max_exact) / math.log(MAX_DISTANCE / max_exact)
                         * (NUM_BUCKETS - max_exact)).astype(jnp.int32)
    large = jnp.minimum(large, NUM_BUCKETS - 1)
    return jnp.where(n < max_exact, n, large)


def _bias_by_distance(rel_bias, n):
    return rel_bias[_rel_bucket(jnp.arange(n, dtype=jnp.int32))].astype(F32).T


def _toeplitz_kernel(win_ref, o_ref):
    n_win, span = win_ref.shape
    for n in range(n_win):
        y = jnp.broadcast_to(win_ref[n:n + 1, :], (LANES, span))
        y = pltpu.roll(y, 0, 1, stride=1, stride_axis=0)
        o_ref[n] = y[:, LANES:]


def _toeplitz_tiles(windows, per_step=8):
    n, span = windows.shape
    width = span - LANES
    return pl.pallas_call(
        _toeplitz_kernel,
        grid=(n // per_step,),
        in_specs=[pl.BlockSpec((per_step, span), lambda i: (i, 0))],
        out_specs=pl.BlockSpec((per_step, LANES, width), lambda i: (i, 0, 0)),
        out_shape=jax.ShapeDtypeStruct((n, LANES, width), F32),
        compiler_params=pltpu.CompilerParams(dimension_semantics=("parallel",)),
        name="bias_toeplitz",
    )(windows)


def _dsa_bias_tiles(rel_bias, seq):
    nblk = seq // Q_BLOCK
    heads = rel_bias.shape[1]
    f = _bias_by_distance(rel_bias, seq)
    u = jnp.concatenate([jnp.broadcast_to(f[:, :1], (heads, 2 * Q_BLOCK)), f], axis=1)
    blocks = u.reshape(heads, nblk + 2, Q_BLOCK)
    windows = jnp.concatenate([blocks[:, :-1], blocks[:, 1:]], axis=-1)
    windows = windows.transpose(1, 0, 2).reshape((nblk + 1) * heads, 2 * Q_BLOCK)
    return _toeplitz_tiles(windows).reshape(nblk + 1, heads, Q_BLOCK, Q_BLOCK)


def _dil_bias_tiles(rel_bias):
    max_dil = max(d for _, d in DIL_GROUPS)
    heads = rel_bias.shape[1]
    span = 2 * BAND_BLOCK
    f = _bias_by_distance(rel_bias, span * max_dil + 1)
    windows = []
    for _, dil in DIL_GROUPS:
        fd = f[:, 0:span * dil + 1:dil]
        tail = jnp.broadcast_to(fd[:, :1], (heads, BAND_BLOCK - 1))
        windows.append(jnp.concatenate([fd[:, ::-1], tail], axis=1))
    tiles = _toeplitz_tiles(jnp.concatenate(windows, axis=0))
    return tiles.reshape(len(DIL_GROUPS), heads, BAND_BLOCK, span)


_KC = 2 * Q_BLOCK
_VT_ROWS = DSA_HEAD_DIM + 16


def _sortable_key(score):
    score = jnp.where(score == 0.0, 0.0, score)
    bits = lax.bitcast_convert_type(score, jnp.int32)
    return jnp.where(bits < 0, bits ^ jnp.int32(0x7FFFFFFF), bits)


def _rows_reduce(x, op):
    rows = x.shape[0]
    return op(op(x.reshape(rows // 8, 8, LANES), axis=0), axis=0, keepdims=True)


def _keys_reduce(x, op):
    h, keys, _ = x.shape
    return op(op(x.reshape(h, keys // 8, 8, LANES), axis=1), axis=1, keepdims=True)


_QG = 2


def _dsa_t_kernel(q_ref, qi_ref, wq_ref, k_ref, vt_ref, ki_ref, bias_hbm, o_ref,
                  bias_ref, keys_ref, qm_ref, acc_ref, dots_a_ref, dots_b_ref, sem, *, topk):
    b = pl.program_id(0)
    i2 = pl.program_id(1)

    @pl.when((b == 0) & (i2 == 0))
    def _():
        cp = pltpu.make_async_copy(bias_hbm, bias_ref, sem)
        cp.start()
        cp.wait()

    nch = i2 + 1
    key_off = lax.broadcasted_iota(jnp.int32, (_KC, LANES), 0)
    q_lane = lax.broadcasted_iota(jnp.int32, (_KC, LANES), 1)
    lane = lax.broadcasted_iota(jnp.int32, (Q_BLOCK, LANES), 1)
    low_half = lane < DSA_HEAD_DIM
    n_heads = _QG * DSA_HEADS

    w_ts, qi_heads = [], []
    for g in range(_QG):
        rows = slice(g * Q_BLOCK, (g + 1) * Q_BLOCK)
        w_ts.append((wq_ref[0, rows, :].astype(F32) * (IDX_HEADS ** -0.5 * IDX_DIM ** -0.5)).T)
        qi = qi_ref[0, rows, :].astype(F32)
        for p in range(IDX_HEADS // 2):
            pair = qi[:, p * LANES:(p + 1) * LANES]
            qi_heads.append(jnp.where(low_half, pair, 0.0))
            qi_heads.append(jnp.where(low_half, pltpu.roll(pair, IDX_DIM, 1), 0.0))
        q_scaled = (q_ref[0, rows, :].astype(F32) * (DSA_HEAD_DIM ** -0.5)).astype(BF16)
        for p in range(DSA_HEADS // 2):
            qp = q_scaled[:, p * LANES:(p + 1) * LANES]
            qm_ref[g * DSA_HEADS + 2 * p] = jnp.where(low_half, qp, jnp.zeros_like(qp))
            qm_ref[g * DSA_HEADS + 2 * p + 1] = jnp.where(low_half, jnp.zeros_like(qp), qp)
    qi_stack = jnp.concatenate(qi_heads, axis=0).astype(BF16)

    def index_dots(c, dst_ref):
        c = jnp.minimum(c, nch - 1)
        k0 = pl.multiple_of(c * _KC, _KC)
        ki = ki_ref[0, pl.ds(k0, _KC), :]
        dst_ref[...] = _dot_nt(ki, qi_stack)

    def consume_dots(c, src_ref):
        k0 = c * _KC
        for g in range(_QG):
            score = jnp.zeros((_KC, LANES), F32)
            for h in range(IDX_HEADS):
                w_row = w_ts[g][IDX_DIM + h:IDX_DIM + h + 1, :]
                col = (g * IDX_HEADS + h) * Q_BLOCK
                score = score + w_row * jnp.maximum(src_ref[:, col:col + Q_BLOCK], 0.0)
            q_pos = (_QG * i2 + g) * Q_BLOCK + q_lane
            keys_ref[g, c] = jnp.where(k0 + key_off <= q_pos, _sortable_key(score), jnp.int32(INT_MIN))

    index_dots(0, dots_a_ref)

    def score_body(c2, carry):
        index_dots(2 * c2 + 1, dots_b_ref)
        consume_dots(2 * c2, dots_a_ref)
        index_dots(2 * c2 + 2, dots_a_ref)
        consume_dots(2 * c2 + 1, dots_b_ref)
        return carry

    lax.fori_loop(0, (nch + 1) // 2, score_body, 0)

    def count_ge(cand):
        cb = jnp.broadcast_to(cand, (_QG, 8, LANES))[:, None, None]
        chains = 4

        def body(c2, acc):
            for half in range(2):
                kk = keys_ref[:, 2 * c2 + half].reshape(_QG, _KC // (8 * chains), chains, 8, LANES)
                acc = acc + jnp.sum(jnp.where(kk >= cb, 1.0, 0.0), axis=1)
            return acc

        acc = lax.fori_loop(0, (nch + 1) // 2, body, jnp.zeros((_QG, chains, 8, LANES), F32))
        return jnp.sum(jnp.sum(acc, axis=1), axis=1, keepdims=True)

    kf = float(topk)
    zero = jnp.zeros((_QG, 1, LANES), jnp.int32)
    thr0 = jnp.where(count_ge(zero) >= kf, zero, jnp.full((_QG, 1, LANES), INT_MIN, jnp.int32))

    def bit_body(t, thr):
        cand = thr | jnp.left_shift(jnp.int32(1), 30 - t)
        return jnp.where(count_ge(cand) >= kf, cand, thr)

    thr = lax.fori_loop(0, 31, bit_body, thr0)
    n_gt = count_ge(jnp.where(thr == jnp.int32(2 ** 31 - 1), thr, thr + 1))
    need = kf - n_gt
    tri = (lax.broadcasted_iota(jnp.int32, (_KC, _KC), 1)
           <= lax.broadcasted_iota(jnp.int32, (_KC, _KC), 0)).astype(BF16)

    acc_ref[...] = jnp.zeros_like(acc_ref)

    def attn_body(c, carry):
        ms, tie_carry = carry
        k0 = pl.multiple_of(c * _KC, _KC)
        kk = keys_ref[:, c]
        causal = kk > jnp.int32(INT_MIN)
        eq = (kk == thr) & causal
        eq_f = jnp.where(eq, 1.0, 0.0)
        eq_b = eq_f.astype(BF16)
        rank = jnp.stack([_dot(tri, eq_b[g]) for g in range(_QG)]) + tie_carry
        sel = ((kk > thr) & causal) | (eq & (rank <= need))
        tie_carry = tie_carry + jnp.stack([_rows_reduce(eq_f[g], jnp.sum) for g in range(_QG)])
        kc = k_ref[0, pl.ds(k0, _KC), :]
        k_heads = jnp.stack([kc[:, (h // 2) * LANES:(h // 2 + 1) * LANES] for h in range(DSA_HEADS)] * _QG)
        s = jnp.einsum("hkd,hqd->hkq", k_heads, qm_ref[...], preferred_element_type=F32)
        biases = []
        for g in range(_QG):
            d = _QG * i2 + g - 2 * c
            biases.append(jnp.concatenate([bias_ref[d + 1], bias_ref[d]], axis=1))
        bias = jnp.concatenate(biases, axis=0)
        sel_h = jnp.broadcast_to(sel[:, None], (_QG, DSA_HEADS, _KC, LANES)).reshape(n_heads, _KC, LANES)
        s = jnp.where(sel_h, s + bias, NEG)
        m_new = jnp.maximum(ms, _keys_reduce(s, jnp.max))
        alpha = jnp.exp(ms - m_new)
        pr = jnp.exp((s - m_new).astype(BF16))
        vt = jnp.concatenate([vt_ref[0, c]] * _QG, axis=0)
        pv = jnp.einsum("hdk,hkq->hdq", vt, pr, preferred_element_type=F32)
        acc_ref[...] = alpha * acc_ref[...] + pv
        return m_new, tie_carry

    init = (jnp.full((n_heads, 1, LANES), NEG, F32), jnp.zeros((_QG, 1, LANES), F32))
    lax.fori_loop(0, nch, attn_body, init)
    acc = acc_ref[...]
    out_t = acc[:, :DSA_HEAD_DIM] / acc[:, DSA_HEAD_DIM:DSA_HEAD_DIM + 1]
    for g in range(_QG):
        block = out_t[g * DSA_HEADS:(g + 1) * DSA_HEADS].reshape(DSA_HEADS * DSA_HEAD_DIM, Q_BLOCK)
        o_ref[0, g * Q_BLOCK:(g + 1) * Q_BLOCK, :] = block.T.astype(o_ref.dtype)


_VT_CHUNKS = 4


def _vt_kernel(v_ref, o_ref):
    for c in range(_VT_CHUNKS):
        vt = v_ref[0, c * _KC:(c + 1) * _KC, :].astype(F32).T
        o_ref[0, c, :, :DSA_HEAD_DIM, :] = vt.reshape(DSA_HEADS, DSA_HEAD_DIM, _KC).astype(o_ref.dtype)
        o_ref[0, c, :, DSA_HEAD_DIM:, :] = jnp.ones((DSA_HEADS, _VT_ROWS - DSA_HEAD_DIM, _KC), o_ref.dtype)


def _values_key_minor(h):
    b, s, _ = h.shape
    width = DSA_HEADS * DSA_HEAD_DIM
    v_block = (2 * CONV_CH + 2 * width) // width
    return pl.pallas_call(
        _vt_kernel,
        grid=(b, s // (_VT_CHUNKS * _KC)),
        in_specs=[pl.BlockSpec((1, _VT_CHUNKS * _KC, width), lambda bi, ci: (bi, ci, v_block))],
        out_specs=pl.BlockSpec((1, _VT_CHUNKS, DSA_HEADS, _VT_ROWS, _KC), lambda bi, ci: (bi, ci, 0, 0, 0)),
        out_shape=jax.ShapeDtypeStruct((b, s // _KC, DSA_HEADS, _VT_ROWS, _KC), BF16),
        compiler_params=_PARALLEL2,
        name="dsa_values_key_minor",
    )(h)


def _dsa_attention(h, bias_tiles):
    b, s, _ = h.shape
    width = DSA_HEADS * DSA_HEAD_DIM
    nblk = s // Q_BLOCK
    topk = min(DSA_TOPK_MAX, s // 4)
    kw_block = (CONV_CH * 2 + 4 * width) // LANES
    vt = _values_key_minor(h)
    return pl.pallas_call(
        functools.partial(_dsa_t_kernel, topk=topk),
        grid=(b, nblk // _QG),
        in_specs=[pl.BlockSpec((1, _QG * Q_BLOCK, width), lambda bi, qi: (bi, qi, 2)),
                  pl.BlockSpec((1, _QG * Q_BLOCK, width), lambda bi, qi: (bi, qi, 5)),
                  pl.BlockSpec((1, _QG * Q_BLOCK, LANES), lambda bi, qi: (bi, qi, kw_block)),
                  pl.BlockSpec((1, s, width), lambda bi, qi: (bi, 0, 3)),
                  pl.BlockSpec((1, s // _KC, DSA_HEADS, _VT_ROWS, _KC), lambda bi, qi: (bi, 0, 0, 0, 0)),
                  pl.BlockSpec((1, s, LANES), lambda bi, qi: (bi, 0, kw_block)),
                  pl.BlockSpec(memory_space=pl.ANY)],
        out_specs=pl.BlockSpec((1, _QG * Q_BLOCK, width), lambda bi, qi: (bi, qi, 0)),
        out_shape=jax.ShapeDtypeStruct((b, s, width), BF16),
        scratch_shapes=[pltpu.VMEM((nblk + 1, DSA_HEADS, Q_BLOCK, Q_BLOCK), F32),
                        pltpu.VMEM((_QG, s // _KC, _KC, Q_BLOCK), jnp.int32),
                        pltpu.VMEM((_QG * DSA_HEADS, Q_BLOCK, LANES), BF16),
                        pltpu.VMEM((_QG * DSA_HEADS, _VT_ROWS, Q_BLOCK), F32),
                        pltpu.VMEM((_KC, _QG * IDX_HEADS * Q_BLOCK), F32),
                        pltpu.VMEM((_KC, _QG * IDX_HEADS * Q_BLOCK), F32),
                        pltpu.SemaphoreType.DMA(())],
        compiler_params=pltpu.CompilerParams(dimension_semantics=("arbitrary", "arbitrary")),
        name="dsa_attention",
    )(h, h, h, h, vt, h, bias_tiles)


_DIL_WAVE = 16


def _dil_kernel(*refs, seq):
    n_g = len(DIL_GROUPS)
    qkv = refs[:3 * n_g]
    bias_ref, o_ref, og_ref, lg_ref = refs[3 * n_g:]
    c = pl.program_id(2)
    ii = lax.broadcasted_iota(jnp.int32, (BAND_BLOCK, BAND_BLOCK), 0)
    jj = lax.broadcasted_iota(jnp.int32, (BAND_BLOCK, BAND_BLOCK), 1)
    valid_prev = jj >= ii
    valid_cur = ii >= jj
    scale = DIL_HEAD_DIM ** -0.5
    for g, (window, dil) in enumerate(DIL_GROUPS):
        assert window // dil == BAND_BLOCK
        q_ref, k_ref, v_ref = qkv[3 * g:3 * g + 3]
        sub_len = seq // dil
        blocks_per_chunk = DIL_CHUNK // (BAND_BLOCK * dil)
        bias_prev = bias_ref[g, 0, :, :BAND_BLOCK]
        bias_cur = bias_ref[g, 0, :, BAND_BLOCK:]
        tiles = [(r, nl) for r in range(dil) for nl in range(blocks_per_chunk)]
        for w0 in range(0, len(tiles), _DIL_WAVE):
            wave = tiles[w0:w0 + _DIL_WAVE]
            cur, prev, mask_prev = [], [], []
            for r, nl in wave:
                row0 = pl.multiple_of(r * sub_len + c * (DIL_CHUNK // dil) + nl * BAND_BLOCK, BAND_BLOCK)
                cur.append(pl.ds(row0, BAND_BLOCK))
                prev.append(pl.ds(pl.multiple_of(jnp.maximum(row0 - BAND_BLOCK, 0), BAND_BLOCK), BAND_BLOCK))
                mask_prev.append(valid_prev if nl > 0 else (valid_prev & (c > 0)))
            gather = lambda ref, slices: jnp.stack([ref[0, sl, :] for sl in slices])
            qb = gather(q_ref, cur)
            s_cur = jnp.einsum("wqd,wkd->wqk", qb, gather(k_ref, cur), preferred_element_type=F32)
            s_prev = jnp.einsum("wqd,wkd->wqk", qb, gather(k_ref, prev), preferred_element_type=F32)
            s_cur = jnp.where(valid_cur, s_cur * scale + bias_cur, NEG)
            s_prev = jnp.where(jnp.stack(mask_prev), s_prev * scale + bias_prev, NEG)
            m = jnp.max(jnp.maximum(s_cur, s_prev), axis=2, keepdims=True)
            p_cur = jnp.exp(s_cur - m)
            p_prev = jnp.exp(s_prev - m)
            ssum = jnp.sum(p_cur + p_prev, axis=2, keepdims=True)
            o = (jnp.einsum("wqk,wkd->wqd", p_cur.astype(BF16), gather(v_ref, cur), preferred_element_type=F32)
                 + jnp.einsum("wqk,wkd->wqd", p_prev.astype(BF16), gather(v_ref, prev),
                              preferred_element_type=F32)) / ssum
            lse = m + jnp.log(ssum)
            for w, (r, nl) in enumerate(wave):
                dst = pl.ds(r + nl * BAND_BLOCK * dil, BAND_BLOCK, stride=dil) if dil > 1 else pl.ds(
                    nl * BAND_BLOCK, BAND_BLOCK)
                og_ref[g, dst, :] = o[w]
                lg_ref[g, dst, :] = jnp.broadcast_to(lse[w], (BAND_BLOCK, LANES))
    rows = 256
    for rc in range(DIL_CHUNK // rows):
        sl = slice(rc * rows, (rc + 1) * rows)
        lses = [lg_ref[g, sl, :] for g in range(n_g)]
        mx = functools.reduce(jnp.maximum, lses)
        ws = [jnp.exp(l - mx) for l in lses]
        den = functools.reduce(lambda a, b_: a + b_, ws)
        num = functools.reduce(lambda a, b_: a + b_, [ws[g] * og_ref[g, sl, :] for g in range(n_g)])
        o_ref[0, sl, :] = (num / den).astype(o_ref.dtype)


def _dilated_attention(hs, bias_tiles):
    b, s, _ = hs[0].shape
    in_specs = []
    args = []
    for hg in hs:
        for part in range(3):
            in_specs.append(pl.BlockSpec((1, s, DIL_HEAD_DIM),
                                         lambda bi, hi, ci, part=part: (bi, 0, part * DIL_HEADS + hi)))
            args.append(hg)
    in_specs.append(pl.BlockSpec((len(DIL_GROUPS), 1, BAND_BLOCK, 2 * BAND_BLOCK),
                                 lambda bi, hi, ci: (0, hi, 0, 0)))
    return pl.pallas_call(
        functools.partial(_dil_kernel, seq=s),
        grid=(b, DIL_HEADS, s // DIL_CHUNK),
        in_specs=in_specs,
        out_specs=pl.BlockSpec((1, DIL_CHUNK, DIL_HEAD_DIM), lambda bi, hi, ci: (bi, ci, hi)),
        out_shape=jax.ShapeDtypeStruct((b, s, DIL_HEADS * DIL_HEAD_DIM), BF16),
        scratch_shapes=[pltpu.VMEM((len(DIL_GROUPS), DIL_CHUNK, LANES), F32),
                        pltpu.VMEM((len(DIL_GROUPS), DIL_CHUNK, LANES), F32)],
        compiler_params=pltpu.CompilerParams(dimension_semantics=("parallel", "parallel", "arbitrary")),
        name="dilated_attention",
    )(*args, bias_tiles)


def _outproj_kernel(*refs, n_in, with_router):
    ins = refs[:n_in]
    ws = refs[n_in:2 * n_in]
    x_ref, g_ref, b_ref = refs[2 * n_in:2 * n_in + 3]
    rest = refs[2 * n_in + 3:]
    mix = _dot(ins[0][...], ws[0][...])
    for a_ref, w_ref in zip(ins[1:], ws[1:]):
        mix = mix + _dot(a_ref[...], w_ref[...])
    xn = _layer_norm(DEEPNORM_ALPHA * x_ref[...] + mix, g_ref[...], b_ref[...])
    if not with_router:
        o_ref, ob_ref = rest
    else:
        router_ref, o_ref, gates_ref, counts_ref, run_ref, tri_ref = rest
        ob_ref = None
        tm = xn.shape[0]

        @pl.when(pl.program_id(0) == 0)
        def _():
            run_ref[...] = jnp.zeros_like(run_ref)
            tri_ref[...] = (lax.broadcasted_iota(jnp.int32, (tm, tm), 0)
                            > lax.broadcasted_iota(jnp.int32, (tm, tm), 1)).astype(BF16)

        logits = _dot(xn.astype(BF16), router_ref[...])
        lane = lax.broadcasted_iota(jnp.int32, logits.shape, 1).astype(F32)
        lg = jnp.where(lane < N_EXPERTS, logits, -jnp.inf)
        m1 = jnp.max(lg, axis=1, keepdims=True)
        i1 = jnp.min(jnp.where(lg == m1, lane, float(LANES)), axis=1, keepdims=True)
        lg2 = jnp.where(lane == i1, -jnp.inf, lg)
        m2 = jnp.max(lg2, axis=1, keepdims=True)
        i2 = jnp.min(jnp.where(lg2 == m2, lane, float(LANES)), axis=1, keepdims=True)
        e2 = jnp.exp(m2 - m1)
        den = 1.0 + e2
        oh1 = jnp.where(lane == i1, 1.0, 0.0)
        oh2 = jnp.where(lane == i2, 1.0, 0.0)
        both = oh1 + oh2
        before = _dot(tri_ref[...], both.astype(BF16)) + run_ref[...]
        r1 = jnp.sum(before * oh1, axis=1, keepdims=True)
        r2 = jnp.sum(before * oh2, axis=1, keepdims=True)
        run_ref[...] += jnp.sum(both, axis=0, keepdims=True)
        counts_ref[...] = run_ref[...]
        gates_ref[...] = (jnp.where(lane == 0.0, i1, 0.0) + jnp.where(lane == 1.0, i2, 0.0)
                          + jnp.where(lane == 2.0, 1.0 / den, 0.0) + jnp.where(lane == 3.0, e2 / den, 0.0)
                          + jnp.where(lane == 4.0, r1, 0.0) + jnp.where(lane == 5.0, r2, 0.0))
    o_ref[...] = xn
    if ob_ref is not None:
        ob_ref[...] = xn.astype(BF16)


def _outproj_ln(ins, ws, x, ln_g, ln_b, router=None, tm=512):
    t, d = x.shape
    n_in = len(ins)
    row = lambda width: pl.BlockSpec((tm, width), lambda i: (i, 0))
    full = lambda a: pl.BlockSpec(a.shape, lambda i: (0, 0))
    vec = lambda v: v.reshape(1, d).astype(F32)
    args = list(ins) + list(ws) + [x, vec(ln_g), vec(ln_b)]
    in_specs = [row(a.shape[1]) for a in ins] + [full(w) for w in ws] + [row(d), full(vec(ln_g)), full(vec(ln_b))]
    out_shape = [jax.ShapeDtypeStruct((t, d), F32), jax.ShapeDtypeStruct((t, d), BF16)]
    out_specs = [row(d), row(d)]
    if router is not None:
        router_pad = jnp.zeros((d, LANES), BF16).at[:, :N_EXPERTS].set(router.astype(BF16))
        args.append(router_pad)
        in_specs.append(full(router_pad))
        out_shape = [out_shape[0], jax.ShapeDtypeStruct((t, LANES), F32), jax.ShapeDtypeStruct((1, LANES), F32)]
        out_specs = [row(d), row(LANES), pl.BlockSpec((1, LANES), lambda i: (0, 0))]
        scratch = [pltpu.VMEM((1, LANES), F32), pltpu.VMEM((tm, tm), BF16)]
    else:
        scratch = []
    semantics = "arbitrary" if router is not None else "parallel"
    return pl.pallas_call(
        functools.partial(_outproj_kernel, n_in=n_in, with_router=router is not None),
        grid=(t // tm,),
        in_specs=in_specs,
        out_specs=out_specs,
        out_shape=out_shape,
        scratch_shapes=scratch,
        compiler_params=pltpu.CompilerParams(dimension_semantics=(semantics,)),
        name="outproj_ln",
    )(*args)


def _ffn_kernel(xb_ref, x_ref, wg_ref, wu_ref, wd_ref, g_ref, b_ref, o_ref, ob_ref, acc_ref):
    f = pl.program_id(1)

    @pl.when(f == 0)
    def _():
        acc_ref[...] = jnp.zeros_like(acc_ref)

    xb = xb_ref[...]
    hg = _dot(xb, wg_ref[...])
    hu = _dot(xb, wu_ref[...])
    act = (hg * _sigmoid(hg) * hu).astype(BF16)
    acc_ref[...] += _dot(act, wd_ref[...])

    @pl.when(f == pl.num_programs(1) - 1)
    def _():
        xn = _layer_norm(DEEPNORM_ALPHA * x_ref[...] + acc_ref[...], g_ref[...], b_ref[...])
        o_ref[...] = xn
        ob_ref[...] = xn.astype(BF16)


def _ffn_ln(xb, x, wg, wu, wd, ln_g, ln_b, tm=512, tf=1408):
    t, d = x.shape
    ff = wg.shape[1]
    vec = lambda v: v.reshape(1, d).astype(F32)
    return pl.pallas_call(
        _ffn_kernel,
        grid=(t // tm, ff // tf),
        in_specs=[pl.BlockSpec((tm, d), lambda i, f: (i, 0)),
                  pl.BlockSpec((tm, d), lambda i, f: (i, 0)),
                  pl.BlockSpec((d, tf), lambda i, f: (0, f)),
                  pl.BlockSpec((d, tf), lambda i, f: (0, f)),
                  pl.BlockSpec((tf, d), lambda i, f: (f, 0)),
                  pl.BlockSpec((1, d), lambda i, f: (0, 0)),
                  pl.BlockSpec((1, d), lambda i, f: (0, 0))],
        out_specs=[pl.BlockSpec((tm, d), lambda i, f: (i, 0)),
                   pl.BlockSpec((tm, d), lambda i, f: (i, 0))],
        out_shape=[jax.ShapeDtypeStruct((t, d), F32), jax.ShapeDtypeStruct((t, d), BF16)],
        scratch_shapes=[pltpu.VMEM((tm, d), F32)],
        compiler_params=pltpu.CompilerParams(dimension_semantics=("parallel", "arbitrary")),
        name="ffn_ln",
    )(xb, x, wg, wu, wd, vec(ln_g), vec(ln_b))


_MOE_TM = 1024


def _route(route, counts, n_tiles):
    expert = route[:, 0:2].astype(jnp.int32)
    rank = route[:, 4:6].astype(jnp.int32)
    counts = counts[0, :N_EXPERTS].astype(jnp.int32)
    padded = (counts + _MOE_TM - 1) // _MOE_TM * _MOE_TM
    ends = jnp.cumsum(padded)
    starts = ends - padded
    onehot = expert[:, :, None] == jnp.arange(N_EXPERTS, dtype=jnp.int32)
    pos = (rank + jnp.sum(jnp.where(onehot, starts, 0), axis=-1)).reshape(-1)
    tile_start = jnp.arange(n_tiles, dtype=jnp.int32) * _MOE_TM
    tile_expert = jnp.minimum(jnp.sum((tile_start[:, None] >= ends[None, :]).astype(jnp.int32), axis=1),
                              N_EXPERTS - 1)
    return pos.astype(jnp.int32), tile_expert.astype(jnp.int32)


def _row_scatter_kernel(pos_ref, x_ref, init_ref, o_ref, sem, *, tm):
    del init_ref

    def body(j, carry):
        for choice in range(2):
            p = pos_ref[0, 0, 2 * j + choice]
            pltpu.make_async_copy(x_ref.at[pl.ds(j, 1)], o_ref.at[pl.ds(p, 1)], sem).start()
        return carry

    lax.fori_loop(0, tm, body, 0, unroll=8)
    for _ in range(2):
        pltpu.make_async_copy(x_ref, o_ref.at[pl.ds(0, tm)], sem).wait()


def _row_scatter(x, pos, init, tm=512):
    t, d = x.shape
    n_rows = init.shape[0]
    return pl.pallas_call(
        functools.partial(_row_scatter_kernel, tm=tm),
        grid=(t // tm,),
        in_specs=[pl.BlockSpec((1, 1, 2 * tm), lambda i: (i, 0, 0), memory_space=pltpu.SMEM),
                  pl.BlockSpec((tm, d), lambda i: (i, 0)),
                  pl.BlockSpec(memory_space=pl.ANY)],
        out_specs=pl.BlockSpec(memory_space=pl.ANY),
        out_shape=jax.ShapeDtypeStruct((n_rows, d), x.dtype),
        scratch_shapes=[pltpu.SemaphoreType.DMA(())],
        input_output_aliases={2: 0},
        compiler_params=pltpu.CompilerParams(dimension_semantics=("arbitrary",)),
        name="moe_row_scatter",
    )(pos.reshape(t // tm, 1, 2 * tm), x, init)


def _expert_ffn_kernel(te_ref, x_ref, wg_ref, wu_ref, wd_ref, o_ref, acc_ref):
    del te_ref
    f = pl.program_id(1)

    @pl.when(f == 0)
    def _():
        acc_ref[...] = jnp.zeros_like(acc_ref)

    xb = x_ref[...].astype(BF16)
    hg = _dot(xb, wg_ref[0].astype(BF16))
    hu = _dot(xb, wu_ref[0].astype(BF16))
    act = (hg * _sigmoid(hg) * hu).astype(BF16)
    acc_ref[...] += _dot(act, wd_ref[0].astype(BF16))

    @pl.when(f == pl.num_programs(1) - 1)
    def _():
        o_ref[...] = acc_ref[...]


def _expert_ffn(xs, tile_expert, wg, wu, wd, tf=512):
    n_rows, d = xs.shape
    ff = wg.shape[2]
    grid_spec = pltpu.PrefetchScalarGridSpec(
        num_scalar_prefetch=1,
        grid=(n_rows // _MOE_TM, ff // tf),
        in_specs=[pl.BlockSpec((_MOE_TM, d), lambda i, f, te: (i, 0)),
                  pl.BlockSpec((1, d, tf), lambda i, f, te: (te[i], 0, f)),
                  pl.BlockSpec((1, d, tf), lambda i, f, te: (te[i], 0, f)),
                  pl.BlockSpec((1, tf, d), lambda i, f, te: (te[i], f, 0))],
        out_specs=pl.BlockSpec((_MOE_TM, d), lambda i, f, te: (i, 0)),
        scratch_shapes=[pltpu.VMEM((_MOE_TM, d), F32)])
    return pl.pallas_call(
        _expert_ffn_kernel,
        grid_spec=grid_spec,
        out_shape=jax.ShapeDtypeStruct((n_rows, d), F32),
        compiler_params=pltpu.CompilerParams(dimension_semantics=("parallel", "arbitrary")),
        name="moe_expert_ffn",
    )(tile_expert, xs, wg, wu, wd)


def _combine_kernel(pos_ref, pos_next_ref, route_ref, x_ref, g_ref, b_ref, y_ref, o_ref, ybuf_ref, sems, *, tm):
    i = pl.program_id(0)
    n = pl.num_programs(0)

    def gather(src_pos_ref, slot):
        def body(j, carry):
            for choice in range(2):
                p = src_pos_ref[0, 0, 2 * j + choice]
                pltpu.make_async_copy(y_ref.at[pl.ds(p, 1)], ybuf_ref.at[slot, choice, pl.ds(j, 1)],
                                      sems.at[slot]).start()
            return carry

        lax.fori_loop(0, tm, body, 0, unroll=8)

    @pl.when(i == 0)
    def _():
        gather(pos_ref, 0)

    slot = i % 2

    @pl.when(i + 1 < n)
    def _():
        gather(pos_next_ref, 1 - slot)

    for choice in range(2):
        pltpu.make_async_copy(y_ref.at[pl.ds(0, tm)], ybuf_ref.at[slot, choice], sems.at[slot]).wait()
    route = route_ref[...]
    y = route[:, 2:3] * ybuf_ref[slot, 0] + route[:, 3:4] * ybuf_ref[slot, 1]
    o_ref[...] = _layer_norm(DEEPNORM_ALPHA * x_ref[...] + y, g_ref[...], b_ref[...])


def _combine_ln(ys, pos, route, x, ln_g, ln_b, tm=512):
    t, d = x.shape
    vec = lambda v: v.reshape(1, d).astype(F32)
    last = t // tm - 1
    pos3 = pos.reshape(t // tm, 1, 2 * tm)
    return pl.pallas_call(
        functools.partial(_combine_kernel, tm=tm),
        grid=(t // tm,),
        in_specs=[pl.BlockSpec((1, 1, 2 * tm), lambda i: (i, 0, 0), memory_space=pltpu.SMEM),
                  pl.BlockSpec((1, 1, 2 * tm), lambda i: (jnp.minimum(i + 1, last), 0, 0), memory_space=pltpu.SMEM),
                  pl.BlockSpec((tm, LANES), lambda i: (i, 0)),
                  pl.BlockSpec((tm, d), lambda i: (i, 0)),
                  pl.BlockSpec((1, d), lambda i: (0, 0)),
                  pl.BlockSpec((1, d), lambda i: (0, 0)),
                  pl.BlockSpec(memory_space=pl.ANY)],
        out_specs=pl.BlockSpec((tm, d), lambda i: (i, 0)),
        out_shape=jax.ShapeDtypeStruct((t, d), F32),
        scratch_shapes=[pltpu.VMEM((2, 2, tm, d), F32), pltpu.SemaphoreType.DMA((2,))],
        compiler_params=pltpu.CompilerParams(dimension_semantics=("arbitrary",)),
        name="moe_combine_ln",
    )(pos3, pos3, route, x, vec(ln_g), vec(ln_b), ys)


def _moe_ln(x, route, counts, wg, wu, wd, ln_g, ln_b):
    t, _ = x.shape
    n_tiles = 2 * t // _MOE_TM + N_EXPERTS
    pos, tile_expert = _route(route, counts, n_tiles)
    xs = _row_scatter(x, pos, jnp.zeros((n_tiles * _MOE_TM, x.shape[1]), x.dtype))
    ys = _expert_ffn(xs, tile_expert, wg, wu, wd)
    return _combine_ln(ys, pos, route, x, ln_g, ln_b)


def _even_layer(x, xb, batch, seq, dsa_bias, w_in, conv_w, conv_b, conv_ln_g, conv_ln_b, w_out,
                ln1_g, ln1_b, ffn_wg, ffn_wu, ffn_wd, ln2_g, ln2_b):
    w_in_p = jnp.zeros((D_MODEL, EVEN_IN_PAD), BF16).at[:, :w_in.shape[1]].set(w_in.astype(BF16))
    h = _matmul(x if xb is None else xb, w_in_p, 1024, EVEN_IN_PAD, BF16).reshape(batch, seq, EVEN_IN_PAD)
    a_out = _conformer_conv(h, conv_w, conv_b, conv_ln_g, conv_ln_b)
    att = _dsa_attention(h, dsa_bias)
    w_out_b = w_out.astype(BF16)
    x1, x1b = _outproj_ln([a_out.reshape(-1, CONV_CH), att.reshape(-1, DSA_HEADS * DSA_HEAD_DIM)],
                          [w_out_b[:CONV_CH], w_out_b[CONV_CH:]], x, ln1_g, ln1_b)
    return _ffn_ln(x1b, x1, ffn_wg.astype(BF16), ffn_wu.astype(BF16), ffn_wd.astype(BF16), ln2_g, ln2_b)


def _odd_layer(x, xb, batch, seq, dil_bias, w_in, w_out, ln1_g, ln1_b, router, moe_wg, moe_wu, moe_wd,
               ln2_g, ln2_b):
    gw = 3 * DIL_HEADS * DIL_HEAD_DIM
    w_in_b = w_in.astype(BF16)
    hs = [_matmul_residue_major(x if xb is None else xb, w_in_b[:, g * gw:(g + 1) * gw], batch, seq, dil, 2048, 768)
          for g, (_, dil) in enumerate(DIL_GROUPS)]
    o = _dilated_attention(hs, dil_bias).reshape(-1, DIL_HEADS * DIL_HEAD_DIM)
    x1, route, counts = _outproj_ln([o], [w_out.astype(BF16)], x, ln1_g, ln1_b, router=router)
    out = _moe_ln(x1, route, counts, moe_wg, moe_wu, moe_wd, ln2_g, ln2_b)
    return out, out.astype(BF16)


def kernel(x, rel_bias, even_w_in, even_conv_w, even_conv_b, even_conv_ln_g, even_conv_ln_b, even_w_out, even_ln1_g, even_ln1_b, even_ffn_wg, even_ffn_wu, even_ffn_wd, even_ln2_g, even_ln2_b, odd_w_in, odd_w_out, odd_ln1_g, odd_ln1_b, odd_router, odd_moe_wg, odd_moe_wu, odd_moe_wd, odd_ln2_g, odd_ln2_b):
    batch, seq, d = x.shape
    assert d == D_MODEL and seq % DIL_CHUNK == 0
    depth = even_w_in.shape[0] + odd_w_in.shape[0]
    assert depth == DEPTH
    dsa_bias = _dsa_bias_tiles(rel_bias, seq)
    dil_bias = _dil_bias_tiles(rel_bias)
    xf = x.reshape(batch * seq, d)
    xb = None
    for layer in range(depth):
        i = layer // 2
        if layer % 2 == 0:
            xf, xb = _even_layer(xf, xb, batch, seq, dsa_bias, even_w_in[i], even_conv_w[i], even_conv_b[i],
                                 even_conv_ln_g[i], even_conv_ln_b[i], even_w_out[i], even_ln1_g[i],
                                 even_ln1_b[i], even_ffn_wg[i], even_ffn_wu[i], even_ffn_wd[i],
                                 even_ln2_g[i], even_ln2_b[i])
        else:
            xf, xb = _odd_layer(xf, xb, batch, seq, dil_bias, odd_w_in[i], odd_w_out[i], odd_ln1_g[i],
                                odd_ln1_b[i], odd_router[i], odd_moe_wg[i], odd_moe_wu[i], odd_moe_wd[i],
                                odd_ln2_g[i], odd_ln2_b[i])
    return xf.reshape(batch, seq, d)
```

```python
import functools
import math

import jax
import jax.numpy as jnp
from jax import lax
from jax.experimental import pallas as pl
from jax.experimental.pallas import tpu as pltpu

BF16 = jnp.bfloat16
F32 = jnp.float32

D_MODEL = 1024
CONV_CH = 512
CONV_WIDTH = 31
DSA_HEADS = 8
DSA_HEAD_DIM = 64
IDX_HEADS = 8
IDX_DIM = 64
DSA_TOPK_MAX = 256
Q_BLOCK = 128
DIL_GROUPS = ((128, 1), (512, 4), (2048, 16))
DIL_HEADS = 8
DIL_HEAD_DIM = 128
BAND_BLOCK = 128
NUM_BUCKETS = 32
MAX_DISTANCE = 2048
N_EXPERTS = 8
LN_EPS = 1e-5
NEG = -1e30
DEPTH = 2
DEEPNORM_ALPHA = (2 * DEPTH) ** 0.25

LANES = 128
EVEN_IN_PAD = 3200
INT_MIN = -(2 ** 31)
DIL_CHUNK = BAND_BLOCK * max(d for _, d in DIL_GROUPS)

_PARALLEL2 = pltpu.CompilerParams(dimension_semantics=("parallel", "parallel"))


def _layer_norm(y, g, b):
    mu = jnp.mean(y, axis=-1, keepdims=True)
    yc = y - mu
    var = jnp.mean(yc * yc, axis=-1, keepdims=True)
    return yc * lax.rsqrt(var + LN_EPS) * g + b


def _sigmoid(x):
    return 1.0 / (1.0 + jnp.exp(-x))


def _dot_nt(a, b):
    return lax.dot_general(a, b, (((1,), (1,)), ((), ())), preferred_element_type=F32)


def _dot(a, b):
    return jnp.dot(a, b, preferred_element_type=F32)


def _mm_kernel(x_ref, w_ref, o_ref):
    o_ref[...] = _dot(x_ref[...].astype(BF16), w_ref[...]).astype(o_ref.dtype)


def _matmul(x, w, tm, tn, out_dtype):
    t, k = x.shape
    n = w.shape[1]
    return pl.pallas_call(
        _mm_kernel,
        grid=(t // tm, n // tn),
        in_specs=[pl.BlockSpec((tm, k), lambda i, j: (i, 0)),
                  pl.BlockSpec((k, tn), lambda i, j: (0, j))],
        out_specs=pl.BlockSpec((tm, tn), lambda i, j: (i, j)),
        out_shape=jax.ShapeDtypeStruct((t, n), out_dtype),
        compiler_params=_PARALLEL2,
        name="matmul",
    )(x, w)


_MAX_ROW_STRIDE = 4


def _mm_perm_kernel(x_ref, w_ref, o_ref, xp_ref, a_ref, b_ref, *, dil, tm):
    rows = tm // dil

    @pl.when(pl.program_id(1) == 0)
    def _():
        for c in range(x_ref.shape[1] // LANES):
            cols = slice(c * LANES, (c + 1) * LANES)
            a_ref[...] = x_ref[:, cols].astype(F32)
            if dil <= _MAX_ROW_STRIDE:
                for r in range(dil):
                    xp_ref[r * rows:(r + 1) * rows, cols] = a_ref[pl.ds(r, rows, stride=dil), :].astype(BF16)
                continue
            s1 = _MAX_ROW_STRIDE
            s2 = dil // s1
            assert s2 <= _MAX_ROW_STRIDE
            rows1 = tm // s1
            for r1 in range(s1):
                b_ref[r1 * rows1:(r1 + 1) * rows1, :] = a_ref[pl.ds(r1, rows1, stride=s1), :]
            for r1 in range(s1):
                for r2 in range(s2):
                    r = s1 * r2 + r1
                    xp_ref[r * rows:(r + 1) * rows, cols] = (
                        b_ref[pl.ds(r1 * rows1 + r2, rows, stride=s2), :].astype(BF16))

    acc = _dot(xp_ref[...], w_ref[...])
    o_ref[0] = acc.reshape(dil, rows, acc.shape[1]).astype(o_ref.dtype)


def _matmul_residue_major(x, w, batch, seq, dil, tm, tn):
    t, k = x.shape
    n = w.shape[1]
    if dil == 1:
        return _matmul(x, w, tm, tn, BF16).reshape(batch, seq, n)
    tiles_per_seq = seq // tm
    out = pl.pallas_call(
        functools.partial(_mm_perm_kernel, dil=dil, tm=tm),
        grid=(t // tm, n // tn),
        in_specs=[pl.BlockSpec((tm, k), lambda i, j: (i, 0)),
                  pl.BlockSpec((k, tn), lambda i, j: (0, j))],
        out_specs=pl.BlockSpec((1, dil, tm // dil, tn),
                               lambda i, j: (i // tiles_per_seq, 0, i % tiles_per_seq, j)),
        out_shape=jax.ShapeDtypeStruct((batch, dil, seq // dil, n), BF16),
        scratch_shapes=[pltpu.VMEM((tm, k), BF16), pltpu.VMEM((tm, LANES), F32), pltpu.VMEM((tm, LANES), F32)],
        compiler_params=pltpu.CompilerParams(dimension_semantics=("parallel", "arbitrary")),
        name="matmul_residue_major",
    )(x, w)
    return out.reshape(batch, seq, n)


_CONV_HALO = 32
_CONV_ROWS = 128
_SUBLANES = 8


def _conv_kernel(val_ref, gate_ref, w_ref, cb_ref, g_ref, b_ref, o_ref, ext_ref, *, ts):
    s = pl.program_id(1)

    @pl.when(s == 0)
    def _():
        ext_ref[0, 0:_CONV_HALO, :] = jnp.zeros((_CONV_HALO, CONV_CH), F32)

    val = val_ref[0].astype(F32)
    gate = gate_ref[0].astype(F32)
    ext_ref[0, _CONV_HALO:_CONV_HALO + ts, :] = val * _sigmoid(gate)
    span = ts + _CONV_HALO - _SUBLANES
    for k in range(1, _SUBLANES):
        ext_ref[k, 0:span, :] = ext_ref[0, k:k + span, :]
    first_tap = _CONV_HALO - (CONV_WIDTH - 1)
    for rc in range(ts // _CONV_ROWS):
        r0 = rc * _CONV_ROWS
        acc = jnp.zeros((_CONV_ROWS, CONV_CH), F32)
        for j in range(CONV_WIDTH):
            shift = (first_tap + j) % _SUBLANES
            lo = r0 + first_tap + j - shift
            acc = acc + ext_ref[shift, lo:lo + _CONV_ROWS, :] * w_ref[j:j + 1, :]
        y = _layer_norm(acc + cb_ref[...], g_ref[...], b_ref[...])
        o_ref[0, r0:r0 + _CONV_ROWS, :] = (y * _sigmoid(y)).astype(o_ref.dtype)
    ext_ref[0, 0:_CONV_HALO, :] = ext_ref[0, ts:ts + _CONV_HALO, :]


def _conformer_conv(h, conv_w, conv_b, ln_g, ln_b, ts=512):
    b, s, _ = h.shape
    w_pad = jnp.zeros((32, CONV_CH), F32).at[:CONV_WIDTH].set(conv_w)
    vec = lambda v: v.reshape(1, CONV_CH).astype(F32)
    full = lambda shape: pl.BlockSpec(shape, lambda bi, si: (0, 0))
    return pl.pallas_call(
        functools.partial(_conv_kernel, ts=ts),
        grid=(b, s // ts),
        in_specs=[pl.BlockSpec((1, ts, CONV_CH), lambda bi, si: (bi, si, 0)),
                  pl.BlockSpec((1, ts, CONV_CH), lambda bi, si: (bi, si, 1)),
                  full((32, CONV_CH)), full((1, CONV_CH)), full((1, CONV_CH)), full((1, CONV_CH))],
        out_specs=pl.BlockSpec((1, ts, CONV_CH), lambda bi, si: (bi, si, 0)),
        out_shape=jax.ShapeDtypeStruct((b, s, CONV_CH), BF16),
        scratch_shapes=[pltpu.VMEM((_SUBLANES, ts + _CONV_HALO, CONV_CH), F32)],
        compiler_params=pltpu.CompilerParams(dimension_semantics=("parallel", "arbitrary")),
        name="conformer_conv",
    )(h, h, w_pad, vec(conv_b), vec(ln_g), vec(ln_b))


def _rel_bucket(dist):
    max_exact = NUM_BUCKETS // 2
    n = dist.astype(jnp.int32)
    nf = jnp.maximum(n, 1).astype(F32)
    large = max_exact + (jnp.log(nf / max_exact) / math.log(MAX_DISTANCE / max_exact)
                         * (NUM_BUCKETS - max_exact)).astype(jnp.int32)
    large = jnp.minimum(large, NUM_BUCKETS - 1)
    return jnp.where(n < max_exact, n, large)


def _bias_by_distance(rel_bias, n):
    return rel_bias[_rel_bucket(jnp.arange(n, dtype=jnp.int32))].astype(F32).T


def _toeplitz_kernel(win_ref, o_ref):
    n_win, span = win_ref.shape
    for n in range(n_win):
        y = jnp.broadcast_to(win_ref[n:n + 1, :], (LANES, span))
        y = pltpu.roll(y, 0, 1, stride=1, stride_axis=0)
        o_ref[n] = y[:, LANES:]


def _toeplitz_tiles(windows, per_step=8):
    n, span = windows.shape
    width = span - LANES
    return pl.pallas_call(
        _toeplitz_kernel,
        grid=(n // per_step,),
        in_specs=[pl.BlockSpec((per_step, span), lambda i: (i, 0))],
        out_specs=pl.BlockSpec((per_step, LANES, width), lambda i: (i, 0, 0)),
        out_shape=jax.ShapeDtypeStruct((n, LANES, width), F32),
        compiler_params=pltpu.CompilerParams(dimension_semantics=("parallel",)),
        name="bias_toeplitz",
    )(windows)


def _dsa_bias_tiles(rel_bias, seq):
    nblk = seq // Q_BLOCK
    heads = rel_bias.shape[1]
    f = _bias_by_distance(rel_bias, seq)
    u = jnp.concatenate([jnp.broadcast_to(f[:, :1], (heads, 2 * Q_BLOCK)), f], axis=1)
    blocks = u.reshape(heads, nblk + 2, Q_BLOCK)
    windows = jnp.concatenate([blocks[:, :-1], blocks[:, 1:]], axis=-1)
    windows = windows.transpose(1, 0, 2).reshape((nblk + 1) * heads, 2 * Q_BLOCK)
    return _toeplitz_tiles(windows).reshape(nblk + 1, heads, Q_BLOCK, Q_BLOCK)


def _dil_bias_tiles(rel_bias):
    max_dil = max(d for _, d in DIL_GROUPS)
    heads = rel_bias.shape[1]
    span = 2 * BAND_BLOCK
    f = _bias_by_distance(rel_bias, span * max_dil + 1)
    windows = []
    for _, dil in DIL_GROUPS:
        fd = f[:, 0:span * dil + 1:dil]
        tail = jnp.broadcast_to(fd[:, :1], (heads, BAND_BLOCK - 1))
        windows.append(jnp.concatenate([fd[:, ::-1], tail], axis=1))
    tiles = _toeplitz_tiles(jnp.concatenate(windows, axis=0))
    return tiles.reshape(len(DIL_GROUPS), heads, BAND_BLOCK, span)


_KC = 2 * Q_BLOCK
_VT_ROWS = DSA_HEAD_DIM + 16


def _sortable_key(score):
    score = jnp.where(score == 0.0, 0.0, score)
    bits = lax.bitcast_convert_type(score, jnp.int32)
    return jnp.where(bits < 0, bits ^ jnp.int32(0x7FFFFFFF), bits)


def _rows_reduce(x, op):
    rows = x.shape[0]
    return op(op(x.reshape(rows // 8, 8, LANES), axis=0), axis=0, keepdims=True)


def _keys_reduce(x, op):
    h, keys, _ = x.shape
    return op(op(x.reshape(h, keys // 8, 8, LANES), axis=1), axis=1, keepdims=True)


_QG = 2


def _dsa_t_kernel(q_ref, qi_ref, wq_ref, k_ref, vt_ref, ki_ref, bias_hbm, o_ref,
                  bias_ref, keys_ref, qm_ref, acc_ref, dots_a_ref, dots_b_ref, sem, *, topk):
    b = pl.program_id(0)
    i2 = pl.program_id(1)

    @pl.when((b == 0) & (i2 == 0))
    def _():
        cp = pltpu.make_async_copy(bias_hbm, bias_ref, sem)
        cp.start()
        cp.wait()

    nch = i2 + 1
    key_off = lax.broadcasted_iota(jnp.int32, (_KC, LANES), 0)
    q_lane = lax.broadcasted_iota(jnp.int32, (_KC, LANES), 1)
    lane = lax.broadcasted_iota(jnp.int32, (Q_BLOCK, LANES), 1)
    low_half = lane < DSA_HEAD_DIM
    n_heads = _QG * DSA_HEADS

    w_ts, qi_heads = [], []
    for g in range(_QG):
        rows = slice(g * Q_BLOCK, (g + 1) * Q_BLOCK)
        w_ts.append((wq_ref[0, rows, :].astype(F32) * (IDX_HEADS ** -0.5 * IDX_DIM ** -0.5)).T)
        qi = qi_ref[0, rows, :].astype(F32)
        for p in range(IDX_HEADS // 2):
            pair = qi[:, p * LANES:(p + 1) * LANES]
            qi_heads.append(jnp.where(low_half, pair, 0.0))
            qi_heads.append(jnp.where(low_half, pltpu.roll(pair, IDX_DIM, 1), 0.0))
        q_scaled = (q_ref[0, rows, :].astype(F32) * (DSA_HEAD_DIM ** -0.5)).astype(BF16)
        for p in range(DSA_HEADS // 2):
            qp = q_scaled[:, p * LANES:(p + 1) * LANES]
            qm_ref[g * DSA_HEADS + 2 * p] = jnp.where(low_half, qp, jnp.zeros_like(qp))
            qm_ref[g * DSA_HEADS + 2 * p + 1] = jnp.where(low_half, jnp.zeros_like(qp), qp)
    qi_stack = jnp.concatenate(qi_heads, axis=0).astype(BF16)

    def index_dots(c, dst_ref):
        c = jnp.minimum(c, nch - 1)
        k0 = pl.multiple_of(c * _KC, _KC)
        ki = ki_ref[0, pl.ds(k0, _KC), :]
        dst_ref[...] = _dot_nt(ki, qi_stack)

    def consume_dots(c, src_ref):
        k0 = c * _KC
        for g in range(_QG):
            score = jnp.zeros((_KC, LANES), F32)
            for h in range(IDX_HEADS):
                w_row = w_ts[g][IDX_DIM + h:IDX_DIM + h + 1, :]
                col = (g * IDX_HEADS + h) * Q_BLOCK
                score = score + w_row * jnp.maximum(src_ref[:, col:col + Q_BLOCK], 0.0)
            q_pos = (_QG * i2 + g) * Q_BLOCK + q_lane
            keys_ref[g, c] = jnp.where(k0 + key_off <= q_pos, _sortable_key(score), jnp.int32(INT_MIN))

    index_dots(0, dots_a_ref)

    def score_body(c2, carry):
        index_dots(2 * c2 + 1, dots_b_ref)
        consume_dots(2 * c2, dots_a_ref)
        index_dots(2 * c2 + 2, dots_a_ref)
        consume_dots(2 * c2 + 1, dots_b_ref)
        return carry

    lax.fori_loop(0, (nch + 1) // 2, score_body, 0)

    def count_ge(cand):
        cb = jnp.broadcast_to(cand, (_QG, 8, LANES))[:, None, None]
        chains = 4

        def body(c2, acc):
            for half in range(2):
                kk = keys_ref[:, 2 * c2 + half].reshape(_QG, _KC // (8 * chains), chains, 8, LANES)
                acc = acc + jnp.sum(jnp.where(kk >= cb, 1.0, 0.0), axis=1)
            return acc

        acc = lax.fori_loop(0, (nch + 1) // 2, body, jnp.zeros((_QG, chains, 8, LANES), F32))
        return jnp.sum(jnp.sum(acc, axis=1), axis=1, keepdims=True)

    kf = float(topk)
    zero = jnp.zeros((_QG, 1, LANES), jnp.int32)
    thr0 = jnp.where(count_ge(zero) >= kf, zero, jnp.full((_QG, 1, LANES), INT_MIN, jnp.int32))

    def bit_body(t, thr):
        cand = thr | jnp.left_shift(jnp.int32(1), 30 - t)
        return jnp.where(count_ge(cand) >= kf, cand, thr)

    thr = lax.fori_loop(0, 31, bit_body, thr0)
    n_gt = count_ge(jnp.where(thr == jnp.int32(2 ** 31 - 1), thr, thr + 1))
    need = kf - n_gt
    tri = (lax.broadcasted_iota(jnp.int32, (_KC, _KC), 1)
           <= lax.broadcasted_iota(jnp.int32, (_KC, _KC), 0)).astype(BF16)

    acc_ref[...] = jnp.zeros_like(acc_ref)

    def attn_body(c, carry):
        ms, tie_carry = carry
        k0 = pl.multiple_of(c * _KC, _KC)
        kk = keys_ref[:, c]
        causal = kk > jnp.int32(INT_MIN)
        eq = (kk == thr) & causal
        eq_f = jnp.where(eq, 1.0, 0.0)
        eq_b = eq_f.astype(BF16)
        rank = jnp.stack([_dot(tri, eq_b[g]) for g in range(_QG)]) + tie_carry
        sel = ((kk > thr) & causal) | (eq & (rank <= need))
        tie_carry = tie_carry + jnp.stack([_rows_reduce(eq_f[g], jnp.sum) for g in range(_QG)])
        kc = k_ref[0, pl.ds(k0, _KC), :]
        k_heads = jnp.stack([kc[:, (h // 2) * LANES:(h // 2 + 1) * LANES] for h in range(DSA_HEADS)] * _QG)
        s = jnp.einsum("hkd,hqd->hkq", k_heads, qm_ref[...], preferred_element_type=F32)
        biases = []
        for g in range(_QG):
            d = _QG * i2 + g - 2 * c
            biases.append(jnp.concatenate([bias_ref[d + 1], bias_ref[d]], axis=1))
        bias = jnp.concatenate(biases, axis=0)
        sel_h = jnp.broadcast_to(sel[:, None], (_QG, DSA_HEADS, _KC, LANES)).reshape(n_heads, _KC, LANES)
        s = jnp.where(sel_h, s + bias, NEG)
        m_new = jnp.maximum(ms, _keys_reduce(s, jnp.max))
        alpha = jnp.exp(ms - m_new)
        pr = jnp.exp((s - m_new).astype(BF16))
        vt = jnp.concatenate([vt_ref[0, c]] * _QG, axis=0)
        pv = jnp.einsum("hdk,hkq->hdq", vt, pr, preferred_element_type=F32)
        acc_ref[...] = alpha * acc_ref[...] + pv
        return m_new, tie_carry

    init = (jnp.full((n_heads, 1, LANES), NEG, F32), jnp.zeros((_QG, 1, LANES), F32))
    lax.fori_loop(0, nch, attn_body, init)
    acc = acc_ref[...]
    out_t = acc[:, :DSA_HEAD_DIM] / acc[:, DSA_HEAD_DIM:DSA_HEAD_DIM + 1]
    for g in range(_QG):
        block = out_t[g * DSA_HEADS:(g + 1) * DSA_HEADS].reshape(DSA_HEADS * DSA_HEAD_DIM, Q_BLOCK)
        o_ref[0, g * Q_BLOCK:(g + 1) * Q_BLOCK, :] = block.T.astype(o_ref.dtype)


_VT_CHUNKS = 4


def _vt_kernel(v_ref, o_ref):
    for c in range(_VT_CHUNKS):
        vt = v_ref[0, c * _KC:(c + 1) * _KC, :].astype(F32).T
        o_ref[0, c, :, :DSA_HEAD_DIM, :] = vt.reshape(DSA_HEADS, DSA_HEAD_DIM, _KC).astype(o_ref.dtype)
        o_ref[0, c, :, DSA_HEAD_DIM:, :] = jnp.ones((DSA_HEADS, _VT_ROWS - DSA_HEAD_DIM, _KC), o_ref.dtype)


def _values_key_minor(h):
    b, s, _ = h.shape
    width = DSA_HEADS * DSA_HEAD_DIM
    v_block = (2 * CONV_CH + 2 * width) // width
    return pl.pallas_call(
        _vt_kernel,
        grid=(b, s // (_VT_CHUNKS * _KC)),
        in_specs=[pl.BlockSpec((1, _VT_CHUNKS * _KC, width), lambda bi, ci: (bi, ci, v_block))],
        out_specs=pl.BlockSpec((1, _VT_CHUNKS, DSA_HEADS, _VT_ROWS, _KC), lambda bi, ci: (bi, ci, 0, 0, 0)),
        out_shape=jax.ShapeDtypeStruct((b, s // _KC, DSA_HEADS, _VT_ROWS, _KC), BF16),
        compiler_params=_PARALLEL2,
        name="dsa_values_key_minor",
    )(h)


def _dsa_attention(h, bias_tiles):
    b, s, _ = h.shape
    width = DSA_HEADS * DSA_HEAD_DIM
    nblk = s // Q_BLOCK
    topk = min(DSA_TOPK_MAX, s // 4)
    kw_block = (CONV_CH * 2 + 4 * width) // LANES
    vt = _values_key_minor(h)
    return pl.pallas_call(
        functools.partial(_dsa_t_kernel, topk=topk),
        grid=(b, nblk // _QG),
        in_specs=[pl.BlockSpec((1, _QG * Q_BLOCK, width), lambda bi, qi: (bi, qi, 2)),
                  pl.BlockSpec((1, _QG * Q_BLOCK, width), lambda bi, qi: (bi, qi, 5)),
                  pl.BlockSpec((1, _QG * Q_BLOCK, LANES), lambda bi, qi: (bi, qi, kw_block)),
                  pl.BlockSpec((1, s, width), lambda bi, qi: (bi, 0, 3)),
                  pl.BlockSpec((1, s // _KC, DSA_HEADS, _VT_ROWS, _KC), lambda bi, qi: (bi, 0, 0, 0, 0)),
                  pl.BlockSpec((1, s, LANES), lambda bi, qi: (bi, 0, kw_block)),
                  pl.BlockSpec(memory_space=pl.ANY)],
        out_specs=pl.BlockSpec((1, _QG * Q_BLOCK, width), lambda bi, qi: (bi, qi, 0)),
        out_shape=jax.ShapeDtypeStruct((b, s, width), BF16),
        scratch_shapes=[pltpu.VMEM((nblk + 1, DSA_HEADS, Q_BLOCK, Q_BLOCK), F32),
                        pltpu.VMEM((_QG, s // _KC, _KC, Q_BLOCK), jnp.int32),
                        pltpu.VMEM((_QG * DSA_HEADS, Q_BLOCK, LANES), BF16),
                        pltpu.VMEM((_QG * DSA_HEADS, _VT_ROWS, Q_BLOCK), F32),
                        pltpu.VMEM((_KC, _QG * IDX_HEADS * Q_BLOCK), F32),
                        pltpu.VMEM((_KC, _QG * IDX_HEADS * Q_BLOCK), F32),
                        pltpu.SemaphoreType.DMA(())],
        compiler_params=pltpu.CompilerParams(dimension_semantics=("arbitrary", "arbitrary")),
        name="dsa_attention",
    )(h, h, h, h, vt, h, bias_tiles)


_DIL_WAVE = 16


def _dil_kernel(*refs, seq):
    n_g = len(DIL_GROUPS)
    qkv = refs[:3 * n_g]
    bias_ref, o_ref, og_ref, lg_ref = refs[3 * n_g:]
    c = pl.program_id(2)
    ii = lax.broadcasted_iota(jnp.int32, (BAND_BLOCK, BAND_BLOCK), 0)
    jj = lax.broadcasted_iota(jnp.int32, (BAND_BLOCK, BAND_BLOCK), 1)
    valid_prev = jj >= ii
    valid_cur = ii >= jj
    scale = DIL_HEAD_DIM ** -0.5
    for g, (window, dil) in enumerate(DIL_GROUPS):
        assert window // dil == BAND_BLOCK
        q_ref, k_ref, v_ref = qkv[3 * g:3 * g + 3]
        sub_len = seq // dil
        blocks_per_chunk = DIL_CHUNK // (BAND_BLOCK * dil)
        bias_prev = bias_ref[g, 0, :, :BAND_BLOCK]
        bias_cur = bias_ref[g, 0, :, BAND_BLOCK:]
        tiles = [(r, nl) for r in range(dil) for nl in range(blocks_per_chunk)]
        for w0 in range(0, len(tiles), _DIL_WAVE):
            wave = tiles[w0:w0 + _DIL_WAVE]
            cur, prev, mask_prev = [], [], []
            for r, nl in wave:
                row0 = pl.multiple_of(r * sub_len + c * (DIL_CHUNK // dil) + nl * BAND_BLOCK, BAND_BLOCK)
                cur.append(pl.ds(row0, BAND_BLOCK))
                prev.append(pl.ds(pl.multiple_of(jnp.maximum(row0 - BAND_BLOCK, 0), BAND_BLOCK), BAND_BLOCK))
                mask_prev.append(valid_prev if nl > 0 else (valid_prev & (c > 0)))
            gather = lambda ref, slices: jnp.stack([ref[0, sl, :] for sl in slices])
            qb = gather(q_ref, cur)
            s_cur = jnp.einsum("wqd,wkd->wqk", qb, gather(k_ref, cur), preferred_element_type=F32)
            s_prev = jnp.einsum("wqd,wkd->wqk", qb, gather(k_ref, prev), preferred_element_type=F32)
            s_cur = jnp.where(valid_cur, s_cur * scale + bias_cur, NEG)
            s_prev = jnp.where(jnp.stack(mask_prev), s_prev * scale + bias_prev, NEG)
            m = jnp.max(jnp.maximum(s_cur, s_prev), axis=2, keepdims=True)
            p_cur = jnp.exp(s_cur - m)
            p_prev = jnp.exp(s_prev - m)
            ssum = jnp.sum(p_cur + p_prev, axis=2, keepdims=True)
            o = (jnp.einsum("wqk,wkd->wqd", p_cur.astype(BF16), gather(v_ref, cur), preferred_element_type=F32)
                 + jnp.einsum("wqk,wkd->wqd", p_prev.astype(BF16), gather(v_ref, prev),
                              preferred_element_type=F32)) / ssum
            lse = m + jnp.log(ssum)
            for w, (r, nl) in enumerate(wave):
                dst = pl.ds(r + nl * BAND_BLOCK * dil, BAND_BLOCK, stride=dil) if dil > 1 else pl.ds(
                    nl * BAND_BLOCK, BAND_BLOCK)
                og_ref[g, dst, :] = o[w]
                lg_ref[g, dst, :] = jnp.broadcast_to(lse[w], (BAND_BLOCK, LANES))
    rows = 256
    for rc in range(DIL_CHUNK // rows):
        sl = slice(rc * rows, (rc + 1) * rows)
        lses = [lg_ref[g, sl, :] for g in range(n_g)]
        mx = functools.reduce(jnp.maximum, lses)
        ws = [jnp.exp(l - mx) for l in lses]
        den = functools.reduce(lambda a, b_: a + b_, ws)
        num = functools.reduce(lambda a, b_: a + b_, [ws[g] * og_ref[g, sl, :] for g in range(n_g)])
        o_ref[0, sl, :] = (num / den).astype(o_ref.dtype)


def _dilated_attention(hs, bias_tiles):
    b, s, _ = hs[0].shape
    in_specs = []
    args = []
    for hg in hs:
        for part in range(3):
            in_specs.append(pl.BlockSpec((1, s, DIL_HEAD_DIM),
                                         lambda bi, hi, ci, part=part: (bi, 0, part * DIL_HEADS + hi)))
            args.append(hg)
    in_specs.append(pl.BlockSpec((len(DIL_GROUPS), 1, BAND_BLOCK, 2 * BAND_BLOCK),
                                 lambda bi, hi, ci: (0, hi, 0, 0)))
    return pl.pallas_call(
        functools.partial(_dil_kernel, seq=s),
        grid=(b, DIL_HEADS, s // DIL_CHUNK),
        in_specs=in_specs,
        out_specs=pl.BlockSpec((1, DIL_CHUNK, DIL_HEAD_DIM), lambda bi, hi, ci: (bi, ci, hi)),
        out_shape=jax.ShapeDtypeStruct((b, s, DIL_HEADS * DIL_HEAD_DIM), BF16),
        scratch_shapes=[pltpu.VMEM((len(DIL_GROUPS), DIL_CHUNK, LANES), F32),
                        pltpu.VMEM((len(DIL_GROUPS), DIL_CHUNK, LANES), F32)],
        compiler_params=pltpu.CompilerParams(dimension_semantics=("parallel", "parallel", "arbitrary")),
        name="dilated_attention",
    )(*args, bias_tiles)


def _outproj_kernel(*refs, n_in, with_router):
    ins = refs[:n_in]
    ws = refs[n_in:2 * n_in]
    x_ref, g_ref, b_ref = refs[2 * n_in:2 * n_in + 3]
    rest = refs[2 * n_in + 3:]
    mix = _dot(ins[0][...], ws[0][...])
    for a_ref, w_ref in zip(ins[1:], ws[1:]):
        mix = mix + _dot(a_ref[...], w_ref[...])
    xn = _layer_norm(DEEPNORM_ALPHA * x_ref[...] + mix, g_ref[...], b_ref[...])
    if not with_router:
        o_ref, ob_ref = rest
    else:
        router_ref, o_ref, gates_ref, counts_ref, run_ref, tri_ref = rest
        ob_ref = None
        tm = xn.shape[0]

        @pl.when(pl.program_id(0) == 0)
        def _():
            run_ref[...] = jnp.zeros_like(run_ref)
            tri_ref[...] = (lax.broadcasted_iota(jnp.int32, (tm, tm), 0)
                            > lax.broadcasted_iota(jnp.int32, (tm, tm), 1)).astype(BF16)

        logits = _dot(xn.astype(BF16), router_ref[...])
        lane = lax.broadcasted_iota(jnp.int32, logits.shape, 1).astype(F32)
        lg = jnp.where(lane < N_EXPERTS, logits, -jnp.inf)
        m1 = jnp.max(lg, axis=1, keepdims=True)
        i1 = jnp.min(jnp.where(lg == m1, lane, float(LANES)), axis=1, keepdims=True)
        lg2 = jnp.where(lane == i1, -jnp.inf, lg)
        m2 = jnp.max(lg2, axis=1, keepdims=True)
        i2 = jnp.min(jnp.where(lg2 == m2, lane, float(LANES)), axis=1, keepdims=True)
        e2 = jnp.exp(m2 - m1)
        den = 1.0 + e2
        oh1 = jnp.where(lane == i1, 1.0, 0.0)
        oh2 = jnp.where(lane == i2, 1.0, 0.0)
        both = oh1 + oh2
        before = _dot(tri_ref[...], both.astype(BF16)) + run_ref[...]
        r1 = jnp.sum(before * oh1, axis=1, keepdims=True)
        r2 = jnp.sum(before * oh2, axis=1, keepdims=True)
        run_ref[...] += jnp.sum(both, axis=0, keepdims=True)
        counts_ref[...] = run_ref[...]
        gates_ref[...] = (jnp.where(lane == 0.0, i1, 0.0) + jnp.where(lane == 1.0, i2, 0.0)
                          + jnp.where(lane == 2.0, 1.0 / den, 0.0) + jnp.where(lane == 3.0, e2 / den, 0.0)
                          + jnp.where(lane == 4.0, r1, 0.0) + jnp.where(lane == 5.0, r2, 0.0))
    o_ref[...] = xn
    if ob_ref is not None:
        ob_ref[...] = xn.astype(BF16)


def _outproj_ln(ins, ws, x, ln_g, ln_b, router=None, tm=512):
    t, d = x.shape
    n_in = len(ins)
    row = lambda width: pl.BlockSpec((tm, width), lambda i: (i, 0))
    full = lambda a: pl.BlockSpec(a.shape, lambda i: (0, 0))
    vec = lambda v: v.reshape(1, d).astype(F32)
    args = list(ins) + list(ws) + [x, vec(ln_g), vec(ln_b)]
    in_specs = [row(a.shape[1]) for a in ins] + [full(w) for w in ws] + [row(d), full(vec(ln_g)), full(vec(ln_b))]
    out_shape = [jax.ShapeDtypeStruct((t, d), F32), jax.ShapeDtypeStruct((t, d), BF16)]
    out_specs = [row(d), row(d)]
    if router is not None:
        router_pad = jnp.zeros((d, LANES), BF16).at[:, :N_EXPERTS].set(router.astype(BF16))
        args.append(router_pad)
        in_specs.append(full(router_pad))
        out_shape = [out_shape[0], jax.ShapeDtypeStruct((t, LANES), F32), jax.ShapeDtypeStruct((1, LANES), F32)]
        out_specs = [row(d), row(LANES), pl.BlockSpec((1, LANES), lambda i: (0, 0))]
        scratch = [pltpu.VMEM((1, LANES), F32), pltpu.VMEM((tm, tm), BF16)]
    else:
        scratch = []
    semantics = "arbitrary" if router is not None else "parallel"
    return pl.pallas_call(
        functools.partial(_outproj_kernel, n_in=n_in, with_router=router is not None),
        grid=(t // tm,),
        in_specs=in_specs,
        out_specs=out_specs,
        out_shape=out_shape,
        scratch_shapes=scratch,
        compiler_params=pltpu.CompilerParams(dimension_semantics=(semantics,)),
        name="outproj_ln",
    )(*args)


def _ffn_kernel(xb_ref, x_ref, wg_ref, wu_ref, wd_ref, g_ref, b_ref, o_ref, ob_ref, acc_ref):
    f = pl.program_id(1)

    @pl.when(f == 0)
    def _():
        acc_ref[...] = jnp.zeros_like(acc_ref)

    xb = xb_ref[...]
    hg = _dot(xb, wg_ref[...])
    hu = _dot(xb, wu_ref[...])
    act = (hg * _sigmoid(hg) * hu).astype(BF16)
    acc_ref[...] += _dot(act, wd_ref[...])

    @pl.when(f == pl.num_programs(1) - 1)
    def _():
        xn = _layer_norm(DEEPNORM_ALPHA * x_ref[...] + acc_ref[...], g_ref[...], b_ref[...])
        o_ref[...] = xn
        ob_ref[...] = xn.astype(BF16)


def _ffn_ln(xb, x, wg, wu, wd, ln_g, ln_b, tm=512, tf=1408):
    t, d = x.shape
    ff = wg.shape[1]
    vec = lambda v: v.reshape(1, d).astype(F32)
    return pl.pallas_call(
        _ffn_kernel,
        grid=(t // tm, ff // tf),
        in_specs=[pl.BlockSpec((tm, d), lambda i, f: (i, 0)),
                  pl.BlockSpec((tm, d), lambda i, f: (i, 0)),
                  pl.BlockSpec((d, tf), lambda i, f: (0, f)),
                  pl.BlockSpec((d, tf), lambda i, f: (0, f)),
                  pl.BlockSpec((tf, d), lambda i, f: (f, 0)),
                  pl.BlockSpec((1, d), lambda i, f: (0, 0)),
                  pl.BlockSpec((1, d), lambda i, f: (0, 0))],
        out_specs=[pl.BlockSpec((tm, d), lambda i, f: (i, 0)),
                   pl.BlockSpec((tm, d), lambda i, f: (i, 0))],
        out_shape=[jax.ShapeDtypeStruct((t, d), F32), jax.ShapeDtypeStruct((t, d), BF16)],
        scratch_shapes=[pltpu.VMEM((tm, d), F32)],
        compiler_params=pltpu.CompilerParams(dimension_semantics=("parallel", "arbitrary")),
        name="ffn_ln",
    )(xb, x, wg, wu, wd, vec(ln_g), vec(ln_b))


_MOE_TM = 1024


def _route(route, counts, n_tiles):
    expert = route[:, 0:2].astype(jnp.int32)
    rank = route[:, 4:6].astype(jnp.int32)
    counts = counts[0, :N_EXPERTS].astype(jnp.int32)
    padded = (counts + _MOE_TM - 1) // _MOE_TM * _MOE_TM
    ends = jnp.cumsum(padded)
    starts = ends - padded
    onehot = expert[:, :, None] == jnp.arange(N_EXPERTS, dtype=jnp.int32)
    pos = (rank + jnp.sum(jnp.where(onehot, starts, 0), axis=-1)).reshape(-1)
    tile_start = jnp.arange(n_tiles, dtype=jnp.int32) * _MOE_TM
    tile_expert = jnp.sum((tile_start[:, None] >= ends[None, :]).astype(jnp.int32), axis=1)
    return pos.astype(jnp.int32), tile_expert.astype(jnp.int32)


def _row_scatter_kernel(pos_ref, x_ref, init_ref, o_ref, sem, *, tm):
    del init_ref

    def body(j, carry):
        for choice in range(2):
            p = pos_ref[0, 0, 2 * j + choice]
            pltpu.make_async_copy(x_ref.at[pl.ds(j, 1)], o_ref.at[pl.ds(p, 1)], sem).start()
        return carry

    lax.fori_loop(0, tm, body, 0, unroll=8)
    for _ in range(2):
        pltpu.make_async_copy(x_ref, o_ref.at[pl.ds(0, tm)], sem).wait()


def _row_scatter(x, pos, init, tm=512):
    t, d = x.shape
    n_rows = init.shape[0]
    return pl.pallas_call(
        functools.partial(_row_scatter_kernel, tm=tm),
        grid=(t // tm,),
        in_specs=[pl.BlockSpec((1, 1, 2 * tm), lambda i: (i, 0, 0), memory_space=pltpu.SMEM),
                  pl.BlockSpec((tm, d), lambda i: (i, 0)),
                  pl.BlockSpec(memory_space=pl.ANY)],
        out_specs=pl.BlockSpec(memory_space=pl.ANY),
        out_shape=jax.ShapeDtypeStruct((n_rows, d), x.dtype),
        scratch_shapes=[pltpu.SemaphoreType.DMA(())],
        input_output_aliases={2: 0},
        compiler_params=pltpu.CompilerParams(dimension_semantics=("arbitrary",)),
        name="moe_row_scatter",
    )(pos.reshape(t // tm, 1, 2 * tm), x, init)


def _expert_ffn_kernel(te_ref, x_ref, wg_ref, wu_ref, wd_ref, o_ref, acc_ref):
    f = pl.program_id(1)
    last = pl.num_programs(1) - 1
    used = te_ref[pl.program_id(0)] < N_EXPERTS

    @pl.when(used & (f == 0))
    def _():
        acc_ref[...] = jnp.zeros_like(acc_ref)

    @pl.when(used)
    def _():
        xb = x_ref[...].astype(BF16)
        hg = _dot(xb, wg_ref[0].astype(BF16))
        hu = _dot(xb, wu_ref[0].astype(BF16))
        act = (hg * _sigmoid(hg) * hu).astype(BF16)
        acc_ref[...] += _dot(act, wd_ref[0].astype(BF16))

    @pl.when(used & (f == last))
    def _():
        o_ref[...] = acc_ref[...]

    @pl.when(jnp.logical_not(used) & (f == last))
    def _():
        o_ref[...] = jnp.zeros_like(o_ref)


def _expert_ffn(xs, tile_expert, wg, wu, wd, tf=512):
    n_rows, d = xs.shape
    ff = wg.shape[2]
    nf = ff // tf
    expert = lambda i, te: jnp.minimum(te[i], N_EXPERTS - 1)
    chunk = lambda i, f, te: jnp.where(te[i] < N_EXPERTS, f, nf - 1)
    grid_spec = pltpu.PrefetchScalarGridSpec(
        num_scalar_prefetch=1,
        grid=(n_rows // _MOE_TM, nf),
        in_specs=[pl.BlockSpec((_MOE_TM, d), lambda i, f, te: (i, 0)),
                  pl.BlockSpec((1, d, tf), lambda i, f, te: (expert(i, te), 0, chunk(i, f, te))),
                  pl.BlockSpec((1, d, tf), lambda i, f, te: (expert(i, te), 0, chunk(i, f, te))),
                  pl.BlockSpec((1, tf, d), lambda i, f, te: (expert(i, te), chunk(i, f, te), 0))],
        out_specs=pl.BlockSpec((_MOE_TM, d), lambda i, f, te: (i, 0)),
        scratch_shapes=[pltpu.VMEM((_MOE_TM, d), F32)])
    return pl.pallas_call(
        _expert_ffn_kernel,
        grid_spec=grid_spec,
        out_shape=jax.ShapeDtypeStruct((n_rows, d), F32),
        compiler_params=pltpu.CompilerParams(dimension_semantics=("parallel", "arbitrary")),
        name="moe_expert_ffn",
    )(tile_expert, xs, wg, wu, wd)


def _combine_kernel(pos_ref, pos_next_ref, route_ref, x_ref, g_ref, b_ref, y_ref, o_ref, ybuf_ref, sems, *, tm):
    i = pl.program_id(0)
    n = pl.num_programs(0)

    def gather(src_pos_ref, slot):
        def body(j, carry):
            for choice in range(2):
                p = src_pos_ref[0, 0, 2 * j + choice]
                pltpu.make_async_copy(y_ref.at[pl.ds(p, 1)], ybuf_ref.at[slot, choice, pl.ds(j, 1)],
                                      sems.at[slot]).start()
            return carry

        lax.fori_loop(0, tm, body, 0, unroll=8)

    @pl.when(i == 0)
    def _():
        gather(pos_ref, 0)

    slot = i % 2

    @pl.when(i + 1 < n)
    def _():
        gather(pos_next_ref, 1 - slot)

    for choice in range(2):
        pltpu.make_async_copy(y_ref.at[pl.ds(0, tm)], ybuf_ref.at[slot, choice], sems.at[slot]).wait()
    route = route_ref[...]
    y = route[:, 2:3] * ybuf_ref[slot, 0] + route[:, 3:4] * ybuf_ref[slot, 1]
    o_ref[...] = _layer_norm(DEEPNORM_ALPHA * x_ref[...] + y, g_ref[...], b_ref[...])


def _combine_ln(ys, pos, route, x, ln_g, ln_b, tm=512):
    t, d = x.shape
    vec = lambda v: v.reshape(1, d).astype(F32)
    last = t // tm - 1
    pos3 = pos.reshape(t // tm, 1, 2 * tm)
    return pl.pallas_call(
        functools.partial(_combine_kernel, tm=tm),
        grid=(t // tm,),
        in_specs=[pl.BlockSpec((1, 1, 2 * tm), lambda i: (i, 0, 0), memory_space=pltpu.SMEM),
                  pl.BlockSpec((1, 1, 2 * tm), lambda i: (jnp.minimum(i + 1, last), 0, 0), memory_space=pltpu.SMEM),
                  pl.BlockSpec((tm, LANES), lambda i: (i, 0)),
                  pl.BlockSpec((tm, d), lambda i: (i, 0)),
                  pl.BlockSpec((1, d), lambda i: (0, 0)),
                  pl.BlockSpec((1, d), lambda i: (0, 0)),
                  pl.BlockSpec(memory_space=pl.ANY)],
        out_specs=pl.BlockSpec((tm, d), lambda i: (i, 0)),
        out_shape=jax.ShapeDtypeStruct((t, d), F32),
        scratch_shapes=[pltpu.VMEM((2, 2, tm, d), F32), pltpu.SemaphoreType.DMA((2,))],
        compiler_params=pltpu.CompilerParams(dimension_semantics=("arbitrary",)),
        name="moe_combine_ln",
    )(pos3, pos3, route, x, vec(ln_g), vec(ln_b), ys)


def _moe_ln(x, route, counts, wg, wu, wd, ln_g, ln_b):
    t, _ = x.shape
    n_tiles = 2 * t // _MOE_TM + N_EXPERTS
    pos, tile_expert = _route(route, counts, n_tiles)
    xs = _row_scatter(x, pos, jnp.zeros((n_tiles * _MOE_TM, x.shape[1]), x.dtype))
    ys = _expert_ffn(xs, tile_expert, wg, wu, wd)
    return _combine_ln(ys, pos, route, x, ln_g, ln_b)


def _even_layer(x, xb, batch, seq, dsa_bias, w_in, conv_w, conv_b, conv_ln_g, conv_ln_b, w_out,
                ln1_g, ln1_b, ffn_wg, ffn_wu, ffn_wd, ln2_g, ln2_b):
    w_in_p = jnp.zeros((D_MODEL, EVEN_IN_PAD), BF16).at[:, :w_in.shape[1]].set(w_in.astype(BF16))
    h = _matmul(x if xb is None else xb, w_in_p, 1024, EVEN_IN_PAD, BF16).reshape(batch, seq, EVEN_IN_PAD)
    a_out = _conformer_conv(h, conv_w, conv_b, conv_ln_g, conv_ln_b)
    att = _dsa_attention(h, dsa_bias)
    w_out_b = w_out.astype(BF16)
    x1, x1b = _outproj_ln([a_out.reshape(-1, CONV_CH), att.reshape(-1, DSA_HEADS * DSA_HEAD_DIM)],
                          [w_out_b[:CONV_CH], w_out_b[CONV_CH:]], x, ln1_g, ln1_b)
    return _ffn_ln(x1b, x1, ffn_wg.astype(BF16), ffn_wu.astype(BF16), ffn_wd.astype(BF16), ln2_g, ln2_b)


def _odd_layer(x, xb, batch, seq, dil_bias, w_in, w_out, ln1_g, ln1_b, router, moe_wg, moe_wu, moe_wd,
               ln2_g, ln2_b):
    gw = 3 * DIL_HEADS * DIL_HEAD_DIM
    w_in_b = w_in.astype(BF16)
    hs = [_matmul_residue_major(x if xb is None else xb, w_in_b[:, g * gw:(g + 1) * gw], batch, seq, dil, 2048, 768)
          for g, (_, dil) in enumerate(DIL_GROUPS)]
    o = _dilated_attention(hs, dil_bias).reshape(-1, DIL_HEADS * DIL_HEAD_DIM)
    x1, route, counts = _outproj_ln([o], [w_out.astype(BF16)], x, ln1_g, ln1_b, router=router)
    out = _moe_ln(x1, route, counts, moe_wg, moe_wu, moe_wd, ln2_g, ln2_b)
    return out, out.astype(BF16)


def kernel(x, rel_bias, even_w_in, even_conv_w, even_conv_b, even_conv_ln_g, even_conv_ln_b, even_w_out, even_ln1_g, even_ln1_b, even_ffn_wg, even_ffn_wu, even_ffn_wd, even_ln2_g, even_ln2_b, odd_w_in, odd_w_out, odd_ln1_g, odd_ln1_b, odd_router, odd_moe_wg, odd_moe_wu, odd_moe_wd, odd_ln2_g, odd_ln2_b):
    batch, seq, d = x.shape
    assert d == D_MODEL and seq % DIL_CHUNK == 0
    depth = even_w_in.shape[0] + odd_w_in.shape[0]
    assert depth == DEPTH
    dsa_bias = _dsa_bias_tiles(rel_bias, seq)
    dil_bias = _dil_bias_tiles(rel_bias)
    xf = x.reshape(batch * seq, d)
    xb = None
    for layer in range(depth):
        i = layer // 2
        if layer % 2 == 0:
            xf, xb = _even_layer(xf, xb, batch, seq, dsa_bias, even_w_in[i], even_conv_w[i], even_conv_b[i],
                                 even_conv_ln_g[i], even_conv_ln_b[i], even_w_out[i], even_ln1_g[i],
                                 even_ln1_b[i], even_ffn_wg[i], even_ffn_wu[i], even_ffn_wd[i],
                                 even_ln2_g[i], even_ln2_b[i])
        else:
            xf, xb = _odd_layer(xf, xb, batch, seq, dil_bias, odd_w_in[i], odd_w_out[i], odd_ln1_g[i],
                                odd_ln1_b[i], odd_router[i], odd_moe_wg[i], odd_moe_wu[i], odd_moe_wd[i],
                                odd_ln2_g[i], odd_ln2_b[i])
    return xf.reshape(batch, seq, d)
```

```python
import functools
import math

import jax
import jax.numpy as jnp
from jax import lax
from jax.experimental import pallas as pl
from jax.experimental.pallas import tpu as pltpu

BF16 = jnp.bfloat16
F32 = jnp.float32

D_MODEL = 1024
CONV_CH = 512
CONV_WIDTH = 31
DSA_HEADS = 8
DSA_HEAD_DIM = 64
IDX_HEADS = 8
IDX_DIM = 64
DSA_TOPK_MAX = 256
Q_BLOCK = 128
DIL_GROUPS = ((128, 1), (512, 4), (2048, 16))
DIL_HEADS = 8
DIL_HEAD_DIM = 128
BAND_BLOCK = 128
NUM_BUCKETS = 32
MAX_DISTANCE = 2048
N_EXPERTS = 8
LN_EPS = 1e-5
NEG = -1e30
DEPTH = 2
DEEPNORM_ALPHA = (2 * DEPTH) ** 0.25

LANES = 128
EVEN_IN_PAD = 3200
INT_MIN = -(2 ** 31)
DIL_CHUNK = BAND_BLOCK * max(d for _, d in DIL_GROUPS)

_PARALLEL2 = pltpu.CompilerParams(dimension_semantics=("parallel", "parallel"))


def _layer_norm(y, g, b):
    mu = jnp.mean(y, axis=-1, keepdims=True)
    yc = y - mu
    var = jnp.mean(yc * yc, axis=-1, keepdims=True)
    return yc * lax.rsqrt(var + LN_EPS) * g + b


def _sigmoid(x):
    return 1.0 / (1.0 + jnp.exp(-x))


def _dot_nt(a, b):
    return lax.dot_general(a, b, (((1,), (1,)), ((), ())), preferred_element_type=F32)


def _dot(a, b):
    return jnp.dot(a, b, preferred_element_type=F32)


def _mm_kernel(x_ref, w_ref, o_ref):
    o_ref[...] = _dot(x_ref[...].astype(BF16), w_ref[...]).astype(o_ref.dtype)


def _matmul(x, w, tm, tn, out_dtype):
    t, k = x.shape
    n = w.shape[1]
    return pl.pallas_call(
        _mm_kernel,
        grid=(t // tm, n // tn),
        in_specs=[pl.BlockSpec((tm, k), lambda i, j: (i, 0)),
                  pl.BlockSpec((k, tn), lambda i, j: (0, j))],
        out_specs=pl.BlockSpec((tm, tn), lambda i, j: (i, j)),
        out_shape=jax.ShapeDtypeStruct((t, n), out_dtype),
        compiler_params=_PARALLEL2,
        name="matmul",
    )(x, w)


_MAX_ROW_STRIDE = 4


def _mm_perm_kernel(x_ref, w_ref, o_ref, xp_ref, a_ref, b_ref, *, dil, tm):
    rows = tm // dil

    @pl.when(pl.program_id(1) == 0)
    def _():
        for c in range(x_ref.shape[1] // LANES):
            cols = slice(c * LANES, (c + 1) * LANES)
            a_ref[...] = x_ref[:, cols].astype(F32)
            if dil <= _MAX_ROW_STRIDE:
                for r in range(dil):
                    xp_ref[r * rows:(r + 1) * rows, cols] = a_ref[pl.ds(r, rows, stride=dil), :].astype(BF16)
                continue
            s1 = _MAX_ROW_STRIDE
            s2 = dil // s1
            assert s2 <= _MAX_ROW_STRIDE
            rows1 = tm // s1
            for r1 in range(s1):
                b_ref[r1 * rows1:(r1 + 1) * rows1, :] = a_ref[pl.ds(r1, rows1, stride=s1), :]
            for r1 in range(s1):
                for r2 in range(s2):
                    r = s1 * r2 + r1
                    xp_ref[r * rows:(r + 1) * rows, cols] = (
                        b_ref[pl.ds(r1 * rows1 + r2, rows, stride=s2), :].astype(BF16))

    acc = _dot(xp_ref[...], w_ref[...])
    o_ref[0] = acc.reshape(dil, rows, acc.shape[1]).astype(o_ref.dtype)


def _matmul_residue_major(x, w, batch, seq, dil, tm, tn):
    t, k = x.shape
    n = w.shape[1]
    if dil == 1:
        return _matmul(x, w, tm, tn, BF16).reshape(batch, seq, n)
    tiles_per_seq = seq // tm
    out = pl.pallas_call(
        functools.partial(_mm_perm_kernel, dil=dil, tm=tm),
        grid=(t // tm, n // tn),
        in_specs=[pl.BlockSpec((tm, k), lambda i, j: (i, 0)),
                  pl.BlockSpec((k, tn), lambda i, j: (0, j))],
        out_specs=pl.BlockSpec((1, dil, tm // dil, tn),
                               lambda i, j: (i // tiles_per_seq, 0, i % tiles_per_seq, j)),
        out_shape=jax.ShapeDtypeStruct((batch, dil, seq // dil, n), BF16),
        scratch_shapes=[pltpu.VMEM((tm, k), BF16), pltpu.VMEM((tm, LANES), F32), pltpu.VMEM((tm, LANES), F32)],
        compiler_params=pltpu.CompilerParams(dimension_semantics=("parallel", "arbitrary")),
        name="matmul_residue_major",
    )(x, w)
    return out.reshape(batch, seq, n)


_CONV_HALO = 32
_CONV_ROWS = 128
_SUBLANES = 8


def _conv_kernel(val_ref, gate_ref, w_ref, cb_ref, g_ref, b_ref, o_ref, ext_ref, *, ts):
    s = pl.program_id(1)

    @pl.when(s == 0)
    def _():
        ext_ref[0, 0:_CONV_HALO, :] = jnp.zeros((_CONV_HALO, CONV_CH), F32)

    val = val_ref[0].astype(F32)
    gate = gate_ref[0].astype(F32)
    ext_ref[0, _CONV_HALO:_CONV_HALO + ts, :] = val * _sigmoid(gate)
    span = ts + _CONV_HALO - _SUBLANES
    for k in range(1, _SUBLANES):
        ext_ref[k, 0:span, :] = ext_ref[0, k:k + span, :]
    first_tap = _CONV_HALO - (CONV_WIDTH - 1)
    for rc in range(ts // _CONV_ROWS):
        r0 = rc * _CONV_ROWS
        acc = jnp.zeros((_CONV_ROWS, CONV_CH), F32)
        for j in range(CONV_WIDTH):
            shift = (first_tap + j) % _SUBLANES
            lo = r0 + first_tap + j - shift
            acc = acc + ext_ref[shift, lo:lo + _CONV_ROWS, :] * w_ref[j:j + 1, :]
        y = _layer_norm(acc + cb_ref[...], g_ref[...], b_ref[...])
        o_ref[0, r0:r0 + _CONV_ROWS, :] = (y * _sigmoid(y)).astype(o_ref.dtype)
    ext_ref[0, 0:_CONV_HALO, :] = ext_ref[0, ts:ts + _CONV_HALO, :]


def _conformer_conv(h, conv_w, conv_b, ln_g, ln_b, ts=512):
    b, s, _ = h.shape
    w_pad = jnp.zeros((32, CONV_CH), F32).at[:CONV_WIDTH].set(conv_w)
    vec = lambda v: v.reshape(1, CONV_CH).astype(F32)
    full = lambda shape: pl.BlockSpec(shape, lambda bi, si: (0, 0))
    return pl.pallas_call(
        functools.partial(_conv_kernel, ts=ts),
        grid=(b, s // ts),
        in_specs=[pl.BlockSpec((1, ts, CONV_CH), lambda bi, si: (bi, si, 0)),
                  pl.BlockSpec((1, ts, CONV_CH), lambda bi, si: (bi, si, 1)),
                  full((32, CONV_CH)), full((1, CONV_CH)), full((1, CONV_CH)), full((1, CONV_CH))],
        out_specs=pl.BlockSpec((1, ts, CONV_CH), lambda bi, si: (bi, si, 0)),
        out_shape=jax.ShapeDtypeStruct((b, s, CONV_CH), BF16),
        scratch_shapes=[pltpu.VMEM((_SUBLANES, ts + _CONV_HALO, CONV_CH), F32)],
        compiler_params=pltpu.CompilerParams(dimension_semantics=("parallel", "arbitrary")),
        name="conformer_conv",
    )(h, h, w_pad, vec(conv_b), vec(ln_g), vec(ln_b))


def _rel_bucket(dist):
    max_exact = NUM_BUCKETS // 2
    n = dist.astype(jnp.int32)
    nf = jnp.maximum(n, 1).astype(F32)
    large = max_exact + (jnp.log(nf / max_exact) / math.log(MAX_DISTANCE / max_exact)
                         * (NUM_BUCKETS - max_exact)).astype(jnp.int32)
    large = jnp.minimum(large, NUM_BUCKETS - 1)
    return jnp.where(n < max_exact, n, large)


def _bias_by_distance(rel_bias, n):
    return rel_bias[_rel_bucket(jnp.arange(n, dtype=jnp.int32))].astype(F32).T


def _toeplitz_kernel(win_ref, o_ref):
    n_win, span = win_ref.shape
    for n in range(n_win):
        y = jnp.broadcast_to(win_ref[n:n + 1, :], (LANES, span))
        y = pltpu.roll(y, 0, 1, stride=1, stride_axis=0)
        o_ref[n] = y[:, LANES:]


def _toeplitz_tiles(windows, per_step=8):
    n, span = windows.shape
    width = span - LANES
    return pl.pallas_call(
        _toeplitz_kernel,
        grid=(n // per_step,),
        in_specs=[pl.BlockSpec((per_step, span), lambda i: (i, 0))],
        out_specs=pl.BlockSpec((per_step, LANES, width), lambda i: (i, 0, 0)),
        out_shape=jax.ShapeDtypeStruct((n, LANES, width), F32),
        compiler_params=pltpu.CompilerParams(dimension_semantics=("parallel",)),
        name="bias_toeplitz",
    )(windows)


def _dsa_bias_tiles(rel_bias, seq):
    nblk = seq // Q_BLOCK
    heads = rel_bias.shape[1]
    f = _bias_by_distance(rel_bias, seq)
    u = jnp.concatenate([jnp.broadcast_to(f[:, :1], (heads, 2 * Q_BLOCK)), f], axis=1)
    blocks = u.reshape(heads, nblk + 2, Q_BLOCK)
    windows = jnp.concatenate([blocks[:, :-1], blocks[:, 1:]], axis=-1)
    windows = windows.transpose(1, 0, 2).reshape((nblk + 1) * heads, 2 * Q_BLOCK)
    return _toeplitz_tiles(windows).reshape(nblk + 1, heads, Q_BLOCK, Q_BLOCK)


def _dil_bias_tiles(rel_bias):
    max_dil = max(d for _, d in DIL_GROUPS)
    heads = rel_bias.shape[1]
    span = 2 * BAND_BLOCK
    f = _bias_by_distance(rel_bias, span * max_dil + 1)
    windows = []
    for _, dil in DIL_GROUPS:
        fd = f[:, 0:span * dil + 1:dil]
        tail = jnp.broadcast_to(fd[:, :1], (heads, BAND_BLOCK - 1))
        windows.append(jnp.concatenate([fd[:, ::-1], tail], axis=1))
    tiles = _toeplitz_tiles(jnp.concatenate(windows, axis=0))
    return tiles.reshape(len(DIL_GROUPS), heads, BAND_BLOCK, span)


_KC = 2 * Q_BLOCK
_VT_ROWS = DSA_HEAD_DIM + 16


def _sortable_key(score):
    score = jnp.where(score == 0.0, 0.0, score)
    bits = lax.bitcast_convert_type(score, jnp.int32)
    return jnp.where(bits < 0, bits ^ jnp.int32(0x7FFFFFFF), bits)


def _rows_reduce(x, op):
    rows = x.shape[0]
    return op(op(x.reshape(rows // 8, 8, LANES), axis=0), axis=0, keepdims=True)


def _keys_reduce(x, op):
    h, keys, _ = x.shape
    return op(op(x.reshape(h, keys // 8, 8, LANES), axis=1), axis=1, keepdims=True)


_QG = 2


def _dsa_t_kernel(q_ref, qi_ref, wq_ref, k_ref, vt_ref, ki_ref, bias_hbm, o_ref,
                  bias_ref, keys_ref, qm_ref, acc_ref, dots_a_ref, dots_b_ref, sem, *, topk):
    b = pl.program_id(0)
    i2 = pl.program_id(1)

    @pl.when((b == 0) & (i2 == 0))
    def _():
        cp = pltpu.make_async_copy(bias_hbm, bias_ref, sem)
        cp.start()
        cp.wait()

    nch = i2 + 1
    key_off = lax.broadcasted_iota(jnp.int32, (_KC, LANES), 0)
    q_lane = lax.broadcasted_iota(jnp.int32, (_KC, LANES), 1)
    lane = lax.broadcasted_iota(jnp.int32, (Q_BLOCK, LANES), 1)
    low_half = lane < DSA_HEAD_DIM
    n_heads = _QG * DSA_HEADS

    w_ts, qi_heads = [], []
    for g in range(_QG):
        rows = slice(g * Q_BLOCK, (g + 1) * Q_BLOCK)
        w_ts.append((wq_ref[0, rows, :].astype(F32) * (IDX_HEADS ** -0.5 * IDX_DIM ** -0.5)).T)
        qi = qi_ref[0, rows, :].astype(F32)
        for p in range(IDX_HEADS // 2):
            pair = qi[:, p * LANES:(p + 1) * LANES]
            qi_heads.append(jnp.where(low_half, pair, 0.0))
            qi_heads.append(jnp.where(low_half, pltpu.roll(pair, IDX_DIM, 1), 0.0))
        q_scaled = (q_ref[0, rows, :].astype(F32) * (DSA_HEAD_DIM ** -0.5)).astype(BF16)
        for p in range(DSA_HEADS // 2):
            qp = q_scaled[:, p * LANES:(p + 1) * LANES]
            qm_ref[g * DSA_HEADS + 2 * p] = jnp.where(low_half, qp, jnp.zeros_like(qp))
            qm_ref[g * DSA_HEADS + 2 * p + 1] = jnp.where(low_half, jnp.zeros_like(qp), qp)
    qi_stack = jnp.concatenate(qi_heads, axis=0).astype(BF16)

    def index_dots(c, dst_ref):
        c = jnp.minimum(c, nch - 1)
        k0 = pl.multiple_of(c * _KC, _KC)
        ki = ki_ref[0, pl.ds(k0, _KC), :]
        dst_ref[...] = _dot_nt(ki, qi_stack)

    def consume_dots(c, src_ref):
        k0 = c * _KC
        for g in range(_QG):
            score = jnp.zeros((_KC, LANES), F32)
            for h in range(IDX_HEADS):
                w_row = w_ts[g][IDX_DIM + h:IDX_DIM + h + 1, :]
                col = (g * IDX_HEADS + h) * Q_BLOCK
                score = score + w_row * jnp.maximum(src_ref[:, col:col + Q_BLOCK], 0.0)
            q_pos = (_QG * i2 + g) * Q_BLOCK + q_lane
            keys_ref[g, c] = jnp.where(k0 + key_off <= q_pos, _sortable_key(score), jnp.int32(INT_MIN))

    index_dots(0, dots_a_ref)

    def score_body(c2, carry):
        index_dots(2 * c2 + 1, dots_b_ref)
        consume_dots(2 * c2, dots_a_ref)
        index_dots(2 * c2 + 2, dots_a_ref)
        consume_dots(2 * c2 + 1, dots_b_ref)
        return carry

    lax.fori_loop(0, (nch + 1) // 2, score_body, 0)

    def count_ge(cand):
        cb = jnp.broadcast_to(cand, (_QG, 8, LANES))[:, None, None]
        chains = 4

        def body(c2, acc):
            for half in range(2):
                kk = keys_ref[:, 2 * c2 + half].reshape(_QG, _KC // (8 * chains), chains, 8, LANES)
                acc = acc + jnp.sum(jnp.where(kk >= cb, 1.0, 0.0), axis=1)
            return acc

        acc = lax.fori_loop(0, (nch + 1) // 2, body, jnp.zeros((_QG, chains, 8, LANES), F32))
        return jnp.sum(jnp.sum(acc, axis=1), axis=1, keepdims=True)

    kf = float(topk)
    zero = jnp.zeros((_QG, 1, LANES), jnp.int32)
    thr0 = jnp.where(count_ge(zero) >= kf, zero, jnp.full((_QG, 1, LANES), INT_MIN, jnp.int32))

    def bit_body(t, thr):
        cand = thr | jnp.left_shift(jnp.int32(1), 30 - t)
        return jnp.where(count_ge(cand) >= kf, cand, thr)

    thr = lax.fori_loop(0, 31, bit_body, thr0)
    n_gt = count_ge(jnp.where(thr == jnp.int32(2 ** 31 - 1), thr, thr + 1))
    need = kf - n_gt
    tri = (lax.broadcasted_iota(jnp.int32, (_KC, _KC), 1)
           <= lax.broadcasted_iota(jnp.int32, (_KC, _KC), 0)).astype(BF16)

    acc_ref[...] = jnp.zeros_like(acc_ref)

    def attn_body(c, carry):
        ms, tie_carry = carry
        k0 = pl.multiple_of(c * _KC, _KC)
        kk = keys_ref[:, c]
        causal = kk > jnp.int32(INT_MIN)
        eq = (kk == thr) & causal
        eq_f = jnp.where(eq, 1.0, 0.0)
        eq_b = eq_f.astype(BF16)
        rank = jnp.stack([_dot(tri, eq_b[g]) for g in range(_QG)]) + tie_carry
        sel = ((kk > thr) & causal) | (eq & (rank <= need))
        tie_carry = tie_carry + jnp.stack([_rows_reduce(eq_f[g], jnp.sum) for g in range(_QG)])
        kc = k_ref[0, pl.ds(k0, _KC), :]
        k_heads = jnp.stack([kc[:, (h // 2) * LANES:(h // 2 + 1) * LANES] for h in range(DSA_HEADS)] * _QG)
        s = jnp.einsum("hkd,hqd->hkq", k_heads, qm_ref[...], preferred_element_type=F32)
        biases = []
        for g in range(_QG):
            d = _QG * i2 + g - 2 * c
            biases.append(jnp.concatenate([bias_ref[d + 1], bias_ref[d]], axis=1))
        bias = jnp.concatenate(biases, axis=0)
        sel_h = jnp.broadcast_to(sel[:, None], (_QG, DSA_HEADS, _KC, LANES)).reshape(n_heads, _KC, LANES)
        s = jnp.where(sel_h, s + bias, NEG)
        m_new = jnp.maximum(ms, _keys_reduce(s, jnp.max))
        alpha = jnp.exp(ms - m_new)
        pr = jnp.exp((s - m_new).astype(BF16))
        vt = jnp.concatenate([vt_ref[0, c]] * _QG, axis=0)
        pv = jnp.einsum("hdk,hkq->hdq", vt, pr, preferred_element_type=F32)
        acc_ref[...] = alpha * acc_ref[...] + pv
        return m_new, tie_carry

    init = (jnp.full((n_heads, 1, LANES), NEG, F32), jnp.zeros((_QG, 1, LANES), F32))
    lax.fori_loop(0, nch, attn_body, init)
    acc = acc_ref[...]
    out_t = acc[:, :DSA_HEAD_DIM] / acc[:, DSA_HEAD_DIM:DSA_HEAD_DIM + 1]
    for g in range(_QG):
        block = out_t[g * DSA_HEADS:(g + 1) * DSA_HEADS].reshape(DSA_HEADS * DSA_HEAD_DIM, Q_BLOCK)
        o_ref[0, g * Q_BLOCK:(g + 1) * Q_BLOCK, :] = block.T.astype(o_ref.dtype)


_VT_CHUNKS = 4


def _vt_kernel(v_ref, o_ref):
    for c in range(_VT_CHUNKS):
        vt = v_ref[0, c * _KC:(c + 1) * _KC, :].astype(F32).T
        o_ref[0, c, :, :DSA_HEAD_DIM, :] = vt.reshape(DSA_HEADS, DSA_HEAD_DIM, _KC).astype(o_ref.dtype)
        o_ref[0, c, :, DSA_HEAD_DIM:, :] = jnp.ones((DSA_HEADS, _VT_ROWS - DSA_HEAD_DIM, _KC), o_ref.dtype)


def _values_key_minor(h):
    b, s, _ = h.shape
    width = DSA_HEADS * DSA_HEAD_DIM
    v_block = (2 * CONV_CH + 2 * width) // width
    return pl.pallas_call(
        _vt_kernel,
        grid=(b, s // (_VT_CHUNKS * _KC)),
        in_specs=[pl.BlockSpec((1, _VT_CHUNKS * _KC, width), lambda bi, ci: (bi, ci, v_block))],
        out_specs=pl.BlockSpec((1, _VT_CHUNKS, DSA_HEADS, _VT_ROWS, _KC), lambda bi, ci: (bi, ci, 0, 0, 0)),
        out_shape=jax.ShapeDtypeStruct((b, s // _KC, DSA_HEADS, _VT_ROWS, _KC), BF16),
        compiler_params=_PARALLEL2,
        name="dsa_values_key_minor",
    )(h)


def _dsa_attention(h, bias_tiles):
    b, s, _ = h.shape
    width = DSA_HEADS * DSA_HEAD_DIM
    nblk = s // Q_BLOCK
    topk = min(DSA_TOPK_MAX, s // 4)
    kw_block = (CONV_CH * 2 + 4 * width) // LANES
    vt = _values_key_minor(h)
    return pl.pallas_call(
        functools.partial(_dsa_t_kernel, topk=topk),
        grid=(b, nblk // _QG),
        in_specs=[pl.BlockSpec((1, _QG * Q_BLOCK, width), lambda bi, qi: (bi, qi, 2)),
                  pl.BlockSpec((1, _QG * Q_BLOCK, width), lambda bi, qi: (bi, qi, 5)),
                  pl.BlockSpec((1, _QG * Q_BLOCK, LANES), lambda bi, qi: (bi, qi, kw_block)),
                  pl.BlockSpec((1, s, width), lambda bi, qi: (bi, 0, 3)),
                  pl.BlockSpec((1, s // _KC, DSA_HEADS, _VT_ROWS, _KC), lambda bi, qi: (bi, 0, 0, 0, 0)),
                  pl.BlockSpec((1, s, LANES), lambda bi, qi: (bi, 0, kw_block)),
                  pl.BlockSpec(memory_space=pl.ANY)],
        out_specs=pl.BlockSpec((1, _QG * Q_BLOCK, width), lambda bi, qi: (bi, qi, 0)),
        out_shape=jax.ShapeDtypeStruct((b, s, width), BF16),
        scratch_shapes=[pltpu.VMEM((nblk + 1, DSA_HEADS, Q_BLOCK, Q_BLOCK), F32),
                        pltpu.VMEM((_QG, s // _KC, _KC, Q_BLOCK), jnp.int32),
                        pltpu.VMEM((_QG * DSA_HEADS, Q_BLOCK, LANES), BF16),
                        pltpu.VMEM((_QG * DSA_HEADS, _VT_ROWS, Q_BLOCK), F32),
                        pltpu.VMEM((_KC, _QG * IDX_HEADS * Q_BLOCK), F32),
                        pltpu.VMEM((_KC, _QG * IDX_HEADS * Q_BLOCK), F32),
                        pltpu.SemaphoreType.DMA(())],
        compiler_params=pltpu.CompilerParams(dimension_semantics=("arbitrary", "arbitrary")),
        name="dsa_attention",
    )(h, h, h, h, vt, h, bias_tiles)


_DIL_WAVE = 16


def _dil_kernel(*refs, seq):
    n_g = len(DIL_GROUPS)
    qkv = refs[:3 * n_g]
    bias_ref, o_ref, og_ref, lg_ref = refs[3 * n_g:]
    c = pl.program_id(2)
    ii = lax.broadcasted_iota(jnp.int32, (BAND_BLOCK, BAND_BLOCK), 0)
    jj = lax.broadcasted_iota(jnp.int32, (BAND_BLOCK, BAND_BLOCK), 1)
    valid_prev = jj >= ii
    valid_cur = ii >= jj
    scale = DIL_HEAD_DIM ** -0.5
    for g, (window, dil) in enumerate(DIL_GROUPS):
        assert window // dil == BAND_BLOCK
        q_ref, k_ref, v_ref = qkv[3 * g:3 * g + 3]
        sub_len = seq // dil
        blocks_per_chunk = DIL_CHUNK // (BAND_BLOCK * dil)
        bias_prev = bias_ref[g, 0, :, :BAND_BLOCK]
        bias_cur = bias_ref[g, 0, :, BAND_BLOCK:]
        tiles = [(r, nl) for r in range(dil) for nl in range(blocks_per_chunk)]
        for w0 in range(0, len(tiles), _DIL_WAVE):
            wave = tiles[w0:w0 + _DIL_WAVE]
            cur, prev, mask_prev = [], [], []
            for r, nl in wave:
                row0 = pl.multiple_of(r * sub_len + c * (DIL_CHUNK // dil) + nl * BAND_BLOCK, BAND_BLOCK)
                cur.append(pl.ds(row0, BAND_BLOCK))
                prev.append(pl.ds(pl.multiple_of(jnp.maximum(row0 - BAND_BLOCK, 0), BAND_BLOCK), BAND_BLOCK))
                mask_prev.append(valid_prev if nl > 0 else (valid_prev & (c > 0)))
            gather = lambda ref, slices: jnp.stack([ref[0, sl, :] for sl in slices])
            qb = gather(q_ref, cur)
            s_cur = jnp.einsum("wqd,wkd->wqk", qb, gather(k_ref, cur), preferred_element_type=F32)
            s_prev = jnp.einsum("wqd,wkd->wqk", qb, gather(k_ref, prev), preferred_element_type=F32)
            s_cur = jnp.where(valid_cur, s_cur * scale + bias_cur, NEG)
            s_prev = jnp.where(jnp.stack(mask_prev), s_prev * scale + bias_prev, NEG)
            m = jnp.max(jnp.maximum(s_cur, s_prev), axis=2, keepdims=True)
            p_cur = jnp.exp(s_cur - m)
            p_prev = jnp.exp(s_prev - m)
            ssum = jnp.sum(p_cur + p_prev, axis=2, keepdims=True)
            o = (jnp.einsum("wqk,wkd->wqd", p_cur.astype(BF16), gather(v_ref, cur), preferred_element_type=F32)
                 + jnp.einsum("wqk,wkd->wqd", p_prev.astype(BF16), gather(v_ref, prev),
                              preferred_element_type=F32)) / ssum
            lse = m + jnp.log(ssum)
            for w, (r, nl) in enumerate(wave):
                dst = pl.ds(r + nl * BAND_BLOCK * dil, BAND_BLOCK, stride=dil) if dil > 1 else pl.ds(
                    nl * BAND_BLOCK, BAND_BLOCK)
                og_ref[g, dst, :] = o[w]
                lg_ref[g, dst, :] = jnp.broadcast_to(lse[w], (BAND_BLOCK, LANES))
    rows = 256
    for rc in range(DIL_CHUNK // rows):
        sl = slice(rc * rows, (rc + 1) * rows)
        lses = [lg_ref[g, sl, :] for g in range(n_g)]
        mx = functools.reduce(jnp.maximum, lses)
        ws = [jnp.exp(l - mx) for l in lses]
        den = functools.reduce(lambda a, b_: a + b_, ws)
        num = functools.reduce(lambda a, b_: a + b_, [ws[g] * og_ref[g, sl, :] for g in range(n_g)])
        o_ref[0, sl, :] = (num / den).astype(o_ref.dtype)


def _dilated_attention(hs, bias_tiles):
    b, s, _ = hs[0].shape
    in_specs = []
    args = []
    for hg in hs:
        for part in range(3):
            in_specs.append(pl.BlockSpec((1, s, DIL_HEAD_DIM),
                                         lambda bi, hi, ci, part=part: (bi, 0, part * DIL_HEADS + hi)))
            args.append(hg)
    in_specs.append(pl.BlockSpec((len(DIL_GROUPS), 1, BAND_BLOCK, 2 * BAND_BLOCK),
                                 lambda bi, hi, ci: (0, hi, 0, 0)))
    return pl.pallas_call(
        functools.partial(_dil_kernel, seq=s),
        grid=(b, DIL_HEADS, s // DIL_CHUNK),
        in_specs=in_specs,
        out_specs=pl.BlockSpec((1, DIL_CHUNK, DIL_HEAD_DIM), lambda bi, hi, ci: (bi, ci, hi)),
        out_shape=jax.ShapeDtypeStruct((b, s, DIL_HEADS * DIL_HEAD_DIM), BF16),
        scratch_shapes=[pltpu.VMEM((len(DIL_GROUPS), DIL_CHUNK, LANES), F32),
                        pltpu.VMEM((len(DIL_GROUPS), DIL_CHUNK, LANES), F32)],
        compiler_params=pltpu.CompilerParams(dimension_semantics=("parallel", "parallel", "arbitrary")),
        name="dilated_attention",
    )(*args, bias_tiles)


def _outproj_kernel(*refs, n_in, with_router):
    ins = refs[:n_in]
    ws = refs[n_in:2 * n_in]
    x_ref, g_ref, b_ref = refs[2 * n_in:2 * n_in + 3]
    rest = refs[2 * n_in + 3:]
    mix = _dot(ins[0][...], ws[0][...])
    for a_ref, w_ref in zip(ins[1:], ws[1:]):
        mix = mix + _dot(a_ref[...], w_ref[...])
    xn = _layer_norm(DEEPNORM_ALPHA * x_ref[...] + mix, g_ref[...], b_ref[...])
    if not with_router:
        o_ref, ob_ref = rest
    else:
        router_ref, o_ref, gates_ref, counts_ref, run_ref, tri_ref = rest
        ob_ref = None
        tm = xn.shape[0]

        @pl.when(pl.program_id(0) == 0)
        def _():
            run_ref[...] = jnp.zeros_like(run_ref)
            tri_ref[...] = (lax.broadcasted_iota(jnp.int32, (tm, tm), 0)
                            > lax.broadcasted_iota(jnp.int32, (tm, tm), 1)).astype(BF16)

        logits = _dot(xn.astype(BF16), router_ref[...])
        lane = lax.broadcasted_iota(jnp.int32, logits.shape, 1).astype(F32)
        lg = jnp.where(lane < N_EXPERTS, logits, -jnp.inf)
        m1 = jnp.max(lg, axis=1, keepdims=True)
        i1 = jnp.min(jnp.where(lg == m1, lane, float(LANES)), axis=1, keepdims=True)
        lg2 = jnp.where(lane == i1, -jnp.inf, lg)
        m2 = jnp.max(lg2, axis=1, keepdims=True)
        i2 = jnp.min(jnp.where(lg2 == m2, lane, float(LANES)), axis=1, keepdims=True)
        e2 = jnp.exp(m2 - m1)
        den = 1.0 + e2
        oh1 = jnp.where(lane == i1, 1.0, 0.0)
        oh2 = jnp.where(lane == i2, 1.0, 0.0)
        both = oh1 + oh2
        before = _dot(tri_ref[...], both.astype(BF16)) + run_ref[...]
        r1 = jnp.sum(before * oh1, axis=1, keepdims=True)
        r2 = jnp.sum(before * oh2, axis=1, keepdims=True)
        run_ref[...] += jnp.sum(both, axis=0, keepdims=True)
        counts_ref[...] = run_ref[...]
        gates_ref[...] = (jnp.where(lane == 0.0, i1, 0.0) + jnp.where(lane == 1.0, i2, 0.0)
                          + jnp.where(lane == 2.0, 1.0 / den, 0.0) + jnp.where(lane == 3.0, e2 / den, 0.0)
                          + jnp.where(lane == 4.0, r1, 0.0) + jnp.where(lane == 5.0, r2, 0.0))
    o_ref[...] = xn
    if ob_ref is not None:
        ob_ref[...] = xn.astype(BF16)


def _outproj_ln(ins, ws, x, ln_g, ln_b, router=None, tm=512):
    t, d = x.shape
    n_in = len(ins)
    row = lambda width: pl.BlockSpec((tm, width), lambda i: (i, 0))
    full = lambda a: pl.BlockSpec(a.shape, lambda i: (0, 0))
    vec = lambda v: v.reshape(1, d).astype(F32)
    args = list(ins) + list(ws) + [x, vec(ln_g), vec(ln_b)]
    in_specs = [row(a.shape[1]) for a in ins] + [full(w) for w in ws] + [row(d), full(vec(ln_g)), full(vec(ln_b))]
    out_shape = [jax.ShapeDtypeStruct((t, d), F32), jax.ShapeDtypeStruct((t, d), BF16)]
    out_specs = [row(d), row(d)]
    if router is not None:
        router_pad = jnp.zeros((d, LANES), BF16).at[:, :N_EXPERTS].set(router.astype(BF16))
        args.append(router_pad)
        in_specs.append(full(router_pad))
        out_shape = [out_shape[0], jax.ShapeDtypeStruct((t, LANES), F32), jax.ShapeDtypeStruct((1, LANES), F32)]
        out_specs = [row(d), row(LANES), pl.BlockSpec((1, LANES), lambda i: (0, 0))]
        scratch = [pltpu.VMEM((1, LANES), F32), pltpu.VMEM((tm, tm), BF16)]
    else:
        scratch = []
    semantics = "arbitrary" if router is not None else "parallel"
    return pl.pallas_call(
        functools.partial(_outproj_kernel, n_in=n_in, with_router=router is not None),
        grid=(t // tm,),
        in_specs=in_specs,
        out_specs=out_specs,
        out_shape=out_shape,
        scratch_shapes=scratch,
        compiler_params=pltpu.CompilerParams(dimension_semantics=(semantics,)),
        name="outproj_ln",
    )(*args)


def _ffn_kernel(xb_ref, x_ref, wg_ref, wu_ref, wd_ref, g_ref, b_ref, o_ref, ob_ref, acc_ref):
    f = pl.program_id(1)

    @pl.when(f == 0)
    def _():
        acc_ref[...] = jnp.zeros_like(acc_ref)

    xb = xb_ref[...]
    hg = _dot(xb, wg_ref[...])
    hu = _dot(xb, wu_ref[...])
    act = (hg * _sigmoid(hg) * hu).astype(BF16)
    acc_ref[...] += _dot(act, wd_ref[...])

    @pl.when(f == pl.num_programs(1) - 1)
    def _():
        xn = _layer_norm(DEEPNORM_ALPHA * x_ref[...] + acc_ref[...], g_ref[...], b_ref[...])
        o_ref[...] = xn
        ob_ref[...] = xn.astype(BF16)


def _ffn_ln(xb, x, wg, wu, wd, ln_g, ln_b, tm=512, tf=1408):
    t, d = x.shape
    ff = wg.shape[1]
    vec = lambda v: v.reshape(1, d).astype(F32)
    return pl.pallas_call(
        _ffn_kernel,
        grid=(t // tm, ff // tf),
        in_specs=[pl.BlockSpec((tm, d), lambda i, f: (i, 0)),
                  pl.BlockSpec((tm, d), lambda i, f: (i, 0)),
                  pl.BlockSpec((d, tf), lambda i, f: (0, f)),
                  pl.BlockSpec((d, tf), lambda i, f: (0, f)),
                  pl.BlockSpec((tf, d), lambda i, f: (f, 0)),
                  pl.BlockSpec((1, d), lambda i, f: (0, 0)),
                  pl.BlockSpec((1, d), lambda i, f: (0, 0))],
        out_specs=[pl.BlockSpec((tm, d), lambda i, f: (i, 0)),
                   pl.BlockSpec((tm, d), lambda i, f: (i, 0))],
        out_shape=[jax.ShapeDtypeStruct((t, d), F32), jax.ShapeDtypeStruct((t, d), BF16)],
        scratch_shapes=[pltpu.VMEM((tm, d), F32)],
        compiler_params=pltpu.CompilerParams(dimension_semantics=("parallel", "arbitrary")),
        name="ffn_ln",
    )(xb, x, wg, wu, wd, vec(ln_g), vec(ln_b))


_MOE_TM = 1024


def _route(route, counts, n_tiles):
    expert = route[:, 0:2].astype(jnp.int32)
    rank = route[:, 4:6].astype(jnp.int32)
    counts = counts[0, :N_EXPERTS].astype(jnp.int32)
    padded = (counts + _MOE_TM - 1) // _MOE_TM * _MOE_TM
    ends = jnp.cumsum(padded)
    starts = ends - padded
    onehot = expert[:, :, None] == jnp.arange(N_EXPERTS, dtype=jnp.int32)
    pos = (rank + jnp.sum(jnp.where(onehot, starts, 0), axis=-1)).reshape(-1)
    tile_start = jnp.arange(n_tiles, dtype=jnp.int32) * _MOE_TM
    tile_expert = jnp.sum((tile_start[:, None] >= ends[None, :]).astype(jnp.int32), axis=1)
    valid_end = jnp.concatenate([starts + counts, jnp.zeros((1,), jnp.int32)])[tile_expert]
    tile_rows = jnp.clip(valid_end - tile_start, 0, _MOE_TM)
    return pos.astype(jnp.int32), tile_expert.astype(jnp.int32), tile_rows.astype(jnp.int32)


def _row_scatter_kernel(pos_ref, x_ref, init_ref, o_ref, sem, *, tm):
    del init_ref

    def body(j, carry):
        for choice in range(2):
            p = pos_ref[0, 0, 2 * j + choice]
            pltpu.make_async_copy(x_ref.at[pl.ds(j, 1)], o_ref.at[pl.ds(p, 1)], sem).start()
        return carry

    lax.fori_loop(0, tm, body, 0, unroll=8)
    for _ in range(2):
        pltpu.make_async_copy(x_ref, o_ref.at[pl.ds(0, tm)], sem).wait()


def _row_scatter(x, pos, init, tm=512):
    t, d = x.shape
    n_rows = init.shape[0]
    return pl.pallas_call(
        functools.partial(_row_scatter_kernel, tm=tm),
        grid=(t // tm,),
        in_specs=[pl.BlockSpec((1, 1, 2 * tm), lambda i: (i, 0, 0), memory_space=pltpu.SMEM),
                  pl.BlockSpec((tm, d), lambda i: (i, 0)),
                  pl.BlockSpec(memory_space=pl.ANY)],
        out_specs=pl.BlockSpec(memory_space=pl.ANY),
        out_shape=jax.ShapeDtypeStruct((n_rows, d), x.dtype),
        scratch_shapes=[pltpu.SemaphoreType.DMA(())],
        input_output_aliases={2: 0},
        compiler_params=pltpu.CompilerParams(dimension_semantics=("arbitrary",)),
        name="moe_row_scatter",
    )(pos.reshape(t // tm, 1, 2 * tm), x, init)


def _expert_ffn_kernel(te_ref, rows_ref, x_ref, wg_ref, wu_ref, wd_ref, o_ref, acc_ref):
    del te_ref
    f = pl.program_id(1)
    last = pl.num_programs(1) - 1
    rows = rows_ref[pl.program_id(0)]
    used = rows > 0
    half = _MOE_TM // 2

    @pl.when(used & (f == 0))
    def _():
        acc_ref[...] = jnp.zeros_like(acc_ref)

    def swiglu(m):
        xb = x_ref[0:m, :].astype(BF16)
        hg = _dot(xb, wg_ref[0].astype(BF16))
        hu = _dot(xb, wu_ref[0].astype(BF16))
        act = (hg * _sigmoid(hg) * hu).astype(BF16)
        acc_ref[0:m, :] += _dot(act, wd_ref[0].astype(BF16))

    pl.when(rows > half)(lambda: swiglu(_MOE_TM))
    pl.when(used & (rows <= half))(lambda: swiglu(half))

    @pl.when(used & (f == last))
    def _():
        o_ref[...] = acc_ref[...]

    @pl.when(jnp.logical_not(used) & (f == last))
    def _():
        o_ref[...] = jnp.zeros_like(o_ref)


def _expert_ffn(xs, tile_expert, tile_rows, wg, wu, wd, tf=512):
    n_rows, d = xs.shape
    ff = wg.shape[2]
    nf = ff // tf
    expert = lambda i, te: jnp.minimum(te[i], N_EXPERTS - 1)
    chunk = lambda i, f, te: jnp.where(te[i] < N_EXPERTS, f, nf - 1)
    grid_spec = pltpu.PrefetchScalarGridSpec(
        num_scalar_prefetch=2,
        grid=(n_rows // _MOE_TM, nf),
        in_specs=[pl.BlockSpec((_MOE_TM, d), lambda i, f, te, tr: (i, 0)),
                  pl.BlockSpec((1, d, tf), lambda i, f, te, tr: (expert(i, te), 0, chunk(i, f, te))),
                  pl.BlockSpec((1, d, tf), lambda i, f, te, tr: (expert(i, te), 0, chunk(i, f, te))),
                  pl.BlockSpec((1, tf, d), lambda i, f, te, tr: (expert(i, te), chunk(i, f, te), 0))],
        out_specs=pl.BlockSpec((_MOE_TM, d), lambda i, f, te, tr: (i, 0)),
        scratch_shapes=[pltpu.VMEM((_MOE_TM, d), F32)])
    return pl.pallas_call(
        _expert_ffn_kernel,
        grid_spec=grid_spec,
        out_shape=jax.ShapeDtypeStruct((n_rows, d), F32),
        compiler_params=pltpu.CompilerParams(dimension_semantics=("parallel", "arbitrary")),
        name="moe_expert_ffn",
    )(tile_expert, tile_rows, xs, wg, wu, wd)


def _combine_kernel(pos_ref, pos_next_ref, route_ref, x_ref, g_ref, b_ref, y_ref, o_ref, ybuf_ref, sems, *, tm):
    i = pl.program_id(0)
    n = pl.num_programs(0)

    def gather(src_pos_ref, slot):
        def body(j, carry):
            for choice in range(2):
                p = src_pos_ref[0, 0, 2 * j + choice]
                pltpu.make_async_copy(y_ref.at[pl.ds(p, 1)], ybuf_ref.at[slot, choice, pl.ds(j, 1)],
                                      sems.at[slot]).start()
            return carry

        lax.fori_loop(0, tm, body, 0, unroll=8)

    @pl.when(i == 0)
    def _():
        gather(pos_ref, 0)

    slot = i % 2

    @pl.when(i + 1 < n)
    def _():
        gather(pos_next_ref, 1 - slot)

    for choice in range(2):
        pltpu.make_async_copy(y_ref.at[pl.ds(0, tm)], ybuf_ref.at[slot, choice], sems.at[slot]).wait()
    route = route_ref[...]
    y = route[:, 2:3] * ybuf_ref[slot, 0] + route[:, 3:4] * ybuf_ref[slot, 1]
    o_ref[...] = _layer_norm(DEEPNORM_ALPHA * x_ref[...] + y, g_ref[...], b_ref[...])


def _combine_ln(ys, pos, route, x, ln_g, ln_b, tm=512):
    t, d = x.shape
    vec = lambda v: v.reshape(1, d).astype(F32)
    last = t // tm - 1
    pos3 = pos.reshape(t // tm, 1, 2 * tm)
    return pl.pallas_call(
        functools.partial(_combine_kernel, tm=tm),
        grid=(t // tm,),
        in_specs=[pl.BlockSpec((1, 1, 2 * tm), lambda i: (i, 0, 0), memory_space=pltpu.SMEM),
                  pl.BlockSpec((1, 1, 2 * tm), lambda i: (jnp.minimum(i + 1, last), 0, 0), memory_space=pltpu.SMEM),
                  pl.BlockSpec((tm, LANES), lambda i: (i, 0)),
                  pl.BlockSpec((tm, d), lambda i: (i, 0)),
                  pl.BlockSpec((1, d), lambda i: (0, 0)),
                  pl.BlockSpec((1, d), lambda i: (0, 0)),
                  pl.BlockSpec(memory_space=pl.ANY)],
        out_specs=pl.BlockSpec((tm, d), lambda i: (i, 0)),
        out_shape=jax.ShapeDtypeStruct((t, d), F32),
        scratch_shapes=[pltpu.VMEM((2, 2, tm, d), F32), pltpu.SemaphoreType.DMA((2,))],
        compiler_params=pltpu.CompilerParams(dimension_semantics=("arbitrary",)),
        name="moe_combine_ln",
    )(pos3, pos3, route, x, vec(ln_g), vec(ln_b), ys)


def _moe_ln(x, route, counts, wg, wu, wd, ln_g, ln_b):
    t, _ = x.shape
    n_tiles = 2 * t // _MOE_TM + N_EXPERTS
    pos, tile_expert, tile_rows = _route(route, counts, n_tiles)
    xs = _row_scatter(x, pos, jnp.zeros((n_tiles * _MOE_TM, x.shape[1]), x.dtype))
    ys = _expert_ffn(xs, tile_expert, tile_rows, wg, wu, wd)
    return _combine_ln(ys, pos, route, x, ln_g, ln_b)


def _even_layer(x, xb, batch, seq, dsa_bias, w_in, conv_w, conv_b, conv_ln_g, conv_ln_b, w_out,
                ln1_g, ln1_b, ffn_wg, ffn_wu, ffn_wd, ln2_g, ln2_b):
    w_in_p = jnp.zeros((D_MODEL, EVEN_IN_PAD), BF16).at[:, :w_in.shape[1]].set(w_in.astype(BF16))
    h = _matmul(x if xb is None else xb, w_in_p, 1024, EVEN_IN_PAD, BF16).reshape(batch, seq, EVEN_IN_PAD)
    a_out = _conformer_conv(h, conv_w, conv_b, conv_ln_g, conv_ln_b)
    att = _dsa_attention(h, dsa_bias)
    w_out_b = w_out.astype(BF16)
    x1, x1b = _outproj_ln([a_out.reshape(-1, CONV_CH), att.reshape(-1, DSA_HEADS * DSA_HEAD_DIM)],
                          [w_out_b[:CONV_CH], w_out_b[CONV_CH:]], x, ln1_g, ln1_b)
    return _ffn_ln(x1b, x1, ffn_wg.astype(BF16), ffn_wu.astype(BF16), ffn_wd.astype(BF16), ln2_g, ln2_b)


def _odd_layer(x, xb, batch, seq, dil_bias, w_in, w_out, ln1_g, ln1_b, router, moe_wg, moe_wu, moe_wd,
               ln2_g, ln2_b):
    gw = 3 * DIL_HEADS * DIL_HEAD_DIM
    w_in_b = w_in.astype(BF16)
    hs = [_matmul_residue_major(x if xb is None else xb, w_in_b[:, g * gw:(g + 1) * gw], batch, seq, dil, 2048, 768)
          for g, (_, dil) in enumerate(DIL_GROUPS)]
    o = _dilated_attention(hs, dil_bias).reshape(-1, DIL_HEADS * DIL_HEAD_DIM)
    x1, route, counts = _outproj_ln([o], [w_out.astype(BF16)], x, ln1_g, ln1_b, router=router)
    out = _moe_ln(x1, route, counts, moe_wg, moe_wu, moe_wd, ln2_g, ln2_b)
    return out, out.astype(BF16)


def kernel(x, rel_bias, even_w_in, even_conv_w, even_conv_b, even_conv_ln_g, even_conv_ln_b, even_w_out, even_ln1_g, even_ln1_b, even_ffn_wg, even_ffn_wu, even_ffn_wd, even_ln2_g, even_ln2_b, odd_w_in, odd_w_out, odd_ln1_g, odd_ln1_b, odd_router, odd_moe_wg, odd_moe_wu, odd_moe_wd, odd_ln2_g, odd_ln2_b):
    batch, seq, d = x.shape
    assert d == D_MODEL and seq % DIL_CHUNK == 0
    depth = even_w_in.shape[0] + odd_w_in.shape[0]
    assert depth == DEPTH
    dsa_bias = _dsa_bias_tiles(rel_bias, seq)
    dil_bias = _dil_bias_tiles(rel_bias)
    xf = x.reshape(batch * seq, d)
    xb = None
    for layer in range(depth):
        i = layer // 2
        if layer % 2 == 0:
            xf, xb = _even_layer(xf, xb, batch, seq, dsa_bias, even_w_in[i], even_conv_w[i], even_conv_b[i],
                                 even_conv_ln_g[i], even_conv_ln_b[i], even_w_out[i], even_ln1_g[i],
                                 even_ln1_b[i], even_ffn_wg[i], even_ffn_wu[i], even_ffn_wd[i],
                                 even_ln2_g[i], even_ln2_b[i])
        else:
            xf, xb = _odd_layer(xf, xb, batch, seq, dil_bias, odd_w_in[i], odd_w_out[i], odd_ln1_g[i],
                                odd_ln1_b[i], odd_router[i], odd_moe_wg[i], odd_moe_wu[i], odd_moe_wd[i],
                                odd_ln2_g[i], odd_ln2_b[i])
    return xf.reshape(batch, seq, d)
```

```python
import functools
import math

import jax
import jax.numpy as jnp
from jax import lax
from jax.experimental import pallas as pl
from jax.experimental.pallas import tpu as pltpu

BF16 = jnp.bfloat16
F32 = jnp.float32

D_MODEL = 1024
CONV_CH = 512
CONV_WIDTH = 31
DSA_HEADS = 8
DSA_HEAD_DIM = 64
IDX_HEADS = 8
IDX_DIM = 64
DSA_TOPK_MAX = 256
Q_BLOCK = 128
DIL_GROUPS = ((128, 1), (512, 4), (2048, 16))
DIL_HEADS = 8
DIL_HEAD_DIM = 128
BAND_BLOCK = 128
NUM_BUCKETS = 32
MAX_DISTANCE = 2048
N_EXPERTS = 8
LN_EPS = 1e-5
NEG = -1e30
DEPTH = 2
DEEPNORM_ALPHA = (2 * DEPTH) ** 0.25

LANES = 128
EVEN_IN_PAD = 3200
INT_MIN = -(2 ** 31)
DIL_CHUNK = BAND_BLOCK * max(d for _, d in DIL_GROUPS)

_PARALLEL2 = pltpu.CompilerParams(dimension_semantics=("parallel", "parallel"))


def _layer_norm(y, g, b):
    mu = jnp.mean(y, axis=-1, keepdims=True)
    yc = y - mu
    var = jnp.mean(yc * yc, axis=-1, keepdims=True)
    return yc * lax.rsqrt(var + LN_EPS) * g + b


def _sigmoid(x):
    return 1.0 / (1.0 + jnp.exp(-x))


def _dot_nt(a, b):
    return lax.dot_general(a, b, (((1,), (1,)), ((), ())), preferred_element_type=F32)


def _dot(a, b):
    return jnp.dot(a, b, preferred_element_type=F32)


def _mm_kernel(x_ref, w_ref, o_ref):
    o_ref[...] = _dot(x_ref[...].astype(BF16), w_ref[...]).astype(o_ref.dtype)


def _matmul(x, w, tm, tn, out_dtype):
    t, k = x.shape
    n = w.shape[1]
    return pl.pallas_call(
        _mm_kernel,
        grid=(t // tm, n // tn),
        in_specs=[pl.BlockSpec((tm, k), lambda i, j: (i, 0)),
                  pl.BlockSpec((k, tn), lambda i, j: (0, j))],
        out_specs=pl.BlockSpec((tm, tn), lambda i, j: (i, j)),
        out_shape=jax.ShapeDtypeStruct((t, n), out_dtype),
        compiler_params=_PARALLEL2,
        name="matmul",
    )(x, w)


_MAX_ROW_STRIDE = 4


def _mm_perm_kernel(x_ref, w_ref, o_ref, xp_ref, a_ref, b_ref, *, dil, tm):
    rows = tm // dil

    @pl.when(pl.program_id(1) == 0)
    def _():
        for c in range(x_ref.shape[1] // LANES):
            cols = slice(c * LANES, (c + 1) * LANES)
            a_ref[...] = x_ref[:, cols].astype(F32)
            if dil <= _MAX_ROW_STRIDE:
                for r in range(dil):
                    xp_ref[r * rows:(r + 1) * rows, cols] = a_ref[pl.ds(r, rows, stride=dil), :].astype(BF16)
                continue
            s1 = _MAX_ROW_STRIDE
            s2 = dil // s1
            assert s2 <= _MAX_ROW_STRIDE
            rows1 = tm // s1
            for r1 in range(s1):
                b_ref[r1 * rows1:(r1 + 1) * rows1, :] = a_ref[pl.ds(r1, rows1, stride=s1), :]
            for r1 in range(s1):
                for r2 in range(s2):
                    r = s1 * r2 + r1
                    xp_ref[r * rows:(r + 1) * rows, cols] = (
                        b_ref[pl.ds(r1 * rows1 + r2, rows, stride=s2), :].astype(BF16))

    acc = _dot(xp_ref[...], w_ref[...])
    o_ref[0] = acc.reshape(dil, rows, acc.shape[1]).astype(o_ref.dtype)


def _matmul_residue_major(x, w, batch, seq, dil, tm, tn):
    t, k = x.shape
    n = w.shape[1]
    if dil == 1:
        return _matmul(x, w, tm, tn, BF16).reshape(batch, seq, n)
    tiles_per_seq = seq // tm
    out = pl.pallas_call(
        functools.partial(_mm_perm_kernel, dil=dil, tm=tm),
        grid=(t // tm, n // tn),
        in_specs=[pl.BlockSpec((tm, k), lambda i, j: (i, 0)),
                  pl.BlockSpec((k, tn), lambda i, j: (0, j))],
        out_specs=pl.BlockSpec((1, dil, tm // dil, tn),
                               lambda i, j: (i // tiles_per_seq, 0, i % tiles_per_seq, j)),
        out_shape=jax.ShapeDtypeStruct((batch, dil, seq // dil, n), BF16),
        scratch_shapes=[pltpu.VMEM((tm, k), BF16), pltpu.VMEM((tm, LANES), F32), pltpu.VMEM((tm, LANES), F32)],
        compiler_params=pltpu.CompilerParams(dimension_semantics=("parallel", "arbitrary")),
        name="matmul_residue_major",
    )(x, w)
    return out.reshape(batch, seq, n)


_CONV_HALO = 32
_CONV_ROWS = 128
_SUBLANES = 8


def _conv_kernel(val_ref, gate_ref, w_ref, cb_ref, g_ref, b_ref, o_ref, ext_ref, *, ts):
    s = pl.program_id(1)

    @pl.when(s == 0)
    def _():
        ext_ref[0, 0:_CONV_HALO, :] = jnp.zeros((_CONV_HALO, CONV_CH), F32)

    val = val_ref[0].astype(F32)
    gate = gate_ref[0].astype(F32)
    ext_ref[0, _CONV_HALO:_CONV_HALO + ts, :] = val * _sigmoid(gate)
    span = ts + _CONV_HALO - _SUBLANES
    for k in range(1, _SUBLANES):
        ext_ref[k, 0:span, :] = ext_ref[0, k:k + span, :]
    first_tap = _CONV_HALO - (CONV_WIDTH - 1)
    for rc in range(ts // _CONV_ROWS):
        r0 = rc * _CONV_ROWS
        acc = jnp.zeros((_CONV_ROWS, CONV_CH), F32)
        for j in range(CONV_WIDTH):
            shift = (first_tap + j) % _SUBLANES
            lo = r0 + first_tap + j - shift
            acc = acc + ext_ref[shift, lo:lo + _CONV_ROWS, :] * w_ref[j:j + 1, :]
        y = _layer_norm(acc + cb_ref[...], g_ref[...], b_ref[...])
        o_ref[0, r0:r0 + _CONV_ROWS, :] = (y * _sigmoid(y)).astype(o_ref.dtype)
    ext_ref[0, 0:_CONV_HALO, :] = ext_ref[0, ts:ts + _CONV_HALO, :]


def _conformer_conv(h, conv_w, conv_b, ln_g, ln_b, ts=1024):
    b, s, _ = h.shape
    w_pad = jnp.zeros((32, CONV_CH), F32).at[:CONV_WIDTH].set(conv_w)
    vec = lambda v: v.reshape(1, CONV_CH).astype(F32)
    full = lambda shape: pl.BlockSpec(shape, lambda bi, si: (0, 0))
    return pl.pallas_call(
        functools.partial(_conv_kernel, ts=ts),
        grid=(b, s // ts),
        in_specs=[pl.BlockSpec((1, ts, CONV_CH), lambda bi, si: (bi, si, 0)),
                  pl.BlockSpec((1, ts, CONV_CH), lambda bi, si: (bi, si, 1)),
                  full((32, CONV_CH)), full((1, CONV_CH)), full((1, CONV_CH)), full((1, CONV_CH))],
        out_specs=pl.BlockSpec((1, ts, CONV_CH), lambda bi, si: (bi, si, 0)),
        out_shape=jax.ShapeDtypeStruct((b, s, CONV_CH), BF16),
        scratch_shapes=[pltpu.VMEM((_SUBLANES, ts + _CONV_HALO, CONV_CH), F32)],
        compiler_params=pltpu.CompilerParams(dimension_semantics=("parallel", "arbitrary")),
        name="conformer_conv",
    )(h, h, w_pad, vec(conv_b), vec(ln_g), vec(ln_b))


def _rel_bucket(dist):
    max_exact = NUM_BUCKETS // 2
    n = dist.astype(jnp.int32)
    nf = jnp.maximum(n, 1).astype(F32)
    large = max_exact + (jnp.log(nf / max_exact) / math.log(MAX_DISTANCE / max_exact)
                         * (NUM_BUCKETS - max_exact)).astype(jnp.int32)
    large = jnp.minimum(large, NUM_BUCKETS - 1)
    return jnp.where(n < max_exact, n, large)


def _bias_by_distance(rel_bias, n):
    return rel_bias[_rel_bucket(jnp.arange(n, dtype=jnp.int32))].astype(F32).T


def _toeplitz_kernel(win_ref, o_ref):
    n_win, span = win_ref.shape
    for n in range(n_win):
        y = jnp.broadcast_to(win_ref[n:n + 1, :], (LANES, span))
        y = pltpu.roll(y, 0, 1, stride=1, stride_axis=0)
        o_ref[n] = y[:, LANES:]


def _toeplitz_tiles(windows, per_step=8):
    n, span = windows.shape
    width = span - LANES
    return pl.pallas_call(
        _toeplitz_kernel,
        grid=(n // per_step,),
        in_specs=[pl.BlockSpec((per_step, span), lambda i: (i, 0))],
        out_specs=pl.BlockSpec((per_step, LANES, width), lambda i: (i, 0, 0)),
        out_shape=jax.ShapeDtypeStruct((n, LANES, width), F32),
        compiler_params=pltpu.CompilerParams(dimension_semantics=("parallel",)),
        name="bias_toeplitz",
    )(windows)


def _dsa_bias_tiles(rel_bias, seq):
    nblk = seq // Q_BLOCK
    heads = rel_bias.shape[1]
    f = _bias_by_distance(rel_bias, seq)
    u = jnp.concatenate([jnp.broadcast_to(f[:, :1], (heads, 2 * Q_BLOCK)), f], axis=1)
    blocks = u.reshape(heads, nblk + 2, Q_BLOCK)
    windows = jnp.concatenate([blocks[:, :-1], blocks[:, 1:]], axis=-1)
    windows = windows.transpose(1, 0, 2).reshape((nblk + 1) * heads, 2 * Q_BLOCK)
    return _toeplitz_tiles(windows).reshape(nblk + 1, heads, Q_BLOCK, Q_BLOCK)


def _dil_bias_tiles(rel_bias):
    max_dil = max(d for _, d in DIL_GROUPS)
    heads = rel_bias.shape[1]
    span = 2 * BAND_BLOCK
    f = _bias_by_distance(rel_bias, span * max_dil + 1)
    windows = []
    for _, dil in DIL_GROUPS:
        fd = f[:, 0:span * dil + 1:dil]
        tail = jnp.broadcast_to(fd[:, :1], (heads, BAND_BLOCK - 1))
        windows.append(jnp.concatenate([fd[:, ::-1], tail], axis=1))
    tiles = _toeplitz_tiles(jnp.concatenate(windows, axis=0))
    return tiles.reshape(len(DIL_GROUPS), heads, BAND_BLOCK, span)


_KC = 2 * Q_BLOCK
_VT_ROWS = DSA_HEAD_DIM + 16


def _sortable_key(score):
    score = jnp.where(score == 0.0, 0.0, score)
    bits = lax.bitcast_convert_type(score, jnp.int32)
    return jnp.where(bits < 0, bits ^ jnp.int32(0x7FFFFFFF), bits)


def _rows_reduce(x, op):
    rows = x.shape[0]
    return op(op(x.reshape(rows // 8, 8, LANES), axis=0), axis=0, keepdims=True)


def _keys_reduce(x, op):
    h, keys, _ = x.shape
    return op(op(x.reshape(h, keys // 8, 8, LANES), axis=1), axis=1, keepdims=True)


_QG = 2


def _dsa_t_kernel(q_ref, qi_ref, wq_ref, k_ref, vt_ref, ki_ref, bias_hbm, o_ref,
                  bias_ref, keys_ref, qm_ref, acc_ref, dots_a_ref, dots_b_ref, sem, *, topk):
    b = pl.program_id(0)
    i2 = pl.program_id(1)

    @pl.when((b == 0) & (i2 == 0))
    def _():
        cp = pltpu.make_async_copy(bias_hbm, bias_ref, sem)
        cp.start()
        cp.wait()

    nch = i2 + 1
    key_off = lax.broadcasted_iota(jnp.int32, (_KC, LANES), 0)
    q_lane = lax.broadcasted_iota(jnp.int32, (_KC, LANES), 1)
    lane = lax.broadcasted_iota(jnp.int32, (Q_BLOCK, LANES), 1)
    low_half = lane < DSA_HEAD_DIM
    n_heads = _QG * DSA_HEADS

    w_ts, qi_heads = [], []
    for g in range(_QG):
        rows = slice(g * Q_BLOCK, (g + 1) * Q_BLOCK)
        w_ts.append((wq_ref[0, rows, :].astype(F32) * (IDX_HEADS ** -0.5 * IDX_DIM ** -0.5)).T)
        qi = qi_ref[0, rows, :].astype(F32)
        for p in range(IDX_HEADS // 2):
            pair = qi[:, p * LANES:(p + 1) * LANES]
            qi_heads.append(jnp.where(low_half, pair, 0.0))
            qi_heads.append(jnp.where(low_half, pltpu.roll(pair, IDX_DIM, 1), 0.0))
        q_scaled = (q_ref[0, rows, :].astype(F32) * (DSA_HEAD_DIM ** -0.5)).astype(BF16)
        for p in range(DSA_HEADS // 2):
            qp = q_scaled[:, p * LANES:(p + 1) * LANES]
            qm_ref[g * DSA_HEADS + 2 * p] = jnp.where(low_half, qp, jnp.zeros_like(qp))
            qm_ref[g * DSA_HEADS + 2 * p + 1] = jnp.where(low_half, jnp.zeros_like(qp), qp)
    qi_stack = jnp.concatenate(qi_heads, axis=0).astype(BF16)

    def index_dots(c, dst_ref):
        c = jnp.minimum(c, nch - 1)
        k0 = pl.multiple_of(c * _KC, _KC)
        ki = ki_ref[0, pl.ds(k0, _KC), :]
        dst_ref[...] = _dot_nt(ki, qi_stack)

    def consume_dots(c, src_ref):
        k0 = c * _KC
        for g in range(_QG):
            score = jnp.zeros((_KC, LANES), F32)
            for h in range(IDX_HEADS):
                w_row = w_ts[g][IDX_DIM + h:IDX_DIM + h + 1, :]
                col = (g * IDX_HEADS + h) * Q_BLOCK
                score = score + w_row * jnp.maximum(src_ref[:, col:col + Q_BLOCK], 0.0)
            q_pos = (_QG * i2 + g) * Q_BLOCK + q_lane
            keys_ref[g, c] = jnp.where(k0 + key_off <= q_pos, _sortable_key(score), jnp.int32(INT_MIN))

    index_dots(0, dots_a_ref)

    def score_body(c2, carry):
        index_dots(2 * c2 + 1, dots_b_ref)
        consume_dots(2 * c2, dots_a_ref)
        index_dots(2 * c2 + 2, dots_a_ref)
        consume_dots(2 * c2 + 1, dots_b_ref)
        return carry

    lax.fori_loop(0, (nch + 1) // 2, score_body, 0)

    def count_ge(cand):
        cb = jnp.broadcast_to(cand, (_QG, 8, LANES))[:, None, None]
        chains = 4

        def body(c2, acc):
            for half in range(2):
                kk = keys_ref[:, 2 * c2 + half].reshape(_QG, _KC // (8 * chains), chains, 8, LANES)
                acc = acc + jnp.sum(jnp.where(kk >= cb, 1.0, 0.0), axis=1)
            return acc

        acc = lax.fori_loop(0, (nch + 1) // 2, body, jnp.zeros((_QG, chains, 8, LANES), F32))
        return jnp.sum(jnp.sum(acc, axis=1), axis=1, keepdims=True)

    kf = float(topk)
    zero = jnp.zeros((_QG, 1, LANES), jnp.int32)
    thr0 = jnp.where(count_ge(zero) >= kf, zero, jnp.full((_QG, 1, LANES), INT_MIN, jnp.int32))

    def bit_body(t, thr):
        cand = thr | jnp.left_shift(jnp.int32(1), 30 - t)
        return jnp.where(count_ge(cand) >= kf, cand, thr)

    thr = lax.fori_loop(0, 31, bit_body, thr0)
    n_gt = count_ge(jnp.where(thr == jnp.int32(2 ** 31 - 1), thr, thr + 1))
    need = kf - n_gt
    tri = (lax.broadcasted_iota(jnp.int32, (_KC, _KC), 1)
           <= lax.broadcasted_iota(jnp.int32, (_KC, _KC), 0)).astype(BF16)

    acc_ref[...] = jnp.zeros_like(acc_ref)

    def attn_body(c, carry):
        ms, tie_carry = carry
        k0 = pl.multiple_of(c * _KC, _KC)
        kk = keys_ref[:, c]
        causal = kk > jnp.int32(INT_MIN)
        eq = (kk == thr) & causal
        eq_f = jnp.where(eq, 1.0, 0.0)
        eq_b = eq_f.astype(BF16)
        rank = jnp.stack([_dot(tri, eq_b[g]) for g in range(_QG)]) + tie_carry
        sel = ((kk > thr) & causal) | (eq & (rank <= need))
        tie_carry = tie_carry + jnp.stack([_rows_reduce(eq_f[g], jnp.sum) for g in range(_QG)])
        kc = k_ref[0, pl.ds(k0, _KC), :]
        k_heads = jnp.stack([kc[:, (h // 2) * LANES:(h // 2 + 1) * LANES] for h in range(DSA_HEADS)] * _QG)
        s = jnp.einsum("hkd,hqd->hkq", k_heads, qm_ref[...], preferred_element_type=F32)
        biases = []
        for g in range(_QG):
            d = _QG * i2 + g - 2 * c
            biases.append(jnp.concatenate([bias_ref[d + 1], bias_ref[d]], axis=1))
        bias = jnp.concatenate(biases, axis=0)
        sel_h = jnp.broadcast_to(sel[:, None], (_QG, DSA_HEADS, _KC, LANES)).reshape(n_heads, _KC, LANES)
        s = jnp.where(sel_h, s + bias, NEG)
        m_new = jnp.maximum(ms, _keys_reduce(s, jnp.max))
        alpha = jnp.exp(ms - m_new)
        pr = jnp.exp((s - m_new).astype(BF16))
        vt = jnp.concatenate([vt_ref[0, c]] * _QG, axis=0)
        pv = jnp.einsum("hdk,hkq->hdq", vt, pr, preferred_element_type=F32)
        acc_ref[...] = alpha * acc_ref[...] + pv
        return m_new, tie_carry

    init = (jnp.full((n_heads, 1, LANES), NEG, F32), jnp.zeros((_QG, 1, LANES), F32))
    lax.fori_loop(0, nch, attn_body, init)
    acc = acc_ref[...]
    out_t = acc[:, :DSA_HEAD_DIM] / acc[:, DSA_HEAD_DIM:DSA_HEAD_DIM + 1]
    for g in range(_QG):
        block = out_t[g * DSA_HEADS:(g + 1) * DSA_HEADS].reshape(DSA_HEADS * DSA_HEAD_DIM, Q_BLOCK)
        o_ref[0, g * Q_BLOCK:(g + 1) * Q_BLOCK, :] = block.T.astype(o_ref.dtype)


_VT_CHUNKS = 4


def _vt_kernel(v_ref, o_ref):
    for c in range(_VT_CHUNKS):
        vt = v_ref[0, c * _KC:(c + 1) * _KC, :].astype(F32).T
        o_ref[0, c, :, :DSA_HEAD_DIM, :] = vt.reshape(DSA_HEADS, DSA_HEAD_DIM, _KC).astype(o_ref.dtype)
        o_ref[0, c, :, DSA_HEAD_DIM:, :] = jnp.ones((DSA_HEADS, _VT_ROWS - DSA_HEAD_DIM, _KC), o_ref.dtype)


def _values_key_minor(h):
    b, s, _ = h.shape
    width = DSA_HEADS * DSA_HEAD_DIM
    v_block = (2 * CONV_CH + 2 * width) // width
    return pl.pallas_call(
        _vt_kernel,
        grid=(b, s // (_VT_CHUNKS * _KC)),
        in_specs=[pl.BlockSpec((1, _VT_CHUNKS * _KC, width), lambda bi, ci: (bi, ci, v_block))],
        out_specs=pl.BlockSpec((1, _VT_CHUNKS, DSA_HEADS, _VT_ROWS, _KC), lambda bi, ci: (bi, ci, 0, 0, 0)),
        out_shape=jax.ShapeDtypeStruct((b, s // _KC, DSA_HEADS, _VT_ROWS, _KC), BF16),
        compiler_params=_PARALLEL2,
        name="dsa_values_key_minor",
    )(h)


def _dsa_attention(h, bias_tiles):
    b, s, _ = h.shape
    width = DSA_HEADS * DSA_HEAD_DIM
    nblk = s // Q_BLOCK
    topk = min(DSA_TOPK_MAX, s // 4)
    kw_block = (CONV_CH * 2 + 4 * width) // LANES
    vt = _values_key_minor(h)
    return pl.pallas_call(
        functools.partial(_dsa_t_kernel, topk=topk),
        grid=(b, nblk // _QG),
        in_specs=[pl.BlockSpec((1, _QG * Q_BLOCK, width), lambda bi, qi: (bi, qi, 2)),
                  pl.BlockSpec((1, _QG * Q_BLOCK, width), lambda bi, qi: (bi, qi, 5)),
                  pl.BlockSpec((1, _QG * Q_BLOCK, LANES), lambda bi, qi: (bi, qi, kw_block)),
                  pl.BlockSpec((1, s, width), lambda bi, qi: (bi, 0, 3)),
                  pl.BlockSpec((1, s // _KC, DSA_HEADS, _VT_ROWS, _KC), lambda bi, qi: (bi, 0, 0, 0, 0)),
                  pl.BlockSpec((1, s, LANES), lambda bi, qi: (bi, 0, kw_block)),
                  pl.BlockSpec(memory_space=pl.ANY)],
        out_specs=pl.BlockSpec((1, _QG * Q_BLOCK, width), lambda bi, qi: (bi, qi, 0)),
        out_shape=jax.ShapeDtypeStruct((b, s, width), BF16),
        scratch_shapes=[pltpu.VMEM((nblk + 1, DSA_HEADS, Q_BLOCK, Q_BLOCK), F32),
                        pltpu.VMEM((_QG, s // _KC, _KC, Q_BLOCK), jnp.int32),
                        pltpu.VMEM((_QG * DSA_HEADS, Q_BLOCK, LANES), BF16),
                        pltpu.VMEM((_QG * DSA_HEADS, _VT_ROWS, Q_BLOCK), F32),
                        pltpu.VMEM((_KC, _QG * IDX_HEADS * Q_BLOCK), F32),
                        pltpu.VMEM((_KC, _QG * IDX_HEADS * Q_BLOCK), F32),
                        pltpu.SemaphoreType.DMA(())],
        compiler_params=pltpu.CompilerParams(dimension_semantics=("arbitrary", "arbitrary")),
        name="dsa_attention",
    )(h, h, h, h, vt, h, bias_tiles)


_DIL_WAVE = 16


def _dil_kernel(*refs, seq):
    n_g = len(DIL_GROUPS)
    qkv = refs[:3 * n_g]
    bias_ref, o_ref, og_ref, lg_ref = refs[3 * n_g:]
    c = pl.program_id(2)
    ii = lax.broadcasted_iota(jnp.int32, (BAND_BLOCK, BAND_BLOCK), 0)
    jj = lax.broadcasted_iota(jnp.int32, (BAND_BLOCK, BAND_BLOCK), 1)
    valid_prev = jj >= ii
    valid_cur = ii >= jj
    scale = DIL_HEAD_DIM ** -0.5
    for g, (window, dil) in enumerate(DIL_GROUPS):
        assert window // dil == BAND_BLOCK
        q_ref, k_ref, v_ref = qkv[3 * g:3 * g + 3]
        sub_len = seq // dil
        blocks_per_chunk = DIL_CHUNK // (BAND_BLOCK * dil)
        bias_prev = bias_ref[g, 0, :, :BAND_BLOCK]
        bias_cur = bias_ref[g, 0, :, BAND_BLOCK:]
        tiles = [(r, nl) for r in range(dil) for nl in range(blocks_per_chunk)]
        for w0 in range(0, len(tiles), _DIL_WAVE):
            wave = tiles[w0:w0 + _DIL_WAVE]
            cur, prev, mask_prev = [], [], []
            for r, nl in wave:
                row0 = pl.multiple_of(r * sub_len + c * (DIL_CHUNK // dil) + nl * BAND_BLOCK, BAND_BLOCK)
                cur.append(pl.ds(row0, BAND_BLOCK))
                prev.append(pl.ds(pl.multiple_of(jnp.maximum(row0 - BAND_BLOCK, 0), BAND_BLOCK), BAND_BLOCK))
                mask_prev.append(valid_prev if nl > 0 else (valid_prev & (c > 0)))
            gather = lambda ref, slices: jnp.stack([ref[0, sl, :] for sl in slices])
            qb = gather(q_ref, cur)
            s_cur = jnp.einsum("wqd,wkd->wqk", qb, gather(k_ref, cur), preferred_element_type=F32)
            s_prev = jnp.einsum("wqd,wkd->wqk", qb, gather(k_ref, prev), preferred_element_type=F32)
            s_cur = jnp.where(valid_cur, s_cur * scale + bias_cur, NEG)
            s_prev = jnp.where(jnp.stack(mask_prev), s_prev * scale + bias_prev, NEG)
            m = jnp.max(jnp.maximum(s_cur, s_prev), axis=2, keepdims=True)
            p_cur = jnp.exp(s_cur - m)
            p_prev = jnp.exp(s_prev - m)
            ssum = jnp.sum(p_cur + p_prev, axis=2, keepdims=True)
            o = (jnp.einsum("wqk,wkd->wqd", p_cur.astype(BF16), gather(v_ref, cur), preferred_element_type=F32)
                 + jnp.einsum("wqk,wkd->wqd", p_prev.astype(BF16), gather(v_ref, prev),
                              preferred_element_type=F32)) / ssum
            lse = m + jnp.log(ssum)
            for w, (r, nl) in enumerate(wave):
                dst = pl.ds(r + nl * BAND_BLOCK * dil, BAND_BLOCK, stride=dil) if dil > 1 else pl.ds(
                    nl * BAND_BLOCK, BAND_BLOCK)
                og_ref[g, dst, :] = o[w]
                lg_ref[g, dst, :] = jnp.broadcast_to(lse[w], (BAND_BLOCK, LANES))
    rows = 256
    for rc in range(DIL_CHUNK // rows):
        sl = slice(rc * rows, (rc + 1) * rows)
        lses = [lg_ref[g, sl, :] for g in range(n_g)]
        mx = functools.reduce(jnp.maximum, lses)
        ws = [jnp.exp(l - mx) for l in lses]
        den = functools.reduce(lambda a, b_: a + b_, ws)
        num = functools.reduce(lambda a, b_: a + b_, [ws[g] * og_ref[g, sl, :] for g in range(n_g)])
        o_ref[0, sl, :] = (num / den).astype(o_ref.dtype)


def _dilated_attention(hs, bias_tiles):
    b, s, _ = hs[0].shape
    in_specs = []
    args = []
    for hg in hs:
        for part in range(3):
            in_specs.append(pl.BlockSpec((1, s, DIL_HEAD_DIM),
                                         lambda bi, hi, ci, part=part: (bi, 0, part * DIL_HEADS + hi)))
            args.append(hg)
    in_specs.append(pl.BlockSpec((len(DIL_GROUPS), 1, BAND_BLOCK, 2 * BAND_BLOCK),
                                 lambda bi, hi, ci: (0, hi, 0, 0)))
    return pl.pallas_call(
        functools.partial(_dil_kernel, seq=s),
        grid=(b, DIL_HEADS, s // DIL_CHUNK),
        in_specs=in_specs,
        out_specs=pl.BlockSpec((1, DIL_CHUNK, DIL_HEAD_DIM), lambda bi, hi, ci: (bi, ci, hi)),
        out_shape=jax.ShapeDtypeStruct((b, s, DIL_HEADS * DIL_HEAD_DIM), BF16),
        scratch_shapes=[pltpu.VMEM((len(DIL_GROUPS), DIL_CHUNK, LANES), F32),
                        pltpu.VMEM((len(DIL_GROUPS), DIL_CHUNK, LANES), F32)],
        compiler_params=pltpu.CompilerParams(dimension_semantics=("parallel", "parallel", "arbitrary")),
        name="dilated_attention",
    )(*args, bias_tiles)


def _outproj_kernel(*refs, n_in, with_router):
    ins = refs[:n_in]
    ws = refs[n_in:2 * n_in]
    x_ref, g_ref, b_ref = refs[2 * n_in:2 * n_in + 3]
    rest = refs[2 * n_in + 3:]
    mix = _dot(ins[0][...], ws[0][...])
    for a_ref, w_ref in zip(ins[1:], ws[1:]):
        mix = mix + _dot(a_ref[...], w_ref[...])
    xn = _layer_norm(DEEPNORM_ALPHA * x_ref[...] + mix, g_ref[...], b_ref[...])
    if not with_router:
        o_ref, ob_ref = rest
    else:
        router_ref, o_ref, gates_ref, counts_ref, run_ref, tri_ref = rest
        ob_ref = None
        tm = xn.shape[0]

        @pl.when(pl.program_id(0) == 0)
        def _():
            run_ref[...] = jnp.zeros_like(run_ref)
            tri_ref[...] = (lax.broadcasted_iota(jnp.int32, (tm, tm), 0)
                            > lax.broadcasted_iota(jnp.int32, (tm, tm), 1)).astype(BF16)

        logits = _dot(xn.astype(BF16), router_ref[...])
        lane = lax.broadcasted_iota(jnp.int32, logits.shape, 1).astype(F32)
        lg = jnp.where(lane < N_EXPERTS, logits, -jnp.inf)
        m1 = jnp.max(lg, axis=1, keepdims=True)
        i1 = jnp.min(jnp.where(lg == m1, lane, float(LANES)), axis=1, keepdims=True)
        lg2 = jnp.where(lane == i1, -jnp.inf, lg)
        m2 = jnp.max(lg2, axis=1, keepdims=True)
        i2 = jnp.min(jnp.where(lg2 == m2, lane, float(LANES)), axis=1, keepdims=True)
        e2 = jnp.exp(m2 - m1)
        den = 1.0 + e2
        oh1 = jnp.where(lane == i1, 1.0, 0.0)
        oh2 = jnp.where(lane == i2, 1.0, 0.0)
        both = oh1 + oh2
        before = _dot(tri_ref[...], both.astype(BF16)) + run_ref[...]
        r1 = jnp.sum(before * oh1, axis=1, keepdims=True)
        r2 = jnp.sum(before * oh2, axis=1, keepdims=True)
        run_ref[...] += jnp.sum(both, axis=0, keepdims=True)
        counts_ref[...] = run_ref[...]
        gates_ref[...] = (jnp.where(lane == 0.0, i1, 0.0) + jnp.where(lane == 1.0, i2, 0.0)
                          + jnp.where(lane == 2.0, 1.0 / den, 0.0) + jnp.where(lane == 3.0, e2 / den, 0.0)
                          + jnp.where(lane == 4.0, r1, 0.0) + jnp.where(lane == 5.0, r2, 0.0))
    o_ref[...] = xn
    if ob_ref is not None:
        ob_ref[...] = xn.astype(BF16)


def _outproj_ln(ins, ws, x, ln_g, ln_b, router=None, tm=1024):
    t, d = x.shape
    n_in = len(ins)
    row = lambda width: pl.BlockSpec((tm, width), lambda i: (i, 0))
    full = lambda a: pl.BlockSpec(a.shape, lambda i: (0, 0))
    vec = lambda v: v.reshape(1, d).astype(F32)
    args = list(ins) + list(ws) + [x, vec(ln_g), vec(ln_b)]
    in_specs = [row(a.shape[1]) for a in ins] + [full(w) for w in ws] + [row(d), full(vec(ln_g)), full(vec(ln_b))]
    out_shape = [jax.ShapeDtypeStruct((t, d), F32), jax.ShapeDtypeStruct((t, d), BF16)]
    out_specs = [row(d), row(d)]
    if router is not None:
        router_pad = jnp.zeros((d, LANES), BF16).at[:, :N_EXPERTS].set(router.astype(BF16))
        args.append(router_pad)
        in_specs.append(full(router_pad))
        out_shape = [out_shape[0], jax.ShapeDtypeStruct((t, LANES), F32), jax.ShapeDtypeStruct((1, LANES), F32)]
        out_specs = [row(d), row(LANES), pl.BlockSpec((1, LANES), lambda i: (0, 0))]
        scratch = [pltpu.VMEM((1, LANES), F32), pltpu.VMEM((tm, tm), BF16)]
    else:
        scratch = []
    semantics = "arbitrary" if router is not None else "parallel"
    return pl.pallas_call(
        functools.partial(_outproj_kernel, n_in=n_in, with_router=router is not None),
        grid=(t // tm,),
        in_specs=in_specs,
        out_specs=out_specs,
        out_shape=out_shape,
        scratch_shapes=scratch,
        compiler_params=pltpu.CompilerParams(dimension_semantics=(semantics,)),
        name="outproj_ln",
    )(*args)


def _ffn_kernel(xb_ref, x_ref, wg_ref, wu_ref, wd_ref, g_ref, b_ref, o_ref, ob_ref, acc_ref):
    f = pl.program_id(1)

    @pl.when(f == 0)
    def _():
        acc_ref[...] = jnp.zeros_like(acc_ref)

    xb = xb_ref[...]
    hg = _dot(xb, wg_ref[...])
    hu = _dot(xb, wu_ref[...])
    act = (hg * _sigmoid(hg) * hu).astype(BF16)
    acc_ref[...] += _dot(act, wd_ref[...])

    @pl.when(f == pl.num_programs(1) - 1)
    def _():
        xn = _layer_norm(DEEPNORM_ALPHA * x_ref[...] + acc_ref[...], g_ref[...], b_ref[...])
        o_ref[...] = xn
        ob_ref[...] = xn.astype(BF16)


def _ffn_ln(xb, x, wg, wu, wd, ln_g, ln_b, tm=512, tf=1408):
    t, d = x.shape
    ff = wg.shape[1]
    vec = lambda v: v.reshape(1, d).astype(F32)
    return pl.pallas_call(
        _ffn_kernel,
        grid=(t // tm, ff // tf),
        in_specs=[pl.BlockSpec((tm, d), lambda i, f: (i, 0)),
                  pl.BlockSpec((tm, d), lambda i, f: (i, 0)),
                  pl.BlockSpec((d, tf), lambda i, f: (0, f)),
                  pl.BlockSpec((d, tf), lambda i, f: (0, f)),
                  pl.BlockSpec((tf, d), lambda i, f: (f, 0)),
                  pl.BlockSpec((1, d), lambda i, f: (0, 0)),
                  pl.BlockSpec((1, d), lambda i, f: (0, 0))],
        out_specs=[pl.BlockSpec((tm, d), lambda i, f: (i, 0)),
                   pl.BlockSpec((tm, d), lambda i, f: (i, 0))],
        out_shape=[jax.ShapeDtypeStruct((t, d), F32), jax.ShapeDtypeStruct((t, d), BF16)],
        scratch_shapes=[pltpu.VMEM((tm, d), F32)],
        compiler_params=pltpu.CompilerParams(dimension_semantics=("parallel", "arbitrary")),
        name="ffn_ln",
    )(xb, x, wg, wu, wd, vec(ln_g), vec(ln_b))


_MOE_TM = 1024


def _route(route, counts, n_tiles):
    expert = route[:, 0:2].astype(jnp.int32)
    rank = route[:, 4:6].astype(jnp.int32)
    counts = counts[0, :N_EXPERTS].astype(jnp.int32)
    padded = (counts + _MOE_TM - 1) // _MOE_TM * _MOE_TM
    ends = jnp.cumsum(padded)
    starts = ends - padded
    onehot = expert[:, :, None] == jnp.arange(N_EXPERTS, dtype=jnp.int32)
    pos = (rank + jnp.sum(jnp.where(onehot, starts, 0), axis=-1)).reshape(-1)
    tile_start = jnp.arange(n_tiles, dtype=jnp.int32) * _MOE_TM
    tile_expert = jnp.sum((tile_start[:, None] >= ends[None, :]).astype(jnp.int32), axis=1)
    valid_end = jnp.concatenate([starts + counts, jnp.zeros((1,), jnp.int32)])[tile_expert]
    tile_rows = jnp.clip(valid_end - tile_start, 0, _MOE_TM)
    return pos.astype(jnp.int32), tile_expert.astype(jnp.int32), tile_rows.astype(jnp.int32)


def _row_scatter_kernel(pos_ref, x_ref, init_ref, o_ref, sem, *, tm):
    del init_ref

    def body(j, carry):
        for choice in range(2):
            p = pos_ref[0, 0, 2 * j + choice]
            pltpu.make_async_copy(x_ref.at[pl.ds(j, 1)], o_ref.at[pl.ds(p, 1)], sem).start()
        return carry

    lax.fori_loop(0, tm, body, 0, unroll=8)
    for _ in range(2):
        pltpu.make_async_copy(x_ref, o_ref.at[pl.ds(0, tm)], sem).wait()


def _row_scatter(x, pos, init, tm=512):
    t, d = x.shape
    n_rows = init.shape[0]
    return pl.pallas_call(
        functools.partial(_row_scatter_kernel, tm=tm),
        grid=(t // tm,),
        in_specs=[pl.BlockSpec((1, 1, 2 * tm), lambda i: (i, 0, 0), memory_space=pltpu.SMEM),
                  pl.BlockSpec((tm, d), lambda i: (i, 0)),
                  pl.BlockSpec(memory_space=pl.ANY)],
        out_specs=pl.BlockSpec(memory_space=pl.ANY),
        out_shape=jax.ShapeDtypeStruct((n_rows, d), x.dtype),
        scratch_shapes=[pltpu.SemaphoreType.DMA(())],
        input_output_aliases={2: 0},
        compiler_params=pltpu.CompilerParams(dimension_semantics=("arbitrary",)),
        name="moe_row_scatter",
    )(pos.reshape(t // tm, 1, 2 * tm), x, init)


def _expert_ffn_kernel(te_ref, rows_ref, x_ref, wg_ref, wu_ref, wd_ref, o_ref, acc_ref):
    del te_ref
    f = pl.program_id(1)
    last = pl.num_programs(1) - 1
    rows = rows_ref[pl.program_id(0)]
    used = rows > 0
    half = _MOE_TM // 2

    @pl.when(used & (f == 0))
    def _():
        acc_ref[...] = jnp.zeros_like(acc_ref)

    def swiglu(m):
        xb = x_ref[0:m, :].astype(BF16)
        hg = _dot(xb, wg_ref[0].astype(BF16))
        hu = _dot(xb, wu_ref[0].astype(BF16))
        act = (hg * _sigmoid(hg) * hu).astype(BF16)
        acc_ref[0:m, :] += _dot(act, wd_ref[0].astype(BF16))

    pl.when(rows > half)(lambda: swiglu(_MOE_TM))
    pl.when(used & (rows <= half))(lambda: swiglu(half))

    @pl.when(used & (f == last))
    def _():
        o_ref[...] = acc_ref[...]

    @pl.when(jnp.logical_not(used) & (f == last))
    def _():
        o_ref[...] = jnp.zeros_like(o_ref)


def _expert_ffn(xs, tile_expert, tile_rows, wg, wu, wd, tf=512):
    n_rows, d = xs.shape
    ff = wg.shape[2]
    nf = ff // tf
    expert = lambda i, te: jnp.minimum(te[i], N_EXPERTS - 1)
    chunk = lambda i, f, te: jnp.where(te[i] < N_EXPERTS, f, nf - 1)
    grid_spec = pltpu.PrefetchScalarGridSpec(
        num_scalar_prefetch=2,
        grid=(n_rows // _MOE_TM, nf),
        in_specs=[pl.BlockSpec((_MOE_TM, d), lambda i, f, te, tr: (i, 0)),
                  pl.BlockSpec((1, d, tf), lambda i, f, te, tr: (expert(i, te), 0, chunk(i, f, te))),
                  pl.BlockSpec((1, d, tf), lambda i, f, te, tr: (expert(i, te), 0, chunk(i, f, te))),
                  pl.BlockSpec((1, tf, d), lambda i, f, te, tr: (expert(i, te), chunk(i, f, te), 0))],
        out_specs=pl.BlockSpec((_MOE_TM, d), lambda i, f, te, tr: (i, 0)),
        scratch_shapes=[pltpu.VMEM((_MOE_TM, d), F32)])
    return pl.pallas_call(
        _expert_ffn_kernel,
        grid_spec=grid_spec,
        out_shape=jax.ShapeDtypeStruct((n_rows, d), F32),
        compiler_params=pltpu.CompilerParams(dimension_semantics=("parallel", "arbitrary")),
        name="moe_expert_ffn",
    )(tile_expert, tile_rows, xs, wg, wu, wd)


def _combine_kernel(pos_ref, pos_next_ref, route_ref, x_ref, g_ref, b_ref, y_ref, o_ref, ybuf_ref, sems, *, tm):
    i = pl.program_id(0)
    n = pl.num_programs(0)

    def gather(src_pos_ref, slot):
        def body(j, carry):
            for choice in range(2):
                p = src_pos_ref[0, 0, 2 * j + choice]
                pltpu.make_async_copy(y_ref.at[pl.ds(p, 1)], ybuf_ref.at[slot, choice, pl.ds(j, 1)],
                                      sems.at[slot]).start()
            return carry

        lax.fori_loop(0, tm, body, 0, unroll=8)

    @pl.when(i == 0)
    def _():
        gather(pos_ref, 0)

    slot = i % 2

    @pl.when(i + 1 < n)
    def _():
        gather(pos_next_ref, 1 - slot)

    for choice in range(2):
        pltpu.make_async_copy(y_ref.at[pl.ds(0, tm)], ybuf_ref.at[slot, choice], sems.at[slot]).wait()
    route = route_ref[...]
    y = route[:, 2:3] * ybuf_ref[slot, 0] + route[:, 3:4] * ybuf_ref[slot, 1]
    o_ref[...] = _layer_norm(DEEPNORM_ALPHA * x_ref[...] + y, g_ref[...], b_ref[...])


def _combine_ln(ys, pos, route, x, ln_g, ln_b, tm=512):
    t, d = x.shape
    vec = lambda v: v.reshape(1, d).astype(F32)
    last = t // tm - 1
    pos3 = pos.reshape(t // tm, 1, 2 * tm)
    return pl.pallas_call(
        functools.partial(_combine_kernel, tm=tm),
        grid=(t // tm,),
        in_specs=[pl.BlockSpec((1, 1, 2 * tm), lambda i: (i, 0, 0), memory_space=pltpu.SMEM),
                  pl.BlockSpec((1, 1, 2 * tm), lambda i: (jnp.minimum(i + 1, last), 0, 0), memory_space=pltpu.SMEM),
                  pl.BlockSpec((tm, LANES), lambda i: (i, 0)),
                  pl.BlockSpec((tm, d), lambda i: (i, 0)),
                  pl.BlockSpec((1, d), lambda i: (0, 0)),
                  pl.BlockSpec((1, d), lambda i: (0, 0)),
                  pl.BlockSpec(memory_space=pl.ANY)],
        out_specs=pl.BlockSpec((tm, d), lambda i: (i, 0)),
        out_shape=jax.ShapeDtypeStruct((t, d), F32),
        scratch_shapes=[pltpu.VMEM((2, 2, tm, d), F32), pltpu.SemaphoreType.DMA((2,))],
        compiler_params=pltpu.CompilerParams(dimension_semantics=("arbitrary",)),
        name="moe_combine_ln",
    )(pos3, pos3, route, x, vec(ln_g), vec(ln_b), ys)


def _moe_ln(x, route, counts, wg, wu, wd, ln_g, ln_b):
    t, _ = x.shape
    n_tiles = 2 * t // _MOE_TM + N_EXPERTS
    pos, tile_expert, tile_rows = _route(route, counts, n_tiles)
    xs = _row_scatter(x, pos, jnp.zeros((n_tiles * _MOE_TM, x.shape[1]), x.dtype))
    ys = _expert_ffn(xs, tile_expert, tile_rows, wg, wu, wd)
    return _combine_ln(ys, pos, route, x, ln_g, ln_b)


def _even_layer(x, xb, batch, seq, dsa_bias, w_in, conv_w, conv_b, conv_ln_g, conv_ln_b, w_out,
                ln1_g, ln1_b, ffn_wg, ffn_wu, ffn_wd, ln2_g, ln2_b):
    w_in_p = jnp.zeros((D_MODEL, EVEN_IN_PAD), BF16).at[:, :w_in.shape[1]].set(w_in.astype(BF16))
    h = _matmul(x if xb is None else xb, w_in_p, 1024, EVEN_IN_PAD, BF16).reshape(batch, seq, EVEN_IN_PAD)
    a_out = _conformer_conv(h, conv_w, conv_b, conv_ln_g, conv_ln_b)
    att = _dsa_attention(h, dsa_bias)
    w_out_b = w_out.astype(BF16)
    x1, x1b = _outproj_ln([a_out.reshape(-1, CONV_CH), att.reshape(-1, DSA_HEADS * DSA_HEAD_DIM)],
                          [w_out_b[:CONV_CH], w_out_b[CONV_CH:]], x, ln1_g, ln1_b)
    return _ffn_ln(x1b, x1, ffn_wg.astype(BF16), ffn_wu.astype(BF16), ffn_wd.astype(BF16), ln2_g, ln2_b)


def _odd_layer(x, xb, batch, seq, dil_bias, w_in, w_out, ln1_g, ln1_b, router, moe_wg, moe_wu, moe_wd,
               ln2_g, ln2_b):
    gw = 3 * DIL_HEADS * DIL_HEAD_DIM
    w_in_b = w_in.astype(BF16)
    hs = [_matmul_residue_major(x if xb is None else xb, w_in_b[:, g * gw:(g + 1) * gw], batch, seq, dil, 2048, 768)
          for g, (_, dil) in enumerate(DIL_GROUPS)]
    o = _dilated_attention(hs, dil_bias).reshape(-1, DIL_HEADS * DIL_HEAD_DIM)
    x1, route, counts = _outproj_ln([o], [w_out.astype(BF16)], x, ln1_g, ln1_b, router=router)
    out = _moe_ln(x1, route, counts, moe_wg, moe_wu, moe_wd, ln2_g, ln2_b)
    return out, out.astype(BF16)


def kernel(x, rel_bias, even_w_in, even_conv_w, even_conv_b, even_conv_ln_g, even_conv_ln_b, even_w_out, even_ln1_g, even_ln1_b, even_ffn_wg, even_ffn_wu, even_ffn_wd, even_ln2_g, even_ln2_b, odd_w_in, odd_w_out, odd_ln1_g, odd_ln1_b, odd_router, odd_moe_wg, odd_moe_wu, odd_moe_wd, odd_ln2_g, odd_ln2_b):
    batch, seq, d = x.shape
    assert d == D_MODEL and seq % DIL_CHUNK == 0
    depth = even_w_in.shape[0] + odd_w_in.shape[0]
    assert depth == DEPTH
    dsa_bias = _dsa_bias_tiles(rel_bias, seq)
    dil_bias = _dil_bias_tiles(rel_bias)
    xf = x.reshape(batch * seq, d)
    xb = None
    for layer in range(depth):
        i = layer // 2
        if layer % 2 == 0:
            xf, xb = _even_layer(xf, xb, batch, seq, dsa_bias, even_w_in[i], even_conv_w[i], even_conv_b[i],
                                 even_conv_ln_g[i], even_conv_ln_b[i], even_w_out[i], even_ln1_g[i],
                                 even_ln1_b[i], even_ffn_wg[i], even_ffn_wu[i], even_ffn_wd[i],
                                 even_ln2_g[i], even_ln2_b[i])
        else:
            xf, xb = _odd_layer(xf, xb, batch, seq, dil_bias, odd_w_in[i], odd_w_out[i], odd_ln1_g[i],
                                odd_ln1_b[i], odd_router[i], odd_moe_wg[i], odd_moe_wu[i], odd_moe_wd[i],
                                odd_ln2_g[i], odd_ln2_b[i])
    return xf.reshape(batch, seq, d)
```
